```python
import jax, jax.numpy as jnp
from jax import lax
import numpy as np

D_MODEL = 1024
BATCH = 8
SEQ = 4096
DEPTH = 4

CHUNK = 64
Q_BLOCK = 128

GDN_HEADS = 4
GDN_HEAD_DIM = 128
GDN_WIDTH = GDN_HEADS * GDN_HEAD_DIM
GDN_CONV = 4
SB_HEADS = 8
SB_HEAD_DIM = 64
SB_WIDTH = SB_HEADS * SB_HEAD_DIM
SC_GROUPS = 8
SC_WIDTH = 512
SC_CONV = 3
N_BRANCH = 3
BRANCH_WIDTH = 512
D_FF = 4 * D_MODEL
EPS = 1e-6

_SIZES = (3 * GDN_WIDTH,
          GDN_WIDTH,
          GDN_HEADS,
          GDN_HEADS,
          3 * SB_WIDTH,
          SC_WIDTH,
          SC_WIDTH,
          SC_WIDTH,
          N_BRANCH * D_MODEL)
IN_PROJ_WIDTH = sum(_SIZES)
SPLIT_POINTS = tuple(int(s) for s in np.cumsum(_SIZES)[:-1])

kernel_name = "hybrid_gdn_stickbreak_shortconv_block"


def rms_norm(x, w):
    x32 = x.astype(jnp.float32)
    y = x32 * lax.rsqrt(jnp.mean(x32 * x32, axis=-1, keepdims=True) + EPS)
    return (y * w.astype(jnp.float32)).astype(x.dtype)


def l2_normalize(x):
    return x * lax.rsqrt(jnp.sum(x * x, axis=-1, keepdims=True) + EPS)


def causal_depthwise_conv(x, w):
    k_len, ch = w.shape
    return lax.conv_general_dilated(
        x, w.astype(x.dtype)[:, None, :], window_strides=(1,), padding=[(k_len - 1, 0)],
        dimension_numbers=("NWC", "WIO", "NWC"), feature_group_count=ch)


def gated_delta_rule(q, k, v, g, beta):
    bsz, seq, heads, dk = q.shape
    dv = v.shape[-1]
    n_chunks = seq // CHUNK

    def chunkify(t):
        return jnp.moveaxis(t.reshape((bsz, n_chunks, CHUNK) + t.shape[2:]), 3, 2)

    q, k, v, g, beta = (chunkify(t) for t in (q, k, v, g, beta))
    g = jnp.cumsum(g, axis=-1)
    tri = jnp.tril(jnp.ones((CHUNK, CHUNK), dtype=bool))
    strict = jnp.tril(jnp.ones((CHUNK, CHUNK), dtype=bool), k=-1)
    decay = jnp.exp(jnp.where(tri, g[..., :, None] - g[..., None, :], -jnp.inf))

    k_beta = k * beta[..., None]
    v_beta = v * beta[..., None]
    a_kk = jnp.where(strict, jnp.einsum("bnhcd,bnhed->bnhce", k_beta, k) * decay, 0.0)
    lhs = a_kk + jnp.eye(CHUNK, dtype=a_kk.dtype)
    rhs = jnp.concatenate([v_beta, k_beta * jnp.exp(g)[..., None]], axis=-1)
    sol = lax.linalg.triangular_solve(lhs, rhs, left_side=True, lower=True, unit_diagonal=True)
    u, w = sol[..., :dv], sol[..., dv:]

    a_qk = jnp.where(tri, jnp.einsum("bnhcd,bnhed->bnhce", q, k) * decay, 0.0)
    q_dec = q * jnp.exp(g)[..., None]
    k_dec = k * jnp.exp(g[..., -1:] - g)[..., None]
    g_last = jnp.exp(g[..., -1])

    def step(state, inp):
        q_c, k_c, u_c, w_c, a_c, gl_c = inp
        v_new = u_c - jnp.einsum("bhcd,bhde->bhce", w_c, state)
        o_c = (jnp.einsum("bhcd,bhde->bhce", q_c, state)
               + jnp.einsum("bhcs,bhse->bhce", a_c, v_new))
        state = state * gl_c[..., None, None] + jnp.einsum("bhcd,bhce->bhde", k_c, v_new)
        return state, o_c

    xs = tuple(jnp.moveaxis(t, 1, 0) for t in (q_dec, k_dec, u, w, a_qk, g_last))
    state0 = jnp.zeros((bsz, heads, dk, dv), jnp.float32)
    _, o = lax.scan(step, state0, xs)
    return o.transpose(1, 0, 3, 2, 4).reshape(bsz, seq, heads, dv)


def gdn_branch(qkv, gate, a, b, conv_w, a_log, dt_bias, norm_w):
    bsz, seq, _ = qkv.shape
    out_dtype = qkv.dtype
    qkv = jax.nn.silu(causal_depthwise_conv(qkv, conv_w)).astype(jnp.float32)
    q, k, v = jnp.split(qkv, 3, axis=-1)
    shp = (bsz, seq, GDN_HEADS, GDN_HEAD_DIM)
    q = l2_normalize(q.reshape(shp)) * (GDN_HEAD_DIM ** -0.5)
    k = l2_normalize(k.reshape(shp))
    v = v.reshape(shp)
    beta = jax.nn.sigmoid(b.astype(jnp.float32))
    g = -jnp.exp(a_log.astype(jnp.float32)) * jax.nn.softplus(a.astype(jnp.float32) + dt_bias.astype(jnp.float32))
    o = gated_delta_rule(q, k, v, g, beta)
    o = o * lax.rsqrt(jnp.mean(o * o, axis=-1, keepdims=True) + EPS) * norm_w.astype(jnp.float32)
    o = o * jax.nn.silu(gate.astype(jnp.float32).reshape(shp))
    return o.reshape(bsz, seq, GDN_WIDTH).astype(out_dtype)


def stick_breaking_branch(qkv):
    bsz, seq, _ = qkv.shape
    q, k, v = jnp.split(qkv, 3, axis=-1)
    shp = (bsz, seq, SB_HEADS, SB_HEAD_DIM)
    q, k, v = q.reshape(shp), k.reshape(shp), v.reshape(shp)
    scale = SB_HEAD_DIM ** -0.5
    outs = []
    for blk in range(seq // Q_BLOCK):
        q0 = blk * Q_BLOCK
        k_len = q0 + Q_BLOCK
        z = jnp.einsum("bqhd,bkhd->bhqk", q[:, q0:k_len], k[:, :k_len]).astype(jnp.float32) * scale
        t_idx = q0 + jnp.arange(Q_BLOCK)
        s_idx = jnp.arange(k_len)
        mask = s_idx[None, :] < t_idx[:, None]
        log_1m = jnp.where(mask, jax.nn.log_sigmoid(-z), 0.0)
        after = lax.cumsum(log_1m, axis=3, reverse=True) - log_1m
        att = jnp.where(mask, jnp.exp(jax.nn.log_sigmoid(z) + after), 0.0)
        outs.append(jnp.einsum("bhqk,bkhd->bqhd", att.astype(v.dtype), v[:, :k_len]))
    return jnp.concatenate(outs, axis=1).reshape(bsz, seq, SB_WIDTH)


def short_conv_branch(xin, gate_b, gate_c, conv_w):
    return gate_b * causal_depthwise_conv(gate_c * xin, conv_w)


def _fwd_setup_inputs(seed: int = 0) -> dict:
    key = jax.random.key(seed)
    ks = jax.random.split(key, 16)
    f32 = jnp.float32

    def normal(k, shape, scale):
        return jax.random.normal(k, shape, f32) * scale

    def gain(k):
        return 1.0 + 0.02 * jax.random.normal(k, (DEPTH, D_MODEL), f32)

    dt = jnp.exp(jax.random.uniform(ks[5], (DEPTH, GDN_HEADS), f32, np.log(1e-3), np.log(1e-1)))
    return {
        "x": jax.random.normal(ks[0], (BATCH, SEQ, D_MODEL), f32),
        "norm_mix_pre": gain(ks[1]),
        "w_in": normal(ks[2], (DEPTH, D_MODEL, IN_PROJ_WIDTH), D_MODEL ** -0.5),
        "conv_qkv_w": normal(ks[3], (DEPTH, GDN_CONV, 3 * GDN_WIDTH), GDN_CONV ** -0.5),
        "gdn_a_log": jnp.log(jax.random.uniform(ks[4], (DEPTH, GDN_HEADS), f32, 1.0, 16.0)),
        "gdn_dt_bias": dt + jnp.log(-jnp.expm1(-dt)),
        "gdn_norm_w": 1.0 + 0.02 * jax.random.normal(ks[6], (DEPTH, GDN_HEAD_DIM), f32),
        "conv_sc_w": normal(ks[7], (DEPTH, SC_CONV, SC_WIDTH), SC_CONV ** -0.5),
        "w_branch": normal(ks[8], (DEPTH, N_BRANCH, BRANCH_WIDTH, D_MODEL), BRANCH_WIDTH ** -0.5),
        "w_out": normal(ks[9], (DEPTH, D_MODEL, D_MODEL), D_MODEL ** -0.5),
        "norm_mix_post": gain(ks[10]),
        "norm_ffn_pre": gain(ks[11]),
        "w_ff1": normal(ks[12], (DEPTH, D_MODEL, D_FF), D_MODEL ** -0.5),
        "w_ff2": normal(ks[13], (DEPTH, D_FF, D_MODEL), D_FF ** -0.5),
        "norm_ffn_post": gain(ks[14]),
    }


def _fwd_reference(x, norm_mix_pre, w_in, conv_qkv_w, gdn_a_log, gdn_dt_bias, gdn_norm_w, conv_sc_w,
              w_branch, w_out, norm_mix_post, norm_ffn_pre, w_ff1, w_ff2, norm_ffn_post):
    bsz, seq, _ = x.shape
    for l in range(DEPTH):
        h = rms_norm(x, norm_mix_pre[l])
        proj = h @ w_in[l]
        (gdn_qkv, gdn_gate, gdn_a, gdn_b, sb_qkv, sc_x, sc_b, sc_c, gates) = jnp.split(
            proj, SPLIT_POINTS, axis=-1)
        y_a = gdn_branch(gdn_qkv, gdn_gate, gdn_a, gdn_b, conv_qkv_w[l], gdn_a_log[l],
                         gdn_dt_bias[l], gdn_norm_w[l])
        y_b = stick_breaking_branch(sb_qkv)
        y_c = short_conv_branch(sc_x, sc_b, sc_c, conv_sc_w[l])
        gates = jax.nn.sigmoid(gates.reshape(bsz, seq, N_BRANCH, D_MODEL))
        merged = (gates[:, :, 0] * (y_a @ w_branch[l, 0])
                  + gates[:, :, 1] * (y_b @ w_branch[l, 1])
                  + gates[:, :, 2] * (y_c @ w_branch[l, 2]))
        x = x + rms_norm(merged @ w_out[l], norm_mix_post[l])
        h = rms_norm(x, norm_ffn_pre[l])
        f = jnp.square(jax.nn.relu(h @ w_ff1[l])) @ w_ff2[l]
        x = x + rms_norm(f, norm_ffn_post[l])
    return x


import jax as _jax
import jax.numpy as _jnp

TWIN_FORMAT = 'train_step'
FWD_PARAMS = ['x', 'norm_mix_pre', 'w_in', 'conv_qkv_w', 'gdn_a_log', 'gdn_dt_bias', 'gdn_norm_w', 'conv_sc_w', 'w_branch', 'w_out', 'norm_mix_post', 'norm_ffn_pre', 'w_ff1', 'w_ff2', 'norm_ffn_post']
TWIN_WEIGHTS = ['norm_mix_pre', 'w_in', 'conv_qkv_w', 'gdn_a_log', 'gdn_dt_bias', 'gdn_norm_w', 'conv_sc_w', 'w_branch', 'w_out', 'norm_mix_post', 'norm_ffn_pre', 'w_ff1', 'w_ff2', 'norm_ffn_post']
TWIN_DIFF_INPUT = 'x'
TWIN_INPUTS = ['x', 'norm_mix_pre', 'w_in', 'conv_qkv_w', 'gdn_a_log', 'gdn_dt_bias', 'gdn_norm_w', 'conv_sc_w', 'w_branch', 'w_out', 'norm_mix_post', 'norm_ffn_pre', 'w_ff1', 'w_ff2', 'norm_ffn_post', 'loss_target', 'm_norm_mix_pre', 'm_w_in', 'm_conv_qkv_w', 'm_gdn_a_log', 'm_gdn_dt_bias', 'm_gdn_norm_w', 'm_conv_sc_w', 'm_w_branch', 'm_w_out', 'm_norm_mix_post', 'm_norm_ffn_pre', 'm_w_ff1', 'm_w_ff2', 'm_norm_ffn_post', 'v_norm_mix_pre', 'v_w_in', 'v_conv_qkv_w', 'v_gdn_a_log', 'v_gdn_dt_bias', 'v_gdn_norm_w', 'v_conv_sc_w', 'v_w_branch', 'v_w_out', 'v_norm_mix_post', 'v_norm_ffn_pre', 'v_w_ff1', 'v_w_ff2', 'v_norm_ffn_post']
TWIN_OUTPUTS = ['loss', 'grad_x', 'grad_norm_mix_pre', 'grad_w_in', 'grad_conv_qkv_w', 'grad_gdn_a_log', 'grad_gdn_dt_bias', 'grad_gdn_norm_w', 'grad_conv_sc_w', 'grad_w_branch', 'grad_w_out', 'grad_norm_mix_post', 'grad_norm_ffn_pre', 'grad_w_ff1', 'grad_w_ff2', 'grad_norm_ffn_post', 'delta_norm_mix_pre', 'delta_w_in', 'delta_conv_qkv_w', 'delta_gdn_a_log', 'delta_gdn_dt_bias', 'delta_gdn_norm_w', 'delta_conv_sc_w', 'delta_w_branch', 'delta_w_out', 'delta_norm_mix_post', 'delta_norm_ffn_pre', 'delta_w_ff1', 'delta_w_ff2', 'delta_norm_ffn_post', 'new_m_norm_mix_pre', 'new_m_w_in', 'new_m_conv_qkv_w', 'new_m_gdn_a_log', 'new_m_gdn_dt_bias', 'new_m_gdn_norm_w', 'new_m_conv_sc_w', 'new_m_w_branch', 'new_m_w_out', 'new_m_norm_mix_post', 'new_m_norm_ffn_pre', 'new_m_w_ff1', 'new_m_w_ff2', 'new_m_norm_ffn_post', 'new_v_norm_mix_pre', 'new_v_w_in', 'new_v_conv_qkv_w', 'new_v_gdn_a_log', 'new_v_gdn_dt_bias', 'new_v_gdn_norm_w', 'new_v_conv_sc_w', 'new_v_w_branch', 'new_v_w_out', 'new_v_norm_mix_post', 'new_v_norm_ffn_pre', 'new_v_w_ff1', 'new_v_w_ff2', 'new_v_norm_ffn_post']
TWIN_LEAF_KINDS = {'loss': 'loss', 'grad_x': 'grad_x', 'grad_norm_mix_pre': 'grad_w', 'grad_w_in': 'grad_w', 'grad_conv_qkv_w': 'grad_w', 'grad_gdn_a_log': 'grad_w', 'grad_gdn_dt_bias': 'grad_w', 'grad_gdn_norm_w': 'grad_w', 'grad_conv_sc_w': 'grad_w', 'grad_w_branch': 'grad_w', 'grad_w_out': 'grad_w', 'grad_norm_mix_post': 'grad_w', 'grad_norm_ffn_pre': 'grad_w', 'grad_w_ff1': 'grad_w', 'grad_w_ff2': 'grad_w', 'grad_norm_ffn_post': 'grad_w', 'delta_norm_mix_pre': 'delta_w', 'delta_w_in': 'delta_w', 'delta_conv_qkv_w': 'delta_w', 'delta_gdn_a_log': 'delta_w', 'delta_gdn_dt_bias': 'delta_w', 'delta_gdn_norm_w': 'delta_w', 'delta_conv_sc_w': 'delta_w', 'delta_w_branch': 'delta_w', 'delta_w_out': 'delta_w', 'delta_norm_mix_post': 'delta_w', 'delta_norm_ffn_pre': 'delta_w', 'delta_w_ff1': 'delta_w', 'delta_w_ff2': 'delta_w', 'delta_norm_ffn_post': 'delta_w', 'new_m_norm_mix_pre': 'new_m', 'new_m_w_in': 'new_m', 'new_m_conv_qkv_w': 'new_m', 'new_m_gdn_a_log': 'new_m', 'new_m_gdn_dt_bias': 'new_m', 'new_m_gdn_norm_w': 'new_m', 'new_m_conv_sc_w': 'new_m', 'new_m_w_branch': 'new_m', 'new_m_w_out': 'new_m', 'new_m_norm_mix_post': 'new_m', 'new_m_norm_ffn_pre': 'new_m', 'new_m_w_ff1': 'new_m', 'new_m_w_ff2': 'new_m', 'new_m_norm_ffn_post': 'new_m', 'new_v_norm_mix_pre': 'new_v', 'new_v_w_in': 'new_v', 'new_v_conv_qkv_w': 'new_v', 'new_v_gdn_a_log': 'new_v', 'new_v_gdn_dt_bias': 'new_v', 'new_v_gdn_norm_w': 'new_v', 'new_v_conv_sc_w': 'new_v', 'new_v_w_branch': 'new_v', 'new_v_w_out': 'new_v', 'new_v_norm_mix_post': 'new_v', 'new_v_norm_ffn_pre': 'new_v', 'new_v_w_ff1': 'new_v', 'new_v_w_ff2': 'new_v', 'new_v_norm_ffn_post': 'new_v'}


def _forward(args):
    return _fwd_reference(*[args[k] for k in FWD_PARAMS])


def _output_shape():
    out = _jax.eval_shape(lambda: _forward(_fwd_setup_inputs(0)))
    return out.shape, out.dtype

N_MICROBATCH = 1
ADAM_LR = 0.001
ADAM_B1 = 0.9
ADAM_B2 = 0.999
ADAM_EPS = 1e-08
ADAM_WD = 0.01
ADAM_STEP = 10
PER_EXAMPLE_BATCH_AXIS = {'x': 0, 'loss_target': 0}
SHARED_INPUTS = []
_WEIGHT_DTYPES = {'norm_mix_pre': _jnp.float32, 'w_in': _jnp.float32, 'conv_qkv_w': _jnp.float32, 'gdn_a_log': _jnp.float32, 'gdn_dt_bias': _jnp.float32, 'gdn_norm_w': _jnp.float32, 'conv_sc_w': _jnp.float32, 'w_branch': _jnp.float32, 'w_out': _jnp.float32, 'norm_mix_post': _jnp.float32, 'norm_ffn_pre': _jnp.float32, 'w_ff1': _jnp.float32, 'w_ff2': _jnp.float32, 'norm_ffn_post': _jnp.float32}
MOMENT_SCALE = {'norm_mix_pre': 4.283479e+00, 'w_in': 1.565017e+00, 'conv_qkv_w': 1.624641e+00, 'gdn_a_log': 4.582258e+00, 'gdn_dt_bias': 4.356744e+00, 'gdn_norm_w': 8.350572e+00, 'conv_sc_w': 1.322629e+00, 'w_branch': 3.073861e+00, 'w_out': 5.368278e+00, 'norm_mix_post': 3.223987e+01, 'norm_ffn_pre': 4.026880e+00, 'w_ff1': 1.993659e+00, 'w_ff2': 1.342916e+01, 'norm_ffn_post': 3.600949e+01}


def _to_microbatches(a, axis):
    t = _jnp.moveaxis(a, axis, 0)
    t = t.reshape((N_MICROBATCH, t.shape[0] // N_MICROBATCH) + t.shape[1:])
    return _jnp.moveaxis(t, 1, axis + 1)


def setup_inputs(seed: int = 0) -> dict:
    inp = _fwd_setup_inputs(seed)
    key = _jax.random.fold_in(_jax.random.key(seed), 7919)
    shape, _ = _output_shape()
    out = dict(inp)
    out["loss_target"] = _jax.random.normal(_jax.random.fold_in(key, 0), shape, _jnp.float32)
    for i, name in enumerate(TWIN_WEIGHTS):
        w = inp[name].astype(_jnp.float32)
        if MOMENT_SCALE is None:
            s = _jnp.sqrt(_jnp.mean(_jnp.square(w)) + 1e-30)
        else:
            s = MOMENT_SCALE[name]
        km, kv = _jax.random.split(_jax.random.fold_in(key, i + 1))
        out[name] = w
        out["m_" + name] = s * _jax.random.normal(km, w.shape, _jnp.float32)
        out["v_" + name] = (s * s) * _jax.random.uniform(kv, w.shape, _jnp.float32, 0.5, 1.5)
    if N_MICROBATCH > 1:
        for name, axis in PER_EXAMPLE_BATCH_AXIS.items():
            out[name] = _to_microbatches(out[name], axis)
    return {'x': out['x'], 'norm_mix_pre': out['norm_mix_pre'], 'w_in': out['w_in'], 'conv_qkv_w': out['conv_qkv_w'], 'gdn_a_log': out['gdn_a_log'], 'gdn_dt_bias': out['gdn_dt_bias'], 'gdn_norm_w': out['gdn_norm_w'], 'conv_sc_w': out['conv_sc_w'], 'w_branch': out['w_branch'], 'w_out': out['w_out'], 'norm_mix_post': out['norm_mix_post'], 'norm_ffn_pre': out['norm_ffn_pre'], 'w_ff1': out['w_ff1'], 'w_ff2': out['w_ff2'], 'norm_ffn_post': out['norm_ffn_post'], 'loss_target': out['loss_target'], 'm_norm_mix_pre': out['m_norm_mix_pre'], 'm_w_in': out['m_w_in'], 'm_conv_qkv_w': out['m_conv_qkv_w'], 'm_gdn_a_log': out['m_gdn_a_log'], 'm_gdn_dt_bias': out['m_gdn_dt_bias'], 'm_gdn_norm_w': out['m_gdn_norm_w'], 'm_conv_sc_w': out['m_conv_sc_w'], 'm_w_branch': out['m_w_branch'], 'm_w_out': out['m_w_out'], 'm_norm_mix_post': out['m_norm_mix_post'], 'm_norm_ffn_pre': out['m_norm_ffn_pre'], 'm_w_ff1': out['m_w_ff1'], 'm_w_ff2': out['m_w_ff2'], 'm_norm_ffn_post': out['m_norm_ffn_post'], 'v_norm_mix_pre': out['v_norm_mix_pre'], 'v_w_in': out['v_w_in'], 'v_conv_qkv_w': out['v_conv_qkv_w'], 'v_gdn_a_log': out['v_gdn_a_log'], 'v_gdn_dt_bias': out['v_gdn_dt_bias'], 'v_gdn_norm_w': out['v_gdn_norm_w'], 'v_conv_sc_w': out['v_conv_sc_w'], 'v_w_branch': out['v_w_branch'], 'v_w_out': out['v_w_out'], 'v_norm_mix_post': out['v_norm_mix_post'], 'v_norm_ffn_pre': out['v_norm_ffn_pre'], 'v_w_ff1': out['v_w_ff1'], 'v_w_ff2': out['v_w_ff2'], 'v_norm_ffn_post': out['v_norm_ffn_post']}


def _loss(weights, diff, rest, loss_target):
    with _jax.named_scope("forward"):
        args = {**rest, TWIN_DIFF_INPUT: diff, **{k: w.astype(_WEIGHT_DTYPES[k]) for k, w in weights.items()}}
        y = _forward(args)
    with _jax.named_scope("loss_head"):
        err = _jnp.square(y.astype(_jnp.float32) - loss_target)
        return 0.5 * _jnp.sum(_jnp.mean(err, axis=-1)) if err.ndim else 0.5 * err


def _adamw(w, g, m, v):
    m = ADAM_B1 * m + (1.0 - ADAM_B1) * g
    v = ADAM_B2 * v + (1.0 - ADAM_B2) * _jnp.square(g)
    m_hat = m / (1.0 - ADAM_B1 ** ADAM_STEP)
    v_hat = v / (1.0 - ADAM_B2 ** ADAM_STEP)
    delta = -ADAM_LR * (m_hat / (_jnp.sqrt(v_hat) + ADAM_EPS) + ADAM_WD * w)
    return delta, m, v


def reference(x, norm_mix_pre, w_in, conv_qkv_w, gdn_a_log, gdn_dt_bias, gdn_norm_w, conv_sc_w, w_branch, w_out, norm_mix_post, norm_ffn_pre, w_ff1, w_ff2, norm_ffn_post, loss_target, m_norm_mix_pre, m_w_in, m_conv_qkv_w, m_gdn_a_log, m_gdn_dt_bias, m_gdn_norm_w, m_conv_sc_w, m_w_branch, m_w_out, m_norm_mix_post, m_norm_ffn_pre, m_w_ff1, m_w_ff2, m_norm_ffn_post, v_norm_mix_pre, v_w_in, v_conv_qkv_w, v_gdn_a_log, v_gdn_dt_bias, v_gdn_norm_w, v_conv_sc_w, v_w_branch, v_w_out, v_norm_mix_post, v_norm_ffn_pre, v_w_ff1, v_w_ff2, v_norm_ffn_post):
    given = dict(x=x, norm_mix_pre=norm_mix_pre, w_in=w_in, conv_qkv_w=conv_qkv_w, gdn_a_log=gdn_a_log, gdn_dt_bias=gdn_dt_bias, gdn_norm_w=gdn_norm_w, conv_sc_w=conv_sc_w, w_branch=w_branch, w_out=w_out, norm_mix_post=norm_mix_post, norm_ffn_pre=norm_ffn_pre, w_ff1=w_ff1, w_ff2=w_ff2, norm_ffn_post=norm_ffn_post, loss_target=loss_target, m_norm_mix_pre=m_norm_mix_pre, m_w_in=m_w_in, m_conv_qkv_w=m_conv_qkv_w, m_gdn_a_log=m_gdn_a_log, m_gdn_dt_bias=m_gdn_dt_bias, m_gdn_norm_w=m_gdn_norm_w, m_conv_sc_w=m_conv_sc_w, m_w_branch=m_w_branch, m_w_out=m_w_out, m_norm_mix_post=m_norm_mix_post, m_norm_ffn_pre=m_norm_ffn_pre, m_w_ff1=m_w_ff1, m_w_ff2=m_w_ff2, m_norm_ffn_post=m_norm_ffn_post, v_norm_mix_pre=v_norm_mix_pre, v_w_in=v_w_in, v_conv_qkv_w=v_conv_qkv_w, v_gdn_a_log=v_gdn_a_log, v_gdn_dt_bias=v_gdn_dt_bias, v_gdn_norm_w=v_gdn_norm_w, v_conv_sc_w=v_conv_sc_w, v_w_branch=v_w_branch, v_w_out=v_w_out, v_norm_mix_post=v_norm_mix_post, v_norm_ffn_pre=v_norm_ffn_pre, v_w_ff1=v_w_ff1, v_w_ff2=v_w_ff2, v_norm_ffn_post=v_norm_ffn_post)
    weights = {n: given[n] for n in TWIN_WEIGHTS}
    shared = {n: given[n] for n in SHARED_INPUTS}
    per_example = {n: given[n] for n in ['x']}
    grad_fn = _jax.value_and_grad(_loss, argnums=(0, 1))

    def one_microbatch(ex, loss_target):
        ex = dict(ex)
        diff = ex.pop(TWIN_DIFF_INPUT)
        return grad_fn(weights, diff, {**shared, **ex}, loss_target)

    if N_MICROBATCH == 1:
        loss, (grad_w, grad_x) = one_microbatch(per_example, given["loss_target"])
    else:
        def body(carry, xs):
            loss_sum, grad_sum = carry
            l_k, (gw_k, gx_k) = one_microbatch(xs[0], xs[1])
            with _jax.named_scope("update"):
                return (loss_sum + l_k, _jax.tree.map(_jnp.add, grad_sum, gw_k)), gx_k

        init = (_jnp.zeros((), _jnp.float32), _jax.tree.map(_jnp.zeros_like, weights))
        (loss, grad_w), grad_x = _jax.lax.scan(body, init, (per_example, given["loss_target"]))
    with _jax.named_scope("update"):
        delta_w, new_m, new_v = {}, {}, {}
        for n in TWIN_WEIGHTS:
            delta_w[n], new_m[n], new_v[n] = _adamw(weights[n], grad_w[n], given["m_" + n], given["v_" + n])
    return (loss, grad_x, *[grad_w[n] for n in TWIN_WEIGHTS], *[delta_w[n] for n in TWIN_WEIGHTS],
            *[new_m[n] for n in TWIN_WEIGHTS], *[new_v[n] for n in TWIN_WEIGHTS])
```

```python
import functools

import jax
import jax.numpy as jnp
from jax import lax
from jax.experimental import pallas as pl
from jax.experimental.pallas import tpu as pltpu

F32 = jnp.float32
BF16 = jnp.bfloat16
MESH = pl.DeviceIdType.MESH

LANES = 128
D_MODEL = 1024
DEPTH = 4
CHUNK = 64
GDN_HEADS, GDN_DIM = 4, 128
SB_HEADS, SB_DIM = 8, 64
BRANCH_W = 512
N_BRANCH = 3
D_FF = 4 * D_MODEL
EPS = 1e-6
IN_W = 8200
AB_COL = 2048
AB_PAD = LANES - 8
IN_WP = IN_W + AB_PAD
N_CHIPS = 4
N_DEV = 8
GATES_COL = 5128
PB_GATES, PB_QKV, PB_GATE, PB_AB, PB_SB, PB_SCX, PB_SCB, PB_SCC = 0, 24, 36, 40, 41, 53, 57, 61
SB_TILE = 128
SB_SCALE = SB_DIM ** -0.5
GDN_QSCALE = GDN_DIM ** -0.5
VMEM_LIMIT = 56 * 1024 * 1024

ADAM_LR, ADAM_B1, ADAM_B2, ADAM_EPS, ADAM_WD, ADAM_STEP = 0.001, 0.9, 0.999, 1e-08, 0.01, 10

NT = (((1,), (1,)), ((), ()))
TN = (((0,), (0,)), ((), ()))
HI = lax.Precision.HIGHEST


def _pad_in_cols(w):
    return jnp.concatenate([w[:, GATES_COL:], w[:, :AB_COL + 8], jnp.zeros((w.shape[0], AB_PAD), w.dtype),
                            w[:, AB_COL + 8:GATES_COL]], axis=1)


def _unpad_in_cols(g):
    n_gates = IN_W - GATES_COL
    return jnp.concatenate([g[:, n_gates:n_gates + AB_COL + 8], g[:, n_gates + AB_COL + 8 + AB_PAD:], g[:, :n_gates]], axis=1)


def _params(*sem):
    return pltpu.CompilerParams(dimension_semantics=sem if sem else None, vmem_limit_bytes=VMEM_LIMIT)


def _sigmoid(x):
    return 1.0 / (1.0 + jnp.exp(-x))


def _softplus(x):
    return jnp.maximum(x, 0.0) + jnp.log(1.0 + jnp.exp(-jnp.abs(x)))


def _dot(a, b, dims=None, precision=None):
    if dims is None:
        return jnp.dot(a, b, preferred_element_type=F32, precision=precision)
    return lax.dot_general(a, b, dims, preferred_element_type=F32, precision=precision)


def _bdot(a, b, dims=None):
    return _dot(a.astype(BF16), b.astype(BF16), dims)


def _matmul(a, b, *, name, ta=False, tb=False, tm, tn, tk=None, outs=(F32,), epi=None, extras=()):
    if ta:
        kdim, m = a.shape
    else:
        m, kdim = a.shape
    if tb:
        n, kb = b.shape
    else:
        kb, n = b.shape
    assert kdim == kb, (a.shape, b.shape)
    tk = kdim if tk is None else tk
    assert m % tm == 0 and n % tn == 0 and kdim % tk == 0, (m, n, kdim, tm, tn, tk)
    nk = kdim // tk
    a_spec = pl.BlockSpec((tk, tm), lambda i, j, k: (k, i)) if ta else pl.BlockSpec((tm, tk), lambda i, j, k: (i, k))
    b_spec = pl.BlockSpec((tn, tk), lambda i, j, k: (j, k)) if tb else pl.BlockSpec((tk, tn), lambda i, j, k: (k, j))
    mn_spec = pl.BlockSpec((tm, tn), lambda i, j, k: (i, j))
    dims = (((0 if ta else 1,), (1 if tb else 0,)), ((), ()))
    n_ex, n_out = len(extras), len(outs)

    def body(a_ref, b_ref, *rest):
        ex, o, acc = rest[:n_ex], rest[n_ex:n_ex + n_out], rest[n_ex + n_out:]
        part = lax.dot_general(a_ref[...].astype(BF16), b_ref[...].astype(BF16), dims, preferred_element_type=F32)

        def finish(val):
            res = epi(val, *[e[...] for e in ex]) if epi is not None else (val,)
            for r, oref in zip(res, o):
                oref[...] = r.astype(oref.dtype)

        if nk == 1:
            finish(part)
        else:
            k = pl.program_id(2)

            @pl.when(k == 0)
            def _():
                acc[0][...] = part

            @pl.when(k > 0)
            def _():
                acc[0][...] += part

            @pl.when(k == nk - 1)
            def _():
                finish(acc[0][...])

    res = pl.pallas_call(
        body, name=name, grid=(m // tm, n // tn, nk),
        in_specs=[a_spec, b_spec] + [mn_spec] * n_ex,
        out_specs=[mn_spec] * n_out,
        out_shape=[jax.ShapeDtypeStruct((m, n), dt) for dt in outs],
        scratch_shapes=[pltpu.VMEM((tm, tn), F32)] if nk > 1 else [],
        compiler_params=_params("parallel", "parallel", "arbitrary"),
    )(a, b, *extras)
    return res[0] if n_out == 1 else res


ROW_TILE = 512


def _norm_fwd(x, w, *, name):
    s, d = x.shape

    def body(x_ref, w_ref, o_ref):
        xv = x_ref[...]
        r = lax.rsqrt(jnp.mean(xv * xv, axis=-1, keepdims=True) + EPS)
        o_ref[...] = (xv * r * w_ref[...]).astype(o_ref.dtype)

    return pl.pallas_call(
        body, name=name, grid=(s // ROW_TILE,),
        in_specs=[pl.BlockSpec((ROW_TILE, d), lambda i: (i, 0)), pl.BlockSpec((1, d), lambda i: (0, 0))],
        out_specs=pl.BlockSpec((ROW_TILE, d), lambda i: (i, 0)),
        out_shape=jax.ShapeDtypeStruct((s, d), BF16), compiler_params=_params("parallel"),
    )(x, w.reshape(1, d))


def _resnorm_fwd(x, u, w, *, name):
    s, d = x.shape

    def body(x_ref, u_ref, w_ref, o_ref):
        uv = u_ref[...]
        r = lax.rsqrt(jnp.mean(uv * uv, axis=-1, keepdims=True) + EPS)
        o_ref[...] = x_ref[...] + uv * r * w_ref[...]

    row = pl.BlockSpec((ROW_TILE, d), lambda i: (i, 0))
    return pl.pallas_call(
        body, name=name, grid=(s // ROW_TILE,),
        in_specs=[row, row, pl.BlockSpec((1, d), lambda i: (0, 0))], out_specs=row,
        out_shape=jax.ShapeDtypeStruct((s, d), F32), compiler_params=_params("parallel"),
    )(x, u, w.reshape(1, d))


def _norm_bwd(xin, w, dy, res, *, out_dtype, name):
    s, d = xin.shape
    has_res = res is not None

    def body(*refs):
        x_ref, w_ref, dy_ref = refs[:3]
        res_ref = refs[3] if has_res else None
        dx_ref, dw_ref = refs[3 + has_res:]
        xv, dyv = x_ref[...], dy_ref[...].astype(F32)
        r = lax.rsqrt(jnp.mean(xv * xv, axis=-1, keepdims=True) + EPS)
        xh = xv * r
        g = dyv * w_ref[...]
        dx = r * (g - xh * jnp.mean(g * xh, axis=-1, keepdims=True))
        if has_res:
            dx = dx + res_ref[...]
        dx_ref[...] = dx.astype(dx_ref.dtype)

        @pl.when(pl.program_id(0) == 0)
        def _():
            dw_ref[...] = jnp.zeros_like(dw_ref)

        dw_ref[...] += jnp.sum(dyv * xh, axis=0, keepdims=True)

    row = pl.BlockSpec((ROW_TILE, d), lambda i: (i, 0))
    vec = pl.BlockSpec((1, d), lambda i: (0, 0))
    ins = [xin, w.reshape(1, d), dy] + ([res] if has_res else [])
    dx, dw = pl.pallas_call(
        body, name=name, grid=(s // ROW_TILE,),
        in_specs=[row, vec, row] + ([row] if has_res else []), out_specs=[row, vec],
        out_shape=[jax.ShapeDtypeStruct((s, d), out_dtype), jax.ShapeDtypeStruct((1, d), F32)],
        compiler_params=_params("arbitrary"),
    )(*ins)
    return dx, dw.reshape(d)


def _loss_fwd_bwd(y, target, *, name):
    s, d = y.shape

    def body(y_ref, t_ref, loss_ref, dy_ref):
        e = y_ref[...] - t_ref[...]
        dy_ref[...] = e * (1.0 / d)

        @pl.when(pl.program_id(0) == 0)
        def _():
            loss_ref[...] = jnp.zeros_like(loss_ref)

        part = jnp.sum(jnp.sum(e * e, axis=1, keepdims=True), axis=0, keepdims=True)
        loss_ref[...] += part * (0.5 / d)

    row = pl.BlockSpec((ROW_TILE, d), lambda i: (i, 0))
    loss, dy = pl.pallas_call(
        body, name=name, grid=(s // ROW_TILE,), in_specs=[row, row],
        out_specs=[pl.BlockSpec((1, 1), lambda i: (0, 0)), row],
        out_shape=[jax.ShapeDtypeStruct((1, 1), F32), jax.ShapeDtypeStruct((s, d), F32)],
        compiler_params=_params("arbitrary"),
    )(y, target)
    return loss[0, 0], dy


def _shift_down(x, k, rows):
    if k == 0:
        return x
    return jnp.where(rows >= k, pltpu.roll(x, k, 0), 0.0)


def _shift_up(x, k, rows):
    if k == 0:
        return x
    n = x.shape[0]
    return jnp.where(rows < n - k, pltpu.roll(x, n - k, 0), 0.0)


def _col_spec(s, base):
    return pl.BlockSpec((s, LANES), lambda j: (0, base + j))


def _gdn_pre_math(x, w, j, rows):
    taps = w.shape[0]
    c = w[taps - 1:taps, :] * x
    for i in range(taps - 1):
        c = c + w[i:i + 1, :] * _shift_down(x, taps - 1 - i, rows)
    sg = _sigmoid(c)
    y = c * sg
    r = lax.rsqrt(jnp.sum(y * y, axis=-1, keepdims=True) + EPS)
    is_qk = j < 2 * GDN_HEADS
    scale = jnp.where(j < GDN_HEADS, GDN_QSCALE, 1.0)
    return c, sg, y, r, is_qk, scale


def _gdn_pre_fwd(proj, conv_w, *, name):
    s = proj.shape[0]

    def body(x_ref, w_ref, o_ref):
        j = pl.program_id(0)
        rows = lax.broadcasted_iota(jnp.int32, (s, LANES), 0)
        _, _, y, r, is_qk, scale = _gdn_pre_math(x_ref[...], w_ref[...], j, rows)
        o_ref[...] = jnp.where(is_qk, y * (r * scale), y)

    return pl.pallas_call(
        body, name=name, grid=(12,),
        in_specs=[_col_spec(s, PB_QKV), pl.BlockSpec((4, LANES), lambda j: (0, j))],
        out_specs=_col_spec(s, 0), out_shape=jax.ShapeDtypeStruct((s, 3 * BRANCH_W), F32),
        compiler_params=_params("parallel"),
    )(proj, conv_w)


def _gdn_pre_bwd(proj, conv_w, dqkvn, *, name):
    s = proj.shape[0]

    def body(x_ref, w_ref, d_ref, dx_ref, dw_ref):
        j = pl.program_id(0)
        rows = lax.broadcasted_iota(jnp.int32, (s, LANES), 0)
        x, w, dout = x_ref[...], w_ref[...], d_ref[...]
        c, sg, y, r, is_qk, scale = _gdn_pre_math(x, w, j, rows)
        yh = y * r
        dy_n = (scale * r) * (dout - yh * jnp.sum(dout * yh, axis=-1, keepdims=True))
        dy = jnp.where(is_qk, dy_n, dout)
        dc = dy * (sg * (1.0 + c * (1.0 - sg)))
        taps = w.shape[0]
        dx = w[taps - 1:taps, :] * dc
        dws = []
        for i in range(taps - 1):
            k = taps - 1 - i
            dx = dx + w[i:i + 1, :] * _shift_up(dc, k, rows)
            dws.append(jnp.sum(dc * _shift_down(x, k, rows), axis=0, keepdims=True))
        dws.append(jnp.sum(dc * x, axis=0, keepdims=True))
        dx_ref[...] = dx.astype(dx_ref.dtype)
        for i in range(taps):
            dw_ref[i:i + 1, :] = dws[i]

    return pl.pallas_call(
        body, name=name, grid=(12,),
        in_specs=[_col_spec(s, PB_QKV), pl.BlockSpec((4, LANES), lambda j: (0, j)), _col_spec(s, 0)],
        out_specs=[_col_spec(s, 0), pl.BlockSpec((4, LANES), lambda j: (0, j))],
        out_shape=[jax.ShapeDtypeStruct((s, 3 * BRANCH_W), BF16), jax.ShapeDtypeStruct((4, 3 * BRANCH_W), F32)],
        compiler_params=_params("parallel"),
    )(proj, conv_w, dqkvn)


def _lane_pad(v):
    return jnp.pad(v.reshape(1, -1), ((0, 0), (0, LANES - v.shape[0])))


def _gdn_gates_fwd(proj, a_log, dt_bias, *, name):
    s = proj.shape[0]

    def body(ab_ref, al_ref, dt_ref, o_ref):
        ab = ab_ref[...]
        lane = lax.broadcasted_iota(jnp.int32, (1, LANES), 1)
        g = -jnp.exp(al_ref[...]) * _softplus(ab + dt_ref[...])
        o_ref[...] = jnp.where(lane < GDN_HEADS, g, _sigmoid(ab))

    vec = pl.BlockSpec((1, LANES), lambda j: (0, 0))
    return pl.pallas_call(
        body, name=name, grid=(1,), in_specs=[_col_spec(s, PB_AB), vec, vec], out_specs=_col_spec(s, 0),
        out_shape=jax.ShapeDtypeStruct((s, LANES), F32), compiler_params=_params("arbitrary"),
    )(proj, _lane_pad(a_log), _lane_pad(dt_bias))


def _gdn_gates_bwd(proj, a_log, dt_bias, dgb, *, name):
    s = proj.shape[0]

    def body(ab_ref, al_ref, dt_ref, d_ref, dab_ref, dal_ref, ddt_ref):
        ab, d = ab_ref[...], d_ref[...]
        lane = lax.broadcasted_iota(jnp.int32, (1, LANES), 1)
        ea = jnp.exp(al_ref[...])
        pre = ab + dt_ref[...]
        g = -ea * _softplus(pre)
        dpre = d * (-ea) * _sigmoid(pre)
        beta = _sigmoid(ab)
        is_g = lane < GDN_HEADS
        dab = jnp.where(is_g, dpre, jnp.where(lane < 2 * GDN_HEADS, d * beta * (1.0 - beta), 0.0))
        dab_ref[...] = dab.astype(dab_ref.dtype)
        dal_ref[...] = jnp.sum(jnp.where(is_g, d * g, 0.0), axis=0, keepdims=True)
        ddt_ref[...] = jnp.sum(jnp.where(is_g, dpre, 0.0), axis=0, keepdims=True)

    vec = pl.BlockSpec((1, LANES), lambda j: (0, 0))
    dab, dal, ddt = pl.pallas_call(
        body, name=name, grid=(1,), in_specs=[_col_spec(s, PB_AB), vec, vec, _col_spec(s, 0)],
        out_specs=[_col_spec(s, 0), vec, vec],
        out_shape=[jax.ShapeDtypeStruct((s, LANES), BF16), jax.ShapeDtypeStruct((1, LANES), F32),
                   jax.ShapeDtypeStruct((1, LANES), F32)],
        compiler_params=_params("arbitrary"),
    )(proj, _lane_pad(a_log), _lane_pad(dt_bias), dgb)
    return dab, dal[0, :GDN_HEADS], ddt[0, :GDN_HEADS]


def _chunk_common(q, k, v, gb, gbt, h):
    c = CHUNK
    row = lax.broadcasted_iota(jnp.int32, (c, c), 0)
    col = lax.broadcasted_iota(jnp.int32, (c, c), 1)
    tril, strict, eye = row >= col, row > col, row == col
    lane = lax.broadcasted_iota(jnp.int32, (c, LANES), 1)
    sub = lax.broadcasted_iota(jnp.int32, (2 * GDN_HEADS, c), 0)
    g_col = jnp.sum(jnp.where(lane == h, gb, 0.0), axis=1, keepdims=True)
    beta_col = jnp.sum(jnp.where(lane == GDN_HEADS + h, gb, 0.0), axis=1, keepdims=True)
    g_row = jnp.sum(jnp.where(sub == h, gbt, 0.0), axis=0, keepdims=True)
    gc_col = jnp.sum(jnp.where(tril, jnp.broadcast_to(g_row, (c, c)), 0.0), axis=1, keepdims=True)
    gc_row = jnp.sum(jnp.where(row <= col, jnp.broadcast_to(g_col, (c, c)), 0.0), axis=0, keepdims=True)
    g_tot = jnp.sum(g_row, axis=1, keepdims=True)
    dm = jnp.exp(jnp.where(tril, gc_col - gc_row, -1e30))
    e_col = jnp.exp(gc_col)
    kdec_col = jnp.exp(g_tot - gc_col)
    gamma = jnp.exp(g_tot)
    kb = k * beta_col
    vb = v * beta_col
    kbg = kb * e_col
    a = jnp.where(strict, _dot(kb, k, NT, HI) * dm, 0.0)
    bneg = -a
    t = jnp.where(eye, 1.0, 0.0) + bneg
    p = _dot(bneg, bneg, precision=HI)
    for lvl in range(5):
        t = t + _dot(t, p, precision=HI)
        if lvl < 4:
            p = _dot(p, p, precision=HI)
    u = _dot(t, vb, precision=HI)
    w = _dot(t, kbg, precision=HI)
    aqk = jnp.where(tril, _bdot(q, k, NT) * dm, 0.0)
    return dict(tril=tril, strict=strict, eye=eye, row=row, col=col, beta_col=beta_col, dm=dm, e_col=e_col,
                kdec_col=kdec_col, gamma=gamma, kb=kb, vb=vb, kbg=kbg, a=a, t=t, u=u, w=w, aqk=aqk,
                qd=q * e_col, kd=k * kdec_col)


def _gdn_chunk_fwd(qkvn, gb, gbt, *, name):
    s = qkvn.shape[0]
    n_chunks = s // CHUNK

    def body(q_ref, k_ref, v_ref, gb_ref, gbt_ref, o_ref, st_ref, state):
        @pl.when(pl.program_id(0) == 0)
        def _():
            state[...] = jnp.zeros_like(state)

        gbv, gbtv = gb_ref[...], gbt_ref[0]
        for h in range(GDN_HEADS):
            hs = slice(h * GDN_DIM, (h + 1) * GDN_DIM)
            q, k, v = q_ref[:, hs], k_ref[:, hs], v_ref[:, hs]
            m = _chunk_common(q, k, v, gbv, gbtv, h)
            s0 = state[h]
            st_ref[0, h] = s0
            vnew = m["u"] - _bdot(m["w"], s0)
            o_ref[:, hs] = _bdot(m["qd"], s0) + _bdot(m["aqk"], vnew)
            state[h] = m["gamma"] * s0 + _bdot(m["kd"], vnew, TN)

    blk = lambda j: pl.BlockSpec((CHUNK, BRANCH_W), lambda n: (n, j))
    return pl.pallas_call(
        body, name=name, grid=(n_chunks,),
        in_specs=[blk(0), blk(1), blk(2), pl.BlockSpec((CHUNK, LANES), lambda n: (n, 0)),
                  pl.BlockSpec((1, 2 * GDN_HEADS, CHUNK), lambda n: (n, 0, 0))],
        out_specs=[blk(0), pl.BlockSpec((1, GDN_HEADS, GDN_DIM, GDN_DIM), lambda n: (n, 0, 0, 0))],
        out_shape=[jax.ShapeDtypeStruct((s, BRANCH_W), F32),
                   jax.ShapeDtypeStruct((n_chunks, GDN_HEADS, GDN_DIM, GDN_DIM), F32)],
        scratch_shapes=[pltpu.VMEM((GDN_HEADS, GDN_DIM, GDN_DIM), F32)],
        compiler_params=_params("arbitrary"),
    )(qkvn, qkvn, qkvn, gb, gbt)


def _gdn_chunk_bwd(qkvn, gb, gbt, states, do, *, name):
    s = qkvn.shape[0]
    n_chunks = s // CHUNK
    c = CHUNK

    def body(q_ref, k_ref, v_ref, gb_ref, gbt_ref, st_ref, do_ref, dq_ref, dk_ref, dv_ref, dgb_ref, dstate):
        @pl.when(pl.program_id(0) == 0)
        def _():
            dstate[...] = jnp.zeros_like(dstate)

        gbv, gbtv = gb_ref[...], gbt_ref[0]
        lane = lax.broadcasted_iota(jnp.int32, (c, LANES), 1)
        dgb = jnp.zeros((c, LANES), F32)
        for h in range(GDN_HEADS):
            hs = slice(h * GDN_DIM, (h + 1) * GDN_DIM)
            q, k, v, dov = q_ref[:, hs], k_ref[:, hs], v_ref[:, hs], do_ref[:, hs]
            m = _chunk_common(q, k, v, gbv, gbtv, h)
            tril, strict, eye, row, col = m["tril"], m["strict"], m["eye"], m["row"], m["col"]
            s0, ds1 = st_ref[0, h], dstate[h]
            vnew = m["u"] - _bdot(m["w"], s0)
            dvnew = _bdot(m["aqk"], dov, TN) + _bdot(m["kd"], ds1)
            dqd = _bdot(dov, s0, NT)
            daqk = jnp.where(tril, _bdot(dov, vnew, NT), 0.0)
            dkd = _bdot(vnew, ds1, NT)
            dgamma = jnp.sum(jnp.sum(s0 * ds1, axis=1, keepdims=True), axis=0, keepdims=True)
            dw = -_bdot(dvnew, s0, NT)
            dstate[h] = m["gamma"] * ds1 + _bdot(m["qd"], dov, TN) - _bdot(m["w"], dvnew, TN)
            dt = _dot(dvnew, m["vb"], NT, HI) + _dot(dw, m["kbg"], NT, HI)
            dvb = _dot(m["t"], dvnew, TN, HI)
            dkbg = _dot(m["t"], dw, TN, HI)
            da = jnp.where(strict, -_dot(_dot(m["t"], dt, TN, HI), m["t"], NT, HI), 0.0)
            dmat = da * m["dm"]
            dmq = daqk * m["dm"]
            dkb = _dot(dmat, k, precision=HI) + dkbg * m["e_col"]
            dq = _bdot(dmq, k) + dqd * m["e_col"]
            dk = (_dot(dmat, m["kb"], TN, HI) + _bdot(dmq, q, TN) + dkd * m["kdec_col"] + m["beta_col"] * dkb)
            dbeta_col = jnp.sum(dkb * k, axis=1, keepdims=True) + jnp.sum(dvb * v, axis=1, keepdims=True)
            e = da * m["a"] + daqk * m["aqk"]
            rs_kd = jnp.sum(dkd * m["kd"], axis=1, keepdims=True)
            e_colsum = jnp.sum(e, axis=0, keepdims=True)
            e_colsum_c = jnp.sum(jnp.where(eye, jnp.broadcast_to(e_colsum, (c, c)), 0.0), axis=1, keepdims=True)
            dgc = (jnp.sum(e, axis=1, keepdims=True) - e_colsum_c + jnp.sum(dqd * m["qd"], axis=1, keepdims=True)
                   - rs_kd + jnp.sum(dkbg * m["kbg"], axis=1, keepdims=True))
            last = jnp.sum(rs_kd, axis=0, keepdims=True) + dgamma * m["gamma"]
            dgc = dgc + jnp.where(lax.broadcasted_iota(jnp.int32, (c, 1), 0) == c - 1, last, 0.0)
            dgc_row = jnp.sum(jnp.where(eye, jnp.broadcast_to(dgc, (c, c)), 0.0), axis=0, keepdims=True)
            dg_col = jnp.sum(jnp.where(col >= row, jnp.broadcast_to(dgc_row, (c, c)), 0.0), axis=1, keepdims=True)
            dq_ref[:, hs] = dq
            dk_ref[:, hs] = dk
            dv_ref[:, hs] = m["beta_col"] * dvb
            dgb = dgb + jnp.where(lane == h, dg_col, 0.0) + jnp.where(lane == GDN_HEADS + h, dbeta_col, 0.0)
        dgb_ref[...] = dgb

    rev = lambda n: n_chunks - 1 - n
    blk = lambda j: pl.BlockSpec((CHUNK, BRANCH_W), lambda n: (rev(n), j))
    dq, dk, dv, dgb = pl.pallas_call(
        body, name=name, grid=(n_chunks,),
        in_specs=[blk(0), blk(1), blk(2), pl.BlockSpec((CHUNK, LANES), lambda n: (rev(n), 0)),
                  pl.BlockSpec((1, 2 * GDN_HEADS, CHUNK), lambda n: (rev(n), 0, 0)),
                  pl.BlockSpec((1, GDN_HEADS, GDN_DIM, GDN_DIM), lambda n: (rev(n), 0, 0, 0)), blk(0)],
        out_specs=[blk(0), blk(0), blk(0), pl.BlockSpec((CHUNK, LANES), lambda n: (rev(n), 0))],
        out_shape=[jax.ShapeDtypeStruct((s, BRANCH_W), F32)] * 3 + [jax.ShapeDtypeStruct((s, LANES), F32)],
        scratch_shapes=[pltpu.VMEM((GDN_HEADS, GDN_DIM, GDN_DIM), F32)],
        compiler_params=_params("arbitrary"),
    )(qkvn, qkvn, qkvn, gb, gbt, states, do)
    return jnp.concatenate([dq, dk, dv], axis=1), dgb


def _gdn_post_fwd(o, proj, norm_w, *, name):
    s = o.shape[0]

    def body(o_ref, g_ref, w_ref, y_ref):
        ov, gv = o_ref[...], g_ref[...]
        r = lax.rsqrt(jnp.mean(ov * ov, axis=-1, keepdims=True) + EPS)
        y_ref[...] = (ov * r * w_ref[...] * (gv * _sigmoid(gv))).astype(y_ref.dtype)

    return pl.pallas_call(
        body, name=name, grid=(GDN_HEADS,),
        in_specs=[_col_spec(s, 0), _col_spec(s, PB_GATE), pl.BlockSpec((1, LANES), lambda j: (0, 0))],
        out_specs=_col_spec(s, 0), out_shape=jax.ShapeDtypeStruct((s, BRANCH_W), BF16),
        compiler_params=_params("parallel"),
    )(o, proj, norm_w.reshape(1, GDN_DIM))


def _gdn_post_bwd(o, proj, norm_w, dy, *, name):
    s = o.shape[0]

    def body(o_ref, g_ref, w_ref, dy_ref, do_ref, dg_ref, dw_ref):
        ov, gv, w, dyv = o_ref[...], g_ref[...], w_ref[...], dy_ref[...].astype(F32)
        r = lax.rsqrt(jnp.mean(ov * ov, axis=-1, keepdims=True) + EPS)
        oh = ov * r
        sg = _sigmoid(gv)
        silu = gv * sg
        dn = dyv * silu
        dg_ref[...] = (dyv * (oh * w) * (sg * (1.0 + gv * (1.0 - sg)))).astype(dg_ref.dtype)

        @pl.when(pl.program_id(0) == 0)
        def _():
            dw_ref[...] = jnp.zeros_like(dw_ref)

        dw_ref[...] += jnp.sum(dn * oh, axis=0, keepdims=True)
        g2 = dn * w
        do_ref[...] = r * (g2 - oh * jnp.mean(g2 * oh, axis=-1, keepdims=True))

    do, dg, dw = pl.pallas_call(
        body, name=name, grid=(GDN_HEADS,),
        in_specs=[_col_spec(s, 0), _col_spec(s, PB_GATE), pl.BlockSpec((1, LANES), lambda j: (0, 0)), _col_spec(s, 0)],
        out_specs=[_col_spec(s, 0), _col_spec(s, 0), pl.BlockSpec((1, LANES), lambda j: (0, 0))],
        out_shape=[jax.ShapeDtypeStruct((s, BRANCH_W), F32), jax.ShapeDtypeStruct((s, BRANCH_W), BF16),
                   jax.ShapeDtypeStruct((1, LANES), F32)],
        compiler_params=_params("arbitrary"),
    )(o, proj, norm_w.reshape(1, GDN_DIM), dy)
    return do, dg, dw.reshape(LANES)


def _split_dot(x, u):
    hi = x.astype(BF16)
    lo = (x - hi.astype(F32)).astype(BF16)
    return _dot(hi, u) + _dot(lo, u)


def _sb_head_masks():
    lane = lax.broadcasted_iota(jnp.int32, (1, LANES), 1)
    return [(lane < SB_DIM).astype(F32), (lane >= SB_DIM).astype(F32)]


def _sb_fwd(proj, *, name):
    s = proj.shape[0]
    t = SB_TILE
    nq = s // t

    def body(q_ref, k_ref, v_ref, o_ref, tot_ref):
        r = lax.broadcasted_iota(jnp.int32, (t, t), 0)
        c = lax.broadcasted_iota(jnp.int32, (t, t), 1)
        u_strict = (r > c).astype(BF16)
        dmask = c < r
        hm = _sb_head_masks()

        def qloop(i, carry0):
            qs = pl.multiple_of(i * t, t)
            qf = q_ref[pl.ds(qs, t), :] * SB_SCALE
            qh = [(qf * hm[h]).astype(BF16) for h in range(2)]

            def block(j, st, masked):
                ks = pl.multiple_of(j * t, t)
                kb = k_ref[pl.ds(ks, t), :].astype(BF16)
                vf = v_ref[pl.ds(ks, t), :]
                acc, new = st[0], []
                for h in range(2):
                    z = _dot(qh[h], kb, NT)
                    sp = _softplus(z)
                    l1m = -sp
                    if masked:
                        l1m = jnp.where(dmask, l1m, 0.0)
                    att = jnp.exp(z - sp + _split_dot(l1m, u_strict) + st[1 + h])
                    if masked:
                        att = jnp.where(dmask, att, 0.0)
                    acc = acc + _bdot(att, vf * hm[h])
                    new.append(st[1 + h] + jnp.sum(l1m, axis=1, keepdims=True))
                return (acc, *new)

            st = (jnp.zeros((t, LANES), F32), jnp.zeros((t, 1), F32), jnp.zeros((t, 1), F32))
            st = block(i, st, True)
            st = lax.fori_loop(0, i, lambda jj, sv: block(i - 1 - jj, sv, False), st)
            o_ref[pl.ds(qs, t), :] = st[0]
            tot_ref[pl.ds(qs, t), :] = st[1] * hm[0] + st[2] * hm[1]
            return carry0

        lax.fori_loop(0, nq, qloop, 0)

    out = jax.ShapeDtypeStruct((s, BRANCH_W), F32)
    return pl.pallas_call(
        body, name=name, grid=(SB_HEADS // 2,),
        in_specs=[_col_spec(s, PB_SB), _col_spec(s, PB_SB + 4), _col_spec(s, PB_SB + 8)],
        out_specs=[_col_spec(s, 0)] * 2, out_shape=[out] * 2,
        compiler_params=_params("parallel"),
    )(proj, proj, proj)


def _sb_bwd(proj, tot, do, *, name):
    s = proj.shape[0]
    t = SB_TILE
    nq = s // t

    def body(q_ref, k_ref, v_ref, tot_ref, do_ref, dq_ref, dk_ref, dv_ref, dk_acc, dv_acc):
        dk_acc[...] = jnp.zeros_like(dk_acc)
        dv_acc[...] = jnp.zeros_like(dv_acc)
        r = lax.broadcasted_iota(jnp.int32, (t, t), 0)
        c = lax.broadcasted_iota(jnp.int32, (t, t), 1)
        u_le = (r <= c).astype(BF16)
        u_lt = (r < c).astype(BF16)
        dmask = c < r
        hm = _sb_head_masks()

        def qloop(i, carry0):
            qs = pl.multiple_of(i * t, t)
            qraw = q_ref[pl.ds(qs, t), :]
            dov = do_ref[pl.ds(qs, t), :].astype(F32)
            totv = tot_ref[pl.ds(qs, t), :]
            qh = [(qraw * (hm[h] * SB_SCALE)).astype(BF16) for h in range(2)]
            qr = [(qraw * hm[h]).astype(BF16) for h in range(2)]
            doh = [(dov * hm[h]).astype(BF16) for h in range(2)]
            tot = [jnp.min(totv * hm[h], axis=1, keepdims=True) for h in range(2)]

            def block(j, st, masked):
                ks = pl.multiple_of(j * t, t)
                kf = k_ref[pl.ds(ks, t), :]
                kb = kf.astype(BF16)
                vb = v_ref[pl.ds(ks, t), :].astype(BF16)
                dq_acc = st[0]
                dkc = jnp.zeros((t, LANES), F32)
                dvc = jnp.zeros((t, LANES), F32)
                new_c, new_r = [], []
                for h in range(2):
                    carry, rsum = st[1 + h], st[3 + h]
                    z = _dot(qh[h], kb, NT)
                    sp = _softplus(z)
                    l1m = -sp
                    if masked:
                        l1m = jnp.where(dmask, l1m, 0.0)
                    ls = z - sp
                    att = jnp.exp(ls + ((tot[h] - carry) - _split_dot(l1m, u_le)))
                    if masked:
                        att = jnp.where(dmask, att, 0.0)
                    p = att * _dot(doh[h], vb, NT)
                    dvc = dvc + _dot(att.astype(BF16), doh[h], TN)
                    dl1m = rsum + _split_dot(p, u_lt)
                    sig = jnp.exp(ls)
                    dz = p * (1.0 - sig) - dl1m * sig
                    if masked:
                        dz = jnp.where(dmask, dz, 0.0)
                    dzb = (dz * SB_SCALE).astype(BF16)
                    dq_acc = dq_acc + _dot(dzb, (kf * hm[h]).astype(BF16))
                    dkc = dkc + _dot(dzb, qr[h], TN)
                    new_c.append(carry + jnp.sum(l1m, axis=1, keepdims=True))
                    new_r.append(rsum + jnp.sum(p, axis=1, keepdims=True))
                dk_acc[pl.ds(ks, t), :] += dkc
                dv_acc[pl.ds(ks, t), :] += dvc
                return (dq_acc, *new_c, *new_r)

            zc = jnp.zeros((t, 1), F32)
            st = (jnp.zeros((t, LANES), F32), zc, zc, zc, zc)
            st = lax.fori_loop(0, i, lambda jj, sv: block(jj, sv, False), st)
            st = block(i, st, True)
            dq_ref[pl.ds(qs, t), :] = st[0].astype(dq_ref.dtype)
            return carry0

        lax.fori_loop(0, nq, qloop, 0)
        dk_ref[...] = dk_acc[...].astype(dk_ref.dtype)
        dv_ref[...] = dv_acc[...].astype(dv_ref.dtype)

    out = jax.ShapeDtypeStruct((s, BRANCH_W), BF16)
    return pl.pallas_call(
        body, name=name, grid=(SB_HEADS // 2,),
        in_specs=[_col_spec(s, PB_SB), _col_spec(s, PB_SB + 4), _col_spec(s, PB_SB + 8), _col_spec(s, 0), _col_spec(s, 0)],
        out_specs=[_col_spec(s, 0)] * 3, out_shape=[out] * 3,
        scratch_shapes=[pltpu.VMEM((s, LANES), F32), pltpu.VMEM((s, LANES), F32)],
        compiler_params=_params("parallel"),
    )(proj, proj, proj, tot, do)


def _sc_fwd(proj, conv_w, *, name):
    s = proj.shape[0]

    def body(x_ref, b_ref, c_ref, w_ref, y_ref):
        rows = lax.broadcasted_iota(jnp.int32, (s, LANES), 0)
        w = w_ref[...]
        u = c_ref[...] * x_ref[...]
        cv = w[2:3, :] * u + w[1:2, :] * _shift_down(u, 1, rows) + w[0:1, :] * _shift_down(u, 2, rows)
        y_ref[...] = (b_ref[...] * cv).astype(y_ref.dtype)

    return pl.pallas_call(
        body, name=name, grid=(BRANCH_W // LANES,),
        in_specs=[_col_spec(s, PB_SCX), _col_spec(s, PB_SCB), _col_spec(s, PB_SCC), pl.BlockSpec((3, LANES), lambda j: (0, j))],
        out_specs=_col_spec(s, 0), out_shape=jax.ShapeDtypeStruct((s, BRANCH_W), BF16),
        compiler_params=_params("parallel"),
    )(proj, proj, proj, conv_w)


def _sc_bwd(proj, conv_w, dy, *, name):
    s = proj.shape[0]

    def body(x_ref, b_ref, c_ref, w_ref, dy_ref, dx_ref, db_ref, dc_ref, dw_ref):
        rows = lax.broadcasted_iota(jnp.int32, (s, LANES), 0)
        w, x, cg, dyv = w_ref[...], x_ref[...], c_ref[...], dy_ref[...].astype(F32)
        u = cg * x
        u1, u2 = _shift_down(u, 1, rows), _shift_down(u, 2, rows)
        cv = w[2:3, :] * u + w[1:2, :] * u1 + w[0:1, :] * u2
        db_ref[...] = (dyv * cv).astype(db_ref.dtype)
        dcv = dyv * b_ref[...]
        du = w[2:3, :] * dcv + w[1:2, :] * _shift_up(dcv, 1, rows) + w[0:1, :] * _shift_up(dcv, 2, rows)
        dx_ref[...] = (du * cg).astype(dx_ref.dtype)
        dc_ref[...] = (du * x).astype(dc_ref.dtype)
        dw_ref[0:1, :] = jnp.sum(dcv * u2, axis=0, keepdims=True)
        dw_ref[1:2, :] = jnp.sum(dcv * u1, axis=0, keepdims=True)
        dw_ref[2:3, :] = jnp.sum(dcv * u, axis=0, keepdims=True)

    out = jax.ShapeDtypeStruct((s, BRANCH_W), BF16)
    wspec = pl.BlockSpec((3, LANES), lambda j: (0, j))
    return pl.pallas_call(
        body, name=name, grid=(BRANCH_W // LANES,),
        in_specs=[_col_spec(s, PB_SCX), _col_spec(s, PB_SCB), _col_spec(s, PB_SCC), wspec, _col_spec(s, 0)],
        out_specs=[_col_spec(s, 0)] * 3 + [wspec],
        out_shape=[out] * 3 + [jax.ShapeDtypeStruct((3, BRANCH_W), F32)],
        compiler_params=_params("parallel"),
    )(proj, proj, proj, conv_w, dy)


MERGE_TM, MERGE_TN = 512, 512


def _merge_specs():
    tm, tn = MERGE_TM, MERGE_TN
    y_spec = pl.BlockSpec((tm, BRANCH_W), lambda i, j: (i, 0))
    w_spec = pl.BlockSpec((N_BRANCH, BRANCH_W, tn), lambda i, j: (0, 0, j))
    gate_specs = [pl.BlockSpec((tm, tn), functools.partial(
        lambda i, j, b: (i, (PB_GATES * LANES + b * D_MODEL) // tn + j), b=b)) for b in range(N_BRANCH)]
    mn = pl.BlockSpec((tm, tn), lambda i, j: (i, j))
    return y_spec, w_spec, gate_specs, mn


def _merge_fwd(ya, yb, yc, wb, proj, *, name):
    s = ya.shape[0]
    y_spec, w_spec, gate_specs, mn = _merge_specs()

    def body(ya_ref, yb_ref, yc_ref, w_ref, g0, g1, g2, o_ref):
        acc = None
        for b, (y_ref, g_ref) in enumerate(zip((ya_ref, yb_ref, yc_ref), (g0, g1, g2))):
            term = _sigmoid(g_ref[...]) * _bdot(y_ref[...], w_ref[b])
            acc = term if acc is None else acc + term
        o_ref[...] = acc.astype(o_ref.dtype)

    return pl.pallas_call(
        body, name=name, grid=(s // MERGE_TM, D_MODEL // MERGE_TN),
        in_specs=[y_spec] * 3 + [w_spec] + gate_specs, out_specs=mn,
        out_shape=jax.ShapeDtypeStruct((s, D_MODEL), BF16), compiler_params=_params("parallel", "parallel"),
    )(ya, yb, yc, wb, proj, proj, proj)


def _merge_bwd(ya, yb, yc, wb, proj, dm, *, name):
    s = ya.shape[0]
    y_spec, w_spec, gate_specs, mn = _merge_specs()

    def body(ya_ref, yb_ref, yc_ref, w_ref, g0, g1, g2, dm_ref, *outs):
        dmv = dm_ref[...].astype(F32)
        for b, (y_ref, g_ref) in enumerate(zip((ya_ref, yb_ref, yc_ref), (g0, g1, g2))):
            sg = _sigmoid(g_ref[...])
            z = _bdot(y_ref[...], w_ref[b])
            outs[b][...] = (dmv * sg).astype(BF16)
            outs[N_BRANCH + b][...] = (dmv * z * sg * (1.0 - sg)).astype(BF16)

    out = jax.ShapeDtypeStruct((s, D_MODEL), BF16)
    res = pl.pallas_call(
        body, name=name, grid=(s // MERGE_TM, D_MODEL // MERGE_TN),
        in_specs=[y_spec] * 3 + [w_spec] + gate_specs + [mn], out_specs=[mn] * (2 * N_BRANCH),
        out_shape=[out] * (2 * N_BRANCH), compiler_params=_params("parallel", "parallel"),
    )(ya, yb, yc, wb, proj, proj, proj, dm)
    return res[:N_BRANCH], res[N_BRANCH:]


def _chunk_rows(v, s):
    return v[:, :2 * GDN_HEADS].reshape(s // CHUNK, CHUNK, 2 * GDN_HEADS).transpose(0, 2, 1)


def _relu2_epi(acc):
    r = jnp.maximum(acc, 0.0)
    return acc, r * r


def _drelu2_epi(acc, a):
    return (acc * (2.0 * jnp.maximum(a.astype(F32), 0.0)),)


def _layer_fwd(x0, p):
    s = x0.shape[0]
    h1 = _norm_fwd(x0, p["norm_mix_pre"], name="norm_mix_pre")
    proj = _matmul(h1, p["w_in"], name="proj_in", tm=512, tn=1664)
    qkvn = _gdn_pre_fwd(proj, p["conv_qkv_w"], name="gdn_pre")
    gb = _gdn_gates_fwd(proj, p["gdn_a_log"], p["gdn_dt_bias"], name="gdn_gates")
    gbt = _chunk_rows(gb, s)
    o_gdn, states = _gdn_chunk_fwd(qkvn, gb, gbt, name="gdn_chunk")
    ya = _gdn_post_fwd(o_gdn, proj, p["gdn_norm_w"], name="gdn_post")
    o_sb, sb_tot = _sb_fwd(proj, name="sb_attn")
    yc = _sc_fwd(proj, p["conv_sc_w"], name="short_conv")
    merged = _merge_fwd(ya, o_sb, yc, p["w_branch"], proj, name="merge")
    u = _matmul(merged, p["w_out"], name="proj_out", tm=512, tn=1024)
    x1 = _resnorm_fwd(x0, u, p["norm_mix_post"], name="norm_mix_post")
    h2 = _norm_fwd(x1, p["norm_ffn_pre"], name="norm_ffn_pre")
    a, r = _matmul(h2, p["w_ff1"], name="ff1", tm=512, tn=1024, outs=(BF16, BF16), epi=_relu2_epi)
    f = _matmul(r, p["w_ff2"], name="ff2", tm=512, tn=1024, tk=1024)
    x2 = _resnorm_fwd(x1, f, p["norm_ffn_post"], name="norm_ffn_post")
    saved = dict(x0=x0, h1=h1, proj=proj, qkvn=qkvn, gb=gb, gbt=gbt, o_gdn=o_gdn, states=states, ya=ya, o_sb=o_sb,
                 sb_tot=sb_tot, yc=yc, merged=merged, u=u, x1=x1, h2=h2, a=a, r=r, f=f)
    return x2, saved


def _layer_bwd(dx2, p, sv):
    g = {}
    df, g["norm_ffn_post"] = _norm_bwd(sv["f"], p["norm_ffn_post"], dx2, None, out_dtype=BF16, name="norm_ffn_post_bwd")
    g["w_ff2"] = _matmul(sv["r"], df, ta=True, name="ff2_dw", tm=1024, tn=1024, tk=512)
    da = _matmul(df, p["w_ff2"], tb=True, name="ff2_dx", tm=512, tn=1024, outs=(BF16,), epi=_drelu2_epi,
                 extras=(sv["a"],))
    g["w_ff1"] = _matmul(sv["h2"], da, ta=True, name="ff1_dw", tm=1024, tn=1024, tk=512)
    dh2 = _matmul(da, p["w_ff1"], tb=True, name="ff1_dx", tm=512, tn=1024, tk=1024)
    dx1, g["norm_ffn_pre"] = _norm_bwd(sv["x1"], p["norm_ffn_pre"], dh2, dx2, out_dtype=F32, name="norm_ffn_pre_bwd")
    du, g["norm_mix_post"] = _norm_bwd(sv["u"], p["norm_mix_post"], dx1, None, out_dtype=BF16, name="norm_mix_post_bwd")
    g["w_out"] = _matmul(sv["merged"], du, ta=True, name="out_dw", tm=1024, tn=1024, tk=512)
    dmerged = _matmul(du, p["w_out"], tb=True, name="out_dx", tm=512, tn=1024, outs=(BF16,))
    ys = (sv["ya"], sv["o_sb"], sv["yc"])
    dz, dgates = _merge_bwd(*ys, p["w_branch"], sv["proj"], dmerged, name="merge_bwd")
    g["w_branch"] = jnp.stack([_matmul(ys[b], dz[b], ta=True, name=f"branch_dw{b}", tm=512, tn=1024, tk=512)
                               for b in range(N_BRANCH)])
    dys = [_matmul(dz[b], p["w_branch"][b], tb=True, name=f"branch_dx{b}", tm=512, tn=512) for b in range(N_BRANCH)]
    dscx, dscb, dscc, g["conv_sc_w"] = _sc_bwd(sv["proj"], p["conv_sc_w"], dys[2], name="short_conv_bwd")
    dsq, dsk, dsv = _sb_bwd(sv["proj"], sv["sb_tot"], dys[1], name="sb_attn_bwd")
    do_gdn, dgate, dnw = _gdn_post_bwd(sv["o_gdn"], sv["proj"], p["gdn_norm_w"], dys[0], name="gdn_post_bwd")
    g["gdn_norm_w"] = dnw
    dqkvn, dgb = _gdn_chunk_bwd(sv["qkvn"], sv["gb"], sv["gbt"], sv["states"], do_gdn, name="gdn_chunk_bwd")
    dqkv, g["conv_qkv_w"] = _gdn_pre_bwd(sv["proj"], p["conv_qkv_w"], dqkvn, name="gdn_pre_bwd")
    dab, g["gdn_a_log"], g["gdn_dt_bias"] = _gdn_gates_bwd(sv["proj"], p["gdn_a_log"], p["gdn_dt_bias"], dgb,
                                                           name="gdn_gates_bwd")
    dproj = jnp.concatenate([*dgates, dqkv, dgate, dab, dsq, dsk, dsv, dscx, dscb, dscc], axis=1)
    g["w_in"] = _matmul(sv["h1"], dproj, ta=True, name="in_dw", tm=1024, tn=1664, tk=512)
    dh1 = _matmul(dproj, p["w_in"], tb=True, name="in_dx", tm=512, tn=1024, tk=1664)
    dx0, g["norm_mix_pre"] = _norm_bwd(sv["x0"], p["norm_mix_pre"], dh1, dx1, out_dtype=F32, name="norm_mix_pre_bwd")
    return dx0, g


def _local_step(x, target, layers):
    saved = []
    h = x
    for p in layers:
        h, sv = _layer_fwd(h, p)
        saved.append(sv)
    loss, dh = _loss_fwd_bwd(h, target, name="loss")
    grads = [None] * len(layers)
    for l in reversed(range(len(layers))):
        dh, grads[l] = _layer_bwd(dh, layers[l], saved[l])
    return loss, dh, grads


ANY = pl.BlockSpec(memory_space=pl.ANY)


def _me_and_chips():
    x, y, c = lax.axis_index("x"), lax.axis_index("y"), lax.axis_index("c")
    chips = [(1 - x, y), (x, 1 - y), (1 - x, 1 - y)]
    return x, y, c, chips


def _gather_chips(arrs, *, name):
    n = len(arrs)

    def body(*refs):
        ins, outs = refs[:n], refs[n:2 * n]
        send_sems, recv_sems, local_sems = refs[2 * n:]
        x, y, c, chips = _me_and_chips()
        me = 2 * x + y
        sends, locals_ = [], []
        for a in range(n):
            lc = pltpu.make_async_copy(ins[a], outs[a].at[me], local_sems.at[a])
            lc.start()
            locals_.append(lc)
            for k, (px, py) in enumerate(chips):
                cp = pltpu.make_async_remote_copy(src_ref=ins[a], dst_ref=outs[a].at[me], send_sem=send_sems.at[a, k],
                                                  recv_sem=recv_sems.at[a, k], device_id=(px, py, c), device_id_type=MESH)
                cp.start()
                sends.append(cp)
        for a in range(n):
            for k, (px, py) in enumerate(chips):
                pltpu.make_async_remote_copy(src_ref=ins[a], dst_ref=outs[a].at[2 * px + py], send_sem=send_sems.at[a, k],
                                             recv_sem=recv_sems.at[a, k], device_id=(px, py, c),
                                             device_id_type=MESH).wait_recv()
        for cp in sends:
            cp.wait_send()
        for lc in locals_:
            lc.wait()

    return pl.pallas_call(
        body, name=name, in_specs=[ANY] * n, out_specs=[ANY] * n,
        out_shape=[jax.ShapeDtypeStruct((N_CHIPS,) + a.shape, a.dtype) for a in arrs],
        scratch_shapes=[pltpu.SemaphoreType.DMA((n, 3)), pltpu.SemaphoreType.DMA((n, 3)), pltpu.SemaphoreType.DMA((n,))],
    )(*arrs)


def _scatter_partials(arrs, small, *, name):
    n = len(arrs)

    def body(*refs):
        ins, small_ref = refs[:n], refs[n]
        outs, small_out = refs[n + 1:2 * n + 1], refs[2 * n + 1]
        send_sems, recv_sems, local_sems, ssend, srecv = refs[2 * n + 2:]
        x, y, c, chips = _me_and_chips()
        me = 2 * x + y
        sends, locals_ = [], []
        for a in range(n):
            lc = pltpu.make_async_copy(ins[a].at[me], outs[a].at[me], local_sems.at[a])
            lc.start()
            locals_.append(lc)
            for k, (px, py) in enumerate(chips):
                cp = pltpu.make_async_remote_copy(src_ref=ins[a].at[2 * px + py], dst_ref=outs[a].at[me],
                                                  send_sem=send_sems.at[a, k], recv_sem=recv_sems.at[a, k],
                                                  device_id=(px, py, c), device_id_type=MESH)
                cp.start()
                sends.append(cp)
        dev = 4 * x + 2 * y + c
        lc = pltpu.make_async_copy(small_ref, small_out.at[dev], local_sems.at[n])
        lc.start()
        locals_.append(lc)
        peers = [(x, y, 1 - c)] + [(px, py, pc) for (px, py) in chips for pc in (c, 1 - c)]
        for k, peer in enumerate(peers):
            cp = pltpu.make_async_remote_copy(src_ref=small_ref, dst_ref=small_out.at[dev], send_sem=ssend.at[k],
                                              recv_sem=srecv.at[k], device_id=peer, device_id_type=MESH)
            cp.start()
            sends.append(cp)
        for a in range(n):
            for k, (px, py) in enumerate(chips):
                pltpu.make_async_remote_copy(src_ref=ins[a].at[me], dst_ref=outs[a].at[2 * px + py],
                                             send_sem=send_sems.at[a, k], recv_sem=recv_sems.at[a, k],
                                             device_id=(px, py, c), device_id_type=MESH).wait_recv()
        for k, (px, py, pc) in enumerate(peers):
            pltpu.make_async_remote_copy(src_ref=small_ref, dst_ref=small_out.at[4 * px + 2 * py + pc], send_sem=ssend.at[k],
                                         recv_sem=srecv.at[k], device_id=(px, py, pc), device_id_type=MESH).wait_recv()
        for cp in sends:
            cp.wait_send()
        for lc in locals_:
            lc.wait()

    res = pl.pallas_call(
        body, name=name, in_specs=[ANY] * (n + 1), out_specs=[ANY] * (n + 1),
        out_shape=[jax.ShapeDtypeStruct(a.shape, a.dtype) for a in arrs]
        + [jax.ShapeDtypeStruct((N_DEV,) + small.shape, small.dtype)],
        scratch_shapes=[pltpu.SemaphoreType.DMA((n, 3)), pltpu.SemaphoreType.DMA((n, 3)), pltpu.SemaphoreType.DMA((n + 1,)),
                        pltpu.SemaphoreType.DMA((N_DEV - 1,)), pltpu.SemaphoreType.DMA((N_DEV - 1,))],
    )(*arrs, small)
    return res[:n], res[n]


def _swap_sibling(arrs, *, name):
    n = len(arrs)

    def body(*refs):
        ins, outs = refs[:n], refs[n:2 * n]
        send_sems, recv_sems = refs[2 * n:]
        x, y, c = lax.axis_index("x"), lax.axis_index("y"), lax.axis_index("c")
        cps = [pltpu.make_async_remote_copy(src_ref=ins[a], dst_ref=outs[a], send_sem=send_sems.at[a],
                                            recv_sem=recv_sems.at[a], device_id=(x, y, 1 - c), device_id_type=MESH)
               for a in range(n)]
        for cp in cps:
            cp.start()
        for cp in cps:
            cp.wait()

    return pl.pallas_call(
        body, name=name, in_specs=[ANY] * n, out_specs=[ANY] * n,
        out_shape=[jax.ShapeDtypeStruct(a.shape, a.dtype) for a in arrs],
        scratch_shapes=[pltpu.SemaphoreType.DMA((n,)), pltpu.SemaphoreType.DMA((n,))],
    )(*arrs)


def _row_tile(rows, cols, budget=2 * 1024 * 1024):
    best = None
    for t in range(16, rows + 1, 16):
        if rows % t == 0 and t * cols * 4 <= budget:
            best = t
    return best if best is not None else rows


def _sum_slots(parts, *, name):
    n, rows, cols = parts.shape
    tr = _row_tile(rows, cols, 1024 * 1024)

    def body(p_ref, o_ref):
        acc = p_ref[0].astype(F32)
        for i in range(1, n):
            acc = acc + p_ref[i].astype(F32)
        o_ref[...] = acc

    return pl.pallas_call(
        body, name=name, grid=(rows // tr,), in_specs=[pl.BlockSpec((n, tr, cols), lambda i: (0, i, 0))],
        out_specs=pl.BlockSpec((tr, cols), lambda i: (i, 0)), out_shape=jax.ShapeDtypeStruct((rows, cols), F32),
        compiler_params=_params("parallel"),
    )(parts)


def _adamw(w, m, v, g_a, g_b, *, name):
    rows, cols = w.shape
    tr = _row_tile(rows, cols, 1024 * 1024)
    two = g_b is not None
    c1 = 1.0 / (1.0 - ADAM_B1 ** ADAM_STEP)
    c2 = 1.0 / (1.0 - ADAM_B2 ** ADAM_STEP)

    def body(*refs):
        w_ref, m_ref, v_ref, ga_ref = refs[:4]
        g_ref, d_ref, nm_ref, nv_ref = refs[4 + two:]
        g = ga_ref[...]
        if two:
            g = g + refs[4][...]
        nm = ADAM_B1 * m_ref[...] + (1.0 - ADAM_B1) * g
        nv = ADAM_B2 * v_ref[...] + (1.0 - ADAM_B2) * (g * g)
        g_ref[...] = g
        nm_ref[...] = nm
        nv_ref[...] = nv
        d_ref[...] = -ADAM_LR * ((nm * c1) / (jnp.sqrt(nv * c2) + ADAM_EPS) + ADAM_WD * w_ref[...])

    blk = pl.BlockSpec((tr, cols), lambda i: (i, 0))
    ins = [w, m, v, g_a] + ([g_b] if two else [])
    return pl.pallas_call(
        body, name=name, grid=(rows // tr,), in_specs=[blk] * len(ins), out_specs=[blk] * 4,
        out_shape=[jax.ShapeDtypeStruct((rows, cols), F32)] * 4, compiler_params=_params("parallel"),
    )(*ins)


def _cast_bf16(w, *, name):
    rows, cols = w.shape
    tr = _row_tile(rows, cols)

    def body(w_ref, o_ref):
        o_ref[...] = w_ref[...].astype(BF16)

    blk = pl.BlockSpec((tr, cols), lambda i: (i, 0))
    return pl.pallas_call(body, name=name, grid=(rows // tr,), in_specs=[blk], out_specs=blk,
                          out_shape=jax.ShapeDtypeStruct((rows, cols), BF16), compiler_params=_params("parallel"))(w)


BIG = ("w_in", "w_branch", "w_out", "w_ff1", "w_ff2")
SMALL = ("norm_mix_pre", "conv_qkv_w", "gdn_a_log", "gdn_dt_bias", "gdn_norm_w", "conv_sc_w", "norm_mix_post",
         "norm_ffn_pre", "norm_ffn_post")
ORDER = ("norm_mix_pre", "w_in", "conv_qkv_w", "gdn_a_log", "gdn_dt_bias", "gdn_norm_w", "conv_sc_w", "w_branch",
         "w_out", "norm_mix_post", "norm_ffn_pre", "w_ff1", "w_ff2", "norm_ffn_post")


def _full_weights(gathered, rep):
    layers = []
    for l in range(DEPTH):
        p = dict(
            w_in=_pad_in_cols(gathered["w_in"][:, l].transpose(1, 0, 2).reshape(D_MODEL, IN_W)),
            w_branch=gathered["w_branch"][:, l].transpose(1, 2, 0, 3).reshape(N_BRANCH, BRANCH_W, D_MODEL),
            w_out=gathered["w_out"][:, l].reshape(D_MODEL, D_MODEL),
            w_ff1=gathered["w_ff1"][:, l].transpose(1, 0, 2).reshape(D_MODEL, D_FF),
            w_ff2=gathered["w_ff2"][:, l].reshape(D_FF, D_MODEL),
            conv_qkv_w=gathered["conv_qkv_w"][:, l].transpose(1, 0, 2).reshape(4, 3 * BRANCH_W),
            conv_sc_w=gathered["conv_sc_w"][:, l].transpose(1, 0, 2).reshape(3, BRANCH_W),
        )
        for k in ("norm_mix_pre", "gdn_a_log", "gdn_dt_bias", "gdn_norm_w", "norm_mix_post", "norm_ffn_pre", "norm_ffn_post"):
            p[k] = rep[k][l]
        layers.append(p)
    return layers


def _partials_by_chip(grads):
    def stack(name, fn):
        return jnp.stack([fn(g[name]) for g in grads], axis=1).astype(BF16)

    def w_in(gw):
        return _unpad_in_cols(gw).reshape(D_MODEL, N_CHIPS, IN_W // N_CHIPS).transpose(1, 0, 2)

    return dict(
        w_in=stack("w_in", w_in),
        w_branch=stack("w_branch", lambda gw: gw.reshape(N_BRANCH, BRANCH_W, N_CHIPS, D_MODEL // N_CHIPS).transpose(2, 0, 1, 3)),
        w_out=stack("w_out", lambda gw: gw.reshape(N_CHIPS, D_MODEL // N_CHIPS, D_MODEL)),
        w_ff1=stack("w_ff1", lambda gw: gw.reshape(D_MODEL, N_CHIPS, D_FF // N_CHIPS).transpose(1, 0, 2)),
        w_ff2=stack("w_ff2", lambda gw: gw.reshape(N_CHIPS, D_FF // N_CHIPS, D_MODEL)),
    )


def _pack_small(grads):
    pieces, layout = [], []
    for name in SMALL:
        v = jnp.stack([g[name] for g in grads]).astype(F32)
        layout.append((name, v.shape))
        pieces.append(v.reshape(-1))
    flat = jnp.concatenate(pieces)
    rows = -(-flat.shape[0] // LANES)
    rows = -(-rows // 8) * 8
    flat = jnp.pad(flat, (0, rows * LANES - flat.shape[0]))
    return flat.reshape(rows, LANES), layout


def _unpack_small(table, layout):
    flat, out, off = table.reshape(-1), {}, 0
    for name, shape in layout:
        size = 1
        for d in shape:
            size *= d
        out[name] = flat[off:off + size].reshape(shape)
        off += size
    return out


def _as2d(a):
    return a.reshape(-1, a.shape[-1]) if a.ndim > 1 else a.reshape(1, -1)


def kernel(x, norm_mix_pre, w_in, conv_qkv_w, gdn_a_log, gdn_dt_bias, gdn_norm_w, conv_sc_w, w_branch, w_out, norm_mix_post, norm_ffn_pre, w_ff1, w_ff2, norm_ffn_post, loss_target, m_norm_mix_pre, m_w_in, m_conv_qkv_w, m_gdn_a_log, m_gdn_dt_bias, m_gdn_norm_w, m_conv_sc_w, m_w_branch, m_w_out, m_norm_mix_post, m_norm_ffn_pre, m_w_ff1, m_w_ff2, m_norm_ffn_post, v_norm_mix_pre, v_w_in, v_conv_qkv_w, v_gdn_a_log, v_gdn_dt_bias, v_gdn_norm_w, v_conv_sc_w, v_w_branch, v_w_out, v_norm_mix_post, v_norm_ffn_pre, v_w_ff1, v_w_ff2, v_norm_ffn_post):
    w = dict(norm_mix_pre=norm_mix_pre, w_in=w_in, conv_qkv_w=conv_qkv_w, gdn_a_log=gdn_a_log, gdn_dt_bias=gdn_dt_bias,
             gdn_norm_w=gdn_norm_w, conv_sc_w=conv_sc_w, w_branch=w_branch, w_out=w_out, norm_mix_post=norm_mix_post,
             norm_ffn_pre=norm_ffn_pre, w_ff1=w_ff1, w_ff2=w_ff2, norm_ffn_post=norm_ffn_post)
    m = dict(norm_mix_pre=m_norm_mix_pre, w_in=m_w_in, conv_qkv_w=m_conv_qkv_w, gdn_a_log=m_gdn_a_log,
             gdn_dt_bias=m_gdn_dt_bias, gdn_norm_w=m_gdn_norm_w, conv_sc_w=m_conv_sc_w, w_branch=m_w_branch, w_out=m_w_out,
             norm_mix_post=m_norm_mix_post, norm_ffn_pre=m_norm_ffn_pre, w_ff1=m_w_ff1, w_ff2=m_w_ff2,
             norm_ffn_post=m_norm_ffn_post)
    v = dict(norm_mix_pre=v_norm_mix_pre, w_in=v_w_in, conv_qkv_w=v_conv_qkv_w, gdn_a_log=v_gdn_a_log,
             gdn_dt_bias=v_gdn_dt_bias, gdn_norm_w=v_gdn_norm_w, conv_sc_w=v_conv_sc_w, w_branch=v_w_branch, w_out=v_w_out,
             norm_mix_post=v_norm_mix_post, norm_ffn_pre=v_norm_ffn_pre, w_ff1=v_w_ff1, w_ff2=v_w_ff2,
             norm_ffn_post=v_norm_ffn_post)

    names = BIG + ("conv_qkv_w", "conv_sc_w")
    shards = [_cast_bf16(_as2d(w[k]), name=f"cast_{k}").reshape(w[k].shape) for k in BIG] + [conv_qkv_w, conv_sc_w]
    gathered = dict(zip(names, _gather_chips(shards, name="gather_weights")))
    layers = _full_weights(gathered, w)

    loss, dx, grads = _local_step(x[0], loss_target[0], layers)
    loss = lax.psum(loss, ("x", "y", "c"))

    parts = _partials_by_chip(grads)
    small, layout = _pack_small(grads)
    recv, small_all = _scatter_partials([parts[k] for k in BIG], small, name="scatter_grads")
    mine = [_sum_slots(r.reshape(N_CHIPS, -1, r.shape[-1]), name=f"sum_{k}") for k, r in zip(BIG, recv)]
    theirs = _swap_sibling(mine, name="swap_sibling")
    small_g = _unpack_small(_sum_slots(small_all, name="sum_small"), layout)
    xy = 2 * lax.axis_index("x") + lax.axis_index("y")
    for k, width in (("conv_qkv_w", 3 * BRANCH_W // N_CHIPS), ("conv_sc_w", BRANCH_W // N_CHIPS)):
        small_g[k] = lax.dynamic_slice_in_dim(small_g[k], xy * width, width, axis=2)

    out = {}
    for k, s_mine, s_theirs in zip(BIG, mine, theirs):
        res = _adamw(_as2d(w[k]), _as2d(m[k]), _as2d(v[k]), s_mine, s_theirs, name=f"adamw_{k}")
        out[k] = [r.reshape(w[k].shape) for r in res]
    for k in SMALL:
        res = _adamw(_as2d(w[k]), _as2d(m[k]), _as2d(v[k]), _as2d(small_g[k]), None, name=f"adamw_{k}")
        out[k] = [r.reshape(w[k].shape) for r in res]
    return (loss, dx[None], *[out[k][0] for k in ORDER], *[out[k][1] for k in ORDER], *[out[k][2] for k in ORDER],
            *[out[k][3] for k in ORDER])
```

```python
import functools

import jax
import jax.numpy as jnp
from jax import lax
from jax.experimental import pallas as pl
from jax.experimental.pallas import tpu as pltpu

F32 = jnp.float32
BF16 = jnp.bfloat16
MESH = pl.DeviceIdType.MESH

LANES = 128
D_MODEL = 1024
DEPTH = 4
CHUNK = 64
GDN_HEADS, GDN_DIM = 4, 128
SB_HEADS, SB_DIM = 8, 64
BRANCH_W = 512
N_BRANCH = 3
D_FF = 4 * D_MODEL
EPS = 1e-6
IN_W = 8200
AB_COL = 2048
AB_PAD = LANES - 8
IN_WP = IN_W + AB_PAD
N_CHIPS = 4
N_DEV = 8
GATES_COL = 5128
PB_GATES, PB_QKV, PB_GATE, PB_AB, PB_SB, PB_SCX, PB_SCB, PB_SCC = 0, 24, 36, 40, 41, 53, 57, 61
SB_TILE = 128
SB_GROUP = 4
SB_SCALE = SB_DIM ** -0.5
GDN_QSCALE = GDN_DIM ** -0.5
VMEM_LIMIT = 56 * 1024 * 1024

ADAM_LR, ADAM_B1, ADAM_B2, ADAM_EPS, ADAM_WD, ADAM_STEP = 0.001, 0.9, 0.999, 1e-08, 0.01, 10

NT = (((1,), (1,)), ((), ()))
TN = (((0,), (0,)), ((), ()))
HI = lax.Precision.HIGHEST


def _pad_in_cols(w):
    return jnp.concatenate([w[:, GATES_COL:], w[:, :AB_COL + 8], jnp.zeros((w.shape[0], AB_PAD), w.dtype),
                            w[:, AB_COL + 8:GATES_COL]], axis=1)


def _unpad_in_cols(g):
    n_gates = IN_W - GATES_COL
    return jnp.concatenate([g[:, n_gates:n_gates + AB_COL + 8], g[:, n_gates + AB_COL + 8 + AB_PAD:], g[:, :n_gates]], axis=1)


def _params(*sem):
    return pltpu.CompilerParams(dimension_semantics=sem if sem else None, vmem_limit_bytes=VMEM_LIMIT)


def _sigmoid(x):
    return 1.0 / (1.0 + jnp.exp(-x))


def _softplus(x):
    return jnp.maximum(x, 0.0) + jnp.log(1.0 + jnp.exp(-jnp.abs(x)))


def _dot(a, b, dims=None, precision=None):
    if dims is None:
        return jnp.dot(a, b, preferred_element_type=F32, precision=precision)
    return lax.dot_general(a, b, dims, preferred_element_type=F32, precision=precision)


def _bdot(a, b, dims=None):
    return _dot(a.astype(BF16), b.astype(BF16), dims)


def _matmul(a, b, *, name, ta=False, tb=False, tm, tn, tk=None, outs=(F32,), epi=None, extras=()):
    if ta:
        kdim, m = a.shape
    else:
        m, kdim = a.shape
    if tb:
        n, kb = b.shape
    else:
        kb, n = b.shape
    assert kdim == kb, (a.shape, b.shape)
    tk = kdim if tk is None else tk
    assert m % tm == 0 and n % tn == 0 and kdim % tk == 0, (m, n, kdim, tm, tn, tk)
    nk = kdim // tk
    a_spec = pl.BlockSpec((tk, tm), lambda i, j, k: (k, i)) if ta else pl.BlockSpec((tm, tk), lambda i, j, k: (i, k))
    b_spec = pl.BlockSpec((tn, tk), lambda i, j, k: (j, k)) if tb else pl.BlockSpec((tk, tn), lambda i, j, k: (k, j))
    mn_spec = pl.BlockSpec((tm, tn), lambda i, j, k: (i, j))
    dims = (((0 if ta else 1,), (1 if tb else 0,)), ((), ()))
    n_ex, n_out = len(extras), len(outs)

    def body(a_ref, b_ref, *rest):
        ex, o, acc = rest[:n_ex], rest[n_ex:n_ex + n_out], rest[n_ex + n_out:]
        part = lax.dot_general(a_ref[...].astype(BF16), b_ref[...].astype(BF16), dims, preferred_element_type=F32)

        def finish(val):
            res = epi(val, *[e[...] for e in ex]) if epi is not None else (val,)
            for r, oref in zip(res, o):
                oref[...] = r.astype(oref.dtype)

        if nk == 1:
            finish(part)
        else:
            k = pl.program_id(2)

            @pl.when(k == 0)
            def _():
                acc[0][...] = part

            @pl.when(k > 0)
            def _():
                acc[0][...] += part

            @pl.when(k == nk - 1)
            def _():
                finish(acc[0][...])

    res = pl.pallas_call(
        body, name=name, grid=(m // tm, n // tn, nk),
        in_specs=[a_spec, b_spec] + [mn_spec] * n_ex,
        out_specs=[mn_spec] * n_out,
        out_shape=[jax.ShapeDtypeStruct((m, n), dt) for dt in outs],
        scratch_shapes=[pltpu.VMEM((tm, tn), F32)] if nk > 1 else [],
        compiler_params=_params("parallel", "parallel", "arbitrary"),
    )(a, b, *extras)
    return res[0] if n_out == 1 else res


ROW_TILE = 512


def _norm_fwd(x, w, *, name):
    s, d = x.shape

    def body(x_ref, w_ref, o_ref):
        xv = x_ref[...]
        r = lax.rsqrt(jnp.mean(xv * xv, axis=-1, keepdims=True) + EPS)
        o_ref[...] = (xv * r * w_ref[...]).astype(o_ref.dtype)

    return pl.pallas_call(
        body, name=name, grid=(s // ROW_TILE,),
        in_specs=[pl.BlockSpec((ROW_TILE, d), lambda i: (i, 0)), pl.BlockSpec((1, d), lambda i: (0, 0))],
        out_specs=pl.BlockSpec((ROW_TILE, d), lambda i: (i, 0)),
        out_shape=jax.ShapeDtypeStruct((s, d), BF16), compiler_params=_params("parallel"),
    )(x, w.reshape(1, d))


def _resnorm_fwd(x, u, w, *, name):
    s, d = x.shape

    def body(x_ref, u_ref, w_ref, o_ref):
        uv = u_ref[...]
        r = lax.rsqrt(jnp.mean(uv * uv, axis=-1, keepdims=True) + EPS)
        o_ref[...] = x_ref[...] + uv * r * w_ref[...]

    row = pl.BlockSpec((ROW_TILE, d), lambda i: (i, 0))
    return pl.pallas_call(
        body, name=name, grid=(s // ROW_TILE,),
        in_specs=[row, row, pl.BlockSpec((1, d), lambda i: (0, 0))], out_specs=row,
        out_shape=jax.ShapeDtypeStruct((s, d), F32), compiler_params=_params("parallel"),
    )(x, u, w.reshape(1, d))


def _norm_bwd(xin, w, dy, res, *, out_dtype, name):
    s, d = xin.shape
    has_res = res is not None

    def body(*refs):
        x_ref, w_ref, dy_ref = refs[:3]
        res_ref = refs[3] if has_res else None
        dx_ref, dw_ref = refs[3 + has_res:]
        xv, dyv = x_ref[...], dy_ref[...].astype(F32)
        r = lax.rsqrt(jnp.mean(xv * xv, axis=-1, keepdims=True) + EPS)
        xh = xv * r
        g = dyv * w_ref[...]
        dx = r * (g - xh * jnp.mean(g * xh, axis=-1, keepdims=True))
        if has_res:
            dx = dx + res_ref[...]
        dx_ref[...] = dx.astype(dx_ref.dtype)

        @pl.when(pl.program_id(0) == 0)
        def _():
            dw_ref[...] = jnp.zeros_like(dw_ref)

        dw_ref[...] += jnp.sum(dyv * xh, axis=0, keepdims=True)

    row = pl.BlockSpec((ROW_TILE, d), lambda i: (i, 0))
    vec = pl.BlockSpec((1, d), lambda i: (0, 0))
    ins = [xin, w.reshape(1, d), dy] + ([res] if has_res else [])
    dx, dw = pl.pallas_call(
        body, name=name, grid=(s // ROW_TILE,),
        in_specs=[row, vec, row] + ([row] if has_res else []), out_specs=[row, vec],
        out_shape=[jax.ShapeDtypeStruct((s, d), out_dtype), jax.ShapeDtypeStruct((1, d), F32)],
        compiler_params=_params("arbitrary"),
    )(*ins)
    return dx, dw.reshape(d)


def _loss_fwd_bwd(y, target, *, name):
    s, d = y.shape

    def body(y_ref, t_ref, loss_ref, dy_ref):
        e = y_ref[...] - t_ref[...]
        dy_ref[...] = e * (1.0 / d)

        @pl.when(pl.program_id(0) == 0)
        def _():
            loss_ref[...] = jnp.zeros_like(loss_ref)

        part = jnp.sum(jnp.sum(e * e, axis=1, keepdims=True), axis=0, keepdims=True)
        loss_ref[...] += part * (0.5 / d)

    row = pl.BlockSpec((ROW_TILE, d), lambda i: (i, 0))
    loss, dy = pl.pallas_call(
        body, name=name, grid=(s // ROW_TILE,), in_specs=[row, row],
        out_specs=[pl.BlockSpec((1, 1), lambda i: (0, 0)), row],
        out_shape=[jax.ShapeDtypeStruct((1, 1), F32), jax.ShapeDtypeStruct((s, d), F32)],
        compiler_params=_params("arbitrary"),
    )(y, target)
    return loss[0, 0], dy


def _shift_down(x, k, rows):
    if k == 0:
        return x
    return jnp.where(rows >= k, pltpu.roll(x, k, 0), 0.0)


def _shift_up(x, k, rows):
    if k == 0:
        return x
    n = x.shape[0]
    return jnp.where(rows < n - k, pltpu.roll(x, n - k, 0), 0.0)


def _col_spec(s, base):
    return pl.BlockSpec((s, LANES), lambda j: (0, base + j))


def _gdn_pre_math(x, w, j, rows):
    taps = w.shape[0]
    c = w[taps - 1:taps, :] * x
    for i in range(taps - 1):
        c = c + w[i:i + 1, :] * _shift_down(x, taps - 1 - i, rows)
    sg = _sigmoid(c)
    y = c * sg
    r = lax.rsqrt(jnp.sum(y * y, axis=-1, keepdims=True) + EPS)
    is_qk = j < 2 * GDN_HEADS
    scale = jnp.where(j < GDN_HEADS, GDN_QSCALE, 1.0)
    return c, sg, y, r, is_qk, scale


def _gdn_pre_fwd(proj, conv_w, *, name):
    s = proj.shape[0]

    def body(x_ref, w_ref, o_ref):
        j = pl.program_id(0)
        rows = lax.broadcasted_iota(jnp.int32, (s, LANES), 0)
        _, _, y, r, is_qk, scale = _gdn_pre_math(x_ref[...], w_ref[...], j, rows)
        o_ref[...] = jnp.where(is_qk, y * (r * scale), y)

    return pl.pallas_call(
        body, name=name, grid=(12,),
        in_specs=[_col_spec(s, PB_QKV), pl.BlockSpec((4, LANES), lambda j: (0, j))],
        out_specs=_col_spec(s, 0), out_shape=jax.ShapeDtypeStruct((s, 3 * BRANCH_W), F32),
        compiler_params=_params("parallel"),
    )(proj, conv_w)


def _gdn_pre_bwd(proj, conv_w, dqkvn, *, name):
    s = proj.shape[0]

    def body(x_ref, w_ref, d_ref, dx_ref, dw_ref):
        j = pl.program_id(0)
        rows = lax.broadcasted_iota(jnp.int32, (s, LANES), 0)
        x, w, dout = x_ref[...], w_ref[...], d_ref[...]
        c, sg, y, r, is_qk, scale = _gdn_pre_math(x, w, j, rows)
        yh = y * r
        dy_n = (scale * r) * (dout - yh * jnp.sum(dout * yh, axis=-1, keepdims=True))
        dy = jnp.where(is_qk, dy_n, dout)
        dc = dy * (sg * (1.0 + c * (1.0 - sg)))
        taps = w.shape[0]
        dx = w[taps - 1:taps, :] * dc
        dws = []
        for i in range(taps - 1):
            k = taps - 1 - i
            dx = dx + w[i:i + 1, :] * _shift_up(dc, k, rows)
            dws.append(jnp.sum(dc * _shift_down(x, k, rows), axis=0, keepdims=True))
        dws.append(jnp.sum(dc * x, axis=0, keepdims=True))
        dx_ref[...] = dx.astype(dx_ref.dtype)
        for i in range(taps):
            dw_ref[i:i + 1, :] = dws[i]

    return pl.pallas_call(
        body, name=name, grid=(12,),
        in_specs=[_col_spec(s, PB_QKV), pl.BlockSpec((4, LANES), lambda j: (0, j)), _col_spec(s, 0)],
        out_specs=[_col_spec(s, 0), pl.BlockSpec((4, LANES), lambda j: (0, j))],
        out_shape=[jax.ShapeDtypeStruct((s, 3 * BRANCH_W), BF16), jax.ShapeDtypeStruct((4, 3 * BRANCH_W), F32)],
        compiler_params=_params("parallel"),
    )(proj, conv_w, dqkvn)


def _lane_pad(v):
    return jnp.pad(v.reshape(1, -1), ((0, 0), (0, LANES - v.shape[0])))


def _gdn_gates_fwd(proj, a_log, dt_bias, *, name):
    s = proj.shape[0]

    def body(ab_ref, al_ref, dt_ref, o_ref):
        ab = ab_ref[...]
        lane = lax.broadcasted_iota(jnp.int32, (1, LANES), 1)
        g = -jnp.exp(al_ref[...]) * _softplus(ab + dt_ref[...])
        o_ref[...] = jnp.where(lane < GDN_HEADS, g, _sigmoid(ab))

    vec = pl.BlockSpec((1, LANES), lambda j: (0, 0))
    return pl.pallas_call(
        body, name=name, grid=(1,), in_specs=[_col_spec(s, PB_AB), vec, vec], out_specs=_col_spec(s, 0),
        out_shape=jax.ShapeDtypeStruct((s, LANES), F32), compiler_params=_params("arbitrary"),
    )(proj, _lane_pad(a_log), _lane_pad(dt_bias))


def _gdn_gates_bwd(proj, a_log, dt_bias, dgb, *, name):
    s = proj.shape[0]

    def body(ab_ref, al_ref, dt_ref, d_ref, dab_ref, dal_ref, ddt_ref):
        ab, d = ab_ref[...], d_ref[...]
        lane = lax.broadcasted_iota(jnp.int32, (1, LANES), 1)
        ea = jnp.exp(al_ref[...])
        pre = ab + dt_ref[...]
        g = -ea * _softplus(pre)
        dpre = d * (-ea) * _sigmoid(pre)
        beta = _sigmoid(ab)
        is_g = lane < GDN_HEADS
        dab = jnp.where(is_g, dpre, jnp.where(lane < 2 * GDN_HEADS, d * beta * (1.0 - beta), 0.0))
        dab_ref[...] = dab.astype(dab_ref.dtype)
        dal_ref[...] = jnp.sum(jnp.where(is_g, d * g, 0.0), axis=0, keepdims=True)
        ddt_ref[...] = jnp.sum(jnp.where(is_g, dpre, 0.0), axis=0, keepdims=True)

    vec = pl.BlockSpec((1, LANES), lambda j: (0, 0))
    dab, dal, ddt = pl.pallas_call(
        body, name=name, grid=(1,), in_specs=[_col_spec(s, PB_AB), vec, vec, _col_spec(s, 0)],
        out_specs=[_col_spec(s, 0), vec, vec],
        out_shape=[jax.ShapeDtypeStruct((s, LANES), BF16), jax.ShapeDtypeStruct((1, LANES), F32),
                   jax.ShapeDtypeStruct((1, LANES), F32)],
        compiler_params=_params("arbitrary"),
    )(proj, _lane_pad(a_log), _lane_pad(dt_bias), dgb)
    return dab, dal[0, :GDN_HEADS], ddt[0, :GDN_HEADS]


def _chunk_common(q, k, v, gb, gbt, h):
    c = CHUNK
    row = lax.broadcasted_iota(jnp.int32, (c, c), 0)
    col = lax.broadcasted_iota(jnp.int32, (c, c), 1)
    tril, strict, eye = row >= col, row > col, row == col
    lane = lax.broadcasted_iota(jnp.int32, (c, LANES), 1)
    sub = lax.broadcasted_iota(jnp.int32, (2 * GDN_HEADS, c), 0)
    g_col = jnp.sum(jnp.where(lane == h, gb, 0.0), axis=1, keepdims=True)
    beta_col = jnp.sum(jnp.where(lane == GDN_HEADS + h, gb, 0.0), axis=1, keepdims=True)
    g_row = jnp.sum(jnp.where(sub == h, gbt, 0.0), axis=0, keepdims=True)
    gc_col = jnp.sum(jnp.where(tril, jnp.broadcast_to(g_row, (c, c)), 0.0), axis=1, keepdims=True)
    gc_row = jnp.sum(jnp.where(row <= col, jnp.broadcast_to(g_col, (c, c)), 0.0), axis=0, keepdims=True)
    g_tot = jnp.sum(g_row, axis=1, keepdims=True)
    dm = jnp.exp(jnp.where(tril, gc_col - gc_row, -1e30))
    e_col = jnp.exp(gc_col)
    kdec_col = jnp.exp(g_tot - gc_col)
    gamma = jnp.exp(g_tot)
    kb = k * beta_col
    vb = v * beta_col
    kbg = kb * e_col
    a = jnp.where(strict, _dot(kb, k, NT, HI) * dm, 0.0)
    bneg = -a
    t = jnp.where(eye, 1.0, 0.0) + bneg
    p = _dot(bneg, bneg, precision=HI)
    for lvl in range(5):
        t = t + _dot(t, p, precision=HI)
        if lvl < 4:
            p = _dot(p, p, precision=HI)
    u = _dot(t, vb, precision=HI)
    w = _dot(t, kbg, precision=HI)
    aqk = jnp.where(tril, _bdot(q, k, NT) * dm, 0.0)
    return dict(tril=tril, strict=strict, eye=eye, row=row, col=col, beta_col=beta_col, dm=dm, e_col=e_col,
                kdec_col=kdec_col, gamma=gamma, kb=kb, vb=vb, kbg=kbg, a=a, t=t, u=u, w=w, aqk=aqk,
                qd=q * e_col, kd=k * kdec_col)


def _gdn_chunk_fwd(qkvn, gb, gbt, *, name):
    s = qkvn.shape[0]
    n_chunks = s // CHUNK

    def body(q_ref, k_ref, v_ref, gb_ref, gbt_ref, o_ref, st_ref, state):
        @pl.when(pl.program_id(0) == 0)
        def _():
            state[...] = jnp.zeros_like(state)

        gbv, gbtv = gb_ref[...], gbt_ref[0]
        for h in range(GDN_HEADS):
            hs = slice(h * GDN_DIM, (h + 1) * GDN_DIM)
            q, k, v = q_ref[:, hs], k_ref[:, hs], v_ref[:, hs]
            m = _chunk_common(q, k, v, gbv, gbtv, h)
            s0 = state[h]
            st_ref[0, h] = s0
            vnew = m["u"] - _bdot(m["w"], s0)
            o_ref[:, hs] = _bdot(m["qd"], s0) + _bdot(m["aqk"], vnew)
            state[h] = m["gamma"] * s0 + _bdot(m["kd"], vnew, TN)

    blk = lambda j: pl.BlockSpec((CHUNK, BRANCH_W), lambda n: (n, j))
    return pl.pallas_call(
        body, name=name, grid=(n_chunks,),
        in_specs=[blk(0), blk(1), blk(2), pl.BlockSpec((CHUNK, LANES), lambda n: (n, 0)),
                  pl.BlockSpec((1, 2 * GDN_HEADS, CHUNK), lambda n: (n, 0, 0))],
        out_specs=[blk(0), pl.BlockSpec((1, GDN_HEADS, GDN_DIM, GDN_DIM), lambda n: (n, 0, 0, 0))],
        out_shape=[jax.ShapeDtypeStruct((s, BRANCH_W), F32),
                   jax.ShapeDtypeStruct((n_chunks, GDN_HEADS, GDN_DIM, GDN_DIM), F32)],
        scratch_shapes=[pltpu.VMEM((GDN_HEADS, GDN_DIM, GDN_DIM), F32)],
        compiler_params=_params("arbitrary"),
    )(qkvn, qkvn, qkvn, gb, gbt)


def _gdn_chunk_bwd(qkvn, gb, gbt, states, do, *, name):
    s = qkvn.shape[0]
    n_chunks = s // CHUNK
    c = CHUNK

    def body(q_ref, k_ref, v_ref, gb_ref, gbt_ref, st_ref, do_ref, dq_ref, dk_ref, dv_ref, dgb_ref, dstate):
        @pl.when(pl.program_id(0) == 0)
        def _():
            dstate[...] = jnp.zeros_like(dstate)

        gbv, gbtv = gb_ref[...], gbt_ref[0]
        lane = lax.broadcasted_iota(jnp.int32, (c, LANES), 1)
        dgb = jnp.zeros((c, LANES), F32)
        for h in range(GDN_HEADS):
            hs = slice(h * GDN_DIM, (h + 1) * GDN_DIM)
            q, k, v, dov = q_ref[:, hs], k_ref[:, hs], v_ref[:, hs], do_ref[:, hs]
            m = _chunk_common(q, k, v, gbv, gbtv, h)
            tril, strict, eye, row, col = m["tril"], m["strict"], m["eye"], m["row"], m["col"]
            s0, ds1 = st_ref[0, h], dstate[h]
            vnew = m["u"] - _bdot(m["w"], s0)
            dvnew = _bdot(m["aqk"], dov, TN) + _bdot(m["kd"], ds1)
            dqd = _bdot(dov, s0, NT)
            daqk = jnp.where(tril, _bdot(dov, vnew, NT), 0.0)
            dkd = _bdot(vnew, ds1, NT)
            dgamma = jnp.sum(jnp.sum(s0 * ds1, axis=1, keepdims=True), axis=0, keepdims=True)
            dw = -_bdot(dvnew, s0, NT)
            dstate[h] = m["gamma"] * ds1 + _bdot(m["qd"], dov, TN) - _bdot(m["w"], dvnew, TN)
            dt = _dot(dvnew, m["vb"], NT, HI) + _dot(dw, m["kbg"], NT, HI)
            dvb = _dot(m["t"], dvnew, TN, HI)
            dkbg = _dot(m["t"], dw, TN, HI)
            da = jnp.where(strict, -_dot(_dot(m["t"], dt, TN, HI), m["t"], NT, HI), 0.0)
            dmat = da * m["dm"]
            dmq = daqk * m["dm"]
            dkb = _dot(dmat, k, precision=HI) + dkbg * m["e_col"]
            dq = _bdot(dmq, k) + dqd * m["e_col"]
            dk = (_dot(dmat, m["kb"], TN, HI) + _bdot(dmq, q, TN) + dkd * m["kdec_col"] + m["beta_col"] * dkb)
            dbeta_col = jnp.sum(dkb * k, axis=1, keepdims=True) + jnp.sum(dvb * v, axis=1, keepdims=True)
            e = da * m["a"] + daqk * m["aqk"]
            rs_kd = jnp.sum(dkd * m["kd"], axis=1, keepdims=True)
            e_colsum = jnp.sum(e, axis=0, keepdims=True)
            e_colsum_c = jnp.sum(jnp.where(eye, jnp.broadcast_to(e_colsum, (c, c)), 0.0), axis=1, keepdims=True)
            dgc = (jnp.sum(e, axis=1, keepdims=True) - e_colsum_c + jnp.sum(dqd * m["qd"], axis=1, keepdims=True)
                   - rs_kd + jnp.sum(dkbg * m["kbg"], axis=1, keepdims=True))
            last = jnp.sum(rs_kd, axis=0, keepdims=True) + dgamma * m["gamma"]
            dgc = dgc + jnp.where(lax.broadcasted_iota(jnp.int32, (c, 1), 0) == c - 1, last, 0.0)
            dgc_row = jnp.sum(jnp.where(eye, jnp.broadcast_to(dgc, (c, c)), 0.0), axis=0, keepdims=True)
            dg_col = jnp.sum(jnp.where(col >= row, jnp.broadcast_to(dgc_row, (c, c)), 0.0), axis=1, keepdims=True)
            dq_ref[:, hs] = dq
            dk_ref[:, hs] = dk
            dv_ref[:, hs] = m["beta_col"] * dvb
            dgb = dgb + jnp.where(lane == h, dg_col, 0.0) + jnp.where(lane == GDN_HEADS + h, dbeta_col, 0.0)
        dgb_ref[...] = dgb

    rev = lambda n: n_chunks - 1 - n
    blk = lambda j: pl.BlockSpec((CHUNK, BRANCH_W), lambda n: (rev(n), j))
    dq, dk, dv, dgb = pl.pallas_call(
        body, name=name, grid=(n_chunks,),
        in_specs=[blk(0), blk(1), blk(2), pl.BlockSpec((CHUNK, LANES), lambda n: (rev(n), 0)),
                  pl.BlockSpec((1, 2 * GDN_HEADS, CHUNK), lambda n: (rev(n), 0, 0)),
                  pl.BlockSpec((1, GDN_HEADS, GDN_DIM, GDN_DIM), lambda n: (rev(n), 0, 0, 0)), blk(0)],
        out_specs=[blk(0), blk(0), blk(0), pl.BlockSpec((CHUNK, LANES), lambda n: (rev(n), 0))],
        out_shape=[jax.ShapeDtypeStruct((s, BRANCH_W), F32)] * 3 + [jax.ShapeDtypeStruct((s, LANES), F32)],
        scratch_shapes=[pltpu.VMEM((GDN_HEADS, GDN_DIM, GDN_DIM), F32)],
        compiler_params=_params("arbitrary"),
    )(qkvn, qkvn, qkvn, gb, gbt, states, do)
    return jnp.concatenate([dq, dk, dv], axis=1), dgb


def _gdn_post_fwd(o, proj, norm_w, *, name):
    s = o.shape[0]

    def body(o_ref, g_ref, w_ref, y_ref):
        ov, gv = o_ref[...], g_ref[...]
        r = lax.rsqrt(jnp.mean(ov * ov, axis=-1, keepdims=True) + EPS)
        y_ref[...] = (ov * r * w_ref[...] * (gv * _sigmoid(gv))).astype(y_ref.dtype)

    return pl.pallas_call(
        body, name=name, grid=(GDN_HEADS,),
        in_specs=[_col_spec(s, 0), _col_spec(s, PB_GATE), pl.BlockSpec((1, LANES), lambda j: (0, 0))],
        out_specs=_col_spec(s, 0), out_shape=jax.ShapeDtypeStruct((s, BRANCH_W), BF16),
        compiler_params=_params("parallel"),
    )(o, proj, norm_w.reshape(1, GDN_DIM))


def _gdn_post_bwd(o, proj, norm_w, dy, *, name):
    s = o.shape[0]

    def body(o_ref, g_ref, w_ref, dy_ref, do_ref, dg_ref, dw_ref):
        ov, gv, w, dyv = o_ref[...], g_ref[...], w_ref[...], dy_ref[...].astype(F32)
        r = lax.rsqrt(jnp.mean(ov * ov, axis=-1, keepdims=True) + EPS)
        oh = ov * r
        sg = _sigmoid(gv)
        silu = gv * sg
        dn = dyv * silu
        dg_ref[...] = (dyv * (oh * w) * (sg * (1.0 + gv * (1.0 - sg)))).astype(dg_ref.dtype)

        @pl.when(pl.program_id(0) == 0)
        def _():
            dw_ref[...] = jnp.zeros_like(dw_ref)

        dw_ref[...] += jnp.sum(dn * oh, axis=0, keepdims=True)
        g2 = dn * w
        do_ref[...] = r * (g2 - oh * jnp.mean(g2 * oh, axis=-1, keepdims=True))

    do, dg, dw = pl.pallas_call(
        body, name=name, grid=(GDN_HEADS,),
        in_specs=[_col_spec(s, 0), _col_spec(s, PB_GATE), pl.BlockSpec((1, LANES), lambda j: (0, 0)), _col_spec(s, 0)],
        out_specs=[_col_spec(s, 0), _col_spec(s, 0), pl.BlockSpec((1, LANES), lambda j: (0, 0))],
        out_shape=[jax.ShapeDtypeStruct((s, BRANCH_W), F32), jax.ShapeDtypeStruct((s, BRANCH_W), BF16),
                   jax.ShapeDtypeStruct((1, LANES), F32)],
        compiler_params=_params("arbitrary"),
    )(o, proj, norm_w.reshape(1, GDN_DIM), dy)
    return do, dg, dw.reshape(LANES)


def _split_terms(x):
    hi = x.astype(BF16)
    lo = (x - hi.astype(F32)).astype(BF16)
    return jnp.concatenate([hi, lo], axis=1)


def _sb_sum_matrix(pred):
    row = lax.broadcasted_iota(jnp.int32, (2 * SB_TILE, 2 * SB_TILE), 0) % SB_TILE
    col = lax.broadcasted_iota(jnp.int32, (2 * SB_TILE, 2 * SB_TILE), 1)
    return jnp.where((col >= SB_TILE) | pred(row, col), 1.0, 0.0).astype(BF16)


def _sb_head_masks():
    lane = lax.broadcasted_iota(jnp.int32, (1, LANES), 1)
    return [(lane < SB_DIM).astype(F32), (lane >= SB_DIM).astype(F32)]


def _sb_fwd(proj, *, name):
    s = proj.shape[0]
    t = SB_TILE
    nq = s // t

    def body(q_ref, k_ref, v_ref, o_ref, tot_ref):
        cmr = lax.broadcasted_iota(jnp.int32, (t, t), 1) - lax.broadcasted_iota(jnp.int32, (t, t), 0)
        uo = _sb_sum_matrix(lambda row, col: row > col)
        hm = _sb_head_masks()

        def qloop(i, carry0):
            qs = pl.multiple_of(i * t, t)
            qf = q_ref[pl.ds(qs, t), :] * SB_SCALE
            qh = [(qf * hm[h]).astype(BF16) for h in range(2)]
            diag = i // SB_GROUP

            def group(g, st, masked):
                ks = pl.multiple_of(g * (SB_GROUP * t), SB_GROUP * t)
                kb = k_ref[pl.ds(ks, SB_GROUP * t), :].astype(BF16)
                vf = v_ref[pl.ds(ks, SB_GROUP * t), :]
                tiles = [(h, j) for h in range(2) for j in range(SB_GROUP)]
                z = [_dot(qh[h], kb, NT) for h in range(2)]
                keep = {j: cmr < (i - g * SB_GROUP - j) * t for j in range(SB_GROUP)} if masked else None
                base, terms = {}, {}
                for h, j in tiles:
                    zj = z[h][:, j * t:(j + 1) * t]
                    sp = _softplus(zj)
                    l1m = jnp.where(keep[j], -sp, 0.0) if masked else -sp
                    base[h, j] = zj - sp
                    terms[h, j] = _split_terms(l1m)
                sums = {hj: _dot(terms[hj], uo) for hj in tiles}
                acc, new = st[0], []
                for h in range(2):
                    run, att = st[1 + h], [None] * SB_GROUP
                    for j in reversed(range(SB_GROUP)):
                        a = jnp.exp(base[h, j] + sums[h, j][:, :t] + run)
                        att[j] = (jnp.where(keep[j], a, 0.0) if masked else a).astype(BF16)
                        run = run + sums[h, j][:, t:]
                    acc = acc + _dot(jnp.concatenate(att, axis=1), (vf * hm[h]).astype(BF16))
                    new.append(run)
                return (acc, *new)

            zero = jnp.zeros((t, LANES), F32)
            st = group(diag, (zero, zero, zero), True)
            st = lax.fori_loop(0, diag, lambda jj, sv: group(diag - 1 - jj, sv, False), st)
            o_ref[pl.ds(qs, t), :] = st[0]
            tot_ref[pl.ds(qs, t), :] = st[1] * hm[0] + st[2] * hm[1]
            return carry0

        lax.fori_loop(0, nq, qloop, 0)

    out = jax.ShapeDtypeStruct((s, BRANCH_W), F32)
    return pl.pallas_call(
        body, name=name, grid=(SB_HEADS // 2,),
        in_specs=[_col_spec(s, PB_SB), _col_spec(s, PB_SB + 4), _col_spec(s, PB_SB + 8)],
        out_specs=[_col_spec(s, 0)] * 2, out_shape=[out] * 2,
        compiler_params=_params("parallel"),
    )(proj, proj, proj)


def _sb_bwd(proj, tot, do, *, name):
    s = proj.shape[0]
    t = SB_TILE
    nq = s // t

    def body(q_ref, k_ref, v_ref, tot_ref, do_ref, dq_ref, dk_ref, dv_ref, dk_acc, dv_acc):
        dk_acc[...] = jnp.zeros_like(dk_acc)
        dv_acc[...] = jnp.zeros_like(dv_acc)
        cmr = lax.broadcasted_iota(jnp.int32, (t, t), 1) - lax.broadcasted_iota(jnp.int32, (t, t), 0)
        u_le = _sb_sum_matrix(lambda row, col: row <= col)
        u_lt = _sb_sum_matrix(lambda row, col: row < col)
        hm = _sb_head_masks()

        def qloop(i, carry0):
            qs = pl.multiple_of(i * t, t)
            qraw = q_ref[pl.ds(qs, t), :]
            dov = do_ref[pl.ds(qs, t), :].astype(F32)
            totv = tot_ref[pl.ds(qs, t), :]
            qh = [(qraw * (hm[h] * SB_SCALE)).astype(BF16) for h in range(2)]
            q2 = jnp.concatenate([(qraw * hm[h]).astype(BF16) for h in range(2)], axis=0)
            doh = [(dov * hm[h]).astype(BF16) for h in range(2)]
            do2 = jnp.concatenate(doh, axis=0)
            tot = [jnp.min(totv * hm[h], axis=1, keepdims=True) for h in range(2)]
            diag = i // SB_GROUP

            def group(g, st, masked):
                ks = pl.multiple_of(g * (SB_GROUP * t), SB_GROUP * t)
                kf = k_ref[pl.ds(ks, SB_GROUP * t), :]
                kb = kf.astype(BF16)
                vb = v_ref[pl.ds(ks, SB_GROUP * t), :].astype(BF16)
                tiles = [(h, j) for h in range(2) for j in range(SB_GROUP)]
                z = [_dot(qh[h], kb, NT) for h in range(2)]
                datt = [_dot(doh[h], vb, NT) for h in range(2)]
                keep = {j: cmr < (i - g * SB_GROUP - j) * t for j in range(SB_GROUP)} if masked else None
                ls, lterms = {}, {}
                for h, j in tiles:
                    zj = z[h][:, j * t:(j + 1) * t]
                    sp = _softplus(zj)
                    ls[h, j] = zj - sp
                    lterms[h, j] = _split_terms(jnp.where(keep[j], -sp, 0.0) if masked else -sp)
                lsum = {hj: _dot(lterms[hj], u_le) for hj in tiles}
                att, p, pterms, new_c = {}, {}, {}, []
                for h in range(2):
                    run = st[1 + h]
                    for j in range(SB_GROUP):
                        a = jnp.exp(ls[h, j] + ((tot[h] - run) - lsum[h, j][:, :t]))
                        if masked:
                            a = jnp.where(keep[j], a, 0.0)
                        att[h, j] = a.astype(BF16)
                        p[h, j] = a * datt[h][:, j * t:(j + 1) * t]
                        pterms[h, j] = _split_terms(p[h, j])
                        run = run + lsum[h, j][:, t:]
                    new_c.append(run)
                psum = {hj: _dot(pterms[hj], u_lt) for hj in tiles}
                dzb, new_r = {}, []
                for h in range(2):
                    run = st[3 + h]
                    for j in range(SB_GROUP):
                        sig = jnp.exp(ls[h, j])
                        dz = p[h, j] * (1.0 - sig) - (run + psum[h, j][:, :t]) * sig
                        if masked:
                            dz = jnp.where(keep[j], dz, 0.0)
                        dzb[h, j] = (dz * SB_SCALE).astype(BF16)
                        run = run + psum[h, j][:, t:]
                    new_r.append(run)
                k2 = jnp.concatenate([(kf * hm[h]).astype(BF16) for h in range(2)], axis=0)
                dq_acc = st[0] + _dot(jnp.concatenate([dzb[hj] for hj in tiles], axis=1), k2)
                for j in range(SB_GROUP):
                    rows = pl.ds(pl.multiple_of(ks + j * t, t), t)
                    dk_acc[rows, :] += _dot(jnp.concatenate([dzb[0, j], dzb[1, j]], axis=0), q2, TN)
                    dv_acc[rows, :] += _dot(jnp.concatenate([att[0, j], att[1, j]], axis=0), do2, TN)
                return (dq_acc, *new_c, *new_r)

            zero = jnp.zeros((t, LANES), F32)
            st = lax.fori_loop(0, diag, lambda jj, sv: group(jj, sv, False), (zero,) * 5)
            st = group(diag, st, True)
            dq_ref[pl.ds(qs, t), :] = st[0].astype(dq_ref.dtype)
            return carry0

        lax.fori_loop(0, nq, qloop, 0)
        dk_ref[...] = dk_acc[...].astype(dk_ref.dtype)
        dv_ref[...] = dv_acc[...].astype(dv_ref.dtype)

    out = jax.ShapeDtypeStruct((s, BRANCH_W), BF16)
    return pl.pallas_call(
        body, name=name, grid=(SB_HEADS // 2,),
        in_specs=[_col_spec(s, PB_SB), _col_spec(s, PB_SB + 4), _col_spec(s, PB_SB + 8), _col_spec(s, 0), _col_spec(s, 0)],
        out_specs=[_col_spec(s, 0)] * 3, out_shape=[out] * 3,
        scratch_shapes=[pltpu.VMEM((s, LANES), F32), pltpu.VMEM((s, LANES), F32)],
        compiler_params=_params("parallel"),
    )(proj, proj, proj, tot, do)


def _sc_fwd(proj, conv_w, *, name):
    s = proj.shape[0]

    def body(x_ref, b_ref, c_ref, w_ref, y_ref):
        rows = lax.broadcasted_iota(jnp.int32, (s, LANES), 0)
        w = w_ref[...]
        u = c_ref[...] * x_ref[...]
        cv = w[2:3, :] * u + w[1:2, :] * _shift_down(u, 1, rows) + w[0:1, :] * _shift_down(u, 2, rows)
        y_ref[...] = (b_ref[...] * cv).astype(y_ref.dtype)

    return pl.pallas_call(
        body, name=name, grid=(BRANCH_W // LANES,),
        in_specs=[_col_spec(s, PB_SCX), _col_spec(s, PB_SCB), _col_spec(s, PB_SCC), pl.BlockSpec((3, LANES), lambda j: (0, j))],
        out_specs=_col_spec(s, 0), out_shape=jax.ShapeDtypeStruct((s, BRANCH_W), BF16),
        compiler_params=_params("parallel"),
    )(proj, proj, proj, conv_w)


def _sc_bwd(proj, conv_w, dy, *, name):
    s = proj.shape[0]

    def body(x_ref, b_ref, c_ref, w_ref, dy_ref, dx_ref, db_ref, dc_ref, dw_ref):
        rows = lax.broadcasted_iota(jnp.int32, (s, LANES), 0)
        w, x, cg, dyv = w_ref[...], x_ref[...], c_ref[...], dy_ref[...].astype(F32)
        u = cg * x
        u1, u2 = _shift_down(u, 1, rows), _shift_down(u, 2, rows)
        cv = w[2:3, :] * u + w[1:2, :] * u1 + w[0:1, :] * u2
        db_ref[...] = (dyv * cv).astype(db_ref.dtype)
        dcv = dyv * b_ref[...]
        du = w[2:3, :] * dcv + w[1:2, :] * _shift_up(dcv, 1, rows) + w[0:1, :] * _shift_up(dcv, 2, rows)
        dx_ref[...] = (du * cg).astype(dx_ref.dtype)
        dc_ref[...] = (du * x).astype(dc_ref.dtype)
        dw_ref[0:1, :] = jnp.sum(dcv * u2, axis=0, keepdims=True)
        dw_ref[1:2, :] = jnp.sum(dcv * u1, axis=0, keepdims=True)
        dw_ref[2:3, :] = jnp.sum(dcv * u, axis=0, keepdims=True)

    out = jax.ShapeDtypeStruct((s, BRANCH_W), BF16)
    wspec = pl.BlockSpec((3, LANES), lambda j: (0, j))
    return pl.pallas_call(
        body, name=name, grid=(BRANCH_W // LANES,),
        in_specs=[_col_spec(s, PB_SCX), _col_spec(s, PB_SCB), _col_spec(s, PB_SCC), wspec, _col_spec(s, 0)],
        out_specs=[_col_spec(s, 0)] * 3 + [wspec],
        out_shape=[out] * 3 + [jax.ShapeDtypeStruct((3, BRANCH_W), F32)],
        compiler_params=_params("parallel"),
    )(proj, proj, proj, conv_w, dy)


MERGE_TM, MERGE_TN = 512, 512


def _merge_specs():
    tm, tn = MERGE_TM, MERGE_TN
    y_spec = pl.BlockSpec((tm, BRANCH_W), lambda i, j: (i, 0))
    w_spec = pl.BlockSpec((N_BRANCH, BRANCH_W, tn), lambda i, j: (0, 0, j))
    gate_specs = [pl.BlockSpec((tm, tn), functools.partial(
        lambda i, j, b: (i, (PB_GATES * LANES + b * D_MODEL) // tn + j), b=b)) for b in range(N_BRANCH)]
    mn = pl.BlockSpec((tm, tn), lambda i, j: (i, j))
    return y_spec, w_spec, gate_specs, mn


def _merge_fwd(ya, yb, yc, wb, proj, *, name):
    s = ya.shape[0]
    y_spec, w_spec, gate_specs, mn = _merge_specs()

    def body(ya_ref, yb_ref, yc_ref, w_ref, g0, g1, g2, o_ref):
        acc = None
        for b, (y_ref, g_ref) in enumerate(zip((ya_ref, yb_ref, yc_ref), (g0, g1, g2))):
            term = _sigmoid(g_ref[...]) * _bdot(y_ref[...], w_ref[b])
            acc = term if acc is None else acc + term
        o_ref[...] = acc.astype(o_ref.dtype)

    return pl.pallas_call(
        body, name=name, grid=(s // MERGE_TM, D_MODEL // MERGE_TN),
        in_specs=[y_spec] * 3 + [w_spec] + gate_specs, out_specs=mn,
        out_shape=jax.ShapeDtypeStruct((s, D_MODEL), BF16), compiler_params=_params("parallel", "parallel"),
    )(ya, yb, yc, wb, proj, proj, proj)


def _merge_bwd(ya, yb, yc, wb, proj, dm, *, name):
    s = ya.shape[0]
    y_spec, w_spec, gate_specs, mn = _merge_specs()

    def body(ya_ref, yb_ref, yc_ref, w_ref, g0, g1, g2, dm_ref, *outs):
        dmv = dm_ref[...].astype(F32)
        for b, (y_ref, g_ref) in enumerate(zip((ya_ref, yb_ref, yc_ref), (g0, g1, g2))):
            sg = _sigmoid(g_ref[...])
            z = _bdot(y_ref[...], w_ref[b])
            outs[b][...] = (dmv * sg).astype(BF16)
            outs[N_BRANCH + b][...] = (dmv * z * sg * (1.0 - sg)).astype(BF16)

    out = jax.ShapeDtypeStruct((s, D_MODEL), BF16)
    res = pl.pallas_call(
        body, name=name, grid=(s // MERGE_TM, D_MODEL // MERGE_TN),
        in_specs=[y_spec] * 3 + [w_spec] + gate_specs + [mn], out_specs=[mn] * (2 * N_BRANCH),
        out_shape=[out] * (2 * N_BRANCH), compiler_params=_params("parallel", "parallel"),
    )(ya, yb, yc, wb, proj, proj, proj, dm)
    return res[:N_BRANCH], res[N_BRANCH:]


def _chunk_rows(v, s):
    return v[:, :2 * GDN_HEADS].reshape(s // CHUNK, CHUNK, 2 * GDN_HEADS).transpose(0, 2, 1)


def _relu2_epi(acc):
    r = jnp.maximum(acc, 0.0)
    return acc, r * r


def _drelu2_epi(acc, a):
    return (acc * (2.0 * jnp.maximum(a.astype(F32), 0.0)),)


def _layer_fwd(x0, p):
    s = x0.shape[0]
    h1 = _norm_fwd(x0, p["norm_mix_pre"], name="norm_mix_pre")
    proj = _matmul(h1, p["w_in"], name="proj_in", tm=512, tn=1664)
    qkvn = _gdn_pre_fwd(proj, p["conv_qkv_w"], name="gdn_pre")
    gb = _gdn_gates_fwd(proj, p["gdn_a_log"], p["gdn_dt_bias"], name="gdn_gates")
    gbt = _chunk_rows(gb, s)
    o_gdn, states = _gdn_chunk_fwd(qkvn, gb, gbt, name="gdn_chunk")
    ya = _gdn_post_fwd(o_gdn, proj, p["gdn_norm_w"], name="gdn_post")
    o_sb, sb_tot = _sb_fwd(proj, name="sb_attn")
    yc = _sc_fwd(proj, p["conv_sc_w"], name="short_conv")
    merged = _merge_fwd(ya, o_sb, yc, p["w_branch"], proj, name="merge")
    u = _matmul(merged, p["w_out"], name="proj_out", tm=512, tn=1024)
    x1 = _resnorm_fwd(x0, u, p["norm_mix_post"], name="norm_mix_post")
    h2 = _norm_fwd(x1, p["norm_ffn_pre"], name="norm_ffn_pre")
    a, r = _matmul(h2, p["w_ff1"], name="ff1", tm=512, tn=1024, outs=(BF16, BF16), epi=_relu2_epi)
    f = _matmul(r, p["w_ff2"], name="ff2", tm=512, tn=1024, tk=1024)
    x2 = _resnorm_fwd(x1, f, p["norm_ffn_post"], name="norm_ffn_post")
    saved = dict(x0=x0, h1=h1, proj=proj, qkvn=qkvn, gb=gb, gbt=gbt, o_gdn=o_gdn, states=states, ya=ya, o_sb=o_sb,
                 sb_tot=sb_tot, yc=yc, merged=merged, u=u, x1=x1, h2=h2, a=a, r=r, f=f)
    return x2, saved


def _layer_bwd(dx2, p, sv):
    g = {}
    df, g["norm_ffn_post"] = _norm_bwd(sv["f"], p["norm_ffn_post"], dx2, None, out_dtype=BF16, name="norm_ffn_post_bwd")
    g["w_ff2"] = _matmul(sv["r"], df, ta=True, name="ff2_dw", tm=1024, tn=1024, tk=512)
    da = _matmul(df, p["w_ff2"], tb=True, name="ff2_dx", tm=512, tn=1024, outs=(BF16,), epi=_drelu2_epi,
                 extras=(sv["a"],))
    g["w_ff1"] = _matmul(sv["h2"], da, ta=True, name="ff1_dw", tm=1024, tn=1024, tk=512)
    dh2 = _matmul(da, p["w_ff1"], tb=True, name="ff1_dx", tm=512, tn=1024, tk=1024)
    dx1, g["norm_ffn_pre"] = _norm_bwd(sv["x1"], p["norm_ffn_pre"], dh2, dx2, out_dtype=F32, name="norm_ffn_pre_bwd")
    du, g["norm_mix_post"] = _norm_bwd(sv["u"], p["norm_mix_post"], dx1, None, out_dtype=BF16, name="norm_mix_post_bwd")
    g["w_out"] = _matmul(sv["merged"], du, ta=True, name="out_dw", tm=1024, tn=1024, tk=512)
    dmerged = _matmul(du, p["w_out"], tb=True, name="out_dx", tm=512, tn=1024, outs=(BF16,))
    ys = (sv["ya"], sv["o_sb"], sv["yc"])
    dz, dgates = _merge_bwd(*ys, p["w_branch"], sv["proj"], dmerged, name="merge_bwd")
    g["w_branch"] = jnp.stack([_matmul(ys[b], dz[b], ta=True, name=f"branch_dw{b}", tm=512, tn=1024, tk=512)
                               for b in range(N_BRANCH)])
    dys = [_matmul(dz[b], p["w_branch"][b], tb=True, name=f"branch_dx{b}", tm=512, tn=512) for b in range(N_BRANCH)]
    dscx, dscb, dscc, g["conv_sc_w"] = _sc_bwd(sv["proj"], p["conv_sc_w"], dys[2], name="short_conv_bwd")
    dsq, dsk, dsv = _sb_bwd(sv["proj"], sv["sb_tot"], dys[1], name="sb_attn_bwd")
    do_gdn, dgate, dnw = _gdn_post_bwd(sv["o_gdn"], sv["proj"], p["gdn_norm_w"], dys[0], name="gdn_post_bwd")
    g["gdn_norm_w"] = dnw
    dqkvn, dgb = _gdn_chunk_bwd(sv["qkvn"], sv["gb"], sv["gbt"], sv["states"], do_gdn, name="gdn_chunk_bwd")
    dqkv, g["conv_qkv_w"] = _gdn_pre_bwd(sv["proj"], p["conv_qkv_w"], dqkvn, name="gdn_pre_bwd")
    dab, g["gdn_a_log"], g["gdn_dt_bias"] = _gdn_gates_bwd(sv["proj"], p["gdn_a_log"], p["gdn_dt_bias"], dgb,
                                                           name="gdn_gates_bwd")
    dproj = jnp.concatenate([*dgates, dqkv, dgate, dab, dsq, dsk, dsv, dscx, dscb, dscc], axis=1)
    g["w_in"] = _matmul(sv["h1"], dproj, ta=True, name="in_dw", tm=1024, tn=1664, tk=512)
    dh1 = _matmul(dproj, p["w_in"], tb=True, name="in_dx", tm=512, tn=1024, tk=1664)
    dx0, g["norm_mix_pre"] = _norm_bwd(sv["x0"], p["norm_mix_pre"], dh1, dx1, out_dtype=F32, name="norm_mix_pre_bwd")
    return dx0, g


def _local_step(x, target, layers):
    saved = []
    h = x
    for p in layers:
        h, sv = _layer_fwd(h, p)
        saved.append(sv)
    loss, dh = _loss_fwd_bwd(h, target, name="loss")
    grads = [None] * len(layers)
    for l in reversed(range(len(layers))):
        dh, grads[l] = _layer_bwd(dh, layers[l], saved[l])
    return loss, dh, grads


ANY = pl.BlockSpec(memory_space=pl.ANY)


def _me_and_chips():
    x, y, c = lax.axis_index("x"), lax.axis_index("y"), lax.axis_index("c")
    chips = [(1 - x, y), (x, 1 - y), (1 - x, 1 - y)]
    return x, y, c, chips


def _gather_chips(arrs, *, name):
    n = len(arrs)

    def body(*refs):
        ins, outs = refs[:n], refs[n:2 * n]
        send_sems, recv_sems, local_sems = refs[2 * n:]
        x, y, c, chips = _me_and_chips()
        me = 2 * x + y
        sends, locals_ = [], []
        for a in range(n):
            lc = pltpu.make_async_copy(ins[a], outs[a].at[me], local_sems.at[a])
            lc.start()
            locals_.append(lc)
            for k, (px, py) in enumerate(chips):
                cp = pltpu.make_async_remote_copy(src_ref=ins[a], dst_ref=outs[a].at[me], send_sem=send_sems.at[a, k],
                                                  recv_sem=recv_sems.at[a, k], device_id=(px, py, c), device_id_type=MESH)
                cp.start()
                sends.append(cp)
        for a in range(n):
            for k, (px, py) in enumerate(chips):
                pltpu.make_async_remote_copy(src_ref=ins[a], dst_ref=outs[a].at[2 * px + py], send_sem=send_sems.at[a, k],
                                             recv_sem=recv_sems.at[a, k], device_id=(px, py, c),
                                             device_id_type=MESH).wait_recv()
        for cp in sends:
            cp.wait_send()
        for lc in locals_:
            lc.wait()

    return pl.pallas_call(
        body, name=name, in_specs=[ANY] * n, out_specs=[ANY] * n,
        out_shape=[jax.ShapeDtypeStruct((N_CHIPS,) + a.shape, a.dtype) for a in arrs],
        scratch_shapes=[pltpu.SemaphoreType.DMA((n, 3)), pltpu.SemaphoreType.DMA((n, 3)), pltpu.SemaphoreType.DMA((n,))],
    )(*arrs)


def _scatter_partials(arrs, small, *, name):
    n = len(arrs)

    def body(*refs):
        ins, small_ref = refs[:n], refs[n]
        outs, small_out = refs[n + 1:2 * n + 1], refs[2 * n + 1]
        send_sems, recv_sems, local_sems, ssend, srecv = refs[2 * n + 2:]
        x, y, c, chips = _me_and_chips()
        me = 2 * x + y
        sends, locals_ = [], []
        for a in range(n):
            lc = pltpu.make_async_copy(ins[a].at[me], outs[a].at[me], local_sems.at[a])
            lc.start()
            locals_.append(lc)
            for k, (px, py) in enumerate(chips):
                cp = pltpu.make_async_remote_copy(src_ref=ins[a].at[2 * px + py], dst_ref=outs[a].at[me],
                                                  send_sem=send_sems.at[a, k], recv_sem=recv_sems.at[a, k],
                                                  device_id=(px, py, c), device_id_type=MESH)
                cp.start()
                sends.append(cp)
        dev = 4 * x + 2 * y + c
        lc = pltpu.make_async_copy(small_ref, small_out.at[dev], local_sems.at[n])
        lc.start()
        locals_.append(lc)
        peers = [(x, y, 1 - c)] + [(px, py, pc) for (px, py) in chips for pc in (c, 1 - c)]
        for k, peer in enumerate(peers):
            cp = pltpu.make_async_remote_copy(src_ref=small_ref, dst_ref=small_out.at[dev], send_sem=ssend.at[k],
                                              recv_sem=srecv.at[k], device_id=peer, device_id_type=MESH)
            cp.start()
            sends.append(cp)
        for a in range(n):
            for k, (px, py) in enumerate(chips):
                pltpu.make_async_remote_copy(src_ref=ins[a].at[me], dst_ref=outs[a].at[2 * px + py],
                                             send_sem=send_sems.at[a, k], recv_sem=recv_sems.at[a, k],
                                             device_id=(px, py, c), device_id_type=MESH).wait_recv()
        for k, (px, py, pc) in enumerate(peers):
            pltpu.make_async_remote_copy(src_ref=small_ref, dst_ref=small_out.at[4 * px + 2 * py + pc], send_sem=ssend.at[k],
                                         recv_sem=srecv.at[k], device_id=(px, py, pc), device_id_type=MESH).wait_recv()
        for cp in sends:
            cp.wait_send()
        for lc in locals_:
            lc.wait()

    res = pl.pallas_call(
        body, name=name, in_specs=[ANY] * (n + 1), out_specs=[ANY] * (n + 1),
        out_shape=[jax.ShapeDtypeStruct(a.shape, a.dtype) for a in arrs]
        + [jax.ShapeDtypeStruct((N_DEV,) + small.shape, small.dtype)],
        scratch_shapes=[pltpu.SemaphoreType.DMA((n, 3)), pltpu.SemaphoreType.DMA((n, 3)), pltpu.SemaphoreType.DMA((n + 1,)),
                        pltpu.SemaphoreType.DMA((N_DEV - 1,)), pltpu.SemaphoreType.DMA((N_DEV - 1,))],
    )(*arrs, small)
    return res[:n], res[n]


def _swap_sibling(arrs, *, name):
    n = len(arrs)

    def body(*refs):
        ins, outs = refs[:n], refs[n:2 * n]
        send_sems, recv_sems = refs[2 * n:]
        x, y, c = lax.axis_index("x"), lax.axis_index("y"), lax.axis_index("c")
        cps = [pltpu.make_async_remote_copy(src_ref=ins[a], dst_ref=outs[a], send_sem=send_sems.at[a],
                                            recv_sem=recv_sems.at[a], device_id=(x, y, 1 - c), device_id_type=MESH)
               for a in range(n)]
        for cp in cps:
            cp.start()
        for cp in cps:
            cp.wait()

    return pl.pallas_call(
        body, name=name, in_specs=[ANY] * n, out_specs=[ANY] * n,
        out_shape=[jax.ShapeDtypeStruct(a.shape, a.dtype) for a in arrs],
        scratch_shapes=[pltpu.SemaphoreType.DMA((n,)), pltpu.SemaphoreType.DMA((n,))],
    )(*arrs)


def _row_tile(rows, cols, budget=2 * 1024 * 1024):
    best = None
    for t in range(16, rows + 1, 16):
        if rows % t == 0 and t * cols * 4 <= budget:
            best = t
    return best if best is not None else rows


def _sum_slots(parts, *, name):
    n, rows, cols = parts.shape
    tr = _row_tile(rows, cols, 1024 * 1024)

    def body(p_ref, o_ref):
        acc = p_ref[0].astype(F32)
        for i in range(1, n):
            acc = acc + p_ref[i].astype(F32)
        o_ref[...] = acc

    return pl.pallas_call(
        body, name=name, grid=(rows // tr,), in_specs=[pl.BlockSpec((n, tr, cols), lambda i: (0, i, 0))],
        out_specs=pl.BlockSpec((tr, cols), lambda i: (i, 0)), out_shape=jax.ShapeDtypeStruct((rows, cols), F32),
        compiler_params=_params("parallel"),
    )(parts)


def _adamw(w, m, v, g_a, g_b, *, name):
    rows, cols = w.shape
    tr = _row_tile(rows, cols, 1024 * 1024)
    two = g_b is not None
    c1 = 1.0 / (1.0 - ADAM_B1 ** ADAM_STEP)
    c2 = 1.0 / (1.0 - ADAM_B2 ** ADAM_STEP)

    def body(*refs):
        w_ref, m_ref, v_ref, ga_ref = refs[:4]
        g_ref, d_ref, nm_ref, nv_ref = refs[4 + two:]
        g = ga_ref[...]
        if two:
            g = g + refs[4][...]
        nm = ADAM_B1 * m_ref[...] + (1.0 - ADAM_B1) * g
        nv = ADAM_B2 * v_ref[...] + (1.0 - ADAM_B2) * (g * g)
        g_ref[...] = g
        nm_ref[...] = nm
        nv_ref[...] = nv
        d_ref[...] = -ADAM_LR * ((nm * c1) / (jnp.sqrt(nv * c2) + ADAM_EPS) + ADAM_WD * w_ref[...])

    blk = pl.BlockSpec((tr, cols), lambda i: (i, 0))
    ins = [w, m, v, g_a] + ([g_b] if two else [])
    return pl.pallas_call(
        body, name=name, grid=(rows // tr,), in_specs=[blk] * len(ins), out_specs=[blk] * 4,
        out_shape=[jax.ShapeDtypeStruct((rows, cols), F32)] * 4, compiler_params=_params("parallel"),
    )(*ins)


def _cast_bf16(w, *, name):
    rows, cols = w.shape
    tr = _row_tile(rows, cols)

    def body(w_ref, o_ref):
        o_ref[...] = w_ref[...].astype(BF16)

    blk = pl.BlockSpec((tr, cols), lambda i: (i, 0))
    return pl.pallas_call(body, name=name, grid=(rows // tr,), in_specs=[blk], out_specs=blk,
                          out_shape=jax.ShapeDtypeStruct((rows, cols), BF16), compiler_params=_params("parallel"))(w)


BIG = ("w_in", "w_branch", "w_out", "w_ff1", "w_ff2")
SMALL = ("norm_mix_pre", "conv_qkv_w", "gdn_a_log", "gdn_dt_bias", "gdn_norm_w", "conv_sc_w", "norm_mix_post",
         "norm_ffn_pre", "norm_ffn_post")
ORDER = ("norm_mix_pre", "w_in", "conv_qkv_w", "gdn_a_log", "gdn_dt_bias", "gdn_norm_w", "conv_sc_w", "w_branch",
         "w_out", "norm_mix_post", "norm_ffn_pre", "w_ff1", "w_ff2", "norm_ffn_post")


def _full_weights(gathered, rep):
    layers = []
    for l in range(DEPTH):
        p = dict(
            w_in=_pad_in_cols(gathered["w_in"][:, l].transpose(1, 0, 2).reshape(D_MODEL, IN_W)),
            w_branch=gathered["w_branch"][:, l].transpose(1, 2, 0, 3).reshape(N_BRANCH, BRANCH_W, D_MODEL),
            w_out=gathered["w_out"][:, l].reshape(D_MODEL, D_MODEL),
            w_ff1=gathered["w_ff1"][:, l].transpose(1, 0, 2).reshape(D_MODEL, D_FF),
            w_ff2=gathered["w_ff2"][:, l].reshape(D_FF, D_MODEL),
            conv_qkv_w=gathered["conv_qkv_w"][:, l].transpose(1, 0, 2).reshape(4, 3 * BRANCH_W),
            conv_sc_w=gathered["conv_sc_w"][:, l].transpose(1, 0, 2).reshape(3, BRANCH_W),
        )
        for k in ("norm_mix_pre", "gdn_a_log", "gdn_dt_bias", "gdn_norm_w", "norm_mix_post", "norm_ffn_pre", "norm_ffn_post"):
            p[k] = rep[k][l]
        layers.append(p)
    return layers


def _partials_by_chip(grads):
    def stack(name, fn):
        return jnp.stack([fn(g[name]) for g in grads], axis=1).astype(BF16)

    def w_in(gw):
        return _unpad_in_cols(gw).reshape(D_MODEL, N_CHIPS, IN_W // N_CHIPS).transpose(1, 0, 2)

    return dict(
        w_in=stack("w_in", w_in),
        w_branch=stack("w_branch", lambda gw: gw.reshape(N_BRANCH, BRANCH_W, N_CHIPS, D_MODEL // N_CHIPS).transpose(2, 0, 1, 3)),
        w_out=stack("w_out", lambda gw: gw.reshape(N_CHIPS, D_MODEL // N_CHIPS, D_MODEL)),
        w_ff1=stack("w_ff1", lambda gw: gw.reshape(D_MODEL, N_CHIPS, D_FF // N_CHIPS).transpose(1, 0, 2)),
        w_ff2=stack("w_ff2", lambda gw: gw.reshape(N_CHIPS, D_FF // N_CHIPS, D_MODEL)),
    )


def _pack_small(grads):
    pieces, layout = [], []
    for name in SMALL:
        v = jnp.stack([g[name] for g in grads]).astype(F32)
        layout.append((name, v.shape))
        pieces.append(v.reshape(-1))
    flat = jnp.concatenate(pieces)
    rows = -(-flat.shape[0] // LANES)
    rows = -(-rows // 8) * 8
    flat = jnp.pad(flat, (0, rows * LANES - flat.shape[0]))
    return flat.reshape(rows, LANES), layout


def _unpack_small(table, layout):
    flat, out, off = table.reshape(-1), {}, 0
    for name, shape in layout:
        size = 1
        for d in shape:
            size *= d
        out[name] = flat[off:off + size].reshape(shape)
        off += size
    return out


def _as2d(a):
    return a.reshape(-1, a.shape[-1]) if a.ndim > 1 else a.reshape(1, -1)


def kernel(x, norm_mix_pre, w_in, conv_qkv_w, gdn_a_log, gdn_dt_bias, gdn_norm_w, conv_sc_w, w_branch, w_out, norm_mix_post, norm_ffn_pre, w_ff1, w_ff2, norm_ffn_post, loss_target, m_norm_mix_pre, m_w_in, m_conv_qkv_w, m_gdn_a_log, m_gdn_dt_bias, m_gdn_norm_w, m_conv_sc_w, m_w_branch, m_w_out, m_norm_mix_post, m_norm_ffn_pre, m_w_ff1, m_w_ff2, m_norm_ffn_post, v_norm_mix_pre, v_w_in, v_conv_qkv_w, v_gdn_a_log, v_gdn_dt_bias, v_gdn_norm_w, v_conv_sc_w, v_w_branch, v_w_out, v_norm_mix_post, v_norm_ffn_pre, v_w_ff1, v_w_ff2, v_norm_ffn_post):
    w = dict(norm_mix_pre=norm_mix_pre, w_in=w_in, conv_qkv_w=conv_qkv_w, gdn_a_log=gdn_a_log, gdn_dt_bias=gdn_dt_bias,
             gdn_norm_w=gdn_norm_w, conv_sc_w=conv_sc_w, w_branch=w_branch, w_out=w_out, norm_mix_post=norm_mix_post,
             norm_ffn_pre=norm_ffn_pre, w_ff1=w_ff1, w_ff2=w_ff2, norm_ffn_post=norm_ffn_post)
    m = dict(norm_mix_pre=m_norm_mix_pre, w_in=m_w_in, conv_qkv_w=m_conv_qkv_w, gdn_a_log=m_gdn_a_log,
             gdn_dt_bias=m_gdn_dt_bias, gdn_norm_w=m_gdn_norm_w, conv_sc_w=m_conv_sc_w, w_branch=m_w_branch, w_out=m_w_out,
             norm_mix_post=m_norm_mix_post, norm_ffn_pre=m_norm_ffn_pre, w_ff1=m_w_ff1, w_ff2=m_w_ff2,
             norm_ffn_post=m_norm_ffn_post)
    v = dict(norm_mix_pre=v_norm_mix_pre, w_in=v_w_in, conv_qkv_w=v_conv_qkv_w, gdn_a_log=v_gdn_a_log,
             gdn_dt_bias=v_gdn_dt_bias, gdn_norm_w=v_gdn_norm_w, conv_sc_w=v_conv_sc_w, w_branch=v_w_branch, w_out=v_w_out,
             norm_mix_post=v_norm_mix_post, norm_ffn_pre=v_norm_ffn_pre, w_ff1=v_w_ff1, w_ff2=v_w_ff2,
             norm_ffn_post=v_norm_ffn_post)

    names = BIG + ("conv_qkv_w", "conv_sc_w")
    shards = [_cast_bf16(_as2d(w[k]), name=f"cast_{k}").reshape(w[k].shape) for k in BIG] + [conv_qkv_w, conv_sc_w]
    gathered = dict(zip(names, _gather_chips(shards, name="gather_weights")))
    layers = _full_weights(gathered, w)

    loss, dx, grads = _local_step(x[0], loss_target[0], layers)
    loss = lax.psum(loss, ("x", "y", "c"))

    parts = _partials_by_chip(grads)
    small, layout = _pack_small(grads)
    recv, small_all = _scatter_partials([parts[k] for k in BIG], small, name="scatter_grads")
    mine = [_sum_slots(r.reshape(N_CHIPS, -1, r.shape[-1]), name=f"sum_{k}") for k, r in zip(BIG, recv)]
    theirs = _swap_sibling(mine, name="swap_sibling")
    small_g = _unpack_small(_sum_slots(small_all, name="sum_small"), layout)
    xy = 2 * lax.axis_index("x") + lax.axis_index("y")
    for k, width in (("conv_qkv_w", 3 * BRANCH_W // N_CHIPS), ("conv_sc_w", BRANCH_W // N_CHIPS)):
        small_g[k] = lax.dynamic_slice_in_dim(small_g[k], xy * width, width, axis=2)

    out = {}
    for k, s_mine, s_theirs in zip(BIG, mine, theirs):
        res = _adamw(_as2d(w[k]), _as2d(m[k]), _as2d(v[k]), s_mine, s_theirs, name=f"adamw_{k}")
        out[k] = [r.reshape(w[k].shape) for r in res]
    for k in SMALL:
        res = _adamw(_as2d(w[k]), _as2d(m[k]), _as2d(v[k]), _as2d(small_g[k]), None, name=f"adamw_{k}")
        out[k] = [r.reshape(w[k].shape) for r in res]
    return (loss, dx[None], *[out[k][0] for k in ORDER], *[out[k][1] for k in ORDER], *[out[k][2] for k in ORDER],
            *[out[k][3] for k in ORDER])
```

```python
import functools

import jax
import jax.numpy as jnp
from jax import lax
from jax.experimental import pallas as pl
from jax.experimental.pallas import tpu as pltpu

F32 = jnp.float32
BF16 = jnp.bfloat16
MESH = pl.DeviceIdType.MESH

LANES = 128
D_MODEL = 1024
DEPTH = 4
CHUNK = 64
GDN_HEADS, GDN_DIM = 4, 128
SB_HEADS, SB_DIM = 8, 64
BRANCH_W = 512
N_BRANCH = 3
D_FF = 4 * D_MODEL
EPS = 1e-6
IN_W = 8200
AB_COL = 2048
AB_PAD = LANES - 8
IN_WP = IN_W + AB_PAD
N_CHIPS = 4
N_DEV = 8
GATES_COL = 5128
PB_GATES, PB_QKV, PB_GATE, PB_AB, PB_SB, PB_SCX, PB_SCB, PB_SCC = 0, 24, 36, 40, 41, 53, 57, 61
SB_TILE = 128
SB_GROUP = 4
SB_SCALE = SB_DIM ** -0.5
GDN_QSCALE = GDN_DIM ** -0.5
VMEM_LIMIT = 56 * 1024 * 1024

ADAM_LR, ADAM_B1, ADAM_B2, ADAM_EPS, ADAM_WD, ADAM_STEP = 0.001, 0.9, 0.999, 1e-08, 0.01, 10

NT = (((1,), (1,)), ((), ()))
TN = (((0,), (0,)), ((), ()))
HI = lax.Precision.HIGH


def _pad_in_cols(w):
    return jnp.concatenate([w[:, GATES_COL:], w[:, :AB_COL + 8], jnp.zeros((w.shape[0], AB_PAD), w.dtype),
                            w[:, AB_COL + 8:GATES_COL]], axis=1)


def _unpad_in_cols(g):
    n_gates = IN_W - GATES_COL
    return jnp.concatenate([g[:, n_gates:n_gates + AB_COL + 8], g[:, n_gates + AB_COL + 8 + AB_PAD:], g[:, :n_gates]], axis=1)


def _params(*sem):
    return pltpu.CompilerParams(dimension_semantics=sem if sem else None, vmem_limit_bytes=VMEM_LIMIT)


def _sigmoid(x):
    return 1.0 / (1.0 + jnp.exp(-x))


def _softplus(x):
    return jnp.maximum(x, 0.0) + jnp.log(1.0 + jnp.exp(-jnp.abs(x)))


def _dot(a, b, dims=None, precision=None):
    if dims is None:
        return jnp.dot(a, b, preferred_element_type=F32, precision=precision)
    return lax.dot_general(a, b, dims, preferred_element_type=F32, precision=precision)


def _bdot(a, b, dims=None):
    return _dot(a.astype(BF16), b.astype(BF16), dims)


def _matmul(a, b, *, name, ta=False, tb=False, tm, tn, tk=None, outs=(F32,), epi=None, extras=()):
    if ta:
        kdim, m = a.shape
    else:
        m, kdim = a.shape
    if tb:
        n, kb = b.shape
    else:
        kb, n = b.shape
    assert kdim == kb, (a.shape, b.shape)
    tk = kdim if tk is None else tk
    assert m % tm == 0 and n % tn == 0 and kdim % tk == 0, (m, n, kdim, tm, tn, tk)
    nk = kdim // tk
    a_spec = pl.BlockSpec((tk, tm), lambda i, j, k: (k, i)) if ta else pl.BlockSpec((tm, tk), lambda i, j, k: (i, k))
    b_spec = pl.BlockSpec((tn, tk), lambda i, j, k: (j, k)) if tb else pl.BlockSpec((tk, tn), lambda i, j, k: (k, j))
    mn_spec = pl.BlockSpec((tm, tn), lambda i, j, k: (i, j))
    dims = (((0 if ta else 1,), (1 if tb else 0,)), ((), ()))
    n_ex, n_out = len(extras), len(outs)

    def body(a_ref, b_ref, *rest):
        ex, o, acc = rest[:n_ex], rest[n_ex:n_ex + n_out], rest[n_ex + n_out:]
        part = lax.dot_general(a_ref[...].astype(BF16), b_ref[...].astype(BF16), dims, preferred_element_type=F32)

        def finish(val):
            res = epi(val, *[e[...] for e in ex]) if epi is not None else (val,)
            for r, oref in zip(res, o):
                oref[...] = r.astype(oref.dtype)

        if nk == 1:
            finish(part)
        else:
            k = pl.program_id(2)

            @pl.when(k == 0)
            def _():
                acc[0][...] = part

            @pl.when(k > 0)
            def _():
                acc[0][...] += part

            @pl.when(k == nk - 1)
            def _():
                finish(acc[0][...])

    res = pl.pallas_call(
        body, name=name, grid=(m // tm, n // tn, nk),
        in_specs=[a_spec, b_spec] + [mn_spec] * n_ex,
        out_specs=[mn_spec] * n_out,
        out_shape=[jax.ShapeDtypeStruct((m, n), dt) for dt in outs],
        scratch_shapes=[pltpu.VMEM((tm, tn), F32)] if nk > 1 else [],
        compiler_params=_params("parallel", "parallel", "arbitrary"),
    )(a, b, *extras)
    return res[0] if n_out == 1 else res


ROW_TILE = 512


def _norm_fwd(x, w, *, name):
    s, d = x.shape

    def body(x_ref, w_ref, o_ref):
        xv = x_ref[...]
        r = lax.rsqrt(jnp.mean(xv * xv, axis=-1, keepdims=True) + EPS)
        o_ref[...] = (xv * r * w_ref[...]).astype(o_ref.dtype)

    return pl.pallas_call(
        body, name=name, grid=(s // ROW_TILE,),
        in_specs=[pl.BlockSpec((ROW_TILE, d), lambda i: (i, 0)), pl.BlockSpec((1, d), lambda i: (0, 0))],
        out_specs=pl.BlockSpec((ROW_TILE, d), lambda i: (i, 0)),
        out_shape=jax.ShapeDtypeStruct((s, d), BF16), compiler_params=_params("parallel"),
    )(x, w.reshape(1, d))


def _resnorm_fwd(x, u, w, *, name):
    s, d = x.shape

    def body(x_ref, u_ref, w_ref, o_ref):
        uv = u_ref[...]
        r = lax.rsqrt(jnp.mean(uv * uv, axis=-1, keepdims=True) + EPS)
        o_ref[...] = x_ref[...] + uv * r * w_ref[...]

    row = pl.BlockSpec((ROW_TILE, d), lambda i: (i, 0))
    return pl.pallas_call(
        body, name=name, grid=(s // ROW_TILE,),
        in_specs=[row, row, pl.BlockSpec((1, d), lambda i: (0, 0))], out_specs=row,
        out_shape=jax.ShapeDtypeStruct((s, d), F32), compiler_params=_params("parallel"),
    )(x, u, w.reshape(1, d))


def _norm_bwd(xin, w, dy, res, *, out_dtype, name):
    s, d = xin.shape
    has_res = res is not None

    def body(*refs):
        x_ref, w_ref, dy_ref = refs[:3]
        res_ref = refs[3] if has_res else None
        dx_ref, dw_ref = refs[3 + has_res:]
        xv, dyv = x_ref[...], dy_ref[...].astype(F32)
        r = lax.rsqrt(jnp.mean(xv * xv, axis=-1, keepdims=True) + EPS)
        xh = xv * r
        g = dyv * w_ref[...]
        dx = r * (g - xh * jnp.mean(g * xh, axis=-1, keepdims=True))
        if has_res:
            dx = dx + res_ref[...]
        dx_ref[...] = dx.astype(dx_ref.dtype)

        @pl.when(pl.program_id(0) == 0)
        def _():
            dw_ref[...] = jnp.zeros_like(dw_ref)

        dw_ref[...] += jnp.sum(dyv * xh, axis=0, keepdims=True)

    row = pl.BlockSpec((ROW_TILE, d), lambda i: (i, 0))
    vec = pl.BlockSpec((1, d), lambda i: (0, 0))
    ins = [xin, w.reshape(1, d), dy] + ([res] if has_res else [])
    dx, dw = pl.pallas_call(
        body, name=name, grid=(s // ROW_TILE,),
        in_specs=[row, vec, row] + ([row] if has_res else []), out_specs=[row, vec],
        out_shape=[jax.ShapeDtypeStruct((s, d), out_dtype), jax.ShapeDtypeStruct((1, d), F32)],
        compiler_params=_params("arbitrary"),
    )(*ins)
    return dx, dw.reshape(d)


def _loss_fwd_bwd(y, target, *, name):
    s, d = y.shape

    def body(y_ref, t_ref, loss_ref, dy_ref):
        e = y_ref[...] - t_ref[...]
        dy_ref[...] = e * (1.0 / d)

        @pl.when(pl.program_id(0) == 0)
        def _():
            loss_ref[...] = jnp.zeros_like(loss_ref)

        part = jnp.sum(jnp.sum(e * e, axis=1, keepdims=True), axis=0, keepdims=True)
        loss_ref[...] += part * (0.5 / d)

    row = pl.BlockSpec((ROW_TILE, d), lambda i: (i, 0))
    loss, dy = pl.pallas_call(
        body, name=name, grid=(s // ROW_TILE,), in_specs=[row, row],
        out_specs=[pl.BlockSpec((1, 1), lambda i: (0, 0)), row],
        out_shape=[jax.ShapeDtypeStruct((1, 1), F32), jax.ShapeDtypeStruct((s, d), F32)],
        compiler_params=_params("arbitrary"),
    )(y, target)
    return loss[0, 0], dy


def _shift_down(x, k, rows):
    if k == 0:
        return x
    return jnp.where(rows >= k, pltpu.roll(x, k, 0), 0.0)


def _shift_up(x, k, rows):
    if k == 0:
        return x
    n = x.shape[0]
    return jnp.where(rows < n - k, pltpu.roll(x, n - k, 0), 0.0)


def _col_spec(s, base):
    return pl.BlockSpec((s, LANES), lambda j: (0, base + j))


def _gdn_pre_math(x, w, j, rows):
    taps = w.shape[0]
    c = w[taps - 1:taps, :] * x
    for i in range(taps - 1):
        c = c + w[i:i + 1, :] * _shift_down(x, taps - 1 - i, rows)
    sg = _sigmoid(c)
    y = c * sg
    r = lax.rsqrt(jnp.sum(y * y, axis=-1, keepdims=True) + EPS)
    is_qk = j < 2 * GDN_HEADS
    scale = jnp.where(j < GDN_HEADS, GDN_QSCALE, 1.0)
    return c, sg, y, r, is_qk, scale


def _gdn_pre_fwd(proj, conv_w, *, name):
    s = proj.shape[0]

    def body(x_ref, w_ref, o_ref):
        j = pl.program_id(0)
        rows = lax.broadcasted_iota(jnp.int32, (s, LANES), 0)
        _, _, y, r, is_qk, scale = _gdn_pre_math(x_ref[...], w_ref[...], j, rows)
        o_ref[...] = jnp.where(is_qk, y * (r * scale), y)

    return pl.pallas_call(
        body, name=name, grid=(12,),
        in_specs=[_col_spec(s, PB_QKV), pl.BlockSpec((4, LANES), lambda j: (0, j))],
        out_specs=_col_spec(s, 0), out_shape=jax.ShapeDtypeStruct((s, 3 * BRANCH_W), F32),
        compiler_params=_params("parallel"),
    )(proj, conv_w)


def _gdn_pre_bwd(proj, conv_w, dqkvn, *, name):
    s = proj.shape[0]

    def body(x_ref, w_ref, d_ref, dx_ref, dw_ref):
        j = pl.program_id(0)
        rows = lax.broadcasted_iota(jnp.int32, (s, LANES), 0)
        x, w, dout = x_ref[...], w_ref[...], d_ref[...]
        c, sg, y, r, is_qk, scale = _gdn_pre_math(x, w, j, rows)
        yh = y * r
        dy_n = (scale * r) * (dout - yh * jnp.sum(dout * yh, axis=-1, keepdims=True))
        dy = jnp.where(is_qk, dy_n, dout)
        dc = dy * (sg * (1.0 + c * (1.0 - sg)))
        taps = w.shape[0]
        dx = w[taps - 1:taps, :] * dc
        dws = []
        for i in range(taps - 1):
            k = taps - 1 - i
            dx = dx + w[i:i + 1, :] * _shift_up(dc, k, rows)
            dws.append(jnp.sum(dc * _shift_down(x, k, rows), axis=0, keepdims=True))
        dws.append(jnp.sum(dc * x, axis=0, keepdims=True))
        dx_ref[...] = dx.astype(dx_ref.dtype)
        for i in range(taps):
            dw_ref[i:i + 1, :] = dws[i]

    return pl.pallas_call(
        body, name=name, grid=(12,),
        in_specs=[_col_spec(s, PB_QKV), pl.BlockSpec((4, LANES), lambda j: (0, j)), _col_spec(s, 0)],
        out_specs=[_col_spec(s, 0), pl.BlockSpec((4, LANES), lambda j: (0, j))],
        out_shape=[jax.ShapeDtypeStruct((s, 3 * BRANCH_W), BF16), jax.ShapeDtypeStruct((4, 3 * BRANCH_W), F32)],
        compiler_params=_params("parallel"),
    )(proj, conv_w, dqkvn)


def _lane_pad(v):
    return jnp.pad(v.reshape(1, -1), ((0, 0), (0, LANES - v.shape[0])))


def _gdn_gates_fwd(proj, a_log, dt_bias, *, name):
    s = proj.shape[0]

    def body(ab_ref, al_ref, dt_ref, o_ref):
        ab = ab_ref[...]
        lane = lax.broadcasted_iota(jnp.int32, (1, LANES), 1)
        g = -jnp.exp(al_ref[...]) * _softplus(ab + dt_ref[...])
        o_ref[...] = jnp.where(lane < GDN_HEADS, g, _sigmoid(ab))

    vec = pl.BlockSpec((1, LANES), lambda j: (0, 0))
    return pl.pallas_call(
        body, name=name, grid=(1,), in_specs=[_col_spec(s, PB_AB), vec, vec], out_specs=_col_spec(s, 0),
        out_shape=jax.ShapeDtypeStruct((s, LANES), F32), compiler_params=_params("arbitrary"),
    )(proj, _lane_pad(a_log), _lane_pad(dt_bias))


def _gdn_gates_bwd(proj, a_log, dt_bias, dgb, *, name):
    s = proj.shape[0]

    def body(ab_ref, al_ref, dt_ref, d_ref, dab_ref, dal_ref, ddt_ref):
        ab, d = ab_ref[...], d_ref[...]
        lane = lax.broadcasted_iota(jnp.int32, (1, LANES), 1)
        ea = jnp.exp(al_ref[...])
        pre = ab + dt_ref[...]
        g = -ea * _softplus(pre)
        dpre = d * (-ea) * _sigmoid(pre)
        beta = _sigmoid(ab)
        is_g = lane < GDN_HEADS
        dab = jnp.where(is_g, dpre, jnp.where(lane < 2 * GDN_HEADS, d * beta * (1.0 - beta), 0.0))
        dab_ref[...] = dab.astype(dab_ref.dtype)
        dal_ref[...] = jnp.sum(jnp.where(is_g, d * g, 0.0), axis=0, keepdims=True)
        ddt_ref[...] = jnp.sum(jnp.where(is_g, dpre, 0.0), axis=0, keepdims=True)

    vec = pl.BlockSpec((1, LANES), lambda j: (0, 0))
    dab, dal, ddt = pl.pallas_call(
        body, name=name, grid=(1,), in_specs=[_col_spec(s, PB_AB), vec, vec, _col_spec(s, 0)],
        out_specs=[_col_spec(s, 0), vec, vec],
        out_shape=[jax.ShapeDtypeStruct((s, LANES), BF16), jax.ShapeDtypeStruct((1, LANES), F32),
                   jax.ShapeDtypeStruct((1, LANES), F32)],
        compiler_params=_params("arbitrary"),
    )(proj, _lane_pad(a_log), _lane_pad(dt_bias), dgb)
    return dab, dal[0, :GDN_HEADS], ddt[0, :GDN_HEADS]


def _interleave(gens):
    results, live = [None] * len(gens), list(range(len(gens)))
    while live:
        for idx in list(live):
            try:
                next(gens[idx])
            except StopIteration as done:
                results[idx] = done.value
                live.remove(idx)
    return results


def _chunk_common(q, k, v, gb, gbt, h):
    c = CHUNK
    row = lax.broadcasted_iota(jnp.int32, (c, c), 0)
    col = lax.broadcasted_iota(jnp.int32, (c, c), 1)
    tril, strict, eye = row >= col, row > col, row == col
    lane = lax.broadcasted_iota(jnp.int32, (c, LANES), 1)
    sub = lax.broadcasted_iota(jnp.int32, (2 * GDN_HEADS, c), 0)
    g_col = jnp.sum(jnp.where(lane == h, gb, 0.0), axis=1, keepdims=True)
    beta_col = jnp.sum(jnp.where(lane == GDN_HEADS + h, gb, 0.0), axis=1, keepdims=True)
    g_row = jnp.sum(jnp.where(sub == h, gbt, 0.0), axis=0, keepdims=True)
    gc_col = jnp.sum(jnp.where(tril, jnp.broadcast_to(g_row, (c, c)), 0.0), axis=1, keepdims=True)
    gc_row = jnp.sum(jnp.where(row <= col, jnp.broadcast_to(g_col, (c, c)), 0.0), axis=0, keepdims=True)
    g_tot = jnp.sum(g_row, axis=1, keepdims=True)
    dm = jnp.exp(jnp.where(tril, gc_col - gc_row, -1e30))
    e_col = jnp.exp(gc_col)
    kdec_col = jnp.exp(g_tot - gc_col)
    gamma = jnp.exp(g_tot)
    kb = k * beta_col
    vb = v * beta_col
    kbg = kb * e_col
    kk = _dot(kb, k, NT, HI)
    qk = _bdot(q, k, NT)
    yield
    a = jnp.where(strict, kk * dm, 0.0)
    aqk = jnp.where(tril, qk * dm, 0.0)
    bneg = -a
    t = jnp.where(eye, 1.0, 0.0) + bneg
    p = _dot(bneg, bneg, precision=HI)
    yield
    for lvl in range(5):
        t_next = t + _dot(t, p, precision=HI)
        if lvl < 4:
            p = _dot(p, p, precision=HI)
        t = t_next
        yield
    u = _dot(t, vb, precision=HI)
    w = _dot(t, kbg, precision=HI)
    yield
    return dict(tril=tril, strict=strict, eye=eye, row=row, col=col, beta_col=beta_col, dm=dm, e_col=e_col,
                kdec_col=kdec_col, gamma=gamma, kb=kb, vb=vb, kbg=kbg, a=a, t=t, u=u, w=w, aqk=aqk,
                qd=q * e_col, kd=k * kdec_col)


def _gdn_chunk_fwd(qkvn, gb, gbt, *, name):
    s = qkvn.shape[0]
    n_chunks = s // CHUNK

    def body(q_ref, k_ref, v_ref, gb_ref, gbt_ref, o_ref, st_ref, state):
        @pl.when(pl.program_id(0) == 0)
        def _():
            state[...] = jnp.zeros_like(state)

        gbv, gbtv = gb_ref[...], gbt_ref[0]

        def head(h):
            hs = slice(h * GDN_DIM, (h + 1) * GDN_DIM)
            q, k, v = q_ref[:, hs], k_ref[:, hs], v_ref[:, hs]
            m = yield from _chunk_common(q, k, v, gbv, gbtv, h)
            s0 = state[h]
            st_ref[0, h] = s0
            vnew = m["u"] - _bdot(m["w"], s0)
            o_inter = _bdot(m["qd"], s0)
            yield
            o_ref[:, hs] = o_inter + _bdot(m["aqk"], vnew)
            state[h] = m["gamma"] * s0 + _bdot(m["kd"], vnew, TN)

        _interleave([head(h) for h in range(GDN_HEADS)])

    blk = lambda j: pl.BlockSpec((CHUNK, BRANCH_W), lambda n: (n, j))
    return pl.pallas_call(
        body, name=name, grid=(n_chunks,),
        in_specs=[blk(0), blk(1), blk(2), pl.BlockSpec((CHUNK, LANES), lambda n: (n, 0)),
                  pl.BlockSpec((1, 2 * GDN_HEADS, CHUNK), lambda n: (n, 0, 0))],
        out_specs=[blk(0), pl.BlockSpec((1, GDN_HEADS, GDN_DIM, GDN_DIM), lambda n: (n, 0, 0, 0))],
        out_shape=[jax.ShapeDtypeStruct((s, BRANCH_W), F32),
                   jax.ShapeDtypeStruct((n_chunks, GDN_HEADS, GDN_DIM, GDN_DIM), F32)],
        scratch_shapes=[pltpu.VMEM((GDN_HEADS, GDN_DIM, GDN_DIM), F32)],
        compiler_params=_params("arbitrary"),
    )(qkvn, qkvn, qkvn, gb, gbt)


def _gdn_chunk_bwd(qkvn, gb, gbt, states, do, *, name):
    s = qkvn.shape[0]
    n_chunks = s // CHUNK
    c = CHUNK

    def body(q_ref, k_ref, v_ref, gb_ref, gbt_ref, st_ref, do_ref, dq_ref, dk_ref, dv_ref, dgb_ref, dstate):
        @pl.when(pl.program_id(0) == 0)
        def _():
            dstate[...] = jnp.zeros_like(dstate)

        gbv, gbtv = gb_ref[...], gbt_ref[0]
        lane = lax.broadcasted_iota(jnp.int32, (c, LANES), 1)
        def head(h):
            hs = slice(h * GDN_DIM, (h + 1) * GDN_DIM)
            q, k, v, dov = q_ref[:, hs], k_ref[:, hs], v_ref[:, hs], do_ref[:, hs]
            m = yield from _chunk_common(q, k, v, gbv, gbtv, h)
            tril, strict, eye, row, col = m["tril"], m["strict"], m["eye"], m["row"], m["col"]
            s0, ds1 = st_ref[0, h], dstate[h]
            vnew = m["u"] - _bdot(m["w"], s0)
            dvnew_a = _bdot(m["aqk"], dov, TN) + _bdot(m["kd"], ds1)
            dqd = _bdot(dov, s0, NT)
            ds_q = _bdot(m["qd"], dov, TN)
            dgamma = jnp.sum(jnp.sum(s0 * ds1, axis=1, keepdims=True), axis=0, keepdims=True)
            yield
            dvnew = dvnew_a
            daqk = jnp.where(tril, _bdot(dov, vnew, NT), 0.0)
            dkd = _bdot(vnew, ds1, NT)
            dw = -_bdot(dvnew, s0, NT)
            dstate[h] = m["gamma"] * ds1 + ds_q - _bdot(m["w"], dvnew, TN)
            dvb = _dot(m["t"], dvnew, TN, HI)
            yield
            dt = _dot(dvnew, m["vb"], NT, HI) + _dot(dw, m["kbg"], NT, HI)
            dkbg = _dot(m["t"], dw, TN, HI)
            dmq = daqk * m["dm"]
            dq = _bdot(dmq, k) + dqd * m["e_col"]
            dk_q = _bdot(dmq, q, TN)
            yield
            tdt = _dot(m["t"], dt, TN, HI)
            yield
            da = jnp.where(strict, -_dot(tdt, m["t"], NT, HI), 0.0)
            yield
            dmat = da * m["dm"]
            dkb = _dot(dmat, k, precision=HI) + dkbg * m["e_col"]
            dk = (_dot(dmat, m["kb"], TN, HI) + dk_q + dkd * m["kdec_col"] + m["beta_col"] * dkb)
            yield
            dbeta_col = jnp.sum(dkb * k, axis=1, keepdims=True) + jnp.sum(dvb * v, axis=1, keepdims=True)
            e = da * m["a"] + daqk * m["aqk"]
            rs_kd = jnp.sum(dkd * m["kd"], axis=1, keepdims=True)
            e_colsum = jnp.sum(e, axis=0, keepdims=True)
            e_colsum_c = jnp.sum(jnp.where(eye, jnp.broadcast_to(e_colsum, (c, c)), 0.0), axis=1, keepdims=True)
            dgc = (jnp.sum(e, axis=1, keepdims=True) - e_colsum_c + jnp.sum(dqd * m["qd"], axis=1, keepdims=True)
                   - rs_kd + jnp.sum(dkbg * m["kbg"], axis=1, keepdims=True))
            last = jnp.sum(rs_kd, axis=0, keepdims=True) + dgamma * m["gamma"]
            dgc = dgc + jnp.where(lax.broadcasted_iota(jnp.int32, (c, 1), 0) == c - 1, last, 0.0)
            dgc_row = jnp.sum(jnp.where(eye, jnp.broadcast_to(dgc, (c, c)), 0.0), axis=0, keepdims=True)
            dg_col = jnp.sum(jnp.where(col >= row, jnp.broadcast_to(dgc_row, (c, c)), 0.0), axis=1, keepdims=True)
            dq_ref[:, hs] = dq
            dk_ref[:, hs] = dk
            dv_ref[:, hs] = m["beta_col"] * dvb
            return jnp.where(lane == h, dg_col, 0.0) + jnp.where(lane == GDN_HEADS + h, dbeta_col, 0.0)

        parts = _interleave([head(h) for h in range(GDN_HEADS)])
        dgb_ref[...] = (parts[0] + parts[1]) + (parts[2] + parts[3])

    rev = lambda n: n_chunks - 1 - n
    blk = lambda j: pl.BlockSpec((CHUNK, BRANCH_W), lambda n: (rev(n), j))
    dq, dk, dv, dgb = pl.pallas_call(
        body, name=name, grid=(n_chunks,),
        in_specs=[blk(0), blk(1), blk(2), pl.BlockSpec((CHUNK, LANES), lambda n: (rev(n), 0)),
                  pl.BlockSpec((1, 2 * GDN_HEADS, CHUNK), lambda n: (rev(n), 0, 0)),
                  pl.BlockSpec((1, GDN_HEADS, GDN_DIM, GDN_DIM), lambda n: (rev(n), 0, 0, 0)), blk(0)],
        out_specs=[blk(0), blk(0), blk(0), pl.BlockSpec((CHUNK, LANES), lambda n: (rev(n), 0))],
        out_shape=[jax.ShapeDtypeStruct((s, BRANCH_W), F32)] * 3 + [jax.ShapeDtypeStruct((s, LANES), F32)],
        scratch_shapes=[pltpu.VMEM((GDN_HEADS, GDN_DIM, GDN_DIM), F32)],
        compiler_params=_params("arbitrary"),
    )(qkvn, qkvn, qkvn, gb, gbt, states, do)
    return jnp.concatenate([dq, dk, dv], axis=1), dgb


def _gdn_post_fwd(o, proj, norm_w, *, name):
    s = o.shape[0]

    def body(o_ref, g_ref, w_ref, y_ref):
        ov, gv = o_ref[...], g_ref[...]
        r = lax.rsqrt(jnp.mean(ov * ov, axis=-1, keepdims=True) + EPS)
        y_ref[...] = (ov * r * w_ref[...] * (gv * _sigmoid(gv))).astype(y_ref.dtype)

    return pl.pallas_call(
        body, name=name, grid=(GDN_HEADS,),
        in_specs=[_col_spec(s, 0), _col_spec(s, PB_GATE), pl.BlockSpec((1, LANES), lambda j: (0, 0))],
        out_specs=_col_spec(s, 0), out_shape=jax.ShapeDtypeStruct((s, BRANCH_W), BF16),
        compiler_params=_params("parallel"),
    )(o, proj, norm_w.reshape(1, GDN_DIM))


def _gdn_post_bwd(o, proj, norm_w, dy, *, name):
    s = o.shape[0]

    def body(o_ref, g_ref, w_ref, dy_ref, do_ref, dg_ref, dw_ref):
        ov, gv, w, dyv = o_ref[...], g_ref[...], w_ref[...], dy_ref[...].astype(F32)
        r = lax.rsqrt(jnp.mean(ov * ov, axis=-1, keepdims=True) + EPS)
        oh = ov * r
        sg = _sigmoid(gv)
        silu = gv * sg
        dn = dyv * silu
        dg_ref[...] = (dyv * (oh * w) * (sg * (1.0 + gv * (1.0 - sg)))).astype(dg_ref.dtype)

        @pl.when(pl.program_id(0) == 0)
        def _():
            dw_ref[...] = jnp.zeros_like(dw_ref)

        dw_ref[...] += jnp.sum(dn * oh, axis=0, keepdims=True)
        g2 = dn * w
        do_ref[...] = r * (g2 - oh * jnp.mean(g2 * oh, axis=-1, keepdims=True))

    do, dg, dw = pl.pallas_call(
        body, name=name, grid=(GDN_HEADS,),
        in_specs=[_col_spec(s, 0), _col_spec(s, PB_GATE), pl.BlockSpec((1, LANES), lambda j: (0, 0)), _col_spec(s, 0)],
        out_specs=[_col_spec(s, 0), _col_spec(s, 0), pl.BlockSpec((1, LANES), lambda j: (0, 0))],
        out_shape=[jax.ShapeDtypeStruct((s, BRANCH_W), F32), jax.ShapeDtypeStruct((s, BRANCH_W), BF16),
                   jax.ShapeDtypeStruct((1, LANES), F32)],
        compiler_params=_params("arbitrary"),
    )(o, proj, norm_w.reshape(1, GDN_DIM), dy)
    return do, dg, dw.reshape(LANES)


def _split_terms(x):
    hi = x.astype(BF16)
    lo = (x - hi.astype(F32)).astype(BF16)
    return jnp.concatenate([hi, lo], axis=1)


def _sb_sum_matrix(pred):
    row = lax.broadcasted_iota(jnp.int32, (2 * SB_TILE, 2 * SB_TILE), 0) % SB_TILE
    col = lax.broadcasted_iota(jnp.int32, (2 * SB_TILE, 2 * SB_TILE), 1)
    return jnp.where((col >= SB_TILE) | pred(row, col), 1.0, 0.0).astype(BF16)


def _sb_head_masks():
    lane = lax.broadcasted_iota(jnp.int32, (1, LANES), 1)
    return [(lane < SB_DIM).astype(F32), (lane >= SB_DIM).astype(F32)]


def _sb_fwd(proj, *, name):
    s = proj.shape[0]
    t = SB_TILE
    nq = s // t

    def body(q_ref, k_ref, v_ref, o_ref, tot_ref):
        cmr = lax.broadcasted_iota(jnp.int32, (t, t), 1) - lax.broadcasted_iota(jnp.int32, (t, t), 0)
        uo = _sb_sum_matrix(lambda row, col: row > col)
        hm = _sb_head_masks()

        def qloop(i, carry0):
            qs = pl.multiple_of(i * t, t)
            qf = q_ref[pl.ds(qs, t), :] * SB_SCALE
            qh = [(qf * hm[h]).astype(BF16) for h in range(2)]
            diag = i // SB_GROUP

            def group(g, st, masked):
                ks = pl.multiple_of(g * (SB_GROUP * t), SB_GROUP * t)
                kb = k_ref[pl.ds(ks, SB_GROUP * t), :].astype(BF16)
                vf = v_ref[pl.ds(ks, SB_GROUP * t), :]
                tiles = [(h, j) for h in range(2) for j in range(SB_GROUP)]
                z = [_dot(qh[h], kb, NT) for h in range(2)]
                keep = {j: cmr < (i - g * SB_GROUP - j) * t for j in range(SB_GROUP)} if masked else None
                base, terms = {}, {}
                for h, j in tiles:
                    zj = z[h][:, j * t:(j + 1) * t]
                    sp = _softplus(zj)
                    l1m = jnp.where(keep[j], -sp, 0.0) if masked else -sp
                    base[h, j] = zj - sp
                    terms[h, j] = _split_terms(l1m)
                sums = {hj: _dot(terms[hj], uo) for hj in tiles}
                acc, new = st[0], []
                for h in range(2):
                    run, att = st[1 + h], [None] * SB_GROUP
                    for j in reversed(range(SB_GROUP)):
                        a = jnp.exp(base[h, j] + sums[h, j][:, :t] + run)
                        att[j] = (jnp.where(keep[j], a, 0.0) if masked else a).astype(BF16)
                        run = run + sums[h, j][:, t:]
                    acc = acc + _dot(jnp.concatenate(att, axis=1), (vf * hm[h]).astype(BF16))
                    new.append(run)
                return (acc, *new)

            zero = jnp.zeros((t, LANES), F32)
            st = group(diag, (zero, zero, zero), True)
            st = lax.fori_loop(0, diag, lambda jj, sv: group(diag - 1 - jj, sv, False), st)
            o_ref[pl.ds(qs, t), :] = st[0]
            tot_ref[pl.ds(qs, t), :] = st[1] * hm[0] + st[2] * hm[1]
            return carry0

        lax.fori_loop(0, nq, qloop, 0)

    out = jax.ShapeDtypeStruct((s, BRANCH_W), F32)
    return pl.pallas_call(
        body, name=name, grid=(SB_HEADS // 2,),
        in_specs=[_col_spec(s, PB_SB), _col_spec(s, PB_SB + 4), _col_spec(s, PB_SB + 8)],
        out_specs=[_col_spec(s, 0)] * 2, out_shape=[out] * 2,
        compiler_params=_params("parallel"),
    )(proj, proj, proj)


def _sb_bwd(proj, tot, do, *, name):
    s = proj.shape[0]
    t = SB_TILE
    nq = s // t

    def body(q_ref, k_ref, v_ref, tot_ref, do_ref, dq_ref, dk_ref, dv_ref, dk_acc, dv_acc):
        dk_acc[...] = jnp.zeros_like(dk_acc)
        dv_acc[...] = jnp.zeros_like(dv_acc)
        cmr = lax.broadcasted_iota(jnp.int32, (t, t), 1) - lax.broadcasted_iota(jnp.int32, (t, t), 0)
        u_le = _sb_sum_matrix(lambda row, col: row <= col)
        u_lt = _sb_sum_matrix(lambda row, col: row < col)
        hm = _sb_head_masks()

        def qloop(i, carry0):
            qs = pl.multiple_of(i * t, t)
            qraw = q_ref[pl.ds(qs, t), :]
            dov = do_ref[pl.ds(qs, t), :].astype(F32)
            totv = tot_ref[pl.ds(qs, t), :]
            qh = [(qraw * (hm[h] * SB_SCALE)).astype(BF16) for h in range(2)]
            q2 = jnp.concatenate([(qraw * hm[h]).astype(BF16) for h in range(2)], axis=0)
            doh = [(dov * hm[h]).astype(BF16) for h in range(2)]
            do2 = jnp.concatenate(doh, axis=0)
            tot = [jnp.min(totv * hm[h], axis=1, keepdims=True) for h in range(2)]
            diag = i // SB_GROUP

            def group(g, st, masked):
                ks = pl.multiple_of(g * (SB_GROUP * t), SB_GROUP * t)
                kf = k_ref[pl.ds(ks, SB_GROUP * t), :]
                kb = kf.astype(BF16)
                vb = v_ref[pl.ds(ks, SB_GROUP * t), :].astype(BF16)
                tiles = [(h, j) for h in range(2) for j in range(SB_GROUP)]
                z = [_dot(qh[h], kb, NT) for h in range(2)]
                datt = [_dot(doh[h], vb, NT) for h in range(2)]
                keep = {j: cmr < (i - g * SB_GROUP - j) * t for j in range(SB_GROUP)} if masked else None
                ls, lterms = {}, {}
                for h, j in tiles:
                    zj = z[h][:, j * t:(j + 1) * t]
                    sp = _softplus(zj)
                    ls[h, j] = zj - sp
                    lterms[h, j] = _split_terms(jnp.where(keep[j], -sp, 0.0) if masked else -sp)
                lsum = {hj: _dot(lterms[hj], u_le) for hj in tiles}
                att, p, pterms, new_c = {}, {}, {}, []
                for h in range(2):
                    run = st[1 + h]
                    for j in range(SB_GROUP):
                        a = jnp.exp(ls[h, j] + ((tot[h] - run) - lsum[h, j][:, :t]))
                        if masked:
                            a = jnp.where(keep[j], a, 0.0)
                        att[h, j] = a.astype(BF16)
                        p[h, j] = a * datt[h][:, j * t:(j + 1) * t]
                        pterms[h, j] = _split_terms(p[h, j])
                        run = run + lsum[h, j][:, t:]
                    new_c.append(run)
                psum = {hj: _dot(pterms[hj], u_lt) for hj in tiles}
                dzb, new_r = {}, []
                for h in range(2):
                    run = st[3 + h]
                    for j in range(SB_GROUP):
                        sig = jnp.exp(ls[h, j])
                        dz = p[h, j] * (1.0 - sig) - (run + psum[h, j][:, :t]) * sig
                        if masked:
                            dz = jnp.where(keep[j], dz, 0.0)
                        dzb[h, j] = (dz * SB_SCALE).astype(BF16)
                        run = run + psum[h, j][:, t:]
                    new_r.append(run)
                k2 = jnp.concatenate([(kf * hm[h]).astype(BF16) for h in range(2)], axis=0)
                dq_acc = st[0] + _dot(jnp.concatenate([dzb[hj] for hj in tiles], axis=1), k2)
                for j in range(SB_GROUP):
                    rows = pl.ds(pl.multiple_of(ks + j * t, t), t)
                    dk_acc[rows, :] += _dot(jnp.concatenate([dzb[0, j], dzb[1, j]], axis=0), q2, TN)
                    dv_acc[rows, :] += _dot(jnp.concatenate([att[0, j], att[1, j]], axis=0), do2, TN)
                return (dq_acc, *new_c, *new_r)

            zero = jnp.zeros((t, LANES), F32)
            st = lax.fori_loop(0, diag, lambda jj, sv: group(jj, sv, False), (zero,) * 5)
            st = group(diag, st, True)
            dq_ref[pl.ds(qs, t), :] = st[0].astype(dq_ref.dtype)
            return carry0

        lax.fori_loop(0, nq, qloop, 0)
        dk_ref[...] = dk_acc[...].astype(dk_ref.dtype)
        dv_ref[...] = dv_acc[...].astype(dv_ref.dtype)

    out = jax.ShapeDtypeStruct((s, BRANCH_W), BF16)
    return pl.pallas_call(
        body, name=name, grid=(SB_HEADS // 2,),
        in_specs=[_col_spec(s, PB_SB), _col_spec(s, PB_SB + 4), _col_spec(s, PB_SB + 8), _col_spec(s, 0), _col_spec(s, 0)],
        out_specs=[_col_spec(s, 0)] * 3, out_shape=[out] * 3,
        scratch_shapes=[pltpu.VMEM((s, LANES), F32), pltpu.VMEM((s, LANES), F32)],
        compiler_params=_params("parallel"),
    )(proj, proj, proj, tot, do)


def _sc_fwd(proj, conv_w, *, name):
    s = proj.shape[0]

    def body(x_ref, b_ref, c_ref, w_ref, y_ref):
        rows = lax.broadcasted_iota(jnp.int32, (s, LANES), 0)
        w = w_ref[...]
        u = c_ref[...] * x_ref[...]
        cv = w[2:3, :] * u + w[1:2, :] * _shift_down(u, 1, rows) + w[0:1, :] * _shift_down(u, 2, rows)
        y_ref[...] = (b_ref[...] * cv).astype(y_ref.dtype)

    return pl.pallas_call(
        body, name=name, grid=(BRANCH_W // LANES,),
        in_specs=[_col_spec(s, PB_SCX), _col_spec(s, PB_SCB), _col_spec(s, PB_SCC), pl.BlockSpec((3, LANES), lambda j: (0, j))],
        out_specs=_col_spec(s, 0), out_shape=jax.ShapeDtypeStruct((s, BRANCH_W), BF16),
        compiler_params=_params("parallel"),
    )(proj, proj, proj, conv_w)


def _sc_bwd(proj, conv_w, dy, *, name):
    s = proj.shape[0]

    def body(x_ref, b_ref, c_ref, w_ref, dy_ref, dx_ref, db_ref, dc_ref, dw_ref):
        rows = lax.broadcasted_iota(jnp.int32, (s, LANES), 0)
        w, x, cg, dyv = w_ref[...], x_ref[...], c_ref[...], dy_ref[...].astype(F32)
        u = cg * x
        u1, u2 = _shift_down(u, 1, rows), _shift_down(u, 2, rows)
        cv = w[2:3, :] * u + w[1:2, :] * u1 + w[0:1, :] * u2
        db_ref[...] = (dyv * cv).astype(db_ref.dtype)
        dcv = dyv * b_ref[...]
        du = w[2:3, :] * dcv + w[1:2, :] * _shift_up(dcv, 1, rows) + w[0:1, :] * _shift_up(dcv, 2, rows)
        dx_ref[...] = (du * cg).astype(dx_ref.dtype)
        dc_ref[...] = (du * x).astype(dc_ref.dtype)
        dw_ref[0:1, :] = jnp.sum(dcv * u2, axis=0, keepdims=True)
        dw_ref[1:2, :] = jnp.sum(dcv * u1, axis=0, keepdims=True)
        dw_ref[2:3, :] = jnp.sum(dcv * u, axis=0, keepdims=True)

    out = jax.ShapeDtypeStruct((s, BRANCH_W), BF16)
    wspec = pl.BlockSpec((3, LANES), lambda j: (0, j))
    return pl.pallas_call(
        body, name=name, grid=(BRANCH_W // LANES,),
        in_specs=[_col_spec(s, PB_SCX), _col_spec(s, PB_SCB), _col_spec(s, PB_SCC), wspec, _col_spec(s, 0)],
        out_specs=[_col_spec(s, 0)] * 3 + [wspec],
        out_shape=[out] * 3 + [jax.ShapeDtypeStruct((3, BRANCH_W), F32)],
        compiler_params=_params("parallel"),
    )(proj, proj, proj, conv_w, dy)


MERGE_TM, MERGE_TN = 512, 512


def _merge_specs():
    tm, tn = MERGE_TM, MERGE_TN
    y_spec = pl.BlockSpec((tm, BRANCH_W), lambda i, j: (i, 0))
    w_spec = pl.BlockSpec((N_BRANCH, BRANCH_W, tn), lambda i, j: (0, 0, j))
    gate_specs = [pl.BlockSpec((tm, tn), functools.partial(
        lambda i, j, b: (i, (PB_GATES * LANES + b * D_MODEL) // tn + j), b=b)) for b in range(N_BRANCH)]
    mn = pl.BlockSpec((tm, tn), lambda i, j: (i, j))
    return y_spec, w_spec, gate_specs, mn


def _merge_fwd(ya, yb, yc, wb, proj, *, name):
    s = ya.shape[0]
    y_spec, w_spec, gate_specs, mn = _merge_specs()

    def body(ya_ref, yb_ref, yc_ref, w_ref, g0, g1, g2, o_ref):
        acc = None
        for b, (y_ref, g_ref) in enumerate(zip((ya_ref, yb_ref, yc_ref), (g0, g1, g2))):
            term = _sigmoid(g_ref[...]) * _bdot(y_ref[...], w_ref[b])
            acc = term if acc is None else acc + term
        o_ref[...] = acc.astype(o_ref.dtype)

    return pl.pallas_call(
        body, name=name, grid=(s // MERGE_TM, D_MODEL // MERGE_TN),
        in_specs=[y_spec] * 3 + [w_spec] + gate_specs, out_specs=mn,
        out_shape=jax.ShapeDtypeStruct((s, D_MODEL), BF16), compiler_params=_params("parallel", "parallel"),
    )(ya, yb, yc, wb, proj, proj, proj)


def _merge_bwd(ya, yb, yc, wb, proj, dm, *, name):
    s = ya.shape[0]
    y_spec, w_spec, gate_specs, mn = _merge_specs()

    def body(ya_ref, yb_ref, yc_ref, w_ref, g0, g1, g2, dm_ref, *outs):
        dmv = dm_ref[...].astype(F32)
        for b, (y_ref, g_ref) in enumerate(zip((ya_ref, yb_ref, yc_ref), (g0, g1, g2))):
            sg = _sigmoid(g_ref[...])
            z = _bdot(y_ref[...], w_ref[b])
            outs[b][...] = (dmv * sg).astype(BF16)
            outs[N_BRANCH + b][...] = (dmv * z * sg * (1.0 - sg)).astype(BF16)

    out = jax.ShapeDtypeStruct((s, D_MODEL), BF16)
    res = pl.pallas_call(
        body, name=name, grid=(s // MERGE_TM, D_MODEL // MERGE_TN),
        in_specs=[y_spec] * 3 + [w_spec] + gate_specs + [mn], out_specs=[mn] * (2 * N_BRANCH),
        out_shape=[out] * (2 * N_BRANCH), compiler_params=_params("parallel", "parallel"),
    )(ya, yb, yc, wb, proj, proj, proj, dm)
    return res[:N_BRANCH], res[N_BRANCH:]


def _chunk_rows(v, s):
    return v[:, :2 * GDN_HEADS].reshape(s // CHUNK, CHUNK, 2 * GDN_HEADS).transpose(0, 2, 1)


def _relu2_epi(acc):
    r = jnp.maximum(acc, 0.0)
    return acc, r * r


def _drelu2_epi(acc, a):
    return (acc * (2.0 * jnp.maximum(a.astype(F32), 0.0)),)


def _layer_fwd(x0, p):
    s = x0.shape[0]
    h1 = _norm_fwd(x0, p["norm_mix_pre"], name="norm_mix_pre")
    proj = _matmul(h1, p["w_in"], name="proj_in", tm=512, tn=1664)
    qkvn = _gdn_pre_fwd(proj, p["conv_qkv_w"], name="gdn_pre")
    gb = _gdn_gates_fwd(proj, p["gdn_a_log"], p["gdn_dt_bias"], name="gdn_gates")
    gbt = _chunk_rows(gb, s)
    o_gdn, states = _gdn_chunk_fwd(qkvn, gb, gbt, name="gdn_chunk")
    ya = _gdn_post_fwd(o_gdn, proj, p["gdn_norm_w"], name="gdn_post")
    o_sb, sb_tot = _sb_fwd(proj, name="sb_attn")
    yc = _sc_fwd(proj, p["conv_sc_w"], name="short_conv")
    merged = _merge_fwd(ya, o_sb, yc, p["w_branch"], proj, name="merge")
    u = _matmul(merged, p["w_out"], name="proj_out", tm=512, tn=1024)
    x1 = _resnorm_fwd(x0, u, p["norm_mix_post"], name="norm_mix_post")
    h2 = _norm_fwd(x1, p["norm_ffn_pre"], name="norm_ffn_pre")
    a, r = _matmul(h2, p["w_ff1"], name="ff1", tm=512, tn=1024, outs=(BF16, BF16), epi=_relu2_epi)
    f = _matmul(r, p["w_ff2"], name="ff2", tm=512, tn=1024, tk=1024)
    x2 = _resnorm_fwd(x1, f, p["norm_ffn_post"], name="norm_ffn_post")
    saved = dict(x0=x0, h1=h1, proj=proj, qkvn=qkvn, gb=gb, gbt=gbt, o_gdn=o_gdn, states=states, ya=ya, o_sb=o_sb,
                 sb_tot=sb_tot, yc=yc, merged=merged, u=u, x1=x1, h2=h2, a=a, r=r, f=f)
    return x2, saved


def _layer_bwd(dx2, p, sv):
    g = {}
    df, g["norm_ffn_post"] = _norm_bwd(sv["f"], p["norm_ffn_post"], dx2, None, out_dtype=BF16, name="norm_ffn_post_bwd")
    g["w_ff2"] = _matmul(sv["r"], df, ta=True, name="ff2_dw", tm=1024, tn=1024, tk=512)
    da = _matmul(df, p["w_ff2"], tb=True, name="ff2_dx", tm=512, tn=1024, outs=(BF16,), epi=_drelu2_epi,
                 extras=(sv["a"],))
    g["w_ff1"] = _matmul(sv["h2"], da, ta=True, name="ff1_dw", tm=1024, tn=1024, tk=512)
    dh2 = _matmul(da, p["w_ff1"], tb=True, name="ff1_dx", tm=512, tn=1024, tk=1024)
    dx1, g["norm_ffn_pre"] = _norm_bwd(sv["x1"], p["norm_ffn_pre"], dh2, dx2, out_dtype=F32, name="norm_ffn_pre_bwd")
    du, g["norm_mix_post"] = _norm_bwd(sv["u"], p["norm_mix_post"], dx1, None, out_dtype=BF16, name="norm_mix_post_bwd")
    g["w_out"] = _matmul(sv["merged"], du, ta=True, name="out_dw", tm=1024, tn=1024, tk=512)
    dmerged = _matmul(du, p["w_out"], tb=True, name="out_dx", tm=512, tn=1024, outs=(BF16,))
    ys = (sv["ya"], sv["o_sb"], sv["yc"])
    dz, dgates = _merge_bwd(*ys, p["w_branch"], sv["proj"], dmerged, name="merge_bwd")
    g["w_branch"] = jnp.stack([_matmul(ys[b], dz[b], ta=True, name=f"branch_dw{b}", tm=512, tn=1024, tk=512)
                               for b in range(N_BRANCH)])
    dys = [_matmul(dz[b], p["w_branch"][b], tb=True, name=f"branch_dx{b}", tm=512, tn=512) for b in range(N_BRANCH)]
    dscx, dscb, dscc, g["conv_sc_w"] = _sc_bwd(sv["proj"], p["conv_sc_w"], dys[2], name="short_conv_bwd")
    dsq, dsk, dsv = _sb_bwd(sv["proj"], sv["sb_tot"], dys[1], name="sb_attn_bwd")
    do_gdn, dgate, dnw = _gdn_post_bwd(sv["o_gdn"], sv["proj"], p["gdn_norm_w"], dys[0], name="gdn_post_bwd")
    g["gdn_norm_w"] = dnw
    dqkvn, dgb = _gdn_chunk_bwd(sv["qkvn"], sv["gb"], sv["gbt"], sv["states"], do_gdn, name="gdn_chunk_bwd")
    dqkv, g["conv_qkv_w"] = _gdn_pre_bwd(sv["proj"], p["conv_qkv_w"], dqkvn, name="gdn_pre_bwd")
    dab, g["gdn_a_log"], g["gdn_dt_bias"] = _gdn_gates_bwd(sv["proj"], p["gdn_a_log"], p["gdn_dt_bias"], dgb,
                                                           name="gdn_gates_bwd")
    dproj = jnp.concatenate([*dgates, dqkv, dgate, dab, dsq, dsk, dsv, dscx, dscb, dscc], axis=1)
    g["w_in"] = _matmul(sv["h1"], dproj, ta=True, name="in_dw", tm=1024, tn=1664, tk=512)
    dh1 = _matmul(dproj, p["w_in"], tb=True, name="in_dx", tm=512, tn=1024, tk=1664)
    dx0, g["norm_mix_pre"] = _norm_bwd(sv["x0"], p["norm_mix_pre"], dh1, dx1, out_dtype=F32, name="norm_mix_pre_bwd")
    return dx0, g


def _local_step(x, target, layers):
    saved = []
    h = x
    for p in layers:
        h, sv = _layer_fwd(h, p)
        saved.append(sv)
    loss, dh = _loss_fwd_bwd(h, target, name="loss")
    grads = [None] * len(layers)
    for l in reversed(range(len(layers))):
        dh, grads[l] = _layer_bwd(dh, layers[l], saved[l])
    return loss, dh, grads


ANY = pl.BlockSpec(memory_space=pl.ANY)


def _me_and_chips():
    x, y, c = lax.axis_index("x"), lax.axis_index("y"), lax.axis_index("c")
    chips = [(1 - x, y), (x, 1 - y), (1 - x, 1 - y)]
    return x, y, c, chips


def _gather_chips(arrs, *, name):
    n = len(arrs)

    def body(*refs):
        ins, outs = refs[:n], refs[n:2 * n]
        send_sems, recv_sems, local_sems = refs[2 * n:]
        x, y, c, chips = _me_and_chips()
        me = 2 * x + y
        sends, locals_ = [], []
        for a in range(n):
            lc = pltpu.make_async_copy(ins[a], outs[a].at[me], local_sems.at[a])
            lc.start()
            locals_.append(lc)
            for k, (px, py) in enumerate(chips):
                cp = pltpu.make_async_remote_copy(src_ref=ins[a], dst_ref=outs[a].at[me], send_sem=send_sems.at[a, k],
                                                  recv_sem=recv_sems.at[a, k], device_id=(px, py, c), device_id_type=MESH)
                cp.start()
                sends.append(cp)
        for a in range(n):
            for k, (px, py) in enumerate(chips):
                pltpu.make_async_remote_copy(src_ref=ins[a], dst_ref=outs[a].at[2 * px + py], send_sem=send_sems.at[a, k],
                                             recv_sem=recv_sems.at[a, k], device_id=(px, py, c),
                                             device_id_type=MESH).wait_recv()
        for cp in sends:
            cp.wait_send()
        for lc in locals_:
            lc.wait()

    return pl.pallas_call(
        body, name=name, in_specs=[ANY] * n, out_specs=[ANY] * n,
        out_shape=[jax.ShapeDtypeStruct((N_CHIPS,) + a.shape, a.dtype) for a in arrs],
        scratch_shapes=[pltpu.SemaphoreType.DMA((n, 3)), pltpu.SemaphoreType.DMA((n, 3)), pltpu.SemaphoreType.DMA((n,))],
    )(*arrs)


def _scatter_partials(arrs, small, *, name):
    n = len(arrs)

    def body(*refs):
        ins, small_ref = refs[:n], refs[n]
        outs, small_out = refs[n + 1:2 * n + 1], refs[2 * n + 1]
        send_sems, recv_sems, local_sems, ssend, srecv = refs[2 * n + 2:]
        x, y, c, chips = _me_and_chips()
        me = 2 * x + y
        sends, locals_ = [], []
        for a in range(n):
            lc = pltpu.make_async_copy(ins[a].at[me], outs[a].at[me], local_sems.at[a])
            lc.start()
            locals_.append(lc)
            for k, (px, py) in enumerate(chips):
                cp = pltpu.make_async_remote_copy(src_ref=ins[a].at[2 * px + py], dst_ref=outs[a].at[me],
                                                  send_sem=send_sems.at[a, k], recv_sem=recv_sems.at[a, k],
                                                  device_id=(px, py, c), device_id_type=MESH)
                cp.start()
                sends.append(cp)
        dev = 4 * x + 2 * y + c
        lc = pltpu.make_async_copy(small_ref, small_out.at[dev], local_sems.at[n])
        lc.start()
        locals_.append(lc)
        peers = [(x, y, 1 - c)] + [(px, py, pc) for (px, py) in chips for pc in (c, 1 - c)]
        for k, peer in enumerate(peers):
            cp = pltpu.make_async_remote_copy(src_ref=small_ref, dst_ref=small_out.at[dev], send_sem=ssend.at[k],
                                              recv_sem=srecv.at[k], device_id=peer, device_id_type=MESH)
            cp.start()
            sends.append(cp)
        for a in range(n):
            for k, (px, py) in enumerate(chips):
                pltpu.make_async_remote_copy(src_ref=ins[a].at[me], dst_ref=outs[a].at[2 * px + py],
                                             send_sem=send_sems.at[a, k], recv_sem=recv_sems.at[a, k],
                                             device_id=(px, py, c), device_id_type=MESH).wait_recv()
        for k, (px, py, pc) in enumerate(peers):
            pltpu.make_async_remote_copy(src_ref=small_ref, dst_ref=small_out.at[4 * px + 2 * py + pc], send_sem=ssend.at[k],
                                         recv_sem=srecv.at[k], device_id=(px, py, pc), device_id_type=MESH).wait_recv()
        for cp in sends:
            cp.wait_send()
        for lc in locals_:
            lc.wait()

    res = pl.pallas_call(
        body, name=name, in_specs=[ANY] * (n + 1), out_specs=[ANY] * (n + 1),
        out_shape=[jax.ShapeDtypeStruct(a.shape, a.dtype) for a in arrs]
        + [jax.ShapeDtypeStruct((N_DEV,) + small.shape, small.dtype)],
        scratch_shapes=[pltpu.SemaphoreType.DMA((n, 3)), pltpu.SemaphoreType.DMA((n, 3)), pltpu.SemaphoreType.DMA((n + 1,)),
                        pltpu.SemaphoreType.DMA((N_DEV - 1,)), pltpu.SemaphoreType.DMA((N_DEV - 1,))],
    )(*arrs, small)
    return res[:n], res[n]


def _swap_sibling(arrs, *, name):
    n = len(arrs)

    def body(*refs):
        ins, outs = refs[:n], refs[n:2 * n]
        send_sems, recv_sems = refs[2 * n:]
        x, y, c = lax.axis_index("x"), lax.axis_index("y"), lax.axis_index("c")
        cps = [pltpu.make_async_remote_copy(src_ref=ins[a], dst_ref=outs[a], send_sem=send_sems.at[a],
                                            recv_sem=recv_sems.at[a], device_id=(x, y, 1 - c), device_id_type=MESH)
               for a in range(n)]
        for cp in cps:
            cp.start()
        for cp in cps:
            cp.wait()

    return pl.pallas_call(
        body, name=name, in_specs=[ANY] * n, out_specs=[ANY] * n,
        out_shape=[jax.ShapeDtypeStruct(a.shape, a.dtype) for a in arrs],
        scratch_shapes=[pltpu.SemaphoreType.DMA((n,)), pltpu.SemaphoreType.DMA((n,))],
    )(*arrs)


def _row_tile(rows, cols, budget=2 * 1024 * 1024):
    best = None
    for t in range(16, rows + 1, 16):
        if rows % t == 0 and t * cols * 4 <= budget:
            best = t
    return best if best is not None else rows


def _sum_slots(parts, *, name):
    n, rows, cols = parts.shape
    tr = _row_tile(rows, cols, 1024 * 1024)

    def body(p_ref, o_ref):
        acc = p_ref[0].astype(F32)
        for i in range(1, n):
            acc = acc + p_ref[i].astype(F32)
        o_ref[...] = acc

    return pl.pallas_call(
        body, name=name, grid=(rows // tr,), in_specs=[pl.BlockSpec((n, tr, cols), lambda i: (0, i, 0))],
        out_specs=pl.BlockSpec((tr, cols), lambda i: (i, 0)), out_shape=jax.ShapeDtypeStruct((rows, cols), F32),
        compiler_params=_params("parallel"),
    )(parts)


def _adamw(w, m, v, g_a, g_b, *, name):
    rows, cols = w.shape
    tr = _row_tile(rows, cols, 1024 * 1024)
    two = g_b is not None
    c1 = 1.0 / (1.0 - ADAM_B1 ** ADAM_STEP)
    c2 = 1.0 / (1.0 - ADAM_B2 ** ADAM_STEP)

    def body(*refs):
        w_ref, m_ref, v_ref, ga_ref = refs[:4]
        g_ref, d_ref, nm_ref, nv_ref = refs[4 + two:]
        g = ga_ref[...]
        if two:
            g = g + refs[4][...]
        nm = ADAM_B1 * m_ref[...] + (1.0 - ADAM_B1) * g
        nv = ADAM_B2 * v_ref[...] + (1.0 - ADAM_B2) * (g * g)
        g_ref[...] = g
        nm_ref[...] = nm
        nv_ref[...] = nv
        d_ref[...] = -ADAM_LR * ((nm * c1) / (jnp.sqrt(nv * c2) + ADAM_EPS) + ADAM_WD * w_ref[...])

    blk = pl.BlockSpec((tr, cols), lambda i: (i, 0))
    ins = [w, m, v, g_a] + ([g_b] if two else [])
    return pl.pallas_call(
        body, name=name, grid=(rows // tr,), in_specs=[blk] * len(ins), out_specs=[blk] * 4,
        out_shape=[jax.ShapeDtypeStruct((rows, cols), F32)] * 4, compiler_params=_params("parallel"),
    )(*ins)


def _cast_bf16(w, *, name):
    rows, cols = w.shape
    tr = _row_tile(rows, cols)

    def body(w_ref, o_ref):
        o_ref[...] = w_ref[...].astype(BF16)

    blk = pl.BlockSpec((tr, cols), lambda i: (i, 0))
    return pl.pallas_call(body, name=name, grid=(rows // tr,), in_specs=[blk], out_specs=blk,
                          out_shape=jax.ShapeDtypeStruct((rows, cols), BF16), compiler_params=_params("parallel"))(w)


BIG = ("w_in", "w_branch", "w_out", "w_ff1", "w_ff2")
SMALL = ("norm_mix_pre", "conv_qkv_w", "gdn_a_log", "gdn_dt_bias", "gdn_norm_w", "conv_sc_w", "norm_mix_post",
         "norm_ffn_pre", "norm_ffn_post")
ORDER = ("norm_mix_pre", "w_in", "conv_qkv_w", "gdn_a_log", "gdn_dt_bias", "gdn_norm_w", "conv_sc_w", "w_branch",
         "w_out", "norm_mix_post", "norm_ffn_pre", "w_ff1", "w_ff2", "norm_ffn_post")


def _full_weights(gathered, rep):
    layers = []
    for l in range(DEPTH):
        p = dict(
            w_in=_pad_in_cols(gathered["w_in"][:, l].transpose(1, 0, 2).reshape(D_MODEL, IN_W)),
            w_branch=gathered["w_branch"][:, l].transpose(1, 2, 0, 3).reshape(N_BRANCH, BRANCH_W, D_MODEL),
            w_out=gathered["w_out"][:, l].reshape(D_MODEL, D_MODEL),
            w_ff1=gathered["w_ff1"][:, l].transpose(1, 0, 2).reshape(D_MODEL, D_FF),
            w_ff2=gathered["w_ff2"][:, l].reshape(D_FF, D_MODEL),
            conv_qkv_w=gathered["conv_qkv_w"][:, l].transpose(1, 0, 2).reshape(4, 3 * BRANCH_W),
            conv_sc_w=gathered["conv_sc_w"][:, l].transpose(1, 0, 2).reshape(3, BRANCH_W),
        )
        for k in ("norm_mix_pre", "gdn_a_log", "gdn_dt_bias", "gdn_norm_w", "norm_mix_post", "norm_ffn_pre", "norm_ffn_post"):
            p[k] = rep[k][l]
        layers.append(p)
    return layers


def _partials_by_chip(grads):
    def stack(name, fn):
        return jnp.stack([fn(g[name]) for g in grads], axis=1).astype(BF16)

    def w_in(gw):
        return _unpad_in_cols(gw).reshape(D_MODEL, N_CHIPS, IN_W // N_CHIPS).transpose(1, 0, 2)

    return dict(
        w_in=stack("w_in", w_in),
        w_branch=stack("w_branch", lambda gw: gw.reshape(N_BRANCH, BRANCH_W, N_CHIPS, D_MODEL // N_CHIPS).transpose(2, 0, 1, 3)),
        w_out=stack("w_out", lambda gw: gw.reshape(N_CHIPS, D_MODEL // N_CHIPS, D_MODEL)),
        w_ff1=stack("w_ff1", lambda gw: gw.reshape(D_MODEL, N_CHIPS, D_FF // N_CHIPS).transpose(1, 0, 2)),
        w_ff2=stack("w_ff2", lambda gw: gw.reshape(N_CHIPS, D_FF // N_CHIPS, D_MODEL)),
    )


def _pack_small(grads):
    pieces, layout = [], []
    for name in SMALL:
        v = jnp.stack([g[name] for g in grads]).astype(F32)
        layout.append((name, v.shape))
        pieces.append(v.reshape(-1))
    flat = jnp.concatenate(pieces)
    rows = -(-flat.shape[0] // LANES)
    rows = -(-rows // 8) * 8
    flat = jnp.pad(flat, (0, rows * LANES - flat.shape[0]))
    return flat.reshape(rows, LANES), layout


def _unpack_small(table, layout):
    flat, out, off = table.reshape(-1), {}, 0
    for name, shape in layout:
        size = 1
        for d in shape:
            size *= d
        out[name] = flat[off:off + size].reshape(shape)
        off += size
    return out


def _as2d(a):
    return a.reshape(-1, a.shape[-1]) if a.ndim > 1 else a.reshape(1, -1)


def kernel(x, norm_mix_pre, w_in, conv_qkv_w, gdn_a_log, gdn_dt_bias, gdn_norm_w, conv_sc_w, w_branch, w_out, norm_mix_post, norm_ffn_pre, w_ff1, w_ff2, norm_ffn_post, loss_target, m_norm_mix_pre, m_w_in, m_conv_qkv_w, m_gdn_a_log, m_gdn_dt_bias, m_gdn_norm_w, m_conv_sc_w, m_w_branch, m_w_out, m_norm_mix_post, m_norm_ffn_pre, m_w_ff1, m_w_ff2, m_norm_ffn_post, v_norm_mix_pre, v_w_in, v_conv_qkv_w, v_gdn_a_log, v_gdn_dt_bias, v_gdn_norm_w, v_conv_sc_w, v_w_branch, v_w_out, v_norm_mix_post, v_norm_ffn_pre, v_w_ff1, v_w_ff2, v_norm_ffn_post):
    w = dict(norm_mix_pre=norm_mix_pre, w_in=w_in, conv_qkv_w=conv_qkv_w, gdn_a_log=gdn_a_log, gdn_dt_bias=gdn_dt_bias,
             gdn_norm_w=gdn_norm_w, conv_sc_w=conv_sc_w, w_branch=w_branch, w_out=w_out, norm_mix_post=norm_mix_post,
             norm_ffn_pre=norm_ffn_pre, w_ff1=w_ff1, w_ff2=w_ff2, norm_ffn_post=norm_ffn_post)
    m = dict(norm_mix_pre=m_norm_mix_pre, w_in=m_w_in, conv_qkv_w=m_conv_qkv_w, gdn_a_log=m_gdn_a_log,
             gdn_dt_bias=m_gdn_dt_bias, gdn_norm_w=m_gdn_norm_w, conv_sc_w=m_conv_sc_w, w_branch=m_w_branch, w_out=m_w_out,
             norm_mix_post=m_norm_mix_post, norm_ffn_pre=m_norm_ffn_pre, w_ff1=m_w_ff1, w_ff2=m_w_ff2,
             norm_ffn_post=m_norm_ffn_post)
    v = dict(norm_mix_pre=v_norm_mix_pre, w_in=v_w_in, conv_qkv_w=v_conv_qkv_w, gdn_a_log=v_gdn_a_log,
             gdn_dt_bias=v_gdn_dt_bias, gdn_norm_w=v_gdn_norm_w, conv_sc_w=v_conv_sc_w, w_branch=v_w_branch, w_out=v_w_out,
             norm_mix_post=v_norm_mix_post, norm_ffn_pre=v_norm_ffn_pre, w_ff1=v_w_ff1, w_ff2=v_w_ff2,
             norm_ffn_post=v_norm_ffn_post)

    names = BIG + ("conv_qkv_w", "conv_sc_w")
    shards = [_cast_bf16(_as2d(w[k]), name=f"cast_{k}").reshape(w[k].shape) for k in BIG] + [conv_qkv_w, conv_sc_w]
    gathered = dict(zip(names, _gather_chips(shards, name="gather_weights")))
    layers = _full_weights(gathered, w)

    loss, dx, grads = _local_step(x[0], loss_target[0], layers)
    loss = lax.psum(loss, ("x", "y", "c"))

    parts = _partials_by_chip(grads)
    small, layout = _pack_small(grads)
    recv, small_all = _scatter_partials([parts[k] for k in BIG], small, name="scatter_grads")
    mine = [_sum_slots(r.reshape(N_CHIPS, -1, r.shape[-1]), name=f"sum_{k}") for k, r in zip(BIG, recv)]
    theirs = _swap_sibling(mine, name="swap_sibling")
    small_g = _unpack_small(_sum_slots(small_all, name="sum_small"), layout)
    xy = 2 * lax.axis_index("x") + lax.axis_index("y")
    for k, width in (("conv_qkv_w", 3 * BRANCH_W // N_CHIPS), ("conv_sc_w", BRANCH_W // N_CHIPS)):
        small_g[k] = lax.dynamic_slice_in_dim(small_g[k], xy * width, width, axis=2)

    out = {}
    for k, s_mine, s_theirs in zip(BIG, mine, theirs):
        res = _adamw(_as2d(w[k]), _as2d(m[k]), _as2d(v[k]), s_mine, s_theirs, name=f"adamw_{k}")
        out[k] = [r.reshape(w[k].shape) for r in res]
    for k in SMALL:
        res = _adamw(_as2d(w[k]), _as2d(m[k]), _as2d(v[k]), _as2d(small_g[k]), None, name=f"adamw_{k}")
        out[k] = [r.reshape(w[k].shape) for r in res]
    return (loss, dx[None], *[out[k][0] for k in ORDER], *[out[k][1] for k in ORDER], *[out[k][2] for k in ORDER],
            *[out[k][3] for k in ORDER])
```

```python
import functools

import jax
import jax.numpy as jnp
from jax import lax
from jax.experimental import pallas as pl
from jax.experimental.pallas import tpu as pltpu

F32 = jnp.float32
BF16 = jnp.bfloat16
MESH = pl.DeviceIdType.MESH

LANES = 128
D_MODEL = 1024
DEPTH = 4
CHUNK = 64
GDN_HEADS, GDN_DIM = 4, 128
SB_HEADS, SB_DIM = 8, 64
BRANCH_W = 512
N_BRANCH = 3
D_FF = 4 * D_MODEL
EPS = 1e-6
IN_W = 8200
AB_COL = 2048
AB_PAD = LANES - 8
IN_WP = IN_W + AB_PAD
N_CHIPS = 4
N_DEV = 8
GATES_COL = 5128
PB_GATES, PB_QKV, PB_GATE, PB_AB, PB_SB, PB_SCX, PB_SCB, PB_SCC = 0, 24, 36, 40, 41, 53, 57, 61
SB_TILE = 128
SB_GROUP = 4
SB_SCALE = SB_DIM ** -0.5
GDN_QSCALE = GDN_DIM ** -0.5
VMEM_LIMIT = 56 * 1024 * 1024

ADAM_LR, ADAM_B1, ADAM_B2, ADAM_EPS, ADAM_WD, ADAM_STEP = 0.001, 0.9, 0.999, 1e-08, 0.01, 10

NT = (((1,), (1,)), ((), ()))
TN = (((0,), (0,)), ((), ()))
HI = lax.Precision.HIGH


def _pad_in_cols(w):
    return jnp.concatenate([w[:, GATES_COL:], w[:, :AB_COL + 8], jnp.zeros((w.shape[0], AB_PAD), w.dtype),
                            w[:, AB_COL + 8:GATES_COL]], axis=1)


def _unpad_in_cols(g):
    n_gates = IN_W - GATES_COL
    return jnp.concatenate([g[:, n_gates:n_gates + AB_COL + 8], g[:, n_gates + AB_COL + 8 + AB_PAD:], g[:, :n_gates]], axis=1)


IN_SHARD = IN_W // N_CHIPS
_IN_SEGMENTS = ((0, AB_COL + 8, IN_W - GATES_COL), (AB_COL + 8, GATES_COL, IN_W - GATES_COL + AB_PAD),
                (GATES_COL, IN_W, -GATES_COL))


def _in_cols_from_chips(slots):
    def cols(first, last):
        out = []
        for j in range(N_CHIPS):
            lo, hi = max(first, j * IN_SHARD), min(last, (j + 1) * IN_SHARD)
            if lo < hi:
                out.append(slots[j][:, lo - j * IN_SHARD:hi - j * IN_SHARD])
        return out

    head, tail, gates = (cols(first, last) for first, last, _ in _IN_SEGMENTS)
    return jnp.concatenate(gates + head + [jnp.zeros((slots.shape[1], AB_PAD), slots.dtype)] + tail, axis=1)


def _in_cols_to_chips(g):
    shards = []
    for j in range(N_CHIPS):
        pieces = []
        for first, last, shift in _IN_SEGMENTS:
            lo, hi = max(first, j * IN_SHARD), min(last, (j + 1) * IN_SHARD)
            if lo < hi:
                pieces.append(g[:, lo + shift:hi + shift])
        shards.append(jnp.concatenate(pieces, axis=1))
    return jnp.stack(shards)


def _params(*sem):
    return pltpu.CompilerParams(dimension_semantics=sem if sem else None, vmem_limit_bytes=VMEM_LIMIT)


def _sigmoid(x):
    return 1.0 / (1.0 + jnp.exp(-x))


def _softplus(x):
    return jnp.maximum(x, 0.0) + jnp.log(1.0 + jnp.exp(-jnp.abs(x)))


def _dot(a, b, dims=None, precision=None):
    if dims is None:
        return jnp.dot(a, b, preferred_element_type=F32, precision=precision)
    return lax.dot_general(a, b, dims, preferred_element_type=F32, precision=precision)


def _bdot(a, b, dims=None):
    return _dot(a.astype(BF16), b.astype(BF16), dims)


def _matmul(a, b, *, name, ta=False, tb=False, tm, tn, tk=None, outs=(F32,), epi=None, extras=(), b_chips=False,
            out_chips=False):
    if ta:
        kdim, m = a.shape
    else:
        m, kdim = a.shape
    if b_chips:
        per = b.shape[2]
        if tb:
            n, kb = b.shape[1], N_CHIPS * per
        else:
            kb, n = b.shape[1], N_CHIPS * per
    elif tb:
        n, kb = b.shape
    else:
        kb, n = b.shape
    assert kdim == kb, (a.shape, b.shape)
    tk = kdim if tk is None else tk
    assert m % tm == 0 and n % tn == 0 and kdim % tk == 0, (m, n, kdim, tm, tn, tk)
    nk = kdim // tk
    a_spec = pl.BlockSpec((tk, tm), lambda i, j, k: (k, i)) if ta else pl.BlockSpec((tm, tk), lambda i, j, k: (i, k))
    if b_chips and tb:
        assert per % tk == 0
        b_spec = pl.BlockSpec((None, tn, tk), lambda i, j, k: (k // (per // tk), j, k % (per // tk)))
    elif b_chips:
        assert per % tn == 0
        b_spec = pl.BlockSpec((None, tk, tn), lambda i, j, k: (j // (per // tn), k, j % (per // tn)))
    else:
        b_spec = pl.BlockSpec((tn, tk), lambda i, j, k: (j, k)) if tb else pl.BlockSpec((tk, tn), lambda i, j, k: (k, j))
    mn_spec = pl.BlockSpec((tm, tn), lambda i, j, k: (i, j))
    if out_chips:
        per_o = n // N_CHIPS
        assert per_o % tn == 0
        out_spec = pl.BlockSpec((None, tm, tn), lambda i, j, k: (j // (per_o // tn), i, j % (per_o // tn)))
        out_dims = (N_CHIPS, m, per_o)
    else:
        out_spec, out_dims = mn_spec, (m, n)
    dims = (((0 if ta else 1,), (1 if tb else 0,)), ((), ()))
    n_ex, n_out = len(extras), len(outs)

    def body(a_ref, b_ref, *rest):
        ex, o, acc = rest[:n_ex], rest[n_ex:n_ex + n_out], rest[n_ex + n_out:]
        part = lax.dot_general(a_ref[...].astype(BF16), b_ref[...].astype(BF16), dims, preferred_element_type=F32)

        def finish(val):
            res = epi(val, *[e[...] for e in ex]) if epi is not None else (val,)
            for r, oref in zip(res, o):
                oref[...] = r.astype(oref.dtype)

        if nk == 1:
            finish(part)
        else:
            k = pl.program_id(2)

            @pl.when(k == 0)
            def _():
                acc[0][...] = part

            @pl.when(k > 0)
            def _():
                acc[0][...] += part

            @pl.when(k == nk - 1)
            def _():
                finish(acc[0][...])

    res = pl.pallas_call(
        body, name=name, grid=(m // tm, n // tn, nk),
        in_specs=[a_spec, b_spec] + [mn_spec] * n_ex,
        out_specs=[out_spec] * n_out,
        out_shape=[jax.ShapeDtypeStruct(out_dims, dt) for dt in outs],
        scratch_shapes=[pltpu.VMEM((tm, tn), F32)] if nk > 1 else [],
        compiler_params=_params("parallel", "parallel", "arbitrary"),
    )(a, b, *extras)
    return res[0] if n_out == 1 else res


ROW_TILE = 512


def _norm_fwd(x, w, *, name):
    s, d = x.shape

    def body(x_ref, w_ref, o_ref):
        xv = x_ref[...]
        r = lax.rsqrt(jnp.mean(xv * xv, axis=-1, keepdims=True) + EPS)
        o_ref[...] = (xv * r * w_ref[...]).astype(o_ref.dtype)

    return pl.pallas_call(
        body, name=name, grid=(s // ROW_TILE,),
        in_specs=[pl.BlockSpec((ROW_TILE, d), lambda i: (i, 0)), pl.BlockSpec((1, d), lambda i: (0, 0))],
        out_specs=pl.BlockSpec((ROW_TILE, d), lambda i: (i, 0)),
        out_shape=jax.ShapeDtypeStruct((s, d), BF16), compiler_params=_params("parallel"),
    )(x, w.reshape(1, d))


def _resnorm_fwd(x, u, w, *, name):
    s, d = x.shape

    def body(x_ref, u_ref, w_ref, o_ref):
        uv = u_ref[...]
        r = lax.rsqrt(jnp.mean(uv * uv, axis=-1, keepdims=True) + EPS)
        o_ref[...] = x_ref[...] + uv * r * w_ref[...]

    row = pl.BlockSpec((ROW_TILE, d), lambda i: (i, 0))
    return pl.pallas_call(
        body, name=name, grid=(s // ROW_TILE,),
        in_specs=[row, row, pl.BlockSpec((1, d), lambda i: (0, 0))], out_specs=row,
        out_shape=jax.ShapeDtypeStruct((s, d), F32), compiler_params=_params("parallel"),
    )(x, u, w.reshape(1, d))


def _norm_bwd(xin, w, dy, res, *, out_dtype, name):
    s, d = xin.shape
    has_res = res is not None

    def body(*refs):
        x_ref, w_ref, dy_ref = refs[:3]
        res_ref = refs[3] if has_res else None
        dx_ref, dw_ref = refs[3 + has_res:]
        xv, dyv = x_ref[...], dy_ref[...].astype(F32)
        r = lax.rsqrt(jnp.mean(xv * xv, axis=-1, keepdims=True) + EPS)
        xh = xv * r
        g = dyv * w_ref[...]
        dx = r * (g - xh * jnp.mean(g * xh, axis=-1, keepdims=True))
        if has_res:
            dx = dx + res_ref[...]
        dx_ref[...] = dx.astype(dx_ref.dtype)

        @pl.when(pl.program_id(0) == 0)
        def _():
            dw_ref[...] = jnp.zeros_like(dw_ref)

        dw_ref[...] += jnp.sum(dyv * xh, axis=0, keepdims=True)

    row = pl.BlockSpec((ROW_TILE, d), lambda i: (i, 0))
    vec = pl.BlockSpec((1, d), lambda i: (0, 0))
    ins = [xin, w.reshape(1, d), dy] + ([res] if has_res else [])
    dx, dw = pl.pallas_call(
        body, name=name, grid=(s // ROW_TILE,),
        in_specs=[row, vec, row] + ([row] if has_res else []), out_specs=[row, vec],
        out_shape=[jax.ShapeDtypeStruct((s, d), out_dtype), jax.ShapeDtypeStruct((1, d), F32)],
        compiler_params=_params("arbitrary"),
    )(*ins)
    return dx, dw.reshape(d)


def _loss_fwd_bwd(y, target, *, name):
    s, d = y.shape

    def body(y_ref, t_ref, loss_ref, dy_ref):
        e = y_ref[...] - t_ref[...]
        dy_ref[...] = e * (1.0 / d)

        @pl.when(pl.program_id(0) == 0)
        def _():
            loss_ref[...] = jnp.zeros_like(loss_ref)

        part = jnp.sum(jnp.sum(e * e, axis=1, keepdims=True), axis=0, keepdims=True)
        loss_ref[...] += part * (0.5 / d)

    row = pl.BlockSpec((ROW_TILE, d), lambda i: (i, 0))
    loss, dy = pl.pallas_call(
        body, name=name, grid=(s // ROW_TILE,), in_specs=[row, row],
        out_specs=[pl.BlockSpec((1, 1), lambda i: (0, 0)), row],
        out_shape=[jax.ShapeDtypeStruct((1, 1), F32), jax.ShapeDtypeStruct((s, d), F32)],
        compiler_params=_params("arbitrary"),
    )(y, target)
    return loss[0, 0], dy


def _shift_down(x, k, rows):
    if k == 0:
        return x
    return jnp.where(rows >= k, pltpu.roll(x, k, 0), 0.0)


def _shift_up(x, k, rows):
    if k == 0:
        return x
    n = x.shape[0]
    return jnp.where(rows < n - k, pltpu.roll(x, n - k, 0), 0.0)


def _col_spec(s, base):
    return pl.BlockSpec((s, LANES), lambda j: (0, base + j))


def _gdn_pre_math(x, w, j, rows):
    taps = w.shape[0]
    c = w[taps - 1:taps, :] * x
    for i in range(taps - 1):
        c = c + w[i:i + 1, :] * _shift_down(x, taps - 1 - i, rows)
    sg = _sigmoid(c)
    y = c * sg
    r = lax.rsqrt(jnp.sum(y * y, axis=-1, keepdims=True) + EPS)
    is_qk = j < 2 * GDN_HEADS
    scale = jnp.where(j < GDN_HEADS, GDN_QSCALE, 1.0)
    return c, sg, y, r, is_qk, scale


def _gdn_pre_fwd(proj, conv_w, *, name):
    s = proj.shape[0]

    def body(x_ref, w_ref, o_ref):
        j = pl.program_id(0)
        rows = lax.broadcasted_iota(jnp.int32, (s, LANES), 0)
        _, _, y, r, is_qk, scale = _gdn_pre_math(x_ref[...], w_ref[...], j, rows)
        o_ref[...] = jnp.where(is_qk, y * (r * scale), y)

    return pl.pallas_call(
        body, name=name, grid=(12,),
        in_specs=[_col_spec(s, PB_QKV), pl.BlockSpec((4, LANES), lambda j: (0, j))],
        out_specs=_col_spec(s, 0), out_shape=jax.ShapeDtypeStruct((s, 3 * BRANCH_W), F32),
        compiler_params=_params("parallel"),
    )(proj, conv_w)


def _gdn_pre_bwd(proj, conv_w, dqkvn, *, name):
    s = proj.shape[0]

    def body(x_ref, w_ref, d_ref, dx_ref, dw_ref):
        j = pl.program_id(0)
        rows = lax.broadcasted_iota(jnp.int32, (s, LANES), 0)
        x, w, dout = x_ref[...], w_ref[...], d_ref[...]
        c, sg, y, r, is_qk, scale = _gdn_pre_math(x, w, j, rows)
        yh = y * r
        dy_n = (scale * r) * (dout - yh * jnp.sum(dout * yh, axis=-1, keepdims=True))
        dy = jnp.where(is_qk, dy_n, dout)
        dc = dy * (sg * (1.0 + c * (1.0 - sg)))
        taps = w.shape[0]
        dx = w[taps - 1:taps, :] * dc
        dws = []
        for i in range(taps - 1):
            k = taps - 1 - i
            dx = dx + w[i:i + 1, :] * _shift_up(dc, k, rows)
            dws.append(jnp.sum(dc * _shift_down(x, k, rows), axis=0, keepdims=True))
        dws.append(jnp.sum(dc * x, axis=0, keepdims=True))
        dx_ref[...] = dx.astype(dx_ref.dtype)
        for i in range(taps):
            dw_ref[i:i + 1, :] = dws[i]

    return pl.pallas_call(
        body, name=name, grid=(12,),
        in_specs=[_col_spec(s, PB_QKV), pl.BlockSpec((4, LANES), lambda j: (0, j)), _col_spec(s, 0)],
        out_specs=[_col_spec(s, 0), pl.BlockSpec((4, LANES), lambda j: (0, j))],
        out_shape=[jax.ShapeDtypeStruct((s, 3 * BRANCH_W), BF16), jax.ShapeDtypeStruct((4, 3 * BRANCH_W), F32)],
        compiler_params=_params("parallel"),
    )(proj, conv_w, dqkvn)


def _lane_pad(v):
    return jnp.pad(v.reshape(1, -1), ((0, 0), (0, LANES - v.shape[0])))


def _gdn_gates_fwd(proj, a_log, dt_bias, *, name):
    s = proj.shape[0]

    def body(ab_ref, al_ref, dt_ref, o_ref):
        ab = ab_ref[...]
        lane = lax.broadcasted_iota(jnp.int32, (1, LANES), 1)
        g = -jnp.exp(al_ref[...]) * _softplus(ab + dt_ref[...])
        o_ref[...] = jnp.where(lane < GDN_HEADS, g, _sigmoid(ab))

    vec = pl.BlockSpec((1, LANES), lambda j: (0, 0))
    return pl.pallas_call(
        body, name=name, grid=(1,), in_specs=[_col_spec(s, PB_AB), vec, vec], out_specs=_col_spec(s, 0),
        out_shape=jax.ShapeDtypeStruct((s, LANES), F32), compiler_params=_params("arbitrary"),
    )(proj, _lane_pad(a_log), _lane_pad(dt_bias))


def _gdn_gates_bwd(proj, a_log, dt_bias, dgb, *, name):
    s = proj.shape[0]

    def body(ab_ref, al_ref, dt_ref, d_ref, dab_ref, dal_ref, ddt_ref):
        ab, d = ab_ref[...], d_ref[...]
        lane = lax.broadcasted_iota(jnp.int32, (1, LANES), 1)
        ea = jnp.exp(al_ref[...])
        pre = ab + dt_ref[...]
        g = -ea * _softplus(pre)
        dpre = d * (-ea) * _sigmoid(pre)
        beta = _sigmoid(ab)
        is_g = lane < GDN_HEADS
        dab = jnp.where(is_g, dpre, jnp.where(lane < 2 * GDN_HEADS, d * beta * (1.0 - beta), 0.0))
        dab_ref[...] = dab.astype(dab_ref.dtype)
        dal_ref[...] = jnp.sum(jnp.where(is_g, d * g, 0.0), axis=0, keepdims=True)
        ddt_ref[...] = jnp.sum(jnp.where(is_g, dpre, 0.0), axis=0, keepdims=True)

    vec = pl.BlockSpec((1, LANES), lambda j: (0, 0))
    dab, dal, ddt = pl.pallas_call(
        body, name=name, grid=(1,), in_specs=[_col_spec(s, PB_AB), vec, vec, _col_spec(s, 0)],
        out_specs=[_col_spec(s, 0), vec, vec],
        out_shape=[jax.ShapeDtypeStruct((s, LANES), BF16), jax.ShapeDtypeStruct((1, LANES), F32),
                   jax.ShapeDtypeStruct((1, LANES), F32)],
        compiler_params=_params("arbitrary"),
    )(proj, _lane_pad(a_log), _lane_pad(dt_bias), dgb)
    return dab, dal[0, :GDN_HEADS], ddt[0, :GDN_HEADS]


def _interleave(gens):
    results, live = [None] * len(gens), list(range(len(gens)))
    while live:
        for idx in list(live):
            try:
                next(gens[idx])
            except StopIteration as done:
                results[idx] = done.value
                live.remove(idx)
    return results


def _chunk_common(q, k, v, gb, gbt, h):
    c = CHUNK
    row = lax.broadcasted_iota(jnp.int32, (c, c), 0)
    col = lax.broadcasted_iota(jnp.int32, (c, c), 1)
    tril, strict, eye = row >= col, row > col, row == col
    lane = lax.broadcasted_iota(jnp.int32, (c, LANES), 1)
    sub = lax.broadcasted_iota(jnp.int32, (2 * GDN_HEADS, c), 0)
    g_col = jnp.sum(jnp.where(lane == h, gb, 0.0), axis=1, keepdims=True)
    beta_col = jnp.sum(jnp.where(lane == GDN_HEADS + h, gb, 0.0), axis=1, keepdims=True)
    g_row = jnp.sum(jnp.where(sub == h, gbt, 0.0), axis=0, keepdims=True)
    gc_col = jnp.sum(jnp.where(tril, jnp.broadcast_to(g_row, (c, c)), 0.0), axis=1, keepdims=True)
    gc_row = jnp.sum(jnp.where(row <= col, jnp.broadcast_to(g_col, (c, c)), 0.0), axis=0, keepdims=True)
    g_tot = jnp.sum(g_row, axis=1, keepdims=True)
    dm = jnp.exp(jnp.where(tril, gc_col - gc_row, -1e30))
    e_col = jnp.exp(gc_col)
    kdec_col = jnp.exp(g_tot - gc_col)
    gamma = jnp.exp(g_tot)
    kb = k * beta_col
    vb = v * beta_col
    kbg = kb * e_col
    kk = _dot(kb, k, NT, HI)
    qk = _bdot(q, k, NT)
    yield
    a = jnp.where(strict, kk * dm, 0.0)
    aqk = jnp.where(tril, qk * dm, 0.0)
    bneg = -a
    t = jnp.where(eye, 1.0, 0.0) + bneg
    p = _dot(bneg, bneg, precision=HI)
    yield
    for lvl in range(5):
        t_next = t + _dot(t, p, precision=HI)
        if lvl < 4:
            p = _dot(p, p, precision=HI)
        t = t_next
        yield
    u = _dot(t, vb, precision=HI)
    w = _dot(t, kbg, precision=HI)
    yield
    return dict(tril=tril, strict=strict, eye=eye, row=row, col=col, beta_col=beta_col, dm=dm, e_col=e_col,
                kdec_col=kdec_col, gamma=gamma, kb=kb, vb=vb, kbg=kbg, a=a, t=t, u=u, w=w, aqk=aqk,
                qd=q * e_col, kd=k * kdec_col)


def _gdn_chunk_fwd(qkvn, gb, gbt, *, name):
    s = qkvn.shape[0]
    n_chunks = s // CHUNK

    def body(q_ref, k_ref, v_ref, gb_ref, gbt_ref, o_ref, st_ref, state):
        @pl.when(pl.program_id(0) == 0)
        def _():
            state[...] = jnp.zeros_like(state)

        gbv, gbtv = gb_ref[...], gbt_ref[0]

        def head(h):
            hs = slice(h * GDN_DIM, (h + 1) * GDN_DIM)
            q, k, v = q_ref[:, hs], k_ref[:, hs], v_ref[:, hs]
            m = yield from _chunk_common(q, k, v, gbv, gbtv, h)
            s0 = state[h]
            st_ref[0, h] = s0
            vnew = m["u"] - _bdot(m["w"], s0)
            o_inter = _bdot(m["qd"], s0)
            yield
            o_ref[:, hs] = o_inter + _bdot(m["aqk"], vnew)
            state[h] = m["gamma"] * s0 + _bdot(m["kd"], vnew, TN)

        _interleave([head(h) for h in range(GDN_HEADS)])

    blk = lambda j: pl.BlockSpec((CHUNK, BRANCH_W), lambda n: (n, j))
    return pl.pallas_call(
        body, name=name, grid=(n_chunks,),
        in_specs=[blk(0), blk(1), blk(2), pl.BlockSpec((CHUNK, LANES), lambda n: (n, 0)),
                  pl.BlockSpec((1, 2 * GDN_HEADS, CHUNK), lambda n: (n, 0, 0))],
        out_specs=[blk(0), pl.BlockSpec((1, GDN_HEADS, GDN_DIM, GDN_DIM), lambda n: (n, 0, 0, 0))],
        out_shape=[jax.ShapeDtypeStruct((s, BRANCH_W), F32),
                   jax.ShapeDtypeStruct((n_chunks, GDN_HEADS, GDN_DIM, GDN_DIM), F32)],
        scratch_shapes=[pltpu.VMEM((GDN_HEADS, GDN_DIM, GDN_DIM), F32)],
        compiler_params=_params("arbitrary"),
    )(qkvn, qkvn, qkvn, gb, gbt)


def _gdn_chunk_bwd(qkvn, gb, gbt, states, do, *, name):
    s = qkvn.shape[0]
    n_chunks = s // CHUNK
    c = CHUNK

    def body(q_ref, k_ref, v_ref, gb_ref, gbt_ref, st_ref, do_ref, dq_ref, dk_ref, dv_ref, dgb_ref, dstate):
        @pl.when(pl.program_id(0) == 0)
        def _():
            dstate[...] = jnp.zeros_like(dstate)

        gbv, gbtv = gb_ref[...], gbt_ref[0]
        lane = lax.broadcasted_iota(jnp.int32, (c, LANES), 1)
        def head(h):
            hs = slice(h * GDN_DIM, (h + 1) * GDN_DIM)
            q, k, v, dov = q_ref[:, hs], k_ref[:, hs], v_ref[:, hs], do_ref[:, hs]
            m = yield from _chunk_common(q, k, v, gbv, gbtv, h)
            tril, strict, eye, row, col = m["tril"], m["strict"], m["eye"], m["row"], m["col"]
            s0, ds1 = st_ref[0, h], dstate[h]
            vnew = m["u"] - _bdot(m["w"], s0)
            dvnew_a = _bdot(m["aqk"], dov, TN) + _bdot(m["kd"], ds1)
            dqd = _bdot(dov, s0, NT)
            ds_q = _bdot(m["qd"], dov, TN)
            dgamma = jnp.sum(jnp.sum(s0 * ds1, axis=1, keepdims=True), axis=0, keepdims=True)
            yield
            dvnew = dvnew_a
            daqk = jnp.where(tril, _bdot(dov, vnew, NT), 0.0)
            dkd = _bdot(vnew, ds1, NT)
            dw = -_bdot(dvnew, s0, NT)
            dstate[h] = m["gamma"] * ds1 + ds_q - _bdot(m["w"], dvnew, TN)
            dvb = _dot(m["t"], dvnew, TN, HI)
            yield
            dt = _dot(dvnew, m["vb"], NT, HI) + _dot(dw, m["kbg"], NT, HI)
            dkbg = _dot(m["t"], dw, TN, HI)
            dmq = daqk * m["dm"]
            dq = _bdot(dmq, k) + dqd * m["e_col"]
            dk_q = _bdot(dmq, q, TN)
            yield
            tdt = _dot(m["t"], dt, TN, HI)
            yield
            da = jnp.where(strict, -_dot(tdt, m["t"], NT, HI), 0.0)
            yield
            dmat = da * m["dm"]
            dkb = _dot(dmat, k, precision=HI) + dkbg * m["e_col"]
            dk = (_dot(dmat, m["kb"], TN, HI) + dk_q + dkd * m["kdec_col"] + m["beta_col"] * dkb)
            yield
            dbeta_col = jnp.sum(dkb * k, axis=1, keepdims=True) + jnp.sum(dvb * v, axis=1, keepdims=True)
            e = da * m["a"] + daqk * m["aqk"]
            rs_kd = jnp.sum(dkd * m["kd"], axis=1, keepdims=True)
            e_colsum = jnp.sum(e, axis=0, keepdims=True)
            e_colsum_c = jnp.sum(jnp.where(eye, jnp.broadcast_to(e_colsum, (c, c)), 0.0), axis=1, keepdims=True)
            dgc = (jnp.sum(e, axis=1, keepdims=True) - e_colsum_c + jnp.sum(dqd * m["qd"], axis=1, keepdims=True)
                   - rs_kd + jnp.sum(dkbg * m["kbg"], axis=1, keepdims=True))
            last = jnp.sum(rs_kd, axis=0, keepdims=True) + dgamma * m["gamma"]
            dgc = dgc + jnp.where(lax.broadcasted_iota(jnp.int32, (c, 1), 0) == c - 1, last, 0.0)
            dgc_row = jnp.sum(jnp.where(eye, jnp.broadcast_to(dgc, (c, c)), 0.0), axis=0, keepdims=True)
            dg_col = jnp.sum(jnp.where(col >= row, jnp.broadcast_to(dgc_row, (c, c)), 0.0), axis=1, keepdims=True)
            dq_ref[:, hs] = dq
            dk_ref[:, hs] = dk
            dv_ref[:, hs] = m["beta_col"] * dvb
            return jnp.where(lane == h, dg_col, 0.0) + jnp.where(lane == GDN_HEADS + h, dbeta_col, 0.0)

        parts = _interleave([head(h) for h in range(GDN_HEADS)])
        dgb_ref[...] = (parts[0] + parts[1]) + (parts[2] + parts[3])

    rev = lambda n: n_chunks - 1 - n
    blk = lambda j: pl.BlockSpec((CHUNK, BRANCH_W), lambda n: (rev(n), j))
    dq, dk, dv, dgb = pl.pallas_call(
        body, name=name, grid=(n_chunks,),
        in_specs=[blk(0), blk(1), blk(2), pl.BlockSpec((CHUNK, LANES), lambda n: (rev(n), 0)),
                  pl.BlockSpec((1, 2 * GDN_HEADS, CHUNK), lambda n: (rev(n), 0, 0)),
                  pl.BlockSpec((1, GDN_HEADS, GDN_DIM, GDN_DIM), lambda n: (rev(n), 0, 0, 0)), blk(0)],
        out_specs=[blk(0), blk(0), blk(0), pl.BlockSpec((CHUNK, LANES), lambda n: (rev(n), 0))],
        out_shape=[jax.ShapeDtypeStruct((s, BRANCH_W), F32)] * 3 + [jax.ShapeDtypeStruct((s, LANES), F32)],
        scratch_shapes=[pltpu.VMEM((GDN_HEADS, GDN_DIM, GDN_DIM), F32)],
        compiler_params=_params("arbitrary"),
    )(qkvn, qkvn, qkvn, gb, gbt, states, do)
    return jnp.concatenate([dq, dk, dv], axis=1), dgb


def _gdn_post_fwd(o, proj, norm_w, *, name):
    s = o.shape[0]

    def body(o_ref, g_ref, w_ref, y_ref):
        ov, gv = o_ref[...], g_ref[...]
        r = lax.rsqrt(jnp.mean(ov * ov, axis=-1, keepdims=True) + EPS)
        y_ref[...] = (ov * r * w_ref[...] * (gv * _sigmoid(gv))).astype(y_ref.dtype)

    return pl.pallas_call(
        body, name=name, grid=(GDN_HEADS,),
        in_specs=[_col_spec(s, 0), _col_spec(s, PB_GATE), pl.BlockSpec((1, LANES), lambda j: (0, 0))],
        out_specs=_col_spec(s, 0), out_shape=jax.ShapeDtypeStruct((s, BRANCH_W), BF16),
        compiler_params=_params("parallel"),
    )(o, proj, norm_w.reshape(1, GDN_DIM))


def _gdn_post_bwd(o, proj, norm_w, dy, *, name):
    s = o.shape[0]

    def body(o_ref, g_ref, w_ref, dy_ref, do_ref, dg_ref, dw_ref):
        ov, gv, w, dyv = o_ref[...], g_ref[...], w_ref[...], dy_ref[...].astype(F32)
        r = lax.rsqrt(jnp.mean(ov * ov, axis=-1, keepdims=True) + EPS)
        oh = ov * r
        sg = _sigmoid(gv)
        silu = gv * sg
        dn = dyv * silu
        dg_ref[...] = (dyv * (oh * w) * (sg * (1.0 + gv * (1.0 - sg)))).astype(dg_ref.dtype)

        @pl.when(pl.program_id(0) == 0)
        def _():
            dw_ref[...] = jnp.zeros_like(dw_ref)

        dw_ref[...] += jnp.sum(dn * oh, axis=0, keepdims=True)
        g2 = dn * w
        do_ref[...] = r * (g2 - oh * jnp.mean(g2 * oh, axis=-1, keepdims=True))

    do, dg, dw = pl.pallas_call(
        body, name=name, grid=(GDN_HEADS,),
        in_specs=[_col_spec(s, 0), _col_spec(s, PB_GATE), pl.BlockSpec((1, LANES), lambda j: (0, 0)), _col_spec(s, 0)],
        out_specs=[_col_spec(s, 0), _col_spec(s, 0), pl.BlockSpec((1, LANES), lambda j: (0, 0))],
        out_shape=[jax.ShapeDtypeStruct((s, BRANCH_W), F32), jax.ShapeDtypeStruct((s, BRANCH_W), BF16),
                   jax.ShapeDtypeStruct((1, LANES), F32)],
        compiler_params=_params("arbitrary"),
    )(o, proj, norm_w.reshape(1, GDN_DIM), dy)
    return do, dg, dw.reshape(LANES)


def _split_terms(x):
    hi = x.astype(BF16)
    lo = (x - hi.astype(F32)).astype(BF16)
    return jnp.concatenate([hi, lo], axis=1)


def _sb_sum_matrix(pred):
    row = lax.broadcasted_iota(jnp.int32, (2 * SB_TILE, 2 * SB_TILE), 0) % SB_TILE
    col = lax.broadcasted_iota(jnp.int32, (2 * SB_TILE, 2 * SB_TILE), 1)
    return jnp.where((col >= SB_TILE) | pred(row, col), 1.0, 0.0).astype(BF16)


def _sb_head_masks():
    lane = lax.broadcasted_iota(jnp.int32, (1, LANES), 1)
    return [(lane < SB_DIM).astype(F32), (lane >= SB_DIM).astype(F32)]


def _sb_fwd(proj, *, name):
    s = proj.shape[0]
    t = SB_TILE
    nq = s // t

    def body(q_ref, k_ref, v_ref, o_ref, tot_ref):
        cmr = lax.broadcasted_iota(jnp.int32, (t, t), 1) - lax.broadcasted_iota(jnp.int32, (t, t), 0)
        uo = _sb_sum_matrix(lambda row, col: row > col)
        hm = _sb_head_masks()

        def qloop(i, carry0):
            qs = pl.multiple_of(i * t, t)
            qf = q_ref[pl.ds(qs, t), :] * SB_SCALE
            qh = [(qf * hm[h]).astype(BF16) for h in range(2)]
            diag = i // SB_GROUP

            def group(g, st, masked):
                ks = pl.multiple_of(g * (SB_GROUP * t), SB_GROUP * t)
                kb = k_ref[pl.ds(ks, SB_GROUP * t), :].astype(BF16)
                vf = v_ref[pl.ds(ks, SB_GROUP * t), :]
                tiles = [(h, j) for h in range(2) for j in range(SB_GROUP)]
                z = [_dot(qh[h], kb, NT) for h in range(2)]
                keep = {j: cmr < (i - g * SB_GROUP - j) * t for j in range(SB_GROUP)} if masked else None
                base, terms = {}, {}
                for h, j in tiles:
                    zj = z[h][:, j * t:(j + 1) * t]
                    sp = _softplus(zj)
                    l1m = jnp.where(keep[j], -sp, 0.0) if masked else -sp
                    base[h, j] = zj - sp
                    terms[h, j] = _split_terms(l1m)
                sums = {hj: _dot(terms[hj], uo) for hj in tiles}
                acc, new = st[0], []
                for h in range(2):
                    run, att = st[1 + h], [None] * SB_GROUP
                    for j in reversed(range(SB_GROUP)):
                        a = jnp.exp(base[h, j] + sums[h, j][:, :t] + run)
                        att[j] = (jnp.where(keep[j], a, 0.0) if masked else a).astype(BF16)
                        run = run + sums[h, j][:, t:]
                    acc = acc + _dot(jnp.concatenate(att, axis=1), (vf * hm[h]).astype(BF16))
                    new.append(run)
                return (acc, *new)

            zero = jnp.zeros((t, LANES), F32)
            st = group(diag, (zero, zero, zero), True)
            st = lax.fori_loop(0, diag, lambda jj, sv: group(diag - 1 - jj, sv, False), st)
            o_ref[pl.ds(qs, t), :] = st[0]
            tot_ref[pl.ds(qs, t), :] = st[1] * hm[0] + st[2] * hm[1]
            return carry0

        lax.fori_loop(0, nq, qloop, 0)

    out = jax.ShapeDtypeStruct((s, BRANCH_W), F32)
    return pl.pallas_call(
        body, name=name, grid=(SB_HEADS // 2,),
        in_specs=[_col_spec(s, PB_SB), _col_spec(s, PB_SB + 4), _col_spec(s, PB_SB + 8)],
        out_specs=[_col_spec(s, 0)] * 2, out_shape=[out] * 2,
        compiler_params=_params("parallel"),
    )(proj, proj, proj)


def _sb_bwd(proj, tot, do, *, name):
    s = proj.shape[0]
    t = SB_TILE
    nq = s // t

    def body(q_ref, k_ref, v_ref, tot_ref, do_ref, dq_ref, dk_ref, dv_ref, dk_acc, dv_acc):
        dk_acc[...] = jnp.zeros_like(dk_acc)
        dv_acc[...] = jnp.zeros_like(dv_acc)
        cmr = lax.broadcasted_iota(jnp.int32, (t, t), 1) - lax.broadcasted_iota(jnp.int32, (t, t), 0)
        u_le = _sb_sum_matrix(lambda row, col: row <= col)
        u_lt = _sb_sum_matrix(lambda row, col: row < col)
        hm = _sb_head_masks()

        def qloop(i, carry0):
            qs = pl.multiple_of(i * t, t)
            qraw = q_ref[pl.ds(qs, t), :]
            dov = do_ref[pl.ds(qs, t), :].astype(F32)
            totv = tot_ref[pl.ds(qs, t), :]
            qh = [(qraw * (hm[h] * SB_SCALE)).astype(BF16) for h in range(2)]
            q2 = jnp.concatenate([(qraw * hm[h]).astype(BF16) for h in range(2)], axis=0)
            doh = [(dov * hm[h]).astype(BF16) for h in range(2)]
            do2 = jnp.concatenate(doh, axis=0)
            tot = [jnp.min(totv * hm[h], axis=1, keepdims=True) for h in range(2)]
            diag = i // SB_GROUP

            def group(g, st, masked):
                ks = pl.multiple_of(g * (SB_GROUP * t), SB_GROUP * t)
                kf = k_ref[pl.ds(ks, SB_GROUP * t), :]
                kb = kf.astype(BF16)
                vb = v_ref[pl.ds(ks, SB_GROUP * t), :].astype(BF16)
                tiles = [(h, j) for h in range(2) for j in range(SB_GROUP)]
                z = [_dot(qh[h], kb, NT) for h in range(2)]
                datt = [_dot(doh[h], vb, NT) for h in range(2)]
                keep = {j: cmr < (i - g * SB_GROUP - j) * t for j in range(SB_GROUP)} if masked else None
                ls, lterms = {}, {}
                for h, j in tiles:
                    zj = z[h][:, j * t:(j + 1) * t]
                    sp = _softplus(zj)
                    ls[h, j] = zj - sp
                    lterms[h, j] = _split_terms(jnp.where(keep[j], -sp, 0.0) if masked else -sp)
                lsum = {hj: _dot(lterms[hj], u_le) for hj in tiles}
                att, p, pterms, new_c = {}, {}, {}, []
                for h in range(2):
                    run = st[1 + h]
                    for j in range(SB_GROUP):
                        a = jnp.exp(ls[h, j] + ((tot[h] - run) - lsum[h, j][:, :t]))
                        if masked:
                            a = jnp.where(keep[j], a, 0.0)
                        att[h, j] = a.astype(BF16)
                        p[h, j] = a * datt[h][:, j * t:(j + 1) * t]
                        pterms[h, j] = _split_terms(p[h, j])
                        run = run + lsum[h, j][:, t:]
                    new_c.append(run)
                psum = {hj: _dot(pterms[hj], u_lt) for hj in tiles}
                dzb, new_r = {}, []
                for h in range(2):
                    run = st[3 + h]
                    for j in range(SB_GROUP):
                        sig = jnp.exp(ls[h, j])
                        dz = p[h, j] * (1.0 - sig) - (run + psum[h, j][:, :t]) * sig
                        if masked:
                            dz = jnp.where(keep[j], dz, 0.0)
                        dzb[h, j] = (dz * SB_SCALE).astype(BF16)
                        run = run + psum[h, j][:, t:]
                    new_r.append(run)
                k2 = jnp.concatenate([(kf * hm[h]).astype(BF16) for h in range(2)], axis=0)
                dq_acc = st[0] + _dot(jnp.concatenate([dzb[hj] for hj in tiles], axis=1), k2)
                for j in range(SB_GROUP):
                    rows = pl.ds(pl.multiple_of(ks + j * t, t), t)
                    dk_acc[rows, :] += _dot(jnp.concatenate([dzb[0, j], dzb[1, j]], axis=0), q2, TN)
                    dv_acc[rows, :] += _dot(jnp.concatenate([att[0, j], att[1, j]], axis=0), do2, TN)
                return (dq_acc, *new_c, *new_r)

            zero = jnp.zeros((t, LANES), F32)
            st = lax.fori_loop(0, diag, lambda jj, sv: group(jj, sv, False), (zero,) * 5)
            st = group(diag, st, True)
            dq_ref[pl.ds(qs, t), :] = st[0].astype(dq_ref.dtype)
            return carry0

        lax.fori_loop(0, nq, qloop, 0)
        dk_ref[...] = dk_acc[...].astype(dk_ref.dtype)
        dv_ref[...] = dv_acc[...].astype(dv_ref.dtype)

    out = jax.ShapeDtypeStruct((s, BRANCH_W), BF16)
    return pl.pallas_call(
        body, name=name, grid=(SB_HEADS // 2,),
        in_specs=[_col_spec(s, PB_SB), _col_spec(s, PB_SB + 4), _col_spec(s, PB_SB + 8), _col_spec(s, 0), _col_spec(s, 0)],
        out_specs=[_col_spec(s, 0)] * 3, out_shape=[out] * 3,
        scratch_shapes=[pltpu.VMEM((s, LANES), F32), pltpu.VMEM((s, LANES), F32)],
        compiler_params=_params("parallel"),
    )(proj, proj, proj, tot, do)


def _sc_fwd(proj, conv_w, *, name):
    s = proj.shape[0]

    def body(x_ref, b_ref, c_ref, w_ref, y_ref):
        rows = lax.broadcasted_iota(jnp.int32, (s, LANES), 0)
        w = w_ref[...]
        u = c_ref[...] * x_ref[...]
        cv = w[2:3, :] * u + w[1:2, :] * _shift_down(u, 1, rows) + w[0:1, :] * _shift_down(u, 2, rows)
        y_ref[...] = (b_ref[...] * cv).astype(y_ref.dtype)

    return pl.pallas_call(
        body, name=name, grid=(BRANCH_W // LANES,),
        in_specs=[_col_spec(s, PB_SCX), _col_spec(s, PB_SCB), _col_spec(s, PB_SCC), pl.BlockSpec((3, LANES), lambda j: (0, j))],
        out_specs=_col_spec(s, 0), out_shape=jax.ShapeDtypeStruct((s, BRANCH_W), BF16),
        compiler_params=_params("parallel"),
    )(proj, proj, proj, conv_w)


def _sc_bwd(proj, conv_w, dy, *, name):
    s = proj.shape[0]

    def body(x_ref, b_ref, c_ref, w_ref, dy_ref, dx_ref, db_ref, dc_ref, dw_ref):
        rows = lax.broadcasted_iota(jnp.int32, (s, LANES), 0)
        w, x, cg, dyv = w_ref[...], x_ref[...], c_ref[...], dy_ref[...].astype(F32)
        u = cg * x
        u1, u2 = _shift_down(u, 1, rows), _shift_down(u, 2, rows)
        cv = w[2:3, :] * u + w[1:2, :] * u1 + w[0:1, :] * u2
        db_ref[...] = (dyv * cv).astype(db_ref.dtype)
        dcv = dyv * b_ref[...]
        du = w[2:3, :] * dcv + w[1:2, :] * _shift_up(dcv, 1, rows) + w[0:1, :] * _shift_up(dcv, 2, rows)
        dx_ref[...] = (du * cg).astype(dx_ref.dtype)
        dc_ref[...] = (du * x).astype(dc_ref.dtype)
        dw_ref[0:1, :] = jnp.sum(dcv * u2, axis=0, keepdims=True)
        dw_ref[1:2, :] = jnp.sum(dcv * u1, axis=0, keepdims=True)
        dw_ref[2:3, :] = jnp.sum(dcv * u, axis=0, keepdims=True)

    out = jax.ShapeDtypeStruct((s, BRANCH_W), BF16)
    wspec = pl.BlockSpec((3, LANES), lambda j: (0, j))
    return pl.pallas_call(
        body, name=name, grid=(BRANCH_W // LANES,),
        in_specs=[_col_spec(s, PB_SCX), _col_spec(s, PB_SCB), _col_spec(s, PB_SCC), wspec, _col_spec(s, 0)],
        out_specs=[_col_spec(s, 0)] * 3 + [wspec],
        out_shape=[out] * 3 + [jax.ShapeDtypeStruct((3, BRANCH_W), F32)],
        compiler_params=_params("parallel"),
    )(proj, proj, proj, conv_w, dy)


MERGE_TM, MERGE_TN = 512, D_MODEL // N_CHIPS


def _merge_specs():
    tm, tn = MERGE_TM, MERGE_TN
    y_spec = pl.BlockSpec((tm, BRANCH_W), lambda i, j: (i, 0))
    w_spec = pl.BlockSpec((None, N_BRANCH, BRANCH_W, tn), lambda i, j: (j, 0, 0, 0))
    gate_specs = [pl.BlockSpec((tm, tn), functools.partial(
        lambda i, j, b: (i, (PB_GATES * LANES + b * D_MODEL) // tn + j), b=b)) for b in range(N_BRANCH)]
    mn = pl.BlockSpec((tm, tn), lambda i, j: (i, j))
    return y_spec, w_spec, gate_specs, mn


def _merge_fwd(ya, yb, yc, wb, proj, *, name):
    s = ya.shape[0]
    y_spec, w_spec, gate_specs, mn = _merge_specs()

    def body(ya_ref, yb_ref, yc_ref, w_ref, g0, g1, g2, o_ref):
        acc = None
        for b, (y_ref, g_ref) in enumerate(zip((ya_ref, yb_ref, yc_ref), (g0, g1, g2))):
            term = _sigmoid(g_ref[...]) * _bdot(y_ref[...], w_ref[b])
            acc = term if acc is None else acc + term
        o_ref[...] = acc.astype(o_ref.dtype)

    return pl.pallas_call(
        body, name=name, grid=(s // MERGE_TM, D_MODEL // MERGE_TN),
        in_specs=[y_spec] * 3 + [w_spec] + gate_specs, out_specs=mn,
        out_shape=jax.ShapeDtypeStruct((s, D_MODEL), BF16), compiler_params=_params("parallel", "parallel"),
    )(ya, yb, yc, wb, proj, proj, proj)


def _merge_bwd(ya, yb, yc, wb, proj, dm, *, name):
    s = ya.shape[0]
    y_spec, w_spec, gate_specs, mn = _merge_specs()

    def body(ya_ref, yb_ref, yc_ref, w_ref, g0, g1, g2, dm_ref, *outs):
        dmv = dm_ref[...].astype(F32)
        for b, (y_ref, g_ref) in enumerate(zip((ya_ref, yb_ref, yc_ref), (g0, g1, g2))):
            sg = _sigmoid(g_ref[...])
            z = _bdot(y_ref[...], w_ref[b])
            outs[b][...] = (dmv * sg).astype(BF16)
            outs[N_BRANCH + b][...] = (dmv * z * sg * (1.0 - sg)).astype(BF16)

    out = jax.ShapeDtypeStruct((s, D_MODEL), BF16)
    res = pl.pallas_call(
        body, name=name, grid=(s // MERGE_TM, D_MODEL // MERGE_TN),
        in_specs=[y_spec] * 3 + [w_spec] + gate_specs + [mn], out_specs=[mn] * (2 * N_BRANCH),
        out_shape=[out] * (2 * N_BRANCH), compiler_params=_params("parallel", "parallel"),
    )(ya, yb, yc, wb, proj, proj, proj, dm)
    return res[:N_BRANCH], res[N_BRANCH:]


def _chunk_rows(v, s):
    return v[:, :2 * GDN_HEADS].reshape(s // CHUNK, CHUNK, 2 * GDN_HEADS).transpose(0, 2, 1)


def _relu2_epi(acc):
    r = jnp.maximum(acc, 0.0)
    return acc, r * r


def _drelu2_epi(acc, a):
    return (acc * (2.0 * jnp.maximum(a.astype(F32), 0.0)),)


def _layer_fwd(x0, p):
    s = x0.shape[0]
    h1 = _norm_fwd(x0, p["norm_mix_pre"], name="norm_mix_pre")
    proj = _matmul(h1, p["w_in"], name="proj_in", tm=512, tn=1664)
    qkvn = _gdn_pre_fwd(proj, p["conv_qkv_w"], name="gdn_pre")
    gb = _gdn_gates_fwd(proj, p["gdn_a_log"], p["gdn_dt_bias"], name="gdn_gates")
    gbt = _chunk_rows(gb, s)
    o_gdn, states = _gdn_chunk_fwd(qkvn, gb, gbt, name="gdn_chunk")
    ya = _gdn_post_fwd(o_gdn, proj, p["gdn_norm_w"], name="gdn_post")
    o_sb, sb_tot = _sb_fwd(proj, name="sb_attn")
    yc = _sc_fwd(proj, p["conv_sc_w"], name="short_conv")
    merged = _merge_fwd(ya, o_sb, yc, p["w_branch"], proj, name="merge")
    u = _matmul(merged, p["w_out"], name="proj_out", tm=512, tn=1024)
    x1 = _resnorm_fwd(x0, u, p["norm_mix_post"], name="norm_mix_post")
    h2 = _norm_fwd(x1, p["norm_ffn_pre"], name="norm_ffn_pre")
    a, r = _matmul(h2, p["w_ff1"], name="ff1", tm=512, tn=1024, outs=(BF16, BF16), epi=_relu2_epi, b_chips=True)
    f = _matmul(r, p["w_ff2"], name="ff2", tm=512, tn=1024, tk=1024)
    x2 = _resnorm_fwd(x1, f, p["norm_ffn_post"], name="norm_ffn_post")
    saved = dict(x0=x0, h1=h1, proj=proj, qkvn=qkvn, gb=gb, gbt=gbt, o_gdn=o_gdn, states=states, ya=ya, o_sb=o_sb,
                 sb_tot=sb_tot, yc=yc, merged=merged, u=u, x1=x1, h2=h2, a=a, r=r, f=f)
    return x2, saved


def _layer_bwd(dx2, p, sv):
    g = {}
    df, g["norm_ffn_post"] = _norm_bwd(sv["f"], p["norm_ffn_post"], dx2, None, out_dtype=BF16, name="norm_ffn_post_bwd")
    g["w_ff2"] = _matmul(sv["r"], df, ta=True, name="ff2_dw", tm=1024, tn=1024, tk=512)
    da = _matmul(df, p["w_ff2"], tb=True, name="ff2_dx", tm=512, tn=1024, outs=(BF16,), epi=_drelu2_epi,
                 extras=(sv["a"],))
    g["w_ff1"] = _matmul(sv["h2"], da, ta=True, name="ff1_dw", tm=1024, tn=1024, tk=512, out_chips=True)
    dh2 = _matmul(da, p["w_ff1"], tb=True, name="ff1_dx", tm=512, tn=1024, tk=1024, b_chips=True)
    dx1, g["norm_ffn_pre"] = _norm_bwd(sv["x1"], p["norm_ffn_pre"], dh2, dx2, out_dtype=F32, name="norm_ffn_pre_bwd")
    du, g["norm_mix_post"] = _norm_bwd(sv["u"], p["norm_mix_post"], dx1, None, out_dtype=BF16, name="norm_mix_post_bwd")
    g["w_out"] = _matmul(sv["merged"], du, ta=True, name="out_dw", tm=1024, tn=1024, tk=512)
    dmerged = _matmul(du, p["w_out"], tb=True, name="out_dx", tm=512, tn=1024, outs=(BF16,))
    ys = (sv["ya"], sv["o_sb"], sv["yc"])
    dz, dgates = _merge_bwd(*ys, p["w_branch"], sv["proj"], dmerged, name="merge_bwd")
    g["w_branch"] = jnp.stack([_matmul(ys[b], dz[b], ta=True, name=f"branch_dw{b}", tm=512, tn=256, tk=512, out_chips=True)
                               for b in range(N_BRANCH)], axis=1)
    dys = [_matmul(dz[b], p["w_branch"][:, b], tb=True, name=f"branch_dx{b}", tm=512, tn=512, tk=256, b_chips=True)
           for b in range(N_BRANCH)]
    dscx, dscb, dscc, g["conv_sc_w"] = _sc_bwd(sv["proj"], p["conv_sc_w"], dys[2], name="short_conv_bwd")
    dsq, dsk, dsv = _sb_bwd(sv["proj"], sv["sb_tot"], dys[1], name="sb_attn_bwd")
    do_gdn, dgate, dnw = _gdn_post_bwd(sv["o_gdn"], sv["proj"], p["gdn_norm_w"], dys[0], name="gdn_post_bwd")
    g["gdn_norm_w"] = dnw
    dqkvn, dgb = _gdn_chunk_bwd(sv["qkvn"], sv["gb"], sv["gbt"], sv["states"], do_gdn, name="gdn_chunk_bwd")
    dqkv, g["conv_qkv_w"] = _gdn_pre_bwd(sv["proj"], p["conv_qkv_w"], dqkvn, name="gdn_pre_bwd")
    dab, g["gdn_a_log"], g["gdn_dt_bias"] = _gdn_gates_bwd(sv["proj"], p["gdn_a_log"], p["gdn_dt_bias"], dgb,
                                                           name="gdn_gates_bwd")
    dproj = jnp.concatenate([*dgates, dqkv, dgate, dab, dsq, dsk, dsv, dscx, dscb, dscc], axis=1)
    g["w_in"] = _matmul(sv["h1"], dproj, ta=True, name="in_dw", tm=1024, tn=1664, tk=512)
    dh1 = _matmul(dproj, p["w_in"], tb=True, name="in_dx", tm=512, tn=1024, tk=1664)
    dx0, g["norm_mix_pre"] = _norm_bwd(sv["x0"], p["norm_mix_pre"], dh1, dx1, out_dtype=F32, name="norm_mix_pre_bwd")
    return dx0, g


def _local_step(x, target, n_layers, weights_of, grads_done):
    saved, layers = [], []
    h = x
    for l in range(n_layers):
        p = weights_of(l, h)
        h, sv = _layer_fwd(h, p)
        saved.append(sv)
        layers.append(p)
    loss, dh = _loss_fwd_bwd(h, target, name="loss")
    for l in reversed(range(n_layers)):
        dh, g = _layer_bwd(dh, layers[l], saved[l])
        zero = grads_done(l, g)
        if l > 0:
            layers[l - 1] = dict(layers[l - 1], norm_ffn_post=layers[l - 1]["norm_ffn_post"] + zero)
    return loss, dh


ANY = pl.BlockSpec(memory_space=pl.ANY)


def _me_and_chips():
    x, y, c = lax.axis_index("x"), lax.axis_index("y"), lax.axis_index("c")
    chips = [(1 - x, y), (x, 1 - y), (1 - x, 1 - y)]
    return x, y, c, chips


def _gather_chips(arrs, *, name):
    n = len(arrs)

    def body(*refs):
        ins, outs = refs[:n], refs[n:2 * n]
        send_sems, recv_sems, local_sems = refs[2 * n:]
        x, y, c, chips = _me_and_chips()
        me = 2 * x + y
        sends, locals_ = [], []
        for a in range(n):
            lc = pltpu.make_async_copy(ins[a], outs[a].at[me], local_sems.at[a])
            lc.start()
            locals_.append(lc)
            for k, (px, py) in enumerate(chips):
                cp = pltpu.make_async_remote_copy(src_ref=ins[a], dst_ref=outs[a].at[me], send_sem=send_sems.at[a, k],
                                                  recv_sem=recv_sems.at[a, k], device_id=(px, py, c), device_id_type=MESH)
                cp.start()
                sends.append(cp)
        for a in range(n):
            for k, (px, py) in enumerate(chips):
                pltpu.make_async_remote_copy(src_ref=ins[a], dst_ref=outs[a].at[2 * px + py], send_sem=send_sems.at[a, k],
                                             recv_sem=recv_sems.at[a, k], device_id=(px, py, c),
                                             device_id_type=MESH).wait_recv()
        for cp in sends:
            cp.wait_send()
        for lc in locals_:
            lc.wait()

    return pl.pallas_call(
        body, name=name, in_specs=[ANY] * n, out_specs=[ANY] * n,
        out_shape=[jax.ShapeDtypeStruct((N_CHIPS,) + a.shape, a.dtype) for a in arrs],
        scratch_shapes=[pltpu.SemaphoreType.DMA((n, 3)), pltpu.SemaphoreType.DMA((n, 3)), pltpu.SemaphoreType.DMA((n,))],
    )(*arrs)


def _gather_devices(small, *, name):
    def body(small_ref, small_out, ssend, srecv, local_sem):
        x, y, c, chips = _me_and_chips()
        dev = 4 * x + 2 * y + c
        lc = pltpu.make_async_copy(small_ref, small_out.at[dev], local_sem)
        lc.start()
        peers = [(x, y, 1 - c)] + [(px, py, pc) for (px, py) in chips for pc in (c, 1 - c)]
        sends = []
        for k, peer in enumerate(peers):
            cp = pltpu.make_async_remote_copy(src_ref=small_ref, dst_ref=small_out.at[dev], send_sem=ssend.at[k],
                                              recv_sem=srecv.at[k], device_id=peer, device_id_type=MESH)
            cp.start()
            sends.append(cp)
        for k, (px, py, pc) in enumerate(peers):
            pltpu.make_async_remote_copy(src_ref=small_ref, dst_ref=small_out.at[4 * px + 2 * py + pc], send_sem=ssend.at[k],
                                         recv_sem=srecv.at[k], device_id=(px, py, pc), device_id_type=MESH).wait_recv()
        for cp in sends:
            cp.wait_send()
        lc.wait()

    return pl.pallas_call(
        body, name=name, in_specs=[ANY], out_specs=ANY,
        out_shape=jax.ShapeDtypeStruct((N_DEV,) + small.shape, small.dtype),
        scratch_shapes=[pltpu.SemaphoreType.DMA((N_DEV - 1,)), pltpu.SemaphoreType.DMA((N_DEV - 1,)), pltpu.SemaphoreType.DMA],
    )(small)


HBM = pl.BlockSpec(memory_space=pltpu.HBM)
SEM = pl.BlockSpec(memory_space=pltpu.SEMAPHORE)
EFFECT = pltpu.SideEffectType.DATAFLOW_SIDE_EFFECTING


def _exchange_start(srcs, *, by_slot, name):
    n = len(srcs)
    land_shapes = [a.shape if by_slot else (N_CHIPS,) + a.shape for a in srcs]
    lands = [pltpu.with_memory_space_constraint(lax.empty(sh, a.dtype), pltpu.HBM) for sh, a in zip(land_shapes, srcs)]
    srcs = [pltpu.with_memory_space_constraint(a, pltpu.HBM) for a in srcs]

    def body(*refs):
        ins, land = refs[:n], refs[n:2 * n]
        send_sems, recv_sems, token = refs[2 * n], refs[2 * n + 1], refs[-1]
        x, y, c, chips = _me_and_chips()
        me = 2 * x + y
        for a in range(n):
            for k, (px, py) in enumerate(chips):
                pltpu.make_async_remote_copy(
                    src_ref=ins[a].at[2 * px + py] if by_slot else ins[a], dst_ref=land[a].at[me],
                    send_sem=send_sems.at[3 * a + k], recv_sem=recv_sems.at[3 * a + k], device_id=(px, py, c),
                    device_id_type=MESH).start()
        token[...] = jnp.zeros_like(token)

    res = pl.pallas_call(
        body, name=name, in_specs=[HBM] * (2 * n),
        out_specs=[SEM, SEM] + [HBM] * (2 * n) + [pl.BlockSpec(memory_space=pltpu.VMEM)],
        out_shape=[pltpu.SemaphoreType.DMA((3 * n,)), pltpu.SemaphoreType.DMA((3 * n,))]
        + [pltpu.HBM(a.shape, a.dtype) for a in srcs] + [pltpu.HBM(sh, a.dtype) for sh, a in zip(land_shapes, srcs)]
        + [jax.ShapeDtypeStruct((8, LANES), F32)],
        input_output_aliases={i: 2 + i for i in range(2 * n)},
        compiler_params=pltpu.CompilerParams(has_side_effects=EFFECT),
    )(*srcs, *lands)
    return dict(send=res[0], recv=res[1], srcs=res[2:2 + n], lands=res[2 + n:2 + 2 * n], token=res[-1])


def _exchange_wait(ex, after, *, by_slot, name):
    n = len(ex["srcs"])

    def body(*refs):
        ins, land = refs[:n], refs[n:2 * n]
        send_sems, recv_sems = refs[2 * n], refs[2 * n + 1]
        x, y, c, chips = _me_and_chips()
        me = 2 * x + y
        for a in range(n):
            for k, (px, py) in enumerate(chips):
                cp = pltpu.make_async_remote_copy(
                    src_ref=ins[a].at[me] if by_slot else ins[a], dst_ref=land[a].at[2 * px + py],
                    send_sem=send_sems.at[3 * a + k], recv_sem=recv_sems.at[3 * a + k], device_id=(px, py, c),
                    device_id_type=MESH)
                cp.wait_send()
                cp.wait_recv()

    res = pl.pallas_call(
        body, name=name, in_specs=[HBM] * (2 * n) + [SEM, SEM, ANY], out_specs=[HBM] * (2 * n),
        out_shape=[pltpu.HBM(a.shape, a.dtype) for a in ex["srcs"]] + [pltpu.HBM(a.shape, a.dtype) for a in ex["lands"]],
        input_output_aliases={i: i for i in range(2 * n)},
        compiler_params=pltpu.CompilerParams(has_side_effects=EFFECT),
    )(*ex["srcs"], *ex["lands"], ex["send"], ex["recv"], after)
    return res[:n], res[n:]


def _chip_index():
    return 2 * lax.axis_index("x") + lax.axis_index("y")


def _place_own(lands, owns, *, by_slot, name):
    n = len(lands)

    def body(*refs):
        own, out, sems = refs[n:2 * n], refs[2 * n:3 * n], refs[3 * n]
        me = _chip_index()
        cps = [pltpu.make_async_copy(own[a].at[me] if by_slot else own[a], out[a].at[me], sems.at[a]) for a in range(n)]
        for cp in cps:
            cp.start()
        for cp in cps:
            cp.wait()

    return pl.pallas_call(
        body, name=name, in_specs=[ANY] * (2 * n), out_specs=[ANY] * n,
        out_shape=[jax.ShapeDtypeStruct(a.shape, a.dtype) for a in lands],
        input_output_aliases={a: a for a in range(n)}, scratch_shapes=[pltpu.SemaphoreType.DMA((n,))],
    )(*lands, *owns)


def _swap_sibling(arrs, *, name):
    n = len(arrs)

    def body(*refs):
        ins, outs = refs[:n], refs[n:2 * n]
        send_sems, recv_sems = refs[2 * n:]
        x, y, c = lax.axis_index("x"), lax.axis_index("y"), lax.axis_index("c")
        cps = [pltpu.make_async_remote_copy(src_ref=ins[a], dst_ref=outs[a], send_sem=send_sems.at[a],
                                            recv_sem=recv_sems.at[a], device_id=(x, y, 1 - c), device_id_type=MESH)
               for a in range(n)]
        for cp in cps:
            cp.start()
        for cp in cps:
            cp.wait()

    return pl.pallas_call(
        body, name=name, in_specs=[ANY] * n, out_specs=[ANY] * n,
        out_shape=[jax.ShapeDtypeStruct(a.shape, a.dtype) for a in arrs],
        scratch_shapes=[pltpu.SemaphoreType.DMA((n,)), pltpu.SemaphoreType.DMA((n,))],
    )(*arrs)


def _row_tile(rows, cols, budget=2 * 1024 * 1024):
    best = None
    for t in range(16, rows + 1, 16):
        if rows % t == 0 and t * cols * 4 <= budget:
            best = t
    return best if best is not None else rows


def _sum_slots(parts, *, name):
    n, rows, cols = parts.shape
    tr = _row_tile(rows, cols, 1024 * 1024)

    def body(p_ref, o_ref):
        acc = p_ref[0].astype(F32)
        for i in range(1, n):
            acc = acc + p_ref[i].astype(F32)
        o_ref[...] = acc

    return pl.pallas_call(
        body, name=name, grid=(rows // tr,), in_specs=[pl.BlockSpec((n, tr, cols), lambda i: (0, i, 0))],
        out_specs=pl.BlockSpec((tr, cols), lambda i: (i, 0)), out_shape=jax.ShapeDtypeStruct((rows, cols), F32),
        compiler_params=_params("parallel"),
    )(parts)


def _adamw(w, m, v, g_a, g_b, *, name):
    rows, cols = w.shape
    tr = _row_tile(rows, cols, 1024 * 1024)
    two = g_b is not None
    c1 = 1.0 / (1.0 - ADAM_B1 ** ADAM_STEP)
    c2 = 1.0 / (1.0 - ADAM_B2 ** ADAM_STEP)

    def body(*refs):
        w_ref, m_ref, v_ref, ga_ref = refs[:4]
        g_ref, d_ref, nm_ref, nv_ref = refs[4 + two:]
        g = ga_ref[...]
        if two:
            g = g + refs[4][...]
        nm = ADAM_B1 * m_ref[...] + (1.0 - ADAM_B1) * g
        nv = ADAM_B2 * v_ref[...] + (1.0 - ADAM_B2) * (g * g)
        g_ref[...] = g
        nm_ref[...] = nm
        nv_ref[...] = nv
        d_ref[...] = -ADAM_LR * ((nm * c1) / (jnp.sqrt(nv * c2) + ADAM_EPS) + ADAM_WD * w_ref[...])

    blk = pl.BlockSpec((tr, cols), lambda i: (i, 0))
    ins = [w, m, v, g_a] + ([g_b] if two else [])
    return pl.pallas_call(
        body, name=name, grid=(rows // tr,), in_specs=[blk] * len(ins), out_specs=[blk] * 4,
        out_shape=[jax.ShapeDtypeStruct((rows, cols), F32)] * 4, compiler_params=_params("parallel"),
    )(*ins)


def _cast_bf16(w, *, name):
    rows, cols = w.shape
    tr = _row_tile(rows, cols)

    def body(w_ref, o_ref):
        o_ref[...] = w_ref[...].astype(BF16)

    blk = pl.BlockSpec((tr, cols), lambda i: (i, 0))
    return pl.pallas_call(body, name=name, grid=(rows // tr,), in_specs=[blk], out_specs=blk,
                          out_shape=jax.ShapeDtypeStruct((rows, cols), BF16), compiler_params=_params("parallel"))(w)


BIG = ("w_in", "w_branch", "w_out", "w_ff1", "w_ff2")
SMALL = ("norm_mix_pre", "conv_qkv_w", "gdn_a_log", "gdn_dt_bias", "gdn_norm_w", "conv_sc_w", "norm_mix_post",
         "norm_ffn_pre", "norm_ffn_post")
ORDER = ("norm_mix_pre", "w_in", "conv_qkv_w", "gdn_a_log", "gdn_dt_bias", "gdn_norm_w", "conv_sc_w", "w_branch",
         "w_out", "norm_mix_post", "norm_ffn_pre", "w_ff1", "w_ff2", "norm_ffn_post")


def _full_weights(big, conv, rep, l):
    p = dict(
        w_in=_in_cols_from_chips(big["w_in"]),
        w_branch=big["w_branch"],
        w_out=big["w_out"].reshape(D_MODEL, D_MODEL),
        w_ff1=big["w_ff1"],
        w_ff2=big["w_ff2"].reshape(D_FF, D_MODEL),
        conv_qkv_w=conv["conv_qkv_w"][:, l].transpose(1, 0, 2).reshape(4, 3 * BRANCH_W),
        conv_sc_w=conv["conv_sc_w"][:, l].transpose(1, 0, 2).reshape(3, BRANCH_W),
    )
    for k in ("norm_mix_pre", "gdn_a_log", "gdn_dt_bias", "gdn_norm_w", "norm_mix_post", "norm_ffn_pre", "norm_ffn_post"):
        p[k] = rep[k][l]
    return p


def _partials_by_chip(g):
    parts = dict(
        w_in=_in_cols_to_chips(g["w_in"]),
        w_branch=g["w_branch"],
        w_out=g["w_out"].reshape(N_CHIPS, D_MODEL // N_CHIPS, D_MODEL),
        w_ff1=g["w_ff1"],
        w_ff2=g["w_ff2"].reshape(N_CHIPS, D_FF // N_CHIPS, D_MODEL),
    )
    return [parts[k].astype(BF16) for k in BIG]


def _pack_small(grads):
    pieces, layout = [], []
    for name in SMALL:
        v = jnp.stack([g[name] for g in grads]).astype(F32)
        layout.append((name, v.shape))
        pieces.append(v.reshape(-1))
    flat = jnp.concatenate(pieces)
    rows = -(-flat.shape[0] // LANES)
    rows = -(-rows // 8) * 8
    flat = jnp.pad(flat, (0, rows * LANES - flat.shape[0]))
    return flat.reshape(rows, LANES), layout


def _unpack_small(table, layout):
    flat, out, off = table.reshape(-1), {}, 0
    for name, shape in layout:
        size = 1
        for d in shape:
            size *= d
        out[name] = flat[off:off + size].reshape(shape)
        off += size
    return out


def _as2d(a):
    return a.reshape(-1, a.shape[-1]) if a.ndim > 1 else a.reshape(1, -1)


def kernel(x, norm_mix_pre, w_in, conv_qkv_w, gdn_a_log, gdn_dt_bias, gdn_norm_w, conv_sc_w, w_branch, w_out, norm_mix_post, norm_ffn_pre, w_ff1, w_ff2, norm_ffn_post, loss_target, m_norm_mix_pre, m_w_in, m_conv_qkv_w, m_gdn_a_log, m_gdn_dt_bias, m_gdn_norm_w, m_conv_sc_w, m_w_branch, m_w_out, m_norm_mix_post, m_norm_ffn_pre, m_w_ff1, m_w_ff2, m_norm_ffn_post, v_norm_mix_pre, v_w_in, v_conv_qkv_w, v_gdn_a_log, v_gdn_dt_bias, v_gdn_norm_w, v_conv_sc_w, v_w_branch, v_w_out, v_norm_mix_post, v_norm_ffn_pre, v_w_ff1, v_w_ff2, v_norm_ffn_post):
    w = dict(norm_mix_pre=norm_mix_pre, w_in=w_in, conv_qkv_w=conv_qkv_w, gdn_a_log=gdn_a_log, gdn_dt_bias=gdn_dt_bias,
             gdn_norm_w=gdn_norm_w, conv_sc_w=conv_sc_w, w_branch=w_branch, w_out=w_out, norm_mix_post=norm_mix_post,
             norm_ffn_pre=norm_ffn_pre, w_ff1=w_ff1, w_ff2=w_ff2, norm_ffn_post=norm_ffn_post)
    m = dict(norm_mix_pre=m_norm_mix_pre, w_in=m_w_in, conv_qkv_w=m_conv_qkv_w, gdn_a_log=m_gdn_a_log,
             gdn_dt_bias=m_gdn_dt_bias, gdn_norm_w=m_gdn_norm_w, conv_sc_w=m_conv_sc_w, w_branch=m_w_branch, w_out=m_w_out,
             norm_mix_post=m_norm_mix_post, norm_ffn_pre=m_norm_ffn_pre, w_ff1=m_w_ff1, w_ff2=m_w_ff2,
             norm_ffn_post=m_norm_ffn_post)
    v = dict(norm_mix_pre=v_norm_mix_pre, w_in=v_w_in, conv_qkv_w=v_conv_qkv_w, gdn_a_log=v_gdn_a_log,
             gdn_dt_bias=v_gdn_dt_bias, gdn_norm_w=v_gdn_norm_w, conv_sc_w=v_conv_sc_w, w_branch=v_w_branch, w_out=v_w_out,
             norm_mix_post=v_norm_mix_post, norm_ffn_pre=v_norm_ffn_pre, w_ff1=v_w_ff1, w_ff2=v_w_ff2,
             norm_ffn_post=v_norm_ffn_post)

    me = _chip_index()

    conv = dict(zip(("conv_qkv_w", "conv_sc_w"), _gather_chips([conv_qkv_w, conv_sc_w], name="gather_conv")))
    shards = {k: _cast_bf16(_as2d(w[k]), name=f"cast_{k}").reshape(w[k].shape) for k in BIG}
    gathers = [_exchange_start([shards[k][l] for k in BIG], by_slot=False, name=f"gather_start{l}") for l in range(DEPTH)]
    started = gathers[0]["token"]
    for ex in gathers[1:]:
        started = started + ex["token"]

    def weights_of(l, x_l):
        own, lands = _exchange_wait(gathers[l], started if l == 0 else x_l, by_slot=False, name=f"gather_wait{l}")
        lands = _place_own(lands, own, by_slot=False, name="gather_own")
        p = _full_weights(dict(zip(BIG, lands)), conv, w, l)
        if l == 0:
            p["norm_mix_pre"] = p["norm_mix_pre"] + started[0, 0]
        return p

    grads, scatters = [None] * DEPTH, [None] * DEPTH

    def grads_done(l, g):
        grads[l] = g
        scatters[l] = _exchange_start(_partials_by_chip(g), by_slot=True, name=f"scatter_start{l}")
        return scatters[l]["token"][0, 0]

    loss, dx = _local_step(x[0], loss_target[0], DEPTH, weights_of, grads_done)
    loss = lax.psum(loss, ("x", "y", "c"))

    sums = []
    for l in range(DEPTH):
        parts, lands = _exchange_wait(scatters[l], dx, by_slot=True, name=f"scatter_wait{l}")
        lands = _place_own(lands, parts, by_slot=True, name="scatter_own")
        sums.append([_sum_slots(r.reshape(N_CHIPS, -1, r.shape[-1]), name=f"sum_{k}") for k, r in zip(BIG, lands)])
    mine = [jnp.concatenate([sums[l][i] for l in range(DEPTH)], axis=0) for i in range(len(BIG))]
    theirs = _swap_sibling(mine, name="swap_sibling")
    small, layout = _pack_small(grads)
    small_g = _unpack_small(_sum_slots(_gather_devices(small, name="gather_small"), name="sum_small"), layout)
    for k, width in (("conv_qkv_w", 3 * BRANCH_W // N_CHIPS), ("conv_sc_w", BRANCH_W // N_CHIPS)):
        small_g[k] = lax.dynamic_slice_in_dim(small_g[k], me * width, width, axis=2)

    out = {}
    for k, s_mine, s_theirs in zip(BIG, mine, theirs):
        res = _adamw(_as2d(w[k]), _as2d(m[k]), _as2d(v[k]), s_mine, s_theirs, name=f"adamw_{k}")
        out[k] = [r.reshape(w[k].shape) for r in res]
    for k in SMALL:
        res = _adamw(_as2d(w[k]), _as2d(m[k]), _as2d(v[k]), _as2d(small_g[k]), None, name=f"adamw_{k}")
        out[k] = [r.reshape(w[k].shape) for r in res]
    return (loss, dx[None], *[out[k][0] for k in ORDER], *[out[k][1] for k in ORDER], *[out[k][2] for k in ORDER],
            *[out[k][3] for k in ORDER])
```

```python
import functools

import jax
import jax.numpy as jnp
from jax import lax
from jax.experimental import pallas as pl
from jax.experimental.pallas import tpu as pltpu

F32 = jnp.float32
BF16 = jnp.bfloat16
MESH = pl.DeviceIdType.MESH

LANES = 128
D_MODEL = 1024
DEPTH = 4
CHUNK = 64
GDN_HEADS, GDN_DIM = 4, 128
SB_HEADS, SB_DIM = 8, 64
BRANCH_W = 512
N_BRANCH = 3
D_FF = 4 * D_MODEL
EPS = 1e-6
IN_W = 8200
AB_COL = 2048
AB_PAD = LANES - 8
IN_WP = IN_W + AB_PAD
N_CHIPS = 4
N_DEV = 8
GATES_COL = 5128
PB_GATES, PB_QKV, PB_GATE, PB_AB, PB_SB, PB_SCX, PB_SCB, PB_SCC = 0, 24, 36, 40, 41, 53, 57, 61
SB_TILE = 128
SB_GROUP = 4
SB_SCALE = SB_DIM ** -0.5
GDN_QSCALE = GDN_DIM ** -0.5
VMEM_LIMIT = 56 * 1024 * 1024

ADAM_LR, ADAM_B1, ADAM_B2, ADAM_EPS, ADAM_WD, ADAM_STEP = 0.001, 0.9, 0.999, 1e-08, 0.01, 10

NT = (((1,), (1,)), ((), ()))
TN = (((0,), (0,)), ((), ()))
HI = lax.Precision.HIGH


def _pad_in_cols(w):
    return jnp.concatenate([w[:, GATES_COL:], w[:, :AB_COL + 8], jnp.zeros((w.shape[0], AB_PAD), w.dtype),
                            w[:, AB_COL + 8:GATES_COL]], axis=1)


def _unpad_in_cols(g):
    n_gates = IN_W - GATES_COL
    return jnp.concatenate([g[:, n_gates:n_gates + AB_COL + 8], g[:, n_gates + AB_COL + 8 + AB_PAD:], g[:, :n_gates]], axis=1)


IN_SHARD = IN_W // N_CHIPS
_IN_SEGMENTS = ((0, AB_COL + 8, IN_W - GATES_COL), (AB_COL + 8, GATES_COL, IN_W - GATES_COL + AB_PAD),
                (GATES_COL, IN_W, -GATES_COL))


def _in_cols_from_chips(slots):
    def cols(first, last):
        out = []
        for j in range(N_CHIPS):
            lo, hi = max(first, j * IN_SHARD), min(last, (j + 1) * IN_SHARD)
            if lo < hi:
                out.append(slots[j][:, lo - j * IN_SHARD:hi - j * IN_SHARD])
        return out

    head, tail, gates = (cols(first, last) for first, last, _ in _IN_SEGMENTS)
    return jnp.concatenate(gates + head + [jnp.zeros((slots.shape[1], AB_PAD), slots.dtype)] + tail, axis=1)


def _in_cols_to_chips(g):
    shards = []
    for j in range(N_CHIPS):
        pieces = []
        for first, last, shift in _IN_SEGMENTS:
            lo, hi = max(first, j * IN_SHARD), min(last, (j + 1) * IN_SHARD)
            if lo < hi:
                pieces.append(g[:, lo + shift:hi + shift])
        shards.append(jnp.concatenate(pieces, axis=1))
    return jnp.stack(shards)


def _params(*sem):
    return pltpu.CompilerParams(dimension_semantics=sem if sem else None, vmem_limit_bytes=VMEM_LIMIT)


def _sigmoid(x):
    return 1.0 / (1.0 + jnp.exp(-x))


def _softplus(x):
    return jnp.maximum(x, 0.0) + jnp.log(1.0 + jnp.exp(-jnp.abs(x)))


def _dot(a, b, dims=None, precision=None):
    if dims is None:
        return jnp.dot(a, b, preferred_element_type=F32, precision=precision)
    return lax.dot_general(a, b, dims, preferred_element_type=F32, precision=precision)


def _bdot(a, b, dims=None):
    return _dot(a.astype(BF16), b.astype(BF16), dims)


def _matmul(a, b, *, name, ta=False, tb=False, tm, tn, tk=None, outs=(F32,), epi=None, extras=(), b_chips=False,
            out_chips=False):
    if ta:
        kdim, m = a.shape
    else:
        m, kdim = a.shape
    if b_chips:
        per = b.shape[2]
        if tb:
            n, kb = b.shape[1], N_CHIPS * per
        else:
            kb, n = b.shape[1], N_CHIPS * per
    elif tb:
        n, kb = b.shape
    else:
        kb, n = b.shape
    assert kdim == kb, (a.shape, b.shape)
    tk = kdim if tk is None else tk
    assert m % tm == 0 and n % tn == 0 and kdim % tk == 0, (m, n, kdim, tm, tn, tk)
    nk = kdim // tk
    a_spec = pl.BlockSpec((tk, tm), lambda i, j, k: (k, i)) if ta else pl.BlockSpec((tm, tk), lambda i, j, k: (i, k))
    if b_chips and tb:
        assert per % tk == 0
        b_spec = pl.BlockSpec((None, tn, tk), lambda i, j, k: (k // (per // tk), j, k % (per // tk)))
    elif b_chips:
        assert per % tn == 0
        b_spec = pl.BlockSpec((None, tk, tn), lambda i, j, k: (j // (per // tn), k, j % (per // tn)))
    else:
        b_spec = pl.BlockSpec((tn, tk), lambda i, j, k: (j, k)) if tb else pl.BlockSpec((tk, tn), lambda i, j, k: (k, j))
    mn_spec = pl.BlockSpec((tm, tn), lambda i, j, k: (i, j))
    if out_chips:
        per_o = n // N_CHIPS
        assert per_o % tn == 0
        out_spec = pl.BlockSpec((None, tm, tn), lambda i, j, k: (j // (per_o // tn), i, j % (per_o // tn)))
        out_dims = (N_CHIPS, m, per_o)
    else:
        out_spec, out_dims = mn_spec, (m, n)
    dims = (((0 if ta else 1,), (1 if tb else 0,)), ((), ()))
    n_ex, n_out = len(extras), len(outs)

    def body(a_ref, b_ref, *rest):
        ex, o, acc = rest[:n_ex], rest[n_ex:n_ex + n_out], rest[n_ex + n_out:]
        part = lax.dot_general(a_ref[...].astype(BF16), b_ref[...].astype(BF16), dims, preferred_element_type=F32)

        def finish(val):
            res = epi(val, *[e[...] for e in ex]) if epi is not None else (val,)
            for r, oref in zip(res, o):
                oref[...] = r.astype(oref.dtype)

        if nk == 1:
            finish(part)
        else:
            k = pl.program_id(2)

            @pl.when(k == 0)
            def _():
                acc[0][...] = part

            @pl.when(k > 0)
            def _():
                acc[0][...] += part

            @pl.when(k == nk - 1)
            def _():
                finish(acc[0][...])

    res = pl.pallas_call(
        body, name=name, grid=(m // tm, n // tn, nk),
        in_specs=[a_spec, b_spec] + [mn_spec] * n_ex,
        out_specs=[out_spec] * n_out,
        out_shape=[jax.ShapeDtypeStruct(out_dims, dt) for dt in outs],
        scratch_shapes=[pltpu.VMEM((tm, tn), F32)] if nk > 1 else [],
        compiler_params=_params("parallel", "parallel", "arbitrary"),
    )(a, b, *extras)
    return res[0] if n_out == 1 else res


ROW_TILE = 512


def _norm_fwd(x, w, *, name):
    s, d = x.shape

    def body(x_ref, w_ref, o_ref):
        xv = x_ref[...]
        r = lax.rsqrt(jnp.mean(xv * xv, axis=-1, keepdims=True) + EPS)
        o_ref[...] = (xv * r * w_ref[...]).astype(o_ref.dtype)

    return pl.pallas_call(
        body, name=name, grid=(s // ROW_TILE,),
        in_specs=[pl.BlockSpec((ROW_TILE, d), lambda i: (i, 0)), pl.BlockSpec((1, d), lambda i: (0, 0))],
        out_specs=pl.BlockSpec((ROW_TILE, d), lambda i: (i, 0)),
        out_shape=jax.ShapeDtypeStruct((s, d), BF16), compiler_params=_params("parallel"),
    )(x, w.reshape(1, d))


def _resnorm_fwd(x, u, w, *, name):
    s, d = x.shape

    def body(x_ref, u_ref, w_ref, o_ref):
        uv = u_ref[...]
        r = lax.rsqrt(jnp.mean(uv * uv, axis=-1, keepdims=True) + EPS)
        o_ref[...] = x_ref[...] + uv * r * w_ref[...]

    row = pl.BlockSpec((ROW_TILE, d), lambda i: (i, 0))
    return pl.pallas_call(
        body, name=name, grid=(s // ROW_TILE,),
        in_specs=[row, row, pl.BlockSpec((1, d), lambda i: (0, 0))], out_specs=row,
        out_shape=jax.ShapeDtypeStruct((s, d), F32), compiler_params=_params("parallel"),
    )(x, u, w.reshape(1, d))


def _norm_bwd(xin, w, dy, res, *, out_dtype, name):
    s, d = xin.shape
    has_res = res is not None

    def body(*refs):
        x_ref, w_ref, dy_ref = refs[:3]
        res_ref = refs[3] if has_res else None
        dx_ref, dw_ref = refs[3 + has_res:]
        xv, dyv = x_ref[...], dy_ref[...].astype(F32)
        r = lax.rsqrt(jnp.mean(xv * xv, axis=-1, keepdims=True) + EPS)
        xh = xv * r
        g = dyv * w_ref[...]
        dx = r * (g - xh * jnp.mean(g * xh, axis=-1, keepdims=True))
        if has_res:
            dx = dx + res_ref[...]
        dx_ref[...] = dx.astype(dx_ref.dtype)

        @pl.when(pl.program_id(0) == 0)
        def _():
            dw_ref[...] = jnp.zeros_like(dw_ref)

        dw_ref[...] += jnp.sum(dyv * xh, axis=0, keepdims=True)

    row = pl.BlockSpec((ROW_TILE, d), lambda i: (i, 0))
    vec = pl.BlockSpec((1, d), lambda i: (0, 0))
    ins = [xin, w.reshape(1, d), dy] + ([res] if has_res else [])
    dx, dw = pl.pallas_call(
        body, name=name, grid=(s // ROW_TILE,),
        in_specs=[row, vec, row] + ([row] if has_res else []), out_specs=[row, vec],
        out_shape=[jax.ShapeDtypeStruct((s, d), out_dtype), jax.ShapeDtypeStruct((1, d), F32)],
        compiler_params=_params("arbitrary"),
    )(*ins)
    return dx, dw.reshape(d)


def _loss_fwd_bwd(y, target, *, name):
    s, d = y.shape

    def body(y_ref, t_ref, loss_ref, dy_ref):
        e = y_ref[...] - t_ref[...]
        dy_ref[...] = e * (1.0 / d)

        @pl.when(pl.program_id(0) == 0)
        def _():
            loss_ref[...] = jnp.zeros_like(loss_ref)

        part = jnp.sum(jnp.sum(e * e, axis=1, keepdims=True), axis=0, keepdims=True)
        loss_ref[...] += part * (0.5 / d)

    row = pl.BlockSpec((ROW_TILE, d), lambda i: (i, 0))
    loss, dy = pl.pallas_call(
        body, name=name, grid=(s // ROW_TILE,), in_specs=[row, row],
        out_specs=[pl.BlockSpec((1, 1), lambda i: (0, 0)), row],
        out_shape=[jax.ShapeDtypeStruct((1, 1), F32), jax.ShapeDtypeStruct((s, d), F32)],
        compiler_params=_params("arbitrary"),
    )(y, target)
    return loss[0, 0], dy


def _shift_down(x, k, rows):
    if k == 0:
        return x
    return jnp.where(rows >= k, pltpu.roll(x, k, 0), 0.0)


def _shift_up(x, k, rows):
    if k == 0:
        return x
    n = x.shape[0]
    return jnp.where(rows < n - k, pltpu.roll(x, n - k, 0), 0.0)


def _col_spec(s, base):
    return pl.BlockSpec((s, LANES), lambda j: (0, base + j))


def _gdn_pre_math(x, w, j, rows):
    taps = w.shape[0]
    c = w[taps - 1:taps, :] * x
    for i in range(taps - 1):
        c = c + w[i:i + 1, :] * _shift_down(x, taps - 1 - i, rows)
    sg = _sigmoid(c)
    y = c * sg
    r = lax.rsqrt(jnp.sum(y * y, axis=-1, keepdims=True) + EPS)
    is_qk = j < 2 * GDN_HEADS
    scale = jnp.where(j < GDN_HEADS, GDN_QSCALE, 1.0)
    return c, sg, y, r, is_qk, scale


def _gdn_pre_fwd(proj, conv_w, *, name):
    s = proj.shape[0]

    def body(x_ref, w_ref, o_ref):
        j = pl.program_id(0)
        rows = lax.broadcasted_iota(jnp.int32, (s, LANES), 0)
        _, _, y, r, is_qk, scale = _gdn_pre_math(x_ref[...], w_ref[...], j, rows)
        o_ref[...] = jnp.where(is_qk, y * (r * scale), y)

    return pl.pallas_call(
        body, name=name, grid=(12,),
        in_specs=[_col_spec(s, PB_QKV), pl.BlockSpec((4, LANES), lambda j: (0, j))],
        out_specs=_col_spec(s, 0), out_shape=jax.ShapeDtypeStruct((s, 3 * BRANCH_W), F32),
        compiler_params=_params("parallel"),
    )(proj, conv_w)


def _gdn_pre_bwd(proj, conv_w, dqkvn, *, name):
    s = proj.shape[0]

    def body(x_ref, w_ref, d_ref, dx_ref, dw_ref):
        j = pl.program_id(0)
        rows = lax.broadcasted_iota(jnp.int32, (s, LANES), 0)
        x, w, dout = x_ref[...], w_ref[...], d_ref[...]
        c, sg, y, r, is_qk, scale = _gdn_pre_math(x, w, j, rows)
        yh = y * r
        dy_n = (scale * r) * (dout - yh * jnp.sum(dout * yh, axis=-1, keepdims=True))
        dy = jnp.where(is_qk, dy_n, dout)
        dc = dy * (sg * (1.0 + c * (1.0 - sg)))
        taps = w.shape[0]
        dx = w[taps - 1:taps, :] * dc
        dws = []
        for i in range(taps - 1):
            k = taps - 1 - i
            dx = dx + w[i:i + 1, :] * _shift_up(dc, k, rows)
            dws.append(jnp.sum(dc * _shift_down(x, k, rows), axis=0, keepdims=True))
        dws.append(jnp.sum(dc * x, axis=0, keepdims=True))
        dx_ref[...] = dx.astype(dx_ref.dtype)
        for i in range(taps):
            dw_ref[i:i + 1, :] = dws[i]

    return pl.pallas_call(
        body, name=name, grid=(12,),
        in_specs=[_col_spec(s, PB_QKV), pl.BlockSpec((4, LANES), lambda j: (0, j)), _col_spec(s, 0)],
        out_specs=[_col_spec(s, 0), pl.BlockSpec((4, LANES), lambda j: (0, j))],
        out_shape=[jax.ShapeDtypeStruct((s, 3 * BRANCH_W), BF16), jax.ShapeDtypeStruct((4, 3 * BRANCH_W), F32)],
        compiler_params=_params("parallel"),
    )(proj, conv_w, dqkvn)


def _lane_pad(v):
    return jnp.pad(v.reshape(1, -1), ((0, 0), (0, LANES - v.shape[0])))


def _gdn_gates_fwd(proj, a_log, dt_bias, *, name):
    s = proj.shape[0]

    def body(ab_ref, al_ref, dt_ref, o_ref):
        ab = ab_ref[...]
        lane = lax.broadcasted_iota(jnp.int32, (1, LANES), 1)
        g = -jnp.exp(al_ref[...]) * _softplus(ab + dt_ref[...])
        o_ref[...] = jnp.where(lane < GDN_HEADS, g, _sigmoid(ab))

    vec = pl.BlockSpec((1, LANES), lambda j: (0, 0))
    return pl.pallas_call(
        body, name=name, grid=(1,), in_specs=[_col_spec(s, PB_AB), vec, vec], out_specs=_col_spec(s, 0),
        out_shape=jax.ShapeDtypeStruct((s, LANES), F32), compiler_params=_params("arbitrary"),
    )(proj, _lane_pad(a_log), _lane_pad(dt_bias))


def _gdn_gates_bwd(proj, a_log, dt_bias, dgb, *, name):
    s = proj.shape[0]

    def body(ab_ref, al_ref, dt_ref, d_ref, dab_ref, dal_ref, ddt_ref):
        ab, d = ab_ref[...], d_ref[...]
        lane = lax.broadcasted_iota(jnp.int32, (1, LANES), 1)
        ea = jnp.exp(al_ref[...])
        pre = ab + dt_ref[...]
        g = -ea * _softplus(pre)
        dpre = d * (-ea) * _sigmoid(pre)
        beta = _sigmoid(ab)
        is_g = lane < GDN_HEADS
        dab = jnp.where(is_g, dpre, jnp.where(lane < 2 * GDN_HEADS, d * beta * (1.0 - beta), 0.0))
        dab_ref[...] = dab.astype(dab_ref.dtype)
        dal_ref[...] = jnp.sum(jnp.where(is_g, d * g, 0.0), axis=0, keepdims=True)
        ddt_ref[...] = jnp.sum(jnp.where(is_g, dpre, 0.0), axis=0, keepdims=True)

    vec = pl.BlockSpec((1, LANES), lambda j: (0, 0))
    dab, dal, ddt = pl.pallas_call(
        body, name=name, grid=(1,), in_specs=[_col_spec(s, PB_AB), vec, vec, _col_spec(s, 0)],
        out_specs=[_col_spec(s, 0), vec, vec],
        out_shape=[jax.ShapeDtypeStruct((s, LANES), BF16), jax.ShapeDtypeStruct((1, LANES), F32),
                   jax.ShapeDtypeStruct((1, LANES), F32)],
        compiler_params=_params("arbitrary"),
    )(proj, _lane_pad(a_log), _lane_pad(dt_bias), dgb)
    return dab, dal[0, :GDN_HEADS], ddt[0, :GDN_HEADS]


def _interleave(gens):
    results, live = [None] * len(gens), list(range(len(gens)))
    while live:
        for idx in list(live):
            try:
                next(gens[idx])
            except StopIteration as done:
                results[idx] = done.value
                live.remove(idx)
    return results


def _chunk_common(q, k, v, gb, gbt, h):
    c = CHUNK
    row = lax.broadcasted_iota(jnp.int32, (c, c), 0)
    col = lax.broadcasted_iota(jnp.int32, (c, c), 1)
    tril, strict, eye = row >= col, row > col, row == col
    lane = lax.broadcasted_iota(jnp.int32, (c, LANES), 1)
    sub = lax.broadcasted_iota(jnp.int32, (2 * GDN_HEADS, c), 0)
    g_col = jnp.sum(jnp.where(lane == h, gb, 0.0), axis=1, keepdims=True)
    beta_col = jnp.sum(jnp.where(lane == GDN_HEADS + h, gb, 0.0), axis=1, keepdims=True)
    g_row = jnp.sum(jnp.where(sub == h, gbt, 0.0), axis=0, keepdims=True)
    gc_col = jnp.sum(jnp.where(tril, jnp.broadcast_to(g_row, (c, c)), 0.0), axis=1, keepdims=True)
    gc_row = jnp.sum(jnp.where(row <= col, jnp.broadcast_to(g_col, (c, c)), 0.0), axis=0, keepdims=True)
    g_tot = jnp.sum(g_row, axis=1, keepdims=True)
    dm = jnp.exp(jnp.where(tril, gc_col - gc_row, -1e30))
    e_col = jnp.exp(gc_col)
    kdec_col = jnp.exp(g_tot - gc_col)
    gamma = jnp.exp(g_tot)
    kb = k * beta_col
    vb = v * beta_col
    kbg = kb * e_col
    kk = _dot(kb, k, NT, HI)
    qk = _bdot(q, k, NT)
    yield
    a = jnp.where(strict, kk * dm, 0.0)
    aqk = jnp.where(tril, qk * dm, 0.0)
    bneg = -a
    t = jnp.where(eye, 1.0, 0.0) + bneg
    p = _dot(bneg, bneg, precision=HI)
    yield
    for lvl in range(5):
        t_next = t + _dot(t, p, precision=HI)
        if lvl < 4:
            p = _dot(p, p, precision=HI)
        t = t_next
        yield
    u = _dot(t, vb, precision=HI)
    w = _dot(t, kbg, precision=HI)
    yield
    return dict(tril=tril, strict=strict, eye=eye, row=row, col=col, beta_col=beta_col, dm=dm, e_col=e_col,
                kdec_col=kdec_col, gamma=gamma, kb=kb, vb=vb, kbg=kbg, a=a, t=t, u=u, w=w, aqk=aqk,
                qd=q * e_col, kd=k * kdec_col)


def _gdn_chunk_fwd(qkvn, gb, gbt, *, name):
    s = qkvn.shape[0]
    n_chunks = s // CHUNK

    def body(q_ref, k_ref, v_ref, gb_ref, gbt_ref, o_ref, st_ref, state):
        @pl.when(pl.program_id(0) == 0)
        def _():
            state[...] = jnp.zeros_like(state)

        gbv, gbtv = gb_ref[...], gbt_ref[0]

        def head(h):
            hs = slice(h * GDN_DIM, (h + 1) * GDN_DIM)
            q, k, v = q_ref[:, hs], k_ref[:, hs], v_ref[:, hs]
            m = yield from _chunk_common(q, k, v, gbv, gbtv, h)
            s0 = state[h]
            st_ref[0, h] = s0
            vnew = m["u"] - _bdot(m["w"], s0)
            o_inter = _bdot(m["qd"], s0)
            yield
            o_ref[:, hs] = o_inter + _bdot(m["aqk"], vnew)
            state[h] = m["gamma"] * s0 + _bdot(m["kd"], vnew, TN)

        _interleave([head(h) for h in range(GDN_HEADS)])

    blk = lambda j: pl.BlockSpec((CHUNK, BRANCH_W), lambda n: (n, j))
    return pl.pallas_call(
        body, name=name, grid=(n_chunks,),
        in_specs=[blk(0), blk(1), blk(2), pl.BlockSpec((CHUNK, LANES), lambda n: (n, 0)),
                  pl.BlockSpec((1, 2 * GDN_HEADS, CHUNK), lambda n: (n, 0, 0))],
        out_specs=[blk(0), pl.BlockSpec((1, GDN_HEADS, GDN_DIM, GDN_DIM), lambda n: (n, 0, 0, 0))],
        out_shape=[jax.ShapeDtypeStruct((s, BRANCH_W), F32),
                   jax.ShapeDtypeStruct((n_chunks, GDN_HEADS, GDN_DIM, GDN_DIM), F32)],
        scratch_shapes=[pltpu.VMEM((GDN_HEADS, GDN_DIM, GDN_DIM), F32)],
        compiler_params=_params("arbitrary"),
    )(qkvn, qkvn, qkvn, gb, gbt)


def _gdn_chunk_bwd(qkvn, gb, gbt, states, do, *, name):
    s = qkvn.shape[0]
    n_chunks = s // CHUNK
    c = CHUNK

    def body(q_ref, k_ref, v_ref, gb_ref, gbt_ref, st_ref, do_ref, dq_ref, dk_ref, dv_ref, dgb_ref, dstate):
        @pl.when(pl.program_id(0) == 0)
        def _():
            dstate[...] = jnp.zeros_like(dstate)

        gbv, gbtv = gb_ref[...], gbt_ref[0]
        lane = lax.broadcasted_iota(jnp.int32, (c, LANES), 1)
        def head(h):
            hs = slice(h * GDN_DIM, (h + 1) * GDN_DIM)
            q, k, v, dov = q_ref[:, hs], k_ref[:, hs], v_ref[:, hs], do_ref[:, hs]
            m = yield from _chunk_common(q, k, v, gbv, gbtv, h)
            tril, strict, eye, row, col = m["tril"], m["strict"], m["eye"], m["row"], m["col"]
            s0, ds1 = st_ref[0, h], dstate[h]
            vnew = m["u"] - _bdot(m["w"], s0)
            dvnew_a = _bdot(m["aqk"], dov, TN) + _bdot(m["kd"], ds1)
            dqd = _bdot(dov, s0, NT)
            ds_q = _bdot(m["qd"], dov, TN)
            dgamma = jnp.sum(jnp.sum(s0 * ds1, axis=1, keepdims=True), axis=0, keepdims=True)
            yield
            dvnew = dvnew_a
            daqk = jnp.where(tril, _bdot(dov, vnew, NT), 0.0)
            dkd = _bdot(vnew, ds1, NT)
            dw = -_bdot(dvnew, s0, NT)
            dstate[h] = m["gamma"] * ds1 + ds_q - _bdot(m["w"], dvnew, TN)
            dvb = _dot(m["t"], dvnew, TN, HI)
            yield
            dt = _dot(dvnew, m["vb"], NT, HI) + _dot(dw, m["kbg"], NT, HI)
            dkbg = _dot(m["t"], dw, TN, HI)
            dmq = daqk * m["dm"]
            dq = _bdot(dmq, k) + dqd * m["e_col"]
            dk_q = _bdot(dmq, q, TN)
            yield
            tdt = _dot(m["t"], dt, TN, HI)
            yield
            da = jnp.where(strict, -_dot(tdt, m["t"], NT, HI), 0.0)
            yield
            dmat = da * m["dm"]
            dkb = _dot(dmat, k, precision=HI) + dkbg * m["e_col"]
            dk = (_dot(dmat, m["kb"], TN, HI) + dk_q + dkd * m["kdec_col"] + m["beta_col"] * dkb)
            yield
            dbeta_col = jnp.sum(dkb * k, axis=1, keepdims=True) + jnp.sum(dvb * v, axis=1, keepdims=True)
            e = da * m["a"] + daqk * m["aqk"]
            rs_kd = jnp.sum(dkd * m["kd"], axis=1, keepdims=True)
            e_colsum = jnp.sum(e, axis=0, keepdims=True)
            e_colsum_c = jnp.sum(jnp.where(eye, jnp.broadcast_to(e_colsum, (c, c)), 0.0), axis=1, keepdims=True)
            dgc = (jnp.sum(e, axis=1, keepdims=True) - e_colsum_c + jnp.sum(dqd * m["qd"], axis=1, keepdims=True)
                   - rs_kd + jnp.sum(dkbg * m["kbg"], axis=1, keepdims=True))
            last = jnp.sum(rs_kd, axis=0, keepdims=True) + dgamma * m["gamma"]
            dgc = dgc + jnp.where(lax.broadcasted_iota(jnp.int32, (c, 1), 0) == c - 1, last, 0.0)
            dgc_row = jnp.sum(jnp.where(eye, jnp.broadcast_to(dgc, (c, c)), 0.0), axis=0, keepdims=True)
            dg_col = jnp.sum(jnp.where(col >= row, jnp.broadcast_to(dgc_row, (c, c)), 0.0), axis=1, keepdims=True)
            dq_ref[:, hs] = dq
            dk_ref[:, hs] = dk
            dv_ref[:, hs] = m["beta_col"] * dvb
            return jnp.where(lane == h, dg_col, 0.0) + jnp.where(lane == GDN_HEADS + h, dbeta_col, 0.0)

        parts = _interleave([head(h) for h in range(GDN_HEADS)])
        dgb_ref[...] = (parts[0] + parts[1]) + (parts[2] + parts[3])

    rev = lambda n: n_chunks - 1 - n
    blk = lambda j: pl.BlockSpec((CHUNK, BRANCH_W), lambda n: (rev(n), j))
    dq, dk, dv, dgb = pl.pallas_call(
        body, name=name, grid=(n_chunks,),
        in_specs=[blk(0), blk(1), blk(2), pl.BlockSpec((CHUNK, LANES), lambda n: (rev(n), 0)),
                  pl.BlockSpec((1, 2 * GDN_HEADS, CHUNK), lambda n: (rev(n), 0, 0)),
                  pl.BlockSpec((1, GDN_HEADS, GDN_DIM, GDN_DIM), lambda n: (rev(n), 0, 0, 0)), blk(0)],
        out_specs=[blk(0), blk(0), blk(0), pl.BlockSpec((CHUNK, LANES), lambda n: (rev(n), 0))],
        out_shape=[jax.ShapeDtypeStruct((s, BRANCH_W), F32)] * 3 + [jax.ShapeDtypeStruct((s, LANES), F32)],
        scratch_shapes=[pltpu.VMEM((GDN_HEADS, GDN_DIM, GDN_DIM), F32)],
        compiler_params=_params("arbitrary"),
    )(qkvn, qkvn, qkvn, gb, gbt, states, do)
    return jnp.concatenate([dq, dk, dv], axis=1), dgb


def _gdn_post_fwd(o, proj, norm_w, *, name):
    s = o.shape[0]

    def body(o_ref, g_ref, w_ref, y_ref):
        ov, gv = o_ref[...], g_ref[...]
        r = lax.rsqrt(jnp.mean(ov * ov, axis=-1, keepdims=True) + EPS)
        y_ref[...] = (ov * r * w_ref[...] * (gv * _sigmoid(gv))).astype(y_ref.dtype)

    return pl.pallas_call(
        body, name=name, grid=(GDN_HEADS,),
        in_specs=[_col_spec(s, 0), _col_spec(s, PB_GATE), pl.BlockSpec((1, LANES), lambda j: (0, 0))],
        out_specs=_col_spec(s, 0), out_shape=jax.ShapeDtypeStruct((s, BRANCH_W), BF16),
        compiler_params=_params("parallel"),
    )(o, proj, norm_w.reshape(1, GDN_DIM))


def _gdn_post_bwd(o, proj, norm_w, dy, *, name):
    s = o.shape[0]

    def body(o_ref, g_ref, w_ref, dy_ref, do_ref, dg_ref, dw_ref):
        ov, gv, w, dyv = o_ref[...], g_ref[...], w_ref[...], dy_ref[...].astype(F32)
        r = lax.rsqrt(jnp.mean(ov * ov, axis=-1, keepdims=True) + EPS)
        oh = ov * r
        sg = _sigmoid(gv)
        silu = gv * sg
        dn = dyv * silu
        dg_ref[...] = (dyv * (oh * w) * (sg * (1.0 + gv * (1.0 - sg)))).astype(dg_ref.dtype)

        @pl.when(pl.program_id(0) == 0)
        def _():
            dw_ref[...] = jnp.zeros_like(dw_ref)

        dw_ref[...] += jnp.sum(dn * oh, axis=0, keepdims=True)
        g2 = dn * w
        do_ref[...] = r * (g2 - oh * jnp.mean(g2 * oh, axis=-1, keepdims=True))

    do, dg, dw = pl.pallas_call(
        body, name=name, grid=(GDN_HEADS,),
        in_specs=[_col_spec(s, 0), _col_spec(s, PB_GATE), pl.BlockSpec((1, LANES), lambda j: (0, 0)), _col_spec(s, 0)],
        out_specs=[_col_spec(s, 0), _col_spec(s, 0), pl.BlockSpec((1, LANES), lambda j: (0, 0))],
        out_shape=[jax.ShapeDtypeStruct((s, BRANCH_W), F32), jax.ShapeDtypeStruct((s, BRANCH_W), BF16),
                   jax.ShapeDtypeStruct((1, LANES), F32)],
        compiler_params=_params("arbitrary"),
    )(o, proj, norm_w.reshape(1, GDN_DIM), dy)
    return do, dg, dw.reshape(LANES)


def _split_terms(x):
    hi = x.astype(BF16)
    lo = (x - hi.astype(F32)).astype(BF16)
    return jnp.concatenate([hi, lo], axis=1)


def _sb_sum_matrix(pred):
    row = lax.broadcasted_iota(jnp.int32, (2 * SB_TILE, 2 * SB_TILE), 0) % SB_TILE
    col = lax.broadcasted_iota(jnp.int32, (2 * SB_TILE, 2 * SB_TILE), 1)
    return jnp.where((col >= SB_TILE) | pred(row, col), 1.0, 0.0).astype(BF16)


def _sb_head_masks():
    lane = lax.broadcasted_iota(jnp.int32, (1, LANES), 1)
    return [(lane < SB_DIM).astype(F32), (lane >= SB_DIM).astype(F32)]


def _sb_fwd(proj, *, name):
    s = proj.shape[0]
    t = SB_TILE
    nq = s // t

    def body(q_ref, k_ref, v_ref, o_ref, tot_ref):
        cmr = lax.broadcasted_iota(jnp.int32, (t, t), 1) - lax.broadcasted_iota(jnp.int32, (t, t), 0)
        uo = _sb_sum_matrix(lambda row, col: row > col)
        hm = _sb_head_masks()

        def qloop(i, carry0):
            qs = pl.multiple_of(i * t, t)
            qf = q_ref[pl.ds(qs, t), :] * SB_SCALE
            qh = [(qf * hm[h]).astype(BF16) for h in range(2)]
            diag = i // SB_GROUP

            def group(g, st, masked):
                ks = pl.multiple_of(g * (SB_GROUP * t), SB_GROUP * t)
                kb = k_ref[pl.ds(ks, SB_GROUP * t), :].astype(BF16)
                vf = v_ref[pl.ds(ks, SB_GROUP * t), :]
                tiles = [(h, j) for h in range(2) for j in range(SB_GROUP)]
                z = [_dot(qh[h], kb, NT) for h in range(2)]
                keep = {j: cmr < (i - g * SB_GROUP - j) * t for j in range(SB_GROUP)} if masked else None
                base, terms = {}, {}
                for h, j in tiles:
                    zj = z[h][:, j * t:(j + 1) * t]
                    sp = _softplus(zj)
                    l1m = jnp.where(keep[j], -sp, 0.0) if masked else -sp
                    base[h, j] = zj - sp
                    terms[h, j] = _split_terms(l1m)
                sums = {hj: _dot(terms[hj], uo) for hj in tiles}
                acc, new = st[0], []
                for h in range(2):
                    run, att = st[1 + h], [None] * SB_GROUP
                    for j in reversed(range(SB_GROUP)):
                        a = jnp.exp(base[h, j] + sums[h, j][:, :t] + run)
                        att[j] = (jnp.where(keep[j], a, 0.0) if masked else a).astype(BF16)
                        run = run + sums[h, j][:, t:]
                    acc = acc + _dot(jnp.concatenate(att, axis=1), (vf * hm[h]).astype(BF16))
                    new.append(run)
                return (acc, *new)

            zero = jnp.zeros((t, LANES), F32)
            st = group(diag, (zero, zero, zero), True)
            st = lax.fori_loop(0, diag, lambda jj, sv: group(diag - 1 - jj, sv, False), st)
            o_ref[pl.ds(qs, t), :] = st[0]
            tot_ref[pl.ds(qs, t), :] = st[1] * hm[0] + st[2] * hm[1]
            return carry0

        lax.fori_loop(0, nq, qloop, 0)

    out = jax.ShapeDtypeStruct((s, BRANCH_W), F32)
    return pl.pallas_call(
        body, name=name, grid=(SB_HEADS // 2,),
        in_specs=[_col_spec(s, PB_SB), _col_spec(s, PB_SB + 4), _col_spec(s, PB_SB + 8)],
        out_specs=[_col_spec(s, 0)] * 2, out_shape=[out] * 2,
        compiler_params=_params("parallel"),
    )(proj, proj, proj)


def _sb_bwd(proj, tot, do, *, name):
    s = proj.shape[0]
    t = SB_TILE
    nq = s // t

    def body(q_ref, k_ref, v_ref, tot_ref, do_ref, dq_ref, dk_ref, dv_ref, dk_acc, dv_acc):
        dk_acc[...] = jnp.zeros_like(dk_acc)
        dv_acc[...] = jnp.zeros_like(dv_acc)
        cmr = lax.broadcasted_iota(jnp.int32, (t, t), 1) - lax.broadcasted_iota(jnp.int32, (t, t), 0)
        u_le = _sb_sum_matrix(lambda row, col: row <= col)
        u_lt = _sb_sum_matrix(lambda row, col: row < col)
        hm = _sb_head_masks()

        def qloop(i, carry0):
            qs = pl.multiple_of(i * t, t)
            qraw = q_ref[pl.ds(qs, t), :]
            dov = do_ref[pl.ds(qs, t), :].astype(F32)
            totv = tot_ref[pl.ds(qs, t), :]
            qh = [(qraw * (hm[h] * SB_SCALE)).astype(BF16) for h in range(2)]
            q2 = jnp.concatenate([(qraw * hm[h]).astype(BF16) for h in range(2)], axis=0)
            doh = [(dov * hm[h]).astype(BF16) for h in range(2)]
            do2 = jnp.concatenate(doh, axis=0)
            tot = [jnp.min(totv * hm[h], axis=1, keepdims=True) for h in range(2)]
            diag = i // SB_GROUP

            def group(g, st, masked):
                ks = pl.multiple_of(g * (SB_GROUP * t), SB_GROUP * t)
                kf = k_ref[pl.ds(ks, SB_GROUP * t), :]
                kb = kf.astype(BF16)
                vb = v_ref[pl.ds(ks, SB_GROUP * t), :].astype(BF16)
                tiles = [(h, j) for h in range(2) for j in range(SB_GROUP)]
                z = [_dot(qh[h], kb, NT) for h in range(2)]
                datt = [_dot(doh[h], vb, NT) for h in range(2)]
                keep = {j: cmr < (i - g * SB_GROUP - j) * t for j in range(SB_GROUP)} if masked else None
                ls, lterms = {}, {}
                for h, j in tiles:
                    zj = z[h][:, j * t:(j + 1) * t]
                    sp = _softplus(zj)
                    ls[h, j] = zj - sp
                    lterms[h, j] = _split_terms(jnp.where(keep[j], -sp, 0.0) if masked else -sp)
                lsum = {hj: _dot(lterms[hj], u_le) for hj in tiles}
                att, p, pterms, new_c = {}, {}, {}, []
                for h in range(2):
                    run = st[1 + h]
                    for j in range(SB_GROUP):
                        a = jnp.exp(ls[h, j] + ((tot[h] - run) - lsum[h, j][:, :t]))
                        if masked:
                            a = jnp.where(keep[j], a, 0.0)
                        att[h, j] = a.astype(BF16)
                        p[h, j] = a * datt[h][:, j * t:(j + 1) * t]
                        pterms[h, j] = _split_terms(p[h, j])
                        run = run + lsum[h, j][:, t:]
                    new_c.append(run)
                psum = {hj: _dot(pterms[hj], u_lt) for hj in tiles}
                dzb, new_r = {}, []
                for h in range(2):
                    run = st[3 + h]
                    for j in range(SB_GROUP):
                        sig = jnp.exp(ls[h, j])
                        dz = p[h, j] * (1.0 - sig) - (run + psum[h, j][:, :t]) * sig
                        if masked:
                            dz = jnp.where(keep[j], dz, 0.0)
                        dzb[h, j] = (dz * SB_SCALE).astype(BF16)
                        run = run + psum[h, j][:, t:]
                    new_r.append(run)
                k2 = jnp.concatenate([(kf * hm[h]).astype(BF16) for h in range(2)], axis=0)
                dq_acc = st[0] + _dot(jnp.concatenate([dzb[hj] for hj in tiles], axis=1), k2)
                for j in range(SB_GROUP):
                    rows = pl.ds(pl.multiple_of(ks + j * t, t), t)
                    dk_acc[rows, :] += _dot(jnp.concatenate([dzb[0, j], dzb[1, j]], axis=0), q2, TN)
                    dv_acc[rows, :] += _dot(jnp.concatenate([att[0, j], att[1, j]], axis=0), do2, TN)
                return (dq_acc, *new_c, *new_r)

            zero = jnp.zeros((t, LANES), F32)
            st = lax.fori_loop(0, diag, lambda jj, sv: group(jj, sv, False), (zero,) * 5)
            st = group(diag, st, True)
            dq_ref[pl.ds(qs, t), :] = st[0].astype(dq_ref.dtype)
            return carry0

        lax.fori_loop(0, nq, qloop, 0)
        dk_ref[...] = dk_acc[...].astype(dk_ref.dtype)
        dv_ref[...] = dv_acc[...].astype(dv_ref.dtype)

    out = jax.ShapeDtypeStruct((s, BRANCH_W), BF16)
    return pl.pallas_call(
        body, name=name, grid=(SB_HEADS // 2,),
        in_specs=[_col_spec(s, PB_SB), _col_spec(s, PB_SB + 4), _col_spec(s, PB_SB + 8), _col_spec(s, 0), _col_spec(s, 0)],
        out_specs=[_col_spec(s, 0)] * 3, out_shape=[out] * 3,
        scratch_shapes=[pltpu.VMEM((s, LANES), F32), pltpu.VMEM((s, LANES), F32)],
        compiler_params=_params("parallel"),
    )(proj, proj, proj, tot, do)


def _sc_fwd(proj, conv_w, *, name):
    s = proj.shape[0]

    def body(x_ref, b_ref, c_ref, w_ref, y_ref):
        rows = lax.broadcasted_iota(jnp.int32, (s, LANES), 0)
        w = w_ref[...]
        u = c_ref[...] * x_ref[...]
        cv = w[2:3, :] * u + w[1:2, :] * _shift_down(u, 1, rows) + w[0:1, :] * _shift_down(u, 2, rows)
        y_ref[...] = (b_ref[...] * cv).astype(y_ref.dtype)

    return pl.pallas_call(
        body, name=name, grid=(BRANCH_W // LANES,),
        in_specs=[_col_spec(s, PB_SCX), _col_spec(s, PB_SCB), _col_spec(s, PB_SCC), pl.BlockSpec((3, LANES), lambda j: (0, j))],
        out_specs=_col_spec(s, 0), out_shape=jax.ShapeDtypeStruct((s, BRANCH_W), BF16),
        compiler_params=_params("parallel"),
    )(proj, proj, proj, conv_w)


def _sc_bwd(proj, conv_w, dy, *, name):
    s = proj.shape[0]

    def body(x_ref, b_ref, c_ref, w_ref, dy_ref, dx_ref, db_ref, dc_ref, dw_ref):
        rows = lax.broadcasted_iota(jnp.int32, (s, LANES), 0)
        w, x, cg, dyv = w_ref[...], x_ref[...], c_ref[...], dy_ref[...].astype(F32)
        u = cg * x
        u1, u2 = _shift_down(u, 1, rows), _shift_down(u, 2, rows)
        cv = w[2:3, :] * u + w[1:2, :] * u1 + w[0:1, :] * u2
        db_ref[...] = (dyv * cv).astype(db_ref.dtype)
        dcv = dyv * b_ref[...]
        du = w[2:3, :] * dcv + w[1:2, :] * _shift_up(dcv, 1, rows) + w[0:1, :] * _shift_up(dcv, 2, rows)
        dx_ref[...] = (du * cg).astype(dx_ref.dtype)
        dc_ref[...] = (du * x).astype(dc_ref.dtype)
        dw_ref[0:1, :] = jnp.sum(dcv * u2, axis=0, keepdims=True)
        dw_ref[1:2, :] = jnp.sum(dcv * u1, axis=0, keepdims=True)
        dw_ref[2:3, :] = jnp.sum(dcv * u, axis=0, keepdims=True)

    out = jax.ShapeDtypeStruct((s, BRANCH_W), BF16)
    wspec = pl.BlockSpec((3, LANES), lambda j: (0, j))
    return pl.pallas_call(
        body, name=name, grid=(BRANCH_W // LANES,),
        in_specs=[_col_spec(s, PB_SCX), _col_spec(s, PB_SCB), _col_spec(s, PB_SCC), wspec, _col_spec(s, 0)],
        out_specs=[_col_spec(s, 0)] * 3 + [wspec],
        out_shape=[out] * 3 + [jax.ShapeDtypeStruct((3, BRANCH_W), F32)],
        compiler_params=_params("parallel"),
    )(proj, proj, proj, conv_w, dy)


MERGE_TM, MERGE_TN = 512, D_MODEL // N_CHIPS


def _merge_specs():
    tm, tn = MERGE_TM, MERGE_TN
    y_spec = pl.BlockSpec((tm, BRANCH_W), lambda i, j: (i, 0))
    w_spec = pl.BlockSpec((None, N_BRANCH, BRANCH_W, tn), lambda i, j: (j, 0, 0, 0))
    gate_specs = [pl.BlockSpec((tm, tn), functools.partial(
        lambda i, j, b: (i, (PB_GATES * LANES + b * D_MODEL) // tn + j), b=b)) for b in range(N_BRANCH)]
    mn = pl.BlockSpec((tm, tn), lambda i, j: (i, j))
    return y_spec, w_spec, gate_specs, mn


def _merge_fwd(ya, yb, yc, wb, proj, *, name):
    s = ya.shape[0]
    y_spec, w_spec, gate_specs, mn = _merge_specs()

    def body(ya_ref, yb_ref, yc_ref, w_ref, g0, g1, g2, o_ref):
        acc = None
        for b, (y_ref, g_ref) in enumerate(zip((ya_ref, yb_ref, yc_ref), (g0, g1, g2))):
            term = _sigmoid(g_ref[...]) * _bdot(y_ref[...], w_ref[b])
            acc = term if acc is None else acc + term
        o_ref[...] = acc.astype(o_ref.dtype)

    return pl.pallas_call(
        body, name=name, grid=(s // MERGE_TM, D_MODEL // MERGE_TN),
        in_specs=[y_spec] * 3 + [w_spec] + gate_specs, out_specs=mn,
        out_shape=jax.ShapeDtypeStruct((s, D_MODEL), BF16), compiler_params=_params("parallel", "parallel"),
    )(ya, yb, yc, wb, proj, proj, proj)


def _merge_bwd(ya, yb, yc, wb, proj, dm, *, name):
    s = ya.shape[0]
    y_spec, w_spec, gate_specs, mn = _merge_specs()

    def body(ya_ref, yb_ref, yc_ref, w_ref, g0, g1, g2, dm_ref, *outs):
        dmv = dm_ref[...].astype(F32)
        for b, (y_ref, g_ref) in enumerate(zip((ya_ref, yb_ref, yc_ref), (g0, g1, g2))):
            sg = _sigmoid(g_ref[...])
            z = _bdot(y_ref[...], w_ref[b])
            outs[b][...] = (dmv * sg).astype(BF16)
            outs[N_BRANCH + b][...] = (dmv * z * sg * (1.0 - sg)).astype(BF16)

    out = jax.ShapeDtypeStruct((s, D_MODEL), BF16)
    res = pl.pallas_call(
        body, name=name, grid=(s // MERGE_TM, D_MODEL // MERGE_TN),
        in_specs=[y_spec] * 3 + [w_spec] + gate_specs + [mn], out_specs=[mn] * (2 * N_BRANCH),
        out_shape=[out] * (2 * N_BRANCH), compiler_params=_params("parallel", "parallel"),
    )(ya, yb, yc, wb, proj, proj, proj, dm)
    return res[:N_BRANCH], res[N_BRANCH:]


def _chunk_rows(v, s):
    return v[:, :2 * GDN_HEADS].reshape(s // CHUNK, CHUNK, 2 * GDN_HEADS).transpose(0, 2, 1)


def _relu2_epi(acc):
    r = jnp.maximum(acc, 0.0)
    return acc, r * r


def _drelu2_epi(acc, a):
    return (acc * (2.0 * jnp.maximum(a.astype(F32), 0.0)),)


def _layer_fwd(x0, p):
    s = x0.shape[0]
    h1 = _norm_fwd(x0, p["norm_mix_pre"], name="norm_mix_pre")
    proj = _matmul(h1, p["w_in"], name="proj_in", tm=512, tn=1664)
    qkvn = _gdn_pre_fwd(proj, p["conv_qkv_w"], name="gdn_pre")
    gb = _gdn_gates_fwd(proj, p["gdn_a_log"], p["gdn_dt_bias"], name="gdn_gates")
    gbt = _chunk_rows(gb, s)
    o_gdn, states = _gdn_chunk_fwd(qkvn, gb, gbt, name="gdn_chunk")
    ya = _gdn_post_fwd(o_gdn, proj, p["gdn_norm_w"], name="gdn_post")
    o_sb, sb_tot = _sb_fwd(proj, name="sb_attn")
    yc = _sc_fwd(proj, p["conv_sc_w"], name="short_conv")
    merged = _merge_fwd(ya, o_sb, yc, p["w_branch"], proj, name="merge")
    u = _matmul(merged, p["w_out"], name="proj_out", tm=512, tn=1024)
    x1 = _resnorm_fwd(x0, u, p["norm_mix_post"], name="norm_mix_post")
    h2 = _norm_fwd(x1, p["norm_ffn_pre"], name="norm_ffn_pre")
    a, r = _matmul(h2, p["w_ff1"], name="ff1", tm=512, tn=1024, outs=(BF16, BF16), epi=_relu2_epi, b_chips=True)
    f = _matmul(r, p["w_ff2"], name="ff2", tm=512, tn=1024, tk=1024)
    x2 = _resnorm_fwd(x1, f, p["norm_ffn_post"], name="norm_ffn_post")
    saved = dict(x0=x0, h1=h1, proj=proj, qkvn=qkvn, gb=gb, gbt=gbt, o_gdn=o_gdn, states=states, ya=ya, o_sb=o_sb,
                 sb_tot=sb_tot, yc=yc, merged=merged, u=u, x1=x1, h2=h2, a=a, r=r, f=f)
    return x2, saved


def _layer_bwd(dx2, p, sv):
    g = {}
    df, g["norm_ffn_post"] = _norm_bwd(sv["f"], p["norm_ffn_post"], dx2, None, out_dtype=BF16, name="norm_ffn_post_bwd")
    g["w_ff2"] = _matmul(sv["r"], df, ta=True, name="ff2_dw", tm=1024, tn=1024, tk=512)
    da = _matmul(df, p["w_ff2"], tb=True, name="ff2_dx", tm=512, tn=1024, outs=(BF16,), epi=_drelu2_epi,
                 extras=(sv["a"],))
    g["w_ff1"] = _matmul(sv["h2"], da, ta=True, name="ff1_dw", tm=1024, tn=1024, tk=512, out_chips=True)
    dh2 = _matmul(da, p["w_ff1"], tb=True, name="ff1_dx", tm=512, tn=1024, tk=1024, b_chips=True)
    dx1, g["norm_ffn_pre"] = _norm_bwd(sv["x1"], p["norm_ffn_pre"], dh2, dx2, out_dtype=F32, name="norm_ffn_pre_bwd")
    du, g["norm_mix_post"] = _norm_bwd(sv["u"], p["norm_mix_post"], dx1, None, out_dtype=BF16, name="norm_mix_post_bwd")
    g["w_out"] = _matmul(sv["merged"], du, ta=True, name="out_dw", tm=1024, tn=1024, tk=512)
    dmerged = _matmul(du, p["w_out"], tb=True, name="out_dx", tm=512, tn=1024, outs=(BF16,))
    ys = (sv["ya"], sv["o_sb"], sv["yc"])
    dz, dgates = _merge_bwd(*ys, p["w_branch"], sv["proj"], dmerged, name="merge_bwd")
    g["w_branch"] = jnp.stack([_matmul(ys[b], dz[b], ta=True, name=f"branch_dw{b}", tm=512, tn=256, tk=512, out_chips=True)
                               for b in range(N_BRANCH)], axis=1)
    dys = [_matmul(dz[b], p["w_branch"][:, b], tb=True, name=f"branch_dx{b}", tm=512, tn=512, tk=256, b_chips=True)
           for b in range(N_BRANCH)]
    dscx, dscb, dscc, g["conv_sc_w"] = _sc_bwd(sv["proj"], p["conv_sc_w"], dys[2], name="short_conv_bwd")
    dsq, dsk, dsv = _sb_bwd(sv["proj"], sv["sb_tot"], dys[1], name="sb_attn_bwd")
    do_gdn, dgate, dnw = _gdn_post_bwd(sv["o_gdn"], sv["proj"], p["gdn_norm_w"], dys[0], name="gdn_post_bwd")
    g["gdn_norm_w"] = dnw
    dqkvn, dgb = _gdn_chunk_bwd(sv["qkvn"], sv["gb"], sv["gbt"], sv["states"], do_gdn, name="gdn_chunk_bwd")
    dqkv, g["conv_qkv_w"] = _gdn_pre_bwd(sv["proj"], p["conv_qkv_w"], dqkvn, name="gdn_pre_bwd")
    dab, g["gdn_a_log"], g["gdn_dt_bias"] = _gdn_gates_bwd(sv["proj"], p["gdn_a_log"], p["gdn_dt_bias"], dgb,
                                                           name="gdn_gates_bwd")
    dproj = jnp.concatenate([*dgates, dqkv, dgate, dab, dsq, dsk, dsv, dscx, dscb, dscc], axis=1)
    g["w_in"] = _matmul(sv["h1"], dproj, ta=True, name="in_dw", tm=1024, tn=1664, tk=512)
    dh1 = _matmul(dproj, p["w_in"], tb=True, name="in_dx", tm=512, tn=1024, tk=1664)
    dx0, g["norm_mix_pre"] = _norm_bwd(sv["x0"], p["norm_mix_pre"], dh1, dx1, out_dtype=F32, name="norm_mix_pre_bwd")
    return dx0, g


def _local_step(x, target, n_layers, weights_of, grads_done):
    saved, layers = [], []
    h = x
    for l in range(n_layers):
        p = weights_of(l, h)
        h, sv = _layer_fwd(h, p)
        saved.append(sv)
        layers.append(p)
    loss, dh = _loss_fwd_bwd(h, target, name="loss")
    for l in reversed(range(n_layers)):
        dh, g = _layer_bwd(dh, layers[l], saved[l])
        zero = grads_done(l, g)
        if l > 0:
            layers[l - 1] = dict(layers[l - 1], norm_ffn_post=layers[l - 1]["norm_ffn_post"] + zero)
    return loss, dh


ANY = pl.BlockSpec(memory_space=pl.ANY)


def _me_and_chips():
    x, y, c = lax.axis_index("x"), lax.axis_index("y"), lax.axis_index("c")
    chips = [(1 - x, y), (x, 1 - y), (1 - x, 1 - y)]
    return x, y, c, chips


def _gather_chips(arrs, *, name):
    n = len(arrs)

    def body(*refs):
        ins, outs = refs[:n], refs[n:2 * n]
        send_sems, recv_sems, local_sems = refs[2 * n:]
        x, y, c, chips = _me_and_chips()
        me = 2 * x + y
        sends, locals_ = [], []
        for a in range(n):
            lc = pltpu.make_async_copy(ins[a], outs[a].at[me], local_sems.at[a])
            lc.start()
            locals_.append(lc)
            for k, (px, py) in enumerate(chips):
                cp = pltpu.make_async_remote_copy(src_ref=ins[a], dst_ref=outs[a].at[me], send_sem=send_sems.at[a, k],
                                                  recv_sem=recv_sems.at[a, k], device_id=(px, py, c), device_id_type=MESH)
                cp.start()
                sends.append(cp)
        for a in range(n):
            for k, (px, py) in enumerate(chips):
                pltpu.make_async_remote_copy(src_ref=ins[a], dst_ref=outs[a].at[2 * px + py], send_sem=send_sems.at[a, k],
                                             recv_sem=recv_sems.at[a, k], device_id=(px, py, c),
                                             device_id_type=MESH).wait_recv()
        for cp in sends:
            cp.wait_send()
        for lc in locals_:
            lc.wait()

    return pl.pallas_call(
        body, name=name, in_specs=[ANY] * n, out_specs=[ANY] * n,
        out_shape=[jax.ShapeDtypeStruct((N_CHIPS,) + a.shape, a.dtype) for a in arrs],
        scratch_shapes=[pltpu.SemaphoreType.DMA((n, 3)), pltpu.SemaphoreType.DMA((n, 3)), pltpu.SemaphoreType.DMA((n,))],
    )(*arrs)


def _gather_devices(small, *, name):
    def body(small_ref, small_out, ssend, srecv, local_sem):
        x, y, c, chips = _me_and_chips()
        dev = 4 * x + 2 * y + c
        lc = pltpu.make_async_copy(small_ref, small_out.at[dev], local_sem)
        lc.start()
        peers = [(x, y, 1 - c)] + [(px, py, pc) for (px, py) in chips for pc in (c, 1 - c)]
        sends = []
        for k, peer in enumerate(peers):
            cp = pltpu.make_async_remote_copy(src_ref=small_ref, dst_ref=small_out.at[dev], send_sem=ssend.at[k],
                                              recv_sem=srecv.at[k], device_id=peer, device_id_type=MESH)
            cp.start()
            sends.append(cp)
        for k, (px, py, pc) in enumerate(peers):
            pltpu.make_async_remote_copy(src_ref=small_ref, dst_ref=small_out.at[4 * px + 2 * py + pc], send_sem=ssend.at[k],
                                         recv_sem=srecv.at[k], device_id=(px, py, pc), device_id_type=MESH).wait_recv()
        for cp in sends:
            cp.wait_send()
        lc.wait()

    return pl.pallas_call(
        body, name=name, in_specs=[ANY], out_specs=ANY,
        out_shape=jax.ShapeDtypeStruct((N_DEV,) + small.shape, small.dtype),
        scratch_shapes=[pltpu.SemaphoreType.DMA((N_DEV - 1,)), pltpu.SemaphoreType.DMA((N_DEV - 1,)), pltpu.SemaphoreType.DMA],
    )(small)


HBM = pl.BlockSpec(memory_space=pltpu.HBM)
SEM = pl.BlockSpec(memory_space=pltpu.SEMAPHORE)
EFFECT = pltpu.SideEffectType.DATAFLOW_SIDE_EFFECTING


def _exchange_start(srcs, *, by_slot, name):
    n = len(srcs)
    land_shapes = [a.shape if by_slot else (N_CHIPS,) + a.shape for a in srcs]
    lands = [pltpu.with_memory_space_constraint(lax.empty(sh, a.dtype), pltpu.HBM) for sh, a in zip(land_shapes, srcs)]
    srcs = [pltpu.with_memory_space_constraint(a, pltpu.HBM) for a in srcs]

    def body(*refs):
        ins, land = refs[:n], refs[n:2 * n]
        send_sems, recv_sems, token = refs[2 * n], refs[2 * n + 1], refs[-1]
        x, y, c, chips = _me_and_chips()
        me = 2 * x + y
        for a in range(n):
            for k, (px, py) in enumerate(chips):
                pltpu.make_async_remote_copy(
                    src_ref=ins[a].at[2 * px + py] if by_slot else ins[a], dst_ref=land[a].at[me],
                    send_sem=send_sems.at[3 * a + k], recv_sem=recv_sems.at[3 * a + k], device_id=(px, py, c),
                    device_id_type=MESH).start()
        token[...] = jnp.zeros_like(token)

    res = pl.pallas_call(
        body, name=name, in_specs=[HBM] * (2 * n),
        out_specs=[SEM, SEM] + [HBM] * (2 * n) + [pl.BlockSpec(memory_space=pltpu.VMEM)],
        out_shape=[pltpu.SemaphoreType.DMA((3 * n,)), pltpu.SemaphoreType.DMA((3 * n,))]
        + [pltpu.HBM(a.shape, a.dtype) for a in srcs] + [pltpu.HBM(sh, a.dtype) for sh, a in zip(land_shapes, srcs)]
        + [jax.ShapeDtypeStruct((8, LANES), F32)],
        input_output_aliases={i: 2 + i for i in range(2 * n)},
        compiler_params=pltpu.CompilerParams(has_side_effects=EFFECT),
    )(*srcs, *lands)
    return dict(send=res[0], recv=res[1], srcs=res[2:2 + n], lands=res[2 + n:2 + 2 * n], token=res[-1])


def _exchange_wait(ex, after, *, by_slot, name):
    n = len(ex["srcs"])

    def body(*refs):
        ins, land = refs[:n], refs[n:2 * n]
        send_sems, recv_sems = refs[2 * n], refs[2 * n + 1]
        x, y, c, chips = _me_and_chips()
        me = 2 * x + y
        for a in range(n):
            for k, (px, py) in enumerate(chips):
                cp = pltpu.make_async_remote_copy(
                    src_ref=ins[a].at[me] if by_slot else ins[a], dst_ref=land[a].at[2 * px + py],
                    send_sem=send_sems.at[3 * a + k], recv_sem=recv_sems.at[3 * a + k], device_id=(px, py, c),
                    device_id_type=MESH)
                cp.wait_send()
                cp.wait_recv()

    res = pl.pallas_call(
        body, name=name, in_specs=[HBM] * (2 * n) + [SEM, SEM, ANY], out_specs=[HBM] * (2 * n),
        out_shape=[pltpu.HBM(a.shape, a.dtype) for a in ex["srcs"]] + [pltpu.HBM(a.shape, a.dtype) for a in ex["lands"]],
        input_output_aliases={i: i for i in range(2 * n)},
        compiler_params=pltpu.CompilerParams(has_side_effects=EFFECT),
    )(*ex["srcs"], *ex["lands"], ex["send"], ex["recv"], after)
    return res[:n], res[n:]


def _chip_index():
    return 2 * lax.axis_index("x") + lax.axis_index("y")


def _me_operand():
    return jnp.reshape(_chip_index(), (1,)).astype(jnp.int32)


def _place_own(land, own, *, name):
    rows, cols = _as2d(own).shape
    tr = _row_tile(rows, cols)

    def body(me_ref, own_ref, land_ref, out_ref):
        out_ref[...] = own_ref[...]

    res = pl.pallas_call(
        body, name=name,
        grid_spec=pltpu.PrefetchScalarGridSpec(
            num_scalar_prefetch=1, grid=(rows // tr,),
            in_specs=[pl.BlockSpec((tr, cols), lambda i, me: (i, 0)), ANY],
            out_specs=pl.BlockSpec((None, tr, cols), lambda i, me: (me[0], i, 0))),
        out_shape=jax.ShapeDtypeStruct((N_CHIPS, rows, cols), land.dtype), input_output_aliases={2: 0},
        compiler_params=_params("arbitrary"),
    )(_me_operand(), _as2d(own), land.reshape(N_CHIPS, rows, cols))
    return res.reshape(land.shape)


def _sum_partials(lands, parts, *, name):
    n, rows, cols = lands.shape
    tr = _row_tile(rows, cols, 1024 * 1024)

    def body(me_ref, land_ref, own_ref, o_ref):
        me = me_ref[0]
        acc = None
        for i in range(n):
            term = jnp.where(me == i, own_ref[...], land_ref[i]).astype(F32)
            acc = term if acc is None else acc + term
        o_ref[...] = acc

    return pl.pallas_call(
        body, name=name,
        grid_spec=pltpu.PrefetchScalarGridSpec(
            num_scalar_prefetch=1, grid=(rows // tr,),
            in_specs=[pl.BlockSpec((n, tr, cols), lambda i, me: (0, i, 0)),
                      pl.BlockSpec((None, tr, cols), lambda i, me: (me[0], i, 0))],
            out_specs=pl.BlockSpec((tr, cols), lambda i, me: (i, 0))),
        out_shape=jax.ShapeDtypeStruct((rows, cols), F32), compiler_params=_params("arbitrary"),
    )(_me_operand(), lands, parts)


def _swap_sibling(arrs, *, name):
    n = len(arrs)

    def body(*refs):
        ins, outs = refs[:n], refs[n:2 * n]
        send_sems, recv_sems = refs[2 * n:]
        x, y, c = lax.axis_index("x"), lax.axis_index("y"), lax.axis_index("c")
        cps = [pltpu.make_async_remote_copy(src_ref=ins[a], dst_ref=outs[a], send_sem=send_sems.at[a],
                                            recv_sem=recv_sems.at[a], device_id=(x, y, 1 - c), device_id_type=MESH)
               for a in range(n)]
        for cp in cps:
            cp.start()
        for cp in cps:
            cp.wait()

    return pl.pallas_call(
        body, name=name, in_specs=[ANY] * n, out_specs=[ANY] * n,
        out_shape=[jax.ShapeDtypeStruct(a.shape, a.dtype) for a in arrs],
        scratch_shapes=[pltpu.SemaphoreType.DMA((n,)), pltpu.SemaphoreType.DMA((n,))],
    )(*arrs)


def _row_tile(rows, cols, budget=2 * 1024 * 1024):
    best = None
    for t in range(16, rows + 1, 16):
        if rows % t == 0 and t * cols * 4 <= budget:
            best = t
    return best if best is not None else rows


def _sum_slots(parts, *, name):
    n, rows, cols = parts.shape
    tr = _row_tile(rows, cols, 1024 * 1024)

    def body(p_ref, o_ref):
        acc = p_ref[0].astype(F32)
        for i in range(1, n):
            acc = acc + p_ref[i].astype(F32)
        o_ref[...] = acc

    return pl.pallas_call(
        body, name=name, grid=(rows // tr,), in_specs=[pl.BlockSpec((n, tr, cols), lambda i: (0, i, 0))],
        out_specs=pl.BlockSpec((tr, cols), lambda i: (i, 0)), out_shape=jax.ShapeDtypeStruct((rows, cols), F32),
        compiler_params=_params("parallel"),
    )(parts)


def _adamw(w, m, v, g_a, g_b, *, name):
    rows, cols = w.shape
    tr = _row_tile(rows, cols, 1024 * 1024)
    two = g_b is not None
    c1 = 1.0 / (1.0 - ADAM_B1 ** ADAM_STEP)
    c2 = 1.0 / (1.0 - ADAM_B2 ** ADAM_STEP)

    def body(*refs):
        w_ref, m_ref, v_ref, ga_ref = refs[:4]
        g_ref, d_ref, nm_ref, nv_ref = refs[4 + two:]
        g = ga_ref[...]
        if two:
            g = g + refs[4][...]
        nm = ADAM_B1 * m_ref[...] + (1.0 - ADAM_B1) * g
        nv = ADAM_B2 * v_ref[...] + (1.0 - ADAM_B2) * (g * g)
        g_ref[...] = g
        nm_ref[...] = nm
        nv_ref[...] = nv
        d_ref[...] = -ADAM_LR * ((nm * c1) / (jnp.sqrt(nv * c2) + ADAM_EPS) + ADAM_WD * w_ref[...])

    blk = pl.BlockSpec((tr, cols), lambda i: (i, 0))
    ins = [w, m, v, g_a] + ([g_b] if two else [])
    return pl.pallas_call(
        body, name=name, grid=(rows // tr,), in_specs=[blk] * len(ins), out_specs=[blk] * 4,
        out_shape=[jax.ShapeDtypeStruct((rows, cols), F32)] * 4, compiler_params=_params("parallel"),
    )(*ins)


def _cast_bf16(w, *, name):
    rows, cols = w.shape
    tr = _row_tile(rows, cols)

    def body(w_ref, o_ref):
        o_ref[...] = w_ref[...].astype(BF16)

    blk = pl.BlockSpec((tr, cols), lambda i: (i, 0))
    return pl.pallas_call(body, name=name, grid=(rows // tr,), in_specs=[blk], out_specs=blk,
                          out_shape=jax.ShapeDtypeStruct((rows, cols), BF16), compiler_params=_params("parallel"))(w)


BIG = ("w_in", "w_branch", "w_out", "w_ff1", "w_ff2")
SMALL = ("norm_mix_pre", "conv_qkv_w", "gdn_a_log", "gdn_dt_bias", "gdn_norm_w", "conv_sc_w", "norm_mix_post",
         "norm_ffn_pre", "norm_ffn_post")
ORDER = ("norm_mix_pre", "w_in", "conv_qkv_w", "gdn_a_log", "gdn_dt_bias", "gdn_norm_w", "conv_sc_w", "w_branch",
         "w_out", "norm_mix_post", "norm_ffn_pre", "w_ff1", "w_ff2", "norm_ffn_post")


def _full_weights(big, conv, rep, l):
    p = dict(
        w_in=_in_cols_from_chips(big["w_in"]),
        w_branch=big["w_branch"],
        w_out=big["w_out"].reshape(D_MODEL, D_MODEL),
        w_ff1=big["w_ff1"],
        w_ff2=big["w_ff2"].reshape(D_FF, D_MODEL),
        conv_qkv_w=conv["conv_qkv_w"][:, l].transpose(1, 0, 2).reshape(4, 3 * BRANCH_W),
        conv_sc_w=conv["conv_sc_w"][:, l].transpose(1, 0, 2).reshape(3, BRANCH_W),
    )
    for k in ("norm_mix_pre", "gdn_a_log", "gdn_dt_bias", "gdn_norm_w", "norm_mix_post", "norm_ffn_pre", "norm_ffn_post"):
        p[k] = rep[k][l]
    return p


def _partials_by_chip(g):
    parts = dict(
        w_in=_in_cols_to_chips(g["w_in"]),
        w_branch=g["w_branch"],
        w_out=g["w_out"].reshape(N_CHIPS, D_MODEL // N_CHIPS, D_MODEL),
        w_ff1=g["w_ff1"],
        w_ff2=g["w_ff2"].reshape(N_CHIPS, D_FF // N_CHIPS, D_MODEL),
    )
    return [parts[k].astype(BF16) for k in BIG]


def _pack_small(grads):
    pieces, layout = [], []
    for name in SMALL:
        v = jnp.stack([g[name] for g in grads]).astype(F32)
        layout.append((name, v.shape))
        pieces.append(v.reshape(-1))
    flat = jnp.concatenate(pieces)
    rows = -(-flat.shape[0] // LANES)
    rows = -(-rows // 8) * 8
    flat = jnp.pad(flat, (0, rows * LANES - flat.shape[0]))
    return flat.reshape(rows, LANES), layout


def _unpack_small(table, layout):
    flat, out, off = table.reshape(-1), {}, 0
    for name, shape in layout:
        size = 1
        for d in shape:
            size *= d
        out[name] = flat[off:off + size].reshape(shape)
        off += size
    return out


def _as2d(a):
    return a.reshape(-1, a.shape[-1]) if a.ndim > 1 else a.reshape(1, -1)


def kernel(x, norm_mix_pre, w_in, conv_qkv_w, gdn_a_log, gdn_dt_bias, gdn_norm_w, conv_sc_w, w_branch, w_out, norm_mix_post, norm_ffn_pre, w_ff1, w_ff2, norm_ffn_post, loss_target, m_norm_mix_pre, m_w_in, m_conv_qkv_w, m_gdn_a_log, m_gdn_dt_bias, m_gdn_norm_w, m_conv_sc_w, m_w_branch, m_w_out, m_norm_mix_post, m_norm_ffn_pre, m_w_ff1, m_w_ff2, m_norm_ffn_post, v_norm_mix_pre, v_w_in, v_conv_qkv_w, v_gdn_a_log, v_gdn_dt_bias, v_gdn_norm_w, v_conv_sc_w, v_w_branch, v_w_out, v_norm_mix_post, v_norm_ffn_pre, v_w_ff1, v_w_ff2, v_norm_ffn_post):
    w = dict(norm_mix_pre=norm_mix_pre, w_in=w_in, conv_qkv_w=conv_qkv_w, gdn_a_log=gdn_a_log, gdn_dt_bias=gdn_dt_bias,
             gdn_norm_w=gdn_norm_w, conv_sc_w=conv_sc_w, w_branch=w_branch, w_out=w_out, norm_mix_post=norm_mix_post,
             norm_ffn_pre=norm_ffn_pre, w_ff1=w_ff1, w_ff2=w_ff2, norm_ffn_post=norm_ffn_post)
    m = dict(norm_mix_pre=m_norm_mix_pre, w_in=m_w_in, conv_qkv_w=m_conv_qkv_w, gdn_a_log=m_gdn_a_log,
             gdn_dt_bias=m_gdn_dt_bias, gdn_norm_w=m_gdn_norm_w, conv_sc_w=m_conv_sc_w, w_branch=m_w_branch, w_out=m_w_out,
             norm_mix_post=m_norm_mix_post, norm_ffn_pre=m_norm_ffn_pre, w_ff1=m_w_ff1, w_ff2=m_w_ff2,
             norm_ffn_post=m_norm_ffn_post)
    v = dict(norm_mix_pre=v_norm_mix_pre, w_in=v_w_in, conv_qkv_w=v_conv_qkv_w, gdn_a_log=v_gdn_a_log,
             gdn_dt_bias=v_gdn_dt_bias, gdn_norm_w=v_gdn_norm_w, conv_sc_w=v_conv_sc_w, w_branch=v_w_branch, w_out=v_w_out,
             norm_mix_post=v_norm_mix_post, norm_ffn_pre=v_norm_ffn_pre, w_ff1=v_w_ff1, w_ff2=v_w_ff2,
             norm_ffn_post=v_norm_ffn_post)

    me = _chip_index()

    shards = {k: _cast_bf16(_as2d(w[k]), name=f"cast_{k}").reshape(w[k].shape) for k in BIG}
    conv_names = ("conv_qkv_w", "conv_sc_w")
    gathers = [_exchange_start([shards[k][l] for k in BIG] + ([w[k] for k in conv_names] if l == 0 else []),
                               by_slot=False, name=f"gather_start{l}") for l in range(DEPTH)]
    started = gathers[0]["token"]
    for ex in gathers[1:]:
        started = started + ex["token"]
    conv = {}

    def weights_of(l, x_l):
        own, lands = _exchange_wait(gathers[l], started if l == 0 else x_l, by_slot=False, name=f"gather_wait{l}")
        names = BIG + (conv_names if l == 0 else ())
        full = {k: _place_own(land, o, name=f"own_{k}") for k, land, o in zip(names, lands, own)}
        if l == 0:
            conv.update({k: full[k] for k in conv_names})
        p = _full_weights(full, conv, w, l)
        if l == 0:
            p["norm_mix_pre"] = p["norm_mix_pre"] + started[0, 0]
        return p

    grads, scatters = [None] * DEPTH, [None] * DEPTH

    def grads_done(l, g):
        grads[l] = g
        scatters[l] = _exchange_start(_partials_by_chip(g), by_slot=True, name=f"scatter_start{l}")
        return scatters[l]["token"][0, 0]

    loss, dx = _local_step(x[0], loss_target[0], DEPTH, weights_of, grads_done)
    loss = lax.psum(loss, ("x", "y", "c"))

    sums = []
    for l in range(DEPTH):
        parts, lands = _exchange_wait(scatters[l], dx, by_slot=True, name=f"scatter_wait{l}")
        sums.append([_sum_partials(r.reshape(N_CHIPS, -1, r.shape[-1]), o.reshape(N_CHIPS, -1, o.shape[-1]), name=f"sum_{k}")
                     for k, r, o in zip(BIG, lands, parts)])
    mine = [jnp.concatenate([sums[l][i] for l in range(DEPTH)], axis=0) for i in range(len(BIG))]
    theirs = _swap_sibling(mine, name="swap_sibling")
    small, layout = _pack_small(grads)
    small_g = _unpack_small(_sum_slots(_gather_devices(small, name="gather_small"), name="sum_small"), layout)
    for k, width in (("conv_qkv_w", 3 * BRANCH_W // N_CHIPS), ("conv_sc_w", BRANCH_W // N_CHIPS)):
        small_g[k] = lax.dynamic_slice_in_dim(small_g[k], me * width, width, axis=2)

    out = {}
    for k, s_mine, s_theirs in zip(BIG, mine, theirs):
        res = _adamw(_as2d(w[k]), _as2d(m[k]), _as2d(v[k]), s_mine, s_theirs, name=f"adamw_{k}")
        out[k] = [r.reshape(w[k].shape) for r in res]
    for k in SMALL:
        res = _adamw(_as2d(w[k]), _as2d(m[k]), _as2d(v[k]), _as2d(small_g[k]), None, name=f"adamw_{k}")
        out[k] = [r.reshape(w[k].shape) for r in res]
    return (loss, dx[None], *[out[k][0] for k in ORDER], *[out[k][1] for k in ORDER], *[out[k][2] for k in ORDER],
            *[out[k][3] for k in ORDER])
```

```python
import functools

import jax
import jax.numpy as jnp
from jax import lax
from jax.experimental import pallas as pl
from jax.experimental.pallas import tpu as pltpu

F32 = jnp.float32
BF16 = jnp.bfloat16
MESH = pl.DeviceIdType.MESH

LANES = 128
D_MODEL = 1024
DEPTH = 4
CHUNK = 64
GDN_HEADS, GDN_DIM = 4, 128
SB_HEADS, SB_DIM = 8, 64
BRANCH_W = 512
N_BRANCH = 3
D_FF = 4 * D_MODEL
EPS = 1e-6
IN_W = 8200
AB_COL = 2048
AB_PAD = LANES - 8
IN_WP = IN_W + AB_PAD
N_CHIPS = 4
N_DEV = 8
GATES_COL = 5128
PB_GATES, PB_QKV, PB_GATE, PB_AB, PB_SB, PB_SCX, PB_SCB, PB_SCC = 0, 24, 36, 40, 41, 53, 57, 61
SB_TILE = 128
SB_GROUP = 4
SB_SCALE = SB_DIM ** -0.5
SB_SCALE2 = SB_SCALE * 1.4426950408889634
GDN_QSCALE = GDN_DIM ** -0.5
VMEM_LIMIT = 56 * 1024 * 1024

ADAM_LR, ADAM_B1, ADAM_B2, ADAM_EPS, ADAM_WD, ADAM_STEP = 0.001, 0.9, 0.999, 1e-08, 0.01, 10

NT = (((1,), (1,)), ((), ()))
TN = (((0,), (0,)), ((), ()))
HI = lax.Precision.HIGH


def _pad_in_cols(w):
    return jnp.concatenate([w[:, GATES_COL:], w[:, :AB_COL + 8], jnp.zeros((w.shape[0], AB_PAD), w.dtype),
                            w[:, AB_COL + 8:GATES_COL]], axis=1)


def _unpad_in_cols(g):
    n_gates = IN_W - GATES_COL
    return jnp.concatenate([g[:, n_gates:n_gates + AB_COL + 8], g[:, n_gates + AB_COL + 8 + AB_PAD:], g[:, :n_gates]], axis=1)


IN_SHARD = IN_W // N_CHIPS
_IN_SEGMENTS = ((0, AB_COL + 8, IN_W - GATES_COL), (AB_COL + 8, GATES_COL, IN_W - GATES_COL + AB_PAD),
                (GATES_COL, IN_W, -GATES_COL))


def _in_cols_from_chips(slots):
    def cols(first, last):
        out = []
        for j in range(N_CHIPS):
            lo, hi = max(first, j * IN_SHARD), min(last, (j + 1) * IN_SHARD)
            if lo < hi:
                out.append(slots[j][:, lo - j * IN_SHARD:hi - j * IN_SHARD])
        return out

    head, tail, gates = (cols(first, last) for first, last, _ in _IN_SEGMENTS)
    return jnp.concatenate(gates + head + [jnp.zeros((slots.shape[1], AB_PAD), slots.dtype)] + tail, axis=1)


def _in_cols_to_chips(g):
    shards = []
    for j in range(N_CHIPS):
        pieces = []
        for first, last, shift in _IN_SEGMENTS:
            lo, hi = max(first, j * IN_SHARD), min(last, (j + 1) * IN_SHARD)
            if lo < hi:
                pieces.append(g[:, lo + shift:hi + shift])
        shards.append(jnp.concatenate(pieces, axis=1))
    return jnp.stack(shards)


def _params(*sem):
    return pltpu.CompilerParams(dimension_semantics=sem if sem else None, vmem_limit_bytes=VMEM_LIMIT)


def _sigmoid(x):
    return 1.0 / (1.0 + jnp.exp(-x))


def _softplus(x):
    return jnp.maximum(x, 0.0) + jnp.log(1.0 + jnp.exp(-jnp.abs(x)))


def _softplus2(x):
    return jnp.maximum(x, 0.0) + jnp.log2(1.0 + jnp.exp2(-jnp.abs(x)))


def _dot(a, b, dims=None, precision=None):
    if dims is None:
        return jnp.dot(a, b, preferred_element_type=F32, precision=precision)
    return lax.dot_general(a, b, dims, preferred_element_type=F32, precision=precision)


def _bdot(a, b, dims=None):
    return _dot(a.astype(BF16), b.astype(BF16), dims)


def _matmul(a, b, *, name, ta=False, tb=False, tm, tn, tk=None, outs=(F32,), epi=None, extras=(), b_chips=False,
            out_chips=False):
    if ta:
        kdim, m = a.shape
    else:
        m, kdim = a.shape
    if b_chips:
        per = b.shape[2]
        if tb:
            n, kb = b.shape[1], N_CHIPS * per
        else:
            kb, n = b.shape[1], N_CHIPS * per
    elif tb:
        n, kb = b.shape
    else:
        kb, n = b.shape
    assert kdim == kb, (a.shape, b.shape)
    tk = kdim if tk is None else tk
    assert m % tm == 0 and n % tn == 0 and kdim % tk == 0, (m, n, kdim, tm, tn, tk)
    nk = kdim // tk
    a_spec = pl.BlockSpec((tk, tm), lambda i, j, k: (k, i)) if ta else pl.BlockSpec((tm, tk), lambda i, j, k: (i, k))
    if b_chips and tb:
        assert per % tk == 0
        b_spec = pl.BlockSpec((None, tn, tk), lambda i, j, k: (k // (per // tk), j, k % (per // tk)))
    elif b_chips:
        assert per % tn == 0
        b_spec = pl.BlockSpec((None, tk, tn), lambda i, j, k: (j // (per // tn), k, j % (per // tn)))
    else:
        b_spec = pl.BlockSpec((tn, tk), lambda i, j, k: (j, k)) if tb else pl.BlockSpec((tk, tn), lambda i, j, k: (k, j))
    mn_spec = pl.BlockSpec((tm, tn), lambda i, j, k: (i, j))
    if out_chips:
        per_o = n // N_CHIPS
        assert per_o % tn == 0
        out_spec = pl.BlockSpec((None, tm, tn), lambda i, j, k: (j // (per_o // tn), i, j % (per_o // tn)))
        out_dims = (N_CHIPS, m, per_o)
    else:
        out_spec, out_dims = mn_spec, (m, n)
    dims = (((0 if ta else 1,), (1 if tb else 0,)), ((), ()))
    n_ex, n_out = len(extras), len(outs)

    def body(a_ref, b_ref, *rest):
        ex, o, acc = rest[:n_ex], rest[n_ex:n_ex + n_out], rest[n_ex + n_out:]
        part = lax.dot_general(a_ref[...].astype(BF16), b_ref[...].astype(BF16), dims, preferred_element_type=F32)

        def finish(val):
            res = epi(val, *[e[...] for e in ex]) if epi is not None else (val,)
            for r, oref in zip(res, o):
                oref[...] = r.astype(oref.dtype)

        if nk == 1:
            finish(part)
        else:
            k = pl.program_id(2)

            @pl.when(k == 0)
            def _():
                acc[0][...] = part

            @pl.when(k > 0)
            def _():
                acc[0][...] += part

            @pl.when(k == nk - 1)
            def _():
                finish(acc[0][...])

    res = pl.pallas_call(
        body, name=name, grid=(m // tm, n // tn, nk),
        in_specs=[a_spec, b_spec] + [mn_spec] * n_ex,
        out_specs=[out_spec] * n_out,
        out_shape=[jax.ShapeDtypeStruct(out_dims, dt) for dt in outs],
        scratch_shapes=[pltpu.VMEM((tm, tn), F32)] if nk > 1 else [],
        compiler_params=_params("parallel", "parallel", "arbitrary"),
    )(a, b, *extras)
    return res[0] if n_out == 1 else res


ROW_TILE = 512


def _norm_fwd(x, w, *, name):
    s, d = x.shape

    def body(x_ref, w_ref, o_ref):
        xv = x_ref[...]
        r = lax.rsqrt(jnp.mean(xv * xv, axis=-1, keepdims=True) + EPS)
        o_ref[...] = (xv * r * w_ref[...]).astype(o_ref.dtype)

    return pl.pallas_call(
        body, name=name, grid=(s // ROW_TILE,),
        in_specs=[pl.BlockSpec((ROW_TILE, d), lambda i: (i, 0)), pl.BlockSpec((1, d), lambda i: (0, 0))],
        out_specs=pl.BlockSpec((ROW_TILE, d), lambda i: (i, 0)),
        out_shape=jax.ShapeDtypeStruct((s, d), BF16), compiler_params=_params("parallel"),
    )(x, w.reshape(1, d))


def _resnorm_fwd(x, u, w, *, name):
    s, d = x.shape

    def body(x_ref, u_ref, w_ref, o_ref):
        uv = u_ref[...]
        r = lax.rsqrt(jnp.mean(uv * uv, axis=-1, keepdims=True) + EPS)
        o_ref[...] = x_ref[...] + uv * r * w_ref[...]

    row = pl.BlockSpec((ROW_TILE, d), lambda i: (i, 0))
    return pl.pallas_call(
        body, name=name, grid=(s // ROW_TILE,),
        in_specs=[row, row, pl.BlockSpec((1, d), lambda i: (0, 0))], out_specs=row,
        out_shape=jax.ShapeDtypeStruct((s, d), F32), compiler_params=_params("parallel"),
    )(x, u, w.reshape(1, d))


def _norm_bwd(xin, w, dy, res, *, out_dtype, name):
    s, d = xin.shape
    has_res = res is not None

    def body(*refs):
        x_ref, w_ref, dy_ref = refs[:3]
        res_ref = refs[3] if has_res else None
        dx_ref, dw_ref = refs[3 + has_res:]
        xv, dyv = x_ref[...], dy_ref[...].astype(F32)
        r = lax.rsqrt(jnp.mean(xv * xv, axis=-1, keepdims=True) + EPS)
        xh = xv * r
        g = dyv * w_ref[...]
        dx = r * (g - xh * jnp.mean(g * xh, axis=-1, keepdims=True))
        if has_res:
            dx = dx + res_ref[...]
        dx_ref[...] = dx.astype(dx_ref.dtype)

        @pl.when(pl.program_id(0) == 0)
        def _():
            dw_ref[...] = jnp.zeros_like(dw_ref)

        dw_ref[...] += jnp.sum(dyv * xh, axis=0, keepdims=True)

    row = pl.BlockSpec((ROW_TILE, d), lambda i: (i, 0))
    vec = pl.BlockSpec((1, d), lambda i: (0, 0))
    ins = [xin, w.reshape(1, d), dy] + ([res] if has_res else [])
    dx, dw = pl.pallas_call(
        body, name=name, grid=(s // ROW_TILE,),
        in_specs=[row, vec, row] + ([row] if has_res else []), out_specs=[row, vec],
        out_shape=[jax.ShapeDtypeStruct((s, d), out_dtype), jax.ShapeDtypeStruct((1, d), F32)],
        compiler_params=_params("arbitrary"),
    )(*ins)
    return dx, dw.reshape(d)


def _loss_fwd_bwd(y, target, *, name):
    s, d = y.shape

    def body(y_ref, t_ref, loss_ref, dy_ref):
        e = y_ref[...] - t_ref[...]
        dy_ref[...] = e * (1.0 / d)

        @pl.when(pl.program_id(0) == 0)
        def _():
            loss_ref[...] = jnp.zeros_like(loss_ref)

        part = jnp.sum(jnp.sum(e * e, axis=1, keepdims=True), axis=0, keepdims=True)
        loss_ref[...] += part * (0.5 / d)

    row = pl.BlockSpec((ROW_TILE, d), lambda i: (i, 0))
    loss, dy = pl.pallas_call(
        body, name=name, grid=(s // ROW_TILE,), in_specs=[row, row],
        out_specs=[pl.BlockSpec((1, 1), lambda i: (0, 0)), row],
        out_shape=[jax.ShapeDtypeStruct((1, 1), F32), jax.ShapeDtypeStruct((s, d), F32)],
        compiler_params=_params("arbitrary"),
    )(y, target)
    return loss[0, 0], dy


def _shift_down(x, k, rows):
    if k == 0:
        return x
    return jnp.where(rows >= k, pltpu.roll(x, k, 0), 0.0)


def _shift_up(x, k, rows):
    if k == 0:
        return x
    n = x.shape[0]
    return jnp.where(rows < n - k, pltpu.roll(x, n - k, 0), 0.0)


def _col_spec(s, base):
    return pl.BlockSpec((s, LANES), lambda j: (0, base + j))


def _gdn_pre_math(x, w, j, rows):
    taps = w.shape[0]
    c = w[taps - 1:taps, :] * x
    for i in range(taps - 1):
        c = c + w[i:i + 1, :] * _shift_down(x, taps - 1 - i, rows)
    sg = _sigmoid(c)
    y = c * sg
    r = lax.rsqrt(jnp.sum(y * y, axis=-1, keepdims=True) + EPS)
    is_qk = j < 2 * GDN_HEADS
    scale = jnp.where(j < GDN_HEADS, GDN_QSCALE, 1.0)
    return c, sg, y, r, is_qk, scale


def _gdn_pre_fwd(proj, conv_w, *, name):
    s = proj.shape[0]

    def body(x_ref, w_ref, o_ref):
        j = pl.program_id(0)
        rows = lax.broadcasted_iota(jnp.int32, (s, LANES), 0)
        _, _, y, r, is_qk, scale = _gdn_pre_math(x_ref[...], w_ref[...], j, rows)
        o_ref[...] = jnp.where(is_qk, y * (r * scale), y)

    return pl.pallas_call(
        body, name=name, grid=(12,),
        in_specs=[_col_spec(s, PB_QKV), pl.BlockSpec((4, LANES), lambda j: (0, j))],
        out_specs=_col_spec(s, 0), out_shape=jax.ShapeDtypeStruct((s, 3 * BRANCH_W), F32),
        compiler_params=_params("parallel"),
    )(proj, conv_w)


def _gdn_pre_bwd(proj, conv_w, dqkvn, *, name):
    s = proj.shape[0]

    def body(x_ref, w_ref, d_ref, dx_ref, dw_ref):
        j = pl.program_id(0)
        rows = lax.broadcasted_iota(jnp.int32, (s, LANES), 0)
        x, w, dout = x_ref[...], w_ref[...], d_ref[...]
        c, sg, y, r, is_qk, scale = _gdn_pre_math(x, w, j, rows)
        yh = y * r
        dy_n = (scale * r) * (dout - yh * jnp.sum(dout * yh, axis=-1, keepdims=True))
        dy = jnp.where(is_qk, dy_n, dout)
        dc = dy * (sg * (1.0 + c * (1.0 - sg)))
        taps = w.shape[0]
        dx = w[taps - 1:taps, :] * dc
        dws = []
        for i in range(taps - 1):
            k = taps - 1 - i
            dx = dx + w[i:i + 1, :] * _shift_up(dc, k, rows)
            dws.append(jnp.sum(dc * _shift_down(x, k, rows), axis=0, keepdims=True))
        dws.append(jnp.sum(dc * x, axis=0, keepdims=True))
        dx_ref[...] = dx.astype(dx_ref.dtype)
        for i in range(taps):
            dw_ref[i:i + 1, :] = dws[i]

    return pl.pallas_call(
        body, name=name, grid=(12,),
        in_specs=[_col_spec(s, PB_QKV), pl.BlockSpec((4, LANES), lambda j: (0, j)), _col_spec(s, 0)],
        out_specs=[_col_spec(s, 0), pl.BlockSpec((4, LANES), lambda j: (0, j))],
        out_shape=[jax.ShapeDtypeStruct((s, 3 * BRANCH_W), BF16), jax.ShapeDtypeStruct((4, 3 * BRANCH_W), F32)],
        compiler_params=_params("parallel"),
    )(proj, conv_w, dqkvn)


def _lane_pad(v):
    return jnp.pad(v.reshape(1, -1), ((0, 0), (0, LANES - v.shape[0])))


def _gdn_gates_fwd(proj, a_log, dt_bias, *, name):
    s = proj.shape[0]

    def body(ab_ref, al_ref, dt_ref, o_ref):
        ab = ab_ref[...]
        lane = lax.broadcasted_iota(jnp.int32, (1, LANES), 1)
        g = -jnp.exp(al_ref[...]) * _softplus(ab + dt_ref[...])
        o_ref[...] = jnp.where(lane < GDN_HEADS, g, _sigmoid(ab))

    vec = pl.BlockSpec((1, LANES), lambda j: (0, 0))
    return pl.pallas_call(
        body, name=name, grid=(1,), in_specs=[_col_spec(s, PB_AB), vec, vec], out_specs=_col_spec(s, 0),
        out_shape=jax.ShapeDtypeStruct((s, LANES), F32), compiler_params=_params("arbitrary"),
    )(proj, _lane_pad(a_log), _lane_pad(dt_bias))


def _gdn_gates_bwd(proj, a_log, dt_bias, dgb, *, name):
    s = proj.shape[0]

    def body(ab_ref, al_ref, dt_ref, d_ref, dab_ref, dal_ref, ddt_ref):
        ab, d = ab_ref[...], d_ref[...]
        lane = lax.broadcasted_iota(jnp.int32, (1, LANES), 1)
        ea = jnp.exp(al_ref[...])
        pre = ab + dt_ref[...]
        g = -ea * _softplus(pre)
        dpre = d * (-ea) * _sigmoid(pre)
        beta = _sigmoid(ab)
        is_g = lane < GDN_HEADS
        dab = jnp.where(is_g, dpre, jnp.where(lane < 2 * GDN_HEADS, d * beta * (1.0 - beta), 0.0))
        dab_ref[...] = dab.astype(dab_ref.dtype)
        dal_ref[...] = jnp.sum(jnp.where(is_g, d * g, 0.0), axis=0, keepdims=True)
        ddt_ref[...] = jnp.sum(jnp.where(is_g, dpre, 0.0), axis=0, keepdims=True)

    vec = pl.BlockSpec((1, LANES), lambda j: (0, 0))
    dab, dal, ddt = pl.pallas_call(
        body, name=name, grid=(1,), in_specs=[_col_spec(s, PB_AB), vec, vec, _col_spec(s, 0)],
        out_specs=[_col_spec(s, 0), vec, vec],
        out_shape=[jax.ShapeDtypeStruct((s, LANES), BF16), jax.ShapeDtypeStruct((1, LANES), F32),
                   jax.ShapeDtypeStruct((1, LANES), F32)],
        compiler_params=_params("arbitrary"),
    )(proj, _lane_pad(a_log), _lane_pad(dt_bias), dgb)
    return dab, dal[0, :GDN_HEADS], ddt[0, :GDN_HEADS]


def _interleave(gens):
    results, live = [None] * len(gens), list(range(len(gens)))
    while live:
        for idx in list(live):
            try:
                next(gens[idx])
            except StopIteration as done:
                results[idx] = done.value
                live.remove(idx)
    return results


def _chunk_common(q, k, v, gb, gbt, h):
    c = CHUNK
    row = lax.broadcasted_iota(jnp.int32, (c, c), 0)
    col = lax.broadcasted_iota(jnp.int32, (c, c), 1)
    tril, strict, eye = row >= col, row > col, row == col
    lane = lax.broadcasted_iota(jnp.int32, (c, LANES), 1)
    sub = lax.broadcasted_iota(jnp.int32, (2 * GDN_HEADS, c), 0)
    g_col = jnp.sum(jnp.where(lane == h, gb, 0.0), axis=1, keepdims=True)
    beta_col = jnp.sum(jnp.where(lane == GDN_HEADS + h, gb, 0.0), axis=1, keepdims=True)
    g_row = jnp.sum(jnp.where(sub == h, gbt, 0.0), axis=0, keepdims=True)
    gc_col = jnp.sum(jnp.where(tril, jnp.broadcast_to(g_row, (c, c)), 0.0), axis=1, keepdims=True)
    gc_row = jnp.sum(jnp.where(row <= col, jnp.broadcast_to(g_col, (c, c)), 0.0), axis=0, keepdims=True)
    g_tot = jnp.sum(g_row, axis=1, keepdims=True)
    dm = jnp.exp(jnp.where(tril, gc_col - gc_row, -1e30))
    e_col = jnp.exp(gc_col)
    kdec_col = jnp.exp(g_tot - gc_col)
    gamma = jnp.exp(g_tot)
    kb = k * beta_col
    vb = v * beta_col
    kbg = kb * e_col
    kk = _dot(kb, k, NT, HI)
    qk = _bdot(q, k, NT)
    yield
    a = jnp.where(strict, kk * dm, 0.0)
    aqk = jnp.where(tril, qk * dm, 0.0)
    bneg = -a
    t = jnp.where(eye, 1.0, 0.0) + bneg
    p = _dot(bneg, bneg, precision=HI)
    yield
    for lvl in range(5):
        t_next = t + _dot(t, p, precision=HI)
        if lvl < 4:
            p = _dot(p, p, precision=HI)
        t = t_next
        yield
    u = _dot(t, vb, precision=HI)
    w = _dot(t, kbg, precision=HI)
    yield
    return dict(tril=tril, strict=strict, eye=eye, row=row, col=col, beta_col=beta_col, dm=dm, e_col=e_col,
                kdec_col=kdec_col, gamma=gamma, kb=kb, vb=vb, kbg=kbg, a=a, t=t, u=u, w=w, aqk=aqk,
                qd=q * e_col, kd=k * kdec_col)


def _gdn_chunk_fwd(qkvn, gb, gbt, *, name):
    s = qkvn.shape[0]
    n_chunks = s // CHUNK

    def body(q_ref, k_ref, v_ref, gb_ref, gbt_ref, o_ref, st_ref, state):
        @pl.when(pl.program_id(0) == 0)
        def _():
            state[...] = jnp.zeros_like(state)

        gbv, gbtv = gb_ref[...], gbt_ref[0]

        def head(h):
            hs = slice(h * GDN_DIM, (h + 1) * GDN_DIM)
            q, k, v = q_ref[:, hs], k_ref[:, hs], v_ref[:, hs]
            m = yield from _chunk_common(q, k, v, gbv, gbtv, h)
            s0 = state[h]
            st_ref[0, h] = s0
            vnew = m["u"] - _bdot(m["w"], s0)
            o_inter = _bdot(m["qd"], s0)
            yield
            o_ref[:, hs] = o_inter + _bdot(m["aqk"], vnew)
            state[h] = m["gamma"] * s0 + _bdot(m["kd"], vnew, TN)

        _interleave([head(h) for h in range(GDN_HEADS)])

    blk = lambda j: pl.BlockSpec((CHUNK, BRANCH_W), lambda n: (n, j))
    return pl.pallas_call(
        body, name=name, grid=(n_chunks,),
        in_specs=[blk(0), blk(1), blk(2), pl.BlockSpec((CHUNK, LANES), lambda n: (n, 0)),
                  pl.BlockSpec((1, 2 * GDN_HEADS, CHUNK), lambda n: (n, 0, 0))],
        out_specs=[blk(0), pl.BlockSpec((1, GDN_HEADS, GDN_DIM, GDN_DIM), lambda n: (n, 0, 0, 0))],
        out_shape=[jax.ShapeDtypeStruct((s, BRANCH_W), F32),
                   jax.ShapeDtypeStruct((n_chunks, GDN_HEADS, GDN_DIM, GDN_DIM), F32)],
        scratch_shapes=[pltpu.VMEM((GDN_HEADS, GDN_DIM, GDN_DIM), F32)],
        compiler_params=_params("arbitrary"),
    )(qkvn, qkvn, qkvn, gb, gbt)


def _gdn_chunk_bwd(qkvn, gb, gbt, states, do, *, name):
    s = qkvn.shape[0]
    n_chunks = s // CHUNK
    c = CHUNK

    def body(q_ref, k_ref, v_ref, gb_ref, gbt_ref, st_ref, do_ref, dq_ref, dk_ref, dv_ref, dgb_ref, dstate):
        @pl.when(pl.program_id(0) == 0)
        def _():
            dstate[...] = jnp.zeros_like(dstate)

        gbv, gbtv = gb_ref[...], gbt_ref[0]
        lane = lax.broadcasted_iota(jnp.int32, (c, LANES), 1)
        def head(h):
            hs = slice(h * GDN_DIM, (h + 1) * GDN_DIM)
            q, k, v, dov = q_ref[:, hs], k_ref[:, hs], v_ref[:, hs], do_ref[:, hs]
            m = yield from _chunk_common(q, k, v, gbv, gbtv, h)
            tril, strict, eye, row, col = m["tril"], m["strict"], m["eye"], m["row"], m["col"]
            s0, ds1 = st_ref[0, h], dstate[h]
            vnew = m["u"] - _bdot(m["w"], s0)
            dvnew_a = _bdot(m["aqk"], dov, TN) + _bdot(m["kd"], ds1)
            dqd = _bdot(dov, s0, NT)
            ds_q = _bdot(m["qd"], dov, TN)
            dgamma = jnp.sum(jnp.sum(s0 * ds1, axis=1, keepdims=True), axis=0, keepdims=True)
            yield
            dvnew = dvnew_a
            daqk = jnp.where(tril, _bdot(dov, vnew, NT), 0.0)
            dkd = _bdot(vnew, ds1, NT)
            dw = -_bdot(dvnew, s0, NT)
            dstate[h] = m["gamma"] * ds1 + ds_q - _bdot(m["w"], dvnew, TN)
            dvb = _dot(m["t"], dvnew, TN, HI)
            yield
            dt = _dot(dvnew, m["vb"], NT, HI) + _dot(dw, m["kbg"], NT, HI)
            dkbg = _dot(m["t"], dw, TN, HI)
            dmq = daqk * m["dm"]
            dq = _bdot(dmq, k) + dqd * m["e_col"]
            dk_q = _bdot(dmq, q, TN)
            yield
            tdt = _dot(m["t"], dt, TN, HI)
            yield
            da = jnp.where(strict, -_dot(tdt, m["t"], NT, HI), 0.0)
            yield
            dmat = da * m["dm"]
            dkb = _dot(dmat, k, precision=HI) + dkbg * m["e_col"]
            dk = (_dot(dmat, m["kb"], TN, HI) + dk_q + dkd * m["kdec_col"] + m["beta_col"] * dkb)
            yield
            dbeta_col = jnp.sum(dkb * k, axis=1, keepdims=True) + jnp.sum(dvb * v, axis=1, keepdims=True)
            e = da * m["a"] + daqk * m["aqk"]
            rs_kd = jnp.sum(dkd * m["kd"], axis=1, keepdims=True)
            e_colsum = jnp.sum(e, axis=0, keepdims=True)
            e_colsum_c = jnp.sum(jnp.where(eye, jnp.broadcast_to(e_colsum, (c, c)), 0.0), axis=1, keepdims=True)
            dgc = (jnp.sum(e, axis=1, keepdims=True) - e_colsum_c + jnp.sum(dqd * m["qd"], axis=1, keepdims=True)
                   - rs_kd + jnp.sum(dkbg * m["kbg"], axis=1, keepdims=True))
            last = jnp.sum(rs_kd, axis=0, keepdims=True) + dgamma * m["gamma"]
            dgc = dgc + jnp.where(lax.broadcasted_iota(jnp.int32, (c, 1), 0) == c - 1, last, 0.0)
            dgc_row = jnp.sum(jnp.where(eye, jnp.broadcast_to(dgc, (c, c)), 0.0), axis=0, keepdims=True)
            dg_col = jnp.sum(jnp.where(col >= row, jnp.broadcast_to(dgc_row, (c, c)), 0.0), axis=1, keepdims=True)
            dq_ref[:, hs] = dq
            dk_ref[:, hs] = dk
            dv_ref[:, hs] = m["beta_col"] * dvb
            return jnp.where(lane == h, dg_col, 0.0) + jnp.where(lane == GDN_HEADS + h, dbeta_col, 0.0)

        parts = _interleave([head(h) for h in range(GDN_HEADS)])
        dgb_ref[...] = (parts[0] + parts[1]) + (parts[2] + parts[3])

    rev = lambda n: n_chunks - 1 - n
    blk = lambda j: pl.BlockSpec((CHUNK, BRANCH_W), lambda n: (rev(n), j))
    dq, dk, dv, dgb = pl.pallas_call(
        body, name=name, grid=(n_chunks,),
        in_specs=[blk(0), blk(1), blk(2), pl.BlockSpec((CHUNK, LANES), lambda n: (rev(n), 0)),
                  pl.BlockSpec((1, 2 * GDN_HEADS, CHUNK), lambda n: (rev(n), 0, 0)),
                  pl.BlockSpec((1, GDN_HEADS, GDN_DIM, GDN_DIM), lambda n: (rev(n), 0, 0, 0)), blk(0)],
        out_specs=[blk(0), blk(0), blk(0), pl.BlockSpec((CHUNK, LANES), lambda n: (rev(n), 0))],
        out_shape=[jax.ShapeDtypeStruct((s, BRANCH_W), F32)] * 3 + [jax.ShapeDtypeStruct((s, LANES), F32)],
        scratch_shapes=[pltpu.VMEM((GDN_HEADS, GDN_DIM, GDN_DIM), F32)],
        compiler_params=_params("arbitrary"),
    )(qkvn, qkvn, qkvn, gb, gbt, states, do)
    return jnp.concatenate([dq, dk, dv], axis=1), dgb


def _gdn_post_fwd(o, proj, norm_w, *, name):
    s = o.shape[0]

    def body(o_ref, g_ref, w_ref, y_ref):
        ov, gv = o_ref[...], g_ref[...]
        r = lax.rsqrt(jnp.mean(ov * ov, axis=-1, keepdims=True) + EPS)
        y_ref[...] = (ov * r * w_ref[...] * (gv * _sigmoid(gv))).astype(y_ref.dtype)

    return pl.pallas_call(
        body, name=name, grid=(GDN_HEADS,),
        in_specs=[_col_spec(s, 0), _col_spec(s, PB_GATE), pl.BlockSpec((1, LANES), lambda j: (0, 0))],
        out_specs=_col_spec(s, 0), out_shape=jax.ShapeDtypeStruct((s, BRANCH_W), BF16),
        compiler_params=_params("parallel"),
    )(o, proj, norm_w.reshape(1, GDN_DIM))


def _gdn_post_bwd(o, proj, norm_w, dy, *, name):
    s = o.shape[0]

    def body(o_ref, g_ref, w_ref, dy_ref, do_ref, dg_ref, dw_ref):
        ov, gv, w, dyv = o_ref[...], g_ref[...], w_ref[...], dy_ref[...].astype(F32)
        r = lax.rsqrt(jnp.mean(ov * ov, axis=-1, keepdims=True) + EPS)
        oh = ov * r
        sg = _sigmoid(gv)
        silu = gv * sg
        dn = dyv * silu
        dg_ref[...] = (dyv * (oh * w) * (sg * (1.0 + gv * (1.0 - sg)))).astype(dg_ref.dtype)

        @pl.when(pl.program_id(0) == 0)
        def _():
            dw_ref[...] = jnp.zeros_like(dw_ref)

        dw_ref[...] += jnp.sum(dn * oh, axis=0, keepdims=True)
        g2 = dn * w
        do_ref[...] = r * (g2 - oh * jnp.mean(g2 * oh, axis=-1, keepdims=True))

    do, dg, dw = pl.pallas_call(
        body, name=name, grid=(GDN_HEADS,),
        in_specs=[_col_spec(s, 0), _col_spec(s, PB_GATE), pl.BlockSpec((1, LANES), lambda j: (0, 0)), _col_spec(s, 0)],
        out_specs=[_col_spec(s, 0), _col_spec(s, 0), pl.BlockSpec((1, LANES), lambda j: (0, 0))],
        out_shape=[jax.ShapeDtypeStruct((s, BRANCH_W), F32), jax.ShapeDtypeStruct((s, BRANCH_W), BF16),
                   jax.ShapeDtypeStruct((1, LANES), F32)],
        compiler_params=_params("arbitrary"),
    )(o, proj, norm_w.reshape(1, GDN_DIM), dy)
    return do, dg, dw.reshape(LANES)


def _split_terms(x):
    hi = x.astype(BF16)
    lo = (x - hi.astype(F32)).astype(BF16)
    return jnp.concatenate([hi, lo], axis=1)


def _sb_sum_matrix(pred):
    row = lax.broadcasted_iota(jnp.int32, (2 * SB_TILE, 2 * SB_TILE), 0) % SB_TILE
    col = lax.broadcasted_iota(jnp.int32, (2 * SB_TILE, 2 * SB_TILE), 1)
    return jnp.where((col >= SB_TILE) | pred(row, col), 1.0, 0.0).astype(BF16)


def _sb_head_masks():
    lane = lax.broadcasted_iota(jnp.int32, (1, LANES), 1)
    return [(lane < SB_DIM).astype(F32), (lane >= SB_DIM).astype(F32)]


def _sb_fwd(proj, *, name):
    s = proj.shape[0]
    t = SB_TILE
    nq = s // t

    def body(q_ref, k_ref, v_ref, o_ref, tot_ref):
        cmr = lax.broadcasted_iota(jnp.int32, (t, t), 1) - lax.broadcasted_iota(jnp.int32, (t, t), 0)
        uo = _sb_sum_matrix(lambda row, col: row > col)
        hm = _sb_head_masks()

        def qloop(i, carry0):
            qs = pl.multiple_of(i * t, t)
            qf = q_ref[pl.ds(qs, t), :] * SB_SCALE2
            qh = [(qf * hm[h]).astype(BF16) for h in range(2)]
            diag = i // SB_GROUP

            def group(g, st, masked):
                ks = pl.multiple_of(g * (SB_GROUP * t), SB_GROUP * t)
                kb = k_ref[pl.ds(ks, SB_GROUP * t), :].astype(BF16)
                vf = v_ref[pl.ds(ks, SB_GROUP * t), :]
                tiles = [(h, j) for h in range(2) for j in range(SB_GROUP)]
                z = [_dot(qh[h], kb, NT) for h in range(2)]
                keep = {j: cmr < (i - g * SB_GROUP - j) * t for j in range(SB_GROUP)} if masked else None
                base, terms = {}, {}
                for h, j in tiles:
                    zj = z[h][:, j * t:(j + 1) * t]
                    sp = _softplus2(zj)
                    base[h, j] = zj - sp
                    terms[h, j] = _split_terms(jnp.where(keep[j], sp, 0.0) if masked else sp)
                sums = {hj: _dot(terms[hj], uo) for hj in tiles}
                acc, new = st[0], []
                for h in range(2):
                    run, att = st[1 + h], [None] * SB_GROUP
                    for j in reversed(range(SB_GROUP)):
                        a = jnp.exp2(base[h, j] - (sums[h, j][:, :t] + run))
                        att[j] = (jnp.where(keep[j], a, 0.0) if masked else a).astype(BF16)
                        run = run + sums[h, j][:, t:]
                    acc = acc + _dot(jnp.concatenate(att, axis=1), (vf * hm[h]).astype(BF16))
                    new.append(run)
                return (acc, *new)

            zero = jnp.zeros((t, LANES), F32)
            st = group(diag, (zero, zero, zero), True)
            st = lax.fori_loop(0, diag, lambda jj, sv: group(diag - 1 - jj, sv, False), st)
            o_ref[pl.ds(qs, t), :] = st[0]
            tot_ref[pl.ds(qs, t), :] = st[1] * hm[0] + st[2] * hm[1]
            return carry0

        lax.fori_loop(0, nq, qloop, 0)

    out = jax.ShapeDtypeStruct((s, BRANCH_W), F32)
    return pl.pallas_call(
        body, name=name, grid=(SB_HEADS // 2,),
        in_specs=[_col_spec(s, PB_SB), _col_spec(s, PB_SB + 4), _col_spec(s, PB_SB + 8)],
        out_specs=[_col_spec(s, 0)] * 2, out_shape=[out] * 2,
        compiler_params=_params("parallel"),
    )(proj, proj, proj)


def _sb_bwd(proj, tot, do, *, name):
    s = proj.shape[0]
    t = SB_TILE
    nq = s // t

    def body(q_ref, k_ref, v_ref, tot_ref, do_ref, dq_ref, dk_ref, dv_ref, dk_acc, dv_acc):
        dk_acc[...] = jnp.zeros_like(dk_acc)
        dv_acc[...] = jnp.zeros_like(dv_acc)
        cmr = lax.broadcasted_iota(jnp.int32, (t, t), 1) - lax.broadcasted_iota(jnp.int32, (t, t), 0)
        u_le = _sb_sum_matrix(lambda row, col: row <= col)
        u_lt = _sb_sum_matrix(lambda row, col: row < col)
        hm = _sb_head_masks()

        def qloop(i, carry0):
            qs = pl.multiple_of(i * t, t)
            qraw = q_ref[pl.ds(qs, t), :]
            dov = do_ref[pl.ds(qs, t), :].astype(F32)
            totv = tot_ref[pl.ds(qs, t), :]
            qh = [(qraw * (hm[h] * SB_SCALE2)).astype(BF16) for h in range(2)]
            q2 = jnp.concatenate([(qraw * hm[h]).astype(BF16) for h in range(2)], axis=0)
            doh = [(dov * hm[h]).astype(BF16) for h in range(2)]
            do2 = jnp.concatenate(doh, axis=0)
            tot = [jnp.max(totv * hm[h], axis=1, keepdims=True) for h in range(2)]
            diag = i // SB_GROUP

            def group(g, st, masked):
                ks = pl.multiple_of(g * (SB_GROUP * t), SB_GROUP * t)
                kf = k_ref[pl.ds(ks, SB_GROUP * t), :]
                kb = kf.astype(BF16)
                vb = v_ref[pl.ds(ks, SB_GROUP * t), :].astype(BF16)
                tiles = [(h, j) for h in range(2) for j in range(SB_GROUP)]
                z = [_dot(qh[h], kb, NT) for h in range(2)]
                datt = [_dot(doh[h], vb, NT) for h in range(2)]
                keep = {j: cmr < (i - g * SB_GROUP - j) * t for j in range(SB_GROUP)} if masked else None
                ls, lterms = {}, {}
                for h, j in tiles:
                    zj = z[h][:, j * t:(j + 1) * t]
                    sp = _softplus2(zj)
                    ls[h, j] = zj - sp
                    lterms[h, j] = _split_terms(jnp.where(keep[j], sp, 0.0) if masked else sp)
                lsum = {hj: _dot(lterms[hj], u_le) for hj in tiles}
                att, p, pterms, new_c = {}, {}, {}, []
                for h in range(2):
                    run = st[1 + h]
                    for j in range(SB_GROUP):
                        a = jnp.exp2(ls[h, j] - ((tot[h] - run) - lsum[h, j][:, :t]))
                        if masked:
                            a = jnp.where(keep[j], a, 0.0)
                        att[h, j] = a.astype(BF16)
                        p[h, j] = a * datt[h][:, j * t:(j + 1) * t]
                        pterms[h, j] = _split_terms(p[h, j])
                        run = run + lsum[h, j][:, t:]
                    new_c.append(run)
                psum = {hj: _dot(pterms[hj], u_lt) for hj in tiles}
                dzb, new_r = {}, []
                for h in range(2):
                    run = st[3 + h]
                    for j in range(SB_GROUP):
                        sig = jnp.exp2(ls[h, j])
                        dz = p[h, j] * (1.0 - sig) - (run + psum[h, j][:, :t]) * sig
                        if masked:
                            dz = jnp.where(keep[j], dz, 0.0)
                        dzb[h, j] = (dz * SB_SCALE).astype(BF16)
                        run = run + psum[h, j][:, t:]
                    new_r.append(run)
                k2 = jnp.concatenate([(kf * hm[h]).astype(BF16) for h in range(2)], axis=0)
                dq_acc = st[0] + _dot(jnp.concatenate([dzb[hj] for hj in tiles], axis=1), k2)
                for j in range(SB_GROUP):
                    rows = pl.ds(pl.multiple_of(ks + j * t, t), t)
                    dk_acc[rows, :] += _dot(jnp.concatenate([dzb[0, j], dzb[1, j]], axis=0), q2, TN)
                    dv_acc[rows, :] += _dot(jnp.concatenate([att[0, j], att[1, j]], axis=0), do2, TN)
                return (dq_acc, *new_c, *new_r)

            zero = jnp.zeros((t, LANES), F32)
            st = lax.fori_loop(0, diag, lambda jj, sv: group(jj, sv, False), (zero,) * 5)
            st = group(diag, st, True)
            dq_ref[pl.ds(qs, t), :] = st[0].astype(dq_ref.dtype)
            return carry0

        lax.fori_loop(0, nq, qloop, 0)
        dk_ref[...] = dk_acc[...].astype(dk_ref.dtype)
        dv_ref[...] = dv_acc[...].astype(dv_ref.dtype)

    out = jax.ShapeDtypeStruct((s, BRANCH_W), BF16)
    return pl.pallas_call(
        body, name=name, grid=(SB_HEADS // 2,),
        in_specs=[_col_spec(s, PB_SB), _col_spec(s, PB_SB + 4), _col_spec(s, PB_SB + 8), _col_spec(s, 0), _col_spec(s, 0)],
        out_specs=[_col_spec(s, 0)] * 3, out_shape=[out] * 3,
        scratch_shapes=[pltpu.VMEM((s, LANES), F32), pltpu.VMEM((s, LANES), F32)],
        compiler_params=_params("parallel"),
    )(proj, proj, proj, tot, do)


def _sc_fwd(proj, conv_w, *, name):
    s = proj.shape[0]

    def body(x_ref, b_ref, c_ref, w_ref, y_ref):
        rows = lax.broadcasted_iota(jnp.int32, (s, LANES), 0)
        w = w_ref[...]
        u = c_ref[...] * x_ref[...]
        cv = w[2:3, :] * u + w[1:2, :] * _shift_down(u, 1, rows) + w[0:1, :] * _shift_down(u, 2, rows)
        y_ref[...] = (b_ref[...] * cv).astype(y_ref.dtype)

    return pl.pallas_call(
        body, name=name, grid=(BRANCH_W // LANES,),
        in_specs=[_col_spec(s, PB_SCX), _col_spec(s, PB_SCB), _col_spec(s, PB_SCC), pl.BlockSpec((3, LANES), lambda j: (0, j))],
        out_specs=_col_spec(s, 0), out_shape=jax.ShapeDtypeStruct((s, BRANCH_W), BF16),
        compiler_params=_params("parallel"),
    )(proj, proj, proj, conv_w)


def _sc_bwd(proj, conv_w, dy, *, name):
    s = proj.shape[0]

    def body(x_ref, b_ref, c_ref, w_ref, dy_ref, dx_ref, db_ref, dc_ref, dw_ref):
        rows = lax.broadcasted_iota(jnp.int32, (s, LANES), 0)
        w, x, cg, dyv = w_ref[...], x_ref[...], c_ref[...], dy_ref[...].astype(F32)
        u = cg * x
        u1, u2 = _shift_down(u, 1, rows), _shift_down(u, 2, rows)
        cv = w[2:3, :] * u + w[1:2, :] * u1 + w[0:1, :] * u2
        db_ref[...] = (dyv * cv).astype(db_ref.dtype)
        dcv = dyv * b_ref[...]
        du = w[2:3, :] * dcv + w[1:2, :] * _shift_up(dcv, 1, rows) + w[0:1, :] * _shift_up(dcv, 2, rows)
        dx_ref[...] = (du * cg).astype(dx_ref.dtype)
        dc_ref[...] = (du * x).astype(dc_ref.dtype)
        dw_ref[0:1, :] = jnp.sum(dcv * u2, axis=0, keepdims=True)
        dw_ref[1:2, :] = jnp.sum(dcv * u1, axis=0, keepdims=True)
        dw_ref[2:3, :] = jnp.sum(dcv * u, axis=0, keepdims=True)

    out = jax.ShapeDtypeStruct((s, BRANCH_W), BF16)
    wspec = pl.BlockSpec((3, LANES), lambda j: (0, j))
    return pl.pallas_call(
        body, name=name, grid=(BRANCH_W // LANES,),
        in_specs=[_col_spec(s, PB_SCX), _col_spec(s, PB_SCB), _col_spec(s, PB_SCC), wspec, _col_spec(s, 0)],
        out_specs=[_col_spec(s, 0)] * 3 + [wspec],
        out_shape=[out] * 3 + [jax.ShapeDtypeStruct((3, BRANCH_W), F32)],
        compiler_params=_params("parallel"),
    )(proj, proj, proj, conv_w, dy)


MERGE_TM, MERGE_TN = 512, D_MODEL // N_CHIPS


def _merge_specs():
    tm, tn = MERGE_TM, MERGE_TN
    y_spec = pl.BlockSpec((tm, BRANCH_W), lambda i, j: (i, 0))
    w_spec = pl.BlockSpec((None, N_BRANCH, BRANCH_W, tn), lambda i, j: (j, 0, 0, 0))
    gate_specs = [pl.BlockSpec((tm, tn), functools.partial(
        lambda i, j, b: (i, (PB_GATES * LANES + b * D_MODEL) // tn + j), b=b)) for b in range(N_BRANCH)]
    mn = pl.BlockSpec((tm, tn), lambda i, j: (i, j))
    return y_spec, w_spec, gate_specs, mn


def _merge_fwd(ya, yb, yc, wb, proj, *, name):
    s = ya.shape[0]
    y_spec, w_spec, gate_specs, mn = _merge_specs()

    def body(ya_ref, yb_ref, yc_ref, w_ref, g0, g1, g2, o_ref):
        acc = None
        for b, (y_ref, g_ref) in enumerate(zip((ya_ref, yb_ref, yc_ref), (g0, g1, g2))):
            term = _sigmoid(g_ref[...]) * _bdot(y_ref[...], w_ref[b])
            acc = term if acc is None else acc + term
        o_ref[...] = acc.astype(o_ref.dtype)

    return pl.pallas_call(
        body, name=name, grid=(s // MERGE_TM, D_MODEL // MERGE_TN),
        in_specs=[y_spec] * 3 + [w_spec] + gate_specs, out_specs=mn,
        out_shape=jax.ShapeDtypeStruct((s, D_MODEL), BF16), compiler_params=_params("parallel", "parallel"),
    )(ya, yb, yc, wb, proj, proj, proj)


def _merge_bwd(ya, yb, yc, wb, proj, dm, *, name):
    s = ya.shape[0]
    y_spec, w_spec, gate_specs, mn = _merge_specs()

    def body(ya_ref, yb_ref, yc_ref, w_ref, g0, g1, g2, dm_ref, *outs):
        dmv = dm_ref[...].astype(F32)
        for b, (y_ref, g_ref) in enumerate(zip((ya_ref, yb_ref, yc_ref), (g0, g1, g2))):
            sg = _sigmoid(g_ref[...])
            z = _bdot(y_ref[...], w_ref[b])
            outs[b][...] = (dmv * sg).astype(BF16)
            outs[N_BRANCH + b][...] = (dmv * z * sg * (1.0 - sg)).astype(BF16)

    out = jax.ShapeDtypeStruct((s, D_MODEL), BF16)
    res = pl.pallas_call(
        body, name=name, grid=(s // MERGE_TM, D_MODEL // MERGE_TN),
        in_specs=[y_spec] * 3 + [w_spec] + gate_specs + [mn], out_specs=[mn] * (2 * N_BRANCH),
        out_shape=[out] * (2 * N_BRANCH), compiler_params=_params("parallel", "parallel"),
    )(ya, yb, yc, wb, proj, proj, proj, dm)
    return res[:N_BRANCH], res[N_BRANCH:]


def _chunk_rows(v, s):
    return v[:, :2 * GDN_HEADS].reshape(s // CHUNK, CHUNK, 2 * GDN_HEADS).transpose(0, 2, 1)


def _relu2_epi(acc):
    r = jnp.maximum(acc, 0.0)
    return acc, r * r


def _drelu2_epi(acc, a):
    return (acc * (2.0 * jnp.maximum(a.astype(F32), 0.0)),)


def _layer_fwd(x0, p):
    s = x0.shape[0]
    h1 = _norm_fwd(x0, p["norm_mix_pre"], name="norm_mix_pre")
    proj = _matmul(h1, p["w_in"], name="proj_in", tm=512, tn=1664)
    qkvn = _gdn_pre_fwd(proj, p["conv_qkv_w"], name="gdn_pre")
    gb = _gdn_gates_fwd(proj, p["gdn_a_log"], p["gdn_dt_bias"], name="gdn_gates")
    gbt = _chunk_rows(gb, s)
    o_gdn, states = _gdn_chunk_fwd(qkvn, gb, gbt, name="gdn_chunk")
    ya = _gdn_post_fwd(o_gdn, proj, p["gdn_norm_w"], name="gdn_post")
    o_sb, sb_tot = _sb_fwd(proj, name="sb_attn")
    yc = _sc_fwd(proj, p["conv_sc_w"], name="short_conv")
    merged = _merge_fwd(ya, o_sb, yc, p["w_branch"], proj, name="merge")
    u = _matmul(merged, p["w_out"], name="proj_out", tm=512, tn=1024)
    x1 = _resnorm_fwd(x0, u, p["norm_mix_post"], name="norm_mix_post")
    h2 = _norm_fwd(x1, p["norm_ffn_pre"], name="norm_ffn_pre")
    a, r = _matmul(h2, p["w_ff1"], name="ff1", tm=512, tn=1024, outs=(BF16, BF16), epi=_relu2_epi, b_chips=True)
    f = _matmul(r, p["w_ff2"], name="ff2", tm=512, tn=1024, tk=1024)
    x2 = _resnorm_fwd(x1, f, p["norm_ffn_post"], name="norm_ffn_post")
    saved = dict(x0=x0, h1=h1, proj=proj, qkvn=qkvn, gb=gb, gbt=gbt, o_gdn=o_gdn, states=states, ya=ya, o_sb=o_sb,
                 sb_tot=sb_tot, yc=yc, merged=merged, u=u, x1=x1, h2=h2, a=a, r=r, f=f)
    return x2, saved


def _layer_bwd(dx2, p, sv):
    g = {}
    df, g["norm_ffn_post"] = _norm_bwd(sv["f"], p["norm_ffn_post"], dx2, None, out_dtype=BF16, name="norm_ffn_post_bwd")
    g["w_ff2"] = _matmul(sv["r"], df, ta=True, name="ff2_dw", tm=1024, tn=1024, tk=512)
    da = _matmul(df, p["w_ff2"], tb=True, name="ff2_dx", tm=512, tn=1024, outs=(BF16,), epi=_drelu2_epi,
                 extras=(sv["a"],))
    g["w_ff1"] = _matmul(sv["h2"], da, ta=True, name="ff1_dw", tm=1024, tn=1024, tk=512, out_chips=True)
    dh2 = _matmul(da, p["w_ff1"], tb=True, name="ff1_dx", tm=512, tn=1024, tk=1024, b_chips=True)
    dx1, g["norm_ffn_pre"] = _norm_bwd(sv["x1"], p["norm_ffn_pre"], dh2, dx2, out_dtype=F32, name="norm_ffn_pre_bwd")
    du, g["norm_mix_post"] = _norm_bwd(sv["u"], p["norm_mix_post"], dx1, None, out_dtype=BF16, name="norm_mix_post_bwd")
    g["w_out"] = _matmul(sv["merged"], du, ta=True, name="out_dw", tm=1024, tn=1024, tk=512)
    dmerged = _matmul(du, p["w_out"], tb=True, name="out_dx", tm=512, tn=1024, outs=(BF16,))
    ys = (sv["ya"], sv["o_sb"], sv["yc"])
    dz, dgates = _merge_bwd(*ys, p["w_branch"], sv["proj"], dmerged, name="merge_bwd")
    g["w_branch"] = jnp.stack([_matmul(ys[b], dz[b], ta=True, name=f"branch_dw{b}", tm=512, tn=256, tk=1024, out_chips=True)
                               for b in range(N_BRANCH)], axis=1)
    dys = [_matmul(dz[b], p["w_branch"][:, b], tb=True, name=f"branch_dx{b}", tm=1024, tn=512, tk=256, b_chips=True)
           for b in range(N_BRANCH)]
    dscx, dscb, dscc, g["conv_sc_w"] = _sc_bwd(sv["proj"], p["conv_sc_w"], dys[2], name="short_conv_bwd")
    dsq, dsk, dsv = _sb_bwd(sv["proj"], sv["sb_tot"], dys[1], name="sb_attn_bwd")
    do_gdn, dgate, dnw = _gdn_post_bwd(sv["o_gdn"], sv["proj"], p["gdn_norm_w"], dys[0], name="gdn_post_bwd")
    g["gdn_norm_w"] = dnw
    dqkvn, dgb = _gdn_chunk_bwd(sv["qkvn"], sv["gb"], sv["gbt"], sv["states"], do_gdn, name="gdn_chunk_bwd")
    dqkv, g["conv_qkv_w"] = _gdn_pre_bwd(sv["proj"], p["conv_qkv_w"], dqkvn, name="gdn_pre_bwd")
    dab, g["gdn_a_log"], g["gdn_dt_bias"] = _gdn_gates_bwd(sv["proj"], p["gdn_a_log"], p["gdn_dt_bias"], dgb,
                                                           name="gdn_gates_bwd")
    dproj = jnp.concatenate([*dgates, dqkv, dgate, dab, dsq, dsk, dsv, dscx, dscb, dscc], axis=1)
    g["w_in"] = _matmul(sv["h1"], dproj, ta=True, name="in_dw", tm=1024, tn=1664, tk=512)
    dh1 = _matmul(dproj, p["w_in"], tb=True, name="in_dx", tm=512, tn=1024, tk=1664)
    dx0, g["norm_mix_pre"] = _norm_bwd(sv["x0"], p["norm_mix_pre"], dh1, dx1, out_dtype=F32, name="norm_mix_pre_bwd")
    return dx0, g


def _local_step(x, target, n_layers, weights_of, grads_done):
    saved, layers = [], []
    h = x
    for l in range(n_layers):
        p = weights_of(l, h)
        h, sv = _layer_fwd(h, p)
        saved.append(sv)
        layers.append(p)
    loss, dh = _loss_fwd_bwd(h, target, name="loss")
    for l in reversed(range(n_layers)):
        dh, g = _layer_bwd(dh, layers[l], saved[l])
        zero = grads_done(l, g)
        if l > 0:
            layers[l - 1] = dict(layers[l - 1], norm_ffn_post=layers[l - 1]["norm_ffn_post"] + zero)
    return loss, dh


ANY = pl.BlockSpec(memory_space=pl.ANY)


def _me_and_chips():
    x, y, c = lax.axis_index("x"), lax.axis_index("y"), lax.axis_index("c")
    chips = [(1 - x, y), (x, 1 - y), (1 - x, 1 - y)]
    return x, y, c, chips


def _gather_devices(small, *, name):
    def body(small_ref, small_out, ssend, srecv, local_sem):
        x, y, c, chips = _me_and_chips()
        dev = 4 * x + 2 * y + c
        lc = pltpu.make_async_copy(small_ref, small_out.at[dev], local_sem)
        lc.start()
        peers = [(x, y, 1 - c)] + [(px, py, pc) for (px, py) in chips for pc in (c, 1 - c)]
        sends = []
        for k, peer in enumerate(peers):
            cp = pltpu.make_async_remote_copy(src_ref=small_ref, dst_ref=small_out.at[dev], send_sem=ssend.at[k],
                                              recv_sem=srecv.at[k], device_id=peer, device_id_type=MESH)
            cp.start()
            sends.append(cp)
        for k, (px, py, pc) in enumerate(peers):
            pltpu.make_async_remote_copy(src_ref=small_ref, dst_ref=small_out.at[4 * px + 2 * py + pc], send_sem=ssend.at[k],
                                         recv_sem=srecv.at[k], device_id=(px, py, pc), device_id_type=MESH).wait_recv()
        for cp in sends:
            cp.wait_send()
        lc.wait()

    return pl.pallas_call(
        body, name=name, in_specs=[ANY], out_specs=ANY,
        out_shape=jax.ShapeDtypeStruct((N_DEV,) + small.shape, small.dtype),
        scratch_shapes=[pltpu.SemaphoreType.DMA((N_DEV - 1,)), pltpu.SemaphoreType.DMA((N_DEV - 1,)), pltpu.SemaphoreType.DMA],
    )(small)


HBM = pl.BlockSpec(memory_space=pltpu.HBM)
SEM = pl.BlockSpec(memory_space=pltpu.SEMAPHORE)
EFFECT = pltpu.SideEffectType.DATAFLOW_SIDE_EFFECTING


def _exchange_start(srcs, *, by_slot, name, after=None):
    n = len(srcs)
    n_in = 2 * n + (after is not None)
    land_shapes = [a.shape if by_slot else (N_CHIPS,) + a.shape for a in srcs]
    lands = [pltpu.with_memory_space_constraint(lax.empty(sh, a.dtype), pltpu.HBM) for sh, a in zip(land_shapes, srcs)]
    srcs = [pltpu.with_memory_space_constraint(a, pltpu.HBM) for a in srcs]

    def body(*refs):
        ins, land = refs[:n], refs[n:2 * n]
        send_sems, recv_sems, token = refs[n_in], refs[n_in + 1], refs[-1]
        x, y, c, chips = _me_and_chips()
        me = 2 * x + y
        for a in range(n):
            for k, (px, py) in enumerate(chips):
                pltpu.make_async_remote_copy(
                    src_ref=ins[a].at[2 * px + py] if by_slot else ins[a], dst_ref=land[a].at[me],
                    send_sem=send_sems.at[3 * a + k], recv_sem=recv_sems.at[3 * a + k], device_id=(px, py, c),
                    device_id_type=MESH).start()
        token[...] = jnp.zeros_like(token)

    res = pl.pallas_call(
        body, name=name, in_specs=[HBM] * (2 * n) + ([ANY] if after is not None else []),
        out_specs=[SEM, SEM] + [HBM] * (2 * n) + [pl.BlockSpec(memory_space=pltpu.VMEM)],
        out_shape=[pltpu.SemaphoreType.DMA((3 * n,)), pltpu.SemaphoreType.DMA((3 * n,))]
        + [pltpu.HBM(a.shape, a.dtype) for a in srcs] + [pltpu.HBM(sh, a.dtype) for sh, a in zip(land_shapes, srcs)]
        + [jax.ShapeDtypeStruct((8, LANES), F32)],
        input_output_aliases={i: 2 + i for i in range(2 * n)},
        compiler_params=pltpu.CompilerParams(has_side_effects=EFFECT),
    )(*srcs, *lands, *([after] if after is not None else []))
    return dict(send=res[0], recv=res[1], srcs=res[2:2 + n], lands=res[2 + n:2 + 2 * n], token=res[-1])


def _exchange_wait(ex, after, *, by_slot, name):
    n = len(ex["srcs"])

    def body(*refs):
        ins, land = refs[:n], refs[n:2 * n]
        send_sems, recv_sems = refs[2 * n], refs[2 * n + 1]
        x, y, c, chips = _me_and_chips()
        me = 2 * x + y
        for a in range(n):
            for k, (px, py) in enumerate(chips):
                cp = pltpu.make_async_remote_copy(
                    src_ref=ins[a].at[me] if by_slot else ins[a], dst_ref=land[a].at[2 * px + py],
                    send_sem=send_sems.at[3 * a + k], recv_sem=recv_sems.at[3 * a + k], device_id=(px, py, c),
                    device_id_type=MESH)
                cp.wait_send()
                cp.wait_recv()

    res = pl.pallas_call(
        body, name=name, in_specs=[HBM] * (2 * n) + [SEM, SEM, ANY], out_specs=[HBM] * (2 * n),
        out_shape=[pltpu.HBM(a.shape, a.dtype) for a in ex["srcs"]] + [pltpu.HBM(a.shape, a.dtype) for a in ex["lands"]],
        input_output_aliases={i: i for i in range(2 * n)},
        compiler_params=pltpu.CompilerParams(has_side_effects=EFFECT),
    )(*ex["srcs"], *ex["lands"], ex["send"], ex["recv"], after)
    return res[:n], res[n:]


def _chip_index():
    return 2 * lax.axis_index("x") + lax.axis_index("y")


def _me_operand():
    return jnp.reshape(_chip_index(), (1,)).astype(jnp.int32)


def _place_own(land, own, *, name):
    rows, cols = _as2d(own).shape
    tr = _row_tile(rows, cols)

    def body(me_ref, own_ref, land_ref, out_ref):
        out_ref[...] = own_ref[...]

    res = pl.pallas_call(
        body, name=name,
        grid_spec=pltpu.PrefetchScalarGridSpec(
            num_scalar_prefetch=1, grid=(rows // tr,),
            in_specs=[pl.BlockSpec((tr, cols), lambda i, me: (i, 0)), ANY],
            out_specs=pl.BlockSpec((None, tr, cols), lambda i, me: (me[0], i, 0))),
        out_shape=jax.ShapeDtypeStruct((N_CHIPS, rows, cols), land.dtype), input_output_aliases={2: 0},
        compiler_params=_params("arbitrary"),
    )(_me_operand(), _as2d(own), land.reshape(N_CHIPS, rows, cols))
    return res.reshape(land.shape)


def _sum_partials(lands, parts, *, name):
    n, rows, cols = lands.shape
    tr = _row_tile(rows, cols, 1024 * 1024)

    def body(me_ref, land_ref, own_ref, o_ref):
        me = me_ref[0]
        acc = None
        for i in range(n):
            term = jnp.where(me == i, own_ref[...], land_ref[i]).astype(F32)
            acc = term if acc is None else acc + term
        o_ref[...] = acc

    return pl.pallas_call(
        body, name=name,
        grid_spec=pltpu.PrefetchScalarGridSpec(
            num_scalar_prefetch=1, grid=(rows // tr,),
            in_specs=[pl.BlockSpec((n, tr, cols), lambda i, me: (0, i, 0)),
                      pl.BlockSpec((None, tr, cols), lambda i, me: (me[0], i, 0))],
            out_specs=pl.BlockSpec((tr, cols), lambda i, me: (i, 0))),
        out_shape=jax.ShapeDtypeStruct((rows, cols), F32), compiler_params=_params("arbitrary"),
    )(_me_operand(), lands, parts)


def _swap_sibling(arrs, *, name):
    n = len(arrs)

    def body(*refs):
        ins, outs = refs[:n], refs[n:2 * n]
        send_sems, recv_sems = refs[2 * n:]
        x, y, c = lax.axis_index("x"), lax.axis_index("y"), lax.axis_index("c")
        cps = [pltpu.make_async_remote_copy(src_ref=ins[a], dst_ref=outs[a], send_sem=send_sems.at[a],
                                            recv_sem=recv_sems.at[a], device_id=(x, y, 1 - c), device_id_type=MESH)
               for a in range(n)]
        for cp in cps:
            cp.start()
        for cp in cps:
            cp.wait()

    return pl.pallas_call(
        body, name=name, in_specs=[ANY] * n, out_specs=[ANY] * n,
        out_shape=[jax.ShapeDtypeStruct(a.shape, a.dtype) for a in arrs],
        scratch_shapes=[pltpu.SemaphoreType.DMA((n,)), pltpu.SemaphoreType.DMA((n,))],
    )(*arrs)


def _row_tile(rows, cols, budget=2 * 1024 * 1024):
    best = None
    for t in range(16, rows + 1, 16):
        if rows % t == 0 and t * cols * 4 <= budget:
            best = t
    return best if best is not None else rows


def _sum_slots(parts, *, name):
    n, rows, cols = parts.shape
    tr = _row_tile(rows, cols, 1024 * 1024)

    def body(p_ref, o_ref):
        acc = p_ref[0].astype(F32)
        for i in range(1, n):
            acc = acc + p_ref[i].astype(F32)
        o_ref[...] = acc

    return pl.pallas_call(
        body, name=name, grid=(rows // tr,), in_specs=[pl.BlockSpec((n, tr, cols), lambda i: (0, i, 0))],
        out_specs=pl.BlockSpec((tr, cols), lambda i: (i, 0)), out_shape=jax.ShapeDtypeStruct((rows, cols), F32),
        compiler_params=_params("parallel"),
    )(parts)


def _adamw(w, m, v, g_a, g_b, *, name):
    rows, cols = w.shape
    tr = _row_tile(rows, cols, 1024 * 1024)
    two = g_b is not None
    c1 = 1.0 / (1.0 - ADAM_B1 ** ADAM_STEP)
    c2 = 1.0 / (1.0 - ADAM_B2 ** ADAM_STEP)

    def body(*refs):
        w_ref, m_ref, v_ref, ga_ref = refs[:4]
        g_ref, d_ref, nm_ref, nv_ref = refs[4 + two:]
        g = ga_ref[...]
        if two:
            g = g + refs[4][...]
        nm = ADAM_B1 * m_ref[...] + (1.0 - ADAM_B1) * g
        nv = ADAM_B2 * v_ref[...] + (1.0 - ADAM_B2) * (g * g)
        g_ref[...] = g
        nm_ref[...] = nm
        nv_ref[...] = nv
        d_ref[...] = -ADAM_LR * ((nm * c1) / (jnp.sqrt(nv * c2) + ADAM_EPS) + ADAM_WD * w_ref[...])

    blk = pl.BlockSpec((tr, cols), lambda i: (i, 0))
    ins = [w, m, v, g_a] + ([g_b] if two else [])
    return pl.pallas_call(
        body, name=name, grid=(rows // tr,), in_specs=[blk] * len(ins), out_specs=[blk] * 4,
        out_shape=[jax.ShapeDtypeStruct((rows, cols), F32)] * 4, compiler_params=_params("parallel"),
    )(*ins)


def _cast_bf16(w, *, name):
    rows, cols = w.shape
    tr = _row_tile(rows, cols)

    def body(w_ref, o_ref):
        o_ref[...] = w_ref[...].astype(BF16)

    blk = pl.BlockSpec((tr, cols), lambda i: (i, 0))
    return pl.pallas_call(body, name=name, grid=(rows // tr,), in_specs=[blk], out_specs=blk,
                          out_shape=jax.ShapeDtypeStruct((rows, cols), BF16), compiler_params=_params("parallel"))(w)


BIG = ("w_in", "w_branch", "w_out", "w_ff1", "w_ff2")
SMALL = ("norm_mix_pre", "conv_qkv_w", "gdn_a_log", "gdn_dt_bias", "gdn_norm_w", "conv_sc_w", "norm_mix_post",
         "norm_ffn_pre", "norm_ffn_post")
ORDER = ("norm_mix_pre", "w_in", "conv_qkv_w", "gdn_a_log", "gdn_dt_bias", "gdn_norm_w", "conv_sc_w", "w_branch",
         "w_out", "norm_mix_post", "norm_ffn_pre", "w_ff1", "w_ff2", "norm_ffn_post")


def _full_weights(big, conv, rep, l):
    p = dict(
        w_in=_in_cols_from_chips(big["w_in"]),
        w_branch=big["w_branch"],
        w_out=big["w_out"].reshape(D_MODEL, D_MODEL),
        w_ff1=big["w_ff1"],
        w_ff2=big["w_ff2"].reshape(D_FF, D_MODEL),
        conv_qkv_w=conv["conv_qkv_w"][:, l].transpose(1, 0, 2).reshape(4, 3 * BRANCH_W),
        conv_sc_w=conv["conv_sc_w"][:, l].transpose(1, 0, 2).reshape(3, BRANCH_W),
    )
    for k in ("norm_mix_pre", "gdn_a_log", "gdn_dt_bias", "gdn_norm_w", "norm_mix_post", "norm_ffn_pre", "norm_ffn_post"):
        p[k] = rep[k][l]
    return p


def _partials_by_chip(g):
    parts = dict(
        w_in=_in_cols_to_chips(g["w_in"]),
        w_branch=g["w_branch"],
        w_out=g["w_out"].reshape(N_CHIPS, D_MODEL // N_CHIPS, D_MODEL),
        w_ff1=g["w_ff1"],
        w_ff2=g["w_ff2"].reshape(N_CHIPS, D_FF // N_CHIPS, D_MODEL),
    )
    return [parts[k].astype(BF16) for k in BIG]


def _pack_small(grads):
    pieces, layout = [], []
    for name in SMALL:
        v = jnp.stack([g[name] for g in grads]).astype(F32)
        layout.append((name, v.shape))
        pieces.append(v.reshape(-1))
    flat = jnp.concatenate(pieces)
    rows = -(-flat.shape[0] // LANES)
    rows = -(-rows // 8) * 8
    flat = jnp.pad(flat, (0, rows * LANES - flat.shape[0]))
    return flat.reshape(rows, LANES), layout


def _unpack_small(table, layout):
    flat, out, off = table.reshape(-1), {}, 0
    for name, shape in layout:
        size = 1
        for d in shape:
            size *= d
        out[name] = flat[off:off + size].reshape(shape)
        off += size
    return out


def _as2d(a):
    return a.reshape(-1, a.shape[-1]) if a.ndim > 1 else a.reshape(1, -1)


def kernel(x, norm_mix_pre, w_in, conv_qkv_w, gdn_a_log, gdn_dt_bias, gdn_norm_w, conv_sc_w, w_branch, w_out, norm_mix_post, norm_ffn_pre, w_ff1, w_ff2, norm_ffn_post, loss_target, m_norm_mix_pre, m_w_in, m_conv_qkv_w, m_gdn_a_log, m_gdn_dt_bias, m_gdn_norm_w, m_conv_sc_w, m_w_branch, m_w_out, m_norm_mix_post, m_norm_ffn_pre, m_w_ff1, m_w_ff2, m_norm_ffn_post, v_norm_mix_pre, v_w_in, v_conv_qkv_w, v_gdn_a_log, v_gdn_dt_bias, v_gdn_norm_w, v_conv_sc_w, v_w_branch, v_w_out, v_norm_mix_post, v_norm_ffn_pre, v_w_ff1, v_w_ff2, v_norm_ffn_post):
    w = dict(norm_mix_pre=norm_mix_pre, w_in=w_in, conv_qkv_w=conv_qkv_w, gdn_a_log=gdn_a_log, gdn_dt_bias=gdn_dt_bias,
             gdn_norm_w=gdn_norm_w, conv_sc_w=conv_sc_w, w_branch=w_branch, w_out=w_out, norm_mix_post=norm_mix_post,
             norm_ffn_pre=norm_ffn_pre, w_ff1=w_ff1, w_ff2=w_ff2, norm_ffn_post=norm_ffn_post)
    m = dict(norm_mix_pre=m_norm_mix_pre, w_in=m_w_in, conv_qkv_w=m_conv_qkv_w, gdn_a_log=m_gdn_a_log,
             gdn_dt_bias=m_gdn_dt_bias, gdn_norm_w=m_gdn_norm_w, conv_sc_w=m_conv_sc_w, w_branch=m_w_branch, w_out=m_w_out,
             norm_mix_post=m_norm_mix_post, norm_ffn_pre=m_norm_ffn_pre, w_ff1=m_w_ff1, w_ff2=m_w_ff2,
             norm_ffn_post=m_norm_ffn_post)
    v = dict(norm_mix_pre=v_norm_mix_pre, w_in=v_w_in, conv_qkv_w=v_conv_qkv_w, gdn_a_log=v_gdn_a_log,
             gdn_dt_bias=v_gdn_dt_bias, gdn_norm_w=v_gdn_norm_w, conv_sc_w=v_conv_sc_w, w_branch=v_w_branch, w_out=v_w_out,
             norm_mix_post=v_norm_mix_post, norm_ffn_pre=v_norm_ffn_pre, w_ff1=v_w_ff1, w_ff2=v_w_ff2,
             norm_ffn_post=v_norm_ffn_post)

    me = _chip_index()

    shards = {k: _cast_bf16(_as2d(w[k]), name=f"cast_{k}").reshape(w[k].shape) for k in BIG}
    conv_names = ("conv_qkv_w", "conv_sc_w")
    def gather_start(l, after):
        return _exchange_start([shards[k][l] for k in BIG] + ([w[k] for k in conv_names] if l == 0 else []),
                               by_slot=False, name=f"gather_start{l}", after=after)

    gathers = {0: gather_start(0, None)}
    conv = {}

    def weights_of(l, x_l):
        own, lands = _exchange_wait(gathers[l], gathers[0]["token"] if l == 0 else x_l, by_slot=False, name=f"gather_wait{l}")
        names = BIG + (conv_names if l == 0 else ())
        full = {k: _place_own(land, o, name=f"own_{k}") for k, land, o in zip(names, lands, own)}
        if l == 0:
            conv.update({k: full[k] for k in conv_names})
        p = _full_weights(full, conv, w, l)
        if l + 1 < DEPTH:
            gathers[l + 1] = gather_start(l + 1, full["w_out"])
            p["norm_mix_pre"] = p["norm_mix_pre"] + gathers[l + 1]["token"][0, 0]
        return p

    grads, scatters = [None] * DEPTH, [None] * DEPTH

    def grads_done(l, g):
        grads[l] = g
        scatters[l] = _exchange_start(_partials_by_chip(g), by_slot=True, name=f"scatter_start{l}")
        return scatters[l]["token"][0, 0]

    loss, dx = _local_step(x[0], loss_target[0], DEPTH, weights_of, grads_done)
    loss = lax.psum(loss, ("x", "y", "c"))

    sums = []
    for l in range(DEPTH):
        parts, lands = _exchange_wait(scatters[l], dx, by_slot=True, name=f"scatter_wait{l}")
        sums.append([_sum_partials(r.reshape(N_CHIPS, -1, r.shape[-1]), o.reshape(N_CHIPS, -1, o.shape[-1]), name=f"sum_{k}")
                     for k, r, o in zip(BIG, lands, parts)])
    mine = [jnp.concatenate([sums[l][i] for l in range(DEPTH)], axis=0) for i in range(len(BIG))]
    theirs = _swap_sibling(mine, name="swap_sibling")
    small, layout = _pack_small(grads)
    small_g = _unpack_small(_sum_slots(_gather_devices(small, name="gather_small"), name="sum_small"), layout)
    for k, width in (("conv_qkv_w", 3 * BRANCH_W // N_CHIPS), ("conv_sc_w", BRANCH_W // N_CHIPS)):
        small_g[k] = lax.dynamic_slice_in_dim(small_g[k], me * width, width, axis=2)

    out = {}
    for k, s_mine, s_theirs in zip(BIG, mine, theirs):
        res = _adamw(_as2d(w[k]), _as2d(m[k]), _as2d(v[k]), s_mine, s_theirs, name=f"adamw_{k}")
        out[k] = [r.reshape(w[k].shape) for r in res]
    for k in SMALL:
        res = _adamw(_as2d(w[k]), _as2d(m[k]), _as2d(v[k]), _as2d(small_g[k]), None, name=f"adamw_{k}")
        out[k] = [r.reshape(w[k].shape) for r in res]
    return (loss, dx[None], *[out[k][0] for k in ORDER], *[out[k][1] for k in ORDER], *[out[k][2] for k in ORDER],
            *[out[k][3] for k in ORDER])
```

```python
import functools

import jax
import jax.numpy as jnp
from jax import lax
from jax.experimental import pallas as pl
from jax.experimental.pallas import tpu as pltpu

F32 = jnp.float32
BF16 = jnp.bfloat16
MESH = pl.DeviceIdType.MESH

LANES = 128
D_MODEL = 1024
DEPTH = 4
CHUNK = 64
GDN_HEADS, GDN_DIM = 4, 128
SB_HEADS, SB_DIM = 8, 64
BRANCH_W = 512
N_BRANCH = 3
D_FF = 4 * D_MODEL
EPS = 1e-6
IN_W = 8200
AB_COL = 2048
AB_PAD = LANES - 8
IN_WP = IN_W + AB_PAD
N_CHIPS = 4
N_DEV = 8
GATES_COL = 5128
PB_GATES, PB_QKV, PB_GATE, PB_AB, PB_SB, PB_SCX, PB_SCB, PB_SCC = 0, 24, 36, 40, 41, 53, 57, 61
SB_TILE = 128
SB_GROUP = 4
SB_QTILE = 256
SB_SCALE = SB_DIM ** -0.5
SB_SCALE2 = SB_SCALE * 1.4426950408889634
GDN_QSCALE = GDN_DIM ** -0.5
VMEM_LIMIT = 56 * 1024 * 1024

ADAM_LR, ADAM_B1, ADAM_B2, ADAM_EPS, ADAM_WD, ADAM_STEP = 0.001, 0.9, 0.999, 1e-08, 0.01, 10

NT = (((1,), (1,)), ((), ()))
TN = (((0,), (0,)), ((), ()))
HI = lax.Precision.HIGH


def _pad_in_cols(w):
    return jnp.concatenate([w[:, GATES_COL:], w[:, :AB_COL + 8], jnp.zeros((w.shape[0], AB_PAD), w.dtype),
                            w[:, AB_COL + 8:GATES_COL]], axis=1)


def _unpad_in_cols(g):
    n_gates = IN_W - GATES_COL
    return jnp.concatenate([g[:, n_gates:n_gates + AB_COL + 8], g[:, n_gates + AB_COL + 8 + AB_PAD:], g[:, :n_gates]], axis=1)


IN_SHARD = IN_W // N_CHIPS
_IN_SEGMENTS = ((0, AB_COL + 8, IN_W - GATES_COL), (AB_COL + 8, GATES_COL, IN_W - GATES_COL + AB_PAD),
                (GATES_COL, IN_W, -GATES_COL))


def _in_cols_from_chips(slots):
    def cols(first, last):
        out = []
        for j in range(N_CHIPS):
            lo, hi = max(first, j * IN_SHARD), min(last, (j + 1) * IN_SHARD)
            if lo < hi:
                out.append(slots[j][:, lo - j * IN_SHARD:hi - j * IN_SHARD])
        return out

    head, tail, gates = (cols(first, last) for first, last, _ in _IN_SEGMENTS)
    return jnp.concatenate(gates + head + [jnp.zeros((slots.shape[1], AB_PAD), slots.dtype)] + tail, axis=1)


def _in_cols_to_chips(g):
    shards = []
    for j in range(N_CHIPS):
        pieces = []
        for first, last, shift in _IN_SEGMENTS:
            lo, hi = max(first, j * IN_SHARD), min(last, (j + 1) * IN_SHARD)
            if lo < hi:
                pieces.append(g[:, lo + shift:hi + shift])
        shards.append(jnp.concatenate(pieces, axis=1))
    return jnp.stack(shards)


def _params(*sem):
    return pltpu.CompilerParams(dimension_semantics=sem if sem else None, vmem_limit_bytes=VMEM_LIMIT)


def _sigmoid(x):
    return 1.0 / (1.0 + jnp.exp(-x))


def _softplus(x):
    return jnp.maximum(x, 0.0) + jnp.log(1.0 + jnp.exp(-jnp.abs(x)))


def _softplus2(x):
    return jnp.maximum(x, 0.0) + jnp.log2(1.0 + jnp.exp2(-jnp.abs(x)))


def _dot(a, b, dims=None, precision=None):
    if dims is None:
        return jnp.dot(a, b, preferred_element_type=F32, precision=precision)
    return lax.dot_general(a, b, dims, preferred_element_type=F32, precision=precision)


def _bdot(a, b, dims=None):
    return _dot(a.astype(BF16), b.astype(BF16), dims)


def _matmul(a, b, *, name, ta=False, tb=False, tm, tn, tk=None, outs=(F32,), epi=None, extras=(), b_chips=False,
            out_chips=False):
    if ta:
        kdim, m = a.shape
    else:
        m, kdim = a.shape
    if b_chips:
        per = b.shape[2]
        if tb:
            n, kb = b.shape[1], N_CHIPS * per
        else:
            kb, n = b.shape[1], N_CHIPS * per
    elif tb:
        n, kb = b.shape
    else:
        kb, n = b.shape
    assert kdim == kb, (a.shape, b.shape)
    tk = kdim if tk is None else tk
    assert m % tm == 0 and n % tn == 0 and kdim % tk == 0, (m, n, kdim, tm, tn, tk)
    nk = kdim // tk
    a_spec = pl.BlockSpec((tk, tm), lambda i, j, k: (k, i)) if ta else pl.BlockSpec((tm, tk), lambda i, j, k: (i, k))
    if b_chips and tb:
        assert per % tk == 0
        b_spec = pl.BlockSpec((None, tn, tk), lambda i, j, k: (k // (per // tk), j, k % (per // tk)))
    elif b_chips:
        assert per % tn == 0
        b_spec = pl.BlockSpec((None, tk, tn), lambda i, j, k: (j // (per // tn), k, j % (per // tn)))
    else:
        b_spec = pl.BlockSpec((tn, tk), lambda i, j, k: (j, k)) if tb else pl.BlockSpec((tk, tn), lambda i, j, k: (k, j))
    mn_spec = pl.BlockSpec((tm, tn), lambda i, j, k: (i, j))
    if out_chips:
        per_o = n // N_CHIPS
        assert per_o % tn == 0
        out_spec = pl.BlockSpec((None, tm, tn), lambda i, j, k: (j // (per_o // tn), i, j % (per_o // tn)))
        out_dims = (N_CHIPS, m, per_o)
    else:
        out_spec, out_dims = mn_spec, (m, n)
    dims = (((0 if ta else 1,), (1 if tb else 0,)), ((), ()))
    n_ex, n_out = len(extras), len(outs)

    def body(a_ref, b_ref, *rest):
        ex, o, acc = rest[:n_ex], rest[n_ex:n_ex + n_out], rest[n_ex + n_out:]
        part = lax.dot_general(a_ref[...].astype(BF16), b_ref[...].astype(BF16), dims, preferred_element_type=F32)

        def finish(val):
            res = epi(val, *[e[...] for e in ex]) if epi is not None else (val,)
            for r, oref in zip(res, o):
                oref[...] = r.astype(oref.dtype)

        if nk == 1:
            finish(part)
        else:
            k = pl.program_id(2)

            @pl.when(k == 0)
            def _():
                acc[0][...] = part

            @pl.when(k > 0)
            def _():
                acc[0][...] += part

            @pl.when(k == nk - 1)
            def _():
                finish(acc[0][...])

    res = pl.pallas_call(
        body, name=name, grid=(m // tm, n // tn, nk),
        in_specs=[a_spec, b_spec] + [mn_spec] * n_ex,
        out_specs=[out_spec] * n_out,
        out_shape=[jax.ShapeDtypeStruct(out_dims, dt) for dt in outs],
        scratch_shapes=[pltpu.VMEM((tm, tn), F32)] if nk > 1 else [],
        compiler_params=_params("parallel", "parallel", "arbitrary"),
    )(a, b, *extras)
    return res[0] if n_out == 1 else res


ROW_TILE = 512


def _norm_fwd(x, w, *, name):
    s, d = x.shape

    def body(x_ref, w_ref, o_ref):
        xv = x_ref[...]
        r = lax.rsqrt(jnp.mean(xv * xv, axis=-1, keepdims=True) + EPS)
        o_ref[...] = (xv * r * w_ref[...]).astype(o_ref.dtype)

    return pl.pallas_call(
        body, name=name, grid=(s // ROW_TILE,),
        in_specs=[pl.BlockSpec((ROW_TILE, d), lambda i: (i, 0)), pl.BlockSpec((1, d), lambda i: (0, 0))],
        out_specs=pl.BlockSpec((ROW_TILE, d), lambda i: (i, 0)),
        out_shape=jax.ShapeDtypeStruct((s, d), BF16), compiler_params=_params("parallel"),
    )(x, w.reshape(1, d))


def _resnorm_fwd(x, u, w, *, name):
    s, d = x.shape

    def body(x_ref, u_ref, w_ref, o_ref):
        uv = u_ref[...]
        r = lax.rsqrt(jnp.mean(uv * uv, axis=-1, keepdims=True) + EPS)
        o_ref[...] = x_ref[...] + uv * r * w_ref[...]

    row = pl.BlockSpec((ROW_TILE, d), lambda i: (i, 0))
    return pl.pallas_call(
        body, name=name, grid=(s // ROW_TILE,),
        in_specs=[row, row, pl.BlockSpec((1, d), lambda i: (0, 0))], out_specs=row,
        out_shape=jax.ShapeDtypeStruct((s, d), F32), compiler_params=_params("parallel"),
    )(x, u, w.reshape(1, d))


def _norm_bwd(xin, w, dy, res, *, out_dtype, name):
    s, d = xin.shape
    has_res = res is not None

    def body(*refs):
        x_ref, w_ref, dy_ref = refs[:3]
        res_ref = refs[3] if has_res else None
        dx_ref, dw_ref = refs[3 + has_res:]
        xv, dyv = x_ref[...], dy_ref[...].astype(F32)
        r = lax.rsqrt(jnp.mean(xv * xv, axis=-1, keepdims=True) + EPS)
        xh = xv * r
        g = dyv * w_ref[...]
        dx = r * (g - xh * jnp.mean(g * xh, axis=-1, keepdims=True))
        if has_res:
            dx = dx + res_ref[...]
        dx_ref[...] = dx.astype(dx_ref.dtype)

        @pl.when(pl.program_id(0) == 0)
        def _():
            dw_ref[...] = jnp.zeros_like(dw_ref)

        dw_ref[...] += jnp.sum(dyv * xh, axis=0, keepdims=True)

    row = pl.BlockSpec((ROW_TILE, d), lambda i: (i, 0))
    vec = pl.BlockSpec((1, d), lambda i: (0, 0))
    ins = [xin, w.reshape(1, d), dy] + ([res] if has_res else [])
    dx, dw = pl.pallas_call(
        body, name=name, grid=(s // ROW_TILE,),
        in_specs=[row, vec, row] + ([row] if has_res else []), out_specs=[row, vec],
        out_shape=[jax.ShapeDtypeStruct((s, d), out_dtype), jax.ShapeDtypeStruct((1, d), F32)],
        compiler_params=_params("arbitrary"),
    )(*ins)
    return dx, dw.reshape(d)


def _loss_fwd_bwd(y, target, *, name):
    s, d = y.shape

    def body(y_ref, t_ref, loss_ref, dy_ref):
        e = y_ref[...] - t_ref[...]
        dy_ref[...] = e * (1.0 / d)

        @pl.when(pl.program_id(0) == 0)
        def _():
            loss_ref[...] = jnp.zeros_like(loss_ref)

        part = jnp.sum(jnp.sum(e * e, axis=1, keepdims=True), axis=0, keepdims=True)
        loss_ref[...] += part * (0.5 / d)

    row = pl.BlockSpec((ROW_TILE, d), lambda i: (i, 0))
    loss, dy = pl.pallas_call(
        body, name=name, grid=(s // ROW_TILE,), in_specs=[row, row],
        out_specs=[pl.BlockSpec((1, 1), lambda i: (0, 0)), row],
        out_shape=[jax.ShapeDtypeStruct((1, 1), F32), jax.ShapeDtypeStruct((s, d), F32)],
        compiler_params=_params("arbitrary"),
    )(y, target)
    return loss[0, 0], dy


def _shift_down(x, k, rows):
    if k == 0:
        return x
    return jnp.where(rows >= k, pltpu.roll(x, k, 0), 0.0)


def _shift_up(x, k, rows):
    if k == 0:
        return x
    n = x.shape[0]
    return jnp.where(rows < n - k, pltpu.roll(x, n - k, 0), 0.0)


def _col_spec(s, base):
    return pl.BlockSpec((s, LANES), lambda j: (0, base + j))


def _gdn_pre_math(x, w, j, rows):
    taps = w.shape[0]
    c = w[taps - 1:taps, :] * x
    for i in range(taps - 1):
        c = c + w[i:i + 1, :] * _shift_down(x, taps - 1 - i, rows)
    sg = _sigmoid(c)
    y = c * sg
    r = lax.rsqrt(jnp.sum(y * y, axis=-1, keepdims=True) + EPS)
    is_qk = j < 2 * GDN_HEADS
    scale = jnp.where(j < GDN_HEADS, GDN_QSCALE, 1.0)
    return c, sg, y, r, is_qk, scale


def _gdn_pre_fwd(proj, conv_w, *, name):
    s = proj.shape[0]

    def body(x_ref, w_ref, o_ref):
        j = pl.program_id(0)
        rows = lax.broadcasted_iota(jnp.int32, (s, LANES), 0)
        _, _, y, r, is_qk, scale = _gdn_pre_math(x_ref[...], w_ref[...], j, rows)
        o_ref[...] = jnp.where(is_qk, y * (r * scale), y)

    return pl.pallas_call(
        body, name=name, grid=(12,),
        in_specs=[_col_spec(s, PB_QKV), pl.BlockSpec((4, LANES), lambda j: (0, j))],
        out_specs=_col_spec(s, 0), out_shape=jax.ShapeDtypeStruct((s, 3 * BRANCH_W), F32),
        compiler_params=_params("parallel"),
    )(proj, conv_w)


def _gdn_pre_bwd(proj, conv_w, dqkvn, *, name):
    s = proj.shape[0]

    def body(x_ref, w_ref, d_ref, dx_ref, dw_ref):
        j = pl.program_id(0)
        rows = lax.broadcasted_iota(jnp.int32, (s, LANES), 0)
        x, w, dout = x_ref[...], w_ref[...], d_ref[...]
        c, sg, y, r, is_qk, scale = _gdn_pre_math(x, w, j, rows)
        yh = y * r
        dy_n = (scale * r) * (dout - yh * jnp.sum(dout * yh, axis=-1, keepdims=True))
        dy = jnp.where(is_qk, dy_n, dout)
        dc = dy * (sg * (1.0 + c * (1.0 - sg)))
        taps = w.shape[0]
        dx = w[taps - 1:taps, :] * dc
        dws = []
        for i in range(taps - 1):
            k = taps - 1 - i
            dx = dx + w[i:i + 1, :] * _shift_up(dc, k, rows)
            dws.append(jnp.sum(dc * _shift_down(x, k, rows), axis=0, keepdims=True))
        dws.append(jnp.sum(dc * x, axis=0, keepdims=True))
        dx_ref[...] = dx.astype(dx_ref.dtype)
        for i in range(taps):
            dw_ref[i:i + 1, :] = dws[i]

    return pl.pallas_call(
        body, name=name, grid=(12,),
        in_specs=[_col_spec(s, PB_QKV), pl.BlockSpec((4, LANES), lambda j: (0, j)), _col_spec(s, 0)],
        out_specs=[_col_spec(s, 0), pl.BlockSpec((4, LANES), lambda j: (0, j))],
        out_shape=[jax.ShapeDtypeStruct((s, 3 * BRANCH_W), BF16), jax.ShapeDtypeStruct((4, 3 * BRANCH_W), F32)],
        compiler_params=_params("parallel"),
    )(proj, conv_w, dqkvn)


def _lane_pad(v):
    return jnp.pad(v.reshape(1, -1), ((0, 0), (0, LANES - v.shape[0])))


def _gdn_gates_fwd(proj, a_log, dt_bias, *, name):
    s = proj.shape[0]

    def body(ab_ref, al_ref, dt_ref, o_ref):
        ab = ab_ref[...]
        lane = lax.broadcasted_iota(jnp.int32, (1, LANES), 1)
        g = -jnp.exp(al_ref[...]) * _softplus(ab + dt_ref[...])
        o_ref[...] = jnp.where(lane < GDN_HEADS, g, _sigmoid(ab))

    vec = pl.BlockSpec((1, LANES), lambda j: (0, 0))
    return pl.pallas_call(
        body, name=name, grid=(1,), in_specs=[_col_spec(s, PB_AB), vec, vec], out_specs=_col_spec(s, 0),
        out_shape=jax.ShapeDtypeStruct((s, LANES), F32), compiler_params=_params("arbitrary"),
    )(proj, _lane_pad(a_log), _lane_pad(dt_bias))


def _gdn_gates_bwd(proj, a_log, dt_bias, dgb, *, name):
    s = proj.shape[0]

    def body(ab_ref, al_ref, dt_ref, d_ref, dab_ref, dal_ref, ddt_ref):
        ab, d = ab_ref[...], d_ref[...]
        lane = lax.broadcasted_iota(jnp.int32, (1, LANES), 1)
        ea = jnp.exp(al_ref[...])
        pre = ab + dt_ref[...]
        g = -ea * _softplus(pre)
        dpre = d * (-ea) * _sigmoid(pre)
        beta = _sigmoid(ab)
        is_g = lane < GDN_HEADS
        dab = jnp.where(is_g, dpre, jnp.where(lane < 2 * GDN_HEADS, d * beta * (1.0 - beta), 0.0))
        dab_ref[...] = dab.astype(dab_ref.dtype)
        dal_ref[...] = jnp.sum(jnp.where(is_g, d * g, 0.0), axis=0, keepdims=True)
        ddt_ref[...] = jnp.sum(jnp.where(is_g, dpre, 0.0), axis=0, keepdims=True)

    vec = pl.BlockSpec((1, LANES), lambda j: (0, 0))
    dab, dal, ddt = pl.pallas_call(
        body, name=name, grid=(1,), in_specs=[_col_spec(s, PB_AB), vec, vec, _col_spec(s, 0)],
        out_specs=[_col_spec(s, 0), vec, vec],
        out_shape=[jax.ShapeDtypeStruct((s, LANES), BF16), jax.ShapeDtypeStruct((1, LANES), F32),
                   jax.ShapeDtypeStruct((1, LANES), F32)],
        compiler_params=_params("arbitrary"),
    )(proj, _lane_pad(a_log), _lane_pad(dt_bias), dgb)
    return dab, dal[0, :GDN_HEADS], ddt[0, :GDN_HEADS]


def _interleave(gens):
    results, live = [None] * len(gens), list(range(len(gens)))
    while live:
        for idx in list(live):
            try:
                next(gens[idx])
            except StopIteration as done:
                results[idx] = done.value
                live.remove(idx)
    return results


def _chunk_common(q, k, v, gb, gbt, h):
    c = CHUNK
    row = lax.broadcasted_iota(jnp.int32, (c, c), 0)
    col = lax.broadcasted_iota(jnp.int32, (c, c), 1)
    tril, strict, eye = row >= col, row > col, row == col
    lane = lax.broadcasted_iota(jnp.int32, (c, LANES), 1)
    sub = lax.broadcasted_iota(jnp.int32, (2 * GDN_HEADS, c), 0)
    g_col = jnp.sum(jnp.where(lane == h, gb, 0.0), axis=1, keepdims=True)
    beta_col = jnp.sum(jnp.where(lane == GDN_HEADS + h, gb, 0.0), axis=1, keepdims=True)
    g_row = jnp.sum(jnp.where(sub == h, gbt, 0.0), axis=0, keepdims=True)
    gc_col = jnp.sum(jnp.where(tril, jnp.broadcast_to(g_row, (c, c)), 0.0), axis=1, keepdims=True)
    gc_row = jnp.sum(jnp.where(row <= col, jnp.broadcast_to(g_col, (c, c)), 0.0), axis=0, keepdims=True)
    g_tot = jnp.sum(g_row, axis=1, keepdims=True)
    dm = jnp.exp(jnp.where(tril, gc_col - gc_row, -1e30))
    e_col = jnp.exp(gc_col)
    kdec_col = jnp.exp(g_tot - gc_col)
    gamma = jnp.exp(g_tot)
    kb = k * beta_col
    vb = v * beta_col
    kbg = kb * e_col
    kk = _dot(kb, k, NT, HI)
    qk = _bdot(q, k, NT)
    yield
    a = jnp.where(strict, kk * dm, 0.0)
    aqk = jnp.where(tril, qk * dm, 0.0)
    bneg = -a
    t = jnp.where(eye, 1.0, 0.0) + bneg
    p = _dot(bneg, bneg, precision=HI)
    yield
    for lvl in range(5):
        t_next = t + _dot(t, p, precision=HI)
        if lvl < 4:
            p = _dot(p, p, precision=HI)
        t = t_next
        yield
    u = _dot(t, vb, precision=HI)
    w = _dot(t, kbg, precision=HI)
    yield
    return dict(tril=tril, strict=strict, eye=eye, row=row, col=col, beta_col=beta_col, dm=dm, e_col=e_col,
                kdec_col=kdec_col, gamma=gamma, kb=kb, vb=vb, kbg=kbg, a=a, t=t, u=u, w=w, aqk=aqk,
                qd=q * e_col, kd=k * kdec_col)


def _gdn_chunk_fwd(qkvn, gb, gbt, *, name):
    s = qkvn.shape[0]
    n_chunks = s // CHUNK

    def body(q_ref, k_ref, v_ref, gb_ref, gbt_ref, o_ref, st_ref, state):
        @pl.when(pl.program_id(0) == 0)
        def _():
            state[...] = jnp.zeros_like(state)

        gbv, gbtv = gb_ref[...], gbt_ref[0]

        def head(h):
            hs = slice(h * GDN_DIM, (h + 1) * GDN_DIM)
            q, k, v = q_ref[:, hs], k_ref[:, hs], v_ref[:, hs]
            m = yield from _chunk_common(q, k, v, gbv, gbtv, h)
            s0 = state[h]
            st_ref[0, h] = s0
            vnew = m["u"] - _bdot(m["w"], s0)
            o_inter = _bdot(m["qd"], s0)
            yield
            o_ref[:, hs] = o_inter + _bdot(m["aqk"], vnew)
            state[h] = m["gamma"] * s0 + _bdot(m["kd"], vnew, TN)

        _interleave([head(h) for h in range(GDN_HEADS)])

    blk = lambda j: pl.BlockSpec((CHUNK, BRANCH_W), lambda n: (n, j))
    return pl.pallas_call(
        body, name=name, grid=(n_chunks,),
        in_specs=[blk(0), blk(1), blk(2), pl.BlockSpec((CHUNK, LANES), lambda n: (n, 0)),
                  pl.BlockSpec((1, 2 * GDN_HEADS, CHUNK), lambda n: (n, 0, 0))],
        out_specs=[blk(0), pl.BlockSpec((1, GDN_HEADS, GDN_DIM, GDN_DIM), lambda n: (n, 0, 0, 0))],
        out_shape=[jax.ShapeDtypeStruct((s, BRANCH_W), F32),
                   jax.ShapeDtypeStruct((n_chunks, GDN_HEADS, GDN_DIM, GDN_DIM), F32)],
        scratch_shapes=[pltpu.VMEM((GDN_HEADS, GDN_DIM, GDN_DIM), F32)],
        compiler_params=_params("arbitrary"),
    )(qkvn, qkvn, qkvn, gb, gbt)


def _gdn_chunk_bwd(qkvn, gb, gbt, states, do, *, name):
    s = qkvn.shape[0]
    n_chunks = s // CHUNK
    c = CHUNK

    def body(q_ref, k_ref, v_ref, gb_ref, gbt_ref, st_ref, do_ref, dqkv_ref, dgb_ref, dstate):
        @pl.when(pl.program_id(0) == 0)
        def _():
            dstate[...] = jnp.zeros_like(dstate)

        gbv, gbtv = gb_ref[...], gbt_ref[0]
        lane = lax.broadcasted_iota(jnp.int32, (c, LANES), 1)
        def head(h):
            hs = slice(h * GDN_DIM, (h + 1) * GDN_DIM)
            q, k, v, dov = q_ref[:, hs], k_ref[:, hs], v_ref[:, hs], do_ref[:, hs]
            m = yield from _chunk_common(q, k, v, gbv, gbtv, h)
            tril, strict, eye, row, col = m["tril"], m["strict"], m["eye"], m["row"], m["col"]
            s0, ds1 = st_ref[0, h], dstate[h]
            vnew = m["u"] - _bdot(m["w"], s0)
            dvnew_a = _bdot(m["aqk"], dov, TN) + _bdot(m["kd"], ds1)
            dqd = _bdot(dov, s0, NT)
            ds_q = _bdot(m["qd"], dov, TN)
            dgamma = jnp.sum(jnp.sum(s0 * ds1, axis=1, keepdims=True), axis=0, keepdims=True)
            yield
            dvnew = dvnew_a
            daqk = jnp.where(tril, _bdot(dov, vnew, NT), 0.0)
            dkd = _bdot(vnew, ds1, NT)
            dw = -_bdot(dvnew, s0, NT)
            dstate[h] = m["gamma"] * ds1 + ds_q - _bdot(m["w"], dvnew, TN)
            dvb = _dot(m["t"], dvnew, TN, HI)
            yield
            dt = _dot(dvnew, m["vb"], NT, HI) + _dot(dw, m["kbg"], NT, HI)
            dkbg = _dot(m["t"], dw, TN, HI)
            dmq = daqk * m["dm"]
            dq = _bdot(dmq, k) + dqd * m["e_col"]
            dk_q = _bdot(dmq, q, TN)
            yield
            tdt = _dot(m["t"], dt, TN, HI)
            yield
            da = jnp.where(strict, -_dot(tdt, m["t"], NT, HI), 0.0)
            yield
            dmat = da * m["dm"]
            dkb = _dot(dmat, k, precision=HI) + dkbg * m["e_col"]
            dk = (_dot(dmat, m["kb"], TN, HI) + dk_q + dkd * m["kdec_col"] + m["beta_col"] * dkb)
            yield
            dbeta_col = jnp.sum(dkb * k, axis=1, keepdims=True) + jnp.sum(dvb * v, axis=1, keepdims=True)
            e = da * m["a"] + daqk * m["aqk"]
            rs_kd = jnp.sum(dkd * m["kd"], axis=1, keepdims=True)
            e_colsum = jnp.sum(e, axis=0, keepdims=True)
            e_colsum_c = jnp.sum(jnp.where(eye, jnp.broadcast_to(e_colsum, (c, c)), 0.0), axis=1, keepdims=True)
            dgc = (jnp.sum(e, axis=1, keepdims=True) - e_colsum_c + jnp.sum(dqd * m["qd"], axis=1, keepdims=True)
                   - rs_kd + jnp.sum(dkbg * m["kbg"], axis=1, keepdims=True))
            last = jnp.sum(rs_kd, axis=0, keepdims=True) + dgamma * m["gamma"]
            dgc = dgc + jnp.where(lax.broadcasted_iota(jnp.int32, (c, 1), 0) == c - 1, last, 0.0)
            dgc_row = jnp.sum(jnp.where(eye, jnp.broadcast_to(dgc, (c, c)), 0.0), axis=0, keepdims=True)
            dg_col = jnp.sum(jnp.where(col >= row, jnp.broadcast_to(dgc_row, (c, c)), 0.0), axis=1, keepdims=True)
            for part, val in enumerate((dq, dk, m["beta_col"] * dvb)):
                lo = part * BRANCH_W + h * GDN_DIM
                dqkv_ref[:, lo:lo + GDN_DIM] = val
            return jnp.where(lane == h, dg_col, 0.0) + jnp.where(lane == GDN_HEADS + h, dbeta_col, 0.0)

        parts = _interleave([head(h) for h in range(GDN_HEADS)])
        dgb_ref[...] = (parts[0] + parts[1]) + (parts[2] + parts[3])

    rev = lambda n: n_chunks - 1 - n
    blk = lambda j: pl.BlockSpec((CHUNK, BRANCH_W), lambda n: (rev(n), j))
    return pl.pallas_call(
        body, name=name, grid=(n_chunks,),
        in_specs=[blk(0), blk(1), blk(2), pl.BlockSpec((CHUNK, LANES), lambda n: (rev(n), 0)),
                  pl.BlockSpec((1, 2 * GDN_HEADS, CHUNK), lambda n: (rev(n), 0, 0)),
                  pl.BlockSpec((1, GDN_HEADS, GDN_DIM, GDN_DIM), lambda n: (rev(n), 0, 0, 0)), blk(0)],
        out_specs=[pl.BlockSpec((CHUNK, 3 * BRANCH_W), lambda n: (rev(n), 0)),
                   pl.BlockSpec((CHUNK, LANES), lambda n: (rev(n), 0))],
        out_shape=[jax.ShapeDtypeStruct((s, 3 * BRANCH_W), F32), jax.ShapeDtypeStruct((s, LANES), F32)],
        scratch_shapes=[pltpu.VMEM((GDN_HEADS, GDN_DIM, GDN_DIM), F32)],
        compiler_params=_params("arbitrary"),
    )(qkvn, qkvn, qkvn, gb, gbt, states, do)


def _gdn_post_fwd(o, proj, norm_w, *, name):
    s = o.shape[0]

    def body(o_ref, g_ref, w_ref, y_ref):
        ov, gv = o_ref[...], g_ref[...]
        r = lax.rsqrt(jnp.mean(ov * ov, axis=-1, keepdims=True) + EPS)
        y_ref[...] = (ov * r * w_ref[...] * (gv * _sigmoid(gv))).astype(y_ref.dtype)

    return pl.pallas_call(
        body, name=name, grid=(GDN_HEADS,),
        in_specs=[_col_spec(s, 0), _col_spec(s, PB_GATE), pl.BlockSpec((1, LANES), lambda j: (0, 0))],
        out_specs=_col_spec(s, 0), out_shape=jax.ShapeDtypeStruct((s, BRANCH_W), BF16),
        compiler_params=_params("parallel"),
    )(o, proj, norm_w.reshape(1, GDN_DIM))


def _gdn_post_bwd(o, proj, norm_w, dy, *, name):
    s = o.shape[0]

    def body(o_ref, g_ref, w_ref, dy_ref, do_ref, dg_ref, dw_ref):
        ov, gv, w, dyv = o_ref[...], g_ref[...], w_ref[...], dy_ref[...].astype(F32)
        r = lax.rsqrt(jnp.mean(ov * ov, axis=-1, keepdims=True) + EPS)
        oh = ov * r
        sg = _sigmoid(gv)
        silu = gv * sg
        dn = dyv * silu
        dg_ref[...] = (dyv * (oh * w) * (sg * (1.0 + gv * (1.0 - sg)))).astype(dg_ref.dtype)

        @pl.when(pl.program_id(0) == 0)
        def _():
            dw_ref[...] = jnp.zeros_like(dw_ref)

        dw_ref[...] += jnp.sum(dn * oh, axis=0, keepdims=True)
        g2 = dn * w
        do_ref[...] = r * (g2 - oh * jnp.mean(g2 * oh, axis=-1, keepdims=True))

    do, dg, dw = pl.pallas_call(
        body, name=name, grid=(GDN_HEADS,),
        in_specs=[_col_spec(s, 0), _col_spec(s, PB_GATE), pl.BlockSpec((1, LANES), lambda j: (0, 0)), _col_spec(s, 0)],
        out_specs=[_col_spec(s, 0), _col_spec(s, 0), pl.BlockSpec((1, LANES), lambda j: (0, 0))],
        out_shape=[jax.ShapeDtypeStruct((s, BRANCH_W), F32), jax.ShapeDtypeStruct((s, BRANCH_W), BF16),
                   jax.ShapeDtypeStruct((1, LANES), F32)],
        compiler_params=_params("arbitrary"),
    )(o, proj, norm_w.reshape(1, GDN_DIM), dy)
    return do, dg, dw.reshape(LANES)


def _split_terms(x):
    hi = x.astype(BF16)
    lo = (x - hi.astype(F32)).astype(BF16)
    return jnp.concatenate([hi, lo], axis=1)


def _sb_sum_matrix(pred):
    row = lax.broadcasted_iota(jnp.int32, (2 * SB_TILE, 2 * SB_TILE), 0) % SB_TILE
    col = lax.broadcasted_iota(jnp.int32, (2 * SB_TILE, 2 * SB_TILE), 1)
    return jnp.where((col >= SB_TILE) | pred(row, col), 1.0, 0.0).astype(BF16)


def _sb_head_masks():
    lane = lax.broadcasted_iota(jnp.int32, (1, LANES), 1)
    return [(lane < SB_DIM).astype(F32), (lane >= SB_DIM).astype(F32)]


def _sb_fwd(proj, *, name):
    s = proj.shape[0]
    t, tq = SB_TILE, SB_QTILE
    nq = s // tq

    def body(q_ref, k_ref, v_ref, o_ref, tot_ref):
        cmr = lax.broadcasted_iota(jnp.int32, (tq, t), 1) - lax.broadcasted_iota(jnp.int32, (tq, t), 0)
        uo = _sb_sum_matrix(lambda row, col: row > col)
        hm = _sb_head_masks()

        def qloop(i, carry0):
            qs = pl.multiple_of(i * tq, tq)
            qf = q_ref[pl.ds(qs, tq), :] * SB_SCALE2
            qh = [(qf * hm[h]).astype(BF16) for h in range(2)]
            diag = (i * tq) // (SB_GROUP * t)

            def group(g, st, masked):
                ks = pl.multiple_of(g * (SB_GROUP * t), SB_GROUP * t)
                kb = k_ref[pl.ds(ks, SB_GROUP * t), :].astype(BF16)
                vf = v_ref[pl.ds(ks, SB_GROUP * t), :]
                tiles = [(h, j) for h in range(2) for j in range(SB_GROUP)]
                z = [_dot(qh[h], kb, NT) for h in range(2)]
                keep = {j: cmr < i * tq - (g * SB_GROUP + j) * t for j in range(SB_GROUP)} if masked else None
                base, terms = {}, {}
                for h, j in tiles:
                    zj = z[h][:, j * t:(j + 1) * t]
                    sp = _softplus2(zj)
                    base[h, j] = zj - sp
                    terms[h, j] = _split_terms(jnp.where(keep[j], sp, 0.0) if masked else sp)
                sums = {hj: _dot(terms[hj], uo) for hj in tiles}
                acc, new = st[0], []
                for h in range(2):
                    run, att = st[1 + h], [None] * SB_GROUP
                    for j in reversed(range(SB_GROUP)):
                        a = jnp.exp2(base[h, j] - (sums[h, j][:, :t] + run))
                        att[j] = (jnp.where(keep[j], a, 0.0) if masked else a).astype(BF16)
                        run = run + sums[h, j][:, t:]
                    acc = acc + _dot(jnp.concatenate(att, axis=1), (vf * hm[h]).astype(BF16))
                    new.append(run)
                return (acc, *new)

            zero = jnp.zeros((tq, LANES), F32)
            st = group(diag, (zero, zero, zero), True)
            st = lax.fori_loop(0, diag, lambda jj, sv: group(diag - 1 - jj, sv, False), st)
            o_ref[pl.ds(qs, tq), :] = st[0]
            tot_ref[pl.ds(qs, tq), :] = st[1] * hm[0] + st[2] * hm[1]
            return carry0

        lax.fori_loop(0, nq, qloop, 0)

    out = jax.ShapeDtypeStruct((s, BRANCH_W), F32)
    return pl.pallas_call(
        body, name=name, grid=(SB_HEADS // 2,),
        in_specs=[_col_spec(s, PB_SB), _col_spec(s, PB_SB + 4), _col_spec(s, PB_SB + 8)],
        out_specs=[_col_spec(s, 0)] * 2, out_shape=[out] * 2,
        compiler_params=_params("parallel"),
    )(proj, proj, proj)


def _sb_bwd(proj, tot, do, *, name):
    s = proj.shape[0]
    t, tq = SB_TILE, SB_QTILE
    nq = s // tq

    def body(q_ref, k_ref, v_ref, tot_ref, do_ref, dq_ref, dk_ref, dv_ref, dk_acc, dv_acc):
        dk_acc[...] = jnp.zeros_like(dk_acc)
        dv_acc[...] = jnp.zeros_like(dv_acc)
        cmr = lax.broadcasted_iota(jnp.int32, (tq, t), 1) - lax.broadcasted_iota(jnp.int32, (tq, t), 0)
        u_le = _sb_sum_matrix(lambda row, col: row <= col)
        u_lt = _sb_sum_matrix(lambda row, col: row < col)
        hm = _sb_head_masks()

        def qloop(i, carry0):
            qs = pl.multiple_of(i * tq, tq)
            qraw = q_ref[pl.ds(qs, tq), :]
            dov = do_ref[pl.ds(qs, tq), :].astype(F32)
            totv = tot_ref[pl.ds(qs, tq), :]
            qh = [(qraw * (hm[h] * SB_SCALE2)).astype(BF16) for h in range(2)]
            q2 = jnp.concatenate([(qraw * hm[h]).astype(BF16) for h in range(2)], axis=0)
            doh = [(dov * hm[h]).astype(BF16) for h in range(2)]
            do2 = jnp.concatenate(doh, axis=0)
            tot = [jnp.max(totv * hm[h], axis=1, keepdims=True) for h in range(2)]
            diag = (i * tq) // (SB_GROUP * t)

            def group(g, st, masked):
                ks = pl.multiple_of(g * (SB_GROUP * t), SB_GROUP * t)
                kf = k_ref[pl.ds(ks, SB_GROUP * t), :]
                kb = kf.astype(BF16)
                vb = v_ref[pl.ds(ks, SB_GROUP * t), :].astype(BF16)
                tiles = [(h, j) for h in range(2) for j in range(SB_GROUP)]
                z = [_dot(qh[h], kb, NT) for h in range(2)]
                datt = [_dot(doh[h], vb, NT) for h in range(2)]
                keep = {j: cmr < i * tq - (g * SB_GROUP + j) * t for j in range(SB_GROUP)} if masked else None
                ls, lterms = {}, {}
                for h, j in tiles:
                    zj = z[h][:, j * t:(j + 1) * t]
                    sp = _softplus2(zj)
                    ls[h, j] = zj - sp
                    lterms[h, j] = _split_terms(jnp.where(keep[j], sp, 0.0) if masked else sp)
                lsum = {hj: _dot(lterms[hj], u_le) for hj in tiles}
                att, p, pterms, new_c = {}, {}, {}, []
                for h in range(2):
                    run = st[1 + h]
                    for j in range(SB_GROUP):
                        a = jnp.exp2(ls[h, j] - ((tot[h] - run) - lsum[h, j][:, :t]))
                        if masked:
                            a = jnp.where(keep[j], a, 0.0)
                        att[h, j] = a.astype(BF16)
                        p[h, j] = a * datt[h][:, j * t:(j + 1) * t]
                        pterms[h, j] = _split_terms(p[h, j])
                        run = run + lsum[h, j][:, t:]
                    new_c.append(run)
                psum = {hj: _dot(pterms[hj], u_lt) for hj in tiles}
                dzb, new_r = {}, []
                for h in range(2):
                    run = st[3 + h]
                    for j in range(SB_GROUP):
                        sig = jnp.exp2(ls[h, j])
                        dz = p[h, j] - sig * (p[h, j] + run + psum[h, j][:, :t])
                        if masked:
                            dz = jnp.where(keep[j], dz, 0.0)
                        dzb[h, j] = (dz * SB_SCALE).astype(BF16)
                        run = run + psum[h, j][:, t:]
                    new_r.append(run)
                k2 = jnp.concatenate([(kf * hm[h]).astype(BF16) for h in range(2)], axis=0)
                dq_acc = st[0] + _dot(jnp.concatenate([dzb[hj] for hj in tiles], axis=1), k2)
                for j in range(SB_GROUP):
                    rows = pl.ds(pl.multiple_of(ks + j * t, t), t)
                    dk_acc[rows, :] += _dot(jnp.concatenate([dzb[0, j], dzb[1, j]], axis=0), q2, TN)
                    dv_acc[rows, :] += _dot(jnp.concatenate([att[0, j], att[1, j]], axis=0), do2, TN)
                return (dq_acc, *new_c, *new_r)

            zero = jnp.zeros((tq, LANES), F32)
            st = lax.fori_loop(0, diag, lambda jj, sv: group(jj, sv, False), (zero,) * 5)
            st = group(diag, st, True)
            dq_ref[pl.ds(qs, tq), :] = st[0].astype(dq_ref.dtype)
            return carry0

        lax.fori_loop(0, nq, qloop, 0)
        dk_ref[...] = dk_acc[...].astype(dk_ref.dtype)
        dv_ref[...] = dv_acc[...].astype(dv_ref.dtype)

    out = jax.ShapeDtypeStruct((s, BRANCH_W), BF16)
    return pl.pallas_call(
        body, name=name, grid=(SB_HEADS // 2,),
        in_specs=[_col_spec(s, PB_SB), _col_spec(s, PB_SB + 4), _col_spec(s, PB_SB + 8), _col_spec(s, 0), _col_spec(s, 0)],
        out_specs=[_col_spec(s, 0)] * 3, out_shape=[out] * 3,
        scratch_shapes=[pltpu.VMEM((s, LANES), F32), pltpu.VMEM((s, LANES), F32)],
        compiler_params=_params("parallel"),
    )(proj, proj, proj, tot, do)


def _sc_fwd(proj, conv_w, *, name):
    s = proj.shape[0]

    def body(x_ref, b_ref, c_ref, w_ref, y_ref):
        rows = lax.broadcasted_iota(jnp.int32, (s, LANES), 0)
        w = w_ref[...]
        u = c_ref[...] * x_ref[...]
        cv = w[2:3, :] * u + w[1:2, :] * _shift_down(u, 1, rows) + w[0:1, :] * _shift_down(u, 2, rows)
        y_ref[...] = (b_ref[...] * cv).astype(y_ref.dtype)

    return pl.pallas_call(
        body, name=name, grid=(BRANCH_W // LANES,),
        in_specs=[_col_spec(s, PB_SCX), _col_spec(s, PB_SCB), _col_spec(s, PB_SCC), pl.BlockSpec((3, LANES), lambda j: (0, j))],
        out_specs=_col_spec(s, 0), out_shape=jax.ShapeDtypeStruct((s, BRANCH_W), BF16),
        compiler_params=_params("parallel"),
    )(proj, proj, proj, conv_w)


def _sc_bwd(proj, conv_w, dy, *, name):
    s = proj.shape[0]

    def body(x_ref, b_ref, c_ref, w_ref, dy_ref, dx_ref, db_ref, dc_ref, dw_ref):
        rows = lax.broadcasted_iota(jnp.int32, (s, LANES), 0)
        w, x, cg, dyv = w_ref[...], x_ref[...], c_ref[...], dy_ref[...].astype(F32)
        u = cg * x
        u1, u2 = _shift_down(u, 1, rows), _shift_down(u, 2, rows)
        cv = w[2:3, :] * u + w[1:2, :] * u1 + w[0:1, :] * u2
        db_ref[...] = (dyv * cv).astype(db_ref.dtype)
        dcv = dyv * b_ref[...]
        du = w[2:3, :] * dcv + w[1:2, :] * _shift_up(dcv, 1, rows) + w[0:1, :] * _shift_up(dcv, 2, rows)
        dx_ref[...] = (du * cg).astype(dx_ref.dtype)
        dc_ref[...] = (du * x).astype(dc_ref.dtype)
        dw_ref[0:1, :] = jnp.sum(dcv * u2, axis=0, keepdims=True)
        dw_ref[1:2, :] = jnp.sum(dcv * u1, axis=0, keepdims=True)
        dw_ref[2:3, :] = jnp.sum(dcv * u, axis=0, keepdims=True)

    out = jax.ShapeDtypeStruct((s, BRANCH_W), BF16)
    wspec = pl.BlockSpec((3, LANES), lambda j: (0, j))
    return pl.pallas_call(
        body, name=name, grid=(BRANCH_W // LANES,),
        in_specs=[_col_spec(s, PB_SCX), _col_spec(s, PB_SCB), _col_spec(s, PB_SCC), wspec, _col_spec(s, 0)],
        out_specs=[_col_spec(s, 0)] * 3 + [wspec],
        out_shape=[out] * 3 + [jax.ShapeDtypeStruct((3, BRANCH_W), F32)],
        compiler_params=_params("parallel"),
    )(proj, proj, proj, conv_w, dy)


MERGE_TM, MERGE_TN = 512, D_MODEL // N_CHIPS


def _merge_specs():
    tm, tn = MERGE_TM, MERGE_TN
    y_spec = pl.BlockSpec((tm, BRANCH_W), lambda i, j: (i, 0))
    w_spec = pl.BlockSpec((None, N_BRANCH, BRANCH_W, tn), lambda i, j: (j, 0, 0, 0))
    gate_specs = [pl.BlockSpec((tm, tn), functools.partial(
        lambda i, j, b: (i, (PB_GATES * LANES + b * D_MODEL) // tn + j), b=b)) for b in range(N_BRANCH)]
    mn = pl.BlockSpec((tm, tn), lambda i, j: (i, j))
    return y_spec, w_spec, gate_specs, mn


def _merge_fwd(ya, yb, yc, wb, proj, *, name):
    s = ya.shape[0]
    y_spec, w_spec, gate_specs, mn = _merge_specs()

    def body(ya_ref, yb_ref, yc_ref, w_ref, g0, g1, g2, o_ref):
        acc = None
        for b, (y_ref, g_ref) in enumerate(zip((ya_ref, yb_ref, yc_ref), (g0, g1, g2))):
            term = _sigmoid(g_ref[...]) * _bdot(y_ref[...], w_ref[b])
            acc = term if acc is None else acc + term
        o_ref[...] = acc.astype(o_ref.dtype)

    return pl.pallas_call(
        body, name=name, grid=(s // MERGE_TM, D_MODEL // MERGE_TN),
        in_specs=[y_spec] * 3 + [w_spec] + gate_specs, out_specs=mn,
        out_shape=jax.ShapeDtypeStruct((s, D_MODEL), BF16), compiler_params=_params("parallel", "parallel"),
    )(ya, yb, yc, wb, proj, proj, proj)


def _merge_bwd(ya, yb, yc, wb, proj, dm, *, name):
    s = ya.shape[0]
    y_spec, w_spec, gate_specs, mn = _merge_specs()

    def body(ya_ref, yb_ref, yc_ref, w_ref, g0, g1, g2, dm_ref, *outs):
        dmv = dm_ref[...].astype(F32)
        for b, (y_ref, g_ref) in enumerate(zip((ya_ref, yb_ref, yc_ref), (g0, g1, g2))):
            sg = _sigmoid(g_ref[...])
            z = _bdot(y_ref[...], w_ref[b])
            outs[b][...] = (dmv * sg).astype(BF16)
            outs[N_BRANCH + b][...] = (dmv * z * sg * (1.0 - sg)).astype(BF16)

    out = jax.ShapeDtypeStruct((s, D_MODEL), BF16)
    res = pl.pallas_call(
        body, name=name, grid=(s // MERGE_TM, D_MODEL // MERGE_TN),
        in_specs=[y_spec] * 3 + [w_spec] + gate_specs + [mn], out_specs=[mn] * (2 * N_BRANCH),
        out_shape=[out] * (2 * N_BRANCH), compiler_params=_params("parallel", "parallel"),
    )(ya, yb, yc, wb, proj, proj, proj, dm)
    return res[:N_BRANCH], res[N_BRANCH:]


def _chunk_rows(v, s):
    return v[:, :2 * GDN_HEADS].reshape(s // CHUNK, CHUNK, 2 * GDN_HEADS).transpose(0, 2, 1)


def _relu2_epi(acc):
    r = jnp.maximum(acc, 0.0)
    return acc, r * r


def _drelu2_epi(acc, a):
    return (acc * (2.0 * jnp.maximum(a.astype(F32), 0.0)),)


def _layer_fwd(x0, p):
    s = x0.shape[0]
    h1 = _norm_fwd(x0, p["norm_mix_pre"], name="norm_mix_pre")
    proj = _matmul(h1, p["w_in"], name="proj_in", tm=512, tn=1664)
    qkvn = _gdn_pre_fwd(proj, p["conv_qkv_w"], name="gdn_pre")
    gb = _gdn_gates_fwd(proj, p["gdn_a_log"], p["gdn_dt_bias"], name="gdn_gates")
    gbt = _chunk_rows(gb, s)
    o_gdn, states = _gdn_chunk_fwd(qkvn, gb, gbt, name="gdn_chunk")
    ya = _gdn_post_fwd(o_gdn, proj, p["gdn_norm_w"], name="gdn_post")
    o_sb, sb_tot = _sb_fwd(proj, name="sb_attn")
    yc = _sc_fwd(proj, p["conv_sc_w"], name="short_conv")
    merged = _merge_fwd(ya, o_sb, yc, p["w_branch"], proj, name="merge")
    u = _matmul(merged, p["w_out"], name="proj_out", tm=512, tn=1024)
    x1 = _resnorm_fwd(x0, u, p["norm_mix_post"], name="norm_mix_post")
    h2 = _norm_fwd(x1, p["norm_ffn_pre"], name="norm_ffn_pre")
    a, r = _matmul(h2, p["w_ff1"], name="ff1", tm=512, tn=1024, outs=(BF16, BF16), epi=_relu2_epi, b_chips=True)
    f = _matmul(r, p["w_ff2"], name="ff2", tm=512, tn=1024, tk=1024)
    x2 = _resnorm_fwd(x1, f, p["norm_ffn_post"], name="norm_ffn_post")
    saved = dict(x0=x0, h1=h1, proj=proj, qkvn=qkvn, gb=gb, gbt=gbt, o_gdn=o_gdn, states=states, ya=ya, o_sb=o_sb,
                 sb_tot=sb_tot, yc=yc, merged=merged, u=u, x1=x1, h2=h2, a=a, r=r, f=f)
    return x2, saved


def _layer_bwd(dx2, p, sv):
    g = {}
    df, g["norm_ffn_post"] = _norm_bwd(sv["f"], p["norm_ffn_post"], dx2, None, out_dtype=BF16, name="norm_ffn_post_bwd")
    g["w_ff2"] = _matmul(sv["r"], df, ta=True, name="ff2_dw", tm=1024, tn=1024, tk=512)
    da = _matmul(df, p["w_ff2"], tb=True, name="ff2_dx", tm=512, tn=1024, outs=(BF16,), epi=_drelu2_epi,
                 extras=(sv["a"],))
    g["w_ff1"] = _matmul(sv["h2"], da, ta=True, name="ff1_dw", tm=1024, tn=1024, tk=512, out_chips=True)
    dh2 = _matmul(da, p["w_ff1"], tb=True, name="ff1_dx", tm=512, tn=1024, tk=1024, b_chips=True)
    dx1, g["norm_ffn_pre"] = _norm_bwd(sv["x1"], p["norm_ffn_pre"], dh2, dx2, out_dtype=F32, name="norm_ffn_pre_bwd")
    du, g["norm_mix_post"] = _norm_bwd(sv["u"], p["norm_mix_post"], dx1, None, out_dtype=BF16, name="norm_mix_post_bwd")
    g["w_out"] = _matmul(sv["merged"], du, ta=True, name="out_dw", tm=1024, tn=1024, tk=512)
    dmerged = _matmul(du, p["w_out"], tb=True, name="out_dx", tm=512, tn=1024, outs=(BF16,))
    ys = (sv["ya"], sv["o_sb"], sv["yc"])
    dz, dgates = _merge_bwd(*ys, p["w_branch"], sv["proj"], dmerged, name="merge_bwd")
    g["w_branch"] = jnp.stack([_matmul(ys[b], dz[b], ta=True, name=f"branch_dw{b}", tm=512, tn=256, tk=1024, out_chips=True)
                               for b in range(N_BRANCH)], axis=1)
    dys = [_matmul(dz[b], p["w_branch"][:, b], tb=True, name=f"branch_dx{b}", tm=1024, tn=512, tk=256, b_chips=True)
           for b in range(N_BRANCH)]
    dscx, dscb, dscc, g["conv_sc_w"] = _sc_bwd(sv["proj"], p["conv_sc_w"], dys[2], name="short_conv_bwd")
    dsq, dsk, dsv = _sb_bwd(sv["proj"], sv["sb_tot"], dys[1], name="sb_attn_bwd")
    do_gdn, dgate, dnw = _gdn_post_bwd(sv["o_gdn"], sv["proj"], p["gdn_norm_w"], dys[0], name="gdn_post_bwd")
    g["gdn_norm_w"] = dnw
    dqkvn, dgb = _gdn_chunk_bwd(sv["qkvn"], sv["gb"], sv["gbt"], sv["states"], do_gdn, name="gdn_chunk_bwd")
    dqkv, g["conv_qkv_w"] = _gdn_pre_bwd(sv["proj"], p["conv_qkv_w"], dqkvn, name="gdn_pre_bwd")
    dab, g["gdn_a_log"], g["gdn_dt_bias"] = _gdn_gates_bwd(sv["proj"], p["gdn_a_log"], p["gdn_dt_bias"], dgb,
                                                           name="gdn_gates_bwd")
    dproj = jnp.concatenate([*dgates, dqkv, dgate, dab, dsq, dsk, dsv, dscx, dscb, dscc], axis=1)
    g["w_in"] = _matmul(sv["h1"], dproj, ta=True, name="in_dw", tm=1024, tn=1664, tk=512)
    dh1 = _matmul(dproj, p["w_in"], tb=True, name="in_dx", tm=512, tn=1024, tk=1664)
    dx0, g["norm_mix_pre"] = _norm_bwd(sv["x0"], p["norm_mix_pre"], dh1, dx1, out_dtype=F32, name="norm_mix_pre_bwd")
    return dx0, g


def _local_step(x, target, n_layers, weights_of, grads_done):
    saved, layers = [], []
    h = x
    for l in range(n_layers):
        p = weights_of(l, h)
        h, sv = _layer_fwd(h, p)
        saved.append(sv)
        layers.append(p)
    loss, dh = _loss_fwd_bwd(h, target, name="loss")
    for l in reversed(range(n_layers)):
        dh, g = _layer_bwd(dh, layers[l], saved[l])
        zero = grads_done(l, g)
        if l > 0:
            layers[l - 1] = dict(layers[l - 1], norm_ffn_post=layers[l - 1]["norm_ffn_post"] + zero)
    return loss, dh


ANY = pl.BlockSpec(memory_space=pl.ANY)


def _me_and_chips():
    x, y, c = lax.axis_index("x"), lax.axis_index("y"), lax.axis_index("c")
    chips = [(1 - x, y), (x, 1 - y), (1 - x, 1 - y)]
    return x, y, c, chips


def _gather_devices(small, *, name):
    def body(small_ref, small_out, ssend, srecv, local_sem):
        x, y, c, chips = _me_and_chips()
        dev = 4 * x + 2 * y + c
        lc = pltpu.make_async_copy(small_ref, small_out.at[dev], local_sem)
        lc.start()
        peers = [(x, y, 1 - c)] + [(px, py, pc) for (px, py) in chips for pc in (c, 1 - c)]
        sends = []
        for k, peer in enumerate(peers):
            cp = pltpu.make_async_remote_copy(src_ref=small_ref, dst_ref=small_out.at[dev], send_sem=ssend.at[k],
                                              recv_sem=srecv.at[k], device_id=peer, device_id_type=MESH)
            cp.start()
            sends.append(cp)
        for k, (px, py, pc) in enumerate(peers):
            pltpu.make_async_remote_copy(src_ref=small_ref, dst_ref=small_out.at[4 * px + 2 * py + pc], send_sem=ssend.at[k],
                                         recv_sem=srecv.at[k], device_id=(px, py, pc), device_id_type=MESH).wait_recv()
        for cp in sends:
            cp.wait_send()
        lc.wait()

    return pl.pallas_call(
        body, name=name, in_specs=[ANY], out_specs=ANY,
        out_shape=jax.ShapeDtypeStruct((N_DEV,) + small.shape, small.dtype),
        scratch_shapes=[pltpu.SemaphoreType.DMA((N_DEV - 1,)), pltpu.SemaphoreType.DMA((N_DEV - 1,)), pltpu.SemaphoreType.DMA],
    )(small)


HBM = pl.BlockSpec(memory_space=pltpu.HBM)
SEM = pl.BlockSpec(memory_space=pltpu.SEMAPHORE)
EFFECT = pltpu.SideEffectType.DATAFLOW_SIDE_EFFECTING


def _exchange_start(srcs, *, by_slot, name, after=None):
    n = len(srcs)
    n_in = 2 * n + (after is not None)
    land_shapes = [a.shape if by_slot else (N_CHIPS,) + a.shape for a in srcs]
    lands = [pltpu.with_memory_space_constraint(lax.empty(sh, a.dtype), pltpu.HBM) for sh, a in zip(land_shapes, srcs)]
    srcs = [pltpu.with_memory_space_constraint(a, pltpu.HBM) for a in srcs]

    def body(*refs):
        ins, land = refs[:n], refs[n:2 * n]
        send_sems, recv_sems, token = refs[n_in], refs[n_in + 1], refs[-1]
        x, y, c, chips = _me_and_chips()
        me = 2 * x + y
        for a in range(n):
            for k, (px, py) in enumerate(chips):
                pltpu.make_async_remote_copy(
                    src_ref=ins[a].at[2 * px + py] if by_slot else ins[a], dst_ref=land[a].at[me],
                    send_sem=send_sems.at[3 * a + k], recv_sem=recv_sems.at[3 * a + k], device_id=(px, py, c),
                    device_id_type=MESH).start()
        token[...] = jnp.zeros_like(token)

    res = pl.pallas_call(
        body, name=name, in_specs=[HBM] * (2 * n) + ([ANY] if after is not None else []),
        out_specs=[SEM, SEM] + [HBM] * (2 * n) + [pl.BlockSpec(memory_space=pltpu.VMEM)],
        out_shape=[pltpu.SemaphoreType.DMA((3 * n,)), pltpu.SemaphoreType.DMA((3 * n,))]
        + [pltpu.HBM(a.shape, a.dtype) for a in srcs] + [pltpu.HBM(sh, a.dtype) for sh, a in zip(land_shapes, srcs)]
        + [jax.ShapeDtypeStruct((8, LANES), F32)],
        input_output_aliases={i: 2 + i for i in range(2 * n)},
        compiler_params=pltpu.CompilerParams(has_side_effects=EFFECT),
    )(*srcs, *lands, *([after] if after is not None else []))
    return dict(send=res[0], recv=res[1], srcs=res[2:2 + n], lands=res[2 + n:2 + 2 * n], token=res[-1])


def _exchange_wait(ex, after, *, by_slot, name):
    n = len(ex["srcs"])

    def body(*refs):
        ins, land = refs[:n], refs[n:2 * n]
        send_sems, recv_sems = refs[2 * n], refs[2 * n + 1]
        x, y, c, chips = _me_and_chips()
        me = 2 * x + y
        for a in range(n):
            for k, (px, py) in enumerate(chips):
                cp = pltpu.make_async_remote_copy(
                    src_ref=ins[a].at[me] if by_slot else ins[a], dst_ref=land[a].at[2 * px + py],
                    send_sem=send_sems.at[3 * a + k], recv_sem=recv_sems.at[3 * a + k], device_id=(px, py, c),
                    device_id_type=MESH)
                cp.wait_send()
                cp.wait_recv()

    res = pl.pallas_call(
        body, name=name, in_specs=[HBM] * (2 * n) + [SEM, SEM, ANY], out_specs=[HBM] * (2 * n),
        out_shape=[pltpu.HBM(a.shape, a.dtype) for a in ex["srcs"]] + [pltpu.HBM(a.shape, a.dtype) for a in ex["lands"]],
        input_output_aliases={i: i for i in range(2 * n)},
        compiler_params=pltpu.CompilerParams(has_side_effects=EFFECT),
    )(*ex["srcs"], *ex["lands"], ex["send"], ex["recv"], after)
    return res[:n], res[n:]


def _chip_index():
    return 2 * lax.axis_index("x") + lax.axis_index("y")


def _me_operand():
    return jnp.reshape(_chip_index(), (1,)).astype(jnp.int32)


def _place_own(land, own, *, name):
    rows, cols = _as2d(own).shape
    tr = _row_tile(rows, cols)

    def body(me_ref, own_ref, land_ref, out_ref):
        out_ref[...] = own_ref[...]

    res = pl.pallas_call(
        body, name=name,
        grid_spec=pltpu.PrefetchScalarGridSpec(
            num_scalar_prefetch=1, grid=(rows // tr,),
            in_specs=[pl.BlockSpec((tr, cols), lambda i, me: (i, 0)), ANY],
            out_specs=pl.BlockSpec((None, tr, cols), lambda i, me: (me[0], i, 0))),
        out_shape=jax.ShapeDtypeStruct((N_CHIPS, rows, cols), land.dtype), input_output_aliases={2: 0},
        compiler_params=_params("arbitrary"),
    )(_me_operand(), _as2d(own), land.reshape(N_CHIPS, rows, cols))
    return res.reshape(land.shape)


def _sum_partials(lands, parts, *, name):
    n, rows, cols = lands.shape
    tr = _row_tile(rows, cols, 1024 * 1024)

    def body(me_ref, land_ref, own_ref, o_ref):
        me = me_ref[0]
        acc = None
        for i in range(n):
            term = jnp.where(me == i, own_ref[...], land_ref[i]).astype(F32)
            acc = term if acc is None else acc + term
        o_ref[...] = acc

    return pl.pallas_call(
        body, name=name,
        grid_spec=pltpu.PrefetchScalarGridSpec(
            num_scalar_prefetch=1, grid=(rows // tr,),
            in_specs=[pl.BlockSpec((n, tr, cols), lambda i, me: (0, i, 0)),
                      pl.BlockSpec((None, tr, cols), lambda i, me: (me[0], i, 0))],
            out_specs=pl.BlockSpec((tr, cols), lambda i, me: (i, 0))),
        out_shape=jax.ShapeDtypeStruct((rows, cols), F32), compiler_params=_params("arbitrary"),
    )(_me_operand(), lands, parts)


def _swap_sibling(arrs, *, name):
    n = len(arrs)

    def body(*refs):
        ins, outs = refs[:n], refs[n:2 * n]
        send_sems, recv_sems = refs[2 * n:]
        x, y, c = lax.axis_index("x"), lax.axis_index("y"), lax.axis_index("c")
        cps = [pltpu.make_async_remote_copy(src_ref=ins[a], dst_ref=outs[a], send_sem=send_sems.at[a],
                                            recv_sem=recv_sems.at[a], device_id=(x, y, 1 - c), device_id_type=MESH)
               for a in range(n)]
        for cp in cps:
            cp.start()
        for cp in cps:
            cp.wait()

    return pl.pallas_call(
        body, name=name, in_specs=[ANY] * n, out_specs=[ANY] * n,
        out_shape=[jax.ShapeDtypeStruct(a.shape, a.dtype) for a in arrs],
        scratch_shapes=[pltpu.SemaphoreType.DMA((n,)), pltpu.SemaphoreType.DMA((n,))],
    )(*arrs)


def _row_tile(rows, cols, budget=2 * 1024 * 1024):
    best = None
    for t in range(16, rows + 1, 16):
        if rows % t == 0 and t * cols * 4 <= budget:
            best = t
    return best if best is not None else rows


def _sum_slots(parts, *, name):
    n, rows, cols = parts.shape
    tr = _row_tile(rows, cols, 1024 * 1024)

    def body(p_ref, o_ref):
        acc = p_ref[0].astype(F32)
        for i in range(1, n):
            acc = acc + p_ref[i].astype(F32)
        o_ref[...] = acc

    return pl.pallas_call(
        body, name=name, grid=(rows // tr,), in_specs=[pl.BlockSpec((n, tr, cols), lambda i: (0, i, 0))],
        out_specs=pl.BlockSpec((tr, cols), lambda i: (i, 0)), out_shape=jax.ShapeDtypeStruct((rows, cols), F32),
        compiler_params=_params("parallel"),
    )(parts)


def _adamw(w, m, v, g_a, g_b, *, name):
    rows, cols = w.shape
    tr = _row_tile(rows, cols, 1024 * 1024)
    two = g_b is not None
    c1 = 1.0 / (1.0 - ADAM_B1 ** ADAM_STEP)
    c2 = 1.0 / (1.0 - ADAM_B2 ** ADAM_STEP)

    def body(*refs):
        w_ref, m_ref, v_ref, ga_ref = refs[:4]
        g_ref, d_ref, nm_ref, nv_ref = refs[4 + two:]
        g = ga_ref[...]
        if two:
            g = g + refs[4][...]
        nm = ADAM_B1 * m_ref[...] + (1.0 - ADAM_B1) * g
        nv = ADAM_B2 * v_ref[...] + (1.0 - ADAM_B2) * (g * g)
        g_ref[...] = g
        nm_ref[...] = nm
        nv_ref[...] = nv
        d_ref[...] = -ADAM_LR * ((nm * c1) / (jnp.sqrt(nv * c2) + ADAM_EPS) + ADAM_WD * w_ref[...])

    blk = pl.BlockSpec((tr, cols), lambda i: (i, 0))
    ins = [w, m, v, g_a] + ([g_b] if two else [])
    return pl.pallas_call(
        body, name=name, grid=(rows // tr,), in_specs=[blk] * len(ins), out_specs=[blk] * 4,
        out_shape=[jax.ShapeDtypeStruct((rows, cols), F32)] * 4, compiler_params=_params("parallel"),
    )(*ins)


def _cast_bf16(w, *, name):
    rows, cols = w.shape
    tr = _row_tile(rows, cols)

    def body(w_ref, o_ref):
        o_ref[...] = w_ref[...].astype(BF16)

    blk = pl.BlockSpec((tr, cols), lambda i: (i, 0))
    return pl.pallas_call(body, name=name, grid=(rows // tr,), in_specs=[blk], out_specs=blk,
                          out_shape=jax.ShapeDtypeStruct((rows, cols), BF16), compiler_params=_params("parallel"))(w)


BIG = ("w_in", "w_branch", "w_out", "w_ff1", "w_ff2")
SMALL = ("norm_mix_pre", "conv_qkv_w", "gdn_a_log", "gdn_dt_bias", "gdn_norm_w", "conv_sc_w", "norm_mix_post",
         "norm_ffn_pre", "norm_ffn_post")
ORDER = ("norm_mix_pre", "w_in", "conv_qkv_w", "gdn_a_log", "gdn_dt_bias", "gdn_norm_w", "conv_sc_w", "w_branch",
         "w_out", "norm_mix_post", "norm_ffn_pre", "w_ff1", "w_ff2", "norm_ffn_post")


def _full_weights(big, conv, rep, l):
    p = dict(
        w_in=_in_cols_from_chips(big["w_in"]),
        w_branch=big["w_branch"],
        w_out=big["w_out"].reshape(D_MODEL, D_MODEL),
        w_ff1=big["w_ff1"],
        w_ff2=big["w_ff2"].reshape(D_FF, D_MODEL),
        conv_qkv_w=conv["conv_qkv_w"][:, l].transpose(1, 0, 2).reshape(4, 3 * BRANCH_W),
        conv_sc_w=conv["conv_sc_w"][:, l].transpose(1, 0, 2).reshape(3, BRANCH_W),
    )
    for k in ("norm_mix_pre", "gdn_a_log", "gdn_dt_bias", "gdn_norm_w", "norm_mix_post", "norm_ffn_pre", "norm_ffn_post"):
        p[k] = rep[k][l]
    return p


def _partials_by_chip(g):
    parts = dict(
        w_in=_in_cols_to_chips(g["w_in"]),
        w_branch=g["w_branch"],
        w_out=g["w_out"].reshape(N_CHIPS, D_MODEL // N_CHIPS, D_MODEL),
        w_ff1=g["w_ff1"],
        w_ff2=g["w_ff2"].reshape(N_CHIPS, D_FF // N_CHIPS, D_MODEL),
    )
    return [parts[k].astype(BF16) for k in BIG]


def _pack_small(grads):
    pieces, layout = [], []
    for name in SMALL:
        v = jnp.stack([g[name] for g in grads]).astype(F32)
        layout.append((name, v.shape))
        pieces.append(v.reshape(-1))
    flat = jnp.concatenate(pieces)
    rows = -(-flat.shape[0] // LANES)
    rows = -(-rows // 8) * 8
    flat = jnp.pad(flat, (0, rows * LANES - flat.shape[0]))
    return flat.reshape(rows, LANES), layout


def _unpack_small(table, layout):
    flat, out, off = table.reshape(-1), {}, 0
    for name, shape in layout:
        size = 1
        for d in shape:
            size *= d
        out[name] = flat[off:off + size].reshape(shape)
        off += size
    return out


def _as2d(a):
    return a.reshape(-1, a.shape[-1]) if a.ndim > 1 else a.reshape(1, -1)


def kernel(x, norm_mix_pre, w_in, conv_qkv_w, gdn_a_log, gdn_dt_bias, gdn_norm_w, conv_sc_w, w_branch, w_out, norm_mix_post, norm_ffn_pre, w_ff1, w_ff2, norm_ffn_post, loss_target, m_norm_mix_pre, m_w_in, m_conv_qkv_w, m_gdn_a_log, m_gdn_dt_bias, m_gdn_norm_w, m_conv_sc_w, m_w_branch, m_w_out, m_norm_mix_post, m_norm_ffn_pre, m_w_ff1, m_w_ff2, m_norm_ffn_post, v_norm_mix_pre, v_w_in, v_conv_qkv_w, v_gdn_a_log, v_gdn_dt_bias, v_gdn_norm_w, v_conv_sc_w, v_w_branch, v_w_out, v_norm_mix_post, v_norm_ffn_pre, v_w_ff1, v_w_ff2, v_norm_ffn_post):
    w = dict(norm_mix_pre=norm_mix_pre, w_in=w_in, conv_qkv_w=conv_qkv_w, gdn_a_log=gdn_a_log, gdn_dt_bias=gdn_dt_bias,
             gdn_norm_w=gdn_norm_w, conv_sc_w=conv_sc_w, w_branch=w_branch, w_out=w_out, norm_mix_post=norm_mix_post,
             norm_ffn_pre=norm_ffn_pre, w_ff1=w_ff1, w_ff2=w_ff2, norm_ffn_post=norm_ffn_post)
    m = dict(norm_mix_pre=m_norm_mix_pre, w_in=m_w_in, conv_qkv_w=m_conv_qkv_w, gdn_a_log=m_gdn_a_log,
             gdn_dt_bias=m_gdn_dt_bias, gdn_norm_w=m_gdn_norm_w, conv_sc_w=m_conv_sc_w, w_branch=m_w_branch, w_out=m_w_out,
             norm_mix_post=m_norm_mix_post, norm_ffn_pre=m_norm_ffn_pre, w_ff1=m_w_ff1, w_ff2=m_w_ff2,
             norm_ffn_post=m_norm_ffn_post)
    v = dict(norm_mix_pre=v_norm_mix_pre, w_in=v_w_in, conv_qkv_w=v_conv_qkv_w, gdn_a_log=v_gdn_a_log,
             gdn_dt_bias=v_gdn_dt_bias, gdn_norm_w=v_gdn_norm_w, conv_sc_w=v_conv_sc_w, w_branch=v_w_branch, w_out=v_w_out,
             norm_mix_post=v_norm_mix_post, norm_ffn_pre=v_norm_ffn_pre, w_ff1=v_w_ff1, w_ff2=v_w_ff2,
             norm_ffn_post=v_norm_ffn_post)

    me = _chip_index()

    shards = {k: _cast_bf16(_as2d(w[k]), name=f"cast_{k}").reshape(w[k].shape) for k in BIG}
    conv_names = ("conv_qkv_w", "conv_sc_w")
    def gather_start(l, after):
        return _exchange_start([shards[k][l] for k in BIG] + ([w[k] for k in conv_names] if l == 0 else []),
                               by_slot=False, name=f"gather_start{l}", after=after)

    gathers = {0: gather_start(0, None)}
    conv = {}

    def weights_of(l, x_l):
        own, lands = _exchange_wait(gathers[l], gathers[0]["token"] if l == 0 else x_l, by_slot=False, name=f"gather_wait{l}")
        names = BIG + (conv_names if l == 0 else ())
        full = {k: _place_own(land, o, name=f"own_{k}") for k, land, o in zip(names, lands, own)}
        if l == 0:
            conv.update({k: full[k] for k in conv_names})
        p = _full_weights(full, conv, w, l)
        if l + 1 < DEPTH:
            gathers[l + 1] = gather_start(l + 1, full["w_out"])
            p["norm_mix_pre"] = p["norm_mix_pre"] + gathers[l + 1]["token"][0, 0]
        return p

    grads, scatters = [None] * DEPTH, [None] * DEPTH

    def grads_done(l, g):
        grads[l] = g
        scatters[l] = _exchange_start(_partials_by_chip(g), by_slot=True, name=f"scatter_start{l}")
        return scatters[l]["token"][0, 0]

    loss, dx = _local_step(x[0], loss_target[0], DEPTH, weights_of, grads_done)
    loss = lax.psum(loss, ("x", "y", "c"))

    sums = []
    for l in range(DEPTH):
        parts, lands = _exchange_wait(scatters[l], dx, by_slot=True, name=f"scatter_wait{l}")
        sums.append([_sum_partials(r.reshape(N_CHIPS, -1, r.shape[-1]), o.reshape(N_CHIPS, -1, o.shape[-1]), name=f"sum_{k}")
                     for k, r, o in zip(BIG, lands, parts)])
    mine = [jnp.concatenate([sums[l][i] for l in range(DEPTH)], axis=0) for i in range(len(BIG))]
    theirs = _swap_sibling(mine, name="swap_sibling")
    small, layout = _pack_small(grads)
    small_g = _unpack_small(_sum_slots(_gather_devices(small, name="gather_small"), name="sum_small"), layout)
    for k, width in (("conv_qkv_w", 3 * BRANCH_W // N_CHIPS), ("conv_sc_w", BRANCH_W // N_CHIPS)):
        small_g[k] = lax.dynamic_slice_in_dim(small_g[k], me * width, width, axis=2)

    out = {}
    for k, s_mine, s_theirs in zip(BIG, mine, theirs):
        res = _adamw(_as2d(w[k]), _as2d(m[k]), _as2d(v[k]), s_mine, s_theirs, name=f"adamw_{k}")
        out[k] = [r.reshape(w[k].shape) for r in res]
    for k in SMALL:
        res = _adamw(_as2d(w[k]), _as2d(m[k]), _as2d(v[k]), _as2d(small_g[k]), None, name=f"adamw_{k}")
        out[k] = [r.reshape(w[k].shape) for r in res]
    return (loss, dx[None], *[out[k][0] for k in ORDER], *[out[k][1] for k in ORDER], *[out[k][2] for k in ORDER],
            *[out[k][3] for k in ORDER])
```

```python
import functools

import jax
import jax.numpy as jnp
from jax import lax
from jax.experimental import pallas as pl
from jax.experimental.pallas import tpu as pltpu

F32 = jnp.float32
BF16 = jnp.bfloat16
MESH = pl.DeviceIdType.MESH

LANES = 128
D_MODEL = 1024
DEPTH = 4
CHUNK = 64
GDN_HEADS, GDN_DIM = 4, 128
SB_HEADS, SB_DIM = 8, 64
BRANCH_W = 512
N_BRANCH = 3
D_FF = 4 * D_MODEL
EPS = 1e-6
IN_W = 8200
AB_COL = 2048
AB_PAD = LANES - 8
IN_WP = IN_W + AB_PAD
N_CHIPS = 4
N_DEV = 8
GATES_COL = 5128
PB_GATES, PB_QKV, PB_GATE, PB_AB, PB_SB, PB_SCX, PB_SCB, PB_SCC = 0, 24, 36, 40, 41, 53, 57, 61
SB_TILE = 128
SB_GROUP = 4
SB_QTILE = 256
SB_SCALE = SB_DIM ** -0.5
SB_SCALE2 = SB_SCALE * 1.4426950408889634
GDN_QSCALE = GDN_DIM ** -0.5
VMEM_LIMIT = 56 * 1024 * 1024

ADAM_LR, ADAM_B1, ADAM_B2, ADAM_EPS, ADAM_WD, ADAM_STEP = 0.001, 0.9, 0.999, 1e-08, 0.01, 10

NT = (((1,), (1,)), ((), ()))
TN = (((0,), (0,)), ((), ()))
HI = lax.Precision.HIGH


def _pad_in_cols(w):
    return jnp.concatenate([w[:, GATES_COL:], w[:, :AB_COL + 8], jnp.zeros((w.shape[0], AB_PAD), w.dtype),
                            w[:, AB_COL + 8:GATES_COL]], axis=1)


def _unpad_in_cols(g):
    n_gates = IN_W - GATES_COL
    return jnp.concatenate([g[:, n_gates:n_gates + AB_COL + 8], g[:, n_gates + AB_COL + 8 + AB_PAD:], g[:, :n_gates]], axis=1)


IN_SHARD = IN_W // N_CHIPS
_IN_SEGMENTS = ((0, AB_COL + 8, IN_W - GATES_COL), (AB_COL + 8, GATES_COL, IN_W - GATES_COL + AB_PAD),
                (GATES_COL, IN_W, -GATES_COL))


def _in_cols_from_chips(slots):
    def cols(first, last):
        out = []
        for j in range(N_CHIPS):
            lo, hi = max(first, j * IN_SHARD), min(last, (j + 1) * IN_SHARD)
            if lo < hi:
                out.append(slots[j][:, lo - j * IN_SHARD:hi - j * IN_SHARD])
        return out

    head, tail, gates = (cols(first, last) for first, last, _ in _IN_SEGMENTS)
    return jnp.concatenate(gates + head + [jnp.zeros((slots.shape[1], AB_PAD), slots.dtype)] + tail, axis=1)


def _in_cols_to_chips(g):
    shards = []
    for j in range(N_CHIPS):
        pieces = []
        for first, last, shift in _IN_SEGMENTS:
            lo, hi = max(first, j * IN_SHARD), min(last, (j + 1) * IN_SHARD)
            if lo < hi:
                pieces.append(g[:, lo + shift:hi + shift])
        shards.append(jnp.concatenate(pieces, axis=1))
    return jnp.stack(shards)


def _params(*sem):
    return pltpu.CompilerParams(dimension_semantics=sem if sem else None, vmem_limit_bytes=VMEM_LIMIT)


def _sigmoid(x):
    return 1.0 / (1.0 + jnp.exp(-x))


def _softplus(x):
    return jnp.maximum(x, 0.0) + jnp.log(1.0 + jnp.exp(-jnp.abs(x)))


def _softplus2(x):
    return jnp.maximum(x, 0.0) + jnp.log2(1.0 + jnp.exp2(-jnp.abs(x)))


def _dot(a, b, dims=None, precision=None):
    if dims is None:
        return jnp.dot(a, b, preferred_element_type=F32, precision=precision)
    return lax.dot_general(a, b, dims, preferred_element_type=F32, precision=precision)


def _bdot(a, b, dims=None):
    return _dot(a.astype(BF16), b.astype(BF16), dims)


def _matmul(a, b, *, name, ta=False, tb=False, tm, tn, tk=None, outs=(F32,), epi=None, extras=(), b_chips=False,
            out_chips=False):
    if ta:
        kdim, m = a.shape
    else:
        m, kdim = a.shape
    if b_chips:
        per = b.shape[2]
        if tb:
            n, kb = b.shape[1], N_CHIPS * per
        else:
            kb, n = b.shape[1], N_CHIPS * per
    elif tb:
        n, kb = b.shape
    else:
        kb, n = b.shape
    assert kdim == kb, (a.shape, b.shape)
    tk = kdim if tk is None else tk
    assert m % tm == 0 and n % tn == 0 and kdim % tk == 0, (m, n, kdim, tm, tn, tk)
    nk = kdim // tk
    ni, nj = m // tm, n // tn
    a_bytes, b_bytes = a.size * a.dtype.itemsize, b.size * b.dtype.itemsize
    cols_inner = a_bytes * (nj if nk > 1 else 1) + b_bytes * (ni if nj * nk > 1 else 1)
    rows_inner = b_bytes * (ni if nk > 1 else 1) + a_bytes * (nj if ni * nk > 1 else 1)
    swap = rows_inner < cols_inner

    def ix(f):
        return (lambda g0, g1, k: f(g1, g0, k)) if swap else f

    a_spec = pl.BlockSpec((tk, tm), ix(lambda i, j, k: (k, i))) if ta else pl.BlockSpec((tm, tk), ix(lambda i, j, k: (i, k)))
    if b_chips and tb:
        assert per % tk == 0
        b_spec = pl.BlockSpec((None, tn, tk), ix(lambda i, j, k: (k // (per // tk), j, k % (per // tk))))
    elif b_chips:
        assert per % tn == 0
        b_spec = pl.BlockSpec((None, tk, tn), ix(lambda i, j, k: (j // (per // tn), k, j % (per // tn))))
    elif tb:
        b_spec = pl.BlockSpec((tn, tk), ix(lambda i, j, k: (j, k)))
    else:
        b_spec = pl.BlockSpec((tk, tn), ix(lambda i, j, k: (k, j)))
    mn_spec = pl.BlockSpec((tm, tn), ix(lambda i, j, k: (i, j)))
    if out_chips:
        per_o = n // N_CHIPS
        assert per_o % tn == 0
        out_spec = pl.BlockSpec((None, tm, tn), ix(lambda i, j, k: (j // (per_o // tn), i, j % (per_o // tn))))
        out_dims = (N_CHIPS, m, per_o)
    else:
        out_spec, out_dims = mn_spec, (m, n)
    dims = (((0 if ta else 1,), (1 if tb else 0,)), ((), ()))
    n_ex, n_out = len(extras), len(outs)

    def body(a_ref, b_ref, *rest):
        ex, o, acc = rest[:n_ex], rest[n_ex:n_ex + n_out], rest[n_ex + n_out:]
        part = lax.dot_general(a_ref[...].astype(BF16), b_ref[...].astype(BF16), dims, preferred_element_type=F32)

        def finish(val):
            res = epi(val, *[e[...] for e in ex]) if epi is not None else (val,)
            for r, oref in zip(res, o):
                oref[...] = r.astype(oref.dtype)

        if nk == 1:
            finish(part)
        else:
            k = pl.program_id(2)

            @pl.when(k == 0)
            def _():
                acc[0][...] = part

            @pl.when(k > 0)
            def _():
                acc[0][...] += part

            @pl.when(k == nk - 1)
            def _():
                finish(acc[0][...])

    res = pl.pallas_call(
        body, name=name, grid=(nj, ni, nk) if swap else (ni, nj, nk),
        in_specs=[a_spec, b_spec] + [mn_spec] * n_ex,
        out_specs=[out_spec] * n_out,
        out_shape=[jax.ShapeDtypeStruct(out_dims, dt) for dt in outs],
        scratch_shapes=[pltpu.VMEM((tm, tn), F32)] if nk > 1 else [],
        compiler_params=_params("parallel", "parallel", "arbitrary"),
    )(a, b, *extras)
    return res[0] if n_out == 1 else res


ROW_TILE = 512


def _norm_fwd(x, w, *, name):
    s, d = x.shape

    def body(x_ref, w_ref, o_ref):
        xv = x_ref[...]
        r = lax.rsqrt(jnp.mean(xv * xv, axis=-1, keepdims=True) + EPS)
        o_ref[...] = (xv * r * w_ref[...]).astype(o_ref.dtype)

    return pl.pallas_call(
        body, name=name, grid=(s // ROW_TILE,),
        in_specs=[pl.BlockSpec((ROW_TILE, d), lambda i: (i, 0)), pl.BlockSpec((1, d), lambda i: (0, 0))],
        out_specs=pl.BlockSpec((ROW_TILE, d), lambda i: (i, 0)),
        out_shape=jax.ShapeDtypeStruct((s, d), BF16), compiler_params=_params("parallel"),
    )(x, w.reshape(1, d))


def _resnorm_fwd(x, u, w, *, name):
    s, d = x.shape

    def body(x_ref, u_ref, w_ref, o_ref):
        uv = u_ref[...]
        r = lax.rsqrt(jnp.mean(uv * uv, axis=-1, keepdims=True) + EPS)
        o_ref[...] = x_ref[...] + uv * r * w_ref[...]

    row = pl.BlockSpec((ROW_TILE, d), lambda i: (i, 0))
    return pl.pallas_call(
        body, name=name, grid=(s // ROW_TILE,),
        in_specs=[row, row, pl.BlockSpec((1, d), lambda i: (0, 0))], out_specs=row,
        out_shape=jax.ShapeDtypeStruct((s, d), F32), compiler_params=_params("parallel"),
    )(x, u, w.reshape(1, d))


def _norm_bwd(xin, w, dy, res, *, out_dtype, name):
    s, d = xin.shape
    has_res = res is not None

    def body(*refs):
        x_ref, w_ref, dy_ref = refs[:3]
        res_ref = refs[3] if has_res else None
        dx_ref, dw_ref = refs[3 + has_res:]
        xv, dyv = x_ref[...], dy_ref[...].astype(F32)
        r = lax.rsqrt(jnp.mean(xv * xv, axis=-1, keepdims=True) + EPS)
        xh = xv * r
        g = dyv * w_ref[...]
        dx = r * (g - xh * jnp.mean(g * xh, axis=-1, keepdims=True))
        if has_res:
            dx = dx + res_ref[...]
        dx_ref[...] = dx.astype(dx_ref.dtype)

        @pl.when(pl.program_id(0) == 0)
        def _():
            dw_ref[...] = jnp.zeros_like(dw_ref)

        dw_ref[...] += jnp.sum(dyv * xh, axis=0, keepdims=True)

    row = pl.BlockSpec((ROW_TILE, d), lambda i: (i, 0))
    vec = pl.BlockSpec((1, d), lambda i: (0, 0))
    ins = [xin, w.reshape(1, d), dy] + ([res] if has_res else [])
    dx, dw = pl.pallas_call(
        body, name=name, grid=(s // ROW_TILE,),
        in_specs=[row, vec, row] + ([row] if has_res else []), out_specs=[row, vec],
        out_shape=[jax.ShapeDtypeStruct((s, d), out_dtype), jax.ShapeDtypeStruct((1, d), F32)],
        compiler_params=_params("arbitrary"),
    )(*ins)
    return dx, dw.reshape(d)


def _loss_fwd_bwd(y, target, *, name):
    s, d = y.shape

    def body(y_ref, t_ref, loss_ref, dy_ref):
        e = y_ref[...] - t_ref[...]
        dy_ref[...] = e * (1.0 / d)

        @pl.when(pl.program_id(0) == 0)
        def _():
            loss_ref[...] = jnp.zeros_like(loss_ref)

        part = jnp.sum(jnp.sum(e * e, axis=1, keepdims=True), axis=0, keepdims=True)
        loss_ref[...] += part * (0.5 / d)

    row = pl.BlockSpec((ROW_TILE, d), lambda i: (i, 0))
    loss, dy = pl.pallas_call(
        body, name=name, grid=(s // ROW_TILE,), in_specs=[row, row],
        out_specs=[pl.BlockSpec((1, 1), lambda i: (0, 0)), row],
        out_shape=[jax.ShapeDtypeStruct((1, 1), F32), jax.ShapeDtypeStruct((s, d), F32)],
        compiler_params=_params("arbitrary"),
    )(y, target)
    return loss[0, 0], dy


def _shift_down(x, k, rows):
    if k == 0:
        return x
    return jnp.where(rows >= k, pltpu.roll(x, k, 0), 0.0)


def _shift_up(x, k, rows):
    if k == 0:
        return x
    n = x.shape[0]
    return jnp.where(rows < n - k, pltpu.roll(x, n - k, 0), 0.0)


def _col_spec(s, base):
    return pl.BlockSpec((s, LANES), lambda j: (0, base + j))


def _gdn_pre_math(x, w, j, rows):
    taps = w.shape[0]
    c = w[taps - 1:taps, :] * x
    for i in range(taps - 1):
        c = c + w[i:i + 1, :] * _shift_down(x, taps - 1 - i, rows)
    sg = _sigmoid(c)
    y = c * sg
    r = lax.rsqrt(jnp.sum(y * y, axis=-1, keepdims=True) + EPS)
    is_qk = j < 2 * GDN_HEADS
    scale = jnp.where(j < GDN_HEADS, GDN_QSCALE, 1.0)
    return c, sg, y, r, is_qk, scale


def _gdn_pre_fwd(proj, conv_w, *, name):
    s = proj.shape[0]

    def body(x_ref, w_ref, o_ref):
        j = pl.program_id(0)
        rows = lax.broadcasted_iota(jnp.int32, (s, LANES), 0)
        _, _, y, r, is_qk, scale = _gdn_pre_math(x_ref[...], w_ref[...], j, rows)
        o_ref[...] = jnp.where(is_qk, y * (r * scale), y)

    return pl.pallas_call(
        body, name=name, grid=(12,),
        in_specs=[_col_spec(s, PB_QKV), pl.BlockSpec((4, LANES), lambda j: (0, j))],
        out_specs=_col_spec(s, 0), out_shape=jax.ShapeDtypeStruct((s, 3 * BRANCH_W), F32),
        compiler_params=_params("parallel"),
    )(proj, conv_w)


def _gdn_pre_bwd(proj, conv_w, dqkvn, *, name):
    s = proj.shape[0]

    def body(x_ref, w_ref, d_ref, dx_ref, dw_ref):
        j = pl.program_id(0)
        rows = lax.broadcasted_iota(jnp.int32, (s, LANES), 0)
        x, w, dout = x_ref[...], w_ref[...], d_ref[...]
        c, sg, y, r, is_qk, scale = _gdn_pre_math(x, w, j, rows)
        yh = y * r
        dy_n = (scale * r) * (dout - yh * jnp.sum(dout * yh, axis=-1, keepdims=True))
        dy = jnp.where(is_qk, dy_n, dout)
        dc = dy * (sg * (1.0 + c * (1.0 - sg)))
        taps = w.shape[0]
        dx = w[taps - 1:taps, :] * dc
        dws = []
        for i in range(taps - 1):
            k = taps - 1 - i
            dx = dx + w[i:i + 1, :] * _shift_up(dc, k, rows)
            dws.append(jnp.sum(dc * _shift_down(x, k, rows), axis=0, keepdims=True))
        dws.append(jnp.sum(dc * x, axis=0, keepdims=True))
        dx_ref[...] = dx.astype(dx_ref.dtype)
        for i in range(taps):
            dw_ref[i:i + 1, :] = dws[i]

    return pl.pallas_call(
        body, name=name, grid=(12,),
        in_specs=[_col_spec(s, PB_QKV), pl.BlockSpec((4, LANES), lambda j: (0, j)), _col_spec(s, 0)],
        out_specs=[_col_spec(s, 0), pl.BlockSpec((4, LANES), lambda j: (0, j))],
        out_shape=[jax.ShapeDtypeStruct((s, 3 * BRANCH_W), BF16), jax.ShapeDtypeStruct((4, 3 * BRANCH_W), F32)],
        compiler_params=_params("parallel"),
    )(proj, conv_w, dqkvn)


def _lane_pad(v):
    return jnp.pad(v.reshape(1, -1), ((0, 0), (0, LANES - v.shape[0])))


def _gdn_gates_fwd(proj, a_log, dt_bias, *, name):
    s = proj.shape[0]

    def body(ab_ref, al_ref, dt_ref, o_ref):
        ab = ab_ref[...]
        lane = lax.broadcasted_iota(jnp.int32, (1, LANES), 1)
        g = -jnp.exp(al_ref[...]) * _softplus(ab + dt_ref[...])
        o_ref[...] = jnp.where(lane < GDN_HEADS, g, _sigmoid(ab))

    vec = pl.BlockSpec((1, LANES), lambda j: (0, 0))
    return pl.pallas_call(
        body, name=name, grid=(1,), in_specs=[_col_spec(s, PB_AB), vec, vec], out_specs=_col_spec(s, 0),
        out_shape=jax.ShapeDtypeStruct((s, LANES), F32), compiler_params=_params("arbitrary"),
    )(proj, _lane_pad(a_log), _lane_pad(dt_bias))


def _gdn_gates_bwd(proj, a_log, dt_bias, dgb, *, name):
    s = proj.shape[0]

    def body(ab_ref, al_ref, dt_ref, d_ref, dab_ref, dal_ref, ddt_ref):
        ab, d = ab_ref[...], d_ref[...]
        lane = lax.broadcasted_iota(jnp.int32, (1, LANES), 1)
        ea = jnp.exp(al_ref[...])
        pre = ab + dt_ref[...]
        g = -ea * _softplus(pre)
        dpre = d * (-ea) * _sigmoid(pre)
        beta = _sigmoid(ab)
        is_g = lane < GDN_HEADS
        dab = jnp.where(is_g, dpre, jnp.where(lane < 2 * GDN_HEADS, d * beta * (1.0 - beta), 0.0))
        dab_ref[...] = dab.astype(dab_ref.dtype)
        dal_ref[...] = jnp.sum(jnp.where(is_g, d * g, 0.0), axis=0, keepdims=True)
        ddt_ref[...] = jnp.sum(jnp.where(is_g, dpre, 0.0), axis=0, keepdims=True)

    vec = pl.BlockSpec((1, LANES), lambda j: (0, 0))
    dab, dal, ddt = pl.pallas_call(
        body, name=name, grid=(1,), in_specs=[_col_spec(s, PB_AB), vec, vec, _col_spec(s, 0)],
        out_specs=[_col_spec(s, 0), vec, vec],
        out_shape=[jax.ShapeDtypeStruct((s, LANES), BF16), jax.ShapeDtypeStruct((1, LANES), F32),
                   jax.ShapeDtypeStruct((1, LANES), F32)],
        compiler_params=_params("arbitrary"),
    )(proj, _lane_pad(a_log), _lane_pad(dt_bias), dgb)
    return dab, dal[0, :GDN_HEADS], ddt[0, :GDN_HEADS]


def _interleave(gens):
    results, live = [None] * len(gens), list(range(len(gens)))
    while live:
        for idx in list(live):
            try:
                next(gens[idx])
            except StopIteration as done:
                results[idx] = done.value
                live.remove(idx)
    return results


def _chunk_common(q, k, v, gb, gbt, h):
    c = CHUNK
    row = lax.broadcasted_iota(jnp.int32, (c, c), 0)
    col = lax.broadcasted_iota(jnp.int32, (c, c), 1)
    tril, strict, eye = row >= col, row > col, row == col
    lane = lax.broadcasted_iota(jnp.int32, (c, LANES), 1)
    sub = lax.broadcasted_iota(jnp.int32, (2 * GDN_HEADS, c), 0)
    g_col = jnp.sum(jnp.where(lane == h, gb, 0.0), axis=1, keepdims=True)
    beta_col = jnp.sum(jnp.where(lane == GDN_HEADS + h, gb, 0.0), axis=1, keepdims=True)
    g_row = jnp.sum(jnp.where(sub == h, gbt, 0.0), axis=0, keepdims=True)
    gc_col = jnp.sum(jnp.where(tril, jnp.broadcast_to(g_row, (c, c)), 0.0), axis=1, keepdims=True)
    gc_row = jnp.sum(jnp.where(row <= col, jnp.broadcast_to(g_col, (c, c)), 0.0), axis=0, keepdims=True)
    g_tot = jnp.sum(g_row, axis=1, keepdims=True)
    dm = jnp.exp(jnp.where(tril, gc_col - gc_row, -1e30))
    e_col = jnp.exp(gc_col)
    kdec_col = jnp.exp(g_tot - gc_col)
    gamma = jnp.exp(g_tot)
    kb = k * beta_col
    vb = v * beta_col
    kbg = kb * e_col
    kk = _dot(kb, k, NT, HI)
    qk = _bdot(q, k, NT)
    yield
    a = jnp.where(strict, kk * dm, 0.0)
    aqk = jnp.where(tril, qk * dm, 0.0)
    bneg = -a
    t = jnp.where(eye, 1.0, 0.0) + bneg
    p = _dot(bneg, bneg, precision=HI)
    yield
    for lvl in range(5):
        t_next = t + _dot(t, p, precision=HI)
        if lvl < 4:
            p = _dot(p, p, precision=HI)
        t = t_next
        yield
    u = _dot(t, vb, precision=HI)
    w = _dot(t, kbg, precision=HI)
    yield
    return dict(tril=tril, strict=strict, eye=eye, row=row, col=col, beta_col=beta_col, dm=dm, e_col=e_col,
                kdec_col=kdec_col, gamma=gamma, kb=kb, vb=vb, kbg=kbg, a=a, t=t, u=u, w=w, aqk=aqk,
                qd=q * e_col, kd=k * kdec_col)


def _gdn_chunk_fwd(qkvn, gb, gbt, *, name):
    s = qkvn.shape[0]
    n_chunks = s // CHUNK

    def body(q_ref, k_ref, v_ref, gb_ref, gbt_ref, o_ref, st_ref, state):
        @pl.when(pl.program_id(0) == 0)
        def _():
            state[...] = jnp.zeros_like(state)

        gbv, gbtv = gb_ref[...], gbt_ref[0]

        def head(h):
            hs = slice(h * GDN_DIM, (h + 1) * GDN_DIM)
            q, k, v = q_ref[:, hs], k_ref[:, hs], v_ref[:, hs]
            m = yield from _chunk_common(q, k, v, gbv, gbtv, h)
            s0 = state[h]
            st_ref[0, h] = s0
            vnew = m["u"] - _bdot(m["w"], s0)
            o_inter = _bdot(m["qd"], s0)
            yield
            o_ref[:, hs] = o_inter + _bdot(m["aqk"], vnew)
            state[h] = m["gamma"] * s0 + _bdot(m["kd"], vnew, TN)

        _interleave([head(h) for h in range(GDN_HEADS)])

    blk = lambda j: pl.BlockSpec((CHUNK, BRANCH_W), lambda n: (n, j))
    return pl.pallas_call(
        body, name=name, grid=(n_chunks,),
        in_specs=[blk(0), blk(1), blk(2), pl.BlockSpec((CHUNK, LANES), lambda n: (n, 0)),
                  pl.BlockSpec((1, 2 * GDN_HEADS, CHUNK), lambda n: (n, 0, 0))],
        out_specs=[blk(0), pl.BlockSpec((1, GDN_HEADS, GDN_DIM, GDN_DIM), lambda n: (n, 0, 0, 0))],
        out_shape=[jax.ShapeDtypeStruct((s, BRANCH_W), F32),
                   jax.ShapeDtypeStruct((n_chunks, GDN_HEADS, GDN_DIM, GDN_DIM), F32)],
        scratch_shapes=[pltpu.VMEM((GDN_HEADS, GDN_DIM, GDN_DIM), F32)],
        compiler_params=_params("arbitrary"),
    )(qkvn, qkvn, qkvn, gb, gbt)


def _gdn_chunk_bwd(qkvn, gb, gbt, states, do, *, name):
    s = qkvn.shape[0]
    n_chunks = s // CHUNK
    c = CHUNK

    def body(q_ref, k_ref, v_ref, gb_ref, gbt_ref, st_ref, do_ref, dqkv_ref, dgb_ref, dstate):
        @pl.when(pl.program_id(0) == 0)
        def _():
            dstate[...] = jnp.zeros_like(dstate)

        gbv, gbtv = gb_ref[...], gbt_ref[0]
        lane = lax.broadcasted_iota(jnp.int32, (c, LANES), 1)
        def head(h):
            hs = slice(h * GDN_DIM, (h + 1) * GDN_DIM)
            q, k, v, dov = q_ref[:, hs], k_ref[:, hs], v_ref[:, hs], do_ref[:, hs]
            m = yield from _chunk_common(q, k, v, gbv, gbtv, h)
            tril, strict, eye, row, col = m["tril"], m["strict"], m["eye"], m["row"], m["col"]
            s0, ds1 = st_ref[0, h], dstate[h]
            vnew = m["u"] - _bdot(m["w"], s0)
            dvnew_a = _bdot(m["aqk"], dov, TN) + _bdot(m["kd"], ds1)
            dqd = _bdot(dov, s0, NT)
            ds_q = _bdot(m["qd"], dov, TN)
            dgamma = jnp.sum(jnp.sum(s0 * ds1, axis=1, keepdims=True), axis=0, keepdims=True)
            yield
            dvnew = dvnew_a
            daqk = jnp.where(tril, _bdot(dov, vnew, NT), 0.0)
            dkd = _bdot(vnew, ds1, NT)
            dw = -_bdot(dvnew, s0, NT)
            dstate[h] = m["gamma"] * ds1 + ds_q - _bdot(m["w"], dvnew, TN)
            dvb = _dot(m["t"], dvnew, TN, HI)
            yield
            dt = _dot(dvnew, m["vb"], NT, HI) + _dot(dw, m["kbg"], NT, HI)
            dkbg = _dot(m["t"], dw, TN, HI)
            dmq = daqk * m["dm"]
            dq = _bdot(dmq, k) + dqd * m["e_col"]
            dk_q = _bdot(dmq, q, TN)
            yield
            tdt = _dot(m["t"], dt, TN, HI)
            yield
            da = jnp.where(strict, -_dot(tdt, m["t"], NT, HI), 0.0)
            yield
            dmat = da * m["dm"]
            dkb = _dot(dmat, k, precision=HI) + dkbg * m["e_col"]
            dk = (_dot(dmat, m["kb"], TN, HI) + dk_q + dkd * m["kdec_col"] + m["beta_col"] * dkb)
            yield
            dbeta_col = jnp.sum(dkb * k, axis=1, keepdims=True) + jnp.sum(dvb * v, axis=1, keepdims=True)
            e = da * m["a"] + daqk * m["aqk"]
            rs_kd = jnp.sum(dkd * m["kd"], axis=1, keepdims=True)
            e_colsum = jnp.sum(e, axis=0, keepdims=True)
            e_colsum_c = jnp.sum(jnp.where(eye, jnp.broadcast_to(e_colsum, (c, c)), 0.0), axis=1, keepdims=True)
            dgc = (jnp.sum(e, axis=1, keepdims=True) - e_colsum_c + jnp.sum(dqd * m["qd"], axis=1, keepdims=True)
                   - rs_kd + jnp.sum(dkbg * m["kbg"], axis=1, keepdims=True))
            last = jnp.sum(rs_kd, axis=0, keepdims=True) + dgamma * m["gamma"]
            dgc = dgc + jnp.where(lax.broadcasted_iota(jnp.int32, (c, 1), 0) == c - 1, last, 0.0)
            dgc_row = jnp.sum(jnp.where(eye, jnp.broadcast_to(dgc, (c, c)), 0.0), axis=0, keepdims=True)
            dg_col = jnp.sum(jnp.where(col >= row, jnp.broadcast_to(dgc_row, (c, c)), 0.0), axis=1, keepdims=True)
            for part, val in enumerate((dq, dk, m["beta_col"] * dvb)):
                lo = part * BRANCH_W + h * GDN_DIM
                dqkv_ref[:, lo:lo + GDN_DIM] = val
            return jnp.where(lane == h, dg_col, 0.0) + jnp.where(lane == GDN_HEADS + h, dbeta_col, 0.0)

        parts = _interleave([head(h) for h in range(GDN_HEADS)])
        dgb_ref[...] = (parts[0] + parts[1]) + (parts[2] + parts[3])

    rev = lambda n: n_chunks - 1 - n
    blk = lambda j: pl.BlockSpec((CHUNK, BRANCH_W), lambda n: (rev(n), j))
    return pl.pallas_call(
        body, name=name, grid=(n_chunks,),
        in_specs=[blk(0), blk(1), blk(2), pl.BlockSpec((CHUNK, LANES), lambda n: (rev(n), 0)),
                  pl.BlockSpec((1, 2 * GDN_HEADS, CHUNK), lambda n: (rev(n), 0, 0)),
                  pl.BlockSpec((1, GDN_HEADS, GDN_DIM, GDN_DIM), lambda n: (rev(n), 0, 0, 0)), blk(0)],
        out_specs=[pl.BlockSpec((CHUNK, 3 * BRANCH_W), lambda n: (rev(n), 0)),
                   pl.BlockSpec((CHUNK, LANES), lambda n: (rev(n), 0))],
        out_shape=[jax.ShapeDtypeStruct((s, 3 * BRANCH_W), F32), jax.ShapeDtypeStruct((s, LANES), F32)],
        scratch_shapes=[pltpu.VMEM((GDN_HEADS, GDN_DIM, GDN_DIM), F32)],
        compiler_params=_params("arbitrary"),
    )(qkvn, qkvn, qkvn, gb, gbt, states, do)


def _gdn_post_fwd(o, proj, norm_w, *, name):
    s = o.shape[0]

    def body(o_ref, g_ref, w_ref, y_ref):
        ov, gv = o_ref[...], g_ref[...]
        r = lax.rsqrt(jnp.mean(ov * ov, axis=-1, keepdims=True) + EPS)
        y_ref[...] = (ov * r * w_ref[...] * (gv * _sigmoid(gv))).astype(y_ref.dtype)

    return pl.pallas_call(
        body, name=name, grid=(GDN_HEADS,),
        in_specs=[_col_spec(s, 0), _col_spec(s, PB_GATE), pl.BlockSpec((1, LANES), lambda j: (0, 0))],
        out_specs=_col_spec(s, 0), out_shape=jax.ShapeDtypeStruct((s, BRANCH_W), BF16),
        compiler_params=_params("parallel"),
    )(o, proj, norm_w.reshape(1, GDN_DIM))


def _gdn_post_bwd(o, proj, norm_w, dy, *, name):
    s = o.shape[0]

    def body(o_ref, g_ref, w_ref, dy_ref, do_ref, dg_ref, dw_ref):
        ov, gv, w, dyv = o_ref[...], g_ref[...], w_ref[...], dy_ref[...].astype(F32)
        r = lax.rsqrt(jnp.mean(ov * ov, axis=-1, keepdims=True) + EPS)
        oh = ov * r
        sg = _sigmoid(gv)
        silu = gv * sg
        dn = dyv * silu
        dg_ref[...] = (dyv * (oh * w) * (sg * (1.0 + gv * (1.0 - sg)))).astype(dg_ref.dtype)

        @pl.when(pl.program_id(0) == 0)
        def _():
            dw_ref[...] = jnp.zeros_like(dw_ref)

        dw_ref[...] += jnp.sum(dn * oh, axis=0, keepdims=True)
        g2 = dn * w
        do_ref[...] = r * (g2 - oh * jnp.mean(g2 * oh, axis=-1, keepdims=True))

    do, dg, dw = pl.pallas_call(
        body, name=name, grid=(GDN_HEADS,),
        in_specs=[_col_spec(s, 0), _col_spec(s, PB_GATE), pl.BlockSpec((1, LANES), lambda j: (0, 0)), _col_spec(s, 0)],
        out_specs=[_col_spec(s, 0), _col_spec(s, 0), pl.BlockSpec((1, LANES), lambda j: (0, 0))],
        out_shape=[jax.ShapeDtypeStruct((s, BRANCH_W), F32), jax.ShapeDtypeStruct((s, BRANCH_W), BF16),
                   jax.ShapeDtypeStruct((1, LANES), F32)],
        compiler_params=_params("arbitrary"),
    )(o, proj, norm_w.reshape(1, GDN_DIM), dy)
    return do, dg, dw.reshape(LANES)


def _split_terms(x):
    hi = x.astype(BF16)
    lo = (x - hi.astype(F32)).astype(BF16)
    return jnp.concatenate([hi, lo], axis=1)


def _sb_sum_matrix(pred):
    row = lax.broadcasted_iota(jnp.int32, (2 * SB_TILE, 2 * SB_TILE), 0) % SB_TILE
    col = lax.broadcasted_iota(jnp.int32, (2 * SB_TILE, 2 * SB_TILE), 1)
    return jnp.where((col >= SB_TILE) | pred(row, col), 1.0, 0.0).astype(BF16)


def _sb_head_masks():
    lane = lax.broadcasted_iota(jnp.int32, (1, LANES), 1)
    return [(lane < SB_DIM).astype(F32), (lane >= SB_DIM).astype(F32)]


def _sb_fwd(proj, *, name):
    s = proj.shape[0]
    t, tq = SB_TILE, SB_QTILE
    nq = s // tq

    def body(q_ref, k_ref, v_ref, o_ref, tot_ref):
        cmr = lax.broadcasted_iota(jnp.int32, (tq, t), 1) - lax.broadcasted_iota(jnp.int32, (tq, t), 0)
        uo = _sb_sum_matrix(lambda row, col: row > col)
        hm = _sb_head_masks()

        def qloop(i, carry0):
            qs = pl.multiple_of(i * tq, tq)
            qf = q_ref[pl.ds(qs, tq), :] * SB_SCALE2
            qh = [(qf * hm[h]).astype(BF16) for h in range(2)]
            diag = (i * tq) // (SB_GROUP * t)

            def group(g, st, masked):
                ks = pl.multiple_of(g * (SB_GROUP * t), SB_GROUP * t)
                kb = k_ref[pl.ds(ks, SB_GROUP * t), :].astype(BF16)
                vf = v_ref[pl.ds(ks, SB_GROUP * t), :]
                tiles = [(h, j) for h in range(2) for j in range(SB_GROUP)]
                z = [_dot(qh[h], kb, NT) for h in range(2)]
                keep = {j: cmr < i * tq - (g * SB_GROUP + j) * t for j in range(SB_GROUP)} if masked else None
                base, terms = {}, {}
                for h, j in tiles:
                    zj = z[h][:, j * t:(j + 1) * t]
                    sp = _softplus2(zj)
                    base[h, j] = zj - sp
                    terms[h, j] = _split_terms(jnp.where(keep[j], sp, 0.0) if masked else sp)
                sums = {hj: _dot(terms[hj], uo) for hj in tiles}
                acc, new = st[0], []
                for h in range(2):
                    run, att = st[1 + h], [None] * SB_GROUP
                    for j in reversed(range(SB_GROUP)):
                        a = jnp.exp2(base[h, j] - (sums[h, j][:, :t] + run))
                        att[j] = (jnp.where(keep[j], a, 0.0) if masked else a).astype(BF16)
                        run = run + sums[h, j][:, t:]
                    acc = acc + _dot(jnp.concatenate(att, axis=1), (vf * hm[h]).astype(BF16))
                    new.append(run)
                return (acc, *new)

            zero = jnp.zeros((tq, LANES), F32)
            st = group(diag, (zero, zero, zero), True)
            st = lax.fori_loop(0, diag, lambda jj, sv: group(diag - 1 - jj, sv, False), st)
            o_ref[pl.ds(qs, tq), :] = st[0]
            tot_ref[pl.ds(qs, tq), :] = st[1] * hm[0] + st[2] * hm[1]
            return carry0

        lax.fori_loop(0, nq, qloop, 0)

    out = jax.ShapeDtypeStruct((s, BRANCH_W), F32)
    return pl.pallas_call(
        body, name=name, grid=(SB_HEADS // 2,),
        in_specs=[_col_spec(s, PB_SB), _col_spec(s, PB_SB + 4), _col_spec(s, PB_SB + 8)],
        out_specs=[_col_spec(s, 0)] * 2, out_shape=[out] * 2,
        compiler_params=_params("parallel"),
    )(proj, proj, proj)


def _sb_bwd(proj, tot, do, *, name):
    s = proj.shape[0]
    t, tq = SB_TILE, SB_QTILE
    nq = s // tq

    def body(q_ref, k_ref, v_ref, tot_ref, do_ref, dq_ref, dk_ref, dv_ref, dk_acc, dv_acc):
        dk_acc[...] = jnp.zeros_like(dk_acc)
        dv_acc[...] = jnp.zeros_like(dv_acc)
        cmr = lax.broadcasted_iota(jnp.int32, (tq, t), 1) - lax.broadcasted_iota(jnp.int32, (tq, t), 0)
        u_le = _sb_sum_matrix(lambda row, col: row <= col)
        u_lt = _sb_sum_matrix(lambda row, col: row < col)
        hm = _sb_head_masks()

        def qloop(i, carry0):
            qs = pl.multiple_of(i * tq, tq)
            qraw = q_ref[pl.ds(qs, tq), :]
            dov = do_ref[pl.ds(qs, tq), :].astype(F32)
            totv = tot_ref[pl.ds(qs, tq), :]
            qh = [(qraw * (hm[h] * SB_SCALE2)).astype(BF16) for h in range(2)]
            q2 = jnp.concatenate([(qraw * hm[h]).astype(BF16) for h in range(2)], axis=0)
            doh = [(dov * hm[h]).astype(BF16) for h in range(2)]
            do2 = jnp.concatenate(doh, axis=0)
            tot = [jnp.max(totv * hm[h], axis=1, keepdims=True) for h in range(2)]
            diag = (i * tq) // (SB_GROUP * t)

            def group(g, st, masked):
                ks = pl.multiple_of(g * (SB_GROUP * t), SB_GROUP * t)
                kf = k_ref[pl.ds(ks, SB_GROUP * t), :]
                kb = kf.astype(BF16)
                vb = v_ref[pl.ds(ks, SB_GROUP * t), :].astype(BF16)
                tiles = [(h, j) for h in range(2) for j in range(SB_GROUP)]
                z = [_dot(qh[h], kb, NT) for h in range(2)]
                datt = [_dot(doh[h], vb, NT) for h in range(2)]
                keep = {j: cmr < i * tq - (g * SB_GROUP + j) * t for j in range(SB_GROUP)} if masked else None
                ls, lterms = {}, {}
                for h, j in tiles:
                    zj = z[h][:, j * t:(j + 1) * t]
                    sp = _softplus2(zj)
                    ls[h, j] = zj - sp
                    lterms[h, j] = _split_terms(jnp.where(keep[j], sp, 0.0) if masked else sp)
                lsum = {hj: _dot(lterms[hj], u_le) for hj in tiles}
                att, p, pterms, new_c = {}, {}, {}, []
                for h in range(2):
                    run = st[1 + h]
                    for j in range(SB_GROUP):
                        a = jnp.exp2(ls[h, j] - ((tot[h] - run) - lsum[h, j][:, :t]))
                        if masked:
                            a = jnp.where(keep[j], a, 0.0)
                        att[h, j] = a.astype(BF16)
                        p[h, j] = a * datt[h][:, j * t:(j + 1) * t]
                        pterms[h, j] = _split_terms(p[h, j])
                        run = run + lsum[h, j][:, t:]
                    new_c.append(run)
                psum = {hj: _dot(pterms[hj], u_lt) for hj in tiles}
                dzb, new_r = {}, []
                for h in range(2):
                    run = st[3 + h]
                    for j in range(SB_GROUP):
                        sig = jnp.exp2(ls[h, j])
                        dz = p[h, j] - sig * (p[h, j] + run + psum[h, j][:, :t])
                        if masked:
                            dz = jnp.where(keep[j], dz, 0.0)
                        dzb[h, j] = (dz * SB_SCALE).astype(BF16)
                        run = run + psum[h, j][:, t:]
                    new_r.append(run)
                k2 = jnp.concatenate([(kf * hm[h]).astype(BF16) for h in range(2)], axis=0)
                dq_acc = st[0] + _dot(jnp.concatenate([dzb[hj] for hj in tiles], axis=1), k2)
                for j in range(SB_GROUP):
                    rows = pl.ds(pl.multiple_of(ks + j * t, t), t)
                    dk_acc[rows, :] += _dot(jnp.concatenate([dzb[0, j], dzb[1, j]], axis=0), q2, TN)
                    dv_acc[rows, :] += _dot(jnp.concatenate([att[0, j], att[1, j]], axis=0), do2, TN)
                return (dq_acc, *new_c, *new_r)

            zero = jnp.zeros((tq, LANES), F32)
            st = lax.fori_loop(0, diag, lambda jj, sv: group(jj, sv, False), (zero,) * 5)
            st = group(diag, st, True)
            dq_ref[pl.ds(qs, tq), :] = st[0].astype(dq_ref.dtype)
            return carry0

        lax.fori_loop(0, nq, qloop, 0)
        dk_ref[...] = dk_acc[...].astype(dk_ref.dtype)
        dv_ref[...] = dv_acc[...].astype(dv_ref.dtype)

    out = jax.ShapeDtypeStruct((s, BRANCH_W), BF16)
    return pl.pallas_call(
        body, name=name, grid=(SB_HEADS // 2,),
        in_specs=[_col_spec(s, PB_SB), _col_spec(s, PB_SB + 4), _col_spec(s, PB_SB + 8), _col_spec(s, 0), _col_spec(s, 0)],
        out_specs=[_col_spec(s, 0)] * 3, out_shape=[out] * 3,
        scratch_shapes=[pltpu.VMEM((s, LANES), F32), pltpu.VMEM((s, LANES), F32)],
        compiler_params=_params("parallel"),
    )(proj, proj, proj, tot, do)


def _sc_fwd(proj, conv_w, *, name):
    s = proj.shape[0]

    def body(x_ref, b_ref, c_ref, w_ref, y_ref):
        rows = lax.broadcasted_iota(jnp.int32, (s, LANES), 0)
        w = w_ref[...]
        u = c_ref[...] * x_ref[...]
        cv = w[2:3, :] * u + w[1:2, :] * _shift_down(u, 1, rows) + w[0:1, :] * _shift_down(u, 2, rows)
        y_ref[...] = (b_ref[...] * cv).astype(y_ref.dtype)

    return pl.pallas_call(
        body, name=name, grid=(BRANCH_W // LANES,),
        in_specs=[_col_spec(s, PB_SCX), _col_spec(s, PB_SCB), _col_spec(s, PB_SCC), pl.BlockSpec((3, LANES), lambda j: (0, j))],
        out_specs=_col_spec(s, 0), out_shape=jax.ShapeDtypeStruct((s, BRANCH_W), BF16),
        compiler_params=_params("parallel"),
    )(proj, proj, proj, conv_w)


def _sc_bwd(proj, conv_w, dy, *, name):
    s = proj.shape[0]

    def body(x_ref, b_ref, c_ref, w_ref, dy_ref, dx_ref, db_ref, dc_ref, dw_ref):
        rows = lax.broadcasted_iota(jnp.int32, (s, LANES), 0)
        w, x, cg, dyv = w_ref[...], x_ref[...], c_ref[...], dy_ref[...].astype(F32)
        u = cg * x
        u1, u2 = _shift_down(u, 1, rows), _shift_down(u, 2, rows)
        cv = w[2:3, :] * u + w[1:2, :] * u1 + w[0:1, :] * u2
        db_ref[...] = (dyv * cv).astype(db_ref.dtype)
        dcv = dyv * b_ref[...]
        du = w[2:3, :] * dcv + w[1:2, :] * _shift_up(dcv, 1, rows) + w[0:1, :] * _shift_up(dcv, 2, rows)
        dx_ref[...] = (du * cg).astype(dx_ref.dtype)
        dc_ref[...] = (du * x).astype(dc_ref.dtype)
        dw_ref[0:1, :] = jnp.sum(dcv * u2, axis=0, keepdims=True)
        dw_ref[1:2, :] = jnp.sum(dcv * u1, axis=0, keepdims=True)
        dw_ref[2:3, :] = jnp.sum(dcv * u, axis=0, keepdims=True)

    out = jax.ShapeDtypeStruct((s, BRANCH_W), BF16)
    wspec = pl.BlockSpec((3, LANES), lambda j: (0, j))
    return pl.pallas_call(
        body, name=name, grid=(BRANCH_W // LANES,),
        in_specs=[_col_spec(s, PB_SCX), _col_spec(s, PB_SCB), _col_spec(s, PB_SCC), wspec, _col_spec(s, 0)],
        out_specs=[_col_spec(s, 0)] * 3 + [wspec],
        out_shape=[out] * 3 + [jax.ShapeDtypeStruct((3, BRANCH_W), F32)],
        compiler_params=_params("parallel"),
    )(proj, proj, proj, conv_w, dy)


MERGE_TM, MERGE_TN = 512, D_MODEL // N_CHIPS


def _merge_specs():
    tm, tn = MERGE_TM, MERGE_TN
    y_spec = pl.BlockSpec((tm, BRANCH_W), lambda i, j: (i, 0))
    w_spec = pl.BlockSpec((None, N_BRANCH, BRANCH_W, tn), lambda i, j: (j, 0, 0, 0))
    gate_specs = [pl.BlockSpec((tm, tn), functools.partial(
        lambda i, j, b: (i, (PB_GATES * LANES + b * D_MODEL) // tn + j), b=b)) for b in range(N_BRANCH)]
    mn = pl.BlockSpec((tm, tn), lambda i, j: (i, j))
    return y_spec, w_spec, gate_specs, mn


def _merge_fwd(ya, yb, yc, wb, proj, *, name):
    s = ya.shape[0]
    y_spec, w_spec, gate_specs, mn = _merge_specs()

    def body(ya_ref, yb_ref, yc_ref, w_ref, g0, g1, g2, o_ref):
        acc = None
        for b, (y_ref, g_ref) in enumerate(zip((ya_ref, yb_ref, yc_ref), (g0, g1, g2))):
            term = _sigmoid(g_ref[...]) * _bdot(y_ref[...], w_ref[b])
            acc = term if acc is None else acc + term
        o_ref[...] = acc.astype(o_ref.dtype)

    return pl.pallas_call(
        body, name=name, grid=(s // MERGE_TM, D_MODEL // MERGE_TN),
        in_specs=[y_spec] * 3 + [w_spec] + gate_specs, out_specs=mn,
        out_shape=jax.ShapeDtypeStruct((s, D_MODEL), BF16), compiler_params=_params("parallel", "parallel"),
    )(ya, yb, yc, wb, proj, proj, proj)


def _merge_bwd(ya, yb, yc, wb, proj, dm, *, name):
    s = ya.shape[0]
    y_spec, w_spec, gate_specs, mn = _merge_specs()

    def body(ya_ref, yb_ref, yc_ref, w_ref, g0, g1, g2, dm_ref, *outs):
        dmv = dm_ref[...].astype(F32)
        for b, (y_ref, g_ref) in enumerate(zip((ya_ref, yb_ref, yc_ref), (g0, g1, g2))):
            sg = _sigmoid(g_ref[...])
            z = _bdot(y_ref[...], w_ref[b])
            outs[b][...] = (dmv * sg).astype(BF16)
            outs[N_BRANCH + b][...] = (dmv * z * sg * (1.0 - sg)).astype(BF16)

    out = jax.ShapeDtypeStruct((s, D_MODEL), BF16)
    res = pl.pallas_call(
        body, name=name, grid=(s // MERGE_TM, D_MODEL // MERGE_TN),
        in_specs=[y_spec] * 3 + [w_spec] + gate_specs + [mn], out_specs=[mn] * (2 * N_BRANCH),
        out_shape=[out] * (2 * N_BRANCH), compiler_params=_params("parallel", "parallel"),
    )(ya, yb, yc, wb, proj, proj, proj, dm)
    return res[:N_BRANCH], res[N_BRANCH:]


def _chunk_rows(v, s):
    return v[:, :2 * GDN_HEADS].reshape(s // CHUNK, CHUNK, 2 * GDN_HEADS).transpose(0, 2, 1)


def _relu2_epi(acc):
    r = jnp.maximum(acc, 0.0)
    return acc, r * r


def _drelu2_epi(acc, a):
    return (acc * (2.0 * jnp.maximum(a.astype(F32), 0.0)),)


def _layer_fwd(x0, p, late=None):
    s = x0.shape[0]
    h1 = _norm_fwd(x0, p["norm_mix_pre"], name="norm_mix_pre")
    proj = _matmul(h1, p["w_in"], name="proj_in", tm=512, tn=1664)
    qkvn = _gdn_pre_fwd(proj, p["conv_qkv_w"], name="gdn_pre")
    gb = _gdn_gates_fwd(proj, p["gdn_a_log"], p["gdn_dt_bias"], name="gdn_gates")
    gbt = _chunk_rows(gb, s)
    o_gdn, states = _gdn_chunk_fwd(qkvn, gb, gbt, name="gdn_chunk")
    ya = _gdn_post_fwd(o_gdn, proj, p["gdn_norm_w"], name="gdn_post")
    o_sb, sb_tot = _sb_fwd(proj, name="sb_attn")
    yc = _sc_fwd(proj, p["conv_sc_w"], name="short_conv")
    if late is not None:
        p = dict(p, **late(yc))
    merged = _merge_fwd(ya, o_sb, yc, p["w_branch"], proj, name="merge")
    u = _matmul(merged, p["w_out"], name="proj_out", tm=512, tn=1024)
    x1 = _resnorm_fwd(x0, u, p["norm_mix_post"], name="norm_mix_post")
    h2 = _norm_fwd(x1, p["norm_ffn_pre"], name="norm_ffn_pre")
    a, r = _matmul(h2, p["w_ff1"], name="ff1", tm=512, tn=1024, outs=(BF16, BF16), epi=_relu2_epi, b_chips=True)
    f = _matmul(r, p["w_ff2"], name="ff2", tm=1024, tn=1024, tk=1024)
    x2 = _resnorm_fwd(x1, f, p["norm_ffn_post"], name="norm_ffn_post")
    saved = dict(x0=x0, h1=h1, proj=proj, qkvn=qkvn, gb=gb, gbt=gbt, o_gdn=o_gdn, states=states, ya=ya, o_sb=o_sb,
                 sb_tot=sb_tot, yc=yc, merged=merged, u=u, x1=x1, h2=h2, a=a, r=r, f=f)
    return x2, saved, p


def _layer_bwd(dx2, p, sv, early=None):
    g = {}
    df, g["norm_ffn_post"] = _norm_bwd(sv["f"], p["norm_ffn_post"], dx2, None, out_dtype=BF16, name="norm_ffn_post_bwd")
    g["w_ff2"] = _matmul(sv["r"], df, ta=True, name="ff2_dw", tm=1024, tn=1024, tk=512)
    da = _matmul(df, p["w_ff2"], tb=True, name="ff2_dx", tm=512, tn=1024, outs=(BF16,), epi=_drelu2_epi,
                 extras=(sv["a"],))
    g["w_ff1"] = _matmul(sv["h2"], da, ta=True, name="ff1_dw", tm=1024, tn=1024, tk=512, out_chips=True)
    dh2 = _matmul(da, p["w_ff1"], tb=True, name="ff1_dx", tm=1024, tn=1024, tk=1024, b_chips=True)
    dx1, g["norm_ffn_pre"] = _norm_bwd(sv["x1"], p["norm_ffn_pre"], dh2, dx2, out_dtype=F32, name="norm_ffn_pre_bwd")
    du, g["norm_mix_post"] = _norm_bwd(sv["u"], p["norm_mix_post"], dx1, None, out_dtype=BF16, name="norm_mix_post_bwd")
    g["w_out"] = _matmul(sv["merged"], du, ta=True, name="out_dw", tm=1024, tn=1024, tk=512)
    dmerged = _matmul(du, p["w_out"], tb=True, name="out_dx", tm=512, tn=1024, outs=(BF16,))
    ys = (sv["ya"], sv["o_sb"], sv["yc"])
    dz, dgates = _merge_bwd(*ys, p["w_branch"], sv["proj"], dmerged, name="merge_bwd")
    g["w_branch"] = jnp.stack([_matmul(ys[b], dz[b], ta=True, name=f"branch_dw{b}", tm=512, tn=256, tk=1024, out_chips=True)
                               for b in range(N_BRANCH)], axis=1)
    dys = [_matmul(dz[b], p["w_branch"][:, b], tb=True, name=f"branch_dx{b}", tm=1024, tn=512, tk=256, b_chips=True)
           for b in range(N_BRANCH)]
    conv_sc_w = p["conv_sc_w"]
    if early is not None:
        conv_sc_w = conv_sc_w + early({k: g[k] for k in ("w_branch", "w_out", "w_ff1", "w_ff2")})
    dscx, dscb, dscc, g["conv_sc_w"] = _sc_bwd(sv["proj"], conv_sc_w, dys[2], name="short_conv_bwd")
    dsq, dsk, dsv = _sb_bwd(sv["proj"], sv["sb_tot"], dys[1], name="sb_attn_bwd")
    do_gdn, dgate, dnw = _gdn_post_bwd(sv["o_gdn"], sv["proj"], p["gdn_norm_w"], dys[0], name="gdn_post_bwd")
    g["gdn_norm_w"] = dnw
    dqkvn, dgb = _gdn_chunk_bwd(sv["qkvn"], sv["gb"], sv["gbt"], sv["states"], do_gdn, name="gdn_chunk_bwd")
    dqkv, g["conv_qkv_w"] = _gdn_pre_bwd(sv["proj"], p["conv_qkv_w"], dqkvn, name="gdn_pre_bwd")
    dab, g["gdn_a_log"], g["gdn_dt_bias"] = _gdn_gates_bwd(sv["proj"], p["gdn_a_log"], p["gdn_dt_bias"], dgb,
                                                           name="gdn_gates_bwd")
    dproj = jnp.concatenate([*dgates, dqkv, dgate, dab, dsq, dsk, dsv, dscx, dscb, dscc], axis=1)
    g["w_in"] = _matmul(sv["h1"], dproj, ta=True, name="in_dw", tm=1024, tn=1664, tk=512)
    dh1 = _matmul(dproj, p["w_in"], tb=True, name="in_dx", tm=1024, tn=1024, tk=1664)
    dx0, g["norm_mix_pre"] = _norm_bwd(sv["x0"], p["norm_mix_pre"], dh1, dx1, out_dtype=F32, name="norm_mix_pre_bwd")
    return dx0, g


def _local_step(x, target, n_layers, weights_of, grads_done, grads_early=None):
    saved, layers = [], []
    h = x
    for l in range(n_layers):
        p, late = weights_of(l, h)
        h, sv, p = _layer_fwd(h, p, late)
        saved.append(sv)
        layers.append(p)
    loss, dh = _loss_fwd_bwd(h, target, name="loss")
    for l in reversed(range(n_layers)):
        dh, g = _layer_bwd(dh, layers[l], saved[l], grads_early if l == 0 else None)
        zero = grads_done(l, g)
        if l > 0:
            layers[l - 1] = dict(layers[l - 1], norm_ffn_post=layers[l - 1]["norm_ffn_post"] + zero)
    return loss, dh


ANY = pl.BlockSpec(memory_space=pl.ANY)


def _me_and_chips():
    x, y, c = lax.axis_index("x"), lax.axis_index("y"), lax.axis_index("c")
    chips = [(1 - x, y), (x, 1 - y), (1 - x, 1 - y)]
    return x, y, c, chips


def _gather_devices(small, *, name):
    def body(small_ref, small_out, ssend, srecv, local_sem):
        x, y, c, chips = _me_and_chips()
        dev = 4 * x + 2 * y + c
        lc = pltpu.make_async_copy(small_ref, small_out.at[dev], local_sem)
        lc.start()
        peers = [(x, y, 1 - c)] + [(px, py, pc) for (px, py) in chips for pc in (c, 1 - c)]
        sends = []
        for k, peer in enumerate(peers):
            cp = pltpu.make_async_remote_copy(src_ref=small_ref, dst_ref=small_out.at[dev], send_sem=ssend.at[k],
                                              recv_sem=srecv.at[k], device_id=peer, device_id_type=MESH)
            cp.start()
            sends.append(cp)
        for k, (px, py, pc) in enumerate(peers):
            pltpu.make_async_remote_copy(src_ref=small_ref, dst_ref=small_out.at[4 * px + 2 * py + pc], send_sem=ssend.at[k],
                                         recv_sem=srecv.at[k], device_id=(px, py, pc), device_id_type=MESH).wait_recv()
        for cp in sends:
            cp.wait_send()
        lc.wait()

    return pl.pallas_call(
        body, name=name, in_specs=[ANY], out_specs=ANY,
        out_shape=jax.ShapeDtypeStruct((N_DEV,) + small.shape, small.dtype),
        scratch_shapes=[pltpu.SemaphoreType.DMA((N_DEV - 1,)), pltpu.SemaphoreType.DMA((N_DEV - 1,)), pltpu.SemaphoreType.DMA],
    )(small)


HBM = pl.BlockSpec(memory_space=pltpu.HBM)
SEM = pl.BlockSpec(memory_space=pltpu.SEMAPHORE)
EFFECT = pltpu.SideEffectType.DATAFLOW_SIDE_EFFECTING


def _exchange_start(srcs, *, by_slot, name, after=None):
    n = len(srcs)
    n_in = 2 * n + (after is not None)
    land_shapes = [a.shape if by_slot else (N_CHIPS,) + a.shape for a in srcs]
    lands = [pltpu.with_memory_space_constraint(lax.empty(sh, a.dtype), pltpu.HBM) for sh, a in zip(land_shapes, srcs)]
    srcs = [pltpu.with_memory_space_constraint(a, pltpu.HBM) for a in srcs]

    def body(*refs):
        ins, land = refs[:n], refs[n:2 * n]
        send_sems, recv_sems, token = refs[n_in], refs[n_in + 1], refs[-1]
        x, y, c, chips = _me_and_chips()
        me = 2 * x + y
        for a in range(n):
            for k, (px, py) in enumerate(chips):
                pltpu.make_async_remote_copy(
                    src_ref=ins[a].at[2 * px + py] if by_slot else ins[a], dst_ref=land[a].at[me],
                    send_sem=send_sems.at[3 * a + k], recv_sem=recv_sems.at[3 * a + k], device_id=(px, py, c),
                    device_id_type=MESH).start()
        token[...] = jnp.zeros_like(token)

    res = pl.pallas_call(
        body, name=name, in_specs=[HBM] * (2 * n) + ([ANY] if after is not None else []),
        out_specs=[SEM, SEM] + [HBM] * (2 * n) + [pl.BlockSpec(memory_space=pltpu.VMEM)],
        out_shape=[pltpu.SemaphoreType.DMA((3 * n,)), pltpu.SemaphoreType.DMA((3 * n,))]
        + [pltpu.HBM(a.shape, a.dtype) for a in srcs] + [pltpu.HBM(sh, a.dtype) for sh, a in zip(land_shapes, srcs)]
        + [jax.ShapeDtypeStruct((8, LANES), F32)],
        input_output_aliases={i: 2 + i for i in range(2 * n)},
        compiler_params=pltpu.CompilerParams(has_side_effects=EFFECT),
    )(*srcs, *lands, *([after] if after is not None else []))
    return dict(send=res[0], recv=res[1], srcs=res[2:2 + n], lands=res[2 + n:2 + 2 * n], token=res[-1])


def _exchange_wait(ex, after, *, by_slot, name):
    n = len(ex["srcs"])

    def body(*refs):
        ins, land = refs[:n], refs[n:2 * n]
        send_sems, recv_sems = refs[2 * n], refs[2 * n + 1]
        x, y, c, chips = _me_and_chips()
        me = 2 * x + y
        for a in range(n):
            for k, (px, py) in enumerate(chips):
                cp = pltpu.make_async_remote_copy(
                    src_ref=ins[a].at[me] if by_slot else ins[a], dst_ref=land[a].at[2 * px + py],
                    send_sem=send_sems.at[3 * a + k], recv_sem=recv_sems.at[3 * a + k], device_id=(px, py, c),
                    device_id_type=MESH)
                cp.wait_send()
                cp.wait_recv()

    res = pl.pallas_call(
        body, name=name, in_specs=[HBM] * (2 * n) + [SEM, SEM, ANY], out_specs=[HBM] * (2 * n),
        out_shape=[pltpu.HBM(a.shape, a.dtype) for a in ex["srcs"]] + [pltpu.HBM(a.shape, a.dtype) for a in ex["lands"]],
        input_output_aliases={i: i for i in range(2 * n)},
        compiler_params=pltpu.CompilerParams(has_side_effects=EFFECT),
    )(*ex["srcs"], *ex["lands"], ex["send"], ex["recv"], after)
    return res[:n], res[n:]


def _chip_index():
    return 2 * lax.axis_index("x") + lax.axis_index("y")


def _me_operand():
    return jnp.reshape(_chip_index(), (1,)).astype(jnp.int32)


def _place_own(land, own, *, name):
    rows, cols = _as2d(own).shape
    tr = _row_tile(rows, cols)

    def body(me_ref, own_ref, land_ref, out_ref):
        out_ref[...] = own_ref[...]

    res = pl.pallas_call(
        body, name=name,
        grid_spec=pltpu.PrefetchScalarGridSpec(
            num_scalar_prefetch=1, grid=(rows // tr,),
            in_specs=[pl.BlockSpec((tr, cols), lambda i, me: (i, 0)), ANY],
            out_specs=pl.BlockSpec((None, tr, cols), lambda i, me: (me[0], i, 0))),
        out_shape=jax.ShapeDtypeStruct((N_CHIPS, rows, cols), land.dtype), input_output_aliases={2: 0},
        compiler_params=_params("arbitrary"),
    )(_me_operand(), _as2d(own), land.reshape(N_CHIPS, rows, cols))
    return res.reshape(land.shape)


def _sum_partials(lands, parts, *, name):
    n, rows, cols = lands.shape
    tr = _row_tile(rows, cols, 1024 * 1024)

    def body(me_ref, land_ref, own_ref, o_ref):
        me = me_ref[0]
        acc = None
        for i in range(n):
            term = jnp.where(me == i, own_ref[...], land_ref[i]).astype(F32)
            acc = term if acc is None else acc + term
        o_ref[...] = acc

    return pl.pallas_call(
        body, name=name,
        grid_spec=pltpu.PrefetchScalarGridSpec(
            num_scalar_prefetch=1, grid=(rows // tr,),
            in_specs=[pl.BlockSpec((n, tr, cols), lambda i, me: (0, i, 0)),
                      pl.BlockSpec((None, tr, cols), lambda i, me: (me[0], i, 0))],
            out_specs=pl.BlockSpec((tr, cols), lambda i, me: (i, 0))),
        out_shape=jax.ShapeDtypeStruct((rows, cols), F32), compiler_params=_params("arbitrary"),
    )(_me_operand(), lands, parts)


def _swap_sibling(arrs, *, name):
    n = len(arrs)

    def body(*refs):
        ins, outs = refs[:n], refs[n:2 * n]
        send_sems, recv_sems = refs[2 * n:]
        x, y, c = lax.axis_index("x"), lax.axis_index("y"), lax.axis_index("c")
        cps = [pltpu.make_async_remote_copy(src_ref=ins[a], dst_ref=outs[a], send_sem=send_sems.at[a],
                                            recv_sem=recv_sems.at[a], device_id=(x, y, 1 - c), device_id_type=MESH)
               for a in range(n)]
        for cp in cps:
            cp.start()
        for cp in cps:
            cp.wait()

    return pl.pallas_call(
        body, name=name, in_specs=[ANY] * n, out_specs=[ANY] * n,
        out_shape=[jax.ShapeDtypeStruct(a.shape, a.dtype) for a in arrs],
        scratch_shapes=[pltpu.SemaphoreType.DMA((n,)), pltpu.SemaphoreType.DMA((n,))],
    )(*arrs)


def _row_tile(rows, cols, budget=2 * 1024 * 1024):
    best = None
    for t in range(16, rows + 1, 16):
        if rows % t == 0 and t * cols * 4 <= budget:
            best = t
    return best if best is not None else rows


def _sum_slots(parts, *, name):
    n, rows, cols = parts.shape
    tr = _row_tile(rows, cols, 1024 * 1024)

    def body(p_ref, o_ref):
        acc = p_ref[0].astype(F32)
        for i in range(1, n):
            acc = acc + p_ref[i].astype(F32)
        o_ref[...] = acc

    return pl.pallas_call(
        body, name=name, grid=(rows // tr,), in_specs=[pl.BlockSpec((n, tr, cols), lambda i: (0, i, 0))],
        out_specs=pl.BlockSpec((tr, cols), lambda i: (i, 0)), out_shape=jax.ShapeDtypeStruct((rows, cols), F32),
        compiler_params=_params("parallel"),
    )(parts)


def _adamw(w, m, v, g_a, g_b, *, name):
    rows, cols = w.shape
    tr = _row_tile(rows, cols, 1024 * 1024)
    two = g_b is not None
    c1 = 1.0 / (1.0 - ADAM_B1 ** ADAM_STEP)
    c2 = 1.0 / (1.0 - ADAM_B2 ** ADAM_STEP)

    def body(*refs):
        w_ref, m_ref, v_ref, ga_ref = refs[:4]
        g_ref, d_ref, nm_ref, nv_ref = refs[4 + two:]
        g = ga_ref[...]
        if two:
            g = g + refs[4][...]
        nm = ADAM_B1 * m_ref[...] + (1.0 - ADAM_B1) * g
        nv = ADAM_B2 * v_ref[...] + (1.0 - ADAM_B2) * (g * g)
        g_ref[...] = g
        nm_ref[...] = nm
        nv_ref[...] = nv
        d_ref[...] = -ADAM_LR * ((nm * c1) / (jnp.sqrt(nv * c2) + ADAM_EPS) + ADAM_WD * w_ref[...])

    blk = pl.BlockSpec((tr, cols), lambda i: (i, 0))
    ins = [w, m, v, g_a] + ([g_b] if two else [])
    return pl.pallas_call(
        body, name=name, grid=(rows // tr,), in_specs=[blk] * len(ins), out_specs=[blk] * 4,
        out_shape=[jax.ShapeDtypeStruct((rows, cols), F32)] * 4, compiler_params=_params("parallel"),
    )(*ins)


def _cast_bf16(w, *, name):
    rows, cols = w.shape
    tr = _row_tile(rows, cols)

    def body(w_ref, o_ref):
        o_ref[...] = w_ref[...].astype(BF16)

    blk = pl.BlockSpec((tr, cols), lambda i: (i, 0))
    return pl.pallas_call(body, name=name, grid=(rows // tr,), in_specs=[blk], out_specs=blk,
                          out_shape=jax.ShapeDtypeStruct((rows, cols), BF16), compiler_params=_params("parallel"))(w)


BIG = ("w_in", "w_branch", "w_out", "w_ff1", "w_ff2")
SMALL = ("norm_mix_pre", "conv_qkv_w", "gdn_a_log", "gdn_dt_bias", "gdn_norm_w", "conv_sc_w", "norm_mix_post",
         "norm_ffn_pre", "norm_ffn_post")
ORDER = ("norm_mix_pre", "w_in", "conv_qkv_w", "gdn_a_log", "gdn_dt_bias", "gdn_norm_w", "conv_sc_w", "w_branch",
         "w_out", "norm_mix_post", "norm_ffn_pre", "w_ff1", "w_ff2", "norm_ffn_post")


_MATMUL_LAYOUT = dict(
    w_in=_in_cols_from_chips,
    w_branch=lambda a: a,
    w_out=lambda a: a.reshape(D_MODEL, D_MODEL),
    w_ff1=lambda a: a,
    w_ff2=lambda a: a.reshape(D_FF, D_MODEL),
)
_SHARD_LAYOUT = dict(
    w_in=_in_cols_to_chips,
    w_branch=lambda g: g,
    w_out=lambda g: g.reshape(N_CHIPS, D_MODEL // N_CHIPS, D_MODEL),
    w_ff1=lambda g: g,
    w_ff2=lambda g: g.reshape(N_CHIPS, D_FF // N_CHIPS, D_MODEL),
)


def _full_weights(big, conv, rep, l):
    p = {k: _MATMUL_LAYOUT[k](a) for k, a in big.items()}
    if conv is not None:
        p["conv_qkv_w"] = conv["conv_qkv_w"][:, l].transpose(1, 0, 2).reshape(4, 3 * BRANCH_W)
        p["conv_sc_w"] = conv["conv_sc_w"][:, l].transpose(1, 0, 2).reshape(3, BRANCH_W)
    if rep is not None:
        for k in ("norm_mix_pre", "gdn_a_log", "gdn_dt_bias", "gdn_norm_w", "norm_mix_post", "norm_ffn_pre", "norm_ffn_post"):
            p[k] = rep[k][l]
    return p


def _partials_by_chip(g, names):
    return [_SHARD_LAYOUT[k](g[k]).astype(BF16) for k in names]


def _pack_small(grads):
    pieces, layout = [], []
    for name in SMALL:
        v = jnp.stack([g[name] for g in grads]).astype(F32)
        layout.append((name, v.shape))
        pieces.append(v.reshape(-1))
    flat = jnp.concatenate(pieces)
    rows = -(-flat.shape[0] // LANES)
    rows = -(-rows // 8) * 8
    flat = jnp.pad(flat, (0, rows * LANES - flat.shape[0]))
    return flat.reshape(rows, LANES), layout


def _unpack_small(table, layout):
    flat, out, off = table.reshape(-1), {}, 0
    for name, shape in layout:
        size = 1
        for d in shape:
            size *= d
        out[name] = flat[off:off + size].reshape(shape)
        off += size
    return out


def _as2d(a):
    return a.reshape(-1, a.shape[-1]) if a.ndim > 1 else a.reshape(1, -1)


def kernel(x, norm_mix_pre, w_in, conv_qkv_w, gdn_a_log, gdn_dt_bias, gdn_norm_w, conv_sc_w, w_branch, w_out, norm_mix_post, norm_ffn_pre, w_ff1, w_ff2, norm_ffn_post, loss_target, m_norm_mix_pre, m_w_in, m_conv_qkv_w, m_gdn_a_log, m_gdn_dt_bias, m_gdn_norm_w, m_conv_sc_w, m_w_branch, m_w_out, m_norm_mix_post, m_norm_ffn_pre, m_w_ff1, m_w_ff2, m_norm_ffn_post, v_norm_mix_pre, v_w_in, v_conv_qkv_w, v_gdn_a_log, v_gdn_dt_bias, v_gdn_norm_w, v_conv_sc_w, v_w_branch, v_w_out, v_norm_mix_post, v_norm_ffn_pre, v_w_ff1, v_w_ff2, v_norm_ffn_post):
    w = dict(norm_mix_pre=norm_mix_pre, w_in=w_in, conv_qkv_w=conv_qkv_w, gdn_a_log=gdn_a_log, gdn_dt_bias=gdn_dt_bias,
             gdn_norm_w=gdn_norm_w, conv_sc_w=conv_sc_w, w_branch=w_branch, w_out=w_out, norm_mix_post=norm_mix_post,
             norm_ffn_pre=norm_ffn_pre, w_ff1=w_ff1, w_ff2=w_ff2, norm_ffn_post=norm_ffn_post)
    m = dict(norm_mix_pre=m_norm_mix_pre, w_in=m_w_in, conv_qkv_w=m_conv_qkv_w, gdn_a_log=m_gdn_a_log,
             gdn_dt_bias=m_gdn_dt_bias, gdn_norm_w=m_gdn_norm_w, conv_sc_w=m_conv_sc_w, w_branch=m_w_branch, w_out=m_w_out,
             norm_mix_post=m_norm_mix_post, norm_ffn_pre=m_norm_ffn_pre, w_ff1=m_w_ff1, w_ff2=m_w_ff2,
             norm_ffn_post=m_norm_ffn_post)
    v = dict(norm_mix_pre=v_norm_mix_pre, w_in=v_w_in, conv_qkv_w=v_conv_qkv_w, gdn_a_log=v_gdn_a_log,
             gdn_dt_bias=v_gdn_dt_bias, gdn_norm_w=v_gdn_norm_w, conv_sc_w=v_conv_sc_w, w_branch=v_w_branch, w_out=v_w_out,
             norm_mix_post=v_norm_mix_post, norm_ffn_pre=v_norm_ffn_pre, w_ff1=v_w_ff1, w_ff2=v_w_ff2,
             norm_ffn_post=v_norm_ffn_post)

    me = _chip_index()

    shards = {k: _cast_bf16(_as2d(w[k]), name=f"cast_{k}").reshape(w[k].shape) for k in BIG}
    conv_names = ("conv_qkv_w", "conv_sc_w")
    FIRST, REST = ("w_in",), ("w_branch", "w_out", "w_ff1", "w_ff2")

    def gather_start(l, names, tag, after):
        srcs = [shards[k][l] for k in names] + ([w[k] for k in conv_names] if (l == 0 and "w_in" in names) else [])
        return _exchange_start(srcs, by_slot=False, name=f"gather_start{l}{tag}", after=after)

    def gather_land(ex, names, l, tag, after):
        own, lands = _exchange_wait(ex, after, by_slot=False, name=f"gather_wait{l}{tag}")
        return {k: _place_own(land, o, name=f"own_{k}") for k, land, o in zip(names, lands, own)}

    gathers = {0: gather_start(0, FIRST, "a", None)}
    conv = {}

    def weights_of(l, x_l):
        if l > 0:
            full = gather_land(gathers[l], BIG, l, "", x_l)
            p = _full_weights(full, conv, w, l)
            if l + 1 < DEPTH:
                gathers[l + 1] = gather_start(l + 1, BIG, "", full["w_out"])
                p["norm_mix_pre"] = p["norm_mix_pre"] + gathers[l + 1]["token"][0, 0]
            return p, None
        full = gather_land(gathers[0], FIRST + conv_names, 0, "a", gathers[0]["token"])
        conv.update({k: full[k] for k in conv_names})
        p = _full_weights({"w_in": full["w_in"]}, conv, w, 0)
        rest = gather_start(0, REST, "b", full["conv_sc_w"])
        p["norm_mix_pre"] = p["norm_mix_pre"] + rest["token"][0, 0]

        def late(after):
            arrived = gather_land(rest, REST, 0, "b", after)
            q = _full_weights(arrived, None, None, 0)
            gathers[1] = gather_start(1, BIG, "", arrived["w_out"])
            q["norm_mix_post"] = p["norm_mix_post"] + gathers[1]["token"][0, 0]
            return q

        return p, late

    grads, scatters = [None] * DEPTH, {}

    def scatter_start(l, g, names, tag):
        scatters[l, names] = _exchange_start(_partials_by_chip(g, names), by_slot=True, name=f"scatter_start{l}{tag}")
        return scatters[l, names]["token"][0, 0]

    def grads_early(g):
        return scatter_start(0, g, REST, "a")

    def grads_done(l, g):
        grads[l] = g
        return scatter_start(l, g, FIRST, "b") if l == 0 else scatter_start(l, g, BIG, "")

    loss, dx = _local_step(x[0], loss_target[0], DEPTH, weights_of, grads_done, grads_early)
    loss = lax.psum(loss, ("x", "y", "c"))

    sums = [dict() for _ in range(DEPTH)]
    for (l, names), ex in sorted(scatters.items(), key=lambda kv: (-kv[0][0], kv[0][1] != REST)):
        tag = "" if names == BIG else ("a" if names == REST else "b")
        parts, lands = _exchange_wait(ex, dx, by_slot=True, name=f"scatter_wait{l}{tag}")
        for k, r, o in zip(names, lands, parts):
            sums[l][k] = _sum_partials(r.reshape(N_CHIPS, -1, r.shape[-1]), o.reshape(N_CHIPS, -1, o.shape[-1]), name=f"sum_{k}")
    mine = [jnp.concatenate([sums[l][k] for l in range(DEPTH)], axis=0) for k in BIG]
    theirs = _swap_sibling(mine, name="swap_sibling")
    small, layout = _pack_small(grads)
    small_g = _unpack_small(_sum_slots(_gather_devices(small, name="gather_small"), name="sum_small"), layout)
    for k, width in (("conv_qkv_w", 3 * BRANCH_W // N_CHIPS), ("conv_sc_w", BRANCH_W // N_CHIPS)):
        small_g[k] = lax.dynamic_slice_in_dim(small_g[k], me * width, width, axis=2)

    out = {}
    for k, s_mine, s_theirs in zip(BIG, mine, theirs):
        res = _adamw(_as2d(w[k]), _as2d(m[k]), _as2d(v[k]), s_mine, s_theirs, name=f"adamw_{k}")
        out[k] = [r.reshape(w[k].shape) for r in res]
    for k in SMALL:
        res = _adamw(_as2d(w[k]), _as2d(m[k]), _as2d(v[k]), _as2d(small_g[k]), None, name=f"adamw_{k}")
        out[k] = [r.reshape(w[k].shape) for r in res]
    return (loss, dx[None], *[out[k][0] for k in ORDER], *[out[k][1] for k in ORDER], *[out[k][2] for k in ORDER],
            *[out[k][3] for k in ORDER])
```

```python
import functools

import jax
import jax.numpy as jnp
from jax import lax
from jax.experimental import pallas as pl
from jax.experimental.pallas import tpu as pltpu

F32 = jnp.float32
BF16 = jnp.bfloat16
MESH = pl.DeviceIdType.MESH

LANES = 128
D_MODEL = 1024
DEPTH = 4
CHUNK = 64
GDN_HEADS, GDN_DIM = 4, 128
SB_HEADS, SB_DIM = 8, 64
BRANCH_W = 512
N_BRANCH = 3
D_FF = 4 * D_MODEL
EPS = 1e-6
IN_W = 8200
AB_COL = 2048
AB_PAD = LANES - 8
IN_WP = IN_W + AB_PAD
N_CHIPS = 4
N_DEV = 8
GATES_COL = 5128
PB_GATES, PB_QKV, PB_GATE, PB_AB, PB_SB, PB_SCX, PB_SCB, PB_SCC = 0, 24, 36, 40, 41, 53, 57, 61
SB_TILE = 128
SB_GROUP = 4
SB_QTILE = 256
SB_SCALE = SB_DIM ** -0.5
SB_SCALE2 = SB_SCALE * 1.4426950408889634
GDN_QSCALE = GDN_DIM ** -0.5
VMEM_LIMIT = 56 * 1024 * 1024

ADAM_LR, ADAM_B1, ADAM_B2, ADAM_EPS, ADAM_WD, ADAM_STEP = 0.001, 0.9, 0.999, 1e-08, 0.01, 10

NT = (((1,), (1,)), ((), ()))
TN = (((0,), (0,)), ((), ()))
HI = lax.Precision.HIGH


def _pad_in_cols(w):
    return jnp.concatenate([w[:, GATES_COL:], w[:, :AB_COL + 8], jnp.zeros((w.shape[0], AB_PAD), w.dtype),
                            w[:, AB_COL + 8:GATES_COL]], axis=1)


def _unpad_in_cols(g):
    n_gates = IN_W - GATES_COL
    return jnp.concatenate([g[:, n_gates:n_gates + AB_COL + 8], g[:, n_gates + AB_COL + 8 + AB_PAD:], g[:, :n_gates]], axis=1)


IN_SHARD = IN_W // N_CHIPS
_IN_SEGMENTS = ((0, AB_COL + 8, IN_W - GATES_COL), (AB_COL + 8, GATES_COL, IN_W - GATES_COL + AB_PAD),
                (GATES_COL, IN_W, -GATES_COL))


def _in_cols_from_chips(slots):
    def cols(first, last):
        out = []
        for j in range(N_CHIPS):
            lo, hi = max(first, j * IN_SHARD), min(last, (j + 1) * IN_SHARD)
            if lo < hi:
                out.append(slots[j][:, lo - j * IN_SHARD:hi - j * IN_SHARD])
        return out

    head, tail, gates = (cols(first, last) for first, last, _ in _IN_SEGMENTS)
    return jnp.concatenate(gates + head + [jnp.zeros((slots.shape[1], AB_PAD), slots.dtype)] + tail, axis=1)


def _in_cols_to_chips(g):
    shards = []
    for j in range(N_CHIPS):
        pieces = []
        for first, last, shift in _IN_SEGMENTS:
            lo, hi = max(first, j * IN_SHARD), min(last, (j + 1) * IN_SHARD)
            if lo < hi:
                pieces.append(g[:, lo + shift:hi + shift])
        shards.append(jnp.concatenate(pieces, axis=1))
    return jnp.stack(shards)


def _params(*sem):
    return pltpu.CompilerParams(dimension_semantics=sem if sem else None, vmem_limit_bytes=VMEM_LIMIT)


def _sigmoid(x):
    return 1.0 / (1.0 + jnp.exp(-x))


def _softplus(x):
    return jnp.maximum(x, 0.0) + jnp.log(1.0 + jnp.exp(-jnp.abs(x)))


def _softplus2(x):
    return jnp.maximum(x, 0.0) + jnp.log2(1.0 + jnp.exp2(-jnp.abs(x)))


def _dot(a, b, dims=None, precision=None):
    if dims is None:
        return jnp.dot(a, b, preferred_element_type=F32, precision=precision)
    return lax.dot_general(a, b, dims, preferred_element_type=F32, precision=precision)


def _bdot(a, b, dims=None):
    return _dot(a.astype(BF16), b.astype(BF16), dims)


def _matmul(a, b, *, name, ta=False, tb=False, tm, tn, tk=None, outs=(F32,), epi=None, extras=(), b_chips=False,
            out_chips=False):
    if ta:
        kdim, m = a.shape
    else:
        m, kdim = a.shape
    if b_chips:
        per = b.shape[2]
        if tb:
            n, kb = b.shape[1], N_CHIPS * per
        else:
            kb, n = b.shape[1], N_CHIPS * per
    elif tb:
        n, kb = b.shape
    else:
        kb, n = b.shape
    assert kdim == kb, (a.shape, b.shape)
    tk = kdim if tk is None else min(tk, kdim)
    tm = min(tm, m)
    assert m % tm == 0 and n % tn == 0 and kdim % tk == 0, (m, n, kdim, tm, tn, tk)
    nk = kdim // tk
    ni, nj = m // tm, n // tn
    a_bytes, b_bytes = a.size * a.dtype.itemsize, b.size * b.dtype.itemsize
    cols_inner = a_bytes * (nj if nk > 1 else 1) + b_bytes * (ni if nj * nk > 1 else 1)
    rows_inner = b_bytes * (ni if nk > 1 else 1) + a_bytes * (nj if ni * nk > 1 else 1)
    swap = rows_inner < cols_inner

    def ix(f):
        return (lambda g0, g1, k: f(g1, g0, k)) if swap else f

    a_spec = pl.BlockSpec((tk, tm), ix(lambda i, j, k: (k, i))) if ta else pl.BlockSpec((tm, tk), ix(lambda i, j, k: (i, k)))
    if b_chips and tb:
        assert per % tk == 0
        b_spec = pl.BlockSpec((None, tn, tk), ix(lambda i, j, k: (k // (per // tk), j, k % (per // tk))))
    elif b_chips:
        assert per % tn == 0
        b_spec = pl.BlockSpec((None, tk, tn), ix(lambda i, j, k: (j // (per // tn), k, j % (per // tn))))
    elif tb:
        b_spec = pl.BlockSpec((tn, tk), ix(lambda i, j, k: (j, k)))
    else:
        b_spec = pl.BlockSpec((tk, tn), ix(lambda i, j, k: (k, j)))
    mn_spec = pl.BlockSpec((tm, tn), ix(lambda i, j, k: (i, j)))
    if out_chips:
        per_o = n // N_CHIPS
        assert per_o % tn == 0
        out_spec = pl.BlockSpec((None, tm, tn), ix(lambda i, j, k: (j // (per_o // tn), i, j % (per_o // tn))))
        out_dims = (N_CHIPS, m, per_o)
    else:
        out_spec, out_dims = mn_spec, (m, n)
    dims = (((0 if ta else 1,), (1 if tb else 0,)), ((), ()))
    n_ex, n_out = len(extras), len(outs)

    def body(a_ref, b_ref, *rest):
        ex, o, acc = rest[:n_ex], rest[n_ex:n_ex + n_out], rest[n_ex + n_out:]
        part = lax.dot_general(a_ref[...].astype(BF16), b_ref[...].astype(BF16), dims, preferred_element_type=F32)

        def finish(val):
            res = epi(val, *[e[...] for e in ex]) if epi is not None else (val,)
            for r, oref in zip(res, o):
                oref[...] = r.astype(oref.dtype)

        if nk == 1:
            finish(part)
        else:
            k = pl.program_id(2)

            @pl.when(k == 0)
            def _():
                acc[0][...] = part

            @pl.when(k > 0)
            def _():
                acc[0][...] += part

            @pl.when(k == nk - 1)
            def _():
                finish(acc[0][...])

    res = pl.pallas_call(
        body, name=name, grid=(nj, ni, nk) if swap else (ni, nj, nk),
        in_specs=[a_spec, b_spec] + [mn_spec] * n_ex,
        out_specs=[out_spec] * n_out,
        out_shape=[jax.ShapeDtypeStruct(out_dims, dt) for dt in outs],
        scratch_shapes=[pltpu.VMEM((tm, tn), F32)] if nk > 1 else [],
        compiler_params=_params("parallel", "parallel", "arbitrary"),
    )(a, b, *extras)
    return res[0] if n_out == 1 else res


ROW_TILE = 512


def _norm_fwd(x, w, *, name):
    s, d = x.shape

    def body(x_ref, w_ref, o_ref):
        xv = x_ref[...]
        r = lax.rsqrt(jnp.mean(xv * xv, axis=-1, keepdims=True) + EPS)
        o_ref[...] = (xv * r * w_ref[...]).astype(o_ref.dtype)

    return pl.pallas_call(
        body, name=name, grid=(s // ROW_TILE,),
        in_specs=[pl.BlockSpec((ROW_TILE, d), lambda i: (i, 0)), pl.BlockSpec((1, d), lambda i: (0, 0))],
        out_specs=pl.BlockSpec((ROW_TILE, d), lambda i: (i, 0)),
        out_shape=jax.ShapeDtypeStruct((s, d), BF16), compiler_params=_params("parallel"),
    )(x, w.reshape(1, d))


def _resnorm_fwd(x, u, w, *, name):
    s, d = x.shape

    def body(x_ref, u_ref, w_ref, o_ref):
        uv = u_ref[...]
        r = lax.rsqrt(jnp.mean(uv * uv, axis=-1, keepdims=True) + EPS)
        o_ref[...] = x_ref[...] + uv * r * w_ref[...]

    row = pl.BlockSpec((ROW_TILE, d), lambda i: (i, 0))
    return pl.pallas_call(
        body, name=name, grid=(s // ROW_TILE,),
        in_specs=[row, row, pl.BlockSpec((1, d), lambda i: (0, 0))], out_specs=row,
        out_shape=jax.ShapeDtypeStruct((s, d), F32), compiler_params=_params("parallel"),
    )(x, u, w.reshape(1, d))


def _norm_bwd(xin, w, dy, res, *, out_dtype, name):
    s, d = xin.shape
    has_res = res is not None

    def body(*refs):
        x_ref, w_ref, dy_ref = refs[:3]
        res_ref = refs[3] if has_res else None
        dx_ref, dw_ref = refs[3 + has_res:]
        xv, dyv = x_ref[...], dy_ref[...].astype(F32)
        r = lax.rsqrt(jnp.mean(xv * xv, axis=-1, keepdims=True) + EPS)
        xh = xv * r
        g = dyv * w_ref[...]
        dx = r * (g - xh * jnp.mean(g * xh, axis=-1, keepdims=True))
        if has_res:
            dx = dx + res_ref[...]
        dx_ref[...] = dx.astype(dx_ref.dtype)

        @pl.when(pl.program_id(0) == 0)
        def _():
            dw_ref[...] = jnp.zeros_like(dw_ref)

        dw_ref[...] += jnp.sum(dyv * xh, axis=0, keepdims=True)

    row = pl.BlockSpec((ROW_TILE, d), lambda i: (i, 0))
    vec = pl.BlockSpec((1, d), lambda i: (0, 0))
    ins = [xin, w.reshape(1, d), dy] + ([res] if has_res else [])
    dx, dw = pl.pallas_call(
        body, name=name, grid=(s // ROW_TILE,),
        in_specs=[row, vec, row] + ([row] if has_res else []), out_specs=[row, vec],
        out_shape=[jax.ShapeDtypeStruct((s, d), out_dtype), jax.ShapeDtypeStruct((1, d), F32)],
        compiler_params=_params("arbitrary"),
    )(*ins)
    return dx, dw.reshape(d)


def _loss_fwd_bwd(y, target, *, name):
    s, d = y.shape

    def body(y_ref, t_ref, loss_ref, dy_ref):
        e = y_ref[...] - t_ref[...]
        dy_ref[...] = e * (1.0 / d)

        @pl.when(pl.program_id(0) == 0)
        def _():
            loss_ref[...] = jnp.zeros_like(loss_ref)

        part = jnp.sum(jnp.sum(e * e, axis=1, keepdims=True), axis=0, keepdims=True)
        loss_ref[...] += part * (0.5 / d)

    row = pl.BlockSpec((ROW_TILE, d), lambda i: (i, 0))
    loss, dy = pl.pallas_call(
        body, name=name, grid=(s // ROW_TILE,), in_specs=[row, row],
        out_specs=[pl.BlockSpec((1, 1), lambda i: (0, 0)), row],
        out_shape=[jax.ShapeDtypeStruct((1, 1), F32), jax.ShapeDtypeStruct((s, d), F32)],
        compiler_params=_params("arbitrary"),
    )(y, target)
    return loss[0, 0], dy


def _shift_down(x, k, rows):
    if k == 0:
        return x
    return jnp.where(rows >= k, pltpu.roll(x, k, 0), 0.0)


def _shift_up(x, k, rows):
    if k == 0:
        return x
    n = x.shape[0]
    return jnp.where(rows < n - k, pltpu.roll(x, n - k, 0), 0.0)


def _col_spec(s, base):
    return pl.BlockSpec((s, LANES), lambda j: (0, base + j))


def _gdn_pre_math(x, w, j, rows):
    taps = w.shape[0]
    c = w[taps - 1:taps, :] * x
    for i in range(taps - 1):
        c = c + w[i:i + 1, :] * _shift_down(x, taps - 1 - i, rows)
    sg = _sigmoid(c)
    y = c * sg
    r = lax.rsqrt(jnp.sum(y * y, axis=-1, keepdims=True) + EPS)
    is_qk = j < 2 * GDN_HEADS
    scale = jnp.where(j < GDN_HEADS, GDN_QSCALE, 1.0)
    return c, sg, y, r, is_qk, scale


def _gdn_pre_fwd(proj, conv_w, *, name):
    s = proj.shape[0]

    def body(x_ref, w_ref, o_ref):
        j = pl.program_id(0)
        rows = lax.broadcasted_iota(jnp.int32, (s, LANES), 0)
        _, _, y, r, is_qk, scale = _gdn_pre_math(x_ref[...], w_ref[...], j, rows)
        o_ref[...] = jnp.where(is_qk, y * (r * scale), y)

    return pl.pallas_call(
        body, name=name, grid=(12,),
        in_specs=[_col_spec(s, PB_QKV), pl.BlockSpec((4, LANES), lambda j: (0, j))],
        out_specs=_col_spec(s, 0), out_shape=jax.ShapeDtypeStruct((s, 3 * BRANCH_W), F32),
        compiler_params=_params("parallel"),
    )(proj, conv_w)


def _gdn_pre_bwd(proj, conv_w, dqkvn, *, name):
    s = proj.shape[0]

    def body(x_ref, w_ref, d_ref, dx_ref, dw_ref):
        j = pl.program_id(0)
        rows = lax.broadcasted_iota(jnp.int32, (s, LANES), 0)
        x, w, dout = x_ref[...], w_ref[...], d_ref[...]
        c, sg, y, r, is_qk, scale = _gdn_pre_math(x, w, j, rows)
        yh = y * r
        dy_n = (scale * r) * (dout - yh * jnp.sum(dout * yh, axis=-1, keepdims=True))
        dy = jnp.where(is_qk, dy_n, dout)
        dc = dy * (sg * (1.0 + c * (1.0 - sg)))
        taps = w.shape[0]
        dx = w[taps - 1:taps, :] * dc
        dws = []
        for i in range(taps - 1):
            k = taps - 1 - i
            dx = dx + w[i:i + 1, :] * _shift_up(dc, k, rows)
            dws.append(jnp.sum(dc * _shift_down(x, k, rows), axis=0, keepdims=True))
        dws.append(jnp.sum(dc * x, axis=0, keepdims=True))
        dx_ref[...] = dx.astype(dx_ref.dtype)
        for i in range(taps):
            dw_ref[i:i + 1, :] = dws[i]

    return pl.pallas_call(
        body, name=name, grid=(12,),
        in_specs=[_col_spec(s, PB_QKV), pl.BlockSpec((4, LANES), lambda j: (0, j)), _col_spec(s, 0)],
        out_specs=[_col_spec(s, 0), pl.BlockSpec((4, LANES), lambda j: (0, j))],
        out_shape=[jax.ShapeDtypeStruct((s, 3 * BRANCH_W), BF16), jax.ShapeDtypeStruct((4, 3 * BRANCH_W), F32)],
        compiler_params=_params("parallel"),
    )(proj, conv_w, dqkvn)


def _lane_pad(v):
    return jnp.pad(v.reshape(1, -1), ((0, 0), (0, LANES - v.shape[0])))


def _gdn_gates_fwd(proj, a_log, dt_bias, *, name):
    s = proj.shape[0]

    def body(ab_ref, al_ref, dt_ref, o_ref):
        ab = ab_ref[...]
        lane = lax.broadcasted_iota(jnp.int32, (1, LANES), 1)
        g = -jnp.exp(al_ref[...]) * _softplus(ab + dt_ref[...])
        o_ref[...] = jnp.where(lane < GDN_HEADS, g, _sigmoid(ab))

    vec = pl.BlockSpec((1, LANES), lambda j: (0, 0))
    return pl.pallas_call(
        body, name=name, grid=(1,), in_specs=[_col_spec(s, PB_AB), vec, vec], out_specs=_col_spec(s, 0),
        out_shape=jax.ShapeDtypeStruct((s, LANES), F32), compiler_params=_params("arbitrary"),
    )(proj, _lane_pad(a_log), _lane_pad(dt_bias))


def _gdn_gates_bwd(proj, a_log, dt_bias, dgb, *, name):
    s = proj.shape[0]

    def body(ab_ref, al_ref, dt_ref, d_ref, dab_ref, dal_ref, ddt_ref):
        ab, d = ab_ref[...], d_ref[...]
        lane = lax.broadcasted_iota(jnp.int32, (1, LANES), 1)
        ea = jnp.exp(al_ref[...])
        pre = ab + dt_ref[...]
        g = -ea * _softplus(pre)
        dpre = d * (-ea) * _sigmoid(pre)
        beta = _sigmoid(ab)
        is_g = lane < GDN_HEADS
        dab = jnp.where(is_g, dpre, jnp.where(lane < 2 * GDN_HEADS, d * beta * (1.0 - beta), 0.0))
        dab_ref[...] = dab.astype(dab_ref.dtype)
        dal_ref[...] = jnp.sum(jnp.where(is_g, d * g, 0.0), axis=0, keepdims=True)
        ddt_ref[...] = jnp.sum(jnp.where(is_g, dpre, 0.0), axis=0, keepdims=True)

    vec = pl.BlockSpec((1, LANES), lambda j: (0, 0))
    dab, dal, ddt = pl.pallas_call(
        body, name=name, grid=(1,), in_specs=[_col_spec(s, PB_AB), vec, vec, _col_spec(s, 0)],
        out_specs=[_col_spec(s, 0), vec, vec],
        out_shape=[jax.ShapeDtypeStruct((s, LANES), BF16), jax.ShapeDtypeStruct((1, LANES), F32),
                   jax.ShapeDtypeStruct((1, LANES), F32)],
        compiler_params=_params("arbitrary"),
    )(proj, _lane_pad(a_log), _lane_pad(dt_bias), dgb)
    return dab, dal[0, :GDN_HEADS], ddt[0, :GDN_HEADS]


def _interleave(gens):
    results, live = [None] * len(gens), list(range(len(gens)))
    while live:
        for idx in list(live):
            try:
                next(gens[idx])
            except StopIteration as done:
                results[idx] = done.value
                live.remove(idx)
    return results


def _chunk_common(q, k, v, gb, gbt, h):
    c = CHUNK
    row = lax.broadcasted_iota(jnp.int32, (c, c), 0)
    col = lax.broadcasted_iota(jnp.int32, (c, c), 1)
    tril, strict, eye = row >= col, row > col, row == col
    lane = lax.broadcasted_iota(jnp.int32, (c, LANES), 1)
    sub = lax.broadcasted_iota(jnp.int32, (2 * GDN_HEADS, c), 0)
    g_col = jnp.sum(jnp.where(lane == h, gb, 0.0), axis=1, keepdims=True)
    beta_col = jnp.sum(jnp.where(lane == GDN_HEADS + h, gb, 0.0), axis=1, keepdims=True)
    g_row = jnp.sum(jnp.where(sub == h, gbt, 0.0), axis=0, keepdims=True)
    gc_col = jnp.sum(jnp.where(tril, jnp.broadcast_to(g_row, (c, c)), 0.0), axis=1, keepdims=True)
    gc_row = jnp.sum(jnp.where(row <= col, jnp.broadcast_to(g_col, (c, c)), 0.0), axis=0, keepdims=True)
    g_tot = jnp.sum(g_row, axis=1, keepdims=True)
    dm = jnp.exp(jnp.where(tril, gc_col - gc_row, -1e30))
    e_col = jnp.exp(gc_col)
    kdec_col = jnp.exp(g_tot - gc_col)
    gamma = jnp.exp(g_tot)
    kb = k * beta_col
    vb = v * beta_col
    kbg = kb * e_col
    kk = _bdot(kb, k, NT)
    qk = _bdot(q, k, NT)
    yield
    a = jnp.where(strict, kk * dm, 0.0)
    aqk = jnp.where(tril, qk * dm, 0.0)
    bneg = -a
    t = jnp.where(eye, 1.0, 0.0) + bneg
    p = _dot(bneg, bneg, precision=HI)
    yield
    for lvl in range(5):
        t_next = t + _dot(t, p, precision=HI)
        if lvl < 4:
            p = _dot(p, p, precision=HI)
        t = t_next
        yield
    u = _dot(t, vb, precision=HI)
    w = _dot(t, kbg, precision=HI)
    yield
    return dict(tril=tril, strict=strict, eye=eye, row=row, col=col, beta_col=beta_col, dm=dm, e_col=e_col,
                kdec_col=kdec_col, gamma=gamma, kb=kb, vb=vb, kbg=kbg, a=a, t=t, u=u, w=w, aqk=aqk,
                qd=q * e_col, kd=k * kdec_col)


def _gdn_chunk_fwd(qkvn, gb, gbt, *, name):
    s = qkvn.shape[0]
    n_chunks = s // CHUNK

    def body(q_ref, k_ref, v_ref, gb_ref, gbt_ref, o_ref, st_ref, state):
        @pl.when(pl.program_id(0) == 0)
        def _():
            state[...] = jnp.zeros_like(state)

        gbv, gbtv = gb_ref[...], gbt_ref[0]

        def head(h):
            hs = slice(h * GDN_DIM, (h + 1) * GDN_DIM)
            q, k, v = q_ref[:, hs], k_ref[:, hs], v_ref[:, hs]
            m = yield from _chunk_common(q, k, v, gbv, gbtv, h)
            s0 = state[h]
            st_ref[0, h] = s0
            vnew = m["u"] - _bdot(m["w"], s0)
            o_inter = _bdot(m["qd"], s0)
            yield
            o_ref[:, hs] = o_inter + _bdot(m["aqk"], vnew)
            state[h] = m["gamma"] * s0 + _bdot(m["kd"], vnew, TN)

        _interleave([head(h) for h in range(GDN_HEADS)])

    blk = lambda j: pl.BlockSpec((CHUNK, BRANCH_W), lambda n: (n, j))
    return pl.pallas_call(
        body, name=name, grid=(n_chunks,),
        in_specs=[blk(0), blk(1), blk(2), pl.BlockSpec((CHUNK, LANES), lambda n: (n, 0)),
                  pl.BlockSpec((1, 2 * GDN_HEADS, CHUNK), lambda n: (n, 0, 0))],
        out_specs=[blk(0), pl.BlockSpec((1, GDN_HEADS, GDN_DIM, GDN_DIM), lambda n: (n, 0, 0, 0))],
        out_shape=[jax.ShapeDtypeStruct((s, BRANCH_W), F32),
                   jax.ShapeDtypeStruct((n_chunks, GDN_HEADS, GDN_DIM, GDN_DIM), F32)],
        scratch_shapes=[pltpu.VMEM((GDN_HEADS, GDN_DIM, GDN_DIM), F32)],
        compiler_params=_params("arbitrary"),
    )(qkvn, qkvn, qkvn, gb, gbt)


def _gdn_chunk_bwd(qkvn, gb, gbt, states, do, *, name):
    s = qkvn.shape[0]
    n_chunks = s // CHUNK
    c = CHUNK

    def body(q_ref, k_ref, v_ref, gb_ref, gbt_ref, st_ref, do_ref, dqkv_ref, dgb_ref, dstate):
        @pl.when(pl.program_id(0) == 0)
        def _():
            dstate[...] = jnp.zeros_like(dstate)

        gbv, gbtv = gb_ref[...], gbt_ref[0]
        lane = lax.broadcasted_iota(jnp.int32, (c, LANES), 1)
        def head(h):
            hs = slice(h * GDN_DIM, (h + 1) * GDN_DIM)
            q, k, v, dov = q_ref[:, hs], k_ref[:, hs], v_ref[:, hs], do_ref[:, hs]
            m = yield from _chunk_common(q, k, v, gbv, gbtv, h)
            tril, strict, eye, row, col = m["tril"], m["strict"], m["eye"], m["row"], m["col"]
            s0, ds1 = st_ref[0, h], dstate[h]
            vnew = m["u"] - _bdot(m["w"], s0)
            dvnew_a = _bdot(m["aqk"], dov, TN) + _bdot(m["kd"], ds1)
            dqd = _bdot(dov, s0, NT)
            ds_q = _bdot(m["qd"], dov, TN)
            dgamma = jnp.sum(jnp.sum(s0 * ds1, axis=1, keepdims=True), axis=0, keepdims=True)
            yield
            dvnew = dvnew_a
            daqk = jnp.where(tril, _bdot(dov, vnew, NT), 0.0)
            dkd = _bdot(vnew, ds1, NT)
            dw = -_bdot(dvnew, s0, NT)
            dstate[h] = m["gamma"] * ds1 + ds_q - _bdot(m["w"], dvnew, TN)
            dvb = _dot(m["t"], dvnew, TN, HI)
            yield
            dt = _dot(dvnew, m["vb"], NT, HI) + _dot(dw, m["kbg"], NT, HI)
            dkbg = _dot(m["t"], dw, TN, HI)
            dmq = daqk * m["dm"]
            dq = _bdot(dmq, k) + dqd * m["e_col"]
            dk_q = _bdot(dmq, q, TN)
            yield
            tdt = _dot(m["t"], dt, TN, HI)
            yield
            da = jnp.where(strict, -_dot(tdt, m["t"], NT, HI), 0.0)
            yield
            dmat = da * m["dm"]
            dkb = _bdot(dmat, k) + dkbg * m["e_col"]
            dk = (_bdot(dmat, m["kb"], TN) + dk_q + dkd * m["kdec_col"] + m["beta_col"] * dkb)
            yield
            dbeta_col = jnp.sum(dkb * k, axis=1, keepdims=True) + jnp.sum(dvb * v, axis=1, keepdims=True)
            e = da * m["a"] + daqk * m["aqk"]
            rs_kd = jnp.sum(dkd * m["kd"], axis=1, keepdims=True)
            e_colsum = jnp.sum(e, axis=0, keepdims=True)
            e_colsum_c = jnp.sum(jnp.where(eye, jnp.broadcast_to(e_colsum, (c, c)), 0.0), axis=1, keepdims=True)
            dgc = (jnp.sum(e, axis=1, keepdims=True) - e_colsum_c + jnp.sum(dqd * m["qd"], axis=1, keepdims=True)
                   - rs_kd + jnp.sum(dkbg * m["kbg"], axis=1, keepdims=True))
            last = jnp.sum(rs_kd, axis=0, keepdims=True) + dgamma * m["gamma"]
            dgc = dgc + jnp.where(lax.broadcasted_iota(jnp.int32, (c, 1), 0) == c - 1, last, 0.0)
            dgc_row = jnp.sum(jnp.where(eye, jnp.broadcast_to(dgc, (c, c)), 0.0), axis=0, keepdims=True)
            dg_col = jnp.sum(jnp.where(col >= row, jnp.broadcast_to(dgc_row, (c, c)), 0.0), axis=1, keepdims=True)
            for part, val in enumerate((dq, dk, m["beta_col"] * dvb)):
                lo = part * BRANCH_W + h * GDN_DIM
                dqkv_ref[:, lo:lo + GDN_DIM] = val
            return jnp.where(lane == h, dg_col, 0.0) + jnp.where(lane == GDN_HEADS + h, dbeta_col, 0.0)

        parts = _interleave([head(h) for h in range(GDN_HEADS)])
        dgb_ref[...] = (parts[0] + parts[1]) + (parts[2] + parts[3])

    rev = lambda n: n_chunks - 1 - n
    blk = lambda j: pl.BlockSpec((CHUNK, BRANCH_W), lambda n: (rev(n), j))
    return pl.pallas_call(
        body, name=name, grid=(n_chunks,),
        in_specs=[blk(0), blk(1), blk(2), pl.BlockSpec((CHUNK, LANES), lambda n: (rev(n), 0)),
                  pl.BlockSpec((1, 2 * GDN_HEADS, CHUNK), lambda n: (rev(n), 0, 0)),
                  pl.BlockSpec((1, GDN_HEADS, GDN_DIM, GDN_DIM), lambda n: (rev(n), 0, 0, 0)), blk(0)],
        out_specs=[pl.BlockSpec((CHUNK, 3 * BRANCH_W), lambda n: (rev(n), 0)),
                   pl.BlockSpec((CHUNK, LANES), lambda n: (rev(n), 0))],
        out_shape=[jax.ShapeDtypeStruct((s, 3 * BRANCH_W), F32), jax.ShapeDtypeStruct((s, LANES), F32)],
        scratch_shapes=[pltpu.VMEM((GDN_HEADS, GDN_DIM, GDN_DIM), F32)],
        compiler_params=_params("arbitrary"),
    )(qkvn, qkvn, qkvn, gb, gbt, states, do)


def _gdn_post_fwd(o, proj, norm_w, *, name):
    s = o.shape[0]

    def body(o_ref, g_ref, w_ref, y_ref):
        ov, gv = o_ref[...], g_ref[...]
        r = lax.rsqrt(jnp.mean(ov * ov, axis=-1, keepdims=True) + EPS)
        y_ref[...] = (ov * r * w_ref[...] * (gv * _sigmoid(gv))).astype(y_ref.dtype)

    return pl.pallas_call(
        body, name=name, grid=(GDN_HEADS,),
        in_specs=[_col_spec(s, 0), _col_spec(s, PB_GATE), pl.BlockSpec((1, LANES), lambda j: (0, 0))],
        out_specs=_col_spec(s, 0), out_shape=jax.ShapeDtypeStruct((s, BRANCH_W), BF16),
        compiler_params=_params("parallel"),
    )(o, proj, norm_w.reshape(1, GDN_DIM))


def _gdn_post_bwd(o, proj, norm_w, dy, *, name):
    s = o.shape[0]

    def body(o_ref, g_ref, w_ref, dy_ref, do_ref, dg_ref, dw_ref):
        ov, gv, w, dyv = o_ref[...], g_ref[...], w_ref[...], dy_ref[...].astype(F32)
        r = lax.rsqrt(jnp.mean(ov * ov, axis=-1, keepdims=True) + EPS)
        oh = ov * r
        sg = _sigmoid(gv)
        silu = gv * sg
        dn = dyv * silu
        dg_ref[...] = (dyv * (oh * w) * (sg * (1.0 + gv * (1.0 - sg)))).astype(dg_ref.dtype)

        @pl.when(pl.program_id(0) == 0)
        def _():
            dw_ref[...] = jnp.zeros_like(dw_ref)

        dw_ref[...] += jnp.sum(dn * oh, axis=0, keepdims=True)
        g2 = dn * w
        do_ref[...] = r * (g2 - oh * jnp.mean(g2 * oh, axis=-1, keepdims=True))

    do, dg, dw = pl.pallas_call(
        body, name=name, grid=(GDN_HEADS,),
        in_specs=[_col_spec(s, 0), _col_spec(s, PB_GATE), pl.BlockSpec((1, LANES), lambda j: (0, 0)), _col_spec(s, 0)],
        out_specs=[_col_spec(s, 0), _col_spec(s, 0), pl.BlockSpec((1, LANES), lambda j: (0, 0))],
        out_shape=[jax.ShapeDtypeStruct((s, BRANCH_W), F32), jax.ShapeDtypeStruct((s, BRANCH_W), BF16),
                   jax.ShapeDtypeStruct((1, LANES), F32)],
        compiler_params=_params("arbitrary"),
    )(o, proj, norm_w.reshape(1, GDN_DIM), dy)
    return do, dg, dw.reshape(LANES)


def _split_terms(x):
    hi = x.astype(BF16)
    lo = (x - hi.astype(F32)).astype(BF16)
    return jnp.concatenate([hi, lo], axis=1)


def _sb_sum_matrix(pred):
    row = lax.broadcasted_iota(jnp.int32, (2 * SB_TILE, 2 * SB_TILE), 0) % SB_TILE
    col = lax.broadcasted_iota(jnp.int32, (2 * SB_TILE, 2 * SB_TILE), 1)
    return jnp.where((col >= SB_TILE) | pred(row, col), 1.0, 0.0).astype(BF16)


def _sb_head_masks():
    lane = lax.broadcasted_iota(jnp.int32, (1, LANES), 1)
    return [(lane < SB_DIM).astype(F32), (lane >= SB_DIM).astype(F32)]


def _sb_fwd(proj, *, name):
    s = proj.shape[0]
    t, tq = SB_TILE, SB_QTILE
    nq = s // tq

    def body(q_ref, k_ref, v_ref, o_ref, tot_ref):
        cmr = lax.broadcasted_iota(jnp.int32, (tq, t), 1) - lax.broadcasted_iota(jnp.int32, (tq, t), 0)
        uo = _sb_sum_matrix(lambda row, col: row > col)
        hm = _sb_head_masks()

        def qloop(i, carry0):
            qs = pl.multiple_of(i * tq, tq)
            qf = q_ref[pl.ds(qs, tq), :] * SB_SCALE2
            qh = [(qf * hm[h]).astype(BF16) for h in range(2)]
            diag = (i * tq) // (SB_GROUP * t)

            def group(g, st, masked):
                ks = pl.multiple_of(g * (SB_GROUP * t), SB_GROUP * t)
                kb = k_ref[pl.ds(ks, SB_GROUP * t), :].astype(BF16)
                vf = v_ref[pl.ds(ks, SB_GROUP * t), :]
                tiles = [(h, j) for h in range(2) for j in range(SB_GROUP)]
                z = [_dot(qh[h], kb, NT) for h in range(2)]
                keep = {j: cmr < i * tq - (g * SB_GROUP + j) * t for j in range(SB_GROUP)} if masked else None
                base, terms = {}, {}
                for h, j in tiles:
                    zj = z[h][:, j * t:(j + 1) * t]
                    sp = _softplus2(zj)
                    base[h, j] = zj - sp
                    terms[h, j] = _split_terms(jnp.where(keep[j], sp, 0.0) if masked else sp)
                sums = {hj: _dot(terms[hj], uo) for hj in tiles}
                acc, new = st[0], []
                for h in range(2):
                    run, att = st[1 + h], [None] * SB_GROUP
                    for j in reversed(range(SB_GROUP)):
                        a = jnp.exp2(base[h, j] - (sums[h, j][:, :t] + run))
                        att[j] = (jnp.where(keep[j], a, 0.0) if masked else a).astype(BF16)
                        run = run + sums[h, j][:, t:]
                    acc = acc + _dot(jnp.concatenate(att, axis=1), (vf * hm[h]).astype(BF16))
                    new.append(run)
                return (acc, *new)

            zero = jnp.zeros((tq, LANES), F32)
            st = group(diag, (zero, zero, zero), True)
            st = lax.fori_loop(0, diag, lambda jj, sv: group(diag - 1 - jj, sv, False), st)
            o_ref[pl.ds(qs, tq), :] = st[0]
            tot_ref[pl.ds(qs, tq), :] = st[1] * hm[0] + st[2] * hm[1]
            return carry0

        lax.fori_loop(0, nq, qloop, 0)

    out = jax.ShapeDtypeStruct((s, BRANCH_W), F32)
    return pl.pallas_call(
        body, name=name, grid=(SB_HEADS // 2,),
        in_specs=[_col_spec(s, PB_SB), _col_spec(s, PB_SB + 4), _col_spec(s, PB_SB + 8)],
        out_specs=[_col_spec(s, 0)] * 2, out_shape=[out] * 2,
        compiler_params=_params("parallel"),
    )(proj, proj, proj)


def _sb_bwd(proj, tot, do, *, name):
    s = proj.shape[0]
    t, tq = SB_TILE, SB_QTILE
    nq = s // tq

    def body(q_ref, k_ref, v_ref, tot_ref, do_ref, dq_ref, dk_ref, dv_ref, dk_acc, dv_acc):
        dk_acc[...] = jnp.zeros_like(dk_acc)
        dv_acc[...] = jnp.zeros_like(dv_acc)
        cmr = lax.broadcasted_iota(jnp.int32, (tq, t), 1) - lax.broadcasted_iota(jnp.int32, (tq, t), 0)
        u_le = _sb_sum_matrix(lambda row, col: row <= col)
        u_lt = _sb_sum_matrix(lambda row, col: row < col)
        hm = _sb_head_masks()

        def qloop(i, carry0):
            qs = pl.multiple_of(i * tq, tq)
            qraw = q_ref[pl.ds(qs, tq), :]
            dov = do_ref[pl.ds(qs, tq), :].astype(F32)
            totv = tot_ref[pl.ds(qs, tq), :]
            qh = [(qraw * (hm[h] * SB_SCALE2)).astype(BF16) for h in range(2)]
            q2 = jnp.concatenate([(qraw * hm[h]).astype(BF16) for h in range(2)], axis=0)
            doh = [(dov * hm[h]).astype(BF16) for h in range(2)]
            do2 = jnp.concatenate(doh, axis=0)
            tot = [jnp.max(totv * hm[h], axis=1, keepdims=True) for h in range(2)]
            diag = (i * tq) // (SB_GROUP * t)

            def group(g, st, masked):
                ks = pl.multiple_of(g * (SB_GROUP * t), SB_GROUP * t)
                kf = k_ref[pl.ds(ks, SB_GROUP * t), :]
                kb = kf.astype(BF16)
                vb = v_ref[pl.ds(ks, SB_GROUP * t), :].astype(BF16)
                tiles = [(h, j) for h in range(2) for j in range(SB_GROUP)]
                z = [_dot(qh[h], kb, NT) for h in range(2)]
                datt = [_dot(doh[h], vb, NT) for h in range(2)]
                keep = {j: cmr < i * tq - (g * SB_GROUP + j) * t for j in range(SB_GROUP)} if masked else None
                ls, lterms = {}, {}
                for h, j in tiles:
                    zj = z[h][:, j * t:(j + 1) * t]
                    sp = _softplus2(zj)
                    ls[h, j] = zj - sp
                    lterms[h, j] = _split_terms(jnp.where(keep[j], sp, 0.0) if masked else sp)
                lsum = {hj: _dot(lterms[hj], u_le) for hj in tiles}
                att, p, pterms, new_c = {}, {}, {}, []
                for h in range(2):
                    run = st[1 + h]
                    for j in range(SB_GROUP):
                        a = jnp.exp2(ls[h, j] - ((tot[h] - run) - lsum[h, j][:, :t]))
                        if masked:
                            a = jnp.where(keep[j], a, 0.0)
                        att[h, j] = a.astype(BF16)
                        p[h, j] = a * datt[h][:, j * t:(j + 1) * t]
                        pterms[h, j] = _split_terms(p[h, j])
                        run = run + lsum[h, j][:, t:]
                    new_c.append(run)
                psum = {hj: _dot(pterms[hj], u_lt) for hj in tiles}
                dzb, new_r = {}, []
                for h in range(2):
                    run = st[3 + h]
                    for j in range(SB_GROUP):
                        sig = jnp.exp2(ls[h, j])
                        dz = p[h, j] - sig * (p[h, j] + run + psum[h, j][:, :t])
                        if masked:
                            dz = jnp.where(keep[j], dz, 0.0)
                        dzb[h, j] = (dz * SB_SCALE).astype(BF16)
                        run = run + psum[h, j][:, t:]
                    new_r.append(run)
                k2 = jnp.concatenate([(kf * hm[h]).astype(BF16) for h in range(2)], axis=0)
                dq_acc = st[0] + _dot(jnp.concatenate([dzb[hj] for hj in tiles], axis=1), k2)
                for j in range(SB_GROUP):
                    rows = pl.ds(pl.multiple_of(ks + j * t, t), t)
                    dk_acc[rows, :] += _dot(jnp.concatenate([dzb[0, j], dzb[1, j]], axis=0), q2, TN)
                    dv_acc[rows, :] += _dot(jnp.concatenate([att[0, j], att[1, j]], axis=0), do2, TN)
                return (dq_acc, *new_c, *new_r)

            zero = jnp.zeros((tq, LANES), F32)
            st = lax.fori_loop(0, diag, lambda jj, sv: group(jj, sv, False), (zero,) * 5)
            st = group(diag, st, True)
            dq_ref[pl.ds(qs, tq), :] = st[0].astype(dq_ref.dtype)
            return carry0

        lax.fori_loop(0, nq, qloop, 0)
        dk_ref[...] = dk_acc[...].astype(dk_ref.dtype)
        dv_ref[...] = dv_acc[...].astype(dv_ref.dtype)

    out = jax.ShapeDtypeStruct((s, BRANCH_W), BF16)
    return pl.pallas_call(
        body, name=name, grid=(SB_HEADS // 2,),
        in_specs=[_col_spec(s, PB_SB), _col_spec(s, PB_SB + 4), _col_spec(s, PB_SB + 8), _col_spec(s, 0), _col_spec(s, 0)],
        out_specs=[_col_spec(s, 0)] * 3, out_shape=[out] * 3,
        scratch_shapes=[pltpu.VMEM((s, LANES), F32), pltpu.VMEM((s, LANES), F32)],
        compiler_params=_params("parallel"),
    )(proj, proj, proj, tot, do)


def _sc_fwd(proj, conv_w, *, name):
    s = proj.shape[0]

    def body(x_ref, b_ref, c_ref, w_ref, y_ref):
        rows = lax.broadcasted_iota(jnp.int32, (s, LANES), 0)
        w = w_ref[...]
        u = c_ref[...] * x_ref[...]
        cv = w[2:3, :] * u + w[1:2, :] * _shift_down(u, 1, rows) + w[0:1, :] * _shift_down(u, 2, rows)
        y_ref[...] = (b_ref[...] * cv).astype(y_ref.dtype)

    return pl.pallas_call(
        body, name=name, grid=(BRANCH_W // LANES,),
        in_specs=[_col_spec(s, PB_SCX), _col_spec(s, PB_SCB), _col_spec(s, PB_SCC), pl.BlockSpec((3, LANES), lambda j: (0, j))],
        out_specs=_col_spec(s, 0), out_shape=jax.ShapeDtypeStruct((s, BRANCH_W), BF16),
        compiler_params=_params("parallel"),
    )(proj, proj, proj, conv_w)


def _sc_bwd(proj, conv_w, dy, *, name):
    s = proj.shape[0]

    def body(x_ref, b_ref, c_ref, w_ref, dy_ref, dx_ref, db_ref, dc_ref, dw_ref):
        rows = lax.broadcasted_iota(jnp.int32, (s, LANES), 0)
        w, x, cg, dyv = w_ref[...], x_ref[...], c_ref[...], dy_ref[...].astype(F32)
        u = cg * x
        u1, u2 = _shift_down(u, 1, rows), _shift_down(u, 2, rows)
        cv = w[2:3, :] * u + w[1:2, :] * u1 + w[0:1, :] * u2
        db_ref[...] = (dyv * cv).astype(db_ref.dtype)
        dcv = dyv * b_ref[...]
        du = w[2:3, :] * dcv + w[1:2, :] * _shift_up(dcv, 1, rows) + w[0:1, :] * _shift_up(dcv, 2, rows)
        dx_ref[...] = (du * cg).astype(dx_ref.dtype)
        dc_ref[...] = (du * x).astype(dc_ref.dtype)
        dw_ref[0:1, :] = jnp.sum(dcv * u2, axis=0, keepdims=True)
        dw_ref[1:2, :] = jnp.sum(dcv * u1, axis=0, keepdims=True)
        dw_ref[2:3, :] = jnp.sum(dcv * u, axis=0, keepdims=True)

    out = jax.ShapeDtypeStruct((s, BRANCH_W), BF16)
    wspec = pl.BlockSpec((3, LANES), lambda j: (0, j))
    return pl.pallas_call(
        body, name=name, grid=(BRANCH_W // LANES,),
        in_specs=[_col_spec(s, PB_SCX), _col_spec(s, PB_SCB), _col_spec(s, PB_SCC), wspec, _col_spec(s, 0)],
        out_specs=[_col_spec(s, 0)] * 3 + [wspec],
        out_shape=[out] * 3 + [jax.ShapeDtypeStruct((3, BRANCH_W), F32)],
        compiler_params=_params("parallel"),
    )(proj, proj, proj, conv_w, dy)


MERGE_TM, MERGE_TN = 512, D_MODEL // N_CHIPS


def _merge_specs():
    tm, tn = MERGE_TM, MERGE_TN
    y_spec = pl.BlockSpec((tm, BRANCH_W), lambda i, j: (i, 0))
    w_spec = pl.BlockSpec((None, N_BRANCH, BRANCH_W, tn), lambda i, j: (j, 0, 0, 0))
    gate_specs = [pl.BlockSpec((tm, tn), functools.partial(
        lambda i, j, b: (i, (PB_GATES * LANES + b * D_MODEL) // tn + j), b=b)) for b in range(N_BRANCH)]
    mn = pl.BlockSpec((tm, tn), lambda i, j: (i, j))
    return y_spec, w_spec, gate_specs, mn


def _merge_fwd(ya, yb, yc, wb, proj, *, name):
    s = ya.shape[0]
    y_spec, w_spec, gate_specs, mn = _merge_specs()

    def body(ya_ref, yb_ref, yc_ref, w_ref, g0, g1, g2, o_ref):
        acc = None
        for b, (y_ref, g_ref) in enumerate(zip((ya_ref, yb_ref, yc_ref), (g0, g1, g2))):
            term = _sigmoid(g_ref[...]) * _bdot(y_ref[...], w_ref[b])
            acc = term if acc is None else acc + term
        o_ref[...] = acc.astype(o_ref.dtype)

    return pl.pallas_call(
        body, name=name, grid=(s // MERGE_TM, D_MODEL // MERGE_TN),
        in_specs=[y_spec] * 3 + [w_spec] + gate_specs, out_specs=mn,
        out_shape=jax.ShapeDtypeStruct((s, D_MODEL), BF16), compiler_params=_params("parallel", "parallel"),
    )(ya, yb, yc, wb, proj, proj, proj)


def _merge_bwd(ya, yb, yc, wb, proj, dm, *, name):
    s = ya.shape[0]
    y_spec, w_spec, gate_specs, mn = _merge_specs()

    def body(ya_ref, yb_ref, yc_ref, w_ref, g0, g1, g2, dm_ref, *outs):
        dmv = dm_ref[...].astype(F32)
        for b, (y_ref, g_ref) in enumerate(zip((ya_ref, yb_ref, yc_ref), (g0, g1, g2))):
            sg = _sigmoid(g_ref[...])
            z = _bdot(y_ref[...], w_ref[b])
            outs[b][...] = (dmv * sg).astype(BF16)
            outs[N_BRANCH + b][...] = (dmv * z * sg * (1.0 - sg)).astype(BF16)

    out = jax.ShapeDtypeStruct((s, D_MODEL), BF16)
    res = pl.pallas_call(
        body, name=name, grid=(s // MERGE_TM, D_MODEL // MERGE_TN),
        in_specs=[y_spec] * 3 + [w_spec] + gate_specs + [mn], out_specs=[mn] * (2 * N_BRANCH),
        out_shape=[out] * (2 * N_BRANCH), compiler_params=_params("parallel", "parallel"),
    )(ya, yb, yc, wb, proj, proj, proj, dm)
    return res[:N_BRANCH], res[N_BRANCH:]


def _chunk_rows(v, s):
    return v[:, :2 * GDN_HEADS].reshape(s // CHUNK, CHUNK, 2 * GDN_HEADS).transpose(0, 2, 1)


def _relu2_epi(acc):
    r = jnp.maximum(acc, 0.0)
    return acc, r * r


def _drelu2_epi(acc, a):
    return (acc * (2.0 * jnp.maximum(a.astype(F32), 0.0)),)


def _layer_fwd(x0, p, late=None):
    s = x0.shape[0]
    h1 = _norm_fwd(x0, p["norm_mix_pre"], name="norm_mix_pre")
    proj = _matmul(h1, p["w_in"], name="proj_in", tm=512, tn=1664)
    qkvn = _gdn_pre_fwd(proj, p["conv_qkv_w"], name="gdn_pre")
    gb = _gdn_gates_fwd(proj, p["gdn_a_log"], p["gdn_dt_bias"], name="gdn_gates")
    gbt = _chunk_rows(gb, s)
    o_gdn, states = _gdn_chunk_fwd(qkvn, gb, gbt, name="gdn_chunk")
    ya = _gdn_post_fwd(o_gdn, proj, p["gdn_norm_w"], name="gdn_post")
    o_sb, sb_tot = _sb_fwd(proj, name="sb_attn")
    yc = _sc_fwd(proj, p["conv_sc_w"], name="short_conv")
    if late is not None:
        p = dict(p, **late(yc))
    merged = _merge_fwd(ya, o_sb, yc, p["w_branch"], proj, name="merge")
    u = _matmul(merged, p["w_out"], name="proj_out", tm=512, tn=1024)
    x1 = _resnorm_fwd(x0, u, p["norm_mix_post"], name="norm_mix_post")
    h2 = _norm_fwd(x1, p["norm_ffn_pre"], name="norm_ffn_pre")
    a, r = _matmul(h2, p["w_ff1"], name="ff1", tm=512, tn=1024, outs=(BF16, BF16), epi=_relu2_epi, b_chips=True)
    f = _matmul(r, p["w_ff2"], name="ff2", tm=1024, tn=1024, tk=2048)
    x2 = _resnorm_fwd(x1, f, p["norm_ffn_post"], name="norm_ffn_post")
    saved = dict(x0=x0, h1=h1, proj=proj, qkvn=qkvn, gb=gb, gbt=gbt, o_gdn=o_gdn, states=states, ya=ya, o_sb=o_sb,
                 sb_tot=sb_tot, yc=yc, merged=merged, u=u, x1=x1, h2=h2, a=a, r=r, f=f)
    return x2, saved, p


def _layer_bwd(dx2, p, sv, early=None):
    g = {}
    df, g["norm_ffn_post"] = _norm_bwd(sv["f"], p["norm_ffn_post"], dx2, None, out_dtype=BF16, name="norm_ffn_post_bwd")
    g["w_ff2"] = _matmul(sv["r"], df, ta=True, name="ff2_dw", tm=1024, tn=1024, tk=2048)
    da = _matmul(df, p["w_ff2"], tb=True, name="ff2_dx", tm=512, tn=1024, outs=(BF16,), epi=_drelu2_epi,
                 extras=(sv["a"],))
    g["w_ff1"] = _matmul(sv["h2"], da, ta=True, name="ff1_dw", tm=1024, tn=1024, tk=2048, out_chips=True)
    dh2 = _matmul(da, p["w_ff1"], tb=True, name="ff1_dx", tm=1024, tn=1024, tk=1024, b_chips=True)
    dx1, g["norm_ffn_pre"] = _norm_bwd(sv["x1"], p["norm_ffn_pre"], dh2, dx2, out_dtype=F32, name="norm_ffn_pre_bwd")
    du, g["norm_mix_post"] = _norm_bwd(sv["u"], p["norm_mix_post"], dx1, None, out_dtype=BF16, name="norm_mix_post_bwd")
    g["w_out"] = _matmul(sv["merged"], du, ta=True, name="out_dw", tm=1024, tn=1024, tk=2048)
    dmerged = _matmul(du, p["w_out"], tb=True, name="out_dx", tm=512, tn=1024, outs=(BF16,))
    ys = (sv["ya"], sv["o_sb"], sv["yc"])
    dz, dgates = _merge_bwd(*ys, p["w_branch"], sv["proj"], dmerged, name="merge_bwd")
    g["w_branch"] = jnp.stack([_matmul(ys[b], dz[b], ta=True, name=f"branch_dw{b}", tm=512, tn=256, tk=1024, out_chips=True)
                               for b in range(N_BRANCH)], axis=1)
    dys = [_matmul(dz[b], p["w_branch"][:, b], tb=True, name=f"branch_dx{b}", tm=1024, tn=512, tk=256, b_chips=True)
           for b in range(N_BRANCH)]
    conv_sc_w = p["conv_sc_w"]
    if early is not None:
        conv_sc_w = conv_sc_w + early({k: g[k] for k in ("w_branch", "w_out", "w_ff1", "w_ff2")})
    dscx, dscb, dscc, g["conv_sc_w"] = _sc_bwd(sv["proj"], conv_sc_w, dys[2], name="short_conv_bwd")
    dsq, dsk, dsv = _sb_bwd(sv["proj"], sv["sb_tot"], dys[1], name="sb_attn_bwd")
    do_gdn, dgate, dnw = _gdn_post_bwd(sv["o_gdn"], sv["proj"], p["gdn_norm_w"], dys[0], name="gdn_post_bwd")
    g["gdn_norm_w"] = dnw
    dqkvn, dgb = _gdn_chunk_bwd(sv["qkvn"], sv["gb"], sv["gbt"], sv["states"], do_gdn, name="gdn_chunk_bwd")
    dqkv, g["conv_qkv_w"] = _gdn_pre_bwd(sv["proj"], p["conv_qkv_w"], dqkvn, name="gdn_pre_bwd")
    dab, g["gdn_a_log"], g["gdn_dt_bias"] = _gdn_gates_bwd(sv["proj"], p["gdn_a_log"], p["gdn_dt_bias"], dgb,
                                                           name="gdn_gates_bwd")
    dproj = jnp.concatenate([*dgates, dqkv, dgate, dab, dsq, dsk, dsv, dscx, dscb, dscc], axis=1)
    g["w_in"] = _matmul(sv["h1"], dproj, ta=True, name="in_dw", tm=1024, tn=1664, tk=1024)
    dh1 = _matmul(dproj, p["w_in"], tb=True, name="in_dx", tm=1024, tn=1024, tk=1664)
    dx0, g["norm_mix_pre"] = _norm_bwd(sv["x0"], p["norm_mix_pre"], dh1, dx1, out_dtype=F32, name="norm_mix_pre_bwd")
    return dx0, g


def _local_step(x, target, n_layers, weights_of, grads_done, grads_early=None):
    saved, layers = [], []
    h = x
    for l in range(n_layers):
        p, late = weights_of(l, h)
        h, sv, p = _layer_fwd(h, p, late)
        saved.append(sv)
        layers.append(p)
    loss, dh = _loss_fwd_bwd(h, target, name="loss")
    for l in reversed(range(n_layers)):
        dh, g = _layer_bwd(dh, layers[l], saved[l], grads_early if l == 0 else None)
        zero = grads_done(l, g)
        if l > 0:
            layers[l - 1] = dict(layers[l - 1], norm_ffn_post=layers[l - 1]["norm_ffn_post"] + zero)
    return loss, dh


ANY = pl.BlockSpec(memory_space=pl.ANY)


def _me_and_chips():
    x, y, c = lax.axis_index("x"), lax.axis_index("y"), lax.axis_index("c")
    chips = [(1 - x, y), (x, 1 - y), (1 - x, 1 - y)]
    return x, y, c, chips


def _gather_devices(small, *, name):
    def body(small_ref, small_out, ssend, srecv, local_sem):
        x, y, c, chips = _me_and_chips()
        dev = 4 * x + 2 * y + c
        lc = pltpu.make_async_copy(small_ref, small_out.at[dev], local_sem)
        lc.start()
        peers = [(x, y, 1 - c)] + [(px, py, pc) for (px, py) in chips for pc in (c, 1 - c)]
        sends = []
        for k, peer in enumerate(peers):
            cp = pltpu.make_async_remote_copy(src_ref=small_ref, dst_ref=small_out.at[dev], send_sem=ssend.at[k],
                                              recv_sem=srecv.at[k], device_id=peer, device_id_type=MESH)
            cp.start()
            sends.append(cp)
        for k, (px, py, pc) in enumerate(peers):
            pltpu.make_async_remote_copy(src_ref=small_ref, dst_ref=small_out.at[4 * px + 2 * py + pc], send_sem=ssend.at[k],
                                         recv_sem=srecv.at[k], device_id=(px, py, pc), device_id_type=MESH).wait_recv()
        for cp in sends:
            cp.wait_send()
        lc.wait()

    return pl.pallas_call(
        body, name=name, in_specs=[ANY], out_specs=ANY,
        out_shape=jax.ShapeDtypeStruct((N_DEV,) + small.shape, small.dtype),
        scratch_shapes=[pltpu.SemaphoreType.DMA((N_DEV - 1,)), pltpu.SemaphoreType.DMA((N_DEV - 1,)), pltpu.SemaphoreType.DMA],
    )(small)


HBM = pl.BlockSpec(memory_space=pltpu.HBM)
SEM = pl.BlockSpec(memory_space=pltpu.SEMAPHORE)
EFFECT = pltpu.SideEffectType.DATAFLOW_SIDE_EFFECTING


def _exchange_start(srcs, *, by_slot, name, after=None):
    n = len(srcs)
    n_in = 2 * n + (after is not None)
    land_shapes = [a.shape if by_slot else (N_CHIPS,) + a.shape for a in srcs]
    lands = [pltpu.with_memory_space_constraint(lax.empty(sh, a.dtype), pltpu.HBM) for sh, a in zip(land_shapes, srcs)]
    srcs = [pltpu.with_memory_space_constraint(a, pltpu.HBM) for a in srcs]

    def body(*refs):
        ins, land = refs[:n], refs[n:2 * n]
        send_sems, recv_sems, token = refs[n_in], refs[n_in + 1], refs[-1]
        x, y, c, chips = _me_and_chips()
        me = 2 * x + y
        for a in range(n):
            for k, (px, py) in enumerate(chips):
                pltpu.make_async_remote_copy(
                    src_ref=ins[a].at[2 * px + py] if by_slot else ins[a], dst_ref=land[a].at[me],
                    send_sem=send_sems.at[3 * a + k], recv_sem=recv_sems.at[3 * a + k], device_id=(px, py, c),
                    device_id_type=MESH).start()
        token[...] = jnp.zeros_like(token)

    res = pl.pallas_call(
        body, name=name, in_specs=[HBM] * (2 * n) + ([ANY] if after is not None else []),
        out_specs=[SEM, SEM] + [HBM] * (2 * n) + [pl.BlockSpec(memory_space=pltpu.VMEM)],
        out_shape=[pltpu.SemaphoreType.DMA((3 * n,)), pltpu.SemaphoreType.DMA((3 * n,))]
        + [pltpu.HBM(a.shape, a.dtype) for a in srcs] + [pltpu.HBM(sh, a.dtype) for sh, a in zip(land_shapes, srcs)]
        + [jax.ShapeDtypeStruct((8, LANES), F32)],
        input_output_aliases={i: 2 + i for i in range(2 * n)},
        compiler_params=pltpu.CompilerParams(has_side_effects=EFFECT),
    )(*srcs, *lands, *([after] if after is not None else []))
    return dict(send=res[0], recv=res[1], srcs=res[2:2 + n], lands=res[2 + n:2 + 2 * n], token=res[-1])


def _exchange_wait(ex, after, *, by_slot, name):
    n = len(ex["srcs"])

    def body(*refs):
        ins, land = refs[:n], refs[n:2 * n]
        send_sems, recv_sems = refs[2 * n], refs[2 * n + 1]
        x, y, c, chips = _me_and_chips()
        me = 2 * x + y
        for a in range(n):
            for k, (px, py) in enumerate(chips):
                cp = pltpu.make_async_remote_copy(
                    src_ref=ins[a].at[me] if by_slot else ins[a], dst_ref=land[a].at[2 * px + py],
                    send_sem=send_sems.at[3 * a + k], recv_sem=recv_sems.at[3 * a + k], device_id=(px, py, c),
                    device_id_type=MESH)
                cp.wait_send()
                cp.wait_recv()

    res = pl.pallas_call(
        body, name=name, in_specs=[HBM] * (2 * n) + [SEM, SEM, ANY], out_specs=[HBM] * (2 * n),
        out_shape=[pltpu.HBM(a.shape, a.dtype) for a in ex["srcs"]] + [pltpu.HBM(a.shape, a.dtype) for a in ex["lands"]],
        input_output_aliases={i: i for i in range(2 * n)},
        compiler_params=pltpu.CompilerParams(has_side_effects=EFFECT),
    )(*ex["srcs"], *ex["lands"], ex["send"], ex["recv"], after)
    return res[:n], res[n:]


def _chip_index():
    return 2 * lax.axis_index("x") + lax.axis_index("y")


def _me_operand():
    return jnp.reshape(_chip_index(), (1,)).astype(jnp.int32)


def _place_own(land, own, *, name):
    rows, cols = _as2d(own).shape
    tr = _row_tile(rows, cols)

    def body(me_ref, own_ref, land_ref, out_ref):
        out_ref[...] = own_ref[...]

    res = pl.pallas_call(
        body, name=name,
        grid_spec=pltpu.PrefetchScalarGridSpec(
            num_scalar_prefetch=1, grid=(rows // tr,),
            in_specs=[pl.BlockSpec((tr, cols), lambda i, me: (i, 0)), ANY],
            out_specs=pl.BlockSpec((None, tr, cols), lambda i, me: (me[0], i, 0))),
        out_shape=jax.ShapeDtypeStruct((N_CHIPS, rows, cols), land.dtype), input_output_aliases={2: 0},
        compiler_params=_params("arbitrary"),
    )(_me_operand(), _as2d(own), land.reshape(N_CHIPS, rows, cols))
    return res.reshape(land.shape)


def _sum_partials(lands, parts, *, name):
    n, rows, cols = lands.shape
    tr = _row_tile(rows, cols, 1024 * 1024)

    def body(me_ref, land_ref, own_ref, o_ref):
        me = me_ref[0]
        acc = None
        for i in range(n):
            term = jnp.where(me == i, own_ref[...], land_ref[i]).astype(F32)
            acc = term if acc is None else acc + term
        o_ref[...] = acc

    return pl.pallas_call(
        body, name=name,
        grid_spec=pltpu.PrefetchScalarGridSpec(
            num_scalar_prefetch=1, grid=(rows // tr,),
            in_specs=[pl.BlockSpec((n, tr, cols), lambda i, me: (0, i, 0)),
                      pl.BlockSpec((None, tr, cols), lambda i, me: (me[0], i, 0))],
            out_specs=pl.BlockSpec((tr, cols), lambda i, me: (i, 0))),
        out_shape=jax.ShapeDtypeStruct((rows, cols), F32), compiler_params=_params("arbitrary"),
    )(_me_operand(), lands, parts)


def _swap_sibling(arrs, *, name):
    n = len(arrs)

    def body(*refs):
        ins, outs = refs[:n], refs[n:2 * n]
        send_sems, recv_sems = refs[2 * n:]
        x, y, c = lax.axis_index("x"), lax.axis_index("y"), lax.axis_index("c")
        cps = [pltpu.make_async_remote_copy(src_ref=ins[a], dst_ref=outs[a], send_sem=send_sems.at[a],
                                            recv_sem=recv_sems.at[a], device_id=(x, y, 1 - c), device_id_type=MESH)
               for a in range(n)]
        for cp in cps:
            cp.start()
        for cp in cps:
            cp.wait()

    return pl.pallas_call(
        body, name=name, in_specs=[ANY] * n, out_specs=[ANY] * n,
        out_shape=[jax.ShapeDtypeStruct(a.shape, a.dtype) for a in arrs],
        scratch_shapes=[pltpu.SemaphoreType.DMA((n,)), pltpu.SemaphoreType.DMA((n,))],
    )(*arrs)


def _row_tile(rows, cols, budget=2 * 1024 * 1024):
    best = None
    for t in range(16, rows + 1, 16):
        if rows % t == 0 and t * cols * 4 <= budget:
            best = t
    return best if best is not None else rows


def _sum_slots(parts, *, name):
    n, rows, cols = parts.shape
    tr = _row_tile(rows, cols, 1024 * 1024)

    def body(p_ref, o_ref):
        acc = p_ref[0].astype(F32)
        for i in range(1, n):
            acc = acc + p_ref[i].astype(F32)
        o_ref[...] = acc

    return pl.pallas_call(
        body, name=name, grid=(rows // tr,), in_specs=[pl.BlockSpec((n, tr, cols), lambda i: (0, i, 0))],
        out_specs=pl.BlockSpec((tr, cols), lambda i: (i, 0)), out_shape=jax.ShapeDtypeStruct((rows, cols), F32),
        compiler_params=_params("parallel"),
    )(parts)


def _adamw(w, m, v, g_a, g_b, *, name):
    rows, cols = w.shape
    tr = _row_tile(rows, cols, 1024 * 1024)
    two = g_b is not None
    c1 = 1.0 / (1.0 - ADAM_B1 ** ADAM_STEP)
    c2 = 1.0 / (1.0 - ADAM_B2 ** ADAM_STEP)

    def body(*refs):
        w_ref, m_ref, v_ref, ga_ref = refs[:4]
        g_ref, d_ref, nm_ref, nv_ref = refs[4 + two:]
        g = ga_ref[...]
        if two:
            g = g + refs[4][...]
        nm = ADAM_B1 * m_ref[...] + (1.0 - ADAM_B1) * g
        nv = ADAM_B2 * v_ref[...] + (1.0 - ADAM_B2) * (g * g)
        g_ref[...] = g
        nm_ref[...] = nm
        nv_ref[...] = nv
        d_ref[...] = -ADAM_LR * ((nm * c1) / (jnp.sqrt(nv * c2) + ADAM_EPS) + ADAM_WD * w_ref[...])

    blk = pl.BlockSpec((tr, cols), lambda i: (i, 0))
    ins = [w, m, v, g_a] + ([g_b] if two else [])
    return pl.pallas_call(
        body, name=name, grid=(rows // tr,), in_specs=[blk] * len(ins), out_specs=[blk] * 4,
        out_shape=[jax.ShapeDtypeStruct((rows, cols), F32)] * 4, compiler_params=_params("parallel"),
    )(*ins)


def _cast_bf16(w, *, name):
    rows, cols = w.shape
    tr = _row_tile(rows, cols)

    def body(w_ref, o_ref):
        o_ref[...] = w_ref[...].astype(BF16)

    blk = pl.BlockSpec((tr, cols), lambda i: (i, 0))
    return pl.pallas_call(body, name=name, grid=(rows // tr,), in_specs=[blk], out_specs=blk,
                          out_shape=jax.ShapeDtypeStruct((rows, cols), BF16), compiler_params=_params("parallel"))(w)


BIG = ("w_in", "w_branch", "w_out", "w_ff1", "w_ff2")
SMALL = ("norm_mix_pre", "conv_qkv_w", "gdn_a_log", "gdn_dt_bias", "gdn_norm_w", "conv_sc_w", "norm_mix_post",
         "norm_ffn_pre", "norm_ffn_post")
ORDER = ("norm_mix_pre", "w_in", "conv_qkv_w", "gdn_a_log", "gdn_dt_bias", "gdn_norm_w", "conv_sc_w", "w_branch",
         "w_out", "norm_mix_post", "norm_ffn_pre", "w_ff1", "w_ff2", "norm_ffn_post")


_MATMUL_LAYOUT = dict(
    w_in=_in_cols_from_chips,
    w_branch=lambda a: a,
    w_out=lambda a: a.reshape(D_MODEL, D_MODEL),
    w_ff1=lambda a: a,
    w_ff2=lambda a: a.reshape(D_FF, D_MODEL),
)
_SHARD_LAYOUT = dict(
    w_in=_in_cols_to_chips,
    w_branch=lambda g: g,
    w_out=lambda g: g.reshape(N_CHIPS, D_MODEL // N_CHIPS, D_MODEL),
    w_ff1=lambda g: g,
    w_ff2=lambda g: g.reshape(N_CHIPS, D_FF // N_CHIPS, D_MODEL),
)


def _full_weights(big, conv, rep, l):
    p = {k: _MATMUL_LAYOUT[k](a) for k, a in big.items()}
    if conv is not None:
        p["conv_qkv_w"] = conv["conv_qkv_w"][:, l].transpose(1, 0, 2).reshape(4, 3 * BRANCH_W)
        p["conv_sc_w"] = conv["conv_sc_w"][:, l].transpose(1, 0, 2).reshape(3, BRANCH_W)
    if rep is not None:
        for k in ("norm_mix_pre", "gdn_a_log", "gdn_dt_bias", "gdn_norm_w", "norm_mix_post", "norm_ffn_pre", "norm_ffn_post"):
            p[k] = rep[k][l]
    return p


def _partials_by_chip(g, names):
    return [_SHARD_LAYOUT[k](g[k]).astype(BF16) for k in names]


def _pack_small(grads):
    pieces, layout = [], []
    for name in SMALL:
        v = jnp.stack([g[name] for g in grads]).astype(F32)
        layout.append((name, v.shape))
        pieces.append(v.reshape(-1))
    flat = jnp.concatenate(pieces)
    rows = -(-flat.shape[0] // LANES)
    rows = -(-rows // 8) * 8
    flat = jnp.pad(flat, (0, rows * LANES - flat.shape[0]))
    return flat.reshape(rows, LANES), layout


def _unpack_small(table, layout):
    flat, out, off = table.reshape(-1), {}, 0
    for name, shape in layout:
        size = 1
        for d in shape:
            size *= d
        out[name] = flat[off:off + size].reshape(shape)
        off += size
    return out


def _as2d(a):
    return a.reshape(-1, a.shape[-1]) if a.ndim > 1 else a.reshape(1, -1)


def kernel(x, norm_mix_pre, w_in, conv_qkv_w, gdn_a_log, gdn_dt_bias, gdn_norm_w, conv_sc_w, w_branch, w_out, norm_mix_post, norm_ffn_pre, w_ff1, w_ff2, norm_ffn_post, loss_target, m_norm_mix_pre, m_w_in, m_conv_qkv_w, m_gdn_a_log, m_gdn_dt_bias, m_gdn_norm_w, m_conv_sc_w, m_w_branch, m_w_out, m_norm_mix_post, m_norm_ffn_pre, m_w_ff1, m_w_ff2, m_norm_ffn_post, v_norm_mix_pre, v_w_in, v_conv_qkv_w, v_gdn_a_log, v_gdn_dt_bias, v_gdn_norm_w, v_conv_sc_w, v_w_branch, v_w_out, v_norm_mix_post, v_norm_ffn_pre, v_w_ff1, v_w_ff2, v_norm_ffn_post):
    w = dict(norm_mix_pre=norm_mix_pre, w_in=w_in, conv_qkv_w=conv_qkv_w, gdn_a_log=gdn_a_log, gdn_dt_bias=gdn_dt_bias,
             gdn_norm_w=gdn_norm_w, conv_sc_w=conv_sc_w, w_branch=w_branch, w_out=w_out, norm_mix_post=norm_mix_post,
             norm_ffn_pre=norm_ffn_pre, w_ff1=w_ff1, w_ff2=w_ff2, norm_ffn_post=norm_ffn_post)
    m = dict(norm_mix_pre=m_norm_mix_pre, w_in=m_w_in, conv_qkv_w=m_conv_qkv_w, gdn_a_log=m_gdn_a_log,
             gdn_dt_bias=m_gdn_dt_bias, gdn_norm_w=m_gdn_norm_w, conv_sc_w=m_conv_sc_w, w_branch=m_w_branch, w_out=m_w_out,
             norm_mix_post=m_norm_mix_post, norm_ffn_pre=m_norm_ffn_pre, w_ff1=m_w_ff1, w_ff2=m_w_ff2,
             norm_ffn_post=m_norm_ffn_post)
    v = dict(norm_mix_pre=v_norm_mix_pre, w_in=v_w_in, conv_qkv_w=v_conv_qkv_w, gdn_a_log=v_gdn_a_log,
             gdn_dt_bias=v_gdn_dt_bias, gdn_norm_w=v_gdn_norm_w, conv_sc_w=v_conv_sc_w, w_branch=v_w_branch, w_out=v_w_out,
             norm_mix_post=v_norm_mix_post, norm_ffn_pre=v_norm_ffn_pre, w_ff1=v_w_ff1, w_ff2=v_w_ff2,
             norm_ffn_post=v_norm_ffn_post)

    me = _chip_index()

    shards = {k: _cast_bf16(_as2d(w[k]), name=f"cast_{k}").reshape(w[k].shape) for k in BIG}
    conv_names = ("conv_qkv_w", "conv_sc_w")
    FIRST, REST = ("w_in",), ("w_branch", "w_out", "w_ff1", "w_ff2")

    def gather_start(l, names, tag, after):
        srcs = [shards[k][l] for k in names] + ([w[k] for k in conv_names] if (l == 0 and "w_in" in names) else [])
        return _exchange_start(srcs, by_slot=False, name=f"gather_start{l}{tag}", after=after)

    def gather_land(ex, names, l, tag, after):
        own, lands = _exchange_wait(ex, after, by_slot=False, name=f"gather_wait{l}{tag}")
        return {k: _place_own(land, o, name=f"own_{k}") for k, land, o in zip(names, lands, own)}

    gathers = {0: gather_start(0, FIRST, "a", None)}
    conv = {}

    def weights_of(l, x_l):
        if l > 0:
            full = gather_land(gathers[l], BIG, l, "", x_l)
            p = _full_weights(full, conv, w, l)
            if l + 1 < DEPTH:
                gathers[l + 1] = gather_start(l + 1, BIG, "", full["w_out"])
                p["norm_mix_pre"] = p["norm_mix_pre"] + gathers[l + 1]["token"][0, 0]
            return p, None
        full = gather_land(gathers[0], FIRST + conv_names, 0, "a", gathers[0]["token"])
        conv.update({k: full[k] for k in conv_names})
        p = _full_weights({"w_in": full["w_in"]}, conv, w, 0)
        rest = gather_start(0, REST, "b", full["conv_sc_w"])
        p["norm_mix_pre"] = p["norm_mix_pre"] + rest["token"][0, 0]

        def late(after):
            arrived = gather_land(rest, REST, 0, "b", after)
            q = _full_weights(arrived, None, None, 0)
            gathers[1] = gather_start(1, BIG, "", arrived["w_out"])
            q["norm_mix_post"] = p["norm_mix_post"] + gathers[1]["token"][0, 0]
            return q

        return p, late

    grads, scatters = [None] * DEPTH, {}

    def scatter_start(l, g, names, tag):
        scatters[l, names] = _exchange_start(_partials_by_chip(g, names), by_slot=True, name=f"scatter_start{l}{tag}")
        return scatters[l, names]["token"][0, 0]

    def grads_early(g):
        return scatter_start(0, g, REST, "a")

    def grads_done(l, g):
        grads[l] = g
        return scatter_start(l, g, FIRST, "b") if l == 0 else scatter_start(l, g, BIG, "")

    loss, dx = _local_step(x[0], loss_target[0], DEPTH, weights_of, grads_done, grads_early)
    loss = lax.psum(loss, ("x", "y", "c"))

    sums = [dict() for _ in range(DEPTH)]
    for (l, names), ex in sorted(scatters.items(), key=lambda kv: (-kv[0][0], kv[0][1] != REST)):
        tag = "" if names == BIG else ("a" if names == REST else "b")
        parts, lands = _exchange_wait(ex, dx, by_slot=True, name=f"scatter_wait{l}{tag}")
        for k, r, o in zip(names, lands, parts):
            sums[l][k] = _sum_partials(r.reshape(N_CHIPS, -1, r.shape[-1]), o.reshape(N_CHIPS, -1, o.shape[-1]), name=f"sum_{k}")
    mine = [jnp.concatenate([sums[l][k] for l in range(DEPTH)], axis=0) for k in BIG]
    theirs = _swap_sibling(mine, name="swap_sibling")
    small, layout = _pack_small(grads)
    small_g = _unpack_small(_sum_slots(_gather_devices(small, name="gather_small"), name="sum_small"), layout)
    for k, width in (("conv_qkv_w", 3 * BRANCH_W // N_CHIPS), ("conv_sc_w", BRANCH_W // N_CHIPS)):
        small_g[k] = lax.dynamic_slice_in_dim(small_g[k], me * width, width, axis=2)

    out = {}
    for k, s_mine, s_theirs in zip(BIG, mine, theirs):
        res = _adamw(_as2d(w[k]), _as2d(m[k]), _as2d(v[k]), s_mine, s_theirs, name=f"adamw_{k}")
        out[k] = [r.reshape(w[k].shape) for r in res]
    for k in SMALL:
        res = _adamw(_as2d(w[k]), _as2d(m[k]), _as2d(v[k]), _as2d(small_g[k]), None, name=f"adamw_{k}")
        out[k] = [r.reshape(w[k].shape) for r in res]
    return (loss, dx[None], *[out[k][0] for k in ORDER], *[out[k][1] for k in ORDER], *[out[k][2] for k in ORDER],
            *[out[k][3] for k in ORDER])
```

```python
import functools

import jax
import jax.numpy as jnp
from jax import lax
from jax.experimental import pallas as pl
from jax.experimental.pallas import tpu as pltpu

F32 = jnp.float32
BF16 = jnp.bfloat16
MESH = pl.DeviceIdType.MESH

LANES = 128
D_MODEL = 1024
DEPTH = 4
CHUNK = 64
GDN_HEADS, GDN_DIM = 4, 128
SB_HEADS, SB_DIM = 8, 64
BRANCH_W = 512
N_BRANCH = 3
D_FF = 4 * D_MODEL
EPS = 1e-6
IN_W = 8200
AB_COL = 2048
AB_PAD = LANES - 8
IN_WP = IN_W + AB_PAD
N_CHIPS = 4
N_DEV = 8
GATES_COL = 5128
PB_GATES, PB_QKV, PB_GATE, PB_AB, PB_SB, PB_SCX, PB_SCB, PB_SCC = 0, 24, 36, 40, 41, 53, 57, 61
SB_TILE = 128
SB_GROUP = 4
SB_QTILE = 256
SB_SCALE = SB_DIM ** -0.5
SB_SCALE2 = SB_SCALE * 1.4426950408889634
GDN_QSCALE = GDN_DIM ** -0.5
VMEM_LIMIT = 56 * 1024 * 1024

ADAM_LR, ADAM_B1, ADAM_B2, ADAM_EPS, ADAM_WD, ADAM_STEP = 0.001, 0.9, 0.999, 1e-08, 0.01, 10

NT = (((1,), (1,)), ((), ()))
TN = (((0,), (0,)), ((), ()))
HI = lax.Precision.HIGH


def _pad_in_cols(w):
    return jnp.concatenate([w[:, GATES_COL:], w[:, :AB_COL + 8], jnp.zeros((w.shape[0], AB_PAD), w.dtype),
                            w[:, AB_COL + 8:GATES_COL]], axis=1)


def _unpad_in_cols(g):
    n_gates = IN_W - GATES_COL
    return jnp.concatenate([g[:, n_gates:n_gates + AB_COL + 8], g[:, n_gates + AB_COL + 8 + AB_PAD:], g[:, :n_gates]], axis=1)


IN_SHARD = IN_W // N_CHIPS
_IN_SEGMENTS = ((0, AB_COL + 8, IN_W - GATES_COL), (AB_COL + 8, GATES_COL, IN_W - GATES_COL + AB_PAD),
                (GATES_COL, IN_W, -GATES_COL))


def _in_cols_from_chips(slots):
    def cols(first, last):
        out = []
        for j in range(N_CHIPS):
            lo, hi = max(first, j * IN_SHARD), min(last, (j + 1) * IN_SHARD)
            if lo < hi:
                out.append(slots[j][:, lo - j * IN_SHARD:hi - j * IN_SHARD])
        return out

    head, tail, gates = (cols(first, last) for first, last, _ in _IN_SEGMENTS)
    return jnp.concatenate(gates + head + [jnp.zeros((slots.shape[1], AB_PAD), slots.dtype)] + tail, axis=1)


def _in_cols_to_chips(g):
    shards = []
    for j in range(N_CHIPS):
        pieces = []
        for first, last, shift in _IN_SEGMENTS:
            lo, hi = max(first, j * IN_SHARD), min(last, (j + 1) * IN_SHARD)
            if lo < hi:
                pieces.append(g[:, lo + shift:hi + shift])
        shards.append(jnp.concatenate(pieces, axis=1))
    return jnp.stack(shards)


def _params(*sem):
    return pltpu.CompilerParams(dimension_semantics=sem if sem else None, vmem_limit_bytes=VMEM_LIMIT)


def _sigmoid(x):
    return 1.0 / (1.0 + jnp.exp(-x))


def _softplus(x):
    return jnp.maximum(x, 0.0) + jnp.log(1.0 + jnp.exp(-jnp.abs(x)))


def _softplus2(x):
    return jnp.maximum(x, 0.0) + jnp.log2(1.0 + jnp.exp2(-jnp.abs(x)))


def _dot(a, b, dims=None, precision=None):
    if dims is None:
        return jnp.dot(a, b, preferred_element_type=F32, precision=precision)
    return lax.dot_general(a, b, dims, preferred_element_type=F32, precision=precision)


def _bdot(a, b, dims=None):
    return _dot(a.astype(BF16), b.astype(BF16), dims)


def _matmul(a, b, *, name, ta=False, tb=False, tm, tn, tk=None, outs=(F32,), epi=None, extras=(), b_chips=False,
            out_chips=False):
    if ta:
        kdim, m = a.shape
    else:
        m, kdim = a.shape
    if b_chips:
        per = b.shape[2]
        if tb:
            n, kb = b.shape[1], N_CHIPS * per
        else:
            kb, n = b.shape[1], N_CHIPS * per
    elif tb:
        n, kb = b.shape
    else:
        kb, n = b.shape
    assert kdim == kb, (a.shape, b.shape)
    tk = kdim if tk is None else min(tk, kdim)
    tm = min(tm, m)
    assert m % tm == 0 and n % tn == 0 and kdim % tk == 0, (m, n, kdim, tm, tn, tk)
    nk = kdim // tk
    ni, nj = m // tm, n // tn
    a_bytes, b_bytes = a.size * a.dtype.itemsize, b.size * b.dtype.itemsize
    cols_inner = a_bytes * (nj if nk > 1 else 1) + b_bytes * (ni if nj * nk > 1 else 1)
    rows_inner = b_bytes * (ni if nk > 1 else 1) + a_bytes * (nj if ni * nk > 1 else 1)
    swap = rows_inner < cols_inner

    def ix(f):
        return (lambda g0, g1, k: f(g1, g0, k)) if swap else f

    a_spec = pl.BlockSpec((tk, tm), ix(lambda i, j, k: (k, i))) if ta else pl.BlockSpec((tm, tk), ix(lambda i, j, k: (i, k)))
    if b_chips and tb:
        assert per % tk == 0
        b_spec = pl.BlockSpec((None, tn, tk), ix(lambda i, j, k: (k // (per // tk), j, k % (per // tk))))
    elif b_chips:
        assert per % tn == 0
        b_spec = pl.BlockSpec((None, tk, tn), ix(lambda i, j, k: (j // (per // tn), k, j % (per // tn))))
    elif tb:
        b_spec = pl.BlockSpec((tn, tk), ix(lambda i, j, k: (j, k)))
    else:
        b_spec = pl.BlockSpec((tk, tn), ix(lambda i, j, k: (k, j)))
    mn_spec = pl.BlockSpec((tm, tn), ix(lambda i, j, k: (i, j)))
    if out_chips:
        per_o = n // N_CHIPS
        assert per_o % tn == 0
        out_spec = pl.BlockSpec((None, tm, tn), ix(lambda i, j, k: (j // (per_o // tn), i, j % (per_o // tn))))
        out_dims = (N_CHIPS, m, per_o)
    else:
        out_spec, out_dims = mn_spec, (m, n)
    dims = (((0 if ta else 1,), (1 if tb else 0,)), ((), ()))
    n_ex, n_out = len(extras), len(outs)

    def body(a_ref, b_ref, *rest):
        ex, o, acc = rest[:n_ex], rest[n_ex:n_ex + n_out], rest[n_ex + n_out:]
        part = lax.dot_general(a_ref[...].astype(BF16), b_ref[...].astype(BF16), dims, preferred_element_type=F32)

        def finish(val):
            res = epi(val, *[e[...] for e in ex]) if epi is not None else (val,)
            for r, oref in zip(res, o):
                oref[...] = r.astype(oref.dtype)

        if nk == 1:
            finish(part)
        else:
            k = pl.program_id(2)

            @pl.when(k == 0)
            def _():
                acc[0][...] = part

            @pl.when(k > 0)
            def _():
                acc[0][...] += part

            @pl.when(k == nk - 1)
            def _():
                finish(acc[0][...])

    res = pl.pallas_call(
        body, name=name, grid=(nj, ni, nk) if swap else (ni, nj, nk),
        in_specs=[a_spec, b_spec] + [mn_spec] * n_ex,
        out_specs=[out_spec] * n_out,
        out_shape=[jax.ShapeDtypeStruct(out_dims, dt) for dt in outs],
        scratch_shapes=[pltpu.VMEM((tm, tn), F32)] if nk > 1 else [],
        compiler_params=_params("parallel", "parallel", "arbitrary"),
    )(a, b, *extras)
    return res[0] if n_out == 1 else res


ROW_TILE = 512


def _norm_fwd(x, w, *, name):
    s, d = x.shape

    def body(x_ref, w_ref, o_ref):
        xv = x_ref[...]
        r = lax.rsqrt(jnp.mean(xv * xv, axis=-1, keepdims=True) + EPS)
        o_ref[...] = (xv * r * w_ref[...]).astype(o_ref.dtype)

    return pl.pallas_call(
        body, name=name, grid=(s // ROW_TILE,),
        in_specs=[pl.BlockSpec((ROW_TILE, d), lambda i: (i, 0)), pl.BlockSpec((1, d), lambda i: (0, 0))],
        out_specs=pl.BlockSpec((ROW_TILE, d), lambda i: (i, 0)),
        out_shape=jax.ShapeDtypeStruct((s, d), BF16), compiler_params=_params("parallel"),
    )(x, w.reshape(1, d))


def _resnorm_fwd(x, u, w, *, name):
    s, d = x.shape

    def body(x_ref, u_ref, w_ref, o_ref):
        uv = u_ref[...]
        r = lax.rsqrt(jnp.mean(uv * uv, axis=-1, keepdims=True) + EPS)
        o_ref[...] = x_ref[...] + uv * r * w_ref[...]

    row = pl.BlockSpec((ROW_TILE, d), lambda i: (i, 0))
    return pl.pallas_call(
        body, name=name, grid=(s // ROW_TILE,),
        in_specs=[row, row, pl.BlockSpec((1, d), lambda i: (0, 0))], out_specs=row,
        out_shape=jax.ShapeDtypeStruct((s, d), F32), compiler_params=_params("parallel"),
    )(x, u, w.reshape(1, d))


def _norm_bwd(xin, w, dy, res, *, out_dtype, name):
    s, d = xin.shape
    has_res = res is not None

    def body(*refs):
        x_ref, w_ref, dy_ref = refs[:3]
        res_ref = refs[3] if has_res else None
        dx_ref, dw_ref = refs[3 + has_res:]
        xv, dyv = x_ref[...], dy_ref[...].astype(F32)
        r = lax.rsqrt(jnp.mean(xv * xv, axis=-1, keepdims=True) + EPS)
        xh = xv * r
        g = dyv * w_ref[...]
        dx = r * (g - xh * jnp.mean(g * xh, axis=-1, keepdims=True))
        if has_res:
            dx = dx + res_ref[...]
        dx_ref[...] = dx.astype(dx_ref.dtype)

        @pl.when(pl.program_id(0) == 0)
        def _():
            dw_ref[...] = jnp.zeros_like(dw_ref)

        dw_ref[...] += jnp.sum(dyv * xh, axis=0, keepdims=True)

    row = pl.BlockSpec((ROW_TILE, d), lambda i: (i, 0))
    vec = pl.BlockSpec((1, d), lambda i: (0, 0))
    ins = [xin, w.reshape(1, d), dy] + ([res] if has_res else [])
    dx, dw = pl.pallas_call(
        body, name=name, grid=(s // ROW_TILE,),
        in_specs=[row, vec, row] + ([row] if has_res else []), out_specs=[row, vec],
        out_shape=[jax.ShapeDtypeStruct((s, d), out_dtype), jax.ShapeDtypeStruct((1, d), F32)],
        compiler_params=_params("arbitrary"),
    )(*ins)
    return dx, dw.reshape(d)


def _loss_fwd_bwd(y, target, *, name):
    s, d = y.shape

    def body(y_ref, t_ref, loss_ref, dy_ref):
        e = y_ref[...] - t_ref[...]
        dy_ref[...] = e * (1.0 / d)

        @pl.when(pl.program_id(0) == 0)
        def _():
            loss_ref[...] = jnp.zeros_like(loss_ref)

        part = jnp.sum(jnp.sum(e * e, axis=1, keepdims=True), axis=0, keepdims=True)
        loss_ref[...] += part * (0.5 / d)

    row = pl.BlockSpec((ROW_TILE, d), lambda i: (i, 0))
    loss, dy = pl.pallas_call(
        body, name=name, grid=(s // ROW_TILE,), in_specs=[row, row],
        out_specs=[pl.BlockSpec((1, 1), lambda i: (0, 0)), row],
        out_shape=[jax.ShapeDtypeStruct((1, 1), F32), jax.ShapeDtypeStruct((s, d), F32)],
        compiler_params=_params("arbitrary"),
    )(y, target)
    return loss[0, 0], dy


def _shift_down(x, k, rows):
    if k == 0:
        return x
    return jnp.where(rows >= k, pltpu.roll(x, k, 0), 0.0)


def _shift_up(x, k, rows):
    if k == 0:
        return x
    n = x.shape[0]
    return jnp.where(rows < n - k, pltpu.roll(x, n - k, 0), 0.0)


def _col_spec(s, base):
    return pl.BlockSpec((s, LANES), lambda j: (0, base + j))


def _gdn_pre_math(x, w, j, rows):
    taps = w.shape[0]
    c = w[taps - 1:taps, :] * x
    for i in range(taps - 1):
        c = c + w[i:i + 1, :] * _shift_down(x, taps - 1 - i, rows)
    sg = _sigmoid(c)
    y = c * sg
    r = lax.rsqrt(jnp.sum(y * y, axis=-1, keepdims=True) + EPS)
    is_qk = j < 2 * GDN_HEADS
    scale = jnp.where(j < GDN_HEADS, GDN_QSCALE, 1.0)
    return c, sg, y, r, is_qk, scale


def _gdn_pre_fwd(proj, conv_w, *, name):
    s = proj.shape[0]

    def body(x_ref, w_ref, o_ref):
        j = pl.program_id(0)
        rows = lax.broadcasted_iota(jnp.int32, (s, LANES), 0)
        _, _, y, r, is_qk, scale = _gdn_pre_math(x_ref[...], w_ref[...], j, rows)
        o_ref[...] = jnp.where(is_qk, y * (r * scale), y)

    return pl.pallas_call(
        body, name=name, grid=(12,),
        in_specs=[_col_spec(s, PB_QKV), pl.BlockSpec((4, LANES), lambda j: (0, j))],
        out_specs=_col_spec(s, 0), out_shape=jax.ShapeDtypeStruct((s, 3 * BRANCH_W), F32),
        compiler_params=_params("parallel"),
    )(proj, conv_w)


def _gdn_pre_bwd(proj, conv_w, dqkvn, *, name):
    s = proj.shape[0]

    def body(x_ref, w_ref, d_ref, dx_ref, dw_ref):
        j = pl.program_id(0)
        rows = lax.broadcasted_iota(jnp.int32, (s, LANES), 0)
        x, w, dout = x_ref[...], w_ref[...], d_ref[...]
        c, sg, y, r, is_qk, scale = _gdn_pre_math(x, w, j, rows)
        yh = y * r
        dy_n = (scale * r) * (dout - yh * jnp.sum(dout * yh, axis=-1, keepdims=True))
        dy = jnp.where(is_qk, dy_n, dout)
        dc = dy * (sg * (1.0 + c * (1.0 - sg)))
        taps = w.shape[0]
        dx = w[taps - 1:taps, :] * dc
        dws = []
        for i in range(taps - 1):
            k = taps - 1 - i
            dx = dx + w[i:i + 1, :] * _shift_up(dc, k, rows)
            dws.append(jnp.sum(dc * _shift_down(x, k, rows), axis=0, keepdims=True))
        dws.append(jnp.sum(dc * x, axis=0, keepdims=True))
        dx_ref[...] = dx.astype(dx_ref.dtype)
        for i in range(taps):
            dw_ref[i:i + 1, :] = dws[i]

    return pl.pallas_call(
        body, name=name, grid=(12,),
        in_specs=[_col_spec(s, PB_QKV), pl.BlockSpec((4, LANES), lambda j: (0, j)), _col_spec(s, 0)],
        out_specs=[_col_spec(s, 0), pl.BlockSpec((4, LANES), lambda j: (0, j))],
        out_shape=[jax.ShapeDtypeStruct((s, 3 * BRANCH_W), BF16), jax.ShapeDtypeStruct((4, 3 * BRANCH_W), F32)],
        compiler_params=_params("parallel"),
    )(proj, conv_w, dqkvn)


def _lane_pad(v):
    return jnp.pad(v.reshape(1, -1), ((0, 0), (0, LANES - v.shape[0])))


def _gdn_gates_fwd(proj, a_log, dt_bias, *, name):
    s = proj.shape[0]

    def body(ab_ref, al_ref, dt_ref, o_ref):
        ab = ab_ref[...]
        lane = lax.broadcasted_iota(jnp.int32, (1, LANES), 1)
        g = -jnp.exp(al_ref[...]) * _softplus(ab + dt_ref[...])
        o_ref[...] = jnp.where(lane < GDN_HEADS, g, _sigmoid(ab))

    vec = pl.BlockSpec((1, LANES), lambda j: (0, 0))
    return pl.pallas_call(
        body, name=name, grid=(1,), in_specs=[_col_spec(s, PB_AB), vec, vec], out_specs=_col_spec(s, 0),
        out_shape=jax.ShapeDtypeStruct((s, LANES), F32), compiler_params=_params("arbitrary"),
    )(proj, _lane_pad(a_log), _lane_pad(dt_bias))


def _gdn_gates_bwd(proj, a_log, dt_bias, dgb, *, name):
    s = proj.shape[0]

    def body(ab_ref, al_ref, dt_ref, d_ref, dab_ref, dal_ref, ddt_ref):
        ab, d = ab_ref[...], d_ref[...]
        lane = lax.broadcasted_iota(jnp.int32, (1, LANES), 1)
        ea = jnp.exp(al_ref[...])
        pre = ab + dt_ref[...]
        g = -ea * _softplus(pre)
        dpre = d * (-ea) * _sigmoid(pre)
        beta = _sigmoid(ab)
        is_g = lane < GDN_HEADS
        dab = jnp.where(is_g, dpre, jnp.where(lane < 2 * GDN_HEADS, d * beta * (1.0 - beta), 0.0))
        dab_ref[...] = dab.astype(dab_ref.dtype)
        dal_ref[...] = jnp.sum(jnp.where(is_g, d * g, 0.0), axis=0, keepdims=True)
        ddt_ref[...] = jnp.sum(jnp.where(is_g, dpre, 0.0), axis=0, keepdims=True)

    vec = pl.BlockSpec((1, LANES), lambda j: (0, 0))
    dab, dal, ddt = pl.pallas_call(
        body, name=name, grid=(1,), in_specs=[_col_spec(s, PB_AB), vec, vec, _col_spec(s, 0)],
        out_specs=[_col_spec(s, 0), vec, vec],
        out_shape=[jax.ShapeDtypeStruct((s, LANES), BF16), jax.ShapeDtypeStruct((1, LANES), F32),
                   jax.ShapeDtypeStruct((1, LANES), F32)],
        compiler_params=_params("arbitrary"),
    )(proj, _lane_pad(a_log), _lane_pad(dt_bias), dgb)
    return dab, dal[0, :GDN_HEADS], ddt[0, :GDN_HEADS]


def _interleave(gens):
    results, live = [None] * len(gens), list(range(len(gens)))
    while live:
        for idx in list(live):
            try:
                next(gens[idx])
            except StopIteration as done:
                results[idx] = done.value
                live.remove(idx)
    return results


def _chunk_common(q, k, v, gb, gbt, h):
    c = CHUNK
    row = lax.broadcasted_iota(jnp.int32, (c, c), 0)
    col = lax.broadcasted_iota(jnp.int32, (c, c), 1)
    tril, strict, eye = row >= col, row > col, row == col
    lane = lax.broadcasted_iota(jnp.int32, (c, LANES), 1)
    sub = lax.broadcasted_iota(jnp.int32, (2 * GDN_HEADS, c), 0)
    g_col = jnp.sum(jnp.where(lane == h, gb, 0.0), axis=1, keepdims=True)
    beta_col = jnp.sum(jnp.where(lane == GDN_HEADS + h, gb, 0.0), axis=1, keepdims=True)
    g_row = jnp.sum(jnp.where(sub == h, gbt, 0.0), axis=0, keepdims=True)
    gc_col = jnp.sum(jnp.where(tril, jnp.broadcast_to(g_row, (c, c)), 0.0), axis=1, keepdims=True)
    gc_row = jnp.sum(jnp.where(row <= col, jnp.broadcast_to(g_col, (c, c)), 0.0), axis=0, keepdims=True)
    g_tot = jnp.sum(g_row, axis=1, keepdims=True)
    dm = jnp.exp(jnp.where(tril, gc_col - gc_row, -1e30))
    e_col = jnp.exp(gc_col)
    kdec_col = jnp.exp(g_tot - gc_col)
    gamma = jnp.exp(g_tot)
    kb = k * beta_col
    vb = v * beta_col
    kbg = kb * e_col
    kk = _bdot(kb, k, NT)
    qk = _bdot(q, k, NT)
    yield
    a = jnp.where(strict, kk * dm, 0.0)
    aqk = jnp.where(tril, qk * dm, 0.0)
    bneg = -a
    t = jnp.where(eye, 1.0, 0.0) + bneg
    p = _dot(bneg, bneg, precision=HI)
    yield
    for lvl in range(5):
        t_next = t + _dot(t, p, precision=HI)
        if lvl < 4:
            p = _dot(p, p, precision=HI)
        t = t_next
        yield
    u = _dot(t, vb, precision=HI)
    w = _dot(t, kbg, precision=HI)
    yield
    return dict(tril=tril, strict=strict, eye=eye, row=row, col=col, beta_col=beta_col, dm=dm, e_col=e_col,
                kdec_col=kdec_col, gamma=gamma, kb=kb, vb=vb, kbg=kbg, a=a, t=t, u=u, w=w, aqk=aqk,
                qd=q * e_col, kd=k * kdec_col)


def _gdn_chunk_fwd(qkvn, gb, gbt, *, name):
    s = qkvn.shape[0]
    n_chunks = s // CHUNK

    def body(q_ref, k_ref, v_ref, gb_ref, gbt_ref, o_ref, st_ref, state):
        @pl.when(pl.program_id(0) == 0)
        def _():
            state[...] = jnp.zeros_like(state)

        gbv, gbtv = gb_ref[...], gbt_ref[0]

        def head(h):
            hs = slice(h * GDN_DIM, (h + 1) * GDN_DIM)
            q, k, v = q_ref[:, hs], k_ref[:, hs], v_ref[:, hs]
            m = yield from _chunk_common(q, k, v, gbv, gbtv, h)
            s0 = state[h]
            st_ref[0, h] = s0
            vnew = m["u"] - _bdot(m["w"], s0)
            o_inter = _bdot(m["qd"], s0)
            yield
            o_ref[:, hs] = o_inter + _bdot(m["aqk"], vnew)
            state[h] = m["gamma"] * s0 + _bdot(m["kd"], vnew, TN)

        _interleave([head(h) for h in range(GDN_HEADS)])

    blk = lambda j: pl.BlockSpec((CHUNK, BRANCH_W), lambda n: (n, j))
    return pl.pallas_call(
        body, name=name, grid=(n_chunks,),
        in_specs=[blk(0), blk(1), blk(2), pl.BlockSpec((CHUNK, LANES), lambda n: (n, 0)),
                  pl.BlockSpec((1, 2 * GDN_HEADS, CHUNK), lambda n: (n, 0, 0))],
        out_specs=[blk(0), pl.BlockSpec((1, GDN_HEADS, GDN_DIM, GDN_DIM), lambda n: (n, 0, 0, 0))],
        out_shape=[jax.ShapeDtypeStruct((s, BRANCH_W), F32),
                   jax.ShapeDtypeStruct((n_chunks, GDN_HEADS, GDN_DIM, GDN_DIM), F32)],
        scratch_shapes=[pltpu.VMEM((GDN_HEADS, GDN_DIM, GDN_DIM), F32)],
        compiler_params=_params("arbitrary"),
    )(qkvn, qkvn, qkvn, gb, gbt)


def _gdn_chunk_bwd(qkvn, gb, gbt, states, do, *, name):
    s = qkvn.shape[0]
    n_chunks = s // CHUNK
    c = CHUNK

    def body(q_ref, k_ref, v_ref, gb_ref, gbt_ref, st_ref, do_ref, dqkv_ref, dgb_ref, dstate):
        @pl.when(pl.program_id(0) == 0)
        def _():
            dstate[...] = jnp.zeros_like(dstate)

        gbv, gbtv = gb_ref[...], gbt_ref[0]
        lane = lax.broadcasted_iota(jnp.int32, (c, LANES), 1)
        def head(h):
            hs = slice(h * GDN_DIM, (h + 1) * GDN_DIM)
            q, k, v, dov = q_ref[:, hs], k_ref[:, hs], v_ref[:, hs], do_ref[:, hs]
            m = yield from _chunk_common(q, k, v, gbv, gbtv, h)
            tril, strict, eye, row, col = m["tril"], m["strict"], m["eye"], m["row"], m["col"]
            s0, ds1 = st_ref[0, h], dstate[h]
            vnew = m["u"] - _bdot(m["w"], s0)
            dvnew_a = _bdot(m["aqk"], dov, TN) + _bdot(m["kd"], ds1)
            dqd = _bdot(dov, s0, NT)
            ds_q = _bdot(m["qd"], dov, TN)
            dgamma = jnp.sum(jnp.sum(s0 * ds1, axis=1, keepdims=True), axis=0, keepdims=True)
            yield
            dvnew = dvnew_a
            daqk = jnp.where(tril, _bdot(dov, vnew, NT), 0.0)
            dkd = _bdot(vnew, ds1, NT)
            dw = -_bdot(dvnew, s0, NT)
            dstate[h] = m["gamma"] * ds1 + ds_q - _bdot(m["w"], dvnew, TN)
            dvb = _dot(m["t"], dvnew, TN, HI)
            yield
            dt = _dot(dvnew, m["vb"], NT, HI) + _dot(dw, m["kbg"], NT, HI)
            dkbg = _dot(m["t"], dw, TN, HI)
            dmq = daqk * m["dm"]
            dq = _bdot(dmq, k) + dqd * m["e_col"]
            dk_q = _bdot(dmq, q, TN)
            yield
            tdt = _dot(m["t"], dt, TN, HI)
            yield
            da = jnp.where(strict, -_dot(tdt, m["t"], NT, HI), 0.0)
            yield
            dmat = da * m["dm"]
            dkb = _bdot(dmat, k) + dkbg * m["e_col"]
            dk = (_bdot(dmat, m["kb"], TN) + dk_q + dkd * m["kdec_col"] + m["beta_col"] * dkb)
            yield
            dbeta_col = jnp.sum(dkb * k, axis=1, keepdims=True) + jnp.sum(dvb * v, axis=1, keepdims=True)
            e = da * m["a"] + daqk * m["aqk"]
            rs_kd = jnp.sum(dkd * m["kd"], axis=1, keepdims=True)
            e_colsum = jnp.sum(e, axis=0, keepdims=True)
            e_colsum_c = jnp.sum(jnp.where(eye, jnp.broadcast_to(e_colsum, (c, c)), 0.0), axis=1, keepdims=True)
            dgc = (jnp.sum(e, axis=1, keepdims=True) - e_colsum_c + jnp.sum(dqd * m["qd"], axis=1, keepdims=True)
                   - rs_kd + jnp.sum(dkbg * m["kbg"], axis=1, keepdims=True))
            last = jnp.sum(rs_kd, axis=0, keepdims=True) + dgamma * m["gamma"]
            dgc = dgc + jnp.where(lax.broadcasted_iota(jnp.int32, (c, 1), 0) == c - 1, last, 0.0)
            dgc_row = jnp.sum(jnp.where(eye, jnp.broadcast_to(dgc, (c, c)), 0.0), axis=0, keepdims=True)
            dg_col = jnp.sum(jnp.where(col >= row, jnp.broadcast_to(dgc_row, (c, c)), 0.0), axis=1, keepdims=True)
            for part, val in enumerate((dq, dk, m["beta_col"] * dvb)):
                lo = part * BRANCH_W + h * GDN_DIM
                dqkv_ref[:, lo:lo + GDN_DIM] = val
            return jnp.where(lane == h, dg_col, 0.0) + jnp.where(lane == GDN_HEADS + h, dbeta_col, 0.0)

        parts = _interleave([head(h) for h in range(GDN_HEADS)])
        dgb_ref[...] = (parts[0] + parts[1]) + (parts[2] + parts[3])

    rev = lambda n: n_chunks - 1 - n
    blk = lambda j: pl.BlockSpec((CHUNK, BRANCH_W), lambda n: (rev(n), j))
    return pl.pallas_call(
        body, name=name, grid=(n_chunks,),
        in_specs=[blk(0), blk(1), blk(2), pl.BlockSpec((CHUNK, LANES), lambda n: (rev(n), 0)),
                  pl.BlockSpec((1, 2 * GDN_HEADS, CHUNK), lambda n: (rev(n), 0, 0)),
                  pl.BlockSpec((1, GDN_HEADS, GDN_DIM, GDN_DIM), lambda n: (rev(n), 0, 0, 0)), blk(0)],
        out_specs=[pl.BlockSpec((CHUNK, 3 * BRANCH_W), lambda n: (rev(n), 0)),
                   pl.BlockSpec((CHUNK, LANES), lambda n: (rev(n), 0))],
        out_shape=[jax.ShapeDtypeStruct((s, 3 * BRANCH_W), F32), jax.ShapeDtypeStruct((s, LANES), F32)],
        scratch_shapes=[pltpu.VMEM((GDN_HEADS, GDN_DIM, GDN_DIM), F32)],
        compiler_params=_params("arbitrary"),
    )(qkvn, qkvn, qkvn, gb, gbt, states, do)


def _gdn_post_fwd(o, proj, norm_w, *, name):
    s = o.shape[0]

    def body(o_ref, g_ref, w_ref, y_ref):
        ov, gv = o_ref[...], g_ref[...]
        r = lax.rsqrt(jnp.mean(ov * ov, axis=-1, keepdims=True) + EPS)
        y_ref[...] = (ov * r * w_ref[...] * (gv * _sigmoid(gv))).astype(y_ref.dtype)

    return pl.pallas_call(
        body, name=name, grid=(GDN_HEADS,),
        in_specs=[_col_spec(s, 0), _col_spec(s, PB_GATE), pl.BlockSpec((1, LANES), lambda j: (0, 0))],
        out_specs=_col_spec(s, 0), out_shape=jax.ShapeDtypeStruct((s, BRANCH_W), BF16),
        compiler_params=_params("parallel"),
    )(o, proj, norm_w.reshape(1, GDN_DIM))


def _gdn_post_bwd(o, proj, norm_w, dy, *, name):
    s = o.shape[0]

    def body(o_ref, g_ref, w_ref, dy_ref, do_ref, dg_ref, dw_ref):
        ov, gv, w, dyv = o_ref[...], g_ref[...], w_ref[...], dy_ref[...].astype(F32)
        r = lax.rsqrt(jnp.mean(ov * ov, axis=-1, keepdims=True) + EPS)
        oh = ov * r
        sg = _sigmoid(gv)
        silu = gv * sg
        dn = dyv * silu
        dg_ref[...] = (dyv * (oh * w) * (sg * (1.0 + gv * (1.0 - sg)))).astype(dg_ref.dtype)

        @pl.when(pl.program_id(0) == 0)
        def _():
            dw_ref[...] = jnp.zeros_like(dw_ref)

        dw_ref[...] += jnp.sum(dn * oh, axis=0, keepdims=True)
        g2 = dn * w
        do_ref[...] = r * (g2 - oh * jnp.mean(g2 * oh, axis=-1, keepdims=True))

    do, dg, dw = pl.pallas_call(
        body, name=name, grid=(GDN_HEADS,),
        in_specs=[_col_spec(s, 0), _col_spec(s, PB_GATE), pl.BlockSpec((1, LANES), lambda j: (0, 0)), _col_spec(s, 0)],
        out_specs=[_col_spec(s, 0), _col_spec(s, 0), pl.BlockSpec((1, LANES), lambda j: (0, 0))],
        out_shape=[jax.ShapeDtypeStruct((s, BRANCH_W), F32), jax.ShapeDtypeStruct((s, BRANCH_W), BF16),
                   jax.ShapeDtypeStruct((1, LANES), F32)],
        compiler_params=_params("arbitrary"),
    )(o, proj, norm_w.reshape(1, GDN_DIM), dy)
    return do, dg, dw.reshape(LANES)


def _split_terms(x):
    hi = x.astype(BF16)
    lo = (x - hi.astype(F32)).astype(BF16)
    return jnp.concatenate([hi, lo], axis=1)


def _sb_sum_matrix(pred):
    row = lax.broadcasted_iota(jnp.int32, (2 * SB_TILE, 2 * SB_TILE), 0) % SB_TILE
    col = lax.broadcasted_iota(jnp.int32, (2 * SB_TILE, 2 * SB_TILE), 1)
    return jnp.where((col >= SB_TILE) | pred(row, col), 1.0, 0.0).astype(BF16)


def _sb_head_masks():
    lane = lax.broadcasted_iota(jnp.int32, (1, LANES), 1)
    return [(lane < SB_DIM).astype(F32), (lane >= SB_DIM).astype(F32)]


def _sb_fwd(proj, *, name):
    s = proj.shape[0]
    t, tq = SB_TILE, SB_QTILE
    nq = s // tq

    def body(q_ref, k_ref, v_ref, o_ref, tot_ref):
        cmr = lax.broadcasted_iota(jnp.int32, (tq, t), 1) - lax.broadcasted_iota(jnp.int32, (tq, t), 0)
        uo = _sb_sum_matrix(lambda row, col: row > col)
        hm = _sb_head_masks()

        def qloop(i, carry0):
            qs = pl.multiple_of(i * tq, tq)
            qf = q_ref[pl.ds(qs, tq), :] * SB_SCALE2
            qh = [(qf * hm[h]).astype(BF16) for h in range(2)]
            diag = (i * tq) // (SB_GROUP * t)

            def group(g, st, masked):
                ks = pl.multiple_of(g * (SB_GROUP * t), SB_GROUP * t)
                kb = k_ref[pl.ds(ks, SB_GROUP * t), :].astype(BF16)
                vf = v_ref[pl.ds(ks, SB_GROUP * t), :]
                tiles = [(h, j) for h in range(2) for j in range(SB_GROUP)]
                z = [_dot(qh[h], kb, NT) for h in range(2)]
                keep = {j: cmr < i * tq - (g * SB_GROUP + j) * t for j in range(SB_GROUP)} if masked else None
                base, terms = {}, {}
                for h, j in tiles:
                    zj = z[h][:, j * t:(j + 1) * t]
                    sp = _softplus2(zj)
                    base[h, j] = zj - sp
                    terms[h, j] = _split_terms(jnp.where(keep[j], sp, 0.0) if masked else sp)
                sums = {hj: _dot(terms[hj], uo) for hj in tiles}
                acc, new = st[0], []
                for h in range(2):
                    run, att = st[1 + h], [None] * SB_GROUP
                    for j in reversed(range(SB_GROUP)):
                        a = jnp.exp2(base[h, j] - (sums[h, j][:, :t] + run))
                        att[j] = (jnp.where(keep[j], a, 0.0) if masked else a).astype(BF16)
                        run = run + sums[h, j][:, t:]
                    acc = acc + _dot(jnp.concatenate(att, axis=1), (vf * hm[h]).astype(BF16))
                    new.append(run)
                return (acc, *new)

            zero = jnp.zeros((tq, LANES), F32)
            st = group(diag, (zero, zero, zero), True)
            st = lax.fori_loop(0, diag, lambda jj, sv: group(diag - 1 - jj, sv, False), st)
            o_ref[pl.ds(qs, tq), :] = st[0]
            tot_ref[pl.ds(qs, tq), :] = st[1] * hm[0] + st[2] * hm[1]
            return carry0

        lax.fori_loop(0, nq, qloop, 0)

    out = jax.ShapeDtypeStruct((s, BRANCH_W), F32)
    return pl.pallas_call(
        body, name=name, grid=(SB_HEADS // 2,),
        in_specs=[_col_spec(s, PB_SB), _col_spec(s, PB_SB + 4), _col_spec(s, PB_SB + 8)],
        out_specs=[_col_spec(s, 0)] * 2, out_shape=[out] * 2,
        compiler_params=_params("parallel"),
    )(proj, proj, proj)


def _sb_bwd(proj, tot, do, *, name):
    s = proj.shape[0]
    t, tq = SB_TILE, SB_QTILE
    nq = s // tq

    def body(q_ref, k_ref, v_ref, tot_ref, do_ref, dq_ref, dk_ref, dv_ref, dk_acc, dv_acc):
        dk_acc[...] = jnp.zeros_like(dk_acc)
        dv_acc[...] = jnp.zeros_like(dv_acc)
        cmr = lax.broadcasted_iota(jnp.int32, (tq, t), 1) - lax.broadcasted_iota(jnp.int32, (tq, t), 0)
        u_le = _sb_sum_matrix(lambda row, col: row <= col)
        u_lt = _sb_sum_matrix(lambda row, col: row < col)
        hm = _sb_head_masks()

        def qloop(i, carry0):
            qs = pl.multiple_of(i * tq, tq)
            qraw = q_ref[pl.ds(qs, tq), :]
            dov = do_ref[pl.ds(qs, tq), :].astype(F32)
            totv = tot_ref[pl.ds(qs, tq), :]
            qh = [(qraw * (hm[h] * SB_SCALE2)).astype(BF16) for h in range(2)]
            q2 = jnp.concatenate([(qraw * hm[h]).astype(BF16) for h in range(2)], axis=0)
            doh = [(dov * hm[h]).astype(BF16) for h in range(2)]
            do2 = jnp.concatenate(doh, axis=0)
            tot = [jnp.max(totv * hm[h], axis=1, keepdims=True) for h in range(2)]
            diag = (i * tq) // (SB_GROUP * t)

            def group(g, st, masked):
                ks = pl.multiple_of(g * (SB_GROUP * t), SB_GROUP * t)
                kf = k_ref[pl.ds(ks, SB_GROUP * t), :]
                kb = kf.astype(BF16)
                vb = v_ref[pl.ds(ks, SB_GROUP * t), :].astype(BF16)
                tiles = [(h, j) for h in range(2) for j in range(SB_GROUP)]
                z = [_dot(qh[h], kb, NT) for h in range(2)]
                datt = [_dot(doh[h], vb, NT) for h in range(2)]
                keep = {j: cmr < i * tq - (g * SB_GROUP + j) * t for j in range(SB_GROUP)} if masked else None
                ls, lterms = {}, {}
                for h, j in tiles:
                    zj = z[h][:, j * t:(j + 1) * t]
                    sp = _softplus2(zj)
                    ls[h, j] = zj - sp
                    lterms[h, j] = _split_terms(jnp.where(keep[j], sp, 0.0) if masked else sp)
                lsum = {hj: _dot(lterms[hj], u_le) for hj in tiles}
                att, p, pterms, new_c = {}, {}, {}, []
                for h in range(2):
                    run = st[1 + h]
                    for j in range(SB_GROUP):
                        a = jnp.exp2(ls[h, j] - ((tot[h] - run) - lsum[h, j][:, :t]))
                        if masked:
                            a = jnp.where(keep[j], a, 0.0)
                        att[h, j] = a.astype(BF16)
                        p[h, j] = a * datt[h][:, j * t:(j + 1) * t]
                        pterms[h, j] = _split_terms(p[h, j])
                        run = run + lsum[h, j][:, t:]
                    new_c.append(run)
                psum = {hj: _dot(pterms[hj], u_lt) for hj in tiles}
                dzb, new_r = {}, []
                for h in range(2):
                    run = st[3 + h]
                    for j in range(SB_GROUP):
                        sig = jnp.exp2(ls[h, j])
                        dz = p[h, j] - sig * (p[h, j] + run + psum[h, j][:, :t])
                        if masked:
                            dz = jnp.where(keep[j], dz, 0.0)
                        dzb[h, j] = (dz * SB_SCALE).astype(BF16)
                        run = run + psum[h, j][:, t:]
                    new_r.append(run)
                k2 = jnp.concatenate([(kf * hm[h]).astype(BF16) for h in range(2)], axis=0)
                dq_acc = st[0] + _dot(jnp.concatenate([dzb[hj] for hj in tiles], axis=1), k2)
                for j in range(SB_GROUP):
                    rows = pl.ds(pl.multiple_of(ks + j * t, t), t)
                    dk_acc[rows, :] += _dot(jnp.concatenate([dzb[0, j], dzb[1, j]], axis=0), q2, TN)
                    dv_acc[rows, :] += _dot(jnp.concatenate([att[0, j], att[1, j]], axis=0), do2, TN)
                return (dq_acc, *new_c, *new_r)

            zero = jnp.zeros((tq, LANES), F32)
            st = lax.fori_loop(0, diag, lambda jj, sv: group(jj, sv, False), (zero,) * 5)
            st = group(diag, st, True)
            dq_ref[pl.ds(qs, tq), :] = st[0].astype(dq_ref.dtype)
            return carry0

        lax.fori_loop(0, nq, qloop, 0)
        dk_ref[...] = dk_acc[...].astype(dk_ref.dtype)
        dv_ref[...] = dv_acc[...].astype(dv_ref.dtype)

    out = jax.ShapeDtypeStruct((s, BRANCH_W), BF16)
    return pl.pallas_call(
        body, name=name, grid=(SB_HEADS // 2,),
        in_specs=[_col_spec(s, PB_SB), _col_spec(s, PB_SB + 4), _col_spec(s, PB_SB + 8), _col_spec(s, 0), _col_spec(s, 0)],
        out_specs=[_col_spec(s, 0)] * 3, out_shape=[out] * 3,
        scratch_shapes=[pltpu.VMEM((s, LANES), F32), pltpu.VMEM((s, LANES), F32)],
        compiler_params=_params("parallel"),
    )(proj, proj, proj, tot, do)


def _sc_fwd(proj, conv_w, *, name):
    s = proj.shape[0]

    def body(x_ref, b_ref, c_ref, w_ref, y_ref):
        rows = lax.broadcasted_iota(jnp.int32, (s, LANES), 0)
        w = w_ref[...]
        u = c_ref[...] * x_ref[...]
        cv = w[2:3, :] * u + w[1:2, :] * _shift_down(u, 1, rows) + w[0:1, :] * _shift_down(u, 2, rows)
        y_ref[...] = (b_ref[...] * cv).astype(y_ref.dtype)

    return pl.pallas_call(
        body, name=name, grid=(BRANCH_W // LANES,),
        in_specs=[_col_spec(s, PB_SCX), _col_spec(s, PB_SCB), _col_spec(s, PB_SCC), pl.BlockSpec((3, LANES), lambda j: (0, j))],
        out_specs=_col_spec(s, 0), out_shape=jax.ShapeDtypeStruct((s, BRANCH_W), BF16),
        compiler_params=_params("parallel"),
    )(proj, proj, proj, conv_w)


def _sc_bwd(proj, conv_w, dy, *, name):
    s = proj.shape[0]

    def body(x_ref, b_ref, c_ref, w_ref, dy_ref, dx_ref, db_ref, dc_ref, dw_ref):
        rows = lax.broadcasted_iota(jnp.int32, (s, LANES), 0)
        w, x, cg, dyv = w_ref[...], x_ref[...], c_ref[...], dy_ref[...].astype(F32)
        u = cg * x
        u1, u2 = _shift_down(u, 1, rows), _shift_down(u, 2, rows)
        cv = w[2:3, :] * u + w[1:2, :] * u1 + w[0:1, :] * u2
        db_ref[...] = (dyv * cv).astype(db_ref.dtype)
        dcv = dyv * b_ref[...]
        du = w[2:3, :] * dcv + w[1:2, :] * _shift_up(dcv, 1, rows) + w[0:1, :] * _shift_up(dcv, 2, rows)
        dx_ref[...] = (du * cg).astype(dx_ref.dtype)
        dc_ref[...] = (du * x).astype(dc_ref.dtype)
        dw_ref[0:1, :] = jnp.sum(dcv * u2, axis=0, keepdims=True)
        dw_ref[1:2, :] = jnp.sum(dcv * u1, axis=0, keepdims=True)
        dw_ref[2:3, :] = jnp.sum(dcv * u, axis=0, keepdims=True)

    out = jax.ShapeDtypeStruct((s, BRANCH_W), BF16)
    wspec = pl.BlockSpec((3, LANES), lambda j: (0, j))
    return pl.pallas_call(
        body, name=name, grid=(BRANCH_W // LANES,),
        in_specs=[_col_spec(s, PB_SCX), _col_spec(s, PB_SCB), _col_spec(s, PB_SCC), wspec, _col_spec(s, 0)],
        out_specs=[_col_spec(s, 0)] * 3 + [wspec],
        out_shape=[out] * 3 + [jax.ShapeDtypeStruct((3, BRANCH_W), F32)],
        compiler_params=_params("parallel"),
    )(proj, proj, proj, conv_w, dy)


MERGE_TM, MERGE_TN = 1024, D_MODEL // N_CHIPS


def _merge_specs():
    tm, tn = MERGE_TM, MERGE_TN
    y_spec = pl.BlockSpec((tm, BRANCH_W), lambda i, j: (i, 0))
    w_spec = pl.BlockSpec((None, N_BRANCH, BRANCH_W, tn), lambda i, j: (j, 0, 0, 0))
    gate_specs = [pl.BlockSpec((tm, tn), functools.partial(
        lambda i, j, b: (i, (PB_GATES * LANES + b * D_MODEL) // tn + j), b=b)) for b in range(N_BRANCH)]
    mn = pl.BlockSpec((tm, tn), lambda i, j: (i, j))
    return y_spec, w_spec, gate_specs, mn


def _merge_fwd(ya, yb, yc, wb, proj, *, name):
    s = ya.shape[0]
    y_spec, w_spec, gate_specs, mn = _merge_specs()

    def body(ya_ref, yb_ref, yc_ref, w_ref, g0, g1, g2, o_ref):
        acc = None
        for b, (y_ref, g_ref) in enumerate(zip((ya_ref, yb_ref, yc_ref), (g0, g1, g2))):
            term = _sigmoid(g_ref[...]) * _bdot(y_ref[...], w_ref[b])
            acc = term if acc is None else acc + term
        o_ref[...] = acc.astype(o_ref.dtype)

    return pl.pallas_call(
        body, name=name, grid=(s // MERGE_TM, D_MODEL // MERGE_TN),
        in_specs=[y_spec] * 3 + [w_spec] + gate_specs, out_specs=mn,
        out_shape=jax.ShapeDtypeStruct((s, D_MODEL), BF16), compiler_params=_params("parallel", "parallel"),
    )(ya, yb, yc, wb, proj, proj, proj)


def _merge_bwd(ya, yb, yc, wb, proj, dm, *, name):
    s = ya.shape[0]
    y_spec, w_spec, gate_specs, mn = _merge_specs()

    def body(ya_ref, yb_ref, yc_ref, w_ref, g0, g1, g2, dm_ref, *outs):
        dmv = dm_ref[...].astype(F32)
        for b, (y_ref, g_ref) in enumerate(zip((ya_ref, yb_ref, yc_ref), (g0, g1, g2))):
            sg = _sigmoid(g_ref[...])
            z = _bdot(y_ref[...], w_ref[b])
            outs[b][...] = (dmv * sg).astype(BF16)
            outs[N_BRANCH + b][...] = (dmv * z * sg * (1.0 - sg)).astype(BF16)

    out = jax.ShapeDtypeStruct((s, D_MODEL), BF16)
    res = pl.pallas_call(
        body, name=name, grid=(s // MERGE_TM, D_MODEL // MERGE_TN),
        in_specs=[y_spec] * 3 + [w_spec] + gate_specs + [mn], out_specs=[mn] * (2 * N_BRANCH),
        out_shape=[out] * (2 * N_BRANCH), compiler_params=_params("parallel", "parallel"),
    )(ya, yb, yc, wb, proj, proj, proj, dm)
    return res[:N_BRANCH], res[N_BRANCH:]


def _chunk_rows(v, s):
    return v[:, :2 * GDN_HEADS].reshape(s // CHUNK, CHUNK, 2 * GDN_HEADS).transpose(0, 2, 1)


def _relu2_epi(acc):
    r = jnp.maximum(acc, 0.0)
    return acc, r * r


def _drelu2_epi(acc, a):
    return (acc * (2.0 * jnp.maximum(a.astype(F32), 0.0)),)


def _layer_fwd(x0, p, late=None):
    s = x0.shape[0]
    h1 = _norm_fwd(x0, p["norm_mix_pre"], name="norm_mix_pre")
    proj = _matmul(h1, p["w_in"], name="proj_in", tm=512, tn=1664)
    qkvn = _gdn_pre_fwd(proj, p["conv_qkv_w"], name="gdn_pre")
    gb = _gdn_gates_fwd(proj, p["gdn_a_log"], p["gdn_dt_bias"], name="gdn_gates")
    gbt = _chunk_rows(gb, s)
    o_gdn, states = _gdn_chunk_fwd(qkvn, gb, gbt, name="gdn_chunk")
    ya = _gdn_post_fwd(o_gdn, proj, p["gdn_norm_w"], name="gdn_post")
    o_sb, sb_tot = _sb_fwd(proj, name="sb_attn")
    yc = _sc_fwd(proj, p["conv_sc_w"], name="short_conv")
    if late is not None:
        p = dict(p, **late(yc))
    merged = _merge_fwd(ya, o_sb, yc, p["w_branch"], proj, name="merge")
    u = _matmul(merged, p["w_out"], name="proj_out", tm=512, tn=1024)
    x1 = _resnorm_fwd(x0, u, p["norm_mix_post"], name="norm_mix_post")
    h2 = _norm_fwd(x1, p["norm_ffn_pre"], name="norm_ffn_pre")
    a, r = _matmul(h2, p["w_ff1"], name="ff1", tm=512, tn=1024, outs=(BF16, BF16), epi=_relu2_epi, b_chips=True)
    f = _matmul(r, p["w_ff2"], name="ff2", tm=1024, tn=1024, tk=2048)
    x2 = _resnorm_fwd(x1, f, p["norm_ffn_post"], name="norm_ffn_post")
    saved = dict(x0=x0, h1=h1, proj=proj, qkvn=qkvn, gb=gb, gbt=gbt, o_gdn=o_gdn, states=states, ya=ya, o_sb=o_sb,
                 sb_tot=sb_tot, yc=yc, merged=merged, u=u, x1=x1, h2=h2, a=a, r=r, f=f)
    return x2, saved, p


def _layer_bwd(dx2, p, sv, early=None):
    g = {}
    df, g["norm_ffn_post"] = _norm_bwd(sv["f"], p["norm_ffn_post"], dx2, None, out_dtype=BF16, name="norm_ffn_post_bwd")
    g["w_ff2"] = _matmul(sv["r"], df, ta=True, name="ff2_dw", tm=1024, tn=1024, tk=2048, outs=(BF16,))
    da = _matmul(df, p["w_ff2"], tb=True, name="ff2_dx", tm=512, tn=1024, outs=(BF16,), epi=_drelu2_epi,
                 extras=(sv["a"],))
    g["w_ff1"] = _matmul(sv["h2"], da, ta=True, name="ff1_dw", tm=1024, tn=1024, tk=2048, out_chips=True, outs=(BF16,))
    dh2 = _matmul(da, p["w_ff1"], tb=True, name="ff1_dx", tm=1024, tn=1024, tk=1024, b_chips=True)
    dx1, g["norm_ffn_pre"] = _norm_bwd(sv["x1"], p["norm_ffn_pre"], dh2, dx2, out_dtype=F32, name="norm_ffn_pre_bwd")
    du, g["norm_mix_post"] = _norm_bwd(sv["u"], p["norm_mix_post"], dx1, None, out_dtype=BF16, name="norm_mix_post_bwd")
    g["w_out"] = _matmul(sv["merged"], du, ta=True, name="out_dw", tm=1024, tn=1024, tk=2048, outs=(BF16,))
    dmerged = _matmul(du, p["w_out"], tb=True, name="out_dx", tm=512, tn=1024, outs=(BF16,))
    ys = (sv["ya"], sv["o_sb"], sv["yc"])
    dz, dgates = _merge_bwd(*ys, p["w_branch"], sv["proj"], dmerged, name="merge_bwd")
    g["w_branch"] = jnp.stack([_matmul(ys[b], dz[b], ta=True, name=f"branch_dw{b}", tm=512, tn=256, tk=1024, out_chips=True,
                                       outs=(BF16,)) for b in range(N_BRANCH)], axis=1)
    dys = [_matmul(dz[b], p["w_branch"][:, b], tb=True, name=f"branch_dx{b}", tm=1024, tn=512, tk=256, b_chips=True)
           for b in range(N_BRANCH)]
    conv_sc_w = p["conv_sc_w"]
    if early is not None:
        conv_sc_w = conv_sc_w + early({k: g[k] for k in ("w_branch", "w_out", "w_ff1", "w_ff2")})
    dscx, dscb, dscc, g["conv_sc_w"] = _sc_bwd(sv["proj"], conv_sc_w, dys[2], name="short_conv_bwd")
    dsq, dsk, dsv = _sb_bwd(sv["proj"], sv["sb_tot"], dys[1], name="sb_attn_bwd")
    do_gdn, dgate, dnw = _gdn_post_bwd(sv["o_gdn"], sv["proj"], p["gdn_norm_w"], dys[0], name="gdn_post_bwd")
    g["gdn_norm_w"] = dnw
    dqkvn, dgb = _gdn_chunk_bwd(sv["qkvn"], sv["gb"], sv["gbt"], sv["states"], do_gdn, name="gdn_chunk_bwd")
    dqkv, g["conv_qkv_w"] = _gdn_pre_bwd(sv["proj"], p["conv_qkv_w"], dqkvn, name="gdn_pre_bwd")
    dab, g["gdn_a_log"], g["gdn_dt_bias"] = _gdn_gates_bwd(sv["proj"], p["gdn_a_log"], p["gdn_dt_bias"], dgb,
                                                           name="gdn_gates_bwd")
    dproj = jnp.concatenate([*dgates, dqkv, dgate, dab, dsq, dsk, dsv, dscx, dscb, dscc], axis=1)
    g["w_in"] = _matmul(sv["h1"], dproj, ta=True, name="in_dw", tm=1024, tn=1664, tk=1024, outs=(BF16,))
    dh1 = _matmul(dproj, p["w_in"], tb=True, name="in_dx", tm=1024, tn=1024, tk=1664)
    dx0, g["norm_mix_pre"] = _norm_bwd(sv["x0"], p["norm_mix_pre"], dh1, dx1, out_dtype=F32, name="norm_mix_pre_bwd")
    return dx0, g


def _local_step(x, target, n_layers, weights_of, grads_done, grads_early=None):
    saved, layers = [], []
    h = x
    for l in range(n_layers):
        p, late = weights_of(l, h)
        h, sv, p = _layer_fwd(h, p, late)
        saved.append(sv)
        layers.append(p)
    loss, dh = _loss_fwd_bwd(h, target, name="loss")
    for l in reversed(range(n_layers)):
        dh, g = _layer_bwd(dh, layers[l], saved[l], grads_early if l == 0 else None)
        zero = grads_done(l, g)
        if l > 0:
            layers[l - 1] = dict(layers[l - 1], norm_ffn_post=layers[l - 1]["norm_ffn_post"] + zero)
    return loss, dh


ANY = pl.BlockSpec(memory_space=pl.ANY)


def _me_and_chips():
    x, y, c = lax.axis_index("x"), lax.axis_index("y"), lax.axis_index("c")
    chips = [(1 - x, y), (x, 1 - y), (1 - x, 1 - y)]
    return x, y, c, chips


def _gather_devices(small, *, name):
    def body(small_ref, small_out, ssend, srecv, local_sem):
        x, y, c, chips = _me_and_chips()
        dev = 4 * x + 2 * y + c
        lc = pltpu.make_async_copy(small_ref, small_out.at[dev], local_sem)
        lc.start()
        peers = [(x, y, 1 - c)] + [(px, py, pc) for (px, py) in chips for pc in (c, 1 - c)]
        sends = []
        for k, peer in enumerate(peers):
            cp = pltpu.make_async_remote_copy(src_ref=small_ref, dst_ref=small_out.at[dev], send_sem=ssend.at[k],
                                              recv_sem=srecv.at[k], device_id=peer, device_id_type=MESH)
            cp.start()
            sends.append(cp)
        for k, (px, py, pc) in enumerate(peers):
            pltpu.make_async_remote_copy(src_ref=small_ref, dst_ref=small_out.at[4 * px + 2 * py + pc], send_sem=ssend.at[k],
                                         recv_sem=srecv.at[k], device_id=(px, py, pc), device_id_type=MESH).wait_recv()
        for cp in sends:
            cp.wait_send()
        lc.wait()

    return pl.pallas_call(
        body, name=name, in_specs=[ANY], out_specs=ANY,
        out_shape=jax.ShapeDtypeStruct((N_DEV,) + small.shape, small.dtype),
        scratch_shapes=[pltpu.SemaphoreType.DMA((N_DEV - 1,)), pltpu.SemaphoreType.DMA((N_DEV - 1,)), pltpu.SemaphoreType.DMA],
    )(small)


HBM = pl.BlockSpec(memory_space=pltpu.HBM)
SEM = pl.BlockSpec(memory_space=pltpu.SEMAPHORE)
EFFECT = pltpu.SideEffectType.DATAFLOW_SIDE_EFFECTING


def _exchange_start(srcs, *, by_slot, name, after=None):
    n = len(srcs)
    n_in = 2 * n + (after is not None)
    land_shapes = [a.shape if by_slot else (N_CHIPS,) + a.shape for a in srcs]
    lands = [pltpu.with_memory_space_constraint(lax.empty(sh, a.dtype), pltpu.HBM) for sh, a in zip(land_shapes, srcs)]
    srcs = [pltpu.with_memory_space_constraint(a, pltpu.HBM) for a in srcs]

    def body(*refs):
        ins, land = refs[:n], refs[n:2 * n]
        send_sems, recv_sems, token = refs[n_in], refs[n_in + 1], refs[-1]
        x, y, c, chips = _me_and_chips()
        me = 2 * x + y
        for a in range(n):
            for k, (px, py) in enumerate(chips):
                pltpu.make_async_remote_copy(
                    src_ref=ins[a].at[2 * px + py] if by_slot else ins[a], dst_ref=land[a].at[me],
                    send_sem=send_sems.at[3 * a + k], recv_sem=recv_sems.at[3 * a + k], device_id=(px, py, c),
                    device_id_type=MESH).start()
        token[...] = jnp.zeros_like(token)

    res = pl.pallas_call(
        body, name=name, in_specs=[HBM] * (2 * n) + ([ANY] if after is not None else []),
        out_specs=[SEM, SEM] + [HBM] * (2 * n) + [pl.BlockSpec(memory_space=pltpu.VMEM)],
        out_shape=[pltpu.SemaphoreType.DMA((3 * n,)), pltpu.SemaphoreType.DMA((3 * n,))]
        + [pltpu.HBM(a.shape, a.dtype) for a in srcs] + [pltpu.HBM(sh, a.dtype) for sh, a in zip(land_shapes, srcs)]
        + [jax.ShapeDtypeStruct((8, LANES), F32)],
        input_output_aliases={i: 2 + i for i in range(2 * n)},
        compiler_params=pltpu.CompilerParams(has_side_effects=EFFECT),
    )(*srcs, *lands, *([after] if after is not None else []))
    return dict(send=res[0], recv=res[1], srcs=res[2:2 + n], lands=res[2 + n:2 + 2 * n], token=res[-1])


def _exchange_wait(ex, after, *, by_slot, name):
    n = len(ex["srcs"])

    def body(*refs):
        ins, land = refs[:n], refs[n:2 * n]
        send_sems, recv_sems = refs[2 * n], refs[2 * n + 1]
        x, y, c, chips = _me_and_chips()
        me = 2 * x + y
        for a in range(n):
            for k, (px, py) in enumerate(chips):
                cp = pltpu.make_async_remote_copy(
                    src_ref=ins[a].at[me] if by_slot else ins[a], dst_ref=land[a].at[2 * px + py],
                    send_sem=send_sems.at[3 * a + k], recv_sem=recv_sems.at[3 * a + k], device_id=(px, py, c),
                    device_id_type=MESH)
                cp.wait_send()
                cp.wait_recv()

    res = pl.pallas_call(
        body, name=name, in_specs=[HBM] * (2 * n) + [SEM, SEM, ANY], out_specs=[HBM] * (2 * n),
        out_shape=[pltpu.HBM(a.shape, a.dtype) for a in ex["srcs"]] + [pltpu.HBM(a.shape, a.dtype) for a in ex["lands"]],
        input_output_aliases={i: i for i in range(2 * n)},
        compiler_params=pltpu.CompilerParams(has_side_effects=EFFECT),
    )(*ex["srcs"], *ex["lands"], ex["send"], ex["recv"], after)
    return res[:n], res[n:]


def _chip_index():
    return 2 * lax.axis_index("x") + lax.axis_index("y")


def _me_operand():
    return jnp.reshape(_chip_index(), (1,)).astype(jnp.int32)


def _place_own(land, own, *, name):
    rows, cols = _as2d(own).shape
    tr = _row_tile(rows, cols)

    def body(me_ref, own_ref, land_ref, out_ref):
        out_ref[...] = own_ref[...]

    res = pl.pallas_call(
        body, name=name,
        grid_spec=pltpu.PrefetchScalarGridSpec(
            num_scalar_prefetch=1, grid=(rows // tr,),
            in_specs=[pl.BlockSpec((tr, cols), lambda i, me: (i, 0)), ANY],
            out_specs=pl.BlockSpec((None, tr, cols), lambda i, me: (me[0], i, 0))),
        out_shape=jax.ShapeDtypeStruct((N_CHIPS, rows, cols), land.dtype), input_output_aliases={2: 0},
        compiler_params=_params("arbitrary"),
    )(_me_operand(), _as2d(own), land.reshape(N_CHIPS, rows, cols))
    return res.reshape(land.shape)


def _sum_partials(lands, parts, *, name):
    n, rows, cols = lands.shape
    tr = _row_tile(rows, cols, 1024 * 1024)

    def body(me_ref, land_ref, own_ref, o_ref):
        me = me_ref[0]
        acc = None
        for i in range(n):
            term = jnp.where(me == i, own_ref[...], land_ref[i]).astype(F32)
            acc = term if acc is None else acc + term
        o_ref[...] = acc

    return pl.pallas_call(
        body, name=name,
        grid_spec=pltpu.PrefetchScalarGridSpec(
            num_scalar_prefetch=1, grid=(rows // tr,),
            in_specs=[pl.BlockSpec((n, tr, cols), lambda i, me: (0, i, 0)),
                      pl.BlockSpec((None, tr, cols), lambda i, me: (me[0], i, 0))],
            out_specs=pl.BlockSpec((tr, cols), lambda i, me: (i, 0))),
        out_shape=jax.ShapeDtypeStruct((rows, cols), F32), compiler_params=_params("arbitrary"),
    )(_me_operand(), lands, parts)


def _swap_sibling(arrs, *, name):
    n = len(arrs)

    def body(*refs):
        ins, outs = refs[:n], refs[n:2 * n]
        send_sems, recv_sems = refs[2 * n:]
        x, y, c = lax.axis_index("x"), lax.axis_index("y"), lax.axis_index("c")
        cps = [pltpu.make_async_remote_copy(src_ref=ins[a], dst_ref=outs[a], send_sem=send_sems.at[a],
                                            recv_sem=recv_sems.at[a], device_id=(x, y, 1 - c), device_id_type=MESH)
               for a in range(n)]
        for cp in cps:
            cp.start()
        for cp in cps:
            cp.wait()

    return pl.pallas_call(
        body, name=name, in_specs=[ANY] * n, out_specs=[ANY] * n,
        out_shape=[jax.ShapeDtypeStruct(a.shape, a.dtype) for a in arrs],
        scratch_shapes=[pltpu.SemaphoreType.DMA((n,)), pltpu.SemaphoreType.DMA((n,))],
    )(*arrs)


def _row_tile(rows, cols, budget=2 * 1024 * 1024):
    best = None
    for t in range(16, rows + 1, 16):
        if rows % t == 0 and t * cols * 4 <= budget:
            best = t
    return best if best is not None else rows


def _sum_slots(parts, *, name):
    n, rows, cols = parts.shape
    tr = _row_tile(rows, cols, 1024 * 1024)

    def body(p_ref, o_ref):
        acc = p_ref[0].astype(F32)
        for i in range(1, n):
            acc = acc + p_ref[i].astype(F32)
        o_ref[...] = acc

    return pl.pallas_call(
        body, name=name, grid=(rows // tr,), in_specs=[pl.BlockSpec((n, tr, cols), lambda i: (0, i, 0))],
        out_specs=pl.BlockSpec((tr, cols), lambda i: (i, 0)), out_shape=jax.ShapeDtypeStruct((rows, cols), F32),
        compiler_params=_params("parallel"),
    )(parts)


def _adamw(w, m, v, g_a, g_b, *, name):
    rows, cols = w.shape
    tr = _row_tile(rows, cols)
    two = g_b is not None
    c1 = 1.0 / (1.0 - ADAM_B1 ** ADAM_STEP)
    c2 = 1.0 / (1.0 - ADAM_B2 ** ADAM_STEP)

    def body(*refs):
        w_ref, m_ref, v_ref, ga_ref = refs[:4]
        g_ref, d_ref, nm_ref, nv_ref = refs[4 + two:]
        g = ga_ref[...]
        if two:
            g = g + refs[4][...]
        nm = ADAM_B1 * m_ref[...] + (1.0 - ADAM_B1) * g
        nv = ADAM_B2 * v_ref[...] + (1.0 - ADAM_B2) * (g * g)
        g_ref[...] = g
        nm_ref[...] = nm
        nv_ref[...] = nv
        d_ref[...] = -ADAM_LR * ((nm * c1) / (jnp.sqrt(nv * c2) + ADAM_EPS) + ADAM_WD * w_ref[...])

    blk = pl.BlockSpec((tr, cols), lambda i: (i, 0))
    ins = [w, m, v, g_a] + ([g_b] if two else [])
    return pl.pallas_call(
        body, name=name, grid=(rows // tr,), in_specs=[blk] * len(ins), out_specs=[blk] * 4,
        out_shape=[jax.ShapeDtypeStruct((rows, cols), F32)] * 4, compiler_params=_params("parallel"),
    )(*ins)


def _cast_bf16(w, *, name):
    rows, cols = w.shape
    tr = _row_tile(rows, cols)

    def body(w_ref, o_ref):
        o_ref[...] = w_ref[...].astype(BF16)

    blk = pl.BlockSpec((tr, cols), lambda i: (i, 0))
    return pl.pallas_call(body, name=name, grid=(rows // tr,), in_specs=[blk], out_specs=blk,
                          out_shape=jax.ShapeDtypeStruct((rows, cols), BF16), compiler_params=_params("parallel"))(w)


BIG = ("w_in", "w_branch", "w_out", "w_ff1", "w_ff2")
SMALL = ("norm_mix_pre", "conv_qkv_w", "gdn_a_log", "gdn_dt_bias", "gdn_norm_w", "conv_sc_w", "norm_mix_post",
         "norm_ffn_pre", "norm_ffn_post")
ORDER = ("norm_mix_pre", "w_in", "conv_qkv_w", "gdn_a_log", "gdn_dt_bias", "gdn_norm_w", "conv_sc_w", "w_branch",
         "w_out", "norm_mix_post", "norm_ffn_pre", "w_ff1", "w_ff2", "norm_ffn_post")


_MATMUL_LAYOUT = dict(
    w_in=_in_cols_from_chips,
    w_branch=lambda a: a,
    w_out=lambda a: a.reshape(D_MODEL, D_MODEL),
    w_ff1=lambda a: a,
    w_ff2=lambda a: a.reshape(D_FF, D_MODEL),
)
_SHARD_LAYOUT = dict(
    w_in=_in_cols_to_chips,
    w_branch=lambda g: g,
    w_out=lambda g: g.reshape(N_CHIPS, D_MODEL // N_CHIPS, D_MODEL),
    w_ff1=lambda g: g,
    w_ff2=lambda g: g.reshape(N_CHIPS, D_FF // N_CHIPS, D_MODEL),
)


def _full_weights(big, conv, rep, l):
    p = {k: _MATMUL_LAYOUT[k](a) for k, a in big.items()}
    if conv is not None:
        p["conv_qkv_w"] = conv["conv_qkv_w"][:, l].transpose(1, 0, 2).reshape(4, 3 * BRANCH_W)
        p["conv_sc_w"] = conv["conv_sc_w"][:, l].transpose(1, 0, 2).reshape(3, BRANCH_W)
    if rep is not None:
        for k in ("norm_mix_pre", "gdn_a_log", "gdn_dt_bias", "gdn_norm_w", "norm_mix_post", "norm_ffn_pre", "norm_ffn_post"):
            p[k] = rep[k][l]
    return p


def _partials_by_chip(g, names):
    return [_SHARD_LAYOUT[k](g[k]).astype(BF16) for k in names]


def _pack_small(grads):
    pieces, layout = [], []
    for name in SMALL:
        v = jnp.stack([g[name] for g in grads]).astype(F32)
        layout.append((name, v.shape))
        pieces.append(v.reshape(-1))
    flat = jnp.concatenate(pieces)
    rows = -(-flat.shape[0] // LANES)
    rows = -(-rows // 8) * 8
    flat = jnp.pad(flat, (0, rows * LANES - flat.shape[0]))
    return flat.reshape(rows, LANES), layout


def _unpack_small(table, layout):
    flat, out, off = table.reshape(-1), {}, 0
    for name, shape in layout:
        size = 1
        for d in shape:
            size *= d
        out[name] = flat[off:off + size].reshape(shape)
        off += size
    return out


def _as2d(a):
    return a.reshape(-1, a.shape[-1]) if a.ndim > 1 else a.reshape(1, -1)


def kernel(x, norm_mix_pre, w_in, conv_qkv_w, gdn_a_log, gdn_dt_bias, gdn_norm_w, conv_sc_w, w_branch, w_out, norm_mix_post, norm_ffn_pre, w_ff1, w_ff2, norm_ffn_post, loss_target, m_norm_mix_pre, m_w_in, m_conv_qkv_w, m_gdn_a_log, m_gdn_dt_bias, m_gdn_norm_w, m_conv_sc_w, m_w_branch, m_w_out, m_norm_mix_post, m_norm_ffn_pre, m_w_ff1, m_w_ff2, m_norm_ffn_post, v_norm_mix_pre, v_w_in, v_conv_qkv_w, v_gdn_a_log, v_gdn_dt_bias, v_gdn_norm_w, v_conv_sc_w, v_w_branch, v_w_out, v_norm_mix_post, v_norm_ffn_pre, v_w_ff1, v_w_ff2, v_norm_ffn_post):
    w = dict(norm_mix_pre=norm_mix_pre, w_in=w_in, conv_qkv_w=conv_qkv_w, gdn_a_log=gdn_a_log, gdn_dt_bias=gdn_dt_bias,
             gdn_norm_w=gdn_norm_w, conv_sc_w=conv_sc_w, w_branch=w_branch, w_out=w_out, norm_mix_post=norm_mix_post,
             norm_ffn_pre=norm_ffn_pre, w_ff1=w_ff1, w_ff2=w_ff2, norm_ffn_post=norm_ffn_post)
    m = dict(norm_mix_pre=m_norm_mix_pre, w_in=m_w_in, conv_qkv_w=m_conv_qkv_w, gdn_a_log=m_gdn_a_log,
             gdn_dt_bias=m_gdn_dt_bias, gdn_norm_w=m_gdn_norm_w, conv_sc_w=m_conv_sc_w, w_branch=m_w_branch, w_out=m_w_out,
             norm_mix_post=m_norm_mix_post, norm_ffn_pre=m_norm_ffn_pre, w_ff1=m_w_ff1, w_ff2=m_w_ff2,
             norm_ffn_post=m_norm_ffn_post)
    v = dict(norm_mix_pre=v_norm_mix_pre, w_in=v_w_in, conv_qkv_w=v_conv_qkv_w, gdn_a_log=v_gdn_a_log,
             gdn_dt_bias=v_gdn_dt_bias, gdn_norm_w=v_gdn_norm_w, conv_sc_w=v_conv_sc_w, w_branch=v_w_branch, w_out=v_w_out,
             norm_mix_post=v_norm_mix_post, norm_ffn_pre=v_norm_ffn_pre, w_ff1=v_w_ff1, w_ff2=v_w_ff2,
             norm_ffn_post=v_norm_ffn_post)

    me = _chip_index()

    shards = {k: _cast_bf16(_as2d(w[k]), name=f"cast_{k}").reshape(w[k].shape) for k in BIG}
    conv_names = ("conv_qkv_w", "conv_sc_w")
    FIRST, REST = ("w_in",), ("w_branch", "w_out", "w_ff1", "w_ff2")

    def gather_start(l, names, tag, after):
        srcs = [shards[k][l] for k in names] + ([w[k] for k in conv_names] if (l == 0 and "w_in" in names) else [])
        return _exchange_start(srcs, by_slot=False, name=f"gather_start{l}{tag}", after=after)

    def gather_land(ex, names, l, tag, after):
        own, lands = _exchange_wait(ex, after, by_slot=False, name=f"gather_wait{l}{tag}")
        return {k: _place_own(land, o, name=f"own_{k}") for k, land, o in zip(names, lands, own)}

    gathers = {0: gather_start(0, FIRST, "a", None)}
    conv = {}

    def weights_of(l, x_l):
        if l > 0:
            full = gather_land(gathers[l], BIG, l, "", x_l)
            p = _full_weights(full, conv, w, l)
            if l + 1 < DEPTH:
                gathers[l + 1] = gather_start(l + 1, BIG, "", full["w_out"])
                p["norm_mix_pre"] = p["norm_mix_pre"] + gathers[l + 1]["token"][0, 0]
            return p, None
        full = gather_land(gathers[0], FIRST + conv_names, 0, "a", gathers[0]["token"])
        conv.update({k: full[k] for k in conv_names})
        p = _full_weights({"w_in": full["w_in"]}, conv, w, 0)
        rest = gather_start(0, REST, "b", full["conv_sc_w"])
        p["norm_mix_pre"] = p["norm_mix_pre"] + rest["token"][0, 0]

        def late(after):
            arrived = gather_land(rest, REST, 0, "b", after)
            q = _full_weights(arrived, None, None, 0)
            gathers[1] = gather_start(1, BIG, "", arrived["w_out"])
            q["norm_mix_post"] = p["norm_mix_post"] + gathers[1]["token"][0, 0]
            return q

        return p, late

    grads, scatters = [None] * DEPTH, {}

    def scatter_start(l, g, names, tag):
        scatters[l, names] = _exchange_start(_partials_by_chip(g, names), by_slot=True, name=f"scatter_start{l}{tag}")
        return scatters[l, names]["token"][0, 0]

    def grads_early(g):
        return scatter_start(0, g, REST, "a")

    def grads_done(l, g):
        grads[l] = g
        return scatter_start(l, g, FIRST, "b") if l == 0 else scatter_start(l, g, BIG, "")

    loss, dx = _local_step(x[0], loss_target[0], DEPTH, weights_of, grads_done, grads_early)
    loss = lax.psum(loss, ("x", "y", "c"))

    sums = [dict() for _ in range(DEPTH)]
    for (l, names), ex in sorted(scatters.items(), key=lambda kv: (-kv[0][0], kv[0][1] != REST)):
        tag = "" if names == BIG else ("a" if names == REST else "b")
        parts, lands = _exchange_wait(ex, dx, by_slot=True, name=f"scatter_wait{l}{tag}")
        for k, r, o in zip(names, lands, parts):
            sums[l][k] = _sum_partials(r.reshape(N_CHIPS, -1, r.shape[-1]), o.reshape(N_CHIPS, -1, o.shape[-1]), name=f"sum_{k}")
    mine = [jnp.concatenate([sums[l][k] for l in range(DEPTH)], axis=0) for k in BIG]
    theirs = _swap_sibling(mine, name="swap_sibling")
    small, layout = _pack_small(grads)
    small_g = _unpack_small(_sum_slots(_gather_devices(small, name="gather_small"), name="sum_small"), layout)
    for k, width in (("conv_qkv_w", 3 * BRANCH_W // N_CHIPS), ("conv_sc_w", BRANCH_W // N_CHIPS)):
        small_g[k] = lax.dynamic_slice_in_dim(small_g[k], me * width, width, axis=2)

    out = {}
    for k, s_mine, s_theirs in zip(BIG, mine, theirs):
        res = _adamw(_as2d(w[k]), _as2d(m[k]), _as2d(v[k]), s_mine, s_theirs, name=f"adamw_{k}")
        out[k] = [r.reshape(w[k].shape) for r in res]
    for k in SMALL:
        res = _adamw(_as2d(w[k]), _as2d(m[k]), _as2d(v[k]), _as2d(small_g[k]), None, name=f"adamw_{k}")
        out[k] = [r.reshape(w[k].shape) for r in res]
    return (loss, dx[None], *[out[k][0] for k in ORDER], *[out[k][1] for k in ORDER], *[out[k][2] for k in ORDER],
            *[out[k][3] for k in ORDER])
```

```python
import functools

import jax
import jax.numpy as jnp
from jax import lax
from jax.experimental import pallas as pl
from jax.experimental.pallas import tpu as pltpu

F32 = jnp.float32
BF16 = jnp.bfloat16
MESH = pl.DeviceIdType.MESH

LANES = 128
D_MODEL = 1024
DEPTH = 4
CHUNK = 64
GDN_CHUNKS_PER_STEP = 2
GDN_HEADS, GDN_DIM = 4, 128
SB_HEADS, SB_DIM = 8, 64
BRANCH_W = 512
N_BRANCH = 3
D_FF = 4 * D_MODEL
EPS = 1e-6
IN_W = 8200
AB_COL = 2048
AB_PAD = LANES - 8
IN_WP = IN_W + AB_PAD
N_CHIPS = 4
N_DEV = 8
GATES_COL = 5128
PB_GATES, PB_QKV, PB_GATE, PB_AB, PB_SB, PB_SCX, PB_SCB, PB_SCC = 0, 24, 36, 40, 41, 53, 57, 61
SB_TILE = 128
SB_GROUP = 4
SB_QTILE = 256
SB_SCALE = SB_DIM ** -0.5
SB_SCALE2 = SB_SCALE * 1.4426950408889634
GDN_QSCALE = GDN_DIM ** -0.5
VMEM_LIMIT = 56 * 1024 * 1024

ADAM_LR, ADAM_B1, ADAM_B2, ADAM_EPS, ADAM_WD, ADAM_STEP = 0.001, 0.9, 0.999, 1e-08, 0.01, 10

NT = (((1,), (1,)), ((), ()))
TN = (((0,), (0,)), ((), ()))
HI = lax.Precision.HIGH


def _pad_in_cols(w):
    return jnp.concatenate([w[:, GATES_COL:], w[:, :AB_COL + 8], jnp.zeros((w.shape[0], AB_PAD), w.dtype),
                            w[:, AB_COL + 8:GATES_COL]], axis=1)


def _unpad_in_cols(g):
    n_gates = IN_W - GATES_COL
    return jnp.concatenate([g[:, n_gates:n_gates + AB_COL + 8], g[:, n_gates + AB_COL + 8 + AB_PAD:], g[:, :n_gates]], axis=1)


IN_SHARD = IN_W // N_CHIPS
_IN_SEGMENTS = ((0, AB_COL + 8, IN_W - GATES_COL), (AB_COL + 8, GATES_COL, IN_W - GATES_COL + AB_PAD),
                (GATES_COL, IN_W, -GATES_COL))


def _in_cols_from_chips(slots):
    def cols(first, last):
        out = []
        for j in range(N_CHIPS):
            lo, hi = max(first, j * IN_SHARD), min(last, (j + 1) * IN_SHARD)
            if lo < hi:
                out.append(slots[j][:, lo - j * IN_SHARD:hi - j * IN_SHARD])
        return out

    head, tail, gates = (cols(first, last) for first, last, _ in _IN_SEGMENTS)
    return jnp.concatenate(gates + head + [jnp.zeros((slots.shape[1], AB_PAD), slots.dtype)] + tail, axis=1)


def _in_cols_to_chips(g):
    shards = []
    for j in range(N_CHIPS):
        pieces = []
        for first, last, shift in _IN_SEGMENTS:
            lo, hi = max(first, j * IN_SHARD), min(last, (j + 1) * IN_SHARD)
            if lo < hi:
                pieces.append(g[:, lo + shift:hi + shift])
        shards.append(jnp.concatenate(pieces, axis=1))
    return jnp.stack(shards)


def _params(*sem):
    return pltpu.CompilerParams(dimension_semantics=sem if sem else None, vmem_limit_bytes=VMEM_LIMIT)


def _sigmoid(x):
    return 1.0 / (1.0 + jnp.exp(-x))


def _softplus(x):
    return jnp.maximum(x, 0.0) + jnp.log(1.0 + jnp.exp(-jnp.abs(x)))


def _softplus2(x):
    return jnp.maximum(x, 0.0) + jnp.log2(1.0 + jnp.exp2(-jnp.abs(x)))


def _dot(a, b, dims=None, precision=None):
    if dims is None:
        return jnp.dot(a, b, preferred_element_type=F32, precision=precision)
    return lax.dot_general(a, b, dims, preferred_element_type=F32, precision=precision)


def _bdot(a, b, dims=None):
    return _dot(a.astype(BF16), b.astype(BF16), dims)


def _matmul(a, b, *, name, ta=False, tb=False, tm, tn, tk=None, outs=(F32,), epi=None, extras=(), b_chips=False,
            out_chips=False):
    if ta:
        kdim, m = a.shape
    else:
        m, kdim = a.shape
    if b_chips:
        per = b.shape[2]
        if tb:
            n, kb = b.shape[1], N_CHIPS * per
        else:
            kb, n = b.shape[1], N_CHIPS * per
    elif tb:
        n, kb = b.shape
    else:
        kb, n = b.shape
    assert kdim == kb, (a.shape, b.shape)
    tk = kdim if tk is None else min(tk, kdim)
    tm = min(tm, m)
    assert m % tm == 0 and n % tn == 0 and kdim % tk == 0, (m, n, kdim, tm, tn, tk)
    nk = kdim // tk
    ni, nj = m // tm, n // tn
    a_bytes, b_bytes = a.size * a.dtype.itemsize, b.size * b.dtype.itemsize
    cols_inner = a_bytes * (nj if nk > 1 else 1) + b_bytes * (ni if nj * nk > 1 else 1)
    rows_inner = b_bytes * (ni if nk > 1 else 1) + a_bytes * (nj if ni * nk > 1 else 1)
    swap = rows_inner < cols_inner

    def ix(f):
        return (lambda g0, g1, k: f(g1, g0, k)) if swap else f

    a_spec = pl.BlockSpec((tk, tm), ix(lambda i, j, k: (k, i))) if ta else pl.BlockSpec((tm, tk), ix(lambda i, j, k: (i, k)))
    if b_chips and tb:
        assert per % tk == 0
        b_spec = pl.BlockSpec((None, tn, tk), ix(lambda i, j, k: (k // (per // tk), j, k % (per // tk))))
    elif b_chips:
        assert per % tn == 0
        b_spec = pl.BlockSpec((None, tk, tn), ix(lambda i, j, k: (j // (per // tn), k, j % (per // tn))))
    elif tb:
        b_spec = pl.BlockSpec((tn, tk), ix(lambda i, j, k: (j, k)))
    else:
        b_spec = pl.BlockSpec((tk, tn), ix(lambda i, j, k: (k, j)))
    mn_spec = pl.BlockSpec((tm, tn), ix(lambda i, j, k: (i, j)))
    if out_chips:
        per_o = n // N_CHIPS
        assert per_o % tn == 0
        out_spec = pl.BlockSpec((None, tm, tn), ix(lambda i, j, k: (j // (per_o // tn), i, j % (per_o // tn))))
        out_dims = (N_CHIPS, m, per_o)
    else:
        out_spec, out_dims = mn_spec, (m, n)
    dims = (((0 if ta else 1,), (1 if tb else 0,)), ((), ()))
    n_ex, n_out = len(extras), len(outs)

    def body(a_ref, b_ref, *rest):
        ex, o, acc = rest[:n_ex], rest[n_ex:n_ex + n_out], rest[n_ex + n_out:]
        part = lax.dot_general(a_ref[...].astype(BF16), b_ref[...].astype(BF16), dims, preferred_element_type=F32)

        def finish(val):
            res = epi(val, *[e[...] for e in ex]) if epi is not None else (val,)
            for r, oref in zip(res, o):
                oref[...] = r.astype(oref.dtype)

        if nk == 1:
            finish(part)
        else:
            k = pl.program_id(2)

            @pl.when(k == 0)
            def _():
                acc[0][...] = part

            @pl.when(k > 0)
            def _():
                acc[0][...] += part

            @pl.when(k == nk - 1)
            def _():
                finish(acc[0][...])

    res = pl.pallas_call(
        body, name=name, grid=(nj, ni, nk) if swap else (ni, nj, nk),
        in_specs=[a_spec, b_spec] + [mn_spec] * n_ex,
        out_specs=[out_spec] * n_out,
        out_shape=[jax.ShapeDtypeStruct(out_dims, dt) for dt in outs],
        scratch_shapes=[pltpu.VMEM((tm, tn), F32)] if nk > 1 else [],
        compiler_params=_params("parallel", "parallel", "arbitrary"),
    )(a, b, *extras)
    return res[0] if n_out == 1 else res


ROW_TILE = 512


def _norm_fwd(x, w, *, name):
    s, d = x.shape

    def body(x_ref, w_ref, o_ref):
        xv = x_ref[...]
        r = lax.rsqrt(jnp.mean(xv * xv, axis=-1, keepdims=True) + EPS)
        o_ref[...] = (xv * r * w_ref[...]).astype(o_ref.dtype)

    return pl.pallas_call(
        body, name=name, grid=(s // ROW_TILE,),
        in_specs=[pl.BlockSpec((ROW_TILE, d), lambda i: (i, 0)), pl.BlockSpec((1, d), lambda i: (0, 0))],
        out_specs=pl.BlockSpec((ROW_TILE, d), lambda i: (i, 0)),
        out_shape=jax.ShapeDtypeStruct((s, d), BF16), compiler_params=_params("parallel"),
    )(x, w.reshape(1, d))


def _resnorm_fwd(x, u, w, *, name):
    s, d = x.shape

    def body(x_ref, u_ref, w_ref, o_ref):
        uv = u_ref[...]
        r = lax.rsqrt(jnp.mean(uv * uv, axis=-1, keepdims=True) + EPS)
        o_ref[...] = x_ref[...] + uv * r * w_ref[...]

    row = pl.BlockSpec((ROW_TILE, d), lambda i: (i, 0))
    return pl.pallas_call(
        body, name=name, grid=(s // ROW_TILE,),
        in_specs=[row, row, pl.BlockSpec((1, d), lambda i: (0, 0))], out_specs=row,
        out_shape=jax.ShapeDtypeStruct((s, d), F32), compiler_params=_params("parallel"),
    )(x, u, w.reshape(1, d))


def _norm_bwd(xin, w, dy, res, *, out_dtype, name):
    s, d = xin.shape
    has_res = res is not None

    def body(*refs):
        x_ref, w_ref, dy_ref = refs[:3]
        res_ref = refs[3] if has_res else None
        dx_ref, dw_ref = refs[3 + has_res:]
        xv, dyv = x_ref[...], dy_ref[...].astype(F32)
        r = lax.rsqrt(jnp.mean(xv * xv, axis=-1, keepdims=True) + EPS)
        xh = xv * r
        g = dyv * w_ref[...]
        dx = r * (g - xh * jnp.mean(g * xh, axis=-1, keepdims=True))
        if has_res:
            dx = dx + res_ref[...]
        dx_ref[...] = dx.astype(dx_ref.dtype)

        @pl.when(pl.program_id(0) == 0)
        def _():
            dw_ref[...] = jnp.zeros_like(dw_ref)

        dw_ref[...] += jnp.sum(dyv * xh, axis=0, keepdims=True)

    row = pl.BlockSpec((ROW_TILE, d), lambda i: (i, 0))
    vec = pl.BlockSpec((1, d), lambda i: (0, 0))
    ins = [xin, w.reshape(1, d), dy] + ([res] if has_res else [])
    dx, dw = pl.pallas_call(
        body, name=name, grid=(s // ROW_TILE,),
        in_specs=[row, vec, row] + ([row] if has_res else []), out_specs=[row, vec],
        out_shape=[jax.ShapeDtypeStruct((s, d), out_dtype), jax.ShapeDtypeStruct((1, d), F32)],
        compiler_params=_params("arbitrary"),
    )(*ins)
    return dx, dw.reshape(d)


def _loss_fwd_bwd(y, target, *, name):
    s, d = y.shape

    def body(y_ref, t_ref, loss_ref, dy_ref):
        e = y_ref[...] - t_ref[...]
        dy_ref[...] = e * (1.0 / d)

        @pl.when(pl.program_id(0) == 0)
        def _():
            loss_ref[...] = jnp.zeros_like(loss_ref)

        part = jnp.sum(jnp.sum(e * e, axis=1, keepdims=True), axis=0, keepdims=True)
        loss_ref[...] += part * (0.5 / d)

    row = pl.BlockSpec((ROW_TILE, d), lambda i: (i, 0))
    loss, dy = pl.pallas_call(
        body, name=name, grid=(s // ROW_TILE,), in_specs=[row, row],
        out_specs=[pl.BlockSpec((1, 1), lambda i: (0, 0)), row],
        out_shape=[jax.ShapeDtypeStruct((1, 1), F32), jax.ShapeDtypeStruct((s, d), F32)],
        compiler_params=_params("arbitrary"),
    )(y, target)
    return loss[0, 0], dy


def _shift_down(x, k, rows):
    if k == 0:
        return x
    return jnp.where(rows >= k, pltpu.roll(x, k, 0), 0.0)


def _shift_up(x, k, rows):
    if k == 0:
        return x
    n = x.shape[0]
    return jnp.where(rows < n - k, pltpu.roll(x, n - k, 0), 0.0)


def _col_spec(s, base):
    return pl.BlockSpec((s, LANES), lambda j: (0, base + j))


def _gdn_pre_math(x, w, j, rows):
    taps = w.shape[0]
    c = w[taps - 1:taps, :] * x
    for i in range(taps - 1):
        c = c + w[i:i + 1, :] * _shift_down(x, taps - 1 - i, rows)
    sg = _sigmoid(c)
    y = c * sg
    r = lax.rsqrt(jnp.sum(y * y, axis=-1, keepdims=True) + EPS)
    is_qk = j < 2 * GDN_HEADS
    scale = jnp.where(j < GDN_HEADS, GDN_QSCALE, 1.0)
    return c, sg, y, r, is_qk, scale


def _gdn_pre_fwd(proj, conv_w, *, name):
    s = proj.shape[0]

    def body(x_ref, w_ref, o_ref):
        j = pl.program_id(0)
        rows = lax.broadcasted_iota(jnp.int32, (s, LANES), 0)
        _, _, y, r, is_qk, scale = _gdn_pre_math(x_ref[...], w_ref[...], j, rows)
        o_ref[...] = jnp.where(is_qk, y * (r * scale), y)

    return pl.pallas_call(
        body, name=name, grid=(12,),
        in_specs=[_col_spec(s, PB_QKV), pl.BlockSpec((4, LANES), lambda j: (0, j))],
        out_specs=_col_spec(s, 0), out_shape=jax.ShapeDtypeStruct((s, 3 * BRANCH_W), F32),
        compiler_params=_params("parallel"),
    )(proj, conv_w)


def _gdn_pre_bwd(proj, conv_w, dqkvn, *, name):
    s = proj.shape[0]

    def body(x_ref, w_ref, d_ref, dx_ref, dw_ref):
        j = pl.program_id(0)
        rows = lax.broadcasted_iota(jnp.int32, (s, LANES), 0)
        x, w, dout = x_ref[...], w_ref[...], d_ref[...]
        c, sg, y, r, is_qk, scale = _gdn_pre_math(x, w, j, rows)
        yh = y * r
        dy_n = (scale * r) * (dout - yh * jnp.sum(dout * yh, axis=-1, keepdims=True))
        dy = jnp.where(is_qk, dy_n, dout)
        dc = dy * (sg * (1.0 + c * (1.0 - sg)))
        taps = w.shape[0]
        dx = w[taps - 1:taps, :] * dc
        dws = []
        for i in range(taps - 1):
            k = taps - 1 - i
            dx = dx + w[i:i + 1, :] * _shift_up(dc, k, rows)
            dws.append(jnp.sum(dc * _shift_down(x, k, rows), axis=0, keepdims=True))
        dws.append(jnp.sum(dc * x, axis=0, keepdims=True))
        dx_ref[...] = dx.astype(dx_ref.dtype)
        for i in range(taps):
            dw_ref[i:i + 1, :] = dws[i]

    return pl.pallas_call(
        body, name=name, grid=(12,),
        in_specs=[_col_spec(s, PB_QKV), pl.BlockSpec((4, LANES), lambda j: (0, j)), _col_spec(s, 0)],
        out_specs=[_col_spec(s, 0), pl.BlockSpec((4, LANES), lambda j: (0, j))],
        out_shape=[jax.ShapeDtypeStruct((s, 3 * BRANCH_W), BF16), jax.ShapeDtypeStruct((4, 3 * BRANCH_W), F32)],
        compiler_params=_params("parallel"),
    )(proj, conv_w, dqkvn)


def _lane_pad(v):
    return jnp.pad(v.reshape(1, -1), ((0, 0), (0, LANES - v.shape[0])))


def _gdn_gates_fwd(proj, a_log, dt_bias, *, name):
    s = proj.shape[0]

    def body(ab_ref, al_ref, dt_ref, o_ref):
        ab = ab_ref[...]
        lane = lax.broadcasted_iota(jnp.int32, (1, LANES), 1)
        g = -jnp.exp(al_ref[...]) * _softplus(ab + dt_ref[...])
        o_ref[...] = jnp.where(lane < GDN_HEADS, g, _sigmoid(ab))

    vec = pl.BlockSpec((1, LANES), lambda j: (0, 0))
    return pl.pallas_call(
        body, name=name, grid=(1,), in_specs=[_col_spec(s, PB_AB), vec, vec], out_specs=_col_spec(s, 0),
        out_shape=jax.ShapeDtypeStruct((s, LANES), F32), compiler_params=_params("arbitrary"),
    )(proj, _lane_pad(a_log), _lane_pad(dt_bias))


def _gdn_gates_bwd(proj, a_log, dt_bias, dgb, *, name):
    s = proj.shape[0]

    def body(ab_ref, al_ref, dt_ref, d_ref, dab_ref, dal_ref, ddt_ref):
        ab, d = ab_ref[...], d_ref[...]
        lane = lax.broadcasted_iota(jnp.int32, (1, LANES), 1)
        ea = jnp.exp(al_ref[...])
        pre = ab + dt_ref[...]
        g = -ea * _softplus(pre)
        dpre = d * (-ea) * _sigmoid(pre)
        beta = _sigmoid(ab)
        is_g = lane < GDN_HEADS
        dab = jnp.where(is_g, dpre, jnp.where(lane < 2 * GDN_HEADS, d * beta * (1.0 - beta), 0.0))
        dab_ref[...] = dab.astype(dab_ref.dtype)
        dal_ref[...] = jnp.sum(jnp.where(is_g, d * g, 0.0), axis=0, keepdims=True)
        ddt_ref[...] = jnp.sum(jnp.where(is_g, dpre, 0.0), axis=0, keepdims=True)

    vec = pl.BlockSpec((1, LANES), lambda j: (0, 0))
    dab, dal, ddt = pl.pallas_call(
        body, name=name, grid=(1,), in_specs=[_col_spec(s, PB_AB), vec, vec, _col_spec(s, 0)],
        out_specs=[_col_spec(s, 0), vec, vec],
        out_shape=[jax.ShapeDtypeStruct((s, LANES), BF16), jax.ShapeDtypeStruct((1, LANES), F32),
                   jax.ShapeDtypeStruct((1, LANES), F32)],
        compiler_params=_params("arbitrary"),
    )(proj, _lane_pad(a_log), _lane_pad(dt_bias), dgb)
    return dab, dal[0, :GDN_HEADS], ddt[0, :GDN_HEADS]


def _interleave(gens):
    results, live = [None] * len(gens), list(range(len(gens)))
    while live:
        for idx in list(live):
            try:
                next(gens[idx])
            except StopIteration as done:
                results[idx] = done.value
                live.remove(idx)
    return results


def _chunk_common(q, k, v, gb, gbt, h):
    c = CHUNK
    row = lax.broadcasted_iota(jnp.int32, (c, c), 0)
    col = lax.broadcasted_iota(jnp.int32, (c, c), 1)
    tril, strict, eye = row >= col, row > col, row == col
    lane = lax.broadcasted_iota(jnp.int32, (c, LANES), 1)
    sub = lax.broadcasted_iota(jnp.int32, (2 * GDN_HEADS, c), 0)
    g_col = jnp.sum(jnp.where(lane == h, gb, 0.0), axis=1, keepdims=True)
    beta_col = jnp.sum(jnp.where(lane == GDN_HEADS + h, gb, 0.0), axis=1, keepdims=True)
    g_row = jnp.sum(jnp.where(sub == h, gbt, 0.0), axis=0, keepdims=True)
    gc_col = jnp.sum(jnp.where(tril, jnp.broadcast_to(g_row, (c, c)), 0.0), axis=1, keepdims=True)
    gc_row = jnp.sum(jnp.where(row <= col, jnp.broadcast_to(g_col, (c, c)), 0.0), axis=0, keepdims=True)
    g_tot = jnp.sum(g_row, axis=1, keepdims=True)
    dm = jnp.exp(jnp.where(tril, gc_col - gc_row, -1e30))
    e_col = jnp.exp(gc_col)
    kdec_col = jnp.exp(g_tot - gc_col)
    gamma = jnp.exp(g_tot)
    kb = k * beta_col
    vb = v * beta_col
    kbg = kb * e_col
    kk = _bdot(kb, k, NT)
    qk = _bdot(q, k, NT)
    yield
    a = jnp.where(strict, kk * dm, 0.0)
    aqk = jnp.where(tril, qk * dm, 0.0)
    bneg = -a
    t = jnp.where(eye, 1.0, 0.0) + bneg
    p = _dot(bneg, bneg, precision=HI)
    yield
    for lvl in range(5):
        t_next = t + _dot(t, p, precision=HI)
        if lvl < 4:
            p = _dot(p, p, precision=HI)
        t = t_next
        yield
    u = _dot(t, vb, precision=HI)
    w = _dot(t, kbg, precision=HI)
    yield
    return dict(tril=tril, strict=strict, eye=eye, row=row, col=col, beta_col=beta_col, dm=dm, e_col=e_col,
                kdec_col=kdec_col, gamma=gamma, kb=kb, vb=vb, kbg=kbg, a=a, t=t, u=u, w=w, aqk=aqk,
                qd=q * e_col, kd=k * kdec_col)


def _gdn_chunk_fwd(qkvn, gb, gbt, *, name):
    s = qkvn.shape[0]
    n_chunks = s // CHUNK
    per = GDN_CHUNKS_PER_STEP
    rows_per = per * CHUNK

    def body(q_ref, k_ref, v_ref, gb_ref, gbt_ref, o_ref, st_ref, state):
        @pl.when(pl.program_id(0) == 0)
        def _():
            state[...] = jnp.zeros_like(state)

        def head(h, ci):
            hs = slice(h * GDN_DIM, (h + 1) * GDN_DIM)
            rows = slice(ci * CHUNK, (ci + 1) * CHUNK)
            q, k, v = q_ref[rows, hs], k_ref[rows, hs], v_ref[rows, hs]
            m = yield from _chunk_common(q, k, v, gb_ref[rows, :], gbt_ref[ci], h)
            for _ in range(ci):
                yield
            s0 = state[h]
            st_ref[ci, h] = s0
            vnew = m["u"] - _bdot(m["w"], s0)
            o_inter = _bdot(m["qd"], s0)
            yield
            o_ref[rows, hs] = o_inter + _bdot(m["aqk"], vnew)
            state[h] = m["gamma"] * s0 + _bdot(m["kd"], vnew, TN)

        _interleave([head(h, ci) for ci in range(per) for h in range(GDN_HEADS)])

    blk = lambda j: pl.BlockSpec((rows_per, BRANCH_W), lambda n: (n, j))
    return pl.pallas_call(
        body, name=name, grid=(n_chunks // per,),
        in_specs=[blk(0), blk(1), blk(2), pl.BlockSpec((rows_per, LANES), lambda n: (n, 0)),
                  pl.BlockSpec((per, 2 * GDN_HEADS, CHUNK), lambda n: (n, 0, 0))],
        out_specs=[blk(0), pl.BlockSpec((per, GDN_HEADS, GDN_DIM, GDN_DIM), lambda n: (n, 0, 0, 0))],
        out_shape=[jax.ShapeDtypeStruct((s, BRANCH_W), F32),
                   jax.ShapeDtypeStruct((n_chunks, GDN_HEADS, GDN_DIM, GDN_DIM), F32)],
        scratch_shapes=[pltpu.VMEM((GDN_HEADS, GDN_DIM, GDN_DIM), F32)],
        compiler_params=_params("arbitrary"),
    )(qkvn, qkvn, qkvn, gb, gbt)


def _gdn_chunk_bwd(qkvn, gb, gbt, states, do, *, name):
    s = qkvn.shape[0]
    n_chunks = s // CHUNK
    c = CHUNK
    per = GDN_CHUNKS_PER_STEP
    rows_per = per * CHUNK

    def body(q_ref, k_ref, v_ref, gb_ref, gbt_ref, st_ref, do_ref, dqkv_ref, dgb_ref, dstate):
        @pl.when(pl.program_id(0) == 0)
        def _():
            dstate[...] = jnp.zeros_like(dstate)

        lane = lax.broadcasted_iota(jnp.int32, (c, LANES), 1)

        def head(h, ci):
            hs = slice(h * GDN_DIM, (h + 1) * GDN_DIM)
            rows = slice(ci * CHUNK, (ci + 1) * CHUNK)
            q, k, v, dov = q_ref[rows, hs], k_ref[rows, hs], v_ref[rows, hs], do_ref[rows, hs]
            m = yield from _chunk_common(q, k, v, gb_ref[rows, :], gbt_ref[ci], h)
            for _ in range(per - 1 - ci):
                yield
            tril, strict, eye, row, col = m["tril"], m["strict"], m["eye"], m["row"], m["col"]
            s0, ds1 = st_ref[ci, h], dstate[h]
            vnew = m["u"] - _bdot(m["w"], s0)
            dvnew_a = _bdot(m["aqk"], dov, TN) + _bdot(m["kd"], ds1)
            dqd = _bdot(dov, s0, NT)
            ds_q = _bdot(m["qd"], dov, TN)
            dgamma = jnp.sum(jnp.sum(s0 * ds1, axis=1, keepdims=True), axis=0, keepdims=True)
            yield
            dvnew = dvnew_a
            daqk = jnp.where(tril, _bdot(dov, vnew, NT), 0.0)
            dkd = _bdot(vnew, ds1, NT)
            dw = -_bdot(dvnew, s0, NT)
            dstate[h] = m["gamma"] * ds1 + ds_q - _bdot(m["w"], dvnew, TN)
            dvb = _dot(m["t"], dvnew, TN, HI)
            yield
            dt = _dot(dvnew, m["vb"], NT, HI) + _dot(dw, m["kbg"], NT, HI)
            dkbg = _dot(m["t"], dw, TN, HI)
            dmq = daqk * m["dm"]
            dq = _bdot(dmq, k) + dqd * m["e_col"]
            dk_q = _bdot(dmq, q, TN)
            yield
            tdt = _dot(m["t"], dt, TN, HI)
            yield
            da = jnp.where(strict, -_dot(tdt, m["t"], NT, HI), 0.0)
            yield
            dmat = da * m["dm"]
            dkb = _bdot(dmat, k) + dkbg * m["e_col"]
            dk = (_bdot(dmat, m["kb"], TN) + dk_q + dkd * m["kdec_col"] + m["beta_col"] * dkb)
            yield
            dbeta_col = jnp.sum(dkb * k, axis=1, keepdims=True) + jnp.sum(dvb * v, axis=1, keepdims=True)
            e = da * m["a"] + daqk * m["aqk"]
            rs_kd = jnp.sum(dkd * m["kd"], axis=1, keepdims=True)
            e_colsum = jnp.sum(e, axis=0, keepdims=True)
            e_colsum_c = jnp.sum(jnp.where(eye, jnp.broadcast_to(e_colsum, (c, c)), 0.0), axis=1, keepdims=True)
            dgc = (jnp.sum(e, axis=1, keepdims=True) - e_colsum_c + jnp.sum(dqd * m["qd"], axis=1, keepdims=True)
                   - rs_kd + jnp.sum(dkbg * m["kbg"], axis=1, keepdims=True))
            last = jnp.sum(rs_kd, axis=0, keepdims=True) + dgamma * m["gamma"]
            dgc = dgc + jnp.where(lax.broadcasted_iota(jnp.int32, (c, 1), 0) == c - 1, last, 0.0)
            dgc_row = jnp.sum(jnp.where(eye, jnp.broadcast_to(dgc, (c, c)), 0.0), axis=0, keepdims=True)
            dg_col = jnp.sum(jnp.where(col >= row, jnp.broadcast_to(dgc_row, (c, c)), 0.0), axis=1, keepdims=True)
            for part, val in enumerate((dq, dk, m["beta_col"] * dvb)):
                lo = part * BRANCH_W + h * GDN_DIM
                dqkv_ref[rows, lo:lo + GDN_DIM] = val
            return jnp.where(lane == h, dg_col, 0.0) + jnp.where(lane == GDN_HEADS + h, dbeta_col, 0.0)

        order = [ci for ci in reversed(range(per))]
        parts = _interleave([head(h, ci) for ci in order for h in range(GDN_HEADS)])
        for pos, ci in enumerate(order):
            mine = parts[pos * GDN_HEADS:(pos + 1) * GDN_HEADS]
            dgb_ref[ci * CHUNK:(ci + 1) * CHUNK, :] = (mine[0] + mine[1]) + (mine[2] + mine[3])

    rev = lambda n: n_chunks // per - 1 - n
    blk = lambda j: pl.BlockSpec((rows_per, BRANCH_W), lambda n: (rev(n), j))
    return pl.pallas_call(
        body, name=name, grid=(n_chunks // per,),
        in_specs=[blk(0), blk(1), blk(2), pl.BlockSpec((rows_per, LANES), lambda n: (rev(n), 0)),
                  pl.BlockSpec((per, 2 * GDN_HEADS, CHUNK), lambda n: (rev(n), 0, 0)),
                  pl.BlockSpec((per, GDN_HEADS, GDN_DIM, GDN_DIM), lambda n: (rev(n), 0, 0, 0)), blk(0)],
        out_specs=[pl.BlockSpec((rows_per, 3 * BRANCH_W), lambda n: (rev(n), 0)),
                   pl.BlockSpec((rows_per, LANES), lambda n: (rev(n), 0))],
        out_shape=[jax.ShapeDtypeStruct((s, 3 * BRANCH_W), F32), jax.ShapeDtypeStruct((s, LANES), F32)],
        scratch_shapes=[pltpu.VMEM((GDN_HEADS, GDN_DIM, GDN_DIM), F32)],
        compiler_params=_params("arbitrary"),
    )(qkvn, qkvn, qkvn, gb, gbt, states, do)


def _gdn_post_fwd(o, proj, norm_w, *, name):
    s = o.shape[0]

    def body(o_ref, g_ref, w_ref, y_ref):
        ov, gv = o_ref[...], g_ref[...]
        r = lax.rsqrt(jnp.mean(ov * ov, axis=-1, keepdims=True) + EPS)
        y_ref[...] = (ov * r * w_ref[...] * (gv * _sigmoid(gv))).astype(y_ref.dtype)

    return pl.pallas_call(
        body, name=name, grid=(GDN_HEADS,),
        in_specs=[_col_spec(s, 0), _col_spec(s, PB_GATE), pl.BlockSpec((1, LANES), lambda j: (0, 0))],
        out_specs=_col_spec(s, 0), out_shape=jax.ShapeDtypeStruct((s, BRANCH_W), BF16),
        compiler_params=_params("parallel"),
    )(o, proj, norm_w.reshape(1, GDN_DIM))


def _gdn_post_bwd(o, proj, norm_w, dy, *, name):
    s = o.shape[0]

    def body(o_ref, g_ref, w_ref, dy_ref, do_ref, dg_ref, dw_ref):
        ov, gv, w, dyv = o_ref[...], g_ref[...], w_ref[...], dy_ref[...].astype(F32)
        r = lax.rsqrt(jnp.mean(ov * ov, axis=-1, keepdims=True) + EPS)
        oh = ov * r
        sg = _sigmoid(gv)
        silu = gv * sg
        dn = dyv * silu
        dg_ref[...] = (dyv * (oh * w) * (sg * (1.0 + gv * (1.0 - sg)))).astype(dg_ref.dtype)

        @pl.when(pl.program_id(0) == 0)
        def _():
            dw_ref[...] = jnp.zeros_like(dw_ref)

        dw_ref[...] += jnp.sum(dn * oh, axis=0, keepdims=True)
        g2 = dn * w
        do_ref[...] = r * (g2 - oh * jnp.mean(g2 * oh, axis=-1, keepdims=True))

    do, dg, dw = pl.pallas_call(
        body, name=name, grid=(GDN_HEADS,),
        in_specs=[_col_spec(s, 0), _col_spec(s, PB_GATE), pl.BlockSpec((1, LANES), lambda j: (0, 0)), _col_spec(s, 0)],
        out_specs=[_col_spec(s, 0), _col_spec(s, 0), pl.BlockSpec((1, LANES), lambda j: (0, 0))],
        out_shape=[jax.ShapeDtypeStruct((s, BRANCH_W), F32), jax.ShapeDtypeStruct((s, BRANCH_W), BF16),
                   jax.ShapeDtypeStruct((1, LANES), F32)],
        compiler_params=_params("arbitrary"),
    )(o, proj, norm_w.reshape(1, GDN_DIM), dy)
    return do, dg, dw.reshape(LANES)


def _split_terms(x):
    hi = x.astype(BF16)
    lo = (x - hi.astype(F32)).astype(BF16)
    return jnp.concatenate([hi, lo], axis=1)


def _sb_sum_matrix(pred):
    row = lax.broadcasted_iota(jnp.int32, (2 * SB_TILE, 2 * SB_TILE), 0) % SB_TILE
    col = lax.broadcasted_iota(jnp.int32, (2 * SB_TILE, 2 * SB_TILE), 1)
    return jnp.where((col >= SB_TILE) | pred(row, col), 1.0, 0.0).astype(BF16)


def _sb_head_masks():
    lane = lax.broadcasted_iota(jnp.int32, (1, LANES), 1)
    return [(lane < SB_DIM).astype(F32), (lane >= SB_DIM).astype(F32)]


def _sb_fwd(proj, *, name):
    s = proj.shape[0]
    t, tq = SB_TILE, SB_QTILE
    nq = s // tq

    def body(q_ref, k_ref, v_ref, o_ref, tot_ref):
        cmr = lax.broadcasted_iota(jnp.int32, (tq, t), 1) - lax.broadcasted_iota(jnp.int32, (tq, t), 0)
        uo = _sb_sum_matrix(lambda row, col: row > col)
        hm = _sb_head_masks()

        def qloop(i, carry0):
            qs = pl.multiple_of(i * tq, tq)
            qf = q_ref[pl.ds(qs, tq), :] * SB_SCALE2
            qh = [(qf * hm[h]).astype(BF16) for h in range(2)]
            diag = (i * tq) // (SB_GROUP * t)

            def group(g, st, masked):
                ks = pl.multiple_of(g * (SB_GROUP * t), SB_GROUP * t)
                kb = k_ref[pl.ds(ks, SB_GROUP * t), :].astype(BF16)
                vf = v_ref[pl.ds(ks, SB_GROUP * t), :]
                tiles = [(h, j) for h in range(2) for j in range(SB_GROUP)]
                z = [_dot(qh[h], kb, NT) for h in range(2)]
                keep = {j: cmr < i * tq - (g * SB_GROUP + j) * t for j in range(SB_GROUP)} if masked else None
                base, terms = {}, {}
                for h, j in tiles:
                    zj = z[h][:, j * t:(j + 1) * t]
                    sp = _softplus2(zj)
                    base[h, j] = zj - sp
                    terms[h, j] = _split_terms(jnp.where(keep[j], sp, 0.0) if masked else sp)
                sums = {hj: _dot(terms[hj], uo) for hj in tiles}
                acc, new = st[0], []
                for h in range(2):
                    run, att = st[1 + h], [None] * SB_GROUP
                    for j in reversed(range(SB_GROUP)):
                        a = jnp.exp2(base[h, j] - (sums[h, j][:, :t] + run))
                        att[j] = (jnp.where(keep[j], a, 0.0) if masked else a).astype(BF16)
                        run = run + sums[h, j][:, t:]
                    acc = acc + _dot(jnp.concatenate(att, axis=1), (vf * hm[h]).astype(BF16))
                    new.append(run)
                return (acc, *new)

            zero = jnp.zeros((tq, LANES), F32)
            st = group(diag, (zero, zero, zero), True)
            st = lax.fori_loop(0, diag, lambda jj, sv: group(diag - 1 - jj, sv, False), st)
            o_ref[pl.ds(qs, tq), :] = st[0]
            tot_ref[pl.ds(qs, tq), :] = st[1] * hm[0] + st[2] * hm[1]
            return carry0

        lax.fori_loop(0, nq, qloop, 0)

    out = jax.ShapeDtypeStruct((s, BRANCH_W), F32)
    return pl.pallas_call(
        body, name=name, grid=(SB_HEADS // 2,),
        in_specs=[_col_spec(s, PB_SB), _col_spec(s, PB_SB + 4), _col_spec(s, PB_SB + 8)],
        out_specs=[_col_spec(s, 0)] * 2, out_shape=[out] * 2,
        compiler_params=_params("parallel"),
    )(proj, proj, proj)


def _sb_bwd(proj, tot, do, *, name):
    s = proj.shape[0]
    t, tq = SB_TILE, SB_QTILE
    nq = s // tq

    def body(q_ref, k_ref, v_ref, tot_ref, do_ref, dq_ref, dk_ref, dv_ref, dk_acc, dv_acc):
        dk_acc[...] = jnp.zeros_like(dk_acc)
        dv_acc[...] = jnp.zeros_like(dv_acc)
        cmr = lax.broadcasted_iota(jnp.int32, (tq, t), 1) - lax.broadcasted_iota(jnp.int32, (tq, t), 0)
        u_le = _sb_sum_matrix(lambda row, col: row <= col)
        u_lt = _sb_sum_matrix(lambda row, col: row < col)
        hm = _sb_head_masks()

        def qloop(i, carry0):
            qs = pl.multiple_of(i * tq, tq)
            qraw = q_ref[pl.ds(qs, tq), :]
            dov = do_ref[pl.ds(qs, tq), :].astype(F32)
            totv = tot_ref[pl.ds(qs, tq), :]
            qh = [(qraw * (hm[h] * SB_SCALE2)).astype(BF16) for h in range(2)]
            q2 = jnp.concatenate([(qraw * hm[h]).astype(BF16) for h in range(2)], axis=0)
            doh = [(dov * hm[h]).astype(BF16) for h in range(2)]
            do2 = jnp.concatenate(doh, axis=0)
            tot = [jnp.max(totv * hm[h], axis=1, keepdims=True) for h in range(2)]
            diag = (i * tq) // (SB_GROUP * t)

            def group(g, st, masked):
                ks = pl.multiple_of(g * (SB_GROUP * t), SB_GROUP * t)
                kf = k_ref[pl.ds(ks, SB_GROUP * t), :]
                kb = kf.astype(BF16)
                vb = v_ref[pl.ds(ks, SB_GROUP * t), :].astype(BF16)
                tiles = [(h, j) for h in range(2) for j in range(SB_GROUP)]
                z = [_dot(qh[h], kb, NT) for h in range(2)]
                datt = [_dot(doh[h], vb, NT) for h in range(2)]
                keep = {j: cmr < i * tq - (g * SB_GROUP + j) * t for j in range(SB_GROUP)} if masked else None
                ls, lterms = {}, {}
                for h, j in tiles:
                    zj = z[h][:, j * t:(j + 1) * t]
                    sp = _softplus2(zj)
                    ls[h, j] = zj - sp
                    lterms[h, j] = _split_terms(jnp.where(keep[j], sp, 0.0) if masked else sp)
                lsum = {hj: _dot(lterms[hj], u_le) for hj in tiles}
                att, p, pterms, new_c = {}, {}, {}, []
                for h in range(2):
                    run = st[1 + h]
                    for j in range(SB_GROUP):
                        a = jnp.exp2(ls[h, j] - ((tot[h] - run) - lsum[h, j][:, :t]))
                        if masked:
                            a = jnp.where(keep[j], a, 0.0)
                        att[h, j] = a.astype(BF16)
                        p[h, j] = a * datt[h][:, j * t:(j + 1) * t]
                        pterms[h, j] = _split_terms(p[h, j])
                        run = run + lsum[h, j][:, t:]
                    new_c.append(run)
                psum = {hj: _dot(pterms[hj], u_lt) for hj in tiles}
                dzb, new_r = {}, []
                for h in range(2):
                    run = st[3 + h]
                    for j in range(SB_GROUP):
                        sig = jnp.exp2(ls[h, j])
                        dz = p[h, j] - sig * (p[h, j] + run + psum[h, j][:, :t])
                        if masked:
                            dz = jnp.where(keep[j], dz, 0.0)
                        dzb[h, j] = (dz * SB_SCALE).astype(BF16)
                        run = run + psum[h, j][:, t:]
                    new_r.append(run)
                k2 = jnp.concatenate([(kf * hm[h]).astype(BF16) for h in range(2)], axis=0)
                dq_acc = st[0] + _dot(jnp.concatenate([dzb[hj] for hj in tiles], axis=1), k2)
                for j in range(SB_GROUP):
                    rows = pl.ds(pl.multiple_of(ks + j * t, t), t)
                    dk_acc[rows, :] += _dot(jnp.concatenate([dzb[0, j], dzb[1, j]], axis=0), q2, TN)
                    dv_acc[rows, :] += _dot(jnp.concatenate([att[0, j], att[1, j]], axis=0), do2, TN)
                return (dq_acc, *new_c, *new_r)

            zero = jnp.zeros((tq, LANES), F32)
            st = lax.fori_loop(0, diag, lambda jj, sv: group(jj, sv, False), (zero,) * 5)
            st = group(diag, st, True)
            dq_ref[pl.ds(qs, tq), :] = st[0].astype(dq_ref.dtype)
            return carry0

        lax.fori_loop(0, nq, qloop, 0)
        dk_ref[...] = dk_acc[...].astype(dk_ref.dtype)
        dv_ref[...] = dv_acc[...].astype(dv_ref.dtype)

    out = jax.ShapeDtypeStruct((s, BRANCH_W), BF16)
    return pl.pallas_call(
        body, name=name, grid=(SB_HEADS // 2,),
        in_specs=[_col_spec(s, PB_SB), _col_spec(s, PB_SB + 4), _col_spec(s, PB_SB + 8), _col_spec(s, 0), _col_spec(s, 0)],
        out_specs=[_col_spec(s, 0)] * 3, out_shape=[out] * 3,
        scratch_shapes=[pltpu.VMEM((s, LANES), F32), pltpu.VMEM((s, LANES), F32)],
        compiler_params=_params("parallel"),
    )(proj, proj, proj, tot, do)


def _sc_fwd(proj, conv_w, *, name):
    s = proj.shape[0]

    def body(x_ref, b_ref, c_ref, w_ref, y_ref):
        rows = lax.broadcasted_iota(jnp.int32, (s, LANES), 0)
        w = w_ref[...]
        u = c_ref[...] * x_ref[...]
        cv = w[2:3, :] * u + w[1:2, :] * _shift_down(u, 1, rows) + w[0:1, :] * _shift_down(u, 2, rows)
        y_ref[...] = (b_ref[...] * cv).astype(y_ref.dtype)

    return pl.pallas_call(
        body, name=name, grid=(BRANCH_W // LANES,),
        in_specs=[_col_spec(s, PB_SCX), _col_spec(s, PB_SCB), _col_spec(s, PB_SCC), pl.BlockSpec((3, LANES), lambda j: (0, j))],
        out_specs=_col_spec(s, 0), out_shape=jax.ShapeDtypeStruct((s, BRANCH_W), BF16),
        compiler_params=_params("parallel"),
    )(proj, proj, proj, conv_w)


def _sc_bwd(proj, conv_w, dy, *, name):
    s = proj.shape[0]

    def body(x_ref, b_ref, c_ref, w_ref, dy_ref, dx_ref, db_ref, dc_ref, dw_ref):
        rows = lax.broadcasted_iota(jnp.int32, (s, LANES), 0)
        w, x, cg, dyv = w_ref[...], x_ref[...], c_ref[...], dy_ref[...].astype(F32)
        u = cg * x
        u1, u2 = _shift_down(u, 1, rows), _shift_down(u, 2, rows)
        cv = w[2:3, :] * u + w[1:2, :] * u1 + w[0:1, :] * u2
        db_ref[...] = (dyv * cv).astype(db_ref.dtype)
        dcv = dyv * b_ref[...]
        du = w[2:3, :] * dcv + w[1:2, :] * _shift_up(dcv, 1, rows) + w[0:1, :] * _shift_up(dcv, 2, rows)
        dx_ref[...] = (du * cg).astype(dx_ref.dtype)
        dc_ref[...] = (du * x).astype(dc_ref.dtype)
        dw_ref[0:1, :] = jnp.sum(dcv * u2, axis=0, keepdims=True)
        dw_ref[1:2, :] = jnp.sum(dcv * u1, axis=0, keepdims=True)
        dw_ref[2:3, :] = jnp.sum(dcv * u, axis=0, keepdims=True)

    out = jax.ShapeDtypeStruct((s, BRANCH_W), BF16)
    wspec = pl.BlockSpec((3, LANES), lambda j: (0, j))
    return pl.pallas_call(
        body, name=name, grid=(BRANCH_W // LANES,),
        in_specs=[_col_spec(s, PB_SCX), _col_spec(s, PB_SCB), _col_spec(s, PB_SCC), wspec, _col_spec(s, 0)],
        out_specs=[_col_spec(s, 0)] * 3 + [wspec],
        out_shape=[out] * 3 + [jax.ShapeDtypeStruct((3, BRANCH_W), F32)],
        compiler_params=_params("parallel"),
    )(proj, proj, proj, conv_w, dy)


MERGE_TM, MERGE_TN = 1024, D_MODEL // N_CHIPS


def _merge_specs():
    tm, tn = MERGE_TM, MERGE_TN
    y_spec = pl.BlockSpec((tm, BRANCH_W), lambda i, j: (i, 0))
    w_spec = pl.BlockSpec((None, N_BRANCH, BRANCH_W, tn), lambda i, j: (j, 0, 0, 0))
    gate_specs = [pl.BlockSpec((tm, tn), functools.partial(
        lambda i, j, b: (i, (PB_GATES * LANES + b * D_MODEL) // tn + j), b=b)) for b in range(N_BRANCH)]
    mn = pl.BlockSpec((tm, tn), lambda i, j: (i, j))
    return y_spec, w_spec, gate_specs, mn


def _merge_fwd(ya, yb, yc, wb, proj, *, name):
    s = ya.shape[0]
    y_spec, w_spec, gate_specs, mn = _merge_specs()

    def body(ya_ref, yb_ref, yc_ref, w_ref, g0, g1, g2, o_ref):
        acc = None
        for b, (y_ref, g_ref) in enumerate(zip((ya_ref, yb_ref, yc_ref), (g0, g1, g2))):
            term = _sigmoid(g_ref[...]) * _bdot(y_ref[...], w_ref[b])
            acc = term if acc is None else acc + term
        o_ref[...] = acc.astype(o_ref.dtype)

    return pl.pallas_call(
        body, name=name, grid=(s // MERGE_TM, D_MODEL // MERGE_TN),
        in_specs=[y_spec] * 3 + [w_spec] + gate_specs, out_specs=mn,
        out_shape=jax.ShapeDtypeStruct((s, D_MODEL), BF16), compiler_params=_params("parallel", "parallel"),
    )(ya, yb, yc, wb, proj, proj, proj)


def _merge_bwd(ya, yb, yc, wb, proj, dm, *, name):
    s = ya.shape[0]
    y_spec, w_spec, gate_specs, mn = _merge_specs()

    def body(ya_ref, yb_ref, yc_ref, w_ref, g0, g1, g2, dm_ref, *outs):
        dmv = dm_ref[...].astype(F32)
        for b, (y_ref, g_ref) in enumerate(zip((ya_ref, yb_ref, yc_ref), (g0, g1, g2))):
            sg = _sigmoid(g_ref[...])
            z = _bdot(y_ref[...], w_ref[b])
            outs[b][...] = (dmv * sg).astype(BF16)
            outs[N_BRANCH + b][...] = (dmv * z * sg * (1.0 - sg)).astype(BF16)

    out = jax.ShapeDtypeStruct((s, D_MODEL), BF16)
    res = pl.pallas_call(
        body, name=name, grid=(s // MERGE_TM, D_MODEL // MERGE_TN),
        in_specs=[y_spec] * 3 + [w_spec] + gate_specs + [mn], out_specs=[mn] * (2 * N_BRANCH),
        out_shape=[out] * (2 * N_BRANCH), compiler_params=_params("parallel", "parallel"),
    )(ya, yb, yc, wb, proj, proj, proj, dm)
    return res[:N_BRANCH], res[N_BRANCH:]


def _chunk_rows(v, s):
    return v[:, :2 * GDN_HEADS].reshape(s // CHUNK, CHUNK, 2 * GDN_HEADS).transpose(0, 2, 1)


def _relu2_epi(acc):
    r = jnp.maximum(acc, 0.0)
    return acc, r * r


def _drelu2_epi(acc, a):
    return (acc * (2.0 * jnp.maximum(a.astype(F32), 0.0)),)


def _layer_fwd(x0, p, late=None):
    s = x0.shape[0]
    h1 = _norm_fwd(x0, p["norm_mix_pre"], name="norm_mix_pre")
    proj = _matmul(h1, p["w_in"], name="proj_in", tm=512, tn=1664)
    qkvn = _gdn_pre_fwd(proj, p["conv_qkv_w"], name="gdn_pre")
    gb = _gdn_gates_fwd(proj, p["gdn_a_log"], p["gdn_dt_bias"], name="gdn_gates")
    gbt = _chunk_rows(gb, s)
    o_gdn, states = _gdn_chunk_fwd(qkvn, gb, gbt, name="gdn_chunk")
    ya = _gdn_post_fwd(o_gdn, proj, p["gdn_norm_w"], name="gdn_post")
    o_sb, sb_tot = _sb_fwd(proj, name="sb_attn")
    yc = _sc_fwd(proj, p["conv_sc_w"], name="short_conv")
    if late is not None:
        p = dict(p, **late(yc))
    merged = _merge_fwd(ya, o_sb, yc, p["w_branch"], proj, name="merge")
    u = _matmul(merged, p["w_out"], name="proj_out", tm=512, tn=1024)
    x1 = _resnorm_fwd(x0, u, p["norm_mix_post"], name="norm_mix_post")
    h2 = _norm_fwd(x1, p["norm_ffn_pre"], name="norm_ffn_pre")
    a, r = _matmul(h2, p["w_ff1"], name="ff1", tm=512, tn=1024, outs=(BF16, BF16), epi=_relu2_epi, b_chips=True)
    f = _matmul(r, p["w_ff2"], name="ff2", tm=1024, tn=1024, tk=2048)
    x2 = _resnorm_fwd(x1, f, p["norm_ffn_post"], name="norm_ffn_post")
    saved = dict(x0=x0, h1=h1, proj=proj, qkvn=qkvn, gb=gb, gbt=gbt, o_gdn=o_gdn, states=states, ya=ya, o_sb=o_sb,
                 sb_tot=sb_tot, yc=yc, merged=merged, u=u, x1=x1, h2=h2, a=a, r=r, f=f)
    return x2, saved, p


def _layer_bwd(dx2, p, sv, early=None):
    g = {}
    df, g["norm_ffn_post"] = _norm_bwd(sv["f"], p["norm_ffn_post"], dx2, None, out_dtype=BF16, name="norm_ffn_post_bwd")
    g["w_ff2"] = _matmul(sv["r"], df, ta=True, name="ff2_dw", tm=1024, tn=1024, tk=2048, outs=(BF16,))
    da = _matmul(df, p["w_ff2"], tb=True, name="ff2_dx", tm=512, tn=1024, outs=(BF16,), epi=_drelu2_epi,
                 extras=(sv["a"],))
    g["w_ff1"] = _matmul(sv["h2"], da, ta=True, name="ff1_dw", tm=1024, tn=1024, tk=2048, out_chips=True, outs=(BF16,))
    dh2 = _matmul(da, p["w_ff1"], tb=True, name="ff1_dx", tm=1024, tn=1024, tk=1024, b_chips=True)
    dx1, g["norm_ffn_pre"] = _norm_bwd(sv["x1"], p["norm_ffn_pre"], dh2, dx2, out_dtype=F32, name="norm_ffn_pre_bwd")
    du, g["norm_mix_post"] = _norm_bwd(sv["u"], p["norm_mix_post"], dx1, None, out_dtype=BF16, name="norm_mix_post_bwd")
    g["w_out"] = _matmul(sv["merged"], du, ta=True, name="out_dw", tm=1024, tn=1024, tk=2048, outs=(BF16,))
    dmerged = _matmul(du, p["w_out"], tb=True, name="out_dx", tm=512, tn=1024, outs=(BF16,))
    ys = (sv["ya"], sv["o_sb"], sv["yc"])
    dz, dgates = _merge_bwd(*ys, p["w_branch"], sv["proj"], dmerged, name="merge_bwd")
    g["w_branch"] = jnp.stack([_matmul(ys[b], dz[b], ta=True, name=f"branch_dw{b}", tm=512, tn=256, tk=1024, out_chips=True,
                                       outs=(BF16,)) for b in range(N_BRANCH)], axis=1)
    dys = [_matmul(dz[b], p["w_branch"][:, b], tb=True, name=f"branch_dx{b}", tm=1024, tn=512, tk=256, b_chips=True)
           for b in range(N_BRANCH)]
    conv_sc_w = p["conv_sc_w"]
    if early is not None:
        conv_sc_w = conv_sc_w + early({k: g[k] for k in ("w_branch", "w_out", "w_ff1", "w_ff2")})
    dscx, dscb, dscc, g["conv_sc_w"] = _sc_bwd(sv["proj"], conv_sc_w, dys[2], name="short_conv_bwd")
    dsq, dsk, dsv = _sb_bwd(sv["proj"], sv["sb_tot"], dys[1], name="sb_attn_bwd")
    do_gdn, dgate, dnw = _gdn_post_bwd(sv["o_gdn"], sv["proj"], p["gdn_norm_w"], dys[0], name="gdn_post_bwd")
    g["gdn_norm_w"] = dnw
    dqkvn, dgb = _gdn_chunk_bwd(sv["qkvn"], sv["gb"], sv["gbt"], sv["states"], do_gdn, name="gdn_chunk_bwd")
    dqkv, g["conv_qkv_w"] = _gdn_pre_bwd(sv["proj"], p["conv_qkv_w"], dqkvn, name="gdn_pre_bwd")
    dab, g["gdn_a_log"], g["gdn_dt_bias"] = _gdn_gates_bwd(sv["proj"], p["gdn_a_log"], p["gdn_dt_bias"], dgb,
                                                           name="gdn_gates_bwd")
    dproj = jnp.concatenate([*dgates, dqkv, dgate, dab, dsq, dsk, dsv, dscx, dscb, dscc], axis=1)
    g["w_in"] = _matmul(sv["h1"], dproj, ta=True, name="in_dw", tm=1024, tn=1664, tk=1024, outs=(BF16,))
    dh1 = _matmul(dproj, p["w_in"], tb=True, name="in_dx", tm=1024, tn=1024, tk=1664)
    dx0, g["norm_mix_pre"] = _norm_bwd(sv["x0"], p["norm_mix_pre"], dh1, dx1, out_dtype=F32, name="norm_mix_pre_bwd")
    return dx0, g


def _local_step(x, target, n_layers, weights_of, grads_done, grads_early=None):
    saved, layers = [], []
    h = x
    for l in range(n_layers):
        p, late = weights_of(l, h)
        h, sv, p = _layer_fwd(h, p, late)
        saved.append(sv)
        layers.append(p)
    loss, dh = _loss_fwd_bwd(h, target, name="loss")
    for l in reversed(range(n_layers)):
        dh, g = _layer_bwd(dh, layers[l], saved[l], grads_early if l == 0 else None)
        zero = grads_done(l, g)
        if l > 0:
            layers[l - 1] = dict(layers[l - 1], norm_ffn_post=layers[l - 1]["norm_ffn_post"] + zero)
    return loss, dh


ANY = pl.BlockSpec(memory_space=pl.ANY)


def _me_and_chips():
    x, y, c = lax.axis_index("x"), lax.axis_index("y"), lax.axis_index("c")
    chips = [(1 - x, y), (x, 1 - y), (1 - x, 1 - y)]
    return x, y, c, chips


def _gather_devices(small, *, name):
    def body(small_ref, small_out, ssend, srecv, local_sem):
        x, y, c, chips = _me_and_chips()
        dev = 4 * x + 2 * y + c
        lc = pltpu.make_async_copy(small_ref, small_out.at[dev], local_sem)
        lc.start()
        peers = [(x, y, 1 - c)] + [(px, py, pc) for (px, py) in chips for pc in (c, 1 - c)]
        sends = []
        for k, peer in enumerate(peers):
            cp = pltpu.make_async_remote_copy(src_ref=small_ref, dst_ref=small_out.at[dev], send_sem=ssend.at[k],
                                              recv_sem=srecv.at[k], device_id=peer, device_id_type=MESH)
            cp.start()
            sends.append(cp)
        for k, (px, py, pc) in enumerate(peers):
            pltpu.make_async_remote_copy(src_ref=small_ref, dst_ref=small_out.at[4 * px + 2 * py + pc], send_sem=ssend.at[k],
                                         recv_sem=srecv.at[k], device_id=(px, py, pc), device_id_type=MESH).wait_recv()
        for cp in sends:
            cp.wait_send()
        lc.wait()

    return pl.pallas_call(
        body, name=name, in_specs=[ANY], out_specs=ANY,
        out_shape=jax.ShapeDtypeStruct((N_DEV,) + small.shape, small.dtype),
        scratch_shapes=[pltpu.SemaphoreType.DMA((N_DEV - 1,)), pltpu.SemaphoreType.DMA((N_DEV - 1,)), pltpu.SemaphoreType.DMA],
    )(small)


HBM = pl.BlockSpec(memory_space=pltpu.HBM)
SEM = pl.BlockSpec(memory_space=pltpu.SEMAPHORE)
EFFECT = pltpu.SideEffectType.DATAFLOW_SIDE_EFFECTING


def _exchange_start(srcs, *, by_slot, name, after=None):
    n = len(srcs)
    n_in = 2 * n + (after is not None)
    land_shapes = [a.shape if by_slot else (N_CHIPS,) + a.shape for a in srcs]
    lands = [pltpu.with_memory_space_constraint(lax.empty(sh, a.dtype), pltpu.HBM) for sh, a in zip(land_shapes, srcs)]
    srcs = [pltpu.with_memory_space_constraint(a, pltpu.HBM) for a in srcs]

    def body(*refs):
        ins, land = refs[:n], refs[n:2 * n]
        send_sems, recv_sems, token = refs[n_in], refs[n_in + 1], refs[-1]
        x, y, c, chips = _me_and_chips()
        me = 2 * x + y
        for a in range(n):
            for k, (px, py) in enumerate(chips):
                pltpu.make_async_remote_copy(
                    src_ref=ins[a].at[2 * px + py] if by_slot else ins[a], dst_ref=land[a].at[me],
                    send_sem=send_sems.at[3 * a + k], recv_sem=recv_sems.at[3 * a + k], device_id=(px, py, c),
                    device_id_type=MESH).start()
        token[...] = jnp.zeros_like(token)

    res = pl.pallas_call(
        body, name=name, in_specs=[HBM] * (2 * n) + ([ANY] if after is not None else []),
        out_specs=[SEM, SEM] + [HBM] * (2 * n) + [pl.BlockSpec(memory_space=pltpu.VMEM)],
        out_shape=[pltpu.SemaphoreType.DMA((3 * n,)), pltpu.SemaphoreType.DMA((3 * n,))]
        + [pltpu.HBM(a.shape, a.dtype) for a in srcs] + [pltpu.HBM(sh, a.dtype) for sh, a in zip(land_shapes, srcs)]
        + [jax.ShapeDtypeStruct((8, LANES), F32)],
        input_output_aliases={i: 2 + i for i in range(2 * n)},
        compiler_params=pltpu.CompilerParams(has_side_effects=EFFECT),
    )(*srcs, *lands, *([after] if after is not None else []))
    return dict(send=res[0], recv=res[1], srcs=res[2:2 + n], lands=res[2 + n:2 + 2 * n], token=res[-1])


def _exchange_wait(ex, after, *, by_slot, name):
    n = len(ex["srcs"])

    def body(*refs):
        ins, land = refs[:n], refs[n:2 * n]
        send_sems, recv_sems = refs[2 * n], refs[2 * n + 1]
        x, y, c, chips = _me_and_chips()
        me = 2 * x + y
        for a in range(n):
            for k, (px, py) in enumerate(chips):
                cp = pltpu.make_async_remote_copy(
                    src_ref=ins[a].at[me] if by_slot else ins[a], dst_ref=land[a].at[2 * px + py],
                    send_sem=send_sems.at[3 * a + k], recv_sem=recv_sems.at[3 * a + k], device_id=(px, py, c),
                    device_id_type=MESH)
                cp.wait_send()
                cp.wait_recv()

    res = pl.pallas_call(
        body, name=name, in_specs=[HBM] * (2 * n) + [SEM, SEM, ANY], out_specs=[HBM] * (2 * n),
        out_shape=[pltpu.HBM(a.shape, a.dtype) for a in ex["srcs"]] + [pltpu.HBM(a.shape, a.dtype) for a in ex["lands"]],
        input_output_aliases={i: i for i in range(2 * n)},
        compiler_params=pltpu.CompilerParams(has_side_effects=EFFECT),
    )(*ex["srcs"], *ex["lands"], ex["send"], ex["recv"], after)
    return res[:n], res[n:]


def _chip_index():
    return 2 * lax.axis_index("x") + lax.axis_index("y")


def _me_operand():
    return jnp.reshape(_chip_index(), (1,)).astype(jnp.int32)


def _place_own(land, own, *, name):
    rows, cols = _as2d(own).shape
    tr = _row_tile(rows, cols)

    def body(me_ref, own_ref, land_ref, out_ref):
        out_ref[...] = own_ref[...]

    res = pl.pallas_call(
        body, name=name,
        grid_spec=pltpu.PrefetchScalarGridSpec(
            num_scalar_prefetch=1, grid=(rows // tr,),
            in_specs=[pl.BlockSpec((tr, cols), lambda i, me: (i, 0)), ANY],
            out_specs=pl.BlockSpec((None, tr, cols), lambda i, me: (me[0], i, 0))),
        out_shape=jax.ShapeDtypeStruct((N_CHIPS, rows, cols), land.dtype), input_output_aliases={2: 0},
        compiler_params=_params("arbitrary"),
    )(_me_operand(), _as2d(own), land.reshape(N_CHIPS, rows, cols))
    return res.reshape(land.shape)


def _sum_partials(lands, parts, *, name):
    n, rows, cols = lands.shape
    tr = _row_tile(rows, cols, 1024 * 1024)

    def body(me_ref, land_ref, own_ref, o_ref):
        me = me_ref[0]
        acc = None
        for i in range(n):
            term = jnp.where(me == i, own_ref[...], land_ref[i]).astype(F32)
            acc = term if acc is None else acc + term
        o_ref[...] = acc

    return pl.pallas_call(
        body, name=name,
        grid_spec=pltpu.PrefetchScalarGridSpec(
            num_scalar_prefetch=1, grid=(rows // tr,),
            in_specs=[pl.BlockSpec((n, tr, cols), lambda i, me: (0, i, 0)),
                      pl.BlockSpec((None, tr, cols), lambda i, me: (me[0], i, 0))],
            out_specs=pl.BlockSpec((tr, cols), lambda i, me: (i, 0))),
        out_shape=jax.ShapeDtypeStruct((rows, cols), F32), compiler_params=_params("arbitrary"),
    )(_me_operand(), lands, parts)


def _swap_sibling(arrs, *, name):
    n = len(arrs)

    def body(*refs):
        ins, outs = refs[:n], refs[n:2 * n]
        send_sems, recv_sems = refs[2 * n:]
        x, y, c = lax.axis_index("x"), lax.axis_index("y"), lax.axis_index("c")
        cps = [pltpu.make_async_remote_copy(src_ref=ins[a], dst_ref=outs[a], send_sem=send_sems.at[a],
                                            recv_sem=recv_sems.at[a], device_id=(x, y, 1 - c), device_id_type=MESH)
               for a in range(n)]
        for cp in cps:
            cp.start()
        for cp in cps:
            cp.wait()

    return pl.pallas_call(
        body, name=name, in_specs=[ANY] * n, out_specs=[ANY] * n,
        out_shape=[jax.ShapeDtypeStruct(a.shape, a.dtype) for a in arrs],
        scratch_shapes=[pltpu.SemaphoreType.DMA((n,)), pltpu.SemaphoreType.DMA((n,))],
    )(*arrs)


def _row_tile(rows, cols, budget=2 * 1024 * 1024):
    best = None
    for t in range(16, rows + 1, 16):
        if rows % t == 0 and t * cols * 4 <= budget:
            best = t
    return best if best is not None else rows


def _sum_slots(parts, *, name):
    n, rows, cols = parts.shape
    tr = _row_tile(rows, cols, 1024 * 1024)

    def body(p_ref, o_ref):
        acc = p_ref[0].astype(F32)
        for i in range(1, n):
            acc = acc + p_ref[i].astype(F32)
        o_ref[...] = acc

    return pl.pallas_call(
        body, name=name, grid=(rows // tr,), in_specs=[pl.BlockSpec((n, tr, cols), lambda i: (0, i, 0))],
        out_specs=pl.BlockSpec((tr, cols), lambda i: (i, 0)), out_shape=jax.ShapeDtypeStruct((rows, cols), F32),
        compiler_params=_params("parallel"),
    )(parts)


def _adamw(w, m, v, g_a, g_b, *, name):
    rows, cols = w.shape
    tr = _row_tile(rows, cols)
    two = g_b is not None
    c1 = 1.0 / (1.0 - ADAM_B1 ** ADAM_STEP)
    c2 = 1.0 / (1.0 - ADAM_B2 ** ADAM_STEP)

    def body(*refs):
        w_ref, m_ref, v_ref, ga_ref = refs[:4]
        g_ref, d_ref, nm_ref, nv_ref = refs[4 + two:]
        g = ga_ref[...]
        if two:
            g = g + refs[4][...]
        nm = ADAM_B1 * m_ref[...] + (1.0 - ADAM_B1) * g
        nv = ADAM_B2 * v_ref[...] + (1.0 - ADAM_B2) * (g * g)
        g_ref[...] = g
        nm_ref[...] = nm
        nv_ref[...] = nv
        d_ref[...] = -ADAM_LR * ((nm * c1) / (jnp.sqrt(nv * c2) + ADAM_EPS) + ADAM_WD * w_ref[...])

    blk = pl.BlockSpec((tr, cols), lambda i: (i, 0))
    ins = [w, m, v, g_a] + ([g_b] if two else [])
    return pl.pallas_call(
        body, name=name, grid=(rows // tr,), in_specs=[blk] * len(ins), out_specs=[blk] * 4,
        out_shape=[jax.ShapeDtypeStruct((rows, cols), F32)] * 4, compiler_params=_params("parallel"),
    )(*ins)


def _cast_bf16(w, *, name):
    rows, cols = w.shape
    tr = _row_tile(rows, cols)

    def body(w_ref, o_ref):
        o_ref[...] = w_ref[...].astype(BF16)

    blk = pl.BlockSpec((tr, cols), lambda i: (i, 0))
    return pl.pallas_call(body, name=name, grid=(rows // tr,), in_specs=[blk], out_specs=blk,
                          out_shape=jax.ShapeDtypeStruct((rows, cols), BF16), compiler_params=_params("parallel"))(w)


BIG = ("w_in", "w_branch", "w_out", "w_ff1", "w_ff2")
SMALL = ("norm_mix_pre", "conv_qkv_w", "gdn_a_log", "gdn_dt_bias", "gdn_norm_w", "conv_sc_w", "norm_mix_post",
         "norm_ffn_pre", "norm_ffn_post")
ORDER = ("norm_mix_pre", "w_in", "conv_qkv_w", "gdn_a_log", "gdn_dt_bias", "gdn_norm_w", "conv_sc_w", "w_branch",
         "w_out", "norm_mix_post", "norm_ffn_pre", "w_ff1", "w_ff2", "norm_ffn_post")


_MATMUL_LAYOUT = dict(
    w_in=_in_cols_from_chips,
    w_branch=lambda a: a,
    w_out=lambda a: a.reshape(D_MODEL, D_MODEL),
    w_ff1=lambda a: a,
    w_ff2=lambda a: a.reshape(D_FF, D_MODEL),
)
_SHARD_LAYOUT = dict(
    w_in=_in_cols_to_chips,
    w_branch=lambda g: g,
    w_out=lambda g: g.reshape(N_CHIPS, D_MODEL // N_CHIPS, D_MODEL),
    w_ff1=lambda g: g,
    w_ff2=lambda g: g.reshape(N_CHIPS, D_FF // N_CHIPS, D_MODEL),
)


def _full_weights(big, conv, rep, l):
    p = {k: _MATMUL_LAYOUT[k](a) for k, a in big.items()}
    if conv is not None:
        p["conv_qkv_w"] = conv["conv_qkv_w"][:, l].transpose(1, 0, 2).reshape(4, 3 * BRANCH_W)
        p["conv_sc_w"] = conv["conv_sc_w"][:, l].transpose(1, 0, 2).reshape(3, BRANCH_W)
    if rep is not None:
        for k in ("norm_mix_pre", "gdn_a_log", "gdn_dt_bias", "gdn_norm_w", "norm_mix_post", "norm_ffn_pre", "norm_ffn_post"):
            p[k] = rep[k][l]
    return p


def _partials_by_chip(g, names):
    return [_SHARD_LAYOUT[k](g[k]).astype(BF16) for k in names]


def _pack_small(grads):
    pieces, layout = [], []
    for name in SMALL:
        v = jnp.stack([g[name] for g in grads]).astype(F32)
        layout.append((name, v.shape))
        pieces.append(v.reshape(-1))
    flat = jnp.concatenate(pieces)
    rows = -(-flat.shape[0] // LANES)
    rows = -(-rows // 8) * 8
    flat = jnp.pad(flat, (0, rows * LANES - flat.shape[0]))
    return flat.reshape(rows, LANES), layout


def _unpack_small(table, layout):
    flat, out, off = table.reshape(-1), {}, 0
    for name, shape in layout:
        size = 1
        for d in shape:
            size *= d
        out[name] = flat[off:off + size].reshape(shape)
        off += size
    return out


def _as2d(a):
    return a.reshape(-1, a.shape[-1]) if a.ndim > 1 else a.reshape(1, -1)


def kernel(x, norm_mix_pre, w_in, conv_qkv_w, gdn_a_log, gdn_dt_bias, gdn_norm_w, conv_sc_w, w_branch, w_out, norm_mix_post, norm_ffn_pre, w_ff1, w_ff2, norm_ffn_post, loss_target, m_norm_mix_pre, m_w_in, m_conv_qkv_w, m_gdn_a_log, m_gdn_dt_bias, m_gdn_norm_w, m_conv_sc_w, m_w_branch, m_w_out, m_norm_mix_post, m_norm_ffn_pre, m_w_ff1, m_w_ff2, m_norm_ffn_post, v_norm_mix_pre, v_w_in, v_conv_qkv_w, v_gdn_a_log, v_gdn_dt_bias, v_gdn_norm_w, v_conv_sc_w, v_w_branch, v_w_out, v_norm_mix_post, v_norm_ffn_pre, v_w_ff1, v_w_ff2, v_norm_ffn_post):
    w = dict(norm_mix_pre=norm_mix_pre, w_in=w_in, conv_qkv_w=conv_qkv_w, gdn_a_log=gdn_a_log, gdn_dt_bias=gdn_dt_bias,
             gdn_norm_w=gdn_norm_w, conv_sc_w=conv_sc_w, w_branch=w_branch, w_out=w_out, norm_mix_post=norm_mix_post,
             norm_ffn_pre=norm_ffn_pre, w_ff1=w_ff1, w_ff2=w_ff2, norm_ffn_post=norm_ffn_post)
    m = dict(norm_mix_pre=m_norm_mix_pre, w_in=m_w_in, conv_qkv_w=m_conv_qkv_w, gdn_a_log=m_gdn_a_log,
             gdn_dt_bias=m_gdn_dt_bias, gdn_norm_w=m_gdn_norm_w, conv_sc_w=m_conv_sc_w, w_branch=m_w_branch, w_out=m_w_out,
             norm_mix_post=m_norm_mix_post, norm_ffn_pre=m_norm_ffn_pre, w_ff1=m_w_ff1, w_ff2=m_w_ff2,
             norm_ffn_post=m_norm_ffn_post)
    v = dict(norm_mix_pre=v_norm_mix_pre, w_in=v_w_in, conv_qkv_w=v_conv_qkv_w, gdn_a_log=v_gdn_a_log,
             gdn_dt_bias=v_gdn_dt_bias, gdn_norm_w=v_gdn_norm_w, conv_sc_w=v_conv_sc_w, w_branch=v_w_branch, w_out=v_w_out,
             norm_mix_post=v_norm_mix_post, norm_ffn_pre=v_norm_ffn_pre, w_ff1=v_w_ff1, w_ff2=v_w_ff2,
             norm_ffn_post=v_norm_ffn_post)

    me = _chip_index()

    shards = {k: _cast_bf16(_as2d(w[k]), name=f"cast_{k}").reshape(w[k].shape) for k in BIG}
    conv_names = ("conv_qkv_w", "conv_sc_w")
    FIRST, REST = ("w_in",), ("w_branch", "w_out", "w_ff1", "w_ff2")

    def gather_start(l, names, tag, after):
        srcs = [shards[k][l] for k in names] + ([w[k] for k in conv_names] if (l == 0 and "w_in" in names) else [])
        return _exchange_start(srcs, by_slot=False, name=f"gather_start{l}{tag}", after=after)

    def gather_land(ex, names, l, tag, after):
        own, lands = _exchange_wait(ex, after, by_slot=False, name=f"gather_wait{l}{tag}")
        return {k: _place_own(land, o, name=f"own_{k}") for k, land, o in zip(names, lands, own)}

    gathers = {0: gather_start(0, FIRST, "a", None)}
    conv = {}

    def weights_of(l, x_l):
        if l > 0:
            full = gather_land(gathers[l], BIG, l, "", x_l)
            p = _full_weights(full, conv, w, l)
            if l + 1 < DEPTH:
                gathers[l + 1] = gather_start(l + 1, BIG, "", full["w_out"])
                p["norm_mix_pre"] = p["norm_mix_pre"] + gathers[l + 1]["token"][0, 0]
            return p, None
        full = gather_land(gathers[0], FIRST + conv_names, 0, "a", gathers[0]["token"])
        conv.update({k: full[k] for k in conv_names})
        p = _full_weights({"w_in": full["w_in"]}, conv, w, 0)
        rest = gather_start(0, REST, "b", full["conv_sc_w"])
        p["norm_mix_pre"] = p["norm_mix_pre"] + rest["token"][0, 0]

        def late(after):
            arrived = gather_land(rest, REST, 0, "b", after)
            q = _full_weights(arrived, None, None, 0)
            gathers[1] = gather_start(1, BIG, "", arrived["w_out"])
            q["norm_mix_post"] = p["norm_mix_post"] + gathers[1]["token"][0, 0]
            return q

        return p, late

    grads, scatters = [None] * DEPTH, {}

    def scatter_start(l, g, names, tag):
        scatters[l, names] = _exchange_start(_partials_by_chip(g, names), by_slot=True, name=f"scatter_start{l}{tag}")
        return scatters[l, names]["token"][0, 0]

    def grads_early(g):
        return scatter_start(0, g, REST, "a")

    def grads_done(l, g):
        grads[l] = g
        return scatter_start(l, g, FIRST, "b") if l == 0 else scatter_start(l, g, BIG, "")

    loss, dx = _local_step(x[0], loss_target[0], DEPTH, weights_of, grads_done, grads_early)
    loss = lax.psum(loss, ("x", "y", "c"))

    sums = [dict() for _ in range(DEPTH)]
    for (l, names), ex in sorted(scatters.items(), key=lambda kv: (-kv[0][0], kv[0][1] != REST)):
        tag = "" if names == BIG else ("a" if names == REST else "b")
        parts, lands = _exchange_wait(ex, dx, by_slot=True, name=f"scatter_wait{l}{tag}")
        for k, r, o in zip(names, lands, parts):
            sums[l][k] = _sum_partials(r.reshape(N_CHIPS, -1, r.shape[-1]), o.reshape(N_CHIPS, -1, o.shape[-1]), name=f"sum_{k}")
    mine = [jnp.concatenate([sums[l][k] for l in range(DEPTH)], axis=0) for k in BIG]
    theirs = _swap_sibling(mine, name="swap_sibling")
    small, layout = _pack_small(grads)
    small_g = _unpack_small(_sum_slots(_gather_devices(small, name="gather_small"), name="sum_small"), layout)
    for k, width in (("conv_qkv_w", 3 * BRANCH_W // N_CHIPS), ("conv_sc_w", BRANCH_W // N_CHIPS)):
        small_g[k] = lax.dynamic_slice_in_dim(small_g[k], me * width, width, axis=2)

    out = {}
    for k, s_mine, s_theirs in zip(BIG, mine, theirs):
        res = _adamw(_as2d(w[k]), _as2d(m[k]), _as2d(v[k]), s_mine, s_theirs, name=f"adamw_{k}")
        out[k] = [r.reshape(w[k].shape) for r in res]
    for k in SMALL:
        res = _adamw(_as2d(w[k]), _as2d(m[k]), _as2d(v[k]), _as2d(small_g[k]), None, name=f"adamw_{k}")
        out[k] = [r.reshape(w[k].shape) for r in res]
    return (loss, dx[None], *[out[k][0] for k in ORDER], *[out[k][1] for k in ORDER], *[out[k][2] for k in ORDER],
            *[out[k][3] for k in ORDER])
```

```python
import functools

import jax
import jax.numpy as jnp
from jax import lax
from jax.experimental import pallas as pl
from jax.experimental.pallas import tpu as pltpu

F32 = jnp.float32
BF16 = jnp.bfloat16
MESH = pl.DeviceIdType.MESH

LANES = 128
D_MODEL = 1024
DEPTH = 4
CHUNK = 64
GDN_CHUNKS_PER_STEP = 2
GDN_HEADS, GDN_DIM = 4, 128
SB_HEADS, SB_DIM = 8, 64
BRANCH_W = 512
N_BRANCH = 3
D_FF = 4 * D_MODEL
EPS = 1e-6
IN_W = 8200
AB_COL = 2048
AB_PAD = LANES - 8
IN_WP = IN_W + AB_PAD
N_CHIPS = 4
N_DEV = 8
GATES_COL = 5128
PB_GATES, PB_QKV, PB_GATE, PB_AB, PB_SB, PB_SCX, PB_SCB, PB_SCC = 0, 24, 36, 40, 41, 53, 57, 61
SB_TILE = 128
SB_GROUP = 4
SB_QTILE = 256
SB_SCALE = SB_DIM ** -0.5
SB_SCALE2 = SB_SCALE * 1.4426950408889634
GDN_QSCALE = GDN_DIM ** -0.5
VMEM_LIMIT = 56 * 1024 * 1024

ADAM_LR, ADAM_B1, ADAM_B2, ADAM_EPS, ADAM_WD, ADAM_STEP = 0.001, 0.9, 0.999, 1e-08, 0.01, 10

NT = (((1,), (1,)), ((), ()))
TN = (((0,), (0,)), ((), ()))
HI = lax.Precision.HIGH


def _pad_in_cols(w):
    return jnp.concatenate([w[:, GATES_COL:], w[:, :AB_COL + 8], jnp.zeros((w.shape[0], AB_PAD), w.dtype),
                            w[:, AB_COL + 8:GATES_COL]], axis=1)


def _unpad_in_cols(g):
    n_gates = IN_W - GATES_COL
    return jnp.concatenate([g[:, n_gates:n_gates + AB_COL + 8], g[:, n_gates + AB_COL + 8 + AB_PAD:], g[:, :n_gates]], axis=1)


IN_SHARD = IN_W // N_CHIPS
_IN_SEGMENTS = ((0, AB_COL + 8, IN_W - GATES_COL), (AB_COL + 8, GATES_COL, IN_W - GATES_COL + AB_PAD),
                (GATES_COL, IN_W, -GATES_COL))


def _in_cols_from_chips(slots):
    def cols(first, last):
        out = []
        for j in range(N_CHIPS):
            lo, hi = max(first, j * IN_SHARD), min(last, (j + 1) * IN_SHARD)
            if lo < hi:
                out.append(slots[j][:, lo - j * IN_SHARD:hi - j * IN_SHARD])
        return out

    head, tail, gates = (cols(first, last) for first, last, _ in _IN_SEGMENTS)
    return jnp.concatenate(gates + head + [jnp.zeros((slots.shape[1], AB_PAD), slots.dtype)] + tail, axis=1)


def _in_cols_to_chips(g):
    shards = []
    for j in range(N_CHIPS):
        pieces = []
        for first, last, shift in _IN_SEGMENTS:
            lo, hi = max(first, j * IN_SHARD), min(last, (j + 1) * IN_SHARD)
            if lo < hi:
                pieces.append(g[:, lo + shift:hi + shift])
        shards.append(jnp.concatenate(pieces, axis=1))
    return jnp.stack(shards)


def _params(*sem):
    return pltpu.CompilerParams(dimension_semantics=sem if sem else None, vmem_limit_bytes=VMEM_LIMIT)


def _sigmoid(x):
    return 1.0 / (1.0 + jnp.exp(-x))


def _softplus(x):
    return jnp.maximum(x, 0.0) + jnp.log(1.0 + jnp.exp(-jnp.abs(x)))


def _softplus2(x):
    return jnp.maximum(x, 0.0) + jnp.log2(1.0 + jnp.exp2(-jnp.abs(x)))


def _dot(a, b, dims=None, precision=None):
    if dims is None:
        return jnp.dot(a, b, preferred_element_type=F32, precision=precision)
    return lax.dot_general(a, b, dims, preferred_element_type=F32, precision=precision)


def _bdot(a, b, dims=None):
    return _dot(a.astype(BF16), b.astype(BF16), dims)


def _matmul(a, b, *, name, ta=False, tb=False, tm, tn, tk=None, outs=(F32,), epi=None, extras=(), b_chips=False,
            out_chips=False):
    if ta:
        kdim, m = a.shape
    else:
        m, kdim = a.shape
    if b_chips:
        per = b.shape[2]
        if tb:
            n, kb = b.shape[1], N_CHIPS * per
        else:
            kb, n = b.shape[1], N_CHIPS * per
    elif tb:
        n, kb = b.shape
    else:
        kb, n = b.shape
    assert kdim == kb, (a.shape, b.shape)
    tk = kdim if tk is None else min(tk, kdim)
    tm = min(tm, m)
    assert m % tm == 0 and n % tn == 0 and kdim % tk == 0, (m, n, kdim, tm, tn, tk)
    nk = kdim // tk
    ni, nj = m // tm, n // tn
    a_bytes, b_bytes = a.size * a.dtype.itemsize, b.size * b.dtype.itemsize
    cols_inner = a_bytes * (nj if nk > 1 else 1) + b_bytes * (ni if nj * nk > 1 else 1)
    rows_inner = b_bytes * (ni if nk > 1 else 1) + a_bytes * (nj if ni * nk > 1 else 1)
    swap = rows_inner < cols_inner

    def ix(f):
        return (lambda g0, g1, k: f(g1, g0, k)) if swap else f

    a_spec = pl.BlockSpec((tk, tm), ix(lambda i, j, k: (k, i))) if ta else pl.BlockSpec((tm, tk), ix(lambda i, j, k: (i, k)))
    if b_chips and tb:
        assert per % tk == 0
        b_spec = pl.BlockSpec((None, tn, tk), ix(lambda i, j, k: (k // (per // tk), j, k % (per // tk))))
    elif b_chips:
        assert per % tn == 0
        b_spec = pl.BlockSpec((None, tk, tn), ix(lambda i, j, k: (j // (per // tn), k, j % (per // tn))))
    elif tb:
        b_spec = pl.BlockSpec((tn, tk), ix(lambda i, j, k: (j, k)))
    else:
        b_spec = pl.BlockSpec((tk, tn), ix(lambda i, j, k: (k, j)))
    mn_spec = pl.BlockSpec((tm, tn), ix(lambda i, j, k: (i, j)))
    if out_chips:
        per_o = n // N_CHIPS
        assert per_o % tn == 0
        out_spec = pl.BlockSpec((None, tm, tn), ix(lambda i, j, k: (j // (per_o // tn), i, j % (per_o // tn))))
        out_dims = (N_CHIPS, m, per_o)
    else:
        out_spec, out_dims = mn_spec, (m, n)
    dims = (((0 if ta else 1,), (1 if tb else 0,)), ((), ()))
    n_ex, n_out = len(extras), len(outs)

    def body(a_ref, b_ref, *rest):
        ex, o, acc = rest[:n_ex], rest[n_ex:n_ex + n_out], rest[n_ex + n_out:]
        part = lax.dot_general(a_ref[...].astype(BF16), b_ref[...].astype(BF16), dims, preferred_element_type=F32)

        def finish(val):
            res = epi(val, *[e[...] for e in ex]) if epi is not None else (val,)
            for r, oref in zip(res, o):
                oref[...] = r.astype(oref.dtype)

        if nk == 1:
            finish(part)
        else:
            k = pl.program_id(2)

            @pl.when(k == 0)
            def _():
                acc[0][...] = part

            @pl.when(k > 0)
            def _():
                acc[0][...] += part

            @pl.when(k == nk - 1)
            def _():
                finish(acc[0][...])

    res = pl.pallas_call(
        body, name=name, grid=(nj, ni, nk) if swap else (ni, nj, nk),
        in_specs=[a_spec, b_spec] + [mn_spec] * n_ex,
        out_specs=[out_spec] * n_out,
        out_shape=[jax.ShapeDtypeStruct(out_dims, dt) for dt in outs],
        scratch_shapes=[pltpu.VMEM((tm, tn), F32)] if nk > 1 else [],
        compiler_params=_params("parallel", "parallel", "arbitrary"),
    )(a, b, *extras)
    return res[0] if n_out == 1 else res


ROW_TILE = 512


def _norm_fwd(x, w, *, name):
    s, d = x.shape

    def body(x_ref, w_ref, o_ref):
        xv = x_ref[...]
        r = lax.rsqrt(jnp.mean(xv * xv, axis=-1, keepdims=True) + EPS)
        o_ref[...] = (xv * r * w_ref[...]).astype(o_ref.dtype)

    return pl.pallas_call(
        body, name=name, grid=(s // ROW_TILE,),
        in_specs=[pl.BlockSpec((ROW_TILE, d), lambda i: (i, 0)), pl.BlockSpec((1, d), lambda i: (0, 0))],
        out_specs=pl.BlockSpec((ROW_TILE, d), lambda i: (i, 0)),
        out_shape=jax.ShapeDtypeStruct((s, d), BF16), compiler_params=_params("parallel"),
    )(x, w.reshape(1, d))


def _resnorm_fwd(x, u, w, *, name):
    s, d = x.shape

    def body(x_ref, u_ref, w_ref, o_ref):
        uv = u_ref[...]
        r = lax.rsqrt(jnp.mean(uv * uv, axis=-1, keepdims=True) + EPS)
        o_ref[...] = x_ref[...] + uv * r * w_ref[...]

    row = pl.BlockSpec((ROW_TILE, d), lambda i: (i, 0))
    return pl.pallas_call(
        body, name=name, grid=(s // ROW_TILE,),
        in_specs=[row, row, pl.BlockSpec((1, d), lambda i: (0, 0))], out_specs=row,
        out_shape=jax.ShapeDtypeStruct((s, d), F32), compiler_params=_params("parallel"),
    )(x, u, w.reshape(1, d))


def _norm_bwd(xin, w, dy, res, *, out_dtype, name):
    s, d = xin.shape
    has_res = res is not None

    def body(*refs):
        x_ref, w_ref, dy_ref = refs[:3]
        res_ref = refs[3] if has_res else None
        dx_ref, dw_ref = refs[3 + has_res:]
        xv, dyv = x_ref[...], dy_ref[...].astype(F32)
        r = lax.rsqrt(jnp.mean(xv * xv, axis=-1, keepdims=True) + EPS)
        xh = xv * r
        g = dyv * w_ref[...]
        dx = r * (g - xh * jnp.mean(g * xh, axis=-1, keepdims=True))
        if has_res:
            dx = dx + res_ref[...]
        dx_ref[...] = dx.astype(dx_ref.dtype)

        @pl.when(pl.program_id(0) == 0)
        def _():
            dw_ref[...] = jnp.zeros_like(dw_ref)

        dw_ref[...] += jnp.sum(dyv * xh, axis=0, keepdims=True)

    row = pl.BlockSpec((ROW_TILE, d), lambda i: (i, 0))
    vec = pl.BlockSpec((1, d), lambda i: (0, 0))
    ins = [xin, w.reshape(1, d), dy] + ([res] if has_res else [])
    dx, dw = pl.pallas_call(
        body, name=name, grid=(s // ROW_TILE,),
        in_specs=[row, vec, row] + ([row] if has_res else []), out_specs=[row, vec],
        out_shape=[jax.ShapeDtypeStruct((s, d), out_dtype), jax.ShapeDtypeStruct((1, d), F32)],
        compiler_params=_params("arbitrary"),
    )(*ins)
    return dx, dw.reshape(d)


def _loss_fwd_bwd(y, target, *, name):
    s, d = y.shape

    def body(y_ref, t_ref, loss_ref, dy_ref):
        e = y_ref[...] - t_ref[...]
        dy_ref[...] = e * (1.0 / d)

        @pl.when(pl.program_id(0) == 0)
        def _():
            loss_ref[...] = jnp.zeros_like(loss_ref)

        part = jnp.sum(jnp.sum(e * e, axis=1, keepdims=True), axis=0, keepdims=True)
        loss_ref[...] += part * (0.5 / d)

    row = pl.BlockSpec((ROW_TILE, d), lambda i: (i, 0))
    loss, dy = pl.pallas_call(
        body, name=name, grid=(s // ROW_TILE,), in_specs=[row, row],
        out_specs=[pl.BlockSpec((1, 1), lambda i: (0, 0)), row],
        out_shape=[jax.ShapeDtypeStruct((1, 1), F32), jax.ShapeDtypeStruct((s, d), F32)],
        compiler_params=_params("arbitrary"),
    )(y, target)
    return loss[0, 0], dy


def _shift_down(x, k, rows):
    if k == 0:
        return x
    return jnp.where(rows >= k, pltpu.roll(x, k, 0), 0.0)


def _shift_up(x, k, rows):
    if k == 0:
        return x
    n = x.shape[0]
    return jnp.where(rows < n - k, pltpu.roll(x, n - k, 0), 0.0)


def _col_spec(s, base):
    return pl.BlockSpec((s, LANES), lambda j: (0, base + j))


def _gdn_pre_math(x, w, j, rows):
    taps = w.shape[0]
    c = w[taps - 1:taps, :] * x
    for i in range(taps - 1):
        c = c + w[i:i + 1, :] * _shift_down(x, taps - 1 - i, rows)
    sg = _sigmoid(c)
    y = c * sg
    r = lax.rsqrt(jnp.sum(y * y, axis=-1, keepdims=True) + EPS)
    is_qk = j < 2 * GDN_HEADS
    scale = jnp.where(j < GDN_HEADS, GDN_QSCALE, 1.0)
    return c, sg, y, r, is_qk, scale


def _gdn_pre_fwd(proj, conv_w, *, name):
    s = proj.shape[0]

    def body(x_ref, w_ref, o_ref):
        j = pl.program_id(0)
        rows = lax.broadcasted_iota(jnp.int32, (s, LANES), 0)
        _, _, y, r, is_qk, scale = _gdn_pre_math(x_ref[...], w_ref[...], j, rows)
        o_ref[...] = jnp.where(is_qk, y * (r * scale), y)

    return pl.pallas_call(
        body, name=name, grid=(12,),
        in_specs=[_col_spec(s, PB_QKV), pl.BlockSpec((4, LANES), lambda j: (0, j))],
        out_specs=_col_spec(s, 0), out_shape=jax.ShapeDtypeStruct((s, 3 * BRANCH_W), F32),
        compiler_params=_params("parallel"),
    )(proj, conv_w)


def _gdn_pre_bwd(proj, conv_w, dqkvn, *, name):
    s = proj.shape[0]

    def body(x_ref, w_ref, d_ref, dx_ref, dw_ref):
        j = pl.program_id(0)
        rows = lax.broadcasted_iota(jnp.int32, (s, LANES), 0)
        x, w, dout = x_ref[...], w_ref[...], d_ref[...]
        c, sg, y, r, is_qk, scale = _gdn_pre_math(x, w, j, rows)
        yh = y * r
        dy_n = (scale * r) * (dout - yh * jnp.sum(dout * yh, axis=-1, keepdims=True))
        dy = jnp.where(is_qk, dy_n, dout)
        dc = dy * (sg * (1.0 + c * (1.0 - sg)))
        taps = w.shape[0]
        dx = w[taps - 1:taps, :] * dc
        dws = []
        for i in range(taps - 1):
            k = taps - 1 - i
            dx = dx + w[i:i + 1, :] * _shift_up(dc, k, rows)
            dws.append(jnp.sum(dc * _shift_down(x, k, rows), axis=0, keepdims=True))
        dws.append(jnp.sum(dc * x, axis=0, keepdims=True))
        dx_ref[...] = dx.astype(dx_ref.dtype)
        for i in range(taps):
            dw_ref[i:i + 1, :] = dws[i]

    return pl.pallas_call(
        body, name=name, grid=(12,),
        in_specs=[_col_spec(s, PB_QKV), pl.BlockSpec((4, LANES), lambda j: (0, j)), _col_spec(s, 0)],
        out_specs=[_col_spec(s, 0), pl.BlockSpec((4, LANES), lambda j: (0, j))],
        out_shape=[jax.ShapeDtypeStruct((s, 3 * BRANCH_W), BF16), jax.ShapeDtypeStruct((4, 3 * BRANCH_W), F32)],
        compiler_params=_params("parallel"),
    )(proj, conv_w, dqkvn)


def _lane_pad(v):
    return jnp.pad(v.reshape(1, -1), ((0, 0), (0, LANES - v.shape[0])))


def _gdn_gates_fwd(proj, a_log, dt_bias, *, name):
    s = proj.shape[0]

    def body(ab_ref, al_ref, dt_ref, o_ref):
        ab = ab_ref[...]
        lane = lax.broadcasted_iota(jnp.int32, (1, LANES), 1)
        g = -jnp.exp(al_ref[...]) * _softplus(ab + dt_ref[...])
        o_ref[...] = jnp.where(lane < GDN_HEADS, g, _sigmoid(ab))

    vec = pl.BlockSpec((1, LANES), lambda j: (0, 0))
    return pl.pallas_call(
        body, name=name, grid=(1,), in_specs=[_col_spec(s, PB_AB), vec, vec], out_specs=_col_spec(s, 0),
        out_shape=jax.ShapeDtypeStruct((s, LANES), F32), compiler_params=_params("arbitrary"),
    )(proj, _lane_pad(a_log), _lane_pad(dt_bias))


def _gdn_gates_bwd(proj, a_log, dt_bias, dgb, *, name):
    s = proj.shape[0]

    def body(ab_ref, al_ref, dt_ref, d_ref, dab_ref, dal_ref, ddt_ref):
        ab, d = ab_ref[...], d_ref[...]
        lane = lax.broadcasted_iota(jnp.int32, (1, LANES), 1)
        ea = jnp.exp(al_ref[...])
        pre = ab + dt_ref[...]
        g = -ea * _softplus(pre)
        dpre = d * (-ea) * _sigmoid(pre)
        beta = _sigmoid(ab)
        is_g = lane < GDN_HEADS
        dab = jnp.where(is_g, dpre, jnp.where(lane < 2 * GDN_HEADS, d * beta * (1.0 - beta), 0.0))
        dab_ref[...] = dab.astype(dab_ref.dtype)
        dal_ref[...] = jnp.sum(jnp.where(is_g, d * g, 0.0), axis=0, keepdims=True)
        ddt_ref[...] = jnp.sum(jnp.where(is_g, dpre, 0.0), axis=0, keepdims=True)

    vec = pl.BlockSpec((1, LANES), lambda j: (0, 0))
    dab, dal, ddt = pl.pallas_call(
        body, name=name, grid=(1,), in_specs=[_col_spec(s, PB_AB), vec, vec, _col_spec(s, 0)],
        out_specs=[_col_spec(s, 0), vec, vec],
        out_shape=[jax.ShapeDtypeStruct((s, LANES), BF16), jax.ShapeDtypeStruct((1, LANES), F32),
                   jax.ShapeDtypeStruct((1, LANES), F32)],
        compiler_params=_params("arbitrary"),
    )(proj, _lane_pad(a_log), _lane_pad(dt_bias), dgb)
    return dab, dal[0, :GDN_HEADS], ddt[0, :GDN_HEADS]


def _interleave(gens):
    results, live = [None] * len(gens), list(range(len(gens)))
    while live:
        for idx in list(live):
            try:
                next(gens[idx])
            except StopIteration as done:
                results[idx] = done.value
                live.remove(idx)
    return results


def _chunk_common(q, k, v, gb, gbt, h):
    c = CHUNK
    row = lax.broadcasted_iota(jnp.int32, (c, c), 0)
    col = lax.broadcasted_iota(jnp.int32, (c, c), 1)
    tril, strict, eye = row >= col, row > col, row == col
    lane = lax.broadcasted_iota(jnp.int32, (c, LANES), 1)
    sub = lax.broadcasted_iota(jnp.int32, (2 * GDN_HEADS, c), 0)
    g_col = jnp.sum(jnp.where(lane == h, gb, 0.0), axis=1, keepdims=True)
    beta_col = jnp.sum(jnp.where(lane == GDN_HEADS + h, gb, 0.0), axis=1, keepdims=True)
    g_row = jnp.sum(jnp.where(sub == h, gbt, 0.0), axis=0, keepdims=True)
    gc_col = jnp.sum(jnp.where(tril, jnp.broadcast_to(g_row, (c, c)), 0.0), axis=1, keepdims=True)
    gc_row = jnp.sum(jnp.where(row <= col, jnp.broadcast_to(g_col, (c, c)), 0.0), axis=0, keepdims=True)
    g_tot = jnp.sum(g_row, axis=1, keepdims=True)
    dm = jnp.exp(jnp.where(tril, gc_col - gc_row, -1e30))
    e_col = jnp.exp(gc_col)
    kdec_col = jnp.exp(g_tot - gc_col)
    gamma = jnp.exp(g_tot)
    kb = k * beta_col
    vb = v * beta_col
    kbg = kb * e_col
    kk = _bdot(kb, k, NT)
    qk = _bdot(q, k, NT)
    yield
    a = jnp.where(strict, kk * dm, 0.0)
    aqk = jnp.where(tril, qk * dm, 0.0)
    bneg = -a
    t = jnp.where(eye, 1.0, 0.0) + bneg
    p = _dot(bneg, bneg, precision=HI)
    yield
    for lvl in range(5):
        t_next = t + _dot(t, p, precision=HI)
        if lvl < 4:
            p = _dot(p, p, precision=HI)
        t = t_next
        yield
    u = _dot(t, vb, precision=HI)
    w = _dot(t, kbg, precision=HI)
    yield
    return dict(tril=tril, strict=strict, eye=eye, row=row, col=col, beta_col=beta_col, dm=dm, e_col=e_col,
                kdec_col=kdec_col, gamma=gamma, kb=kb, vb=vb, kbg=kbg, a=a, t=t, u=u, w=w, aqk=aqk,
                qd=q * e_col, kd=k * kdec_col)


def _gdn_chunk_fwd(qkvn, gb, gbt, *, name):
    s = qkvn.shape[0]
    n_chunks = s // CHUNK
    per = GDN_CHUNKS_PER_STEP
    rows_per = per * CHUNK

    def body(q_ref, k_ref, v_ref, gb_ref, gbt_ref, o_ref, st_ref, state):
        @pl.when(pl.program_id(0) == 0)
        def _():
            state[...] = jnp.zeros_like(state)

        def head(h, ci):
            hs = slice(h * GDN_DIM, (h + 1) * GDN_DIM)
            rows = slice(ci * CHUNK, (ci + 1) * CHUNK)
            q, k, v = q_ref[rows, hs], k_ref[rows, hs], v_ref[rows, hs]
            m = yield from _chunk_common(q, k, v, gb_ref[rows, :], gbt_ref[ci], h)
            for _ in range(ci):
                yield
            s0 = state[h]
            st_ref[ci, h] = s0
            vnew = m["u"] - _bdot(m["w"], s0)
            o_inter = _bdot(m["qd"], s0)
            yield
            o_ref[rows, hs] = o_inter + _bdot(m["aqk"], vnew)
            state[h] = m["gamma"] * s0 + _bdot(m["kd"], vnew, TN)

        _interleave([head(h, ci) for ci in range(per) for h in range(GDN_HEADS)])

    blk = lambda j: pl.BlockSpec((rows_per, BRANCH_W), lambda n: (n, j))
    return pl.pallas_call(
        body, name=name, grid=(n_chunks // per,),
        in_specs=[blk(0), blk(1), blk(2), pl.BlockSpec((rows_per, LANES), lambda n: (n, 0)),
                  pl.BlockSpec((per, 2 * GDN_HEADS, CHUNK), lambda n: (n, 0, 0))],
        out_specs=[blk(0), pl.BlockSpec((per, GDN_HEADS, GDN_DIM, GDN_DIM), lambda n: (n, 0, 0, 0))],
        out_shape=[jax.ShapeDtypeStruct((s, BRANCH_W), F32),
                   jax.ShapeDtypeStruct((n_chunks, GDN_HEADS, GDN_DIM, GDN_DIM), F32)],
        scratch_shapes=[pltpu.VMEM((GDN_HEADS, GDN_DIM, GDN_DIM), F32)],
        compiler_params=_params("arbitrary"),
    )(qkvn, qkvn, qkvn, gb, gbt)


def _gdn_chunk_bwd(qkvn, gb, gbt, states, do, *, name):
    s = qkvn.shape[0]
    n_chunks = s // CHUNK
    c = CHUNK
    per = GDN_CHUNKS_PER_STEP
    rows_per = per * CHUNK

    def body(q_ref, k_ref, v_ref, gb_ref, gbt_ref, st_ref, do_ref, dqkv_ref, dgb_ref, dstate):
        @pl.when(pl.program_id(0) == 0)
        def _():
            dstate[...] = jnp.zeros_like(dstate)

        lane = lax.broadcasted_iota(jnp.int32, (c, LANES), 1)

        def head(h, ci):
            hs = slice(h * GDN_DIM, (h + 1) * GDN_DIM)
            rows = slice(ci * CHUNK, (ci + 1) * CHUNK)
            q, k, v, dov = q_ref[rows, hs], k_ref[rows, hs], v_ref[rows, hs], do_ref[rows, hs]
            m = yield from _chunk_common(q, k, v, gb_ref[rows, :], gbt_ref[ci], h)
            for _ in range(per - 1 - ci):
                yield
            tril, strict, eye, row, col = m["tril"], m["strict"], m["eye"], m["row"], m["col"]
            s0, ds1 = st_ref[ci, h], dstate[h]
            vnew = m["u"] - _bdot(m["w"], s0)
            dvnew_a = _bdot(m["aqk"], dov, TN) + _bdot(m["kd"], ds1)
            dqd = _bdot(dov, s0, NT)
            ds_q = _bdot(m["qd"], dov, TN)
            dgamma = jnp.sum(jnp.sum(s0 * ds1, axis=1, keepdims=True), axis=0, keepdims=True)
            yield
            dvnew = dvnew_a
            daqk = jnp.where(tril, _bdot(dov, vnew, NT), 0.0)
            dkd = _bdot(vnew, ds1, NT)
            dw = -_bdot(dvnew, s0, NT)
            dstate[h] = m["gamma"] * ds1 + ds_q - _bdot(m["w"], dvnew, TN)
            dvb = _dot(m["t"], dvnew, TN, HI)
            yield
            dt = _dot(dvnew, m["vb"], NT, HI) + _dot(dw, m["kbg"], NT, HI)
            dkbg = _dot(m["t"], dw, TN, HI)
            dmq = daqk * m["dm"]
            dq = _bdot(dmq, k) + dqd * m["e_col"]
            dk_q = _bdot(dmq, q, TN)
            yield
            tdt = _dot(m["t"], dt, TN, HI)
            yield
            da = jnp.where(strict, -_dot(tdt, m["t"], NT, HI), 0.0)
            yield
            dmat = da * m["dm"]
            dkb = _bdot(dmat, k) + dkbg * m["e_col"]
            dk = (_bdot(dmat, m["kb"], TN) + dk_q + dkd * m["kdec_col"] + m["beta_col"] * dkb)
            yield
            dbeta_col = jnp.sum(dkb * k, axis=1, keepdims=True) + jnp.sum(dvb * v, axis=1, keepdims=True)
            e = da * m["a"] + daqk * m["aqk"]
            rs_kd = jnp.sum(dkd * m["kd"], axis=1, keepdims=True)
            e_colsum = jnp.sum(e, axis=0, keepdims=True)
            e_colsum_c = jnp.sum(jnp.where(eye, jnp.broadcast_to(e_colsum, (c, c)), 0.0), axis=1, keepdims=True)
            dgc = (jnp.sum(e, axis=1, keepdims=True) - e_colsum_c + jnp.sum(dqd * m["qd"], axis=1, keepdims=True)
                   - rs_kd + jnp.sum(dkbg * m["kbg"], axis=1, keepdims=True))
            last = jnp.sum(rs_kd, axis=0, keepdims=True) + dgamma * m["gamma"]
            dgc = dgc + jnp.where(lax.broadcasted_iota(jnp.int32, (c, 1), 0) == c - 1, last, 0.0)
            dgc_row = jnp.sum(jnp.where(eye, jnp.broadcast_to(dgc, (c, c)), 0.0), axis=0, keepdims=True)
            dg_col = jnp.sum(jnp.where(col >= row, jnp.broadcast_to(dgc_row, (c, c)), 0.0), axis=1, keepdims=True)
            for part, val in enumerate((dq, dk, m["beta_col"] * dvb)):
                lo = part * BRANCH_W + h * GDN_DIM
                dqkv_ref[rows, lo:lo + GDN_DIM] = val
            return jnp.where(lane == h, dg_col, 0.0) + jnp.where(lane == GDN_HEADS + h, dbeta_col, 0.0)

        order = [ci for ci in reversed(range(per))]
        parts = _interleave([head(h, ci) for ci in order for h in range(GDN_HEADS)])
        for pos, ci in enumerate(order):
            mine = parts[pos * GDN_HEADS:(pos + 1) * GDN_HEADS]
            dgb_ref[ci * CHUNK:(ci + 1) * CHUNK, :] = (mine[0] + mine[1]) + (mine[2] + mine[3])

    rev = lambda n: n_chunks // per - 1 - n
    blk = lambda j: pl.BlockSpec((rows_per, BRANCH_W), lambda n: (rev(n), j))
    return pl.pallas_call(
        body, name=name, grid=(n_chunks // per,),
        in_specs=[blk(0), blk(1), blk(2), pl.BlockSpec((rows_per, LANES), lambda n: (rev(n), 0)),
                  pl.BlockSpec((per, 2 * GDN_HEADS, CHUNK), lambda n: (rev(n), 0, 0)),
                  pl.BlockSpec((per, GDN_HEADS, GDN_DIM, GDN_DIM), lambda n: (rev(n), 0, 0, 0)), blk(0)],
        out_specs=[pl.BlockSpec((rows_per, 3 * BRANCH_W), lambda n: (rev(n), 0)),
                   pl.BlockSpec((rows_per, LANES), lambda n: (rev(n), 0))],
        out_shape=[jax.ShapeDtypeStruct((s, 3 * BRANCH_W), F32), jax.ShapeDtypeStruct((s, LANES), F32)],
        scratch_shapes=[pltpu.VMEM((GDN_HEADS, GDN_DIM, GDN_DIM), F32)],
        compiler_params=_params("arbitrary"),
    )(qkvn, qkvn, qkvn, gb, gbt, states, do)


def _gdn_post_fwd(o, proj, norm_w, *, name):
    s = o.shape[0]

    def body(o_ref, g_ref, w_ref, y_ref):
        ov, gv = o_ref[...], g_ref[...]
        r = lax.rsqrt(jnp.mean(ov * ov, axis=-1, keepdims=True) + EPS)
        y_ref[...] = (ov * r * w_ref[...] * (gv * _sigmoid(gv))).astype(y_ref.dtype)

    return pl.pallas_call(
        body, name=name, grid=(GDN_HEADS,),
        in_specs=[_col_spec(s, 0), _col_spec(s, PB_GATE), pl.BlockSpec((1, LANES), lambda j: (0, 0))],
        out_specs=_col_spec(s, 0), out_shape=jax.ShapeDtypeStruct((s, BRANCH_W), BF16),
        compiler_params=_params("parallel"),
    )(o, proj, norm_w.reshape(1, GDN_DIM))


def _gdn_post_bwd(o, proj, norm_w, dy, *, name):
    s = o.shape[0]

    def body(o_ref, g_ref, w_ref, dy_ref, do_ref, dg_ref, dw_ref):
        ov, gv, w, dyv = o_ref[...], g_ref[...], w_ref[...], dy_ref[...].astype(F32)
        r = lax.rsqrt(jnp.mean(ov * ov, axis=-1, keepdims=True) + EPS)
        oh = ov * r
        sg = _sigmoid(gv)
        silu = gv * sg
        dn = dyv * silu
        dg_ref[...] = (dyv * (oh * w) * (sg * (1.0 + gv * (1.0 - sg)))).astype(dg_ref.dtype)

        @pl.when(pl.program_id(0) == 0)
        def _():
            dw_ref[...] = jnp.zeros_like(dw_ref)

        dw_ref[...] += jnp.sum(dn * oh, axis=0, keepdims=True)
        g2 = dn * w
        do_ref[...] = r * (g2 - oh * jnp.mean(g2 * oh, axis=-1, keepdims=True))

    do, dg, dw = pl.pallas_call(
        body, name=name, grid=(GDN_HEADS,),
        in_specs=[_col_spec(s, 0), _col_spec(s, PB_GATE), pl.BlockSpec((1, LANES), lambda j: (0, 0)), _col_spec(s, 0)],
        out_specs=[_col_spec(s, 0), _col_spec(s, 0), pl.BlockSpec((1, LANES), lambda j: (0, 0))],
        out_shape=[jax.ShapeDtypeStruct((s, BRANCH_W), F32), jax.ShapeDtypeStruct((s, BRANCH_W), BF16),
                   jax.ShapeDtypeStruct((1, LANES), F32)],
        compiler_params=_params("arbitrary"),
    )(o, proj, norm_w.reshape(1, GDN_DIM), dy)
    return do, dg, dw.reshape(LANES)


def _split_terms(x):
    hi = x.astype(BF16)
    lo = (x - hi.astype(F32)).astype(BF16)
    return jnp.concatenate([hi, lo], axis=1)


def _sb_sum_matrix(pred):
    row = lax.broadcasted_iota(jnp.int32, (2 * SB_TILE, 2 * SB_TILE), 0) % SB_TILE
    col = lax.broadcasted_iota(jnp.int32, (2 * SB_TILE, 2 * SB_TILE), 1)
    return jnp.where((col >= SB_TILE) | pred(row, col), 1.0, 0.0).astype(BF16)


def _sb_head_masks():
    lane = lax.broadcasted_iota(jnp.int32, (1, LANES), 1)
    return [(lane < SB_DIM).astype(F32), (lane >= SB_DIM).astype(F32)]


def _sb_fwd(proj, *, name):
    s = proj.shape[0]
    t, tq = SB_TILE, SB_QTILE
    nq = s // tq

    def body(q_ref, k_ref, v_ref, o_ref, tot_ref):
        cmr = lax.broadcasted_iota(jnp.int32, (tq, t), 1) - lax.broadcasted_iota(jnp.int32, (tq, t), 0)
        uo = _sb_sum_matrix(lambda row, col: row > col)
        hm = _sb_head_masks()

        def qloop(i, carry0):
            qs = pl.multiple_of(i * tq, tq)
            qf = q_ref[pl.ds(qs, tq), :] * SB_SCALE2
            qh = [(qf * hm[h]).astype(BF16) for h in range(2)]
            diag = (i * tq) // (SB_GROUP * t)

            def group(g, st, masked):
                ks = pl.multiple_of(g * (SB_GROUP * t), SB_GROUP * t)
                kb = k_ref[pl.ds(ks, SB_GROUP * t), :].astype(BF16)
                vf = v_ref[pl.ds(ks, SB_GROUP * t), :]
                tiles = [(h, j) for h in range(2) for j in range(SB_GROUP)]
                z = [_dot(qh[h], kb, NT) for h in range(2)]
                keep = {j: cmr < i * tq - (g * SB_GROUP + j) * t for j in range(SB_GROUP)} if masked else None
                base, terms = {}, {}
                for h, j in tiles:
                    zj = z[h][:, j * t:(j + 1) * t]
                    sp = _softplus2(zj)
                    base[h, j] = zj - sp
                    terms[h, j] = _split_terms(jnp.where(keep[j], sp, 0.0) if masked else sp)
                sums = {hj: _dot(terms[hj], uo) for hj in tiles}
                acc, new = st[0], []
                for h in range(2):
                    run, att = st[1 + h], [None] * SB_GROUP
                    for j in reversed(range(SB_GROUP)):
                        a = jnp.exp2(base[h, j] - (sums[h, j][:, :t] + run))
                        att[j] = (jnp.where(keep[j], a, 0.0) if masked else a).astype(BF16)
                        run = run + sums[h, j][:, t:]
                    acc = acc + _dot(jnp.concatenate(att, axis=1), (vf * hm[h]).astype(BF16))
                    new.append(run)
                return (acc, *new)

            zero = jnp.zeros((tq, LANES), F32)
            st = group(diag, (zero, zero, zero), True)
            st = lax.fori_loop(0, diag, lambda jj, sv: group(diag - 1 - jj, sv, False), st)
            o_ref[pl.ds(qs, tq), :] = st[0]
            tot_ref[pl.ds(qs, tq), :] = st[1] * hm[0] + st[2] * hm[1]
            return carry0

        lax.fori_loop(0, nq, qloop, 0)

    out = jax.ShapeDtypeStruct((s, BRANCH_W), F32)
    return pl.pallas_call(
        body, name=name, grid=(SB_HEADS // 2,),
        in_specs=[_col_spec(s, PB_SB), _col_spec(s, PB_SB + 4), _col_spec(s, PB_SB + 8)],
        out_specs=[_col_spec(s, 0)] * 2, out_shape=[out] * 2,
        compiler_params=_params("parallel"),
    )(proj, proj, proj)


def _sb_bwd(proj, tot, do, *, name):
    s = proj.shape[0]
    t, tq = SB_TILE, 2 * SB_QTILE
    nq = s // tq

    def body(q_ref, k_ref, v_ref, tot_ref, do_ref, dq_ref, dk_ref, dv_ref, dk_acc, dv_acc):
        dk_acc[...] = jnp.zeros_like(dk_acc)
        dv_acc[...] = jnp.zeros_like(dv_acc)
        cmr = lax.broadcasted_iota(jnp.int32, (tq, t), 1) - lax.broadcasted_iota(jnp.int32, (tq, t), 0)
        u_le = _sb_sum_matrix(lambda row, col: row <= col)
        u_lt = _sb_sum_matrix(lambda row, col: row < col)
        hm = _sb_head_masks()

        def qloop(i, carry0):
            qs = pl.multiple_of(i * tq, tq)
            qraw = q_ref[pl.ds(qs, tq), :]
            dov = do_ref[pl.ds(qs, tq), :].astype(F32)
            totv = tot_ref[pl.ds(qs, tq), :]
            qh = [(qraw * (hm[h] * SB_SCALE2)).astype(BF16) for h in range(2)]
            q2 = jnp.concatenate([(qraw * hm[h]).astype(BF16) for h in range(2)], axis=0)
            doh = [(dov * hm[h]).astype(BF16) for h in range(2)]
            do2 = jnp.concatenate(doh, axis=0)
            tot = [jnp.max(totv * hm[h], axis=1, keepdims=True) for h in range(2)]
            diag = (i * tq) // (SB_GROUP * t)

            def group(g, st, masked):
                ks = pl.multiple_of(g * (SB_GROUP * t), SB_GROUP * t)
                kf = k_ref[pl.ds(ks, SB_GROUP * t), :]
                kb = kf.astype(BF16)
                vb = v_ref[pl.ds(ks, SB_GROUP * t), :].astype(BF16)
                tiles = [(h, j) for h in range(2) for j in range(SB_GROUP)]
                z = [_dot(qh[h], kb, NT) for h in range(2)]
                datt = [_dot(doh[h], vb, NT) for h in range(2)]
                keep = {j: cmr < i * tq - (g * SB_GROUP + j) * t for j in range(SB_GROUP)} if masked else None
                ls, lterms = {}, {}
                for h, j in tiles:
                    zj = z[h][:, j * t:(j + 1) * t]
                    sp = _softplus2(zj)
                    ls[h, j] = zj - sp
                    lterms[h, j] = _split_terms(jnp.where(keep[j], sp, 0.0) if masked else sp)
                lsum = {hj: _dot(lterms[hj], u_le) for hj in tiles}
                att, p, pterms, new_c = {}, {}, {}, []
                for h in range(2):
                    run = st[1 + h]
                    for j in range(SB_GROUP):
                        a = jnp.exp2(ls[h, j] - ((tot[h] - run) - lsum[h, j][:, :t]))
                        if masked:
                            a = jnp.where(keep[j], a, 0.0)
                        att[h, j] = a.astype(BF16)
                        p[h, j] = a * datt[h][:, j * t:(j + 1) * t]
                        pterms[h, j] = _split_terms(p[h, j])
                        run = run + lsum[h, j][:, t:]
                    new_c.append(run)
                psum = {hj: _dot(pterms[hj], u_lt) for hj in tiles}
                dzb, new_r = {}, []
                for h in range(2):
                    run = st[3 + h]
                    for j in range(SB_GROUP):
                        sig = jnp.exp2(ls[h, j])
                        dz = p[h, j] - sig * (p[h, j] + run + psum[h, j][:, :t])
                        if masked:
                            dz = jnp.where(keep[j], dz, 0.0)
                        dzb[h, j] = (dz * SB_SCALE).astype(BF16)
                        run = run + psum[h, j][:, t:]
                    new_r.append(run)
                k2 = jnp.concatenate([(kf * hm[h]).astype(BF16) for h in range(2)], axis=0)
                dq_acc = st[0] + _dot(jnp.concatenate([dzb[hj] for hj in tiles], axis=1), k2)
                for j in range(SB_GROUP):
                    rows = pl.ds(pl.multiple_of(ks + j * t, t), t)
                    dk_acc[rows, :] += _dot(jnp.concatenate([dzb[0, j], dzb[1, j]], axis=0), q2, TN)
                    dv_acc[rows, :] += _dot(jnp.concatenate([att[0, j], att[1, j]], axis=0), do2, TN)
                return (dq_acc, *new_c, *new_r)

            zero = jnp.zeros((tq, LANES), F32)
            st = lax.fori_loop(0, diag, lambda jj, sv: group(jj, sv, False), (zero,) * 5)
            st = group(diag, st, True)
            dq_ref[pl.ds(qs, tq), :] = st[0].astype(dq_ref.dtype)
            return carry0

        lax.fori_loop(0, nq, qloop, 0)
        dk_ref[...] = dk_acc[...].astype(dk_ref.dtype)
        dv_ref[...] = dv_acc[...].astype(dv_ref.dtype)

    out = jax.ShapeDtypeStruct((s, BRANCH_W), BF16)
    return pl.pallas_call(
        body, name=name, grid=(SB_HEADS // 2,),
        in_specs=[_col_spec(s, PB_SB), _col_spec(s, PB_SB + 4), _col_spec(s, PB_SB + 8), _col_spec(s, 0), _col_spec(s, 0)],
        out_specs=[_col_spec(s, 0)] * 3, out_shape=[out] * 3,
        scratch_shapes=[pltpu.VMEM((s, LANES), F32), pltpu.VMEM((s, LANES), F32)],
        compiler_params=_params("parallel"),
    )(proj, proj, proj, tot, do)


def _sc_fwd(proj, conv_w, *, name):
    s = proj.shape[0]

    def body(x_ref, b_ref, c_ref, w_ref, y_ref):
        rows = lax.broadcasted_iota(jnp.int32, (s, LANES), 0)
        w = w_ref[...]
        u = c_ref[...] * x_ref[...]
        cv = w[2:3, :] * u + w[1:2, :] * _shift_down(u, 1, rows) + w[0:1, :] * _shift_down(u, 2, rows)
        y_ref[...] = (b_ref[...] * cv).astype(y_ref.dtype)

    return pl.pallas_call(
        body, name=name, grid=(BRANCH_W // LANES,),
        in_specs=[_col_spec(s, PB_SCX), _col_spec(s, PB_SCB), _col_spec(s, PB_SCC), pl.BlockSpec((3, LANES), lambda j: (0, j))],
        out_specs=_col_spec(s, 0), out_shape=jax.ShapeDtypeStruct((s, BRANCH_W), BF16),
        compiler_params=_params("parallel"),
    )(proj, proj, proj, conv_w)


def _sc_bwd(proj, conv_w, dy, *, name):
    s = proj.shape[0]

    def body(x_ref, b_ref, c_ref, w_ref, dy_ref, dx_ref, db_ref, dc_ref, dw_ref):
        rows = lax.broadcasted_iota(jnp.int32, (s, LANES), 0)
        w, x, cg, dyv = w_ref[...], x_ref[...], c_ref[...], dy_ref[...].astype(F32)
        u = cg * x
        u1, u2 = _shift_down(u, 1, rows), _shift_down(u, 2, rows)
        cv = w[2:3, :] * u + w[1:2, :] * u1 + w[0:1, :] * u2
        db_ref[...] = (dyv * cv).astype(db_ref.dtype)
        dcv = dyv * b_ref[...]
        du = w[2:3, :] * dcv + w[1:2, :] * _shift_up(dcv, 1, rows) + w[0:1, :] * _shift_up(dcv, 2, rows)
        dx_ref[...] = (du * cg).astype(dx_ref.dtype)
        dc_ref[...] = (du * x).astype(dc_ref.dtype)
        dw_ref[0:1, :] = jnp.sum(dcv * u2, axis=0, keepdims=True)
        dw_ref[1:2, :] = jnp.sum(dcv * u1, axis=0, keepdims=True)
        dw_ref[2:3, :] = jnp.sum(dcv * u, axis=0, keepdims=True)

    out = jax.ShapeDtypeStruct((s, BRANCH_W), BF16)
    wspec = pl.BlockSpec((3, LANES), lambda j: (0, j))
    return pl.pallas_call(
        body, name=name, grid=(BRANCH_W // LANES,),
        in_specs=[_col_spec(s, PB_SCX), _col_spec(s, PB_SCB), _col_spec(s, PB_SCC), wspec, _col_spec(s, 0)],
        out_specs=[_col_spec(s, 0)] * 3 + [wspec],
        out_shape=[out] * 3 + [jax.ShapeDtypeStruct((3, BRANCH_W), F32)],
        compiler_params=_params("parallel"),
    )(proj, proj, proj, conv_w, dy)


MERGE_TM, MERGE_TN = 1024, D_MODEL // N_CHIPS


def _merge_specs():
    tm, tn = MERGE_TM, MERGE_TN
    y_spec = pl.BlockSpec((tm, BRANCH_W), lambda i, j: (i, 0))
    w_spec = pl.BlockSpec((None, N_BRANCH, BRANCH_W, tn), lambda i, j: (j, 0, 0, 0))
    gate_specs = [pl.BlockSpec((tm, tn), functools.partial(
        lambda i, j, b: (i, (PB_GATES * LANES + b * D_MODEL) // tn + j), b=b)) for b in range(N_BRANCH)]
    mn = pl.BlockSpec((tm, tn), lambda i, j: (i, j))
    return y_spec, w_spec, gate_specs, mn


def _merge_fwd(ya, yb, yc, wb, proj, *, name):
    s = ya.shape[0]
    y_spec, w_spec, gate_specs, mn = _merge_specs()

    def body(ya_ref, yb_ref, yc_ref, w_ref, g0, g1, g2, o_ref):
        acc = None
        for b, (y_ref, g_ref) in enumerate(zip((ya_ref, yb_ref, yc_ref), (g0, g1, g2))):
            term = _sigmoid(g_ref[...]) * _bdot(y_ref[...], w_ref[b])
            acc = term if acc is None else acc + term
        o_ref[...] = acc.astype(o_ref.dtype)

    return pl.pallas_call(
        body, name=name, grid=(s // MERGE_TM, D_MODEL // MERGE_TN),
        in_specs=[y_spec] * 3 + [w_spec] + gate_specs, out_specs=mn,
        out_shape=jax.ShapeDtypeStruct((s, D_MODEL), BF16), compiler_params=_params("parallel", "parallel"),
    )(ya, yb, yc, wb, proj, proj, proj)


def _merge_bwd(ya, yb, yc, wb, proj, dm, *, name):
    s = ya.shape[0]
    y_spec, w_spec, gate_specs, mn = _merge_specs()

    def body(ya_ref, yb_ref, yc_ref, w_ref, g0, g1, g2, dm_ref, *outs):
        dmv = dm_ref[...].astype(F32)
        for b, (y_ref, g_ref) in enumerate(zip((ya_ref, yb_ref, yc_ref), (g0, g1, g2))):
            sg = _sigmoid(g_ref[...])
            z = _bdot(y_ref[...], w_ref[b])
            outs[b][...] = (dmv * sg).astype(BF16)
            outs[N_BRANCH + b][...] = (dmv * z * sg * (1.0 - sg)).astype(BF16)

    out = jax.ShapeDtypeStruct((s, D_MODEL), BF16)
    res = pl.pallas_call(
        body, name=name, grid=(s // MERGE_TM, D_MODEL // MERGE_TN),
        in_specs=[y_spec] * 3 + [w_spec] + gate_specs + [mn], out_specs=[mn] * (2 * N_BRANCH),
        out_shape=[out] * (2 * N_BRANCH), compiler_params=_params("parallel", "parallel"),
    )(ya, yb, yc, wb, proj, proj, proj, dm)
    return res[:N_BRANCH], res[N_BRANCH:]


def _chunk_rows(v, s):
    return v[:, :2 * GDN_HEADS].reshape(s // CHUNK, CHUNK, 2 * GDN_HEADS).transpose(0, 2, 1)


def _relu2_epi(acc):
    r = jnp.maximum(acc, 0.0)
    return acc, r * r


def _drelu2_epi(acc, a):
    return (acc * (2.0 * jnp.maximum(a.astype(F32), 0.0)),)


def _layer_fwd(x0, p, late=None):
    s = x0.shape[0]
    h1 = _norm_fwd(x0, p["norm_mix_pre"], name="norm_mix_pre")
    proj = _matmul(h1, p["w_in"], name="proj_in", tm=512, tn=1664)
    qkvn = _gdn_pre_fwd(proj, p["conv_qkv_w"], name="gdn_pre")
    gb = _gdn_gates_fwd(proj, p["gdn_a_log"], p["gdn_dt_bias"], name="gdn_gates")
    gbt = _chunk_rows(gb, s)
    o_gdn, states = _gdn_chunk_fwd(qkvn, gb, gbt, name="gdn_chunk")
    ya = _gdn_post_fwd(o_gdn, proj, p["gdn_norm_w"], name="gdn_post")
    o_sb, sb_tot = _sb_fwd(proj, name="sb_attn")
    yc = _sc_fwd(proj, p["conv_sc_w"], name="short_conv")
    if late is not None:
        p = dict(p, **late(yc))
    merged = _merge_fwd(ya, o_sb, yc, p["w_branch"], proj, name="merge")
    u = _matmul(merged, p["w_out"], name="proj_out", tm=512, tn=1024)
    x1 = _resnorm_fwd(x0, u, p["norm_mix_post"], name="norm_mix_post")
    h2 = _norm_fwd(x1, p["norm_ffn_pre"], name="norm_ffn_pre")
    a, r = _matmul(h2, p["w_ff1"], name="ff1", tm=512, tn=1024, outs=(BF16, BF16), epi=_relu2_epi, b_chips=True)
    f = _matmul(r, p["w_ff2"], name="ff2", tm=1024, tn=1024, tk=2048)
    x2 = _resnorm_fwd(x1, f, p["norm_ffn_post"], name="norm_ffn_post")
    saved = dict(x0=x0, h1=h1, proj=proj, qkvn=qkvn, gb=gb, gbt=gbt, o_gdn=o_gdn, states=states, ya=ya, o_sb=o_sb,
                 sb_tot=sb_tot, yc=yc, merged=merged, u=u, x1=x1, h2=h2, a=a, r=r, f=f)
    return x2, saved, p


def _layer_bwd(dx2, p, sv, early=None):
    g = {}
    df, g["norm_ffn_post"] = _norm_bwd(sv["f"], p["norm_ffn_post"], dx2, None, out_dtype=BF16, name="norm_ffn_post_bwd")
    g["w_ff2"] = _matmul(sv["r"], df, ta=True, name="ff2_dw", tm=1024, tn=1024, tk=2048, outs=(BF16,))
    da = _matmul(df, p["w_ff2"], tb=True, name="ff2_dx", tm=512, tn=1024, outs=(BF16,), epi=_drelu2_epi,
                 extras=(sv["a"],))
    g["w_ff1"] = _matmul(sv["h2"], da, ta=True, name="ff1_dw", tm=1024, tn=1024, tk=2048, out_chips=True, outs=(BF16,))
    dh2 = _matmul(da, p["w_ff1"], tb=True, name="ff1_dx", tm=1024, tn=1024, tk=1024, b_chips=True)
    dx1, g["norm_ffn_pre"] = _norm_bwd(sv["x1"], p["norm_ffn_pre"], dh2, dx2, out_dtype=F32, name="norm_ffn_pre_bwd")
    du, g["norm_mix_post"] = _norm_bwd(sv["u"], p["norm_mix_post"], dx1, None, out_dtype=BF16, name="norm_mix_post_bwd")
    g["w_out"] = _matmul(sv["merged"], du, ta=True, name="out_dw", tm=1024, tn=1024, tk=2048, outs=(BF16,))
    dmerged = _matmul(du, p["w_out"], tb=True, name="out_dx", tm=512, tn=1024, outs=(BF16,))
    ys = (sv["ya"], sv["o_sb"], sv["yc"])
    dz, dgates = _merge_bwd(*ys, p["w_branch"], sv["proj"], dmerged, name="merge_bwd")
    g["w_branch"] = jnp.stack([_matmul(ys[b], dz[b], ta=True, name=f"branch_dw{b}", tm=512, tn=256, tk=1024, out_chips=True,
                                       outs=(BF16,)) for b in range(N_BRANCH)], axis=1)
    dys = [_matmul(dz[b], p["w_branch"][:, b], tb=True, name=f"branch_dx{b}", tm=1024, tn=512, tk=256, b_chips=True)
           for b in range(N_BRANCH)]
    conv_sc_w = p["conv_sc_w"]
    if early is not None:
        conv_sc_w = conv_sc_w + early({k: g[k] for k in ("w_branch", "w_out", "w_ff1", "w_ff2")})
    dscx, dscb, dscc, g["conv_sc_w"] = _sc_bwd(sv["proj"], conv_sc_w, dys[2], name="short_conv_bwd")
    dsq, dsk, dsv = _sb_bwd(sv["proj"], sv["sb_tot"], dys[1], name="sb_attn_bwd")
    do_gdn, dgate, dnw = _gdn_post_bwd(sv["o_gdn"], sv["proj"], p["gdn_norm_w"], dys[0], name="gdn_post_bwd")
    g["gdn_norm_w"] = dnw
    dqkvn, dgb = _gdn_chunk_bwd(sv["qkvn"], sv["gb"], sv["gbt"], sv["states"], do_gdn, name="gdn_chunk_bwd")
    dqkv, g["conv_qkv_w"] = _gdn_pre_bwd(sv["proj"], p["conv_qkv_w"], dqkvn, name="gdn_pre_bwd")
    dab, g["gdn_a_log"], g["gdn_dt_bias"] = _gdn_gates_bwd(sv["proj"], p["gdn_a_log"], p["gdn_dt_bias"], dgb,
                                                           name="gdn_gates_bwd")
    dproj = jnp.concatenate([*dgates, dqkv, dgate, dab, dsq, dsk, dsv, dscx, dscb, dscc], axis=1)
    g["w_in"] = _matmul(sv["h1"], dproj, ta=True, name="in_dw", tm=1024, tn=1664, tk=1024, outs=(BF16,))
    dh1 = _matmul(dproj, p["w_in"], tb=True, name="in_dx", tm=1024, tn=1024, tk=1664)
    dx0, g["norm_mix_pre"] = _norm_bwd(sv["x0"], p["norm_mix_pre"], dh1, dx1, out_dtype=F32, name="norm_mix_pre_bwd")
    return dx0, g


def _local_step(x, target, n_layers, weights_of, grads_done, grads_early=None):
    saved, layers = [], []
    h = x
    for l in range(n_layers):
        p, late = weights_of(l, h)
        h, sv, p = _layer_fwd(h, p, late)
        saved.append(sv)
        layers.append(p)
    loss, dh = _loss_fwd_bwd(h, target, name="loss")
    for l in reversed(range(n_layers)):
        dh, g = _layer_bwd(dh, layers[l], saved[l], grads_early if l == 0 else None)
        zero = grads_done(l, g)
        if l > 0:
            layers[l - 1] = dict(layers[l - 1], norm_ffn_post=layers[l - 1]["norm_ffn_post"] + zero)
    return loss, dh


ANY = pl.BlockSpec(memory_space=pl.ANY)


def _me_and_chips():
    x, y, c = lax.axis_index("x"), lax.axis_index("y"), lax.axis_index("c")
    chips = [(1 - x, y), (x, 1 - y), (1 - x, 1 - y)]
    return x, y, c, chips


def _gather_devices(small, *, name):
    def body(small_ref, small_out, ssend, srecv, local_sem):
        x, y, c, chips = _me_and_chips()
        dev = 4 * x + 2 * y + c
        lc = pltpu.make_async_copy(small_ref, small_out.at[dev], local_sem)
        lc.start()
        peers = [(x, y, 1 - c)] + [(px, py, pc) for (px, py) in chips for pc in (c, 1 - c)]
        sends = []
        for k, peer in enumerate(peers):
            cp = pltpu.make_async_remote_copy(src_ref=small_ref, dst_ref=small_out.at[dev], send_sem=ssend.at[k],
                                              recv_sem=srecv.at[k], device_id=peer, device_id_type=MESH)
            cp.start()
            sends.append(cp)
        for k, (px, py, pc) in enumerate(peers):
            pltpu.make_async_remote_copy(src_ref=small_ref, dst_ref=small_out.at[4 * px + 2 * py + pc], send_sem=ssend.at[k],
                                         recv_sem=srecv.at[k], device_id=(px, py, pc), device_id_type=MESH).wait_recv()
        for cp in sends:
            cp.wait_send()
        lc.wait()

    return pl.pallas_call(
        body, name=name, in_specs=[ANY], out_specs=ANY,
        out_shape=jax.ShapeDtypeStruct((N_DEV,) + small.shape, small.dtype),
        scratch_shapes=[pltpu.SemaphoreType.DMA((N_DEV - 1,)), pltpu.SemaphoreType.DMA((N_DEV - 1,)), pltpu.SemaphoreType.DMA],
    )(small)


HBM = pl.BlockSpec(memory_space=pltpu.HBM)
SEM = pl.BlockSpec(memory_space=pltpu.SEMAPHORE)
EFFECT = pltpu.SideEffectType.DATAFLOW_SIDE_EFFECTING


def _exchange_start(srcs, *, by_slot, name, after=None):
    n = len(srcs)
    n_in = 2 * n + (after is not None)
    land_shapes = [a.shape if by_slot else (N_CHIPS,) + a.shape for a in srcs]
    lands = [pltpu.with_memory_space_constraint(lax.empty(sh, a.dtype), pltpu.HBM) for sh, a in zip(land_shapes, srcs)]
    srcs = [pltpu.with_memory_space_constraint(a, pltpu.HBM) for a in srcs]

    def body(*refs):
        ins, land = refs[:n], refs[n:2 * n]
        send_sems, recv_sems, token = refs[n_in], refs[n_in + 1], refs[-1]
        x, y, c, chips = _me_and_chips()
        me = 2 * x + y
        for a in range(n):
            for k, (px, py) in enumerate(chips):
                pltpu.make_async_remote_copy(
                    src_ref=ins[a].at[2 * px + py] if by_slot else ins[a], dst_ref=land[a].at[me],
                    send_sem=send_sems.at[3 * a + k], recv_sem=recv_sems.at[3 * a + k], device_id=(px, py, c),
                    device_id_type=MESH).start()
        token[...] = jnp.zeros_like(token)

    res = pl.pallas_call(
        body, name=name, in_specs=[HBM] * (2 * n) + ([ANY] if after is not None else []),
        out_specs=[SEM, SEM] + [HBM] * (2 * n) + [pl.BlockSpec(memory_space=pltpu.VMEM)],
        out_shape=[pltpu.SemaphoreType.DMA((3 * n,)), pltpu.SemaphoreType.DMA((3 * n,))]
        + [pltpu.HBM(a.shape, a.dtype) for a in srcs] + [pltpu.HBM(sh, a.dtype) for sh, a in zip(land_shapes, srcs)]
        + [jax.ShapeDtypeStruct((8, LANES), F32)],
        input_output_aliases={i: 2 + i for i in range(2 * n)},
        compiler_params=pltpu.CompilerParams(has_side_effects=EFFECT),
    )(*srcs, *lands, *([after] if after is not None else []))
    return dict(send=res[0], recv=res[1], srcs=res[2:2 + n], lands=res[2 + n:2 + 2 * n], token=res[-1])


def _exchange_wait(ex, after, *, by_slot, name):
    n = len(ex["srcs"])

    def body(*refs):
        ins, land = refs[:n], refs[n:2 * n]
        send_sems, recv_sems = refs[2 * n], refs[2 * n + 1]
        x, y, c, chips = _me_and_chips()
        me = 2 * x + y
        for a in range(n):
            for k, (px, py) in enumerate(chips):
                cp = pltpu.make_async_remote_copy(
                    src_ref=ins[a].at[me] if by_slot else ins[a], dst_ref=land[a].at[2 * px + py],
                    send_sem=send_sems.at[3 * a + k], recv_sem=recv_sems.at[3 * a + k], device_id=(px, py, c),
                    device_id_type=MESH)
                cp.wait_send()
                cp.wait_recv()

    res = pl.pallas_call(
        body, name=name, in_specs=[HBM] * (2 * n) + [SEM, SEM, ANY], out_specs=[HBM] * (2 * n),
        out_shape=[pltpu.HBM(a.shape, a.dtype) for a in ex["srcs"]] + [pltpu.HBM(a.shape, a.dtype) for a in ex["lands"]],
        input_output_aliases={i: i for i in range(2 * n)},
        compiler_params=pltpu.CompilerParams(has_side_effects=EFFECT),
    )(*ex["srcs"], *ex["lands"], ex["send"], ex["recv"], after)
    return res[:n], res[n:]


def _chip_index():
    return 2 * lax.axis_index("x") + lax.axis_index("y")


def _me_operand():
    return jnp.reshape(_chip_index(), (1,)).astype(jnp.int32)


def _place_own(land, own, *, name):
    rows, cols = _as2d(own).shape
    tr = _row_tile(rows, cols)

    def body(me_ref, own_ref, land_ref, out_ref):
        out_ref[...] = own_ref[...]

    res = pl.pallas_call(
        body, name=name,
        grid_spec=pltpu.PrefetchScalarGridSpec(
            num_scalar_prefetch=1, grid=(rows // tr,),
            in_specs=[pl.BlockSpec((tr, cols), lambda i, me: (i, 0)), ANY],
            out_specs=pl.BlockSpec((None, tr, cols), lambda i, me: (me[0], i, 0))),
        out_shape=jax.ShapeDtypeStruct((N_CHIPS, rows, cols), land.dtype), input_output_aliases={2: 0},
        compiler_params=_params("arbitrary"),
    )(_me_operand(), _as2d(own), land.reshape(N_CHIPS, rows, cols))
    return res.reshape(land.shape)


def _sum_partials(lands, parts, *, name):
    n, rows, cols = lands.shape
    tr = _row_tile(rows, cols, 1024 * 1024)

    def body(me_ref, land_ref, own_ref, o_ref):
        me = me_ref[0]
        acc = None
        for i in range(n):
            term = jnp.where(me == i, own_ref[...], land_ref[i]).astype(F32)
            acc = term if acc is None else acc + term
        o_ref[...] = acc

    return pl.pallas_call(
        body, name=name,
        grid_spec=pltpu.PrefetchScalarGridSpec(
            num_scalar_prefetch=1, grid=(rows // tr,),
            in_specs=[pl.BlockSpec((n, tr, cols), lambda i, me: (0, i, 0)),
                      pl.BlockSpec((None, tr, cols), lambda i, me: (me[0], i, 0))],
            out_specs=pl.BlockSpec((tr, cols), lambda i, me: (i, 0))),
        out_shape=jax.ShapeDtypeStruct((rows, cols), F32), compiler_params=_params("arbitrary"),
    )(_me_operand(), lands, parts)


def _swap_sibling(arrs, *, name):
    n = len(arrs)

    def body(*refs):
        ins, outs = refs[:n], refs[n:2 * n]
        send_sems, recv_sems = refs[2 * n:]
        x, y, c = lax.axis_index("x"), lax.axis_index("y"), lax.axis_index("c")
        cps = [pltpu.make_async_remote_copy(src_ref=ins[a], dst_ref=outs[a], send_sem=send_sems.at[a],
                                            recv_sem=recv_sems.at[a], device_id=(x, y, 1 - c), device_id_type=MESH)
               for a in range(n)]
        for cp in cps:
            cp.start()
        for cp in cps:
            cp.wait()

    return pl.pallas_call(
        body, name=name, in_specs=[ANY] * n, out_specs=[ANY] * n,
        out_shape=[jax.ShapeDtypeStruct(a.shape, a.dtype) for a in arrs],
        scratch_shapes=[pltpu.SemaphoreType.DMA((n,)), pltpu.SemaphoreType.DMA((n,))],
    )(*arrs)


def _row_tile(rows, cols, budget=2 * 1024 * 1024):
    best = None
    for t in range(16, rows + 1, 16):
        if rows % t == 0 and t * cols * 4 <= budget:
            best = t
    return best if best is not None else rows


def _sum_slots(parts, *, name):
    n, rows, cols = parts.shape
    tr = _row_tile(rows, cols, 1024 * 1024)

    def body(p_ref, o_ref):
        acc = p_ref[0].astype(F32)
        for i in range(1, n):
            acc = acc + p_ref[i].astype(F32)
        o_ref[...] = acc

    return pl.pallas_call(
        body, name=name, grid=(rows // tr,), in_specs=[pl.BlockSpec((n, tr, cols), lambda i: (0, i, 0))],
        out_specs=pl.BlockSpec((tr, cols), lambda i: (i, 0)), out_shape=jax.ShapeDtypeStruct((rows, cols), F32),
        compiler_params=_params("parallel"),
    )(parts)


def _adamw(w, m, v, g_a, g_b, *, name):
    rows, cols = w.shape
    tr = _row_tile(rows, cols)
    two = g_b is not None
    c1 = 1.0 / (1.0 - ADAM_B1 ** ADAM_STEP)
    c2 = 1.0 / (1.0 - ADAM_B2 ** ADAM_STEP)

    def body(*refs):
        w_ref, m_ref, v_ref, ga_ref = refs[:4]
        g_ref, d_ref, nm_ref, nv_ref = refs[4 + two:]
        g = ga_ref[...]
        if two:
            g = g + refs[4][...]
        nm = ADAM_B1 * m_ref[...] + (1.0 - ADAM_B1) * g
        nv = ADAM_B2 * v_ref[...] + (1.0 - ADAM_B2) * (g * g)
        g_ref[...] = g
        nm_ref[...] = nm
        nv_ref[...] = nv
        d_ref[...] = -ADAM_LR * ((nm * c1) / (jnp.sqrt(nv * c2) + ADAM_EPS) + ADAM_WD * w_ref[...])

    blk = pl.BlockSpec((tr, cols), lambda i: (i, 0))
    ins = [w, m, v, g_a] + ([g_b] if two else [])
    return pl.pallas_call(
        body, name=name, grid=(rows // tr,), in_specs=[blk] * len(ins), out_specs=[blk] * 4,
        out_shape=[jax.ShapeDtypeStruct((rows, cols), F32)] * 4, compiler_params=_params("parallel"),
    )(*ins)


def _cast_bf16(w, *, name):
    rows, cols = w.shape
    tr = _row_tile(rows, cols)

    def body(w_ref, o_ref):
        o_ref[...] = w_ref[...].astype(BF16)

    blk = pl.BlockSpec((tr, cols), lambda i: (i, 0))
    return pl.pallas_call(body, name=name, grid=(rows // tr,), in_specs=[blk], out_specs=blk,
                          out_shape=jax.ShapeDtypeStruct((rows, cols), BF16), compiler_params=_params("parallel"))(w)


BIG = ("w_in", "w_branch", "w_out", "w_ff1", "w_ff2")
SMALL = ("norm_mix_pre", "conv_qkv_w", "gdn_a_log", "gdn_dt_bias", "gdn_norm_w", "conv_sc_w", "norm_mix_post",
         "norm_ffn_pre", "norm_ffn_post")
ORDER = ("norm_mix_pre", "w_in", "conv_qkv_w", "gdn_a_log", "gdn_dt_bias", "gdn_norm_w", "conv_sc_w", "w_branch",
         "w_out", "norm_mix_post", "norm_ffn_pre", "w_ff1", "w_ff2", "norm_ffn_post")


_MATMUL_LAYOUT = dict(
    w_in=_in_cols_from_chips,
    w_branch=lambda a: a,
    w_out=lambda a: a.reshape(D_MODEL, D_MODEL),
    w_ff1=lambda a: a,
    w_ff2=lambda a: a.reshape(D_FF, D_MODEL),
)
_SHARD_LAYOUT = dict(
    w_in=_in_cols_to_chips,
    w_branch=lambda g: g,
    w_out=lambda g: g.reshape(N_CHIPS, D_MODEL // N_CHIPS, D_MODEL),
    w_ff1=lambda g: g,
    w_ff2=lambda g: g.reshape(N_CHIPS, D_FF // N_CHIPS, D_MODEL),
)


def _full_weights(big, conv, rep, l):
    p = {k: _MATMUL_LAYOUT[k](a) for k, a in big.items()}
    if conv is not None:
        p["conv_qkv_w"] = conv["conv_qkv_w"][:, l].transpose(1, 0, 2).reshape(4, 3 * BRANCH_W)
        p["conv_sc_w"] = conv["conv_sc_w"][:, l].transpose(1, 0, 2).reshape(3, BRANCH_W)
    if rep is not None:
        for k in ("norm_mix_pre", "gdn_a_log", "gdn_dt_bias", "gdn_norm_w", "norm_mix_post", "norm_ffn_pre", "norm_ffn_post"):
            p[k] = rep[k][l]
    return p


def _partials_by_chip(g, names):
    return [_SHARD_LAYOUT[k](g[k]).astype(BF16) for k in names]


def _pack_small(grads):
    pieces, layout = [], []
    for name in SMALL:
        v = jnp.stack([g[name] for g in grads]).astype(F32)
        layout.append((name, v.shape))
        pieces.append(v.reshape(-1))
    flat = jnp.concatenate(pieces)
    rows = -(-flat.shape[0] // LANES)
    rows = -(-rows // 8) * 8
    flat = jnp.pad(flat, (0, rows * LANES - flat.shape[0]))
    return flat.reshape(rows, LANES), layout


def _unpack_small(table, layout):
    flat, out, off = table.reshape(-1), {}, 0
    for name, shape in layout:
        size = 1
        for d in shape:
            size *= d
        out[name] = flat[off:off + size].reshape(shape)
        off += size
    return out


def _as2d(a):
    return a.reshape(-1, a.shape[-1]) if a.ndim > 1 else a.reshape(1, -1)


def kernel(x, norm_mix_pre, w_in, conv_qkv_w, gdn_a_log, gdn_dt_bias, gdn_norm_w, conv_sc_w, w_branch, w_out, norm_mix_post, norm_ffn_pre, w_ff1, w_ff2, norm_ffn_post, loss_target, m_norm_mix_pre, m_w_in, m_conv_qkv_w, m_gdn_a_log, m_gdn_dt_bias, m_gdn_norm_w, m_conv_sc_w, m_w_branch, m_w_out, m_norm_mix_post, m_norm_ffn_pre, m_w_ff1, m_w_ff2, m_norm_ffn_post, v_norm_mix_pre, v_w_in, v_conv_qkv_w, v_gdn_a_log, v_gdn_dt_bias, v_gdn_norm_w, v_conv_sc_w, v_w_branch, v_w_out, v_norm_mix_post, v_norm_ffn_pre, v_w_ff1, v_w_ff2, v_norm_ffn_post):
    w = dict(norm_mix_pre=norm_mix_pre, w_in=w_in, conv_qkv_w=conv_qkv_w, gdn_a_log=gdn_a_log, gdn_dt_bias=gdn_dt_bias,
             gdn_norm_w=gdn_norm_w, conv_sc_w=conv_sc_w, w_branch=w_branch, w_out=w_out, norm_mix_post=norm_mix_post,
             norm_ffn_pre=norm_ffn_pre, w_ff1=w_ff1, w_ff2=w_ff2, norm_ffn_post=norm_ffn_post)
    m = dict(norm_mix_pre=m_norm_mix_pre, w_in=m_w_in, conv_qkv_w=m_conv_qkv_w, gdn_a_log=m_gdn_a_log,
             gdn_dt_bias=m_gdn_dt_bias, gdn_norm_w=m_gdn_norm_w, conv_sc_w=m_conv_sc_w, w_branch=m_w_branch, w_out=m_w_out,
             norm_mix_post=m_norm_mix_post, norm_ffn_pre=m_norm_ffn_pre, w_ff1=m_w_ff1, w_ff2=m_w_ff2,
             norm_ffn_post=m_norm_ffn_post)
    v = dict(norm_mix_pre=v_norm_mix_pre, w_in=v_w_in, conv_qkv_w=v_conv_qkv_w, gdn_a_log=v_gdn_a_log,
             gdn_dt_bias=v_gdn_dt_bias, gdn_norm_w=v_gdn_norm_w, conv_sc_w=v_conv_sc_w, w_branch=v_w_branch, w_out=v_w_out,
             norm_mix_post=v_norm_mix_post, norm_ffn_pre=v_norm_ffn_pre, w_ff1=v_w_ff1, w_ff2=v_w_ff2,
             norm_ffn_post=v_norm_ffn_post)

    me = _chip_index()

    shards = {k: _cast_bf16(_as2d(w[k]), name=f"cast_{k}").reshape(w[k].shape) for k in BIG}
    conv_names = ("conv_qkv_w", "conv_sc_w")
    FIRST, REST = ("w_in",), ("w_branch", "w_out", "w_ff1", "w_ff2")

    def gather_start(l, names, tag, after):
        srcs = [shards[k][l] for k in names] + ([w[k] for k in conv_names] if (l == 0 and "w_in" in names) else [])
        return _exchange_start(srcs, by_slot=False, name=f"gather_start{l}{tag}", after=after)

    def gather_land(ex, names, l, tag, after):
        own, lands = _exchange_wait(ex, after, by_slot=False, name=f"gather_wait{l}{tag}")
        return {k: _place_own(land, o, name=f"own_{k}") for k, land, o in zip(names, lands, own)}

    gathers = {0: gather_start(0, FIRST, "a", None)}
    conv = {}

    def weights_of(l, x_l):
        if l > 0:
            full = gather_land(gathers[l], BIG, l, "", x_l)
            p = _full_weights(full, conv, w, l)
            if l + 1 < DEPTH:
                gathers[l + 1] = gather_start(l + 1, BIG, "", full["w_out"])
                p["norm_mix_pre"] = p["norm_mix_pre"] + gathers[l + 1]["token"][0, 0]
            return p, None
        full = gather_land(gathers[0], FIRST + conv_names, 0, "a", gathers[0]["token"])
        conv.update({k: full[k] for k in conv_names})
        p = _full_weights({"w_in": full["w_in"]}, conv, w, 0)
        rest = gather_start(0, REST, "b", full["conv_sc_w"])
        p["norm_mix_pre"] = p["norm_mix_pre"] + rest["token"][0, 0]

        def late(after):
            arrived = gather_land(rest, REST, 0, "b", after)
            q = _full_weights(arrived, None, None, 0)
            gathers[1] = gather_start(1, BIG, "", arrived["w_out"])
            q["norm_mix_post"] = p["norm_mix_post"] + gathers[1]["token"][0, 0]
            return q

        return p, late

    grads, scatters = [None] * DEPTH, {}

    def scatter_start(l, g, names, tag):
        scatters[l, names] = _exchange_start(_partials_by_chip(g, names), by_slot=True, name=f"scatter_start{l}{tag}")
        return scatters[l, names]["token"][0, 0]

    def grads_early(g):
        return scatter_start(0, g, REST, "a")

    def grads_done(l, g):
        grads[l] = g
        return scatter_start(l, g, FIRST, "b") if l == 0 else scatter_start(l, g, BIG, "")

    loss, dx = _local_step(x[0], loss_target[0], DEPTH, weights_of, grads_done, grads_early)
    loss = lax.psum(loss, ("x", "y", "c"))

    sums = [dict() for _ in range(DEPTH)]
    for (l, names), ex in sorted(scatters.items(), key=lambda kv: (-kv[0][0], kv[0][1] != REST)):
        tag = "" if names == BIG else ("a" if names == REST else "b")
        parts, lands = _exchange_wait(ex, dx, by_slot=True, name=f"scatter_wait{l}{tag}")
        for k, r, o in zip(names, lands, parts):
            sums[l][k] = _sum_partials(r.reshape(N_CHIPS, -1, r.shape[-1]), o.reshape(N_CHIPS, -1, o.shape[-1]), name=f"sum_{k}")
    mine = [jnp.concatenate([sums[l][k] for l in range(DEPTH)], axis=0) for k in BIG]
    theirs = _swap_sibling(mine, name="swap_sibling")
    small, layout = _pack_small(grads)
    small_g = _unpack_small(_sum_slots(_gather_devices(small, name="gather_small"), name="sum_small"), layout)
    for k, width in (("conv_qkv_w", 3 * BRANCH_W // N_CHIPS), ("conv_sc_w", BRANCH_W // N_CHIPS)):
        small_g[k] = lax.dynamic_slice_in_dim(small_g[k], me * width, width, axis=2)

    out = {}
    for k, s_mine, s_theirs in zip(BIG, mine, theirs):
        res = _adamw(_as2d(w[k]), _as2d(m[k]), _as2d(v[k]), s_mine, s_theirs, name=f"adamw_{k}")
        out[k] = [r.reshape(w[k].shape) for r in res]
    for k in SMALL:
        res = _adamw(_as2d(w[k]), _as2d(m[k]), _as2d(v[k]), _as2d(small_g[k]), None, name=f"adamw_{k}")
        out[k] = [r.reshape(w[k].shape) for r in res]
    return (loss, dx[None], *[out[k][0] for k in ORDER], *[out[k][1] for k in ORDER], *[out[k][2] for k in ORDER],
            *[out[k][3] for k in ORDER])
```

```python
import functools

import jax
import jax.numpy as jnp
from jax import lax
from jax.experimental import pallas as pl
from jax.experimental.pallas import tpu as pltpu

F32 = jnp.float32
BF16 = jnp.bfloat16
MESH = pl.DeviceIdType.MESH

LANES = 128
D_MODEL = 1024
DEPTH = 4
CHUNK = 64
GDN_CHUNKS_PER_STEP = 2
GDN_HEADS, GDN_DIM = 4, 128
SB_HEADS, SB_DIM = 8, 64
BRANCH_W = 512
N_BRANCH = 3
D_FF = 4 * D_MODEL
EPS = 1e-6
IN_W = 8200
AB_COL = 2048
AB_PAD = LANES - 8
IN_WP = IN_W + AB_PAD
N_CHIPS = 4
N_DEV = 8
GATES_COL = 5128
PB_GATES, PB_QKV, PB_GATE, PB_AB, PB_SB, PB_SCX, PB_SCB, PB_SCC = 0, 24, 36, 40, 41, 53, 57, 61
SB_TILE = 128
SB_GROUP = 4
SB_QTILE = 256
SB_SCALE = SB_DIM ** -0.5
SB_SCALE2 = SB_SCALE * 1.4426950408889634
GDN_QSCALE = GDN_DIM ** -0.5
VMEM_LIMIT = 56 * 1024 * 1024

ADAM_LR, ADAM_B1, ADAM_B2, ADAM_EPS, ADAM_WD, ADAM_STEP = 0.001, 0.9, 0.999, 1e-08, 0.01, 10

NT = (((1,), (1,)), ((), ()))
TN = (((0,), (0,)), ((), ()))
HI = lax.Precision.HIGH


def _pad_in_cols(w):
    return jnp.concatenate([w[:, GATES_COL:], w[:, :AB_COL + 8], jnp.zeros((w.shape[0], AB_PAD), w.dtype),
                            w[:, AB_COL + 8:GATES_COL]], axis=1)


def _unpad_in_cols(g):
    n_gates = IN_W - GATES_COL
    return jnp.concatenate([g[:, n_gates:n_gates + AB_COL + 8], g[:, n_gates + AB_COL + 8 + AB_PAD:], g[:, :n_gates]], axis=1)


IN_SHARD = IN_W // N_CHIPS
_IN_SEGMENTS = ((0, AB_COL + 8, IN_W - GATES_COL), (AB_COL + 8, GATES_COL, IN_W - GATES_COL + AB_PAD),
                (GATES_COL, IN_W, -GATES_COL))


def _in_cols_from_chips(slots):
    def cols(first, last):
        out = []
        for j in range(N_CHIPS):
            lo, hi = max(first, j * IN_SHARD), min(last, (j + 1) * IN_SHARD)
            if lo < hi:
                out.append(slots[j][:, lo - j * IN_SHARD:hi - j * IN_SHARD])
        return out

    head, tail, gates = (cols(first, last) for first, last, _ in _IN_SEGMENTS)
    return jnp.concatenate(gates + head + [jnp.zeros((slots.shape[1], AB_PAD), slots.dtype)] + tail, axis=1)


def _in_cols_to_chips(g):
    shards = []
    for j in range(N_CHIPS):
        pieces = []
        for first, last, shift in _IN_SEGMENTS:
            lo, hi = max(first, j * IN_SHARD), min(last, (j + 1) * IN_SHARD)
            if lo < hi:
                pieces.append(g[:, lo + shift:hi + shift])
        shards.append(jnp.concatenate(pieces, axis=1))
    return jnp.stack(shards)


def _params(*sem):
    return pltpu.CompilerParams(dimension_semantics=sem if sem else None, vmem_limit_bytes=VMEM_LIMIT)


def _sigmoid(x):
    return 1.0 / (1.0 + jnp.exp(-x))


def _softplus(x):
    return jnp.maximum(x, 0.0) + jnp.log(1.0 + jnp.exp(-jnp.abs(x)))


def _softplus2(x):
    return jnp.maximum(x, 0.0) + jnp.log2(1.0 + jnp.exp2(-jnp.abs(x)))


def _dot(a, b, dims=None, precision=None):
    if dims is None:
        return jnp.dot(a, b, preferred_element_type=F32, precision=precision)
    return lax.dot_general(a, b, dims, preferred_element_type=F32, precision=precision)


def _bdot(a, b, dims=None):
    return _dot(a.astype(BF16), b.astype(BF16), dims)


def _matmul(a, b, *, name, ta=False, tb=False, tm, tn, tk=None, outs=(F32,), epi=None, extras=(), b_chips=False,
            out_chips=False):
    if ta:
        kdim, m = a.shape
    else:
        m, kdim = a.shape
    if b_chips:
        per = b.shape[2]
        if tb:
            n, kb = b.shape[1], N_CHIPS * per
        else:
            kb, n = b.shape[1], N_CHIPS * per
    elif tb:
        n, kb = b.shape
    else:
        kb, n = b.shape
    assert kdim == kb, (a.shape, b.shape)
    tk = kdim if tk is None else min(tk, kdim)
    tm = min(tm, m)
    assert m % tm == 0 and n % tn == 0 and kdim % tk == 0, (m, n, kdim, tm, tn, tk)
    nk = kdim // tk
    ni, nj = m // tm, n // tn
    a_bytes, b_bytes = a.size * a.dtype.itemsize, b.size * b.dtype.itemsize
    cols_inner = a_bytes * (nj if nk > 1 else 1) + b_bytes * (ni if nj * nk > 1 else 1)
    rows_inner = b_bytes * (ni if nk > 1 else 1) + a_bytes * (nj if ni * nk > 1 else 1)
    swap = rows_inner < cols_inner

    def ix(f):
        return (lambda g0, g1, k: f(g1, g0, k)) if swap else f

    a_spec = pl.BlockSpec((tk, tm), ix(lambda i, j, k: (k, i))) if ta else pl.BlockSpec((tm, tk), ix(lambda i, j, k: (i, k)))
    if b_chips and tb:
        assert per % tk == 0
        b_spec = pl.BlockSpec((None, tn, tk), ix(lambda i, j, k: (k // (per // tk), j, k % (per // tk))))
    elif b_chips:
        assert per % tn == 0
        b_spec = pl.BlockSpec((None, tk, tn), ix(lambda i, j, k: (j // (per // tn), k, j % (per // tn))))
    elif tb:
        b_spec = pl.BlockSpec((tn, tk), ix(lambda i, j, k: (j, k)))
    else:
        b_spec = pl.BlockSpec((tk, tn), ix(lambda i, j, k: (k, j)))
    mn_spec = pl.BlockSpec((tm, tn), ix(lambda i, j, k: (i, j)))
    if out_chips:
        per_o = n // N_CHIPS
        assert per_o % tn == 0
        out_spec = pl.BlockSpec((None, tm, tn), ix(lambda i, j, k: (j // (per_o // tn), i, j % (per_o // tn))))
        out_dims = (N_CHIPS, m, per_o)
    else:
        out_spec, out_dims = mn_spec, (m, n)
    dims = (((0 if ta else 1,), (1 if tb else 0,)), ((), ()))
    n_ex, n_out = len(extras), len(outs)

    def body(a_ref, b_ref, *rest):
        ex, o, acc = rest[:n_ex], rest[n_ex:n_ex + n_out], rest[n_ex + n_out:]
        part = lax.dot_general(a_ref[...].astype(BF16), b_ref[...].astype(BF16), dims, preferred_element_type=F32)

        def finish(val):
            res = epi(val, *[e[...] for e in ex]) if epi is not None else (val,)
            for r, oref in zip(res, o):
                oref[...] = r.astype(oref.dtype)

        if nk == 1:
            finish(part)
        else:
            k = pl.program_id(2)

            @pl.when(k == 0)
            def _():
                acc[0][...] = part

            @pl.when(k > 0)
            def _():
                acc[0][...] += part

            @pl.when(k == nk - 1)
            def _():
                finish(acc[0][...])

    res = pl.pallas_call(
        body, name=name, grid=(nj, ni, nk) if swap else (ni, nj, nk),
        in_specs=[a_spec, b_spec] + [mn_spec] * n_ex,
        out_specs=[out_spec] * n_out,
        out_shape=[jax.ShapeDtypeStruct(out_dims, dt) for dt in outs],
        scratch_shapes=[pltpu.VMEM((tm, tn), F32)] if nk > 1 else [],
        compiler_params=_params("parallel", "parallel", "arbitrary"),
    )(a, b, *extras)
    return res[0] if n_out == 1 else res


ROW_TILE = 512


def _norm_fwd(x, w, *, name):
    s, d = x.shape

    def body(x_ref, w_ref, o_ref):
        xv = x_ref[...]
        r = lax.rsqrt(jnp.mean(xv * xv, axis=-1, keepdims=True) + EPS)
        o_ref[...] = (xv * r * w_ref[...]).astype(o_ref.dtype)

    return pl.pallas_call(
        body, name=name, grid=(s // ROW_TILE,),
        in_specs=[pl.BlockSpec((ROW_TILE, d), lambda i: (i, 0)), pl.BlockSpec((1, d), lambda i: (0, 0))],
        out_specs=pl.BlockSpec((ROW_TILE, d), lambda i: (i, 0)),
        out_shape=jax.ShapeDtypeStruct((s, d), BF16), compiler_params=_params("parallel"),
    )(x, w.reshape(1, d))


def _resnorm_fwd(x, u, w, *, name):
    s, d = x.shape

    def body(x_ref, u_ref, w_ref, o_ref):
        uv = u_ref[...]
        r = lax.rsqrt(jnp.mean(uv * uv, axis=-1, keepdims=True) + EPS)
        o_ref[...] = x_ref[...] + uv * r * w_ref[...]

    row = pl.BlockSpec((ROW_TILE, d), lambda i: (i, 0))
    return pl.pallas_call(
        body, name=name, grid=(s // ROW_TILE,),
        in_specs=[row, row, pl.BlockSpec((1, d), lambda i: (0, 0))], out_specs=row,
        out_shape=jax.ShapeDtypeStruct((s, d), F32), compiler_params=_params("parallel"),
    )(x, u, w.reshape(1, d))


def _norm_bwd(xin, w, dy, res, *, out_dtype, name):
    s, d = xin.shape
    has_res = res is not None

    def body(*refs):
        x_ref, w_ref, dy_ref = refs[:3]
        res_ref = refs[3] if has_res else None
        dx_ref, dw_ref = refs[3 + has_res:]
        xv, dyv = x_ref[...], dy_ref[...].astype(F32)
        r = lax.rsqrt(jnp.mean(xv * xv, axis=-1, keepdims=True) + EPS)
        xh = xv * r
        g = dyv * w_ref[...]
        dx = r * (g - xh * jnp.mean(g * xh, axis=-1, keepdims=True))
        if has_res:
            dx = dx + res_ref[...]
        dx_ref[...] = dx.astype(dx_ref.dtype)

        @pl.when(pl.program_id(0) == 0)
        def _():
            dw_ref[...] = jnp.zeros_like(dw_ref)

        dw_ref[...] += jnp.sum(dyv * xh, axis=0, keepdims=True)

    row = pl.BlockSpec((ROW_TILE, d), lambda i: (i, 0))
    vec = pl.BlockSpec((1, d), lambda i: (0, 0))
    ins = [xin, w.reshape(1, d), dy] + ([res] if has_res else [])
    dx, dw = pl.pallas_call(
        body, name=name, grid=(s // ROW_TILE,),
        in_specs=[row, vec, row] + ([row] if has_res else []), out_specs=[row, vec],
        out_shape=[jax.ShapeDtypeStruct((s, d), out_dtype), jax.ShapeDtypeStruct((1, d), F32)],
        compiler_params=_params("arbitrary"),
    )(*ins)
    return dx, dw.reshape(d)


def _loss_fwd_bwd(y, target, *, name):
    s, d = y.shape

    def body(y_ref, t_ref, loss_ref, dy_ref):
        e = y_ref[...] - t_ref[...]
        dy_ref[...] = e * (1.0 / d)

        @pl.when(pl.program_id(0) == 0)
        def _():
            loss_ref[...] = jnp.zeros_like(loss_ref)

        part = jnp.sum(jnp.sum(e * e, axis=1, keepdims=True), axis=0, keepdims=True)
        loss_ref[...] += part * (0.5 / d)

    row = pl.BlockSpec((ROW_TILE, d), lambda i: (i, 0))
    loss, dy = pl.pallas_call(
        body, name=name, grid=(s // ROW_TILE,), in_specs=[row, row],
        out_specs=[pl.BlockSpec((1, 1), lambda i: (0, 0)), row],
        out_shape=[jax.ShapeDtypeStruct((1, 1), F32), jax.ShapeDtypeStruct((s, d), F32)],
        compiler_params=_params("arbitrary"),
    )(y, target)
    return loss[0, 0], dy


def _shift_down(x, k, rows):
    if k == 0:
        return x
    return jnp.where(rows >= k, pltpu.roll(x, k, 0), 0.0)


def _shift_up(x, k, rows):
    if k == 0:
        return x
    n = x.shape[0]
    return jnp.where(rows < n - k, pltpu.roll(x, n - k, 0), 0.0)


def _col_spec(s, base):
    return pl.BlockSpec((s, LANES), lambda j: (0, base + j))


def _gdn_pre_math(x, w, j, rows):
    taps = w.shape[0]
    c = w[taps - 1:taps, :] * x
    for i in range(taps - 1):
        c = c + w[i:i + 1, :] * _shift_down(x, taps - 1 - i, rows)
    sg = _sigmoid(c)
    y = c * sg
    r = lax.rsqrt(jnp.sum(y * y, axis=-1, keepdims=True) + EPS)
    is_qk = j < 2 * GDN_HEADS
    scale = jnp.where(j < GDN_HEADS, GDN_QSCALE, 1.0)
    return c, sg, y, r, is_qk, scale


def _gdn_pre_fwd(proj, conv_w, *, name):
    s = proj.shape[0]

    def body(x_ref, w_ref, o_ref):
        j = pl.program_id(0)
        rows = lax.broadcasted_iota(jnp.int32, (s, LANES), 0)
        _, _, y, r, is_qk, scale = _gdn_pre_math(x_ref[...], w_ref[...], j, rows)
        o_ref[...] = jnp.where(is_qk, y * (r * scale), y)

    return pl.pallas_call(
        body, name=name, grid=(12,),
        in_specs=[_col_spec(s, PB_QKV), pl.BlockSpec((4, LANES), lambda j: (0, j))],
        out_specs=_col_spec(s, 0), out_shape=jax.ShapeDtypeStruct((s, 3 * BRANCH_W), F32),
        compiler_params=_params("parallel"),
    )(proj, conv_w)


def _gdn_pre_bwd(proj, conv_w, dqkvn, *, name):
    s = proj.shape[0]

    def body(x_ref, w_ref, d_ref, dx_ref, dw_ref):
        j = pl.program_id(0)
        rows = lax.broadcasted_iota(jnp.int32, (s, LANES), 0)
        x, w, dout = x_ref[...], w_ref[...], d_ref[...]
        c, sg, y, r, is_qk, scale = _gdn_pre_math(x, w, j, rows)
        yh = y * r
        dy_n = (scale * r) * (dout - yh * jnp.sum(dout * yh, axis=-1, keepdims=True))
        dy = jnp.where(is_qk, dy_n, dout)
        dc = dy * (sg * (1.0 + c * (1.0 - sg)))
        taps = w.shape[0]
        dx = w[taps - 1:taps, :] * dc
        dws = []
        for i in range(taps - 1):
            k = taps - 1 - i
            dx = dx + w[i:i + 1, :] * _shift_up(dc, k, rows)
            dws.append(jnp.sum(dc * _shift_down(x, k, rows), axis=0, keepdims=True))
        dws.append(jnp.sum(dc * x, axis=0, keepdims=True))
        dx_ref[...] = dx.astype(dx_ref.dtype)
        for i in range(taps):
            dw_ref[i:i + 1, :] = dws[i]

    return pl.pallas_call(
        body, name=name, grid=(12,),
        in_specs=[_col_spec(s, PB_QKV), pl.BlockSpec((4, LANES), lambda j: (0, j)), _col_spec(s, 0)],
        out_specs=[_col_spec(s, 0), pl.BlockSpec((4, LANES), lambda j: (0, j))],
        out_shape=[jax.ShapeDtypeStruct((s, 3 * BRANCH_W), BF16), jax.ShapeDtypeStruct((4, 3 * BRANCH_W), F32)],
        compiler_params=_params("parallel"),
    )(proj, conv_w, dqkvn)


def _lane_pad(v):
    return jnp.pad(v.reshape(1, -1), ((0, 0), (0, LANES - v.shape[0])))


def _gdn_gates_fwd(proj, a_log, dt_bias, *, name):
    s = proj.shape[0]

    def body(ab_ref, al_ref, dt_ref, o_ref):
        ab = ab_ref[...]
        lane = lax.broadcasted_iota(jnp.int32, (1, LANES), 1)
        g = -jnp.exp(al_ref[...]) * _softplus(ab + dt_ref[...])
        o_ref[...] = jnp.where(lane < GDN_HEADS, g, _sigmoid(ab))

    vec = pl.BlockSpec((1, LANES), lambda j: (0, 0))
    return pl.pallas_call(
        body, name=name, grid=(1,), in_specs=[_col_spec(s, PB_AB), vec, vec], out_specs=_col_spec(s, 0),
        out_shape=jax.ShapeDtypeStruct((s, LANES), F32), compiler_params=_params("arbitrary"),
    )(proj, _lane_pad(a_log), _lane_pad(dt_bias))


def _gdn_gates_bwd(proj, a_log, dt_bias, dgb, *, name):
    s = proj.shape[0]

    def body(ab_ref, al_ref, dt_ref, d_ref, dab_ref, dal_ref, ddt_ref):
        ab, d = ab_ref[...], d_ref[...]
        lane = lax.broadcasted_iota(jnp.int32, (1, LANES), 1)
        ea = jnp.exp(al_ref[...])
        pre = ab + dt_ref[...]
        g = -ea * _softplus(pre)
        dpre = d * (-ea) * _sigmoid(pre)
        beta = _sigmoid(ab)
        is_g = lane < GDN_HEADS
        dab = jnp.where(is_g, dpre, jnp.where(lane < 2 * GDN_HEADS, d * beta * (1.0 - beta), 0.0))
        dab_ref[...] = dab.astype(dab_ref.dtype)
        dal_ref[...] = jnp.sum(jnp.where(is_g, d * g, 0.0), axis=0, keepdims=True)
        ddt_ref[...] = jnp.sum(jnp.where(is_g, dpre, 0.0), axis=0, keepdims=True)

    vec = pl.BlockSpec((1, LANES), lambda j: (0, 0))
    dab, dal, ddt = pl.pallas_call(
        body, name=name, grid=(1,), in_specs=[_col_spec(s, PB_AB), vec, vec, _col_spec(s, 0)],
        out_specs=[_col_spec(s, 0), vec, vec],
        out_shape=[jax.ShapeDtypeStruct((s, LANES), BF16), jax.ShapeDtypeStruct((1, LANES), F32),
                   jax.ShapeDtypeStruct((1, LANES), F32)],
        compiler_params=_params("arbitrary"),
    )(proj, _lane_pad(a_log), _lane_pad(dt_bias), dgb)
    return dab, dal[0, :GDN_HEADS], ddt[0, :GDN_HEADS]


def _interleave(gens):
    results, live = [None] * len(gens), list(range(len(gens)))
    while live:
        for idx in list(live):
            try:
                next(gens[idx])
            except StopIteration as done:
                results[idx] = done.value
                live.remove(idx)
    return results


def _chunk_common(q, k, v, gb, gbt, h):
    c = CHUNK
    row = lax.broadcasted_iota(jnp.int32, (c, c), 0)
    col = lax.broadcasted_iota(jnp.int32, (c, c), 1)
    tril, strict, eye = row >= col, row > col, row == col
    lane = lax.broadcasted_iota(jnp.int32, (c, LANES), 1)
    sub = lax.broadcasted_iota(jnp.int32, (2 * GDN_HEADS, c), 0)
    g_col = jnp.sum(jnp.where(lane == h, gb, 0.0), axis=1, keepdims=True)
    beta_col = jnp.sum(jnp.where(lane == GDN_HEADS + h, gb, 0.0), axis=1, keepdims=True)
    g_row = jnp.sum(jnp.where(sub == h, gbt, 0.0), axis=0, keepdims=True)
    gc_col = jnp.sum(jnp.where(tril, jnp.broadcast_to(g_row, (c, c)), 0.0), axis=1, keepdims=True)
    gc_row = jnp.sum(jnp.where(row <= col, jnp.broadcast_to(g_col, (c, c)), 0.0), axis=0, keepdims=True)
    g_tot = jnp.sum(g_row, axis=1, keepdims=True)
    dm = jnp.exp(jnp.where(tril, gc_col - gc_row, -1e30))
    e_col = jnp.exp(gc_col)
    kdec_col = jnp.exp(g_tot - gc_col)
    gamma = jnp.exp(g_tot)
    kb = k * beta_col
    vb = v * beta_col
    kbg = kb * e_col
    kk = _bdot(kb, k, NT)
    qk = _bdot(q, k, NT)
    yield
    a = jnp.where(strict, kk * dm, 0.0)
    aqk = jnp.where(tril, qk * dm, 0.0)
    bneg = -a
    t = jnp.where(eye, 1.0, 0.0) + bneg
    p = _dot(bneg, bneg, precision=HI)
    yield
    for lvl in range(5):
        t_next = t + _dot(t, p, precision=HI)
        if lvl < 4:
            p = _dot(p, p, precision=HI)
        t = t_next
        yield
    u = _dot(t, vb, precision=HI)
    w = _dot(t, kbg, precision=HI)
    yield
    return dict(tril=tril, strict=strict, eye=eye, row=row, col=col, beta_col=beta_col, dm=dm, e_col=e_col,
                kdec_col=kdec_col, gamma=gamma, kb=kb, vb=vb, kbg=kbg, a=a, t=t, u=u, w=w, aqk=aqk,
                qd=q * e_col, kd=k * kdec_col)


def _gdn_chunk_fwd(qkvn, gb, gbt, *, name):
    s = qkvn.shape[0]
    n_chunks = s // CHUNK
    per = GDN_CHUNKS_PER_STEP
    rows_per = per * CHUNK

    def body(q_ref, k_ref, v_ref, gb_ref, gbt_ref, o_ref, st_ref, state):
        @pl.when(pl.program_id(0) == 0)
        def _():
            state[...] = jnp.zeros_like(state)

        def head(h, ci):
            hs = slice(h * GDN_DIM, (h + 1) * GDN_DIM)
            rows = slice(ci * CHUNK, (ci + 1) * CHUNK)
            q, k, v = q_ref[rows, hs], k_ref[rows, hs], v_ref[rows, hs]
            m = yield from _chunk_common(q, k, v, gb_ref[rows, :], gbt_ref[ci], h)
            for _ in range(ci):
                yield
            s0 = state[h]
            st_ref[ci, h] = s0
            vnew = m["u"] - _bdot(m["w"], s0)
            o_inter = _bdot(m["qd"], s0)
            yield
            o_ref[rows, hs] = o_inter + _bdot(m["aqk"], vnew)
            state[h] = m["gamma"] * s0 + _bdot(m["kd"], vnew, TN)

        _interleave([head(h, ci) for ci in range(per) for h in range(GDN_HEADS)])

    blk = lambda j: pl.BlockSpec((rows_per, BRANCH_W), lambda n: (n, j))
    return pl.pallas_call(
        body, name=name, grid=(n_chunks // per,),
        in_specs=[blk(0), blk(1), blk(2), pl.BlockSpec((rows_per, LANES), lambda n: (n, 0)),
                  pl.BlockSpec((per, 2 * GDN_HEADS, CHUNK), lambda n: (n, 0, 0))],
        out_specs=[blk(0), pl.BlockSpec((per, GDN_HEADS, GDN_DIM, GDN_DIM), lambda n: (n, 0, 0, 0))],
        out_shape=[jax.ShapeDtypeStruct((s, BRANCH_W), F32),
                   jax.ShapeDtypeStruct((n_chunks, GDN_HEADS, GDN_DIM, GDN_DIM), F32)],
        scratch_shapes=[pltpu.VMEM((GDN_HEADS, GDN_DIM, GDN_DIM), F32)],
        compiler_params=_params("arbitrary"),
    )(qkvn, qkvn, qkvn, gb, gbt)


def _gdn_chunk_bwd(qkvn, gb, gbt, states, do, *, name):
    s = qkvn.shape[0]
    n_chunks = s // CHUNK
    c = CHUNK
    per = GDN_CHUNKS_PER_STEP
    rows_per = per * CHUNK

    def body(q_ref, k_ref, v_ref, gb_ref, gbt_ref, st_ref, do_ref, dqkv_ref, dgb_ref, dstate):
        @pl.when(pl.program_id(0) == 0)
        def _():
            dstate[...] = jnp.zeros_like(dstate)

        lane = lax.broadcasted_iota(jnp.int32, (c, LANES), 1)

        def head(h, ci):
            hs = slice(h * GDN_DIM, (h + 1) * GDN_DIM)
            rows = slice(ci * CHUNK, (ci + 1) * CHUNK)
            q, k, v, dov = q_ref[rows, hs], k_ref[rows, hs], v_ref[rows, hs], do_ref[rows, hs]
            m = yield from _chunk_common(q, k, v, gb_ref[rows, :], gbt_ref[ci], h)
            for _ in range(per - 1 - ci):
                yield
            tril, strict, eye, row, col = m["tril"], m["strict"], m["eye"], m["row"], m["col"]
            s0, ds1 = st_ref[ci, h], dstate[h]
            vnew = m["u"] - _bdot(m["w"], s0)
            dvnew_a = _bdot(m["aqk"], dov, TN) + _bdot(m["kd"], ds1)
            dqd = _bdot(dov, s0, NT)
            ds_q = _bdot(m["qd"], dov, TN)
            dgamma = jnp.sum(jnp.sum(s0 * ds1, axis=1, keepdims=True), axis=0, keepdims=True)
            yield
            dvnew = dvnew_a
            daqk = jnp.where(tril, _bdot(dov, vnew, NT), 0.0)
            dkd = _bdot(vnew, ds1, NT)
            dw = -_bdot(dvnew, s0, NT)
            dstate[h] = m["gamma"] * ds1 + ds_q - _bdot(m["w"], dvnew, TN)
            dvb = _dot(m["t"], dvnew, TN, HI)
            yield
            dt = _dot(dvnew, m["vb"], NT, HI) + _dot(dw, m["kbg"], NT, HI)
            dkbg = _dot(m["t"], dw, TN, HI)
            dmq = daqk * m["dm"]
            dq = _bdot(dmq, k) + dqd * m["e_col"]
            dk_q = _bdot(dmq, q, TN)
            yield
            tdt = _dot(m["t"], dt, TN, HI)
            yield
            da = jnp.where(strict, -_dot(tdt, m["t"], NT, HI), 0.0)
            yield
            dmat = da * m["dm"]
            dkb = _bdot(dmat, k) + dkbg * m["e_col"]
            dk = (_bdot(dmat, m["kb"], TN) + dk_q + dkd * m["kdec_col"] + m["beta_col"] * dkb)
            yield
            dbeta_col = jnp.sum(dkb * k, axis=1, keepdims=True) + jnp.sum(dvb * v, axis=1, keepdims=True)
            e = da * m["a"] + daqk * m["aqk"]
            rs_kd = jnp.sum(dkd * m["kd"], axis=1, keepdims=True)
            e_colsum = jnp.sum(e, axis=0, keepdims=True)
            e_colsum_c = jnp.sum(jnp.where(eye, jnp.broadcast_to(e_colsum, (c, c)), 0.0), axis=1, keepdims=True)
            dgc = (jnp.sum(e, axis=1, keepdims=True) - e_colsum_c + jnp.sum(dqd * m["qd"], axis=1, keepdims=True)
                   - rs_kd + jnp.sum(dkbg * m["kbg"], axis=1, keepdims=True))
            last = jnp.sum(rs_kd, axis=0, keepdims=True) + dgamma * m["gamma"]
            dgc = dgc + jnp.where(lax.broadcasted_iota(jnp.int32, (c, 1), 0) == c - 1, last, 0.0)
            dgc_row = jnp.sum(jnp.where(eye, jnp.broadcast_to(dgc, (c, c)), 0.0), axis=0, keepdims=True)
            dg_col = jnp.sum(jnp.where(col >= row, jnp.broadcast_to(dgc_row, (c, c)), 0.0), axis=1, keepdims=True)
            for part, val in enumerate((dq, dk, m["beta_col"] * dvb)):
                lo = part * BRANCH_W + h * GDN_DIM
                dqkv_ref[rows, lo:lo + GDN_DIM] = val
            return jnp.where(lane == h, dg_col, 0.0) + jnp.where(lane == GDN_HEADS + h, dbeta_col, 0.0)

        order = [ci for ci in reversed(range(per))]
        parts = _interleave([head(h, ci) for ci in order for h in range(GDN_HEADS)])
        for pos, ci in enumerate(order):
            mine = parts[pos * GDN_HEADS:(pos + 1) * GDN_HEADS]
            dgb_ref[ci * CHUNK:(ci + 1) * CHUNK, :] = (mine[0] + mine[1]) + (mine[2] + mine[3])

    rev = lambda n: n_chunks // per - 1 - n
    blk = lambda j: pl.BlockSpec((rows_per, BRANCH_W), lambda n: (rev(n), j))
    return pl.pallas_call(
        body, name=name, grid=(n_chunks // per,),
        in_specs=[blk(0), blk(1), blk(2), pl.BlockSpec((rows_per, LANES), lambda n: (rev(n), 0)),
                  pl.BlockSpec((per, 2 * GDN_HEADS, CHUNK), lambda n: (rev(n), 0, 0)),
                  pl.BlockSpec((per, GDN_HEADS, GDN_DIM, GDN_DIM), lambda n: (rev(n), 0, 0, 0)), blk(0)],
        out_specs=[pl.BlockSpec((rows_per, 3 * BRANCH_W), lambda n: (rev(n), 0)),
                   pl.BlockSpec((rows_per, LANES), lambda n: (rev(n), 0))],
        out_shape=[jax.ShapeDtypeStruct((s, 3 * BRANCH_W), F32), jax.ShapeDtypeStruct((s, LANES), F32)],
        scratch_shapes=[pltpu.VMEM((GDN_HEADS, GDN_DIM, GDN_DIM), F32)],
        compiler_params=_params("arbitrary"),
    )(qkvn, qkvn, qkvn, gb, gbt, states, do)


def _gdn_post_fwd(o, proj, norm_w, *, name):
    s = o.shape[0]

    def body(o_ref, g_ref, w_ref, y_ref):
        ov, gv = o_ref[...], g_ref[...]
        r = lax.rsqrt(jnp.mean(ov * ov, axis=-1, keepdims=True) + EPS)
        y_ref[...] = (ov * r * w_ref[...] * (gv * _sigmoid(gv))).astype(y_ref.dtype)

    return pl.pallas_call(
        body, name=name, grid=(GDN_HEADS,),
        in_specs=[_col_spec(s, 0), _col_spec(s, PB_GATE), pl.BlockSpec((1, LANES), lambda j: (0, 0))],
        out_specs=_col_spec(s, 0), out_shape=jax.ShapeDtypeStruct((s, BRANCH_W), BF16),
        compiler_params=_params("parallel"),
    )(o, proj, norm_w.reshape(1, GDN_DIM))


def _gdn_post_bwd(o, proj, norm_w, dy, *, name):
    s = o.shape[0]

    def body(o_ref, g_ref, w_ref, dy_ref, do_ref, dg_ref, dw_ref):
        ov, gv, w, dyv = o_ref[...], g_ref[...], w_ref[...], dy_ref[...].astype(F32)
        r = lax.rsqrt(jnp.mean(ov * ov, axis=-1, keepdims=True) + EPS)
        oh = ov * r
        sg = _sigmoid(gv)
        silu = gv * sg
        dn = dyv * silu
        dg_ref[...] = (dyv * (oh * w) * (sg * (1.0 + gv * (1.0 - sg)))).astype(dg_ref.dtype)

        @pl.when(pl.program_id(0) == 0)
        def _():
            dw_ref[...] = jnp.zeros_like(dw_ref)

        dw_ref[...] += jnp.sum(dn * oh, axis=0, keepdims=True)
        g2 = dn * w
        do_ref[...] = r * (g2 - oh * jnp.mean(g2 * oh, axis=-1, keepdims=True))

    do, dg, dw = pl.pallas_call(
        body, name=name, grid=(GDN_HEADS,),
        in_specs=[_col_spec(s, 0), _col_spec(s, PB_GATE), pl.BlockSpec((1, LANES), lambda j: (0, 0)), _col_spec(s, 0)],
        out_specs=[_col_spec(s, 0), _col_spec(s, 0), pl.BlockSpec((1, LANES), lambda j: (0, 0))],
        out_shape=[jax.ShapeDtypeStruct((s, BRANCH_W), F32), jax.ShapeDtypeStruct((s, BRANCH_W), BF16),
                   jax.ShapeDtypeStruct((1, LANES), F32)],
        compiler_params=_params("arbitrary"),
    )(o, proj, norm_w.reshape(1, GDN_DIM), dy)
    return do, dg, dw.reshape(LANES)


def _split_terms(x):
    hi = x.astype(BF16)
    lo = (x - hi.astype(F32)).astype(BF16)
    return jnp.concatenate([hi, lo], axis=1)


def _sb_sum_matrix(pred):
    row = lax.broadcasted_iota(jnp.int32, (2 * SB_TILE, 2 * SB_TILE), 0) % SB_TILE
    col = lax.broadcasted_iota(jnp.int32, (2 * SB_TILE, 2 * SB_TILE), 1)
    return jnp.where((col >= SB_TILE) | pred(row, col), 1.0, 0.0).astype(BF16)


def _sb_head_masks():
    lane = lax.broadcasted_iota(jnp.int32, (1, LANES), 1)
    return [(lane < SB_DIM).astype(F32), (lane >= SB_DIM).astype(F32)]


def _sb_fwd(proj, *, name):
    s = proj.shape[0]
    t, tq = SB_TILE, SB_QTILE
    nq = s // tq

    def body(q_ref, k_ref, v_ref, o_ref, tot_ref):
        cmr = lax.broadcasted_iota(jnp.int32, (tq, t), 1) - lax.broadcasted_iota(jnp.int32, (tq, t), 0)
        uo = _sb_sum_matrix(lambda row, col: row > col)
        hm = _sb_head_masks()

        def qloop(i, carry0):
            qs = pl.multiple_of(i * tq, tq)
            qf = q_ref[pl.ds(qs, tq), :] * SB_SCALE2
            qh = [(qf * hm[h]).astype(BF16) for h in range(2)]
            diag = (i * tq) // (SB_GROUP * t)

            def group(g, st, masked):
                ks = pl.multiple_of(g * (SB_GROUP * t), SB_GROUP * t)
                kb = k_ref[pl.ds(ks, SB_GROUP * t), :].astype(BF16)
                vf = v_ref[pl.ds(ks, SB_GROUP * t), :]
                tiles = [(h, j) for h in range(2) for j in range(SB_GROUP)]
                z = [_dot(qh[h], kb, NT) for h in range(2)]
                keep = {j: cmr < i * tq - (g * SB_GROUP + j) * t for j in range(SB_GROUP)} if masked else None
                base, terms = {}, {}
                for h, j in tiles:
                    zj = z[h][:, j * t:(j + 1) * t]
                    sp = _softplus2(zj)
                    base[h, j] = zj - sp
                    terms[h, j] = _split_terms(jnp.where(keep[j], sp, 0.0) if masked else sp)
                sums = {hj: _dot(terms[hj], uo) for hj in tiles}
                acc, new = st[0], []
                for h in range(2):
                    run, att = st[1 + h], [None] * SB_GROUP
                    for j in reversed(range(SB_GROUP)):
                        a = jnp.exp2(base[h, j] - (sums[h, j][:, :t] + run))
                        att[j] = (jnp.where(keep[j], a, 0.0) if masked else a).astype(BF16)
                        run = run + sums[h, j][:, t:]
                    acc = acc + _dot(jnp.concatenate(att, axis=1), (vf * hm[h]).astype(BF16))
                    new.append(run)
                return (acc, *new)

            zero = jnp.zeros((tq, LANES), F32)
            st = group(diag, (zero, zero, zero), True)
            st = lax.fori_loop(0, diag, lambda jj, sv: group(diag - 1 - jj, sv, False), st)
            o_ref[pl.ds(qs, tq), :] = st[0]
            tot_ref[pl.ds(qs, tq), :] = st[1] * hm[0] + st[2] * hm[1]
            return carry0

        lax.fori_loop(0, nq, qloop, 0)

    out = jax.ShapeDtypeStruct((s, BRANCH_W), F32)
    return pl.pallas_call(
        body, name=name, grid=(SB_HEADS // 2,),
        in_specs=[_col_spec(s, PB_SB), _col_spec(s, PB_SB + 4), _col_spec(s, PB_SB + 8)],
        out_specs=[_col_spec(s, 0)] * 2, out_shape=[out] * 2,
        compiler_params=_params("parallel"),
    )(proj, proj, proj)


def _sb_bwd(proj, tot, do, *, name):
    s = proj.shape[0]
    t, tq = SB_TILE, 2 * SB_QTILE
    nq = s // tq

    def body(q_ref, k_ref, v_ref, tot_ref, do_ref, dq_ref, dk_ref, dv_ref, dk_acc, dv_acc):
        dk_acc[...] = jnp.zeros_like(dk_acc)
        dv_acc[...] = jnp.zeros_like(dv_acc)
        cmr = lax.broadcasted_iota(jnp.int32, (tq, t), 1) - lax.broadcasted_iota(jnp.int32, (tq, t), 0)
        u_le = _sb_sum_matrix(lambda row, col: row <= col)
        u_lt = _sb_sum_matrix(lambda row, col: row < col)
        hm = _sb_head_masks()

        def qloop(i, carry0):
            qs = pl.multiple_of(i * tq, tq)
            qraw = q_ref[pl.ds(qs, tq), :]
            dov = do_ref[pl.ds(qs, tq), :].astype(F32)
            totv = tot_ref[pl.ds(qs, tq), :]
            qh = [(qraw * (hm[h] * SB_SCALE2)).astype(BF16) for h in range(2)]
            q2 = jnp.concatenate([(qraw * hm[h]).astype(BF16) for h in range(2)], axis=0)
            doh = [(dov * hm[h]).astype(BF16) for h in range(2)]
            do2 = jnp.concatenate(doh, axis=0)
            tot = [jnp.max(totv * hm[h], axis=1, keepdims=True) for h in range(2)]
            diag = (i * tq) // (SB_GROUP * t)

            def group(g, st, masked):
                ks = pl.multiple_of(g * (SB_GROUP * t), SB_GROUP * t)
                kf = k_ref[pl.ds(ks, SB_GROUP * t), :]
                kb = kf.astype(BF16)
                vb = v_ref[pl.ds(ks, SB_GROUP * t), :].astype(BF16)
                tiles = [(h, j) for h in range(2) for j in range(SB_GROUP)]
                keep = {j: cmr < i * tq - (g * SB_GROUP + j) * t for j in range(SB_GROUP)} if masked else None
                ls, lterms = {}, {}
                for h, j in tiles:
                    zj = _dot(qh[h], kb[j * t:(j + 1) * t, :], NT)
                    sp = _softplus2(zj)
                    ls[h, j] = zj - sp
                    lterms[h, j] = _split_terms(jnp.where(keep[j], sp, 0.0) if masked else sp)
                lsum = {hj: _dot(lterms[hj], u_le) for hj in tiles}
                att, p, pterms, new_c = {}, {}, {}, []
                for h in range(2):
                    run = st[1 + h]
                    for j in range(SB_GROUP):
                        a = jnp.exp2(ls[h, j] - ((tot[h] - run) - lsum[h, j][:, :t]))
                        if masked:
                            a = jnp.where(keep[j], a, 0.0)
                        att[h, j] = a.astype(BF16)
                        p[h, j] = a * _dot(doh[h], vb[j * t:(j + 1) * t, :], NT)
                        pterms[h, j] = _split_terms(p[h, j])
                        run = run + lsum[h, j][:, t:]
                    new_c.append(run)
                psum = {hj: _dot(pterms[hj], u_lt) for hj in tiles}
                dzb, new_r = {}, []
                for h in range(2):
                    run = st[3 + h]
                    for j in range(SB_GROUP):
                        sig = jnp.exp2(ls[h, j])
                        dz = p[h, j] - sig * (p[h, j] + run + psum[h, j][:, :t])
                        if masked:
                            dz = jnp.where(keep[j], dz, 0.0)
                        dzb[h, j] = (dz * SB_SCALE).astype(BF16)
                        run = run + psum[h, j][:, t:]
                    new_r.append(run)
                k2 = jnp.concatenate([(kf * hm[h]).astype(BF16) for h in range(2)], axis=0)
                dq_acc = st[0] + _dot(jnp.concatenate([dzb[hj] for hj in tiles], axis=1), k2)
                for j in range(SB_GROUP):
                    rows = pl.ds(pl.multiple_of(ks + j * t, t), t)
                    dk_acc[rows, :] += _dot(jnp.concatenate([dzb[0, j], dzb[1, j]], axis=0), q2, TN)
                    dv_acc[rows, :] += _dot(jnp.concatenate([att[0, j], att[1, j]], axis=0), do2, TN)
                return (dq_acc, *new_c, *new_r)

            zero = jnp.zeros((tq, LANES), F32)
            st = lax.fori_loop(0, diag, lambda jj, sv: group(jj, sv, False), (zero,) * 5)
            st = group(diag, st, True)
            dq_ref[pl.ds(qs, tq), :] = st[0].astype(dq_ref.dtype)
            return carry0

        lax.fori_loop(0, nq, qloop, 0)
        dk_ref[...] = dk_acc[...].astype(dk_ref.dtype)
        dv_ref[...] = dv_acc[...].astype(dv_ref.dtype)

    out = jax.ShapeDtypeStruct((s, BRANCH_W), BF16)
    return pl.pallas_call(
        body, name=name, grid=(SB_HEADS // 2,),
        in_specs=[_col_spec(s, PB_SB), _col_spec(s, PB_SB + 4), _col_spec(s, PB_SB + 8), _col_spec(s, 0), _col_spec(s, 0)],
        out_specs=[_col_spec(s, 0)] * 3, out_shape=[out] * 3,
        scratch_shapes=[pltpu.VMEM((s, LANES), F32), pltpu.VMEM((s, LANES), F32)],
        compiler_params=_params("parallel"),
    )(proj, proj, proj, tot, do)


def _sc_fwd(proj, conv_w, *, name):
    s = proj.shape[0]

    def body(x_ref, b_ref, c_ref, w_ref, y_ref):
        rows = lax.broadcasted_iota(jnp.int32, (s, LANES), 0)
        w = w_ref[...]
        u = c_ref[...] * x_ref[...]
        cv = w[2:3, :] * u + w[1:2, :] * _shift_down(u, 1, rows) + w[0:1, :] * _shift_down(u, 2, rows)
        y_ref[...] = (b_ref[...] * cv).astype(y_ref.dtype)

    return pl.pallas_call(
        body, name=name, grid=(BRANCH_W // LANES,),
        in_specs=[_col_spec(s, PB_SCX), _col_spec(s, PB_SCB), _col_spec(s, PB_SCC), pl.BlockSpec((3, LANES), lambda j: (0, j))],
        out_specs=_col_spec(s, 0), out_shape=jax.ShapeDtypeStruct((s, BRANCH_W), BF16),
        compiler_params=_params("parallel"),
    )(proj, proj, proj, conv_w)


def _sc_bwd(proj, conv_w, dy, *, name):
    s = proj.shape[0]

    def body(x_ref, b_ref, c_ref, w_ref, dy_ref, dx_ref, db_ref, dc_ref, dw_ref):
        rows = lax.broadcasted_iota(jnp.int32, (s, LANES), 0)
        w, x, cg, dyv = w_ref[...], x_ref[...], c_ref[...], dy_ref[...].astype(F32)
        u = cg * x
        u1, u2 = _shift_down(u, 1, rows), _shift_down(u, 2, rows)
        cv = w[2:3, :] * u + w[1:2, :] * u1 + w[0:1, :] * u2
        db_ref[...] = (dyv * cv).astype(db_ref.dtype)
        dcv = dyv * b_ref[...]
        du = w[2:3, :] * dcv + w[1:2, :] * _shift_up(dcv, 1, rows) + w[0:1, :] * _shift_up(dcv, 2, rows)
        dx_ref[...] = (du * cg).astype(dx_ref.dtype)
        dc_ref[...] = (du * x).astype(dc_ref.dtype)
        dw_ref[0:1, :] = jnp.sum(dcv * u2, axis=0, keepdims=True)
        dw_ref[1:2, :] = jnp.sum(dcv * u1, axis=0, keepdims=True)
        dw_ref[2:3, :] = jnp.sum(dcv * u, axis=0, keepdims=True)

    out = jax.ShapeDtypeStruct((s, BRANCH_W), BF16)
    wspec = pl.BlockSpec((3, LANES), lambda j: (0, j))
    return pl.pallas_call(
        body, name=name, grid=(BRANCH_W // LANES,),
        in_specs=[_col_spec(s, PB_SCX), _col_spec(s, PB_SCB), _col_spec(s, PB_SCC), wspec, _col_spec(s, 0)],
        out_specs=[_col_spec(s, 0)] * 3 + [wspec],
        out_shape=[out] * 3 + [jax.ShapeDtypeStruct((3, BRANCH_W), F32)],
        compiler_params=_params("parallel"),
    )(proj, proj, proj, conv_w, dy)


MERGE_TM, MERGE_TN = 1024, D_MODEL // N_CHIPS


def _merge_specs():
    tm, tn = MERGE_TM, MERGE_TN
    y_spec = pl.BlockSpec((tm, BRANCH_W), lambda i, j: (i, 0))
    w_spec = pl.BlockSpec((None, N_BRANCH, BRANCH_W, tn), lambda i, j: (j, 0, 0, 0))
    gate_specs = [pl.BlockSpec((tm, tn), functools.partial(
        lambda i, j, b: (i, (PB_GATES * LANES + b * D_MODEL) // tn + j), b=b)) for b in range(N_BRANCH)]
    mn = pl.BlockSpec((tm, tn), lambda i, j: (i, j))
    return y_spec, w_spec, gate_specs, mn


def _merge_fwd(ya, yb, yc, wb, proj, *, name):
    s = ya.shape[0]
    y_spec, w_spec, gate_specs, mn = _merge_specs()

    def body(ya_ref, yb_ref, yc_ref, w_ref, g0, g1, g2, o_ref):
        acc = None
        for b, (y_ref, g_ref) in enumerate(zip((ya_ref, yb_ref, yc_ref), (g0, g1, g2))):
            term = _sigmoid(g_ref[...]) * _bdot(y_ref[...], w_ref[b])
            acc = term if acc is None else acc + term
        o_ref[...] = acc.astype(o_ref.dtype)

    return pl.pallas_call(
        body, name=name, grid=(s // MERGE_TM, D_MODEL // MERGE_TN),
        in_specs=[y_spec] * 3 + [w_spec] + gate_specs, out_specs=mn,
        out_shape=jax.ShapeDtypeStruct((s, D_MODEL), BF16), compiler_params=_params("parallel", "parallel"),
    )(ya, yb, yc, wb, proj, proj, proj)


def _merge_bwd(ya, yb, yc, wb, proj, dm, *, name):
    s = ya.shape[0]
    y_spec, w_spec, gate_specs, mn = _merge_specs()

    def body(ya_ref, yb_ref, yc_ref, w_ref, g0, g1, g2, dm_ref, *outs):
        dmv = dm_ref[...].astype(F32)
        for b, (y_ref, g_ref) in enumerate(zip((ya_ref, yb_ref, yc_ref), (g0, g1, g2))):
            sg = _sigmoid(g_ref[...])
            z = _bdot(y_ref[...], w_ref[b])
            outs[b][...] = (dmv * sg).astype(BF16)
            outs[N_BRANCH + b][...] = (dmv * z * sg * (1.0 - sg)).astype(BF16)

    out = jax.ShapeDtypeStruct((s, D_MODEL), BF16)
    res = pl.pallas_call(
        body, name=name, grid=(s // MERGE_TM, D_MODEL // MERGE_TN),
        in_specs=[y_spec] * 3 + [w_spec] + gate_specs + [mn], out_specs=[mn] * (2 * N_BRANCH),
        out_shape=[out] * (2 * N_BRANCH), compiler_params=_params("parallel", "parallel"),
    )(ya, yb, yc, wb, proj, proj, proj, dm)
    return res[:N_BRANCH], res[N_BRANCH:]


def _chunk_rows(v, s):
    return v[:, :2 * GDN_HEADS].reshape(s // CHUNK, CHUNK, 2 * GDN_HEADS).transpose(0, 2, 1)


def _relu2_epi(acc):
    r = jnp.maximum(acc, 0.0)
    return acc, r * r


def _drelu2_epi(acc, a):
    return (acc * (2.0 * jnp.maximum(a.astype(F32), 0.0)),)


def _layer_fwd(x0, p, late=None):
    s = x0.shape[0]
    h1 = _norm_fwd(x0, p["norm_mix_pre"], name="norm_mix_pre")
    proj = _matmul(h1, p["w_in"], name="proj_in", tm=512, tn=1664)
    qkvn = _gdn_pre_fwd(proj, p["conv_qkv_w"], name="gdn_pre")
    gb = _gdn_gates_fwd(proj, p["gdn_a_log"], p["gdn_dt_bias"], name="gdn_gates")
    gbt = _chunk_rows(gb, s)
    o_gdn, states = _gdn_chunk_fwd(qkvn, gb, gbt, name="gdn_chunk")
    ya = _gdn_post_fwd(o_gdn, proj, p["gdn_norm_w"], name="gdn_post")
    o_sb, sb_tot = _sb_fwd(proj, name="sb_attn")
    yc = _sc_fwd(proj, p["conv_sc_w"], name="short_conv")
    if late is not None:
        p = dict(p, **late(yc))
    merged = _merge_fwd(ya, o_sb, yc, p["w_branch"], proj, name="merge")
    u = _matmul(merged, p["w_out"], name="proj_out", tm=512, tn=1024)
    x1 = _resnorm_fwd(x0, u, p["norm_mix_post"], name="norm_mix_post")
    h2 = _norm_fwd(x1, p["norm_ffn_pre"], name="norm_ffn_pre")
    a, r = _matmul(h2, p["w_ff1"], name="ff1", tm=512, tn=1024, outs=(BF16, BF16), epi=_relu2_epi, b_chips=True)
    f = _matmul(r, p["w_ff2"], name="ff2", tm=1024, tn=1024, tk=2048)
    x2 = _resnorm_fwd(x1, f, p["norm_ffn_post"], name="norm_ffn_post")
    saved = dict(x0=x0, h1=h1, proj=proj, qkvn=qkvn, gb=gb, gbt=gbt, o_gdn=o_gdn, states=states, ya=ya, o_sb=o_sb,
                 sb_tot=sb_tot, yc=yc, merged=merged, u=u, x1=x1, h2=h2, a=a, r=r, f=f)
    return x2, saved, p


def _layer_bwd(dx2, p, sv, early=None):
    g = {}
    df, g["norm_ffn_post"] = _norm_bwd(sv["f"], p["norm_ffn_post"], dx2, None, out_dtype=BF16, name="norm_ffn_post_bwd")
    g["w_ff2"] = _matmul(sv["r"], df, ta=True, name="ff2_dw", tm=1024, tn=1024, tk=2048, outs=(BF16,))
    da = _matmul(df, p["w_ff2"], tb=True, name="ff2_dx", tm=512, tn=1024, outs=(BF16,), epi=_drelu2_epi,
                 extras=(sv["a"],))
    g["w_ff1"] = _matmul(sv["h2"], da, ta=True, name="ff1_dw", tm=1024, tn=1024, tk=2048, out_chips=True, outs=(BF16,))
    dh2 = _matmul(da, p["w_ff1"], tb=True, name="ff1_dx", tm=1024, tn=1024, tk=1024, b_chips=True)
    dx1, g["norm_ffn_pre"] = _norm_bwd(sv["x1"], p["norm_ffn_pre"], dh2, dx2, out_dtype=F32, name="norm_ffn_pre_bwd")
    du, g["norm_mix_post"] = _norm_bwd(sv["u"], p["norm_mix_post"], dx1, None, out_dtype=BF16, name="norm_mix_post_bwd")
    g["w_out"] = _matmul(sv["merged"], du, ta=True, name="out_dw", tm=1024, tn=1024, tk=2048, outs=(BF16,))
    dmerged = _matmul(du, p["w_out"], tb=True, name="out_dx", tm=512, tn=1024, outs=(BF16,))
    ys = (sv["ya"], sv["o_sb"], sv["yc"])
    dz, dgates = _merge_bwd(*ys, p["w_branch"], sv["proj"], dmerged, name="merge_bwd")
    g["w_branch"] = jnp.stack([_matmul(ys[b], dz[b], ta=True, name=f"branch_dw{b}", tm=512, tn=256, tk=1024, out_chips=True,
                                       outs=(BF16,)) for b in range(N_BRANCH)], axis=1)
    dys = [_matmul(dz[b], p["w_branch"][:, b], tb=True, name=f"branch_dx{b}", tm=1024, tn=512, tk=256, b_chips=True)
           for b in range(N_BRANCH)]
    conv_sc_w = p["conv_sc_w"]
    if early is not None:
        conv_sc_w = conv_sc_w + early({k: g[k] for k in ("w_branch", "w_out", "w_ff1", "w_ff2")})
    dscx, dscb, dscc, g["conv_sc_w"] = _sc_bwd(sv["proj"], conv_sc_w, dys[2], name="short_conv_bwd")
    dsq, dsk, dsv = _sb_bwd(sv["proj"], sv["sb_tot"], dys[1], name="sb_attn_bwd")
    do_gdn, dgate, dnw = _gdn_post_bwd(sv["o_gdn"], sv["proj"], p["gdn_norm_w"], dys[0], name="gdn_post_bwd")
    g["gdn_norm_w"] = dnw
    dqkvn, dgb = _gdn_chunk_bwd(sv["qkvn"], sv["gb"], sv["gbt"], sv["states"], do_gdn, name="gdn_chunk_bwd")
    dqkv, g["conv_qkv_w"] = _gdn_pre_bwd(sv["proj"], p["conv_qkv_w"], dqkvn, name="gdn_pre_bwd")
    dab, g["gdn_a_log"], g["gdn_dt_bias"] = _gdn_gates_bwd(sv["proj"], p["gdn_a_log"], p["gdn_dt_bias"], dgb,
                                                           name="gdn_gates_bwd")
    dproj = jnp.concatenate([*dgates, dqkv, dgate, dab, dsq, dsk, dsv, dscx, dscb, dscc], axis=1)
    g["w_in"] = _matmul(sv["h1"], dproj, ta=True, name="in_dw", tm=1024, tn=1664, tk=1024, outs=(BF16,))
    dh1 = _matmul(dproj, p["w_in"], tb=True, name="in_dx", tm=1024, tn=1024, tk=1664)
    dx0, g["norm_mix_pre"] = _norm_bwd(sv["x0"], p["norm_mix_pre"], dh1, dx1, out_dtype=F32, name="norm_mix_pre_bwd")
    return dx0, g


def _local_step(x, target, n_layers, weights_of, grads_done, grads_early=None):
    saved, layers = [], []
    h = x
    for l in range(n_layers):
        p, late = weights_of(l, h)
        h, sv, p = _layer_fwd(h, p, late)
        saved.append(sv)
        layers.append(p)
    loss, dh = _loss_fwd_bwd(h, target, name="loss")
    for l in reversed(range(n_layers)):
        dh, g = _layer_bwd(dh, layers[l], saved[l], grads_early if l == 0 else None)
        zero = grads_done(l, g)
        if l > 0:
            layers[l - 1] = dict(layers[l - 1], norm_ffn_post=layers[l - 1]["norm_ffn_post"] + zero)
    return loss, dh


ANY = pl.BlockSpec(memory_space=pl.ANY)


def _me_and_chips():
    x, y, c = lax.axis_index("x"), lax.axis_index("y"), lax.axis_index("c")
    chips = [(1 - x, y), (x, 1 - y), (1 - x, 1 - y)]
    return x, y, c, chips


def _gather_devices(small, *, name):
    def body(small_ref, small_out, ssend, srecv, local_sem):
        x, y, c, chips = _me_and_chips()
        dev = 4 * x + 2 * y + c
        lc = pltpu.make_async_copy(small_ref, small_out.at[dev], local_sem)
        lc.start()
        peers = [(x, y, 1 - c)] + [(px, py, pc) for (px, py) in chips for pc in (c, 1 - c)]
        sends = []
        for k, peer in enumerate(peers):
            cp = pltpu.make_async_remote_copy(src_ref=small_ref, dst_ref=small_out.at[dev], send_sem=ssend.at[k],
                                              recv_sem=srecv.at[k], device_id=peer, device_id_type=MESH)
            cp.start()
            sends.append(cp)
        for k, (px, py, pc) in enumerate(peers):
            pltpu.make_async_remote_copy(src_ref=small_ref, dst_ref=small_out.at[4 * px + 2 * py + pc], send_sem=ssend.at[k],
                                         recv_sem=srecv.at[k], device_id=(px, py, pc), device_id_type=MESH).wait_recv()
        for cp in sends:
            cp.wait_send()
        lc.wait()

    return pl.pallas_call(
        body, name=name, in_specs=[ANY], out_specs=ANY,
        out_shape=jax.ShapeDtypeStruct((N_DEV,) + small.shape, small.dtype),
        scratch_shapes=[pltpu.SemaphoreType.DMA((N_DEV - 1,)), pltpu.SemaphoreType.DMA((N_DEV - 1,)), pltpu.SemaphoreType.DMA],
    )(small)


HBM = pl.BlockSpec(memory_space=pltpu.HBM)
SEM = pl.BlockSpec(memory_space=pltpu.SEMAPHORE)
EFFECT = pltpu.SideEffectType.DATAFLOW_SIDE_EFFECTING


def _exchange_start(srcs, *, by_slot, name, after=None):
    n = len(srcs)
    n_in = 2 * n + (after is not None)
    land_shapes = [a.shape if by_slot else (N_CHIPS,) + a.shape for a in srcs]
    lands = [pltpu.with_memory_space_constraint(lax.empty(sh, a.dtype), pltpu.HBM) for sh, a in zip(land_shapes, srcs)]
    srcs = [pltpu.with_memory_space_constraint(a, pltpu.HBM) for a in srcs]

    def body(*refs):
        ins, land = refs[:n], refs[n:2 * n]
        send_sems, recv_sems, token = refs[n_in], refs[n_in + 1], refs[-1]
        x, y, c, chips = _me_and_chips()
        me = 2 * x + y
        for a in range(n):
            for k, (px, py) in enumerate(chips):
                pltpu.make_async_remote_copy(
                    src_ref=ins[a].at[2 * px + py] if by_slot else ins[a], dst_ref=land[a].at[me],
                    send_sem=send_sems.at[3 * a + k], recv_sem=recv_sems.at[3 * a + k], device_id=(px, py, c),
                    device_id_type=MESH).start()
        token[...] = jnp.zeros_like(token)

    res = pl.pallas_call(
        body, name=name, in_specs=[HBM] * (2 * n) + ([ANY] if after is not None else []),
        out_specs=[SEM, SEM] + [HBM] * (2 * n) + [pl.BlockSpec(memory_space=pltpu.VMEM)],
        out_shape=[pltpu.SemaphoreType.DMA((3 * n,)), pltpu.SemaphoreType.DMA((3 * n,))]
        + [pltpu.HBM(a.shape, a.dtype) for a in srcs] + [pltpu.HBM(sh, a.dtype) for sh, a in zip(land_shapes, srcs)]
        + [jax.ShapeDtypeStruct((8, LANES), F32)],
        input_output_aliases={i: 2 + i for i in range(2 * n)},
        compiler_params=pltpu.CompilerParams(has_side_effects=EFFECT),
    )(*srcs, *lands, *([after] if after is not None else []))
    return dict(send=res[0], recv=res[1], srcs=res[2:2 + n], lands=res[2 + n:2 + 2 * n], token=res[-1])


def _exchange_wait(ex, after, *, by_slot, name):
    n = len(ex["srcs"])

    def body(*refs):
        ins, land = refs[:n], refs[n:2 * n]
        send_sems, recv_sems = refs[2 * n], refs[2 * n + 1]
        x, y, c, chips = _me_and_chips()
        me = 2 * x + y
        for a in range(n):
            for k, (px, py) in enumerate(chips):
                cp = pltpu.make_async_remote_copy(
                    src_ref=ins[a].at[me] if by_slot else ins[a], dst_ref=land[a].at[2 * px + py],
                    send_sem=send_sems.at[3 * a + k], recv_sem=recv_sems.at[3 * a + k], device_id=(px, py, c),
                    device_id_type=MESH)
                cp.wait_send()
                cp.wait_recv()

    res = pl.pallas_call(
        body, name=name, in_specs=[HBM] * (2 * n) + [SEM, SEM, ANY], out_specs=[HBM] * (2 * n),
        out_shape=[pltpu.HBM(a.shape, a.dtype) for a in ex["srcs"]] + [pltpu.HBM(a.shape, a.dtype) for a in ex["lands"]],
        input_output_aliases={i: i for i in range(2 * n)},
        compiler_params=pltpu.CompilerParams(has_side_effects=EFFECT),
    )(*ex["srcs"], *ex["lands"], ex["send"], ex["recv"], after)
    return res[:n], res[n:]


def _chip_index():
    return 2 * lax.axis_index("x") + lax.axis_index("y")


def _me_operand():
    return jnp.reshape(_chip_index(), (1,)).astype(jnp.int32)


def _place_own(land, own, *, name):
    rows, cols = _as2d(own).shape
    tr = _row_tile(rows, cols)

    def body(me_ref, own_ref, land_ref, out_ref):
        out_ref[...] = own_ref[...]

    res = pl.pallas_call(
        body, name=name,
        grid_spec=pltpu.PrefetchScalarGridSpec(
            num_scalar_prefetch=1, grid=(rows // tr,),
            in_specs=[pl.BlockSpec((tr, cols), lambda i, me: (i, 0)), ANY],
            out_specs=pl.BlockSpec((None, tr, cols), lambda i, me: (me[0], i, 0))),
        out_shape=jax.ShapeDtypeStruct((N_CHIPS, rows, cols), land.dtype), input_output_aliases={2: 0},
        compiler_params=_params("arbitrary"),
    )(_me_operand(), _as2d(own), land.reshape(N_CHIPS, rows, cols))
    return res.reshape(land.shape)


def _sum_partials(lands, parts, *, name):
    n, rows, cols = lands.shape
    tr = _row_tile(rows, cols, 1024 * 1024)

    def body(me_ref, land_ref, own_ref, o_ref):
        me = me_ref[0]
        acc = None
        for i in range(n):
            term = jnp.where(me == i, own_ref[...], land_ref[i]).astype(F32)
            acc = term if acc is None else acc + term
        o_ref[...] = acc

    return pl.pallas_call(
        body, name=name,
        grid_spec=pltpu.PrefetchScalarGridSpec(
            num_scalar_prefetch=1, grid=(rows // tr,),
            in_specs=[pl.BlockSpec((n, tr, cols), lambda i, me: (0, i, 0)),
                      pl.BlockSpec((None, tr, cols), lambda i, me: (me[0], i, 0))],
            out_specs=pl.BlockSpec((tr, cols), lambda i, me: (i, 0))),
        out_shape=jax.ShapeDtypeStruct((rows, cols), F32), compiler_params=_params("arbitrary"),
    )(_me_operand(), lands, parts)


def _swap_sibling(arrs, *, name):
    n = len(arrs)

    def body(*refs):
        ins, outs = refs[:n], refs[n:2 * n]
        send_sems, recv_sems = refs[2 * n:]
        x, y, c = lax.axis_index("x"), lax.axis_index("y"), lax.axis_index("c")
        cps = [pltpu.make_async_remote_copy(src_ref=ins[a], dst_ref=outs[a], send_sem=send_sems.at[a],
                                            recv_sem=recv_sems.at[a], device_id=(x, y, 1 - c), device_id_type=MESH)
               for a in range(n)]
        for cp in cps:
            cp.start()
        for cp in cps:
            cp.wait()

    return pl.pallas_call(
        body, name=name, in_specs=[ANY] * n, out_specs=[ANY] * n,
        out_shape=[jax.ShapeDtypeStruct(a.shape, a.dtype) for a in arrs],
        scratch_shapes=[pltpu.SemaphoreType.DMA((n,)), pltpu.SemaphoreType.DMA((n,))],
    )(*arrs)


def _row_tile(rows, cols, budget=2 * 1024 * 1024):
    best = None
    for t in range(16, rows + 1, 16):
        if rows % t == 0 and t * cols * 4 <= budget:
            best = t
    return best if best is not None else rows


def _sum_slots(parts, *, name):
    n, rows, cols = parts.shape
    tr = _row_tile(rows, cols, 1024 * 1024)

    def body(p_ref, o_ref):
        acc = p_ref[0].astype(F32)
        for i in range(1, n):
            acc = acc + p_ref[i].astype(F32)
        o_ref[...] = acc

    return pl.pallas_call(
        body, name=name, grid=(rows // tr,), in_specs=[pl.BlockSpec((n, tr, cols), lambda i: (0, i, 0))],
        out_specs=pl.BlockSpec((tr, cols), lambda i: (i, 0)), out_shape=jax.ShapeDtypeStruct((rows, cols), F32),
        compiler_params=_params("parallel"),
    )(parts)


def _adamw(w, m, v, g_a, g_b, *, name):
    rows, cols = w.shape
    tr = _row_tile(rows, cols)
    two = g_b is not None
    c1 = 1.0 / (1.0 - ADAM_B1 ** ADAM_STEP)
    c2 = 1.0 / (1.0 - ADAM_B2 ** ADAM_STEP)

    def body(*refs):
        w_ref, m_ref, v_ref, ga_ref = refs[:4]
        g_ref, d_ref, nm_ref, nv_ref = refs[4 + two:]
        g = ga_ref[...]
        if two:
            g = g + refs[4][...]
        nm = ADAM_B1 * m_ref[...] + (1.0 - ADAM_B1) * g
        nv = ADAM_B2 * v_ref[...] + (1.0 - ADAM_B2) * (g * g)
        g_ref[...] = g
        nm_ref[...] = nm
        nv_ref[...] = nv
        d_ref[...] = -ADAM_LR * ((nm * c1) / (jnp.sqrt(nv * c2) + ADAM_EPS) + ADAM_WD * w_ref[...])

    blk = pl.BlockSpec((tr, cols), lambda i: (i, 0))
    ins = [w, m, v, g_a] + ([g_b] if two else [])
    return pl.pallas_call(
        body, name=name, grid=(rows // tr,), in_specs=[blk] * len(ins), out_specs=[blk] * 4,
        out_shape=[jax.ShapeDtypeStruct((rows, cols), F32)] * 4, compiler_params=_params("parallel"),
    )(*ins)


def _cast_bf16(w, *, name):
    rows, cols = w.shape
    tr = _row_tile(rows, cols)

    def body(w_ref, o_ref):
        o_ref[...] = w_ref[...].astype(BF16)

    blk = pl.BlockSpec((tr, cols), lambda i: (i, 0))
    return pl.pallas_call(body, name=name, grid=(rows // tr,), in_specs=[blk], out_specs=blk,
                          out_shape=jax.ShapeDtypeStruct((rows, cols), BF16), compiler_params=_params("parallel"))(w)


BIG = ("w_in", "w_branch", "w_out", "w_ff1", "w_ff2")
SMALL = ("norm_mix_pre", "conv_qkv_w", "gdn_a_log", "gdn_dt_bias", "gdn_norm_w", "conv_sc_w", "norm_mix_post",
         "norm_ffn_pre", "norm_ffn_post")
ORDER = ("norm_mix_pre", "w_in", "conv_qkv_w", "gdn_a_log", "gdn_dt_bias", "gdn_norm_w", "conv_sc_w", "w_branch",
         "w_out", "norm_mix_post", "norm_ffn_pre", "w_ff1", "w_ff2", "norm_ffn_post")


_MATMUL_LAYOUT = dict(
    w_in=_in_cols_from_chips,
    w_branch=lambda a: a,
    w_out=lambda a: a.reshape(D_MODEL, D_MODEL),
    w_ff1=lambda a: a,
    w_ff2=lambda a: a.reshape(D_FF, D_MODEL),
)
_SHARD_LAYOUT = dict(
    w_in=_in_cols_to_chips,
    w_branch=lambda g: g,
    w_out=lambda g: g.reshape(N_CHIPS, D_MODEL // N_CHIPS, D_MODEL),
    w_ff1=lambda g: g,
    w_ff2=lambda g: g.reshape(N_CHIPS, D_FF // N_CHIPS, D_MODEL),
)


def _full_weights(big, conv, rep, l):
    p = {k: _MATMUL_LAYOUT[k](a) for k, a in big.items()}
    if conv is not None:
        p["conv_qkv_w"] = conv["conv_qkv_w"][:, l].transpose(1, 0, 2).reshape(4, 3 * BRANCH_W)
        p["conv_sc_w"] = conv["conv_sc_w"][:, l].transpose(1, 0, 2).reshape(3, BRANCH_W)
    if rep is not None:
        for k in ("norm_mix_pre", "gdn_a_log", "gdn_dt_bias", "gdn_norm_w", "norm_mix_post", "norm_ffn_pre", "norm_ffn_post"):
            p[k] = rep[k][l]
    return p


def _partials_by_chip(g, names):
    return [_SHARD_LAYOUT[k](g[k]).astype(BF16) for k in names]


def _pack_small(grads):
    pieces, layout = [], []
    for name in SMALL:
        v = jnp.stack([g[name] for g in grads]).astype(F32)
        layout.append((name, v.shape))
        pieces.append(v.reshape(-1))
    flat = jnp.concatenate(pieces)
    rows = -(-flat.shape[0] // LANES)
    rows = -(-rows // 8) * 8
    flat = jnp.pad(flat, (0, rows * LANES - flat.shape[0]))
    return flat.reshape(rows, LANES), layout


def _unpack_small(table, layout):
    flat, out, off = table.reshape(-1), {}, 0
    for name, shape in layout:
        size = 1
        for d in shape:
            size *= d
        out[name] = flat[off:off + size].reshape(shape)
        off += size
    return out


def _as2d(a):
    return a.reshape(-1, a.shape[-1]) if a.ndim > 1 else a.reshape(1, -1)


def kernel(x, norm_mix_pre, w_in, conv_qkv_w, gdn_a_log, gdn_dt_bias, gdn_norm_w, conv_sc_w, w_branch, w_out, norm_mix_post, norm_ffn_pre, w_ff1, w_ff2, norm_ffn_post, loss_target, m_norm_mix_pre, m_w_in, m_conv_qkv_w, m_gdn_a_log, m_gdn_dt_bias, m_gdn_norm_w, m_conv_sc_w, m_w_branch, m_w_out, m_norm_mix_post, m_norm_ffn_pre, m_w_ff1, m_w_ff2, m_norm_ffn_post, v_norm_mix_pre, v_w_in, v_conv_qkv_w, v_gdn_a_log, v_gdn_dt_bias, v_gdn_norm_w, v_conv_sc_w, v_w_branch, v_w_out, v_norm_mix_post, v_norm_ffn_pre, v_w_ff1, v_w_ff2, v_norm_ffn_post):
    w = dict(norm_mix_pre=norm_mix_pre, w_in=w_in, conv_qkv_w=conv_qkv_w, gdn_a_log=gdn_a_log, gdn_dt_bias=gdn_dt_bias,
             gdn_norm_w=gdn_norm_w, conv_sc_w=conv_sc_w, w_branch=w_branch, w_out=w_out, norm_mix_post=norm_mix_post,
             norm_ffn_pre=norm_ffn_pre, w_ff1=w_ff1, w_ff2=w_ff2, norm_ffn_post=norm_ffn_post)
    m = dict(norm_mix_pre=m_norm_mix_pre, w_in=m_w_in, conv_qkv_w=m_conv_qkv_w, gdn_a_log=m_gdn_a_log,
             gdn_dt_bias=m_gdn_dt_bias, gdn_norm_w=m_gdn_norm_w, conv_sc_w=m_conv_sc_w, w_branch=m_w_branch, w_out=m_w_out,
             norm_mix_post=m_norm_mix_post, norm_ffn_pre=m_norm_ffn_pre, w_ff1=m_w_ff1, w_ff2=m_w_ff2,
             norm_ffn_post=m_norm_ffn_post)
    v = dict(norm_mix_pre=v_norm_mix_pre, w_in=v_w_in, conv_qkv_w=v_conv_qkv_w, gdn_a_log=v_gdn_a_log,
             gdn_dt_bias=v_gdn_dt_bias, gdn_norm_w=v_gdn_norm_w, conv_sc_w=v_conv_sc_w, w_branch=v_w_branch, w_out=v_w_out,
             norm_mix_post=v_norm_mix_post, norm_ffn_pre=v_norm_ffn_pre, w_ff1=v_w_ff1, w_ff2=v_w_ff2,
             norm_ffn_post=v_norm_ffn_post)

    me = _chip_index()

    shards = {k: _cast_bf16(_as2d(w[k]), name=f"cast_{k}").reshape(w[k].shape) for k in BIG}
    conv_names = ("conv_qkv_w", "conv_sc_w")
    FIRST, REST = ("w_in",), ("w_branch", "w_out", "w_ff1", "w_ff2")

    def gather_start(l, names, tag, after):
        srcs = [shards[k][l] for k in names] + ([w[k] for k in conv_names] if (l == 0 and "w_in" in names) else [])
        return _exchange_start(srcs, by_slot=False, name=f"gather_start{l}{tag}", after=after)

    def gather_land(ex, names, l, tag, after):
        own, lands = _exchange_wait(ex, after, by_slot=False, name=f"gather_wait{l}{tag}")
        return {k: _place_own(land, o, name=f"own_{k}") for k, land, o in zip(names, lands, own)}

    gathers = {0: gather_start(0, FIRST, "a", None)}
    conv = {}

    def weights_of(l, x_l):
        if l > 0:
            full = gather_land(gathers[l], BIG, l, "", x_l)
            p = _full_weights(full, conv, w, l)
            if l + 1 < DEPTH:
                gathers[l + 1] = gather_start(l + 1, BIG, "", full["w_out"])
                p["norm_mix_pre"] = p["norm_mix_pre"] + gathers[l + 1]["token"][0, 0]
            return p, None
        full = gather_land(gathers[0], FIRST + conv_names, 0, "a", gathers[0]["token"])
        conv.update({k: full[k] for k in conv_names})
        p = _full_weights({"w_in": full["w_in"]}, conv, w, 0)
        rest = gather_start(0, REST, "b", full["conv_sc_w"])
        p["norm_mix_pre"] = p["norm_mix_pre"] + rest["token"][0, 0]

        def late(after):
            arrived = gather_land(rest, REST, 0, "b", after)
            q = _full_weights(arrived, None, None, 0)
            gathers[1] = gather_start(1, BIG, "", arrived["w_out"])
            q["norm_mix_post"] = p["norm_mix_post"] + gathers[1]["token"][0, 0]
            return q

        return p, late

    grads, scatters = [None] * DEPTH, {}

    def scatter_start(l, g, names, tag):
        scatters[l, names] = _exchange_start(_partials_by_chip(g, names), by_slot=True, name=f"scatter_start{l}{tag}")
        return scatters[l, names]["token"][0, 0]

    def grads_early(g):
        return scatter_start(0, g, REST, "a")

    def grads_done(l, g):
        grads[l] = g
        return scatter_start(l, g, FIRST, "b") if l == 0 else scatter_start(l, g, BIG, "")

    loss, dx = _local_step(x[0], loss_target[0], DEPTH, weights_of, grads_done, grads_early)
    loss = lax.psum(loss, ("x", "y", "c"))

    sums = [dict() for _ in range(DEPTH)]
    for (l, names), ex in sorted(scatters.items(), key=lambda kv: (-kv[0][0], kv[0][1] != REST)):
        tag = "" if names == BIG else ("a" if names == REST else "b")
        parts, lands = _exchange_wait(ex, dx, by_slot=True, name=f"scatter_wait{l}{tag}")
        for k, r, o in zip(names, lands, parts):
            sums[l][k] = _sum_partials(r.reshape(N_CHIPS, -1, r.shape[-1]), o.reshape(N_CHIPS, -1, o.shape[-1]), name=f"sum_{k}")
    mine = [jnp.concatenate([sums[l][k] for l in range(DEPTH)], axis=0) for k in BIG]
    theirs = _swap_sibling(mine, name="swap_sibling")
    small, layout = _pack_small(grads)
    small_g = _unpack_small(_sum_slots(_gather_devices(small, name="gather_small"), name="sum_small"), layout)
    for k, width in (("conv_qkv_w", 3 * BRANCH_W // N_CHIPS), ("conv_sc_w", BRANCH_W // N_CHIPS)):
        small_g[k] = lax.dynamic_slice_in_dim(small_g[k], me * width, width, axis=2)

    out = {}
    for k, s_mine, s_theirs in zip(BIG, mine, theirs):
        res = _adamw(_as2d(w[k]), _as2d(m[k]), _as2d(v[k]), s_mine, s_theirs, name=f"adamw_{k}")
        out[k] = [r.reshape(w[k].shape) for r in res]
    for k in SMALL:
        res = _adamw(_as2d(w[k]), _as2d(m[k]), _as2d(v[k]), _as2d(small_g[k]), None, name=f"adamw_{k}")
        out[k] = [r.reshape(w[k].shape) for r in res]
    return (loss, dx[None], *[out[k][0] for k in ORDER], *[out[k][1] for k in ORDER], *[out[k][2] for k in ORDER],
            *[out[k][3] for k in ORDER])
```
